```python
import jax, jax.numpy as jnp
from jax import lax
import numpy as np

D_MODEL = 1024
BATCH = 8
SEQ = 4096
DEPTH = 4

N_MIXERS = 2
N_A = (DEPTH + 1) // 2
N_B = DEPTH // 2

D_RNN = 3 * D_MODEL // 2
LRU_HEADS = 12
LRU_BW = D_RNN // LRU_HEADS
CONV_WIDTH = 4
LRU_C = 8.0

D_POOL = D_MODEL
POOL_WINDOWS = (2, 4, 8, 16)
POOL_GROUPS = len(POOL_WINDOWS)
POOL_GW = D_POOL // POOL_GROUPS

D_FF = 4 * D_MODEL
PLE_DIM = 256
ALPHA = (2 * DEPTH) ** 0.25
BETA = (8 * DEPTH) ** (-0.25)
LN_EPS = 1e-5

kernel_name = "hybrid_rglru_pool_deepnorm_trunk"


def layer_norm(x, g, b):
    xf = x.astype(jnp.float32)
    mu = jnp.mean(xf, axis=-1, keepdims=True)
    var = jnp.mean(jnp.square(xf - mu), axis=-1, keepdims=True)
    y = (xf - mu) * lax.rsqrt(var + LN_EPS)
    return (y * g.astype(jnp.float32) + b.astype(jnp.float32)).astype(x.dtype)


def causal_depthwise_conv(u, w, b):
    s = u.shape[1]
    up = jnp.pad(u, ((0, 0), (CONV_WIDTH - 1, 0), (0, 0)))
    out = b
    for k in range(CONV_WIDTH):
        out = out + up[:, k:k + s] * w[k]
    return out


def _lin_rec_combine(left, right):
    a1, b1 = left
    a2, b2 = right
    return a1 * a2, a2 * b1 + b2


def rg_lru(u, wa, ba, wx, bx, lam):
    bsz, s, _ = u.shape
    uh = u.reshape(bsz, s, LRU_HEADS, LRU_BW)
    r = jax.nn.sigmoid(jnp.einsum('bshi,hij->bshj', uh, wa).reshape(bsz, s, D_RNN) + ba)
    ig = jax.nn.sigmoid(jnp.einsum('bshi,hij->bshj', uh, wx).reshape(bsz, s, D_RNN) + bx)
    log_a = -LRU_C * r.astype(jnp.float32) * jax.nn.softplus(-lam.astype(jnp.float32))
    a = jnp.exp(log_a)
    mult = jnp.sqrt(-jnp.expm1(2.0 * log_a))
    mult = mult.at[:, 0].set(1.0)
    bterm = mult * (ig * u).astype(jnp.float32)
    _, h = lax.associative_scan(_lin_rec_combine, (a, bterm), axis=1)
    return h.astype(u.dtype)


def recurrent_mixer(x, w_in, conv_w, conv_b, wa, ba, wx, bx, lam, w_out):
    proj = x @ w_in
    u, y = proj[..., :D_RNN], proj[..., D_RNN:]
    u = causal_depthwise_conv(u, conv_w, conv_b)
    h = rg_lru(u, wa, ba, wx, bx, lam)
    return (h * jax.nn.gelu(y)) @ w_out


def pooling_mixer(x, w_in, w_grp, b_grp, scale, w_out):
    u = x @ w_in
    s = u.shape[1]
    pos = jnp.arange(s, dtype=jnp.int32)
    outs = []
    for g, w in enumerate(POOL_WINDOWS):
        ug = u[..., g * POOL_GW:(g + 1) * POOL_GW].astype(jnp.float32)
        cs = jnp.cumsum(ug, axis=1)
        cs_prev = jnp.pad(cs, ((0, 0), (w, 0), (0, 0)))[:, :s]
        cnt = jnp.minimum(pos + 1, w).astype(jnp.float32)[None, :, None]
        outs.append((cs - cs_prev) / cnt - ug)
    z = jnp.stack(outs, axis=2).astype(u.dtype)
    z = jnp.einsum('bsgi,gij->bsgj', z, w_grp).reshape(u.shape) + b_grp
    return (z * scale) @ w_out


def sq_relu_mlp(x, w1, w2):
    h = jax.nn.relu(x @ w1)
    return (h * h) @ w2


def _fwd_setup_inputs(seed: int = 0) -> dict:
    key = jax.random.key(seed)
    ks = jax.random.split(key, 26)
    f32 = jnp.float32

    def nrm(k, shape, scale):
        return jax.random.normal(k, shape, f32) * scale

    a_c = jax.random.uniform(ks[8], (N_A, D_RNN), f32, minval=0.9, maxval=0.999)
    a0 = a_c ** (1.0 / LRU_C)
    lam = jnp.log(a0) - jnp.log1p(-a0)
    return {
        "x": nrm(ks[0], (BATCH, SEQ, D_MODEL), 1.0),
        "p": nrm(ks[1], (DEPTH, BATCH, SEQ, PLE_DIM), 1.0),
        "lru_w_in": nrm(ks[2], (N_A, D_MODEL, 2 * D_RNN), D_MODEL ** -0.5),
        "lru_conv_w": nrm(ks[3], (N_A, CONV_WIDTH, D_RNN), CONV_WIDTH ** -0.5),
        "lru_conv_b": nrm(ks[4], (N_A, D_RNN), 0.01),
        "lru_wa": nrm(ks[5], (N_A, LRU_HEADS, LRU_BW, LRU_BW), LRU_BW ** -0.5),
        "lru_ba": nrm(ks[6], (N_A, D_RNN), 0.01),
        "lru_wx": nrm(ks[7], (N_A, LRU_HEADS, LRU_BW, LRU_BW), LRU_BW ** -0.5),
        "lru_bx": nrm(ks[9], (N_A, D_RNN), 0.01),
        "lru_lambda": lam,
        "lru_w_out": nrm(ks[10], (N_A, D_RNN, D_MODEL), BETA * D_RNN ** -0.5),
        "pool_w_in": nrm(ks[11], (N_B, D_MODEL, D_POOL), D_MODEL ** -0.5),
        "pool_w_grp": nrm(ks[12], (N_B, POOL_GROUPS, POOL_GW, POOL_GW), POOL_GW ** -0.5),
        "pool_b_grp": nrm(ks[13], (N_B, D_POOL), 0.01),
        "pool_scale": 1.0 + nrm(ks[14], (N_B, D_POOL), 0.1),
        "pool_w_out": nrm(ks[15], (N_B, D_POOL, D_MODEL), BETA * D_POOL ** -0.5),
        "ln_mix_g": 1.0 + nrm(ks[16], (DEPTH, D_MODEL), 0.05),
        "ln_mix_b": nrm(ks[17], (DEPTH, D_MODEL), 0.01),
        "mlp_w1": nrm(ks[18], (DEPTH, D_MODEL, D_FF), D_MODEL ** -0.5),
        "mlp_w2": nrm(ks[19], (DEPTH, D_FF, D_MODEL), BETA * D_FF ** -0.5),
        "ln_mlp_g": 1.0 + nrm(ks[20], (DEPTH, D_MODEL), 0.05),
        "ln_mlp_b": nrm(ks[21], (DEPTH, D_MODEL), 0.01),
        "ple_w": nrm(ks[22], (DEPTH, PLE_DIM, D_MODEL), PLE_DIM ** -0.5),
        "ple_gate_w": nrm(ks[23], (DEPTH, D_MODEL, D_MODEL), D_MODEL ** -0.5),
        "ple_gate_b": nrm(ks[24], (DEPTH, D_MODEL), 0.01),
    }


def _fwd_reference(x, p, lru_w_in, lru_conv_w, lru_conv_b, lru_wa, lru_ba, lru_wx, lru_bx,
              lru_lambda, lru_w_out, pool_w_in, pool_w_grp, pool_b_grp, pool_scale,
              pool_w_out, ln_mix_g, ln_mix_b, mlp_w1, mlp_w2, ln_mlp_g, ln_mlp_b,
              ple_w, ple_gate_w, ple_gate_b):
    for i in range(DEPTH):
        slot = i // N_MIXERS
        if i % N_MIXERS == 0:
            m = recurrent_mixer(x, lru_w_in[slot], lru_conv_w[slot], lru_conv_b[slot],
                                lru_wa[slot], lru_ba[slot], lru_wx[slot], lru_bx[slot],
                                lru_lambda[slot], lru_w_out[slot])
        else:
            m = pooling_mixer(x, pool_w_in[slot], pool_w_grp[slot], pool_b_grp[slot],
                              pool_scale[slot], pool_w_out[slot])
        x = layer_norm(ALPHA * x + m, ln_mix_g[i], ln_mix_b[i])
        x = layer_norm(ALPHA * x + sq_relu_mlp(x, mlp_w1[i], mlp_w2[i]), ln_mlp_g[i], ln_mlp_b[i])
        gate = jax.nn.sigmoid(x @ ple_gate_w[i] + ple_gate_b[i])
        x = x + (p[i] @ ple_w[i]) * gate
    return x


import jax as _jax
import jax.numpy as _jnp

TWIN_FORMAT = 'train_step'
FWD_PARAMS = ['x', 'p', 'lru_w_in', 'lru_conv_w', 'lru_conv_b', 'lru_wa', 'lru_ba', 'lru_wx', 'lru_bx', 'lru_lambda', 'lru_w_out', 'pool_w_in', 'pool_w_grp', 'pool_b_grp', 'pool_scale', 'pool_w_out', 'ln_mix_g', 'ln_mix_b', 'mlp_w1', 'mlp_w2', 'ln_mlp_g', 'ln_mlp_b', 'ple_w', 'ple_gate_w', 'ple_gate_b']
TWIN_WEIGHTS = ['lru_w_in', 'lru_conv_w', 'lru_conv_b', 'lru_wa', 'lru_ba', 'lru_wx', 'lru_bx', 'lru_lambda', 'lru_w_out', 'pool_w_in', 'pool_w_grp', 'pool_b_grp', 'pool_scale', 'pool_w_out', 'ln_mix_g', 'ln_mix_b', 'mlp_w1', 'mlp_w2', 'ln_mlp_g', 'ln_mlp_b', 'ple_w', 'ple_gate_w', 'ple_gate_b']
TWIN_DIFF_INPUT = 'x'
TWIN_INPUTS = ['x', 'p', 'lru_w_in', 'lru_conv_w', 'lru_conv_b', 'lru_wa', 'lru_ba', 'lru_wx', 'lru_bx', 'lru_lambda', 'lru_w_out', 'pool_w_in', 'pool_w_grp', 'pool_b_grp', 'pool_scale', 'pool_w_out', 'ln_mix_g', 'ln_mix_b', 'mlp_w1', 'mlp_w2', 'ln_mlp_g', 'ln_mlp_b', 'ple_w', 'ple_gate_w', 'ple_gate_b', 'loss_target', 'm_lru_w_in', 'm_lru_conv_w', 'm_lru_conv_b', 'm_lru_wa', 'm_lru_ba', 'm_lru_wx', 'm_lru_bx', 'm_lru_lambda', 'm_lru_w_out', 'm_pool_w_in', 'm_pool_w_grp', 'm_pool_b_grp', 'm_pool_scale', 'm_pool_w_out', 'm_ln_mix_g', 'm_ln_mix_b', 'm_mlp_w1', 'm_mlp_w2', 'm_ln_mlp_g', 'm_ln_mlp_b', 'm_ple_w', 'm_ple_gate_w', 'm_ple_gate_b', 'v_lru_w_in', 'v_lru_conv_w', 'v_lru_conv_b', 'v_lru_wa', 'v_lru_ba', 'v_lru_wx', 'v_lru_bx', 'v_lru_lambda', 'v_lru_w_out', 'v_pool_w_in', 'v_pool_w_grp', 'v_pool_b_grp', 'v_pool_scale', 'v_pool_w_out', 'v_ln_mix_g', 'v_ln_mix_b', 'v_mlp_w1', 'v_mlp_w2', 'v_ln_mlp_g', 'v_ln_mlp_b', 'v_ple_w', 'v_ple_gate_w', 'v_ple_gate_b']
TWIN_OUTPUTS = ['loss', 'grad_x', 'grad_lru_w_in', 'grad_lru_conv_w', 'grad_lru_conv_b', 'grad_lru_wa', 'grad_lru_ba', 'grad_lru_wx', 'grad_lru_bx', 'grad_lru_lambda', 'grad_lru_w_out', 'grad_pool_w_in', 'grad_pool_w_grp', 'grad_pool_b_grp', 'grad_pool_scale', 'grad_pool_w_out', 'grad_ln_mix_g', 'grad_ln_mix_b', 'grad_mlp_w1', 'grad_mlp_w2', 'grad_ln_mlp_g', 'grad_ln_mlp_b', 'grad_ple_w', 'grad_ple_gate_w', 'grad_ple_gate_b', 'delta_lru_w_in', 'delta_lru_conv_w', 'delta_lru_conv_b', 'delta_lru_wa', 'delta_lru_ba', 'delta_lru_wx', 'delta_lru_bx', 'delta_lru_lambda', 'delta_lru_w_out', 'delta_pool_w_in', 'delta_pool_w_grp', 'delta_pool_b_grp', 'delta_pool_scale', 'delta_pool_w_out', 'delta_ln_mix_g', 'delta_ln_mix_b', 'delta_mlp_w1', 'delta_mlp_w2', 'delta_ln_mlp_g', 'delta_ln_mlp_b', 'delta_ple_w', 'delta_ple_gate_w', 'delta_ple_gate_b', 'new_m_lru_w_in', 'new_m_lru_conv_w', 'new_m_lru_conv_b', 'new_m_lru_wa', 'new_m_lru_ba', 'new_m_lru_wx', 'new_m_lru_bx', 'new_m_lru_lambda', 'new_m_lru_w_out', 'new_m_pool_w_in', 'new_m_pool_w_grp', 'new_m_pool_b_grp', 'new_m_pool_scale', 'new_m_pool_w_out', 'new_m_ln_mix_g', 'new_m_ln_mix_b', 'new_m_mlp_w1', 'new_m_mlp_w2', 'new_m_ln_mlp_g', 'new_m_ln_mlp_b', 'new_m_ple_w', 'new_m_ple_gate_w', 'new_m_ple_gate_b', 'new_v_lru_w_in', 'new_v_lru_conv_w', 'new_v_lru_conv_b', 'new_v_lru_wa', 'new_v_lru_ba', 'new_v_lru_wx', 'new_v_lru_bx', 'new_v_lru_lambda', 'new_v_lru_w_out', 'new_v_pool_w_in', 'new_v_pool_w_grp', 'new_v_pool_b_grp', 'new_v_pool_scale', 'new_v_pool_w_out', 'new_v_ln_mix_g', 'new_v_ln_mix_b', 'new_v_mlp_w1', 'new_v_mlp_w2', 'new_v_ln_mlp_g', 'new_v_ln_mlp_b', 'new_v_ple_w', 'new_v_ple_gate_w', 'new_v_ple_gate_b']
TWIN_LEAF_KINDS = {'loss': 'loss', 'grad_x': 'grad_x', 'grad_lru_w_in': 'grad_w', 'grad_lru_conv_w': 'grad_w', 'grad_lru_conv_b': 'grad_w', 'grad_lru_wa': 'grad_w', 'grad_lru_ba': 'grad_w', 'grad_lru_wx': 'grad_w', 'grad_lru_bx': 'grad_w', 'grad_lru_lambda': 'grad_w', 'grad_lru_w_out': 'grad_w', 'grad_pool_w_in': 'grad_w', 'grad_pool_w_grp': 'grad_w', 'grad_pool_b_grp': 'grad_w', 'grad_pool_scale': 'grad_w', 'grad_pool_w_out': 'grad_w', 'grad_ln_mix_g': 'grad_w', 'grad_ln_mix_b': 'grad_w', 'grad_mlp_w1': 'grad_w', 'grad_mlp_w2': 'grad_w', 'grad_ln_mlp_g': 'grad_w', 'grad_ln_mlp_b': 'grad_w', 'grad_ple_w': 'grad_w', 'grad_ple_gate_w': 'grad_w', 'grad_ple_gate_b': 'grad_w', 'delta_lru_w_in': 'delta_w', 'delta_lru_conv_w': 'delta_w', 'delta_lru_conv_b': 'delta_w', 'delta_lru_wa': 'delta_w', 'delta_lru_ba': 'delta_w', 'delta_lru_wx': 'delta_w', 'delta_lru_bx': 'delta_w', 'delta_lru_lambda': 'delta_w', 'delta_lru_w_out': 'delta_w', 'delta_pool_w_in': 'delta_w', 'delta_pool_w_grp': 'delta_w', 'delta_pool_b_grp': 'delta_w', 'delta_pool_scale': 'delta_w', 'delta_pool_w_out': 'delta_w', 'delta_ln_mix_g': 'delta_w', 'delta_ln_mix_b': 'delta_w', 'delta_mlp_w1': 'delta_w', 'delta_mlp_w2': 'delta_w', 'delta_ln_mlp_g': 'delta_w', 'delta_ln_mlp_b': 'delta_w', 'delta_ple_w': 'delta_w', 'delta_ple_gate_w': 'delta_w', 'delta_ple_gate_b': 'delta_w', 'new_m_lru_w_in': 'new_m', 'new_m_lru_conv_w': 'new_m', 'new_m_lru_conv_b': 'new_m', 'new_m_lru_wa': 'new_m', 'new_m_lru_ba': 'new_m', 'new_m_lru_wx': 'new_m', 'new_m_lru_bx': 'new_m', 'new_m_lru_lambda': 'new_m', 'new_m_lru_w_out': 'new_m', 'new_m_pool_w_in': 'new_m', 'new_m_pool_w_grp': 'new_m', 'new_m_pool_b_grp': 'new_m', 'new_m_pool_scale': 'new_m', 'new_m_pool_w_out': 'new_m', 'new_m_ln_mix_g': 'new_m', 'new_m_ln_mix_b': 'new_m', 'new_m_mlp_w1': 'new_m', 'new_m_mlp_w2': 'new_m', 'new_m_ln_mlp_g': 'new_m', 'new_m_ln_mlp_b': 'new_m', 'new_m_ple_w': 'new_m', 'new_m_ple_gate_w': 'new_m', 'new_m_ple_gate_b': 'new_m', 'new_v_lru_w_in': 'new_v', 'new_v_lru_conv_w': 'new_v', 'new_v_lru_conv_b': 'new_v', 'new_v_lru_wa': 'new_v', 'new_v_lru_ba': 'new_v', 'new_v_lru_wx': 'new_v', 'new_v_lru_bx': 'new_v', 'new_v_lru_lambda': 'new_v', 'new_v_lru_w_out': 'new_v', 'new_v_pool_w_in': 'new_v', 'new_v_pool_w_grp': 'new_v', 'new_v_pool_b_grp': 'new_v', 'new_v_pool_scale': 'new_v', 'new_v_pool_w_out': 'new_v', 'new_v_ln_mix_g': 'new_v', 'new_v_ln_mix_b': 'new_v', 'new_v_mlp_w1': 'new_v', 'new_v_mlp_w2': 'new_v', 'new_v_ln_mlp_g': 'new_v', 'new_v_ln_mlp_b': 'new_v', 'new_v_ple_w': 'new_v', 'new_v_ple_gate_w': 'new_v', 'new_v_ple_gate_b': 'new_v'}


def _forward(args):
    return _fwd_reference(*[args[k] for k in FWD_PARAMS])


def _output_shape():
    out = _jax.eval_shape(lambda: _forward(_fwd_setup_inputs(0)))
    return out.shape, out.dtype

N_MICROBATCH = 1
ADAM_LR = 0.001
ADAM_B1 = 0.9
ADAM_B2 = 0.999
ADAM_EPS = 1e-08
ADAM_WD = 0.01
ADAM_STEP = 10
PER_EXAMPLE_BATCH_AXIS = {'x': 0, 'p': 1, 'loss_target': 0}
SHARED_INPUTS = []
_WEIGHT_DTYPES = {'lru_w_in': _jnp.float32, 'lru_conv_w': _jnp.float32, 'lru_conv_b': _jnp.float32, 'lru_wa': _jnp.float32, 'lru_ba': _jnp.float32, 'lru_wx': _jnp.float32, 'lru_bx': _jnp.float32, 'lru_lambda': _jnp.float32, 'lru_w_out': _jnp.float32, 'pool_w_in': _jnp.float32, 'pool_w_grp': _jnp.float32, 'pool_b_grp': _jnp.float32, 'pool_scale': _jnp.float32, 'pool_w_out': _jnp.float32, 'ln_mix_g': _jnp.float32, 'ln_mix_b': _jnp.float32, 'mlp_w1': _jnp.float32, 'mlp_w2': _jnp.float32, 'ln_mlp_g': _jnp.float32, 'ln_mlp_b': _jnp.float32, 'ple_w': _jnp.float32, 'ple_gate_w': _jnp.float32, 'ple_gate_b': _jnp.float32}
MOMENT_SCALE = {'lru_w_in': 6.894127e-02, 'lru_conv_w': 1.094727e-01, 'lru_conv_b': 7.656125e-01, 'lru_wa': 2.376464e-02, 'lru_ba': 1.872944e-02, 'lru_wx': 4.460308e-02, 'lru_bx': 4.120348e-02, 'lru_lambda': 4.379107e-02, 'lru_w_out': 3.289859e-01, 'pool_w_in': 4.274954e-02, 'pool_w_grp': 4.319595e-02, 'pool_b_grp': 6.325797e-01, 'pool_scale': 4.429986e-02, 'pool_w_out': 1.028938e-01, 'ln_mix_g': 1.864319e+00, 'ln_mix_b': 2.612202e+00, 'mlp_w1': 3.789141e-02, 'mlp_w2': 6.595033e-01, 'ln_mlp_g': 1.676300e+01, 'ln_mlp_b': 4.235259e+00, 'ple_w': 2.116860e-01, 'ple_gate_w': 2.848355e-01, 'ple_gate_b': 1.638287e+00}


def _to_microbatches(a, axis):
    t = _jnp.moveaxis(a, axis, 0)
    t = t.reshape((N_MICROBATCH, t.shape[0] // N_MICROBATCH) + t.shape[1:])
    return _jnp.moveaxis(t, 1, axis + 1)


def setup_inputs(seed: int = 0) -> dict:
    inp = _fwd_setup_inputs(seed)
    key = _jax.random.fold_in(_jax.random.key(seed), 7919)
    shape, _ = _output_shape()
    out = dict(inp)
    out["loss_target"] = _jax.random.normal(_jax.random.fold_in(key, 0), shape, _jnp.float32)
    for i, name in enumerate(TWIN_WEIGHTS):
        w = inp[name].astype(_jnp.float32)
        if MOMENT_SCALE is None:
            s = _jnp.sqrt(_jnp.mean(_jnp.square(w)) + 1e-30)
        else:
            s = MOMENT_SCALE[name]
        km, kv = _jax.random.split(_jax.random.fold_in(key, i + 1))
        out[name] = w
        out["m_" + name] = s * _jax.random.normal(km, w.shape, _jnp.float32)
        out["v_" + name] = (s * s) * _jax.random.uniform(kv, w.shape, _jnp.float32, 0.5, 1.5)
    if N_MICROBATCH > 1:
        for name, axis in PER_EXAMPLE_BATCH_AXIS.items():
            out[name] = _to_microbatches(out[name], axis)
    return {'x': out['x'], 'p': out['p'], 'lru_w_in': out['lru_w_in'], 'lru_conv_w': out['lru_conv_w'], 'lru_conv_b': out['lru_conv_b'], 'lru_wa': out['lru_wa'], 'lru_ba': out['lru_ba'], 'lru_wx': out['lru_wx'], 'lru_bx': out['lru_bx'], 'lru_lambda': out['lru_lambda'], 'lru_w_out': out['lru_w_out'], 'pool_w_in': out['pool_w_in'], 'pool_w_grp': out['pool_w_grp'], 'pool_b_grp': out['pool_b_grp'], 'pool_scale': out['pool_scale'], 'pool_w_out': out['pool_w_out'], 'ln_mix_g': out['ln_mix_g'], 'ln_mix_b': out['ln_mix_b'], 'mlp_w1': out['mlp_w1'], 'mlp_w2': out['mlp_w2'], 'ln_mlp_g': out['ln_mlp_g'], 'ln_mlp_b': out['ln_mlp_b'], 'ple_w': out['ple_w'], 'ple_gate_w': out['ple_gate_w'], 'ple_gate_b': out['ple_gate_b'], 'loss_target': out['loss_target'], 'm_lru_w_in': out['m_lru_w_in'], 'm_lru_conv_w': out['m_lru_conv_w'], 'm_lru_conv_b': out['m_lru_conv_b'], 'm_lru_wa': out['m_lru_wa'], 'm_lru_ba': out['m_lru_ba'], 'm_lru_wx': out['m_lru_wx'], 'm_lru_bx': out['m_lru_bx'], 'm_lru_lambda': out['m_lru_lambda'], 'm_lru_w_out': out['m_lru_w_out'], 'm_pool_w_in': out['m_pool_w_in'], 'm_pool_w_grp': out['m_pool_w_grp'], 'm_pool_b_grp': out['m_pool_b_grp'], 'm_pool_scale': out['m_pool_scale'], 'm_pool_w_out': out['m_pool_w_out'], 'm_ln_mix_g': out['m_ln_mix_g'], 'm_ln_mix_b': out['m_ln_mix_b'], 'm_mlp_w1': out['m_mlp_w1'], 'm_mlp_w2': out['m_mlp_w2'], 'm_ln_mlp_g': out['m_ln_mlp_g'], 'm_ln_mlp_b': out['m_ln_mlp_b'], 'm_ple_w': out['m_ple_w'], 'm_ple_gate_w': out['m_ple_gate_w'], 'm_ple_gate_b': out['m_ple_gate_b'], 'v_lru_w_in': out['v_lru_w_in'], 'v_lru_conv_w': out['v_lru_conv_w'], 'v_lru_conv_b': out['v_lru_conv_b'], 'v_lru_wa': out['v_lru_wa'], 'v_lru_ba': out['v_lru_ba'], 'v_lru_wx': out['v_lru_wx'], 'v_lru_bx': out['v_lru_bx'], 'v_lru_lambda': out['v_lru_lambda'], 'v_lru_w_out': out['v_lru_w_out'], 'v_pool_w_in': out['v_pool_w_in'], 'v_pool_w_grp': out['v_pool_w_grp'], 'v_pool_b_grp': out['v_pool_b_grp'], 'v_pool_scale': out['v_pool_scale'], 'v_pool_w_out': out['v_pool_w_out'], 'v_ln_mix_g': out['v_ln_mix_g'], 'v_ln_mix_b': out['v_ln_mix_b'], 'v_mlp_w1': out['v_mlp_w1'], 'v_mlp_w2': out['v_mlp_w2'], 'v_ln_mlp_g': out['v_ln_mlp_g'], 'v_ln_mlp_b': out['v_ln_mlp_b'], 'v_ple_w': out['v_ple_w'], 'v_ple_gate_w': out['v_ple_gate_w'], 'v_ple_gate_b': out['v_ple_gate_b']}


def _loss(weights, diff, rest, loss_target):
    with _jax.named_scope("forward"):
        args = {**rest, TWIN_DIFF_INPUT: diff, **{k: w.astype(_WEIGHT_DTYPES[k]) for k, w in weights.items()}}
        y = _forward(args)
    with _jax.named_scope("loss_head"):
        err = _jnp.square(y.astype(_jnp.float32) - loss_target)
        return 0.5 * _jnp.sum(_jnp.mean(err, axis=-1)) if err.ndim else 0.5 * err


def _adamw(w, g, m, v):
    m = ADAM_B1 * m + (1.0 - ADAM_B1) * g
    v = ADAM_B2 * v + (1.0 - ADAM_B2) * _jnp.square(g)
    m_hat = m / (1.0 - ADAM_B1 ** ADAM_STEP)
    v_hat = v / (1.0 - ADAM_B2 ** ADAM_STEP)
    delta = -ADAM_LR * (m_hat / (_jnp.sqrt(v_hat) + ADAM_EPS) + ADAM_WD * w)
    return delta, m, v


def reference(x, p, lru_w_in, lru_conv_w, lru_conv_b, lru_wa, lru_ba, lru_wx, lru_bx, lru_lambda, lru_w_out, pool_w_in, pool_w_grp, pool_b_grp, pool_scale, pool_w_out, ln_mix_g, ln_mix_b, mlp_w1, mlp_w2, ln_mlp_g, ln_mlp_b, ple_w, ple_gate_w, ple_gate_b, loss_target, m_lru_w_in, m_lru_conv_w, m_lru_conv_b, m_lru_wa, m_lru_ba, m_lru_wx, m_lru_bx, m_lru_lambda, m_lru_w_out, m_pool_w_in, m_pool_w_grp, m_pool_b_grp, m_pool_scale, m_pool_w_out, m_ln_mix_g, m_ln_mix_b, m_mlp_w1, m_mlp_w2, m_ln_mlp_g, m_ln_mlp_b, m_ple_w, m_ple_gate_w, m_ple_gate_b, v_lru_w_in, v_lru_conv_w, v_lru_conv_b, v_lru_wa, v_lru_ba, v_lru_wx, v_lru_bx, v_lru_lambda, v_lru_w_out, v_pool_w_in, v_pool_w_grp, v_pool_b_grp, v_pool_scale, v_pool_w_out, v_ln_mix_g, v_ln_mix_b, v_mlp_w1, v_mlp_w2, v_ln_mlp_g, v_ln_mlp_b, v_ple_w, v_ple_gate_w, v_ple_gate_b):
    given = dict(x=x, p=p, lru_w_in=lru_w_in, lru_conv_w=lru_conv_w, lru_conv_b=lru_conv_b, lru_wa=lru_wa, lru_ba=lru_ba, lru_wx=lru_wx, lru_bx=lru_bx, lru_lambda=lru_lambda, lru_w_out=lru_w_out, pool_w_in=pool_w_in, pool_w_grp=pool_w_grp, pool_b_grp=pool_b_grp, pool_scale=pool_scale, pool_w_out=pool_w_out, ln_mix_g=ln_mix_g, ln_mix_b=ln_mix_b, mlp_w1=mlp_w1, mlp_w2=mlp_w2, ln_mlp_g=ln_mlp_g, ln_mlp_b=ln_mlp_b, ple_w=ple_w, ple_gate_w=ple_gate_w, ple_gate_b=ple_gate_b, loss_target=loss_target, m_lru_w_in=m_lru_w_in, m_lru_conv_w=m_lru_conv_w, m_lru_conv_b=m_lru_conv_b, m_lru_wa=m_lru_wa, m_lru_ba=m_lru_ba, m_lru_wx=m_lru_wx, m_lru_bx=m_lru_bx, m_lru_lambda=m_lru_lambda, m_lru_w_out=m_lru_w_out, m_pool_w_in=m_pool_w_in, m_pool_w_grp=m_pool_w_grp, m_pool_b_grp=m_pool_b_grp, m_pool_scale=m_pool_scale, m_pool_w_out=m_pool_w_out, m_ln_mix_g=m_ln_mix_g, m_ln_mix_b=m_ln_mix_b, m_mlp_w1=m_mlp_w1, m_mlp_w2=m_mlp_w2, m_ln_mlp_g=m_ln_mlp_g, m_ln_mlp_b=m_ln_mlp_b, m_ple_w=m_ple_w, m_ple_gate_w=m_ple_gate_w, m_ple_gate_b=m_ple_gate_b, v_lru_w_in=v_lru_w_in, v_lru_conv_w=v_lru_conv_w, v_lru_conv_b=v_lru_conv_b, v_lru_wa=v_lru_wa, v_lru_ba=v_lru_ba, v_lru_wx=v_lru_wx, v_lru_bx=v_lru_bx, v_lru_lambda=v_lru_lambda, v_lru_w_out=v_lru_w_out, v_pool_w_in=v_pool_w_in, v_pool_w_grp=v_pool_w_grp, v_pool_b_grp=v_pool_b_grp, v_pool_scale=v_pool_scale, v_pool_w_out=v_pool_w_out, v_ln_mix_g=v_ln_mix_g, v_ln_mix_b=v_ln_mix_b, v_mlp_w1=v_mlp_w1, v_mlp_w2=v_mlp_w2, v_ln_mlp_g=v_ln_mlp_g, v_ln_mlp_b=v_ln_mlp_b, v_ple_w=v_ple_w, v_ple_gate_w=v_ple_gate_w, v_ple_gate_b=v_ple_gate_b)
    weights = {n: given[n] for n in TWIN_WEIGHTS}
    shared = {n: given[n] for n in SHARED_INPUTS}
    per_example = {n: given[n] for n in ['x', 'p']}
    grad_fn = _jax.value_and_grad(_loss, argnums=(0, 1))

    def one_microbatch(ex, loss_target):
        ex = dict(ex)
        diff = ex.pop(TWIN_DIFF_INPUT)
        return grad_fn(weights, diff, {**shared, **ex}, loss_target)

    if N_MICROBATCH == 1:
        loss, (grad_w, grad_x) = one_microbatch(per_example, given["loss_target"])
    else:
        def body(carry, xs):
            loss_sum, grad_sum = carry
            l_k, (gw_k, gx_k) = one_microbatch(xs[0], xs[1])
            with _jax.named_scope("update"):
                return (loss_sum + l_k, _jax.tree.map(_jnp.add, grad_sum, gw_k)), gx_k

        init = (_jnp.zeros((), _jnp.float32), _jax.tree.map(_jnp.zeros_like, weights))
        (loss, grad_w), grad_x = _jax.lax.scan(body, init, (per_example, given["loss_target"]))
    with _jax.named_scope("update"):
        delta_w, new_m, new_v = {}, {}, {}
        for n in TWIN_WEIGHTS:
            delta_w[n], new_m[n], new_v[n] = _adamw(weights[n], grad_w[n], given["m_" + n], given["v_" + n])
    return (loss, grad_x, *[grad_w[n] for n in TWIN_WEIGHTS], *[delta_w[n] for n in TWIN_WEIGHTS],
            *[new_m[n] for n in TWIN_WEIGHTS], *[new_v[n] for n in TWIN_WEIGHTS])
```

```python
import functools
import math

import jax
import jax.numpy as jnp
from jax import lax
from jax.experimental import pallas as pl
from jax.experimental.pallas import tpu as pltpu

F32 = jnp.float32
BF16 = jnp.bfloat16
MESH = pl.DeviceIdType.MESH
N_DEV = 8
LANES = 128
SUBLANES = 8

LN_EPS = 1e-5
LRU_C = 8.0
CONV_WIDTH = 4
POOL_HALO = 16
ADAM_LR = 0.001
ADAM_B1 = 0.9
ADAM_B2 = 0.999
ADAM_EPS = 1e-08
ADAM_WD = 0.01
ADAM_STEP = 10

VMEM_LIMIT = 48 * 1024 * 1024
SEQ_CHUNK = 256
GELU_C0 = math.sqrt(2.0 / math.pi)
GELU_C1 = 0.044715


def _cparams(*sem):
    return pltpu.CompilerParams(dimension_semantics=tuple(sem) if sem else None, vmem_limit_bytes=VMEM_LIMIT)


def _tile(n, pref):
    if n <= pref:
        return n
    t = pref - pref % LANES
    while t >= LANES:
        if n % t == 0:
            return t
        t -= LANES
    return n


def _row_tile(n, pref):
    if n <= pref:
        return n
    t = pref - pref % SUBLANES
    while t >= SUBLANES:
        if n % t == 0:
            return t
        t -= SUBLANES
    return n


def _mm(name, a, b, mode, out_dtypes, epi=None, extras=(), a_lead=None, b_lead=None, tm=1024, tn=512, tk=512):
    a2 = a.shape[-2:]
    b2 = b.shape[-2:]
    if mode == "nn":
        (M, K), N = a2, b2[1]
        assert b2[0] == K
    elif mode == "nt":
        (M, K), N = a2, b2[0]
        assert b2[1] == K
    else:
        (K, M), N = a2, b2[1]
        assert b2[0] == K
    tm, tn, tk = _tile(M, tm), _tile(N, tn), _tile(K, tk)
    nk = K // tk
    n_extra = len(extras)
    n_out = len(out_dtypes)

    def lead(shape, idx, which):
        if which is None:
            return pl.BlockSpec(shape, idx)
        return pl.BlockSpec((None,) + shape, lambda i, j, k: (which,) + idx(i, j, k))

    if mode == "nn":
        a_spec = lead((tm, tk), lambda i, j, k: (i, k), a_lead)
        b_spec = lead((tk, tn), lambda i, j, k: (k, j), b_lead)
        dims = (((1,), (0,)), ((), ()))
    elif mode == "nt":
        a_spec = lead((tm, tk), lambda i, j, k: (i, k), a_lead)
        b_spec = lead((tn, tk), lambda i, j, k: (j, k), b_lead)
        dims = (((1,), (1,)), ((), ()))
    else:
        a_spec = lead((tk, tm), lambda i, j, k: (k, i), a_lead)
        b_spec = lead((tk, tn), lambda i, j, k: (k, j), b_lead)
        dims = (((0,), (0,)), ((), ()))
    e_specs = []
    for e in extras:
        if e.shape[0] == 1:
            e_specs.append(pl.BlockSpec((1, tn), lambda i, j, k: (0, j)))
        else:
            e_specs.append(pl.BlockSpec((tm, tn), lambda i, j, k: (i, j)))

    def body(a_ref, b_ref, *rest):
        e_refs = rest[:n_extra]
        o_refs = rest[n_extra:n_extra + n_out]
        acc = rest[n_extra + n_out]
        k = pl.program_id(2)

        @pl.when(k == 0)
        def _():
            acc[...] = jnp.zeros_like(acc)

        acc[...] += lax.dot_general(a_ref[...].astype(BF16), b_ref[...].astype(BF16), dims,
                                    preferred_element_type=F32)

        @pl.when(k == nk - 1)
        def _():
            r = acc[...]
            res = (r,) if epi is None else epi(r, *[e[...] for e in e_refs])
            for o, v in zip(o_refs, res):
                o[...] = v.astype(o.dtype)

    outs = pl.pallas_call(
        body,
        name=name,
        grid=(M // tm, N // tn, nk),
        in_specs=[a_spec, b_spec] + e_specs,
        out_specs=[pl.BlockSpec((tm, tn), lambda i, j, k: (i, j)) for _ in out_dtypes],
        out_shape=[jax.ShapeDtypeStruct((M, N), d) for d in out_dtypes],
        scratch_shapes=[pltpu.VMEM((tm, tn), F32)],
        compiler_params=_cparams("parallel", "parallel", "arbitrary"),
    )(a, b, *extras)
    return outs[0] if n_out == 1 else tuple(outs)


def _rowwise(name, fn, tiled, params, outs, accs=(), tm=256):
    S = tiled[0].shape[0]
    tm = _tile(S, tm)
    nt, npar, no = len(tiled), len(params), len(outs)

    def body(*refs):
        t_refs = refs[:nt]
        p_refs = refs[nt:nt + npar]
        o_refs = refs[nt + npar:nt + npar + no]
        a_refs = refs[nt + npar + no:]
        res = fn(*[r[...] for r in t_refs], *[r[...] for r in p_refs])
        for o, v in zip(o_refs, res[:no]):
            o[...] = v.astype(o.dtype)
        first = pl.program_id(0) == 0
        for ar, v in zip(a_refs, res[no:]):
            @pl.when(first)
            def _(ar=ar, v=v):
                ar[...] = v

            @pl.when(jnp.logical_not(first))
            def _(ar=ar, v=v):
                ar[...] += v

    full = lambda p: pl.BlockSpec(p.shape, lambda i, nd=p.ndim: (0,) * nd)
    res = pl.pallas_call(
        body,
        name=name,
        grid=(S // tm,),
        in_specs=[pl.BlockSpec((tm, t.shape[1]), lambda i: (i, 0)) for t in tiled] + [full(p) for p in params],
        out_specs=[pl.BlockSpec((tm, c), lambda i: (i, 0)) for c, _ in outs]
        + [pl.BlockSpec(s, lambda i, nd=len(s): (0,) * nd) for s in accs],
        out_shape=[jax.ShapeDtypeStruct((S, c), d) for c, d in outs] + [jax.ShapeDtypeStruct(s, F32) for s in accs],
        compiler_params=_cparams("arbitrary"),
    )(*tiled, *params)
    return res


def _ln_stats(z):
    mu = jnp.mean(z, axis=-1, keepdims=True)
    zc = z - mu
    var = jnp.mean(zc * zc, axis=-1, keepdims=True)
    return zc, lax.rsqrt(var + LN_EPS)


def _ln_fwd(name, alpha, xp, m, g, b):
    def fn(xp, m, g, b):
        zc, rstd = _ln_stats(alpha * xp + m)
        return (zc * rstd * g + b,)

    return _rowwise(name, fn, [xp, m], [g, b], [(xp.shape[1], F32)])[0]


def _ln_bwd(name, alpha, dy, xp, m, g):
    def fn(dy, xp, m, g):
        zc, rstd = _ln_stats(alpha * xp + m)
        xhat = zc * rstd
        dxh = dy * g
        m1 = jnp.mean(dxh, axis=-1, keepdims=True)
        m2 = jnp.mean(dxh * xhat, axis=-1, keepdims=True)
        dz = rstd * (dxh - m1 - xhat * m2)
        return dz, jnp.sum(dy * xhat, axis=0, keepdims=True), jnp.sum(dy, axis=0, keepdims=True)

    d = xp.shape[1]
    return _rowwise(name, fn, [dy, xp, m], [g], [(d, F32)], accs=[(1, d), (1, d)])


def _loss_and_grad(name, y, target):
    d = y.shape[1]

    def fn(y, t):
        err = y - t
        sq = jnp.sum(jnp.sum(err * err, axis=0, keepdims=True), axis=1, keepdims=True)
        return err * (1.0 / d), jnp.broadcast_to(sq, (1, LANES))

    return _rowwise(name, fn, [y, target], [], [(d, F32)], accs=[(1, LANES)])


def _ple_bwd(name, dx3, gpre, pp):
    def fn(dx3, gpre, pp):
        gate = jax.nn.sigmoid(gpre)
        dgpre = dx3 * pp * gate * (1.0 - gate)
        return dx3 * gate, dgpre, jnp.sum(dgpre, axis=0, keepdims=True)

    d = dx3.shape[1]
    return _rowwise(name, fn, [dx3, gpre, pp], [], [(d, BF16), (d, BF16)], accs=[(1, d)])


def _rows(shape):
    return lax.broadcasted_iota(jnp.int32, shape, 0)


def _gelu(y):
    t = jnp.tanh(GELU_C0 * (y + GELU_C1 * y * y * y))
    return 0.5 * y * (1.0 + t), t


def _gelu_grad(y, t):
    return 0.5 * (1.0 + t) + 0.5 * y * (1.0 - t * t) * GELU_C0 * (1.0 + 3.0 * GELU_C1 * y * y)


def _neg_expm1(x):
    series = -x * (1.0 + x * (0.5 + x * (1.0 / 6.0 + x * (1.0 / 24.0))))
    return jnp.where(x > -0.02, series, 1.0 - jnp.exp(x))


def _softplus(x):
    return jnp.maximum(x, 0.0) + jnp.log(1.0 + jnp.exp(-jnp.abs(x)))


def _conv_fwd(xs, cw, cb):
    n = xs.shape[0]
    u = cw[3:4] * xs
    for k in (1, 2, 3):
        u = u + cw[3 - k:4 - k] * pltpu.roll(xs, k, 0)
    del n
    return u[SUBLANES:] + cb


def _lru_gates(u, wa, wx, ba, bx, sp, grow):
    ub = u.astype(BF16)
    r = jax.nn.sigmoid(jnp.dot(ub, wa, preferred_element_type=F32) + ba)
    ig = jax.nn.sigmoid(jnp.dot(ub, wx, preferred_element_type=F32) + bx)
    log_a = (-LRU_C) * r * sp
    a = jnp.exp(log_a)
    mult = jnp.sqrt(_neg_expm1(2.0 * log_a))
    mult = jnp.where(grow == 0, 1.0, mult)
    return ub, r, ig, a, mult


def _scan8_fwd(a, b):
    row = _rows(a.shape)
    for k in (1, 2, 4):
        m = row >= k
        b = jnp.where(m, a * pltpu.roll(b, k, 0) + b, b)
        a = jnp.where(m, a * pltpu.roll(a, k, 0), a)
    return a, b


def _scan8_bwd(c, d):
    row = _rows(c.shape)
    for k in (1, 2, 4):
        m = row < SUBLANES - k
        d = jnp.where(m, c * pltpu.roll(d, SUBLANES - k, 0) + d, d)
        c = jnp.where(m, c * pltpu.roll(c, SUBLANES - k, 0), c)
    return c, d


def _pad_copy(dst, src, front, back):
    s, c = src.shape
    if front:
        dst[pl.ds(0, front), :] = jnp.zeros((front, c), dst.dtype)
    if back:
        dst[pl.ds(front + s, back), :] = jnp.zeros((back, c), dst.dtype)
    dst[pl.ds(front, s), :] = src[...].astype(dst.dtype)


def _lru_fwd(name, proj, par, wa, wx):
    S = proj.shape[0]
    R = proj.shape[1] // 2
    H = R // LANES
    ch = _tile(S, SEQ_CHUNK)
    nch = S // ch
    H8 = SUBLANES

    def body(up_ref, y_ref, par_ref, wa_ref, wx_ref, gh_ref, h_ref, up_pad):
        _pad_copy(up_pad, up_ref, H8, 0)
        par = par_ref[...]
        cw, cb, ba, bx = par[0:4], par[4:5], par[5:6], par[6:7]
        sp = _softplus(-par[7:8])
        wa_m, wx_m = wa_ref[...], wx_ref[...]

        def chunk(ci, carry):
            r0 = pl.multiple_of(ci * ch, ch)
            xs = up_pad[pl.ds(r0, ch + H8), :]
            u = _conv_fwd(xs, cw, cb)
            grow = _rows(u.shape) + r0
            _, _, ig, a, mult = _lru_gates(u, wa_m, wx_m, ba, bx, sp, grow)
            bt = mult * (ig * u)
            hs = []
            for j in range(ch // H8):
                aa, bb = _scan8_fwd(a[j * H8:(j + 1) * H8], bt[j * H8:(j + 1) * H8])
                hj = bb + aa * carry
                carry = hj[H8 - 1:H8]
                hs.append(hj)
            h = jnp.concatenate(hs, axis=0)
            h_ref[pl.ds(r0, ch), :] = h
            gy, _ = _gelu(y_ref[pl.ds(r0, ch), :])
            gh_ref[pl.ds(r0, ch), :] = (h * gy).astype(gh_ref.dtype)
            return carry

        lax.fori_loop(0, nch, chunk, jnp.zeros((1, LANES), F32))

    col = lambda off: pl.BlockSpec((S, LANES), lambda h: (0, h + off))
    return pl.pallas_call(
        body,
        name=name,
        grid=(H,),
        in_specs=[col(0), col(H), pl.BlockSpec((8, LANES), lambda h: (0, h)),
                  pl.BlockSpec((None, LANES, LANES), lambda h: (h, 0, 0)),
                  pl.BlockSpec((None, LANES, LANES), lambda h: (h, 0, 0))],
        out_specs=[col(0), col(0)],
        out_shape=[jax.ShapeDtypeStruct((S, R), BF16), jax.ShapeDtypeStruct((S, R), F32)],
        scratch_shapes=[pltpu.VMEM((S + H8, LANES), F32)],
        compiler_params=_cparams("parallel"),
    )(proj, proj, par, wa, wx)


def _lru_bwd(name, proj, h, dgh, par, wa, wx):
    S = proj.shape[0]
    R = proj.shape[1] // 2
    H = R // LANES
    ch = _tile(S, SEQ_CHUNK)
    nch = S // ch
    H8 = SUBLANES
    nb = ch // H8

    def body(up_ref, y_ref, h_ref, dgh_ref, par_ref, wa_ref, wx_ref,
             dup_ref, dy_ref, dwa_ref, dwx_ref, dpar_ref, up_pad, h_pad, du_pad, vec_acc):
        _pad_copy(up_pad, up_ref, H8, 0)
        _pad_copy(h_pad, h_ref, H8, 0)
        du_pad[pl.ds(S, H8), :] = jnp.zeros((H8, LANES), F32)
        par = par_ref[...]
        cw, cb, ba, bx, lam = par[0:4], par[4:5], par[5:6], par[6:7], par[7:8]
        sp = _softplus(-lam)
        wa_m, wx_m = wa_ref[...], wx_ref[...]
        dwa_ref[...] = jnp.zeros_like(dwa_ref)
        dwx_ref[...] = jnp.zeros_like(dwx_ref)
        vec_acc[...] = jnp.zeros_like(vec_acc)
        nt_dims = (((1,), (1,)), ((), ()))
        tn_dims = (((0,), (0,)), ((), ()))

        def chunk(it, carry):
            lam_next, a_next = carry
            ci = nch - 1 - it
            r0 = pl.multiple_of(ci * ch, ch)
            xs = up_pad[pl.ds(r0, ch + H8), :]
            u = _conv_fwd(xs, cw, cb)
            row = _rows(u.shape)
            grow = row + r0
            ub, r, ig, a, mult = _lru_gates(u, wa_m, wx_m, ba, bx, sp, grow)
            hs = h_pad[pl.ds(r0, ch + H8), :]
            hcur = hs[H8:]
            hprev = pltpu.roll(hs, 1, 0)[H8:]
            y = y_ref[pl.ds(r0, ch), :]
            dgh = dgh_ref[pl.ds(r0, ch), :]
            gy, t = _gelu(y)
            dy_ref[pl.ds(r0, ch), :] = (dgh * hcur * _gelu_grad(y, t)).astype(dy_ref.dtype)
            dh = dgh * gy
            c = jnp.where(row == ch - 1, a_next, pltpu.roll(a, ch - 1, 0))
            ls = [None] * nb
            for j in range(nb - 1, -1, -1):
                cc, dd = _scan8_bwd(c[j * H8:(j + 1) * H8], dh[j * H8:(j + 1) * H8])
                lj = dd + cc * lam_next
                lam_next = lj[0:1]
                ls[j] = lj
            lmb = jnp.concatenate(ls, axis=0)
            da = lmb * hprev
            gu = ig * u
            dmult = lmb * gu
            dlog_a = da * a + jnp.where(grow == 0, 0.0, dmult * (-(a * a) / mult))
            dr = dlog_a * ((-LRU_C) * sp)
            drp = dr * r * (1.0 - r)
            dip = (lmb * mult * u) * ig * (1.0 - ig)
            drb, dib = drp.astype(BF16), dip.astype(BF16)
            du = (lmb * mult * ig
                  + lax.dot_general(drb, wa_m, nt_dims, preferred_element_type=F32)
                  + lax.dot_general(dib, wx_m, nt_dims, preferred_element_type=F32))
            du_pad[pl.ds(r0, ch), :] = du
            dwa_ref[...] += lax.dot_general(ub, drb, tn_dims, preferred_element_type=F32)
            dwx_ref[...] += lax.dot_general(ub, dib, tn_dims, preferred_element_type=F32)
            ssum = lambda v: jnp.sum(v, axis=0, keepdims=True)
            vec_acc[0:1, :] += ssum(drp)
            vec_acc[1:2, :] += ssum(dip)
            vec_acc[2:3, :] += ssum(dlog_a * ((-LRU_C) * r))
            return lam_next, a[0:1]

        zero = jnp.zeros((1, LANES), F32)
        lax.fori_loop(0, nch, chunk, (zero, zero))

        def conv_chunk(ci, acc):
            r0 = pl.multiple_of(ci * ch, ch)
            ds = du_pad[pl.ds(r0, ch + H8), :]
            xs = up_pad[pl.ds(r0, ch + H8), :]
            n = ch + H8
            du = ds[:ch]
            dup = cw[3:4] * du
            new = [acc[3] + jnp.sum(du * xs[H8:], axis=0, keepdims=True)]
            for k in (1, 2, 3):
                dup = dup + cw[3 - k:4 - k] * pltpu.roll(ds, n - k, 0)[:ch]
                new.append(acc[3 - k] + jnp.sum(du * pltpu.roll(xs, k, 0)[H8:], axis=0, keepdims=True))
            dup_ref[pl.ds(r0, ch), :] = dup.astype(dup_ref.dtype)
            return (new[3], new[2], new[1], new[0], acc[4] + jnp.sum(du, axis=0, keepdims=True))

        acc = lax.fori_loop(0, nch, conv_chunk, (zero,) * 5)
        dlam = vec_acc[2:3, :] * (-jax.nn.sigmoid(-lam))
        dpar_ref[...] = jnp.concatenate(list(acc) + [vec_acc[0:1, :], vec_acc[1:2, :], dlam], axis=0)

    col = lambda off: pl.BlockSpec((S, LANES), lambda h: (0, h + off))
    head = pl.BlockSpec((None, LANES, LANES), lambda h: (h, 0, 0))
    return pl.pallas_call(
        body,
        name=name,
        grid=(H,),
        in_specs=[col(0), col(H), col(0), col(0), pl.BlockSpec((8, LANES), lambda h: (0, h)), head, head],
        out_specs=[col(0), col(0), head, head, pl.BlockSpec((8, LANES), lambda h: (0, h))],
        out_shape=[jax.ShapeDtypeStruct((S, R), BF16), jax.ShapeDtypeStruct((S, R), BF16),
                   jax.ShapeDtypeStruct((H, LANES, LANES), F32), jax.ShapeDtypeStruct((H, LANES, LANES), F32),
                   jax.ShapeDtypeStruct((8, R), F32)],
        scratch_shapes=[pltpu.VMEM((S + H8, LANES), F32), pltpu.VMEM((S + H8, LANES), F32),
                        pltpu.VMEM((S + H8, LANES), F32), pltpu.VMEM((8, LANES), F32)],
        compiler_params=_cparams("parallel"),
    )(proj, proj, h, dgh, par, wa, wx)


def _window_sum(xs, g, up):
    n = xs.shape[0]
    s = xs
    for lvl, k in enumerate((1, 2, 4, 8)):
        sh = pltpu.roll(s, (n - k) if up else k, 0)
        s = s + jnp.where(g >= lvl, sh, 0.0)
    return s


def _pool_count(grow, g):
    return jnp.minimum(grow + 1, lax.shift_left(jnp.int32(2), g)).astype(F32)


def _pool_fwd(name, u, wgrp, par):
    S, D = u.shape
    G, W = wgrp.shape[0], wgrp.shape[1]
    ch = _tile(S, SEQ_CHUNK)
    nch = S // ch
    PH = POOL_HALO

    def body(u_ref, w_ref, par_ref, zs_ref, u_pad):
        g = pl.program_id(0)
        _pad_copy(u_pad, u_ref, PH, 0)
        par = par_ref[...]
        w = w_ref[...]

        def chunk(ci, _):
            r0 = pl.multiple_of(ci * ch, ch)
            xs = u_pad[pl.ds(r0, ch + PH), :]
            ws = _window_sum(xs, g, False)[PH:]
            uc = xs[PH:]
            cnt = _pool_count(_rows(uc.shape) + r0, g)
            pooled = ws / cnt - uc
            z = jnp.dot(pooled.astype(BF16), w, preferred_element_type=F32) + par[0:1]
            zs_ref[pl.ds(r0, ch), :] = (z * par[1:2]).astype(zs_ref.dtype)
            return 0

        lax.fori_loop(0, nch, chunk, 0)

    return pl.pallas_call(
        body,
        name=name,
        grid=(G,),
        in_specs=[pl.BlockSpec((S, W), lambda g: (0, g)), pl.BlockSpec((None, W, W), lambda g: (g, 0, 0)),
                  pl.BlockSpec((2, W), lambda g: (0, g))],
        out_specs=pl.BlockSpec((S, W), lambda g: (0, g)),
        out_shape=jax.ShapeDtypeStruct((S, D), BF16),
        scratch_shapes=[pltpu.VMEM((S + PH, W), F32)],
        compiler_params=_cparams("parallel"),
    )(u, wgrp, par)


def _pool_bwd(name, u, dzs, wgrp, par):
    S, D = u.shape
    G, W = wgrp.shape[0], wgrp.shape[1]
    ch = _tile(S, SEQ_CHUNK)
    nch = S // ch
    PH = POOL_HALO

    def body(u_ref, dzs_ref, w_ref, par_ref, du_ref, dw_ref, dpar_ref, u_pad, q_pad, dw_acc):
        g = pl.program_id(0)
        _pad_copy(u_pad, u_ref, PH, 0)
        q_pad[pl.ds(S, PH), :] = jnp.zeros((PH, W), F32)
        par = par_ref[...]
        w = w_ref[...]
        dw_acc[...] = jnp.zeros_like(dw_acc)

        def chunk(ci, acc):
            db, dsc = acc
            r0 = pl.multiple_of(ci * ch, ch)
            xs = u_pad[pl.ds(r0, ch + PH), :]
            ws = _window_sum(xs, g, False)[PH:]
            uc = xs[PH:]
            cnt = _pool_count(_rows(uc.shape) + r0, g)
            pooled = (ws / cnt - uc).astype(BF16)
            z = jnp.dot(pooled, w, preferred_element_type=F32) + par[0:1]
            dzs = dzs_ref[pl.ds(r0, ch), :]
            dz = dzs * par[1:2]
            dzb = dz.astype(BF16)
            dw_acc[...] += lax.dot_general(pooled, dzb, (((0,), (0,)), ((), ())), preferred_element_type=F32)
            dpooled = lax.dot_general(dzb, w, (((1,), (1,)), ((), ())), preferred_element_type=F32)
            q_pad[pl.ds(r0, ch), :] = dpooled / cnt
            return (db + jnp.sum(dz, axis=0, keepdims=True), dsc + jnp.sum(dzs * z, axis=0, keepdims=True))

        zero = jnp.zeros((1, W), F32)
        db, dsc = lax.fori_loop(0, nch, chunk, (zero, zero))
        dpar_ref[...] = jnp.concatenate([db, dsc], axis=0)
        dw_ref[...] = dw_acc[...].astype(dw_ref.dtype)

        def back(ci, _):
            r0 = pl.multiple_of(ci * ch, ch)
            qs = q_pad[pl.ds(r0, ch + PH), :]
            ws = _window_sum(qs, g, True)[:ch]
            qc = qs[:ch]
            cnt = _pool_count(_rows(qc.shape) + r0, g)
            du_ref[pl.ds(r0, ch), :] = (ws - qc * cnt).astype(du_ref.dtype)
            return 0

        lax.fori_loop(0, nch, back, 0)

    blk = pl.BlockSpec((S, W), lambda g: (0, g))
    wspec = pl.BlockSpec((None, W, W), lambda g: (g, 0, 0))
    pspec = pl.BlockSpec((2, W), lambda g: (0, g))
    return pl.pallas_call(
        body,
        name=name,
        grid=(G,),
        in_specs=[blk, blk, wspec, pspec],
        out_specs=[blk, wspec, pspec],
        out_shape=[jax.ShapeDtypeStruct((S, D), BF16), jax.ShapeDtypeStruct((G, W, W), BF16),
                   jax.ShapeDtypeStruct((2, D), F32)],
        scratch_shapes=[pltpu.VMEM((S + PH, W), F32), pltpu.VMEM((S + PH, W), F32), pltpu.VMEM((W, W), F32)],
        compiler_params=_cparams("parallel"),
    )(u, dzs, wgrp, par)


def _my_place():
    x, y, c = lax.axis_index("x"), lax.axis_index("y"), lax.axis_index("c")
    return x, y, c, 4 * x + 2 * y + c


def _peers(x, y, c):
    out = []
    for d in range(1, N_DEV):
        px = 1 - x if d & 4 else x
        py = 1 - y if d & 2 else y
        pc = 1 - c if d & 1 else c
        out.append(((px, py, pc), 4 * px + 2 * py + pc))
    return out


def _window(ref, axis, start, size):
    idx = [slice(None)] * len(ref.shape)
    idx[axis] = pl.ds(start, size)
    return ref.at[tuple(idx)]


def _to_bf16(name, arrs):
    outs = []
    for i, a in enumerate(arrs):
        a2 = a.reshape(-1, a.shape[-1])
        tr = _tile(a2.shape[0], 512)
        o = pl.pallas_call(
            lambda a_ref, o_ref: o_ref.__setitem__(Ellipsis, a_ref[...].astype(BF16)),
            name=f"{name}_{i}",
            grid=(a2.shape[0] // tr,),
            in_specs=[pl.BlockSpec((tr, a2.shape[1]), lambda r: (r, 0))],
            out_specs=pl.BlockSpec((tr, a2.shape[1]), lambda r: (r, 0)),
            out_shape=jax.ShapeDtypeStruct(a2.shape, BF16),
            compiler_params=_cparams("parallel"),
        )(a2)
        outs.append(o.reshape(a.shape))
    return outs


def _all_gather(name, shards, axes):
    n = len(shards)
    sizes = [s.shape[ax] for s, ax in zip(shards, axes)]

    def body(*refs):
        ins, outs = refs[:n], refs[n:2 * n]
        send, recv, loc = refs[2 * n:]
        x, y, c, me = _my_place()
        peers = _peers(x, y, c)
        local = []
        for i in range(n):
            dst = _window(outs[i], axes[i], me * sizes[i], sizes[i])
            cp = pltpu.make_async_copy(ins[i], dst, loc.at[i])
            cp.start()
            local.append(cp)
            for peer, _ in peers:
                pltpu.make_async_remote_copy(src_ref=ins[i], dst_ref=dst, send_sem=send.at[i], recv_sem=recv.at[i],
                                             device_id=peer, device_id_type=MESH).start()
        for i in range(n):
            local[i].wait()
            seven = _window(outs[i], axes[i], 0, (N_DEV - 1) * sizes[i])
            pltpu.make_async_remote_copy(src_ref=seven, dst_ref=seven, send_sem=send.at[i], recv_sem=recv.at[i],
                                         device_id=(x, y, c), device_id_type=MESH).wait()

    def full_shape(s, ax):
        shp = list(s.shape)
        shp[ax] *= N_DEV
        return jax.ShapeDtypeStruct(tuple(shp), s.dtype)

    any_spec = pl.BlockSpec(memory_space=pl.ANY)
    return pl.pallas_call(
        body,
        name=name,
        in_specs=[any_spec] * n,
        out_specs=[any_spec] * n,
        out_shape=[full_shape(s, ax) for s, ax in zip(shards, axes)],
        scratch_shapes=[pltpu.SemaphoreType.DMA((n,)), pltpu.SemaphoreType.DMA((n,)), pltpu.SemaphoreType.DMA((n,))],
        compiler_params=pltpu.CompilerParams(has_side_effects=True),
    )(*shards)


def _exchange(name, groups):
    n = len(groups)
    flat = [a for arrs, _ in groups for a in arrs]
    offs = []
    o = 0
    for arrs, _ in groups:
        offs.append(o)
        o += len(arrs)
    nf = len(flat)
    sizes = [arrs[0].shape[ax] // N_DEV for arrs, ax in groups]

    def body(*refs):
        ins, outs = refs[:nf], refs[nf:nf + n]
        send, recv, loc = refs[nf + n:]
        x, y, c, me = _my_place()
        peers = _peers(x, y, c)
        local = []
        for gi, (arrs, ax) in enumerate(groups):
            for li in range(len(arrs)):
                src_ref = ins[offs[gi] + li]
                k = offs[gi] + li
                cp = pltpu.make_async_copy(_window(src_ref, ax, me * sizes[gi], sizes[gi]), outs[gi].at[me, li],
                                           loc.at[k])
                cp.start()
                local.append(cp)
                for peer, pidx in peers:
                    pltpu.make_async_remote_copy(
                        src_ref=_window(src_ref, ax, pidx * sizes[gi], sizes[gi]), dst_ref=outs[gi].at[me, li],
                        send_sem=send.at[k], recv_sem=recv.at[k], device_id=peer, device_id_type=MESH).start()
        for gi, (arrs, ax) in enumerate(groups):
            for li in range(len(arrs)):
                k = offs[gi] + li
                local[k].wait()
                seven = outs[gi].at[pl.ds(0, N_DEV - 1), li]
                pltpu.make_async_remote_copy(src_ref=seven, dst_ref=seven, send_sem=send.at[k], recv_sem=recv.at[k],
                                             device_id=(x, y, c), device_id_type=MESH).wait()

    def buf_shape(arrs, ax):
        shp = list(arrs[0].shape)
        shp[ax] //= N_DEV
        return jax.ShapeDtypeStruct((N_DEV, len(arrs)) + tuple(shp), arrs[0].dtype)

    any_spec = pl.BlockSpec(memory_space=pl.ANY)
    return pl.pallas_call(
        body,
        name=name,
        in_specs=[any_spec] * nf,
        out_specs=[any_spec] * n,
        out_shape=[buf_shape(arrs, ax) for arrs, ax in groups],
        scratch_shapes=[pltpu.SemaphoreType.DMA((nf,)), pltpu.SemaphoreType.DMA((nf,)), pltpu.SemaphoreType.DMA((nf,))],
        compiler_params=pltpu.CompilerParams(has_side_effects=True),
    )(*flat)


def _adamw_math(w, g, m, v):
    m = ADAM_B1 * m + (1.0 - ADAM_B1) * g
    v = ADAM_B2 * v + (1.0 - ADAM_B2) * jnp.square(g)
    m_hat = m / (1.0 - ADAM_B1 ** ADAM_STEP)
    v_hat = v / (1.0 - ADAM_B2 ** ADAM_STEP)
    delta = -ADAM_LR * (m_hat / (jnp.sqrt(v_hat) + ADAM_EPS) + ADAM_WD * w)
    return delta, m, v


def _sum_slots(buf_ref):
    g = buf_ref[0].astype(F32)
    for s in range(1, N_DEV):
        g = g + buf_ref[s].astype(F32)
    return g


def _adamw_big(name, buf, w, m, v):
    shape = w.shape
    L, C = shape[0], shape[-1]
    Rr = math.prod(shape[1:-1])
    buf3 = buf.reshape(N_DEV, L, Rr, C)
    w3, m3, v3 = (t.reshape(L, Rr, C) for t in (w, m, v))
    tr = _tile(Rr, 128) if Rr % 128 == 0 else Rr

    def body(buf_ref, w_ref, m_ref, v_ref, g_out, d_out, m_out, v_out):
        g = _sum_slots(buf_ref)
        d, mm, vv = _adamw_math(w_ref[...], g, m_ref[...], v_ref[...])
        g_out[...] = g
        d_out[...] = d
        m_out[...] = mm
        v_out[...] = vv

    spec = pl.BlockSpec((None, tr, C), lambda l, r: (l, r, 0))
    outs = pl.pallas_call(
        body,
        name=name,
        grid=(L, Rr // tr),
        in_specs=[pl.BlockSpec((N_DEV, None, tr, C), lambda l, r: (0, l, r, 0)), spec, spec, spec],
        out_specs=[spec] * 4,
        out_shape=[jax.ShapeDtypeStruct((L, Rr, C), F32)] * 4,
        compiler_params=_cparams("parallel", "parallel"),
    )(buf3, w3, m3, v3)
    return [o.reshape(shape) for o in outs]


def _sum8(name, buf):
    R = buf.shape[1]

    def body(buf_ref, o_ref):
        o_ref[...] = _sum_slots(buf_ref)

    return pl.pallas_call(
        body,
        name=name,
        in_specs=[pl.BlockSpec(buf.shape, lambda: (0, 0, 0))],
        out_specs=pl.BlockSpec((R, LANES), lambda: (0, 0)),
        out_shape=jax.ShapeDtypeStruct((R, LANES), F32),
        compiler_params=_cparams(),
    )(buf)


def _adamw_packed(name, w, g, m, v):
    R = w.shape[0]
    tr = _row_tile(R, 512)

    def body(w_ref, g_ref, m_ref, v_ref, d_out, m_out, v_out):
        d, mm, vv = _adamw_math(w_ref[...], g_ref[...], m_ref[...], v_ref[...])
        d_out[...] = d
        m_out[...] = mm
        v_out[...] = vv

    spec = pl.BlockSpec((tr, LANES), lambda r: (r, 0))
    return pl.pallas_call(
        body,
        name=name,
        grid=(R // tr,),
        in_specs=[spec] * 4,
        out_specs=[spec] * 3,
        out_shape=[jax.ShapeDtypeStruct((R, LANES), F32)] * 3,
        compiler_params=_cparams("parallel"),
    )(w, g, m, v)


def _pack(arrs, pad_rows_to=SUBLANES):
    parts = []
    for a in arrs:
        flat = a.reshape(-1)
        per = LANES * pad_rows_to
        padded = -(-flat.shape[0] // per) * per
        if padded != flat.shape[0]:
            flat = jnp.pad(flat, (0, padded - flat.shape[0]))
        parts.append(flat.reshape(-1, LANES))
    return jnp.concatenate(parts, axis=0)


def _unpack(packed, shapes, pad_rows_to=SUBLANES):
    out = []
    r = 0
    for shp in shapes:
        nel = math.prod(shp)
        per = LANES * pad_rows_to
        rows = -(-nel // per) * pad_rows_to
        out.append(packed[r:r + rows].reshape(-1)[:nel].reshape(shp))
        r += rows
    return out


BIG = ("lru_w_in", "lru_w_out", "pool_w_in", "pool_w_grp", "pool_w_out", "mlp_w1", "mlp_w2", "ple_w", "ple_gate_w")
BIG_AXIS = {"lru_w_in": 2, "lru_w_out": 1, "pool_w_in": 1, "pool_w_grp": 2, "pool_w_out": 1, "mlp_w1": 2,
            "mlp_w2": 1, "ple_w": 2, "ple_gate_w": 1}
SMALL_SHARDED = ("lru_conv_w", "pool_b_grp", "pool_scale")
REPLICATED = ("lru_conv_b", "lru_wa", "lru_ba", "lru_wx", "lru_bx", "lru_lambda", "ln_mix_g", "ln_mix_b",
              "ln_mlp_g", "ln_mlp_b", "ple_gate_b")
WEIGHTS = ("lru_w_in", "lru_conv_w", "lru_conv_b", "lru_wa", "lru_ba", "lru_wx", "lru_bx", "lru_lambda", "lru_w_out",
           "pool_w_in", "pool_w_grp", "pool_b_grp", "pool_scale", "pool_w_out", "ln_mix_g", "ln_mix_b", "mlp_w1",
           "mlp_w2", "ln_mlp_g", "ln_mlp_b", "ple_w", "ple_gate_w", "ple_gate_b")
INPUTS = ("x", "p") + WEIGHTS + ("loss_target",) + tuple("m_" + n for n in WEIGHTS) + tuple("v_" + n for n in WEIGHTS)


def _gather_last_axis(packed_full, shard_shape):
    nel = math.prod(shard_shape)
    blocks = packed_full.reshape(N_DEV, -1)[:, :nel].reshape((N_DEV,) + tuple(shard_shape))
    return jnp.concatenate([blocks[d] for d in range(N_DEV)], axis=-1)


def kernel(x, p, lru_w_in, lru_conv_w, lru_conv_b, lru_wa, lru_ba, lru_wx, lru_bx, lru_lambda, lru_w_out, pool_w_in, pool_w_grp, pool_b_grp, pool_scale, pool_w_out, ln_mix_g, ln_mix_b, mlp_w1, mlp_w2, ln_mlp_g, ln_mlp_b, ple_w, ple_gate_w, ple_gate_b, loss_target, m_lru_w_in, m_lru_conv_w, m_lru_conv_b, m_lru_wa, m_lru_ba, m_lru_wx, m_lru_bx, m_lru_lambda, m_lru_w_out, m_pool_w_in, m_pool_w_grp, m_pool_b_grp, m_pool_scale, m_pool_w_out, m_ln_mix_g, m_ln_mix_b, m_mlp_w1, m_mlp_w2, m_ln_mlp_g, m_ln_mlp_b, m_ple_w, m_ple_gate_w, m_ple_gate_b, v_lru_w_in, v_lru_conv_w, v_lru_conv_b, v_lru_wa, v_lru_ba, v_lru_wx, v_lru_bx, v_lru_lambda, v_lru_w_out, v_pool_w_in, v_pool_w_grp, v_pool_b_grp, v_pool_scale, v_pool_w_out, v_ln_mix_g, v_ln_mix_b, v_mlp_w1, v_mlp_w2, v_ln_mlp_g, v_ln_mlp_b, v_ple_w, v_ple_gate_w, v_ple_gate_b):
    A = dict(zip(INPUTS, (x, p, lru_w_in, lru_conv_w, lru_conv_b, lru_wa, lru_ba, lru_wx, lru_bx, lru_lambda, lru_w_out, pool_w_in, pool_w_grp, pool_b_grp, pool_scale, pool_w_out, ln_mix_g, ln_mix_b, mlp_w1, mlp_w2, ln_mlp_g, ln_mlp_b, ple_w, ple_gate_w, ple_gate_b, loss_target, m_lru_w_in, m_lru_conv_w, m_lru_conv_b, m_lru_wa, m_lru_ba, m_lru_wx, m_lru_bx, m_lru_lambda, m_lru_w_out, m_pool_w_in, m_pool_w_grp, m_pool_b_grp, m_pool_scale, m_pool_w_out, m_ln_mix_g, m_ln_mix_b, m_mlp_w1, m_mlp_w2, m_ln_mlp_g, m_ln_mlp_b, m_ple_w, m_ple_gate_w, m_ple_gate_b, v_lru_w_in, v_lru_conv_w, v_lru_conv_b, v_lru_wa, v_lru_ba, v_lru_wx, v_lru_bx, v_lru_lambda, v_lru_w_out, v_pool_w_in, v_pool_w_grp, v_pool_b_grp, v_pool_scale, v_pool_w_out, v_ln_mix_g, v_ln_mix_b, v_mlp_w1, v_mlp_w2, v_ln_mlp_g, v_ln_mlp_b, v_ple_w, v_ple_gate_w, v_ple_gate_b)))
    depth = ln_mix_g.shape[0]
    alpha = (2 * depth) ** 0.25
    S, D = x.shape[1], x.shape[2]
    xs = x.reshape(S, D)
    tgt = loss_target.reshape(S, D)
    p3 = p.reshape(depth, S, p.shape[-1])
    me = 4 * lax.axis_index("x") + 2 * lax.axis_index("y") + lax.axis_index("c")

    small_shard_shapes = [A[n].shape for n in SMALL_SHARDED]
    small_packed = _pack([A[n] for n in SMALL_SHARDED])
    gathered = _all_gather("gather_weights", _to_bf16("cast_w", [A[n] for n in BIG]) + [small_packed],
                           [BIG_AXIS[n] for n in BIG] + [0])
    W = dict(zip(BIG, gathered[:-1]))
    small_full = gathered[-1].reshape(N_DEV, -1, LANES)
    r = 0
    for n, shp in zip(SMALL_SHARDED, small_shard_shapes):
        rows = -(-math.prod(shp) // (LANES * SUBLANES)) * SUBLANES
        W[n] = _gather_last_axis(small_full[:, r:r + rows], shp)
        r += rows
    wa_b, wx_b = _to_bf16("cast_gates", [lru_wa, lru_wx])
    n_lru = lru_w_in.shape[0]
    lru_par = [jnp.concatenate([W["lru_conv_w"][s], lru_conv_b[s][None], lru_ba[s][None], lru_bx[s][None],
                                lru_lambda[s][None]], axis=0) for s in range(n_lru)]
    pool_par = [jnp.stack([W["pool_b_grp"][s], W["pool_scale"][s]], axis=0) for s in range(pool_w_in.shape[0])]

    saved = []
    h_in = xs
    for i in range(depth):
        s = i // 2
        sv = {"x0": h_in}
        if i % 2 == 0:
            sv["proj"] = _mm(f"l{i}_lru_in", h_in, W["lru_w_in"], "nn", [F32], b_lead=s)
            sv["gh"], sv["h"] = _lru_fwd(f"l{i}_lru_core", sv["proj"], lru_par[s], wa_b[s], wx_b[s])
            sv["mix"] = _mm(f"l{i}_lru_out", sv["gh"], W["lru_w_out"], "nn", [F32], b_lead=s)
        else:
            sv["u"] = _mm(f"l{i}_pool_in", h_in, W["pool_w_in"], "nn", [F32], b_lead=s)
            sv["zs"] = _pool_fwd(f"l{i}_pool_core", sv["u"], W["pool_w_grp"][s], pool_par[s])
            sv["mix"] = _mm(f"l{i}_pool_out", sv["zs"], W["pool_w_out"], "nn", [F32], b_lead=s)
        sv["x1"] = _ln_fwd(f"l{i}_ln_mix", alpha, h_in, sv["mix"], ln_mix_g[i][None], ln_mix_b[i][None])
        sv["hpre"], sv["hact"] = _mm(f"l{i}_mlp_up", sv["x1"], W["mlp_w1"], "nn", [BF16, BF16], b_lead=i,
                                     epi=lambda acc: (acc, jnp.square(jnp.maximum(acc, 0.0))))
        sv["mo"] = _mm(f"l{i}_mlp_down", sv["hact"], W["mlp_w2"], "nn", [F32], b_lead=i)
        sv["x2"] = _ln_fwd(f"l{i}_ln_mlp", alpha, sv["x1"], sv["mo"], ln_mlp_g[i][None], ln_mlp_b[i][None])
        sv["pp"] = _mm(f"l{i}_ple_up", p3, W["ple_w"], "nn", [F32], a_lead=i, b_lead=i)

        def ple_epi(acc, bg, x2t, ppt):
            gpre = acc + bg
            return x2t + ppt * jax.nn.sigmoid(gpre), gpre

        h_in, sv["gpre"] = _mm(f"l{i}_ple_gate", sv["x2"], W["ple_gate_w"], "nn", [F32, F32], b_lead=i,
                               epi=ple_epi, extras=[ple_gate_b[i][None], sv["x2"], sv["pp"]])
        saved.append(sv)

    dx, sq = _loss_and_grad("loss", h_in, tgt)
    loss = lax.psum(0.5 * sq[0, 0] / D, ("x", "y", "c"))

    dW = {n: [None] * A[n].shape[0] for n in BIG}
    dsmall = {n: [None] * A[n].shape[0] for n in REPLICATED + SMALL_SHARDED}
    for i in reversed(range(depth)):
        s = i // 2
        sv = saved[i]
        dpp, dgpre, dbg = _ple_bwd(f"l{i}_ple_bwd", dx, sv["gpre"], sv["pp"])
        dsmall["ple_gate_b"][i] = dbg[0]
        dW["ple_w"][i] = _mm(f"l{i}_d_ple_w", p3, dpp, "tn", [BF16], a_lead=i)
        dW["ple_gate_w"][i] = _mm(f"l{i}_d_ple_gate_w", sv["x2"], dgpre, "tn", [BF16])
        dx2 = _mm(f"l{i}_d_x2", dgpre, W["ple_gate_w"], "nt", [F32], b_lead=i, extras=[dx],
                  epi=lambda acc, d: (acc + d,))
        dz2, dg, db = _ln_bwd(f"l{i}_ln_mlp_bwd", alpha, dx2, sv["x1"], sv["mo"], ln_mlp_g[i][None])
        dsmall["ln_mlp_g"][i], dsmall["ln_mlp_b"][i] = dg[0], db[0]
        dhpre = _mm(f"l{i}_d_hpre", dz2, W["mlp_w2"], "nt", [BF16], b_lead=i, extras=[sv["hpre"]],
                    epi=lambda acc, hp: (acc * (2.0 * jnp.maximum(hp.astype(F32), 0.0)),))
        dW["mlp_w2"][i] = _mm(f"l{i}_d_mlp_w2", sv["hact"], dz2, "tn", [BF16])
        dW["mlp_w1"][i] = _mm(f"l{i}_d_mlp_w1", sv["x1"], dhpre, "tn", [BF16])
        dx1 = _mm(f"l{i}_d_x1", dhpre, W["mlp_w1"], "nt", [F32], b_lead=i, extras=[dz2],
                  epi=lambda acc, d: (acc + alpha * d,))
        dz1, dg, db = _ln_bwd(f"l{i}_ln_mix_bwd", alpha, dx1, sv["x0"], sv["mix"], ln_mix_g[i][None])
        dsmall["ln_mix_g"][i], dsmall["ln_mix_b"][i] = dg[0], db[0]
        if i % 2 == 0:
            dW["lru_w_out"][s] = _mm(f"l{i}_d_lru_w_out", sv["gh"], dz1, "tn", [BF16])
            dgh = _mm(f"l{i}_d_gh", dz1, W["lru_w_out"], "nt", [F32], b_lead=s)
            dup, dy, dwa, dwx, dpar = _lru_bwd(f"l{i}_lru_core_bwd", sv["proj"], sv["h"], dgh, lru_par[s],
                                               wa_b[s], wx_b[s])
            dsmall["lru_wa"][s], dsmall["lru_wx"][s] = dwa, dwx
            dsmall["lru_conv_w"][s] = dpar[0:4]
            for k, n in enumerate(("lru_conv_b", "lru_ba", "lru_bx", "lru_lambda")):
                dsmall[n][s] = dpar[4 + k]
            dmix_in = jnp.concatenate([dup, dy], axis=1)
            win = "lru_w_in"
        else:
            dW["pool_w_out"][s] = _mm(f"l{i}_d_pool_w_out", sv["zs"], dz1, "tn", [BF16])
            dzs = _mm(f"l{i}_d_zs", dz1, W["pool_w_out"], "nt", [F32], b_lead=s)
            dmix_in, dW["pool_w_grp"][s], dpar = _pool_bwd(f"l{i}_pool_core_bwd", sv["u"], dzs, W["pool_w_grp"][s],
                                                          pool_par[s])
            dsmall["pool_b_grp"][s], dsmall["pool_scale"][s] = dpar[0], dpar[1]
            win = "pool_w_in"
        dW[win][s] = _mm(f"l{i}_d_{win}", sv["x0"], dmix_in, "tn", [BF16])
        dx = _mm(f"l{i}_d_x0", dmix_in, W[win], "nt", [F32], b_lead=s, extras=[dz1],
                 epi=lambda acc, d: (acc + alpha * d,))
    grad_x = dx.reshape(x.shape)

    small_names = REPLICATED + SMALL_SHARDED
    small_grads = [jnp.stack(dsmall[n]) for n in small_names]
    small_shapes = [g.shape for g in small_grads]
    packed_g = _pack(small_grads)
    assert packed_g.shape[0] % (N_DEV * SUBLANES) == 0, packed_g.shape
    groups = [(dW[n], BIG_AXIS[n] - 1) for n in BIG] + [([packed_g], 0)]
    bufs = _exchange("exchange_grads", groups)

    outs = {}
    for n, buf in zip(BIG, bufs[:-1]):
        outs[n] = _adamw_big(f"adamw_{n}", buf, A[n], A["m_" + n], A["v_" + n])
    red = _sum8("sum_small", bufs[-1][:, 0])
    red_full = _all_gather("gather_small", [red], [0])[0]
    small_g = dict(zip(small_names, _unpack(red_full, small_shapes)))
    for n in SMALL_SHARDED:
        width = A[n].shape[-1]
        small_g[n] = lax.dynamic_slice_in_dim(small_g[n], me * width, width, axis=small_g[n].ndim - 1)
    pk = lambda pre: _pack([A[pre + n] for n in small_names])
    d_p, m_p, v_p = _adamw_packed("adamw_small", pk(""), _pack([small_g[n] for n in small_names]), pk("m_"), pk("v_"))
    shapes = [A[n].shape for n in small_names]
    for n, d_, m_, v_ in zip(small_names, _unpack(d_p, shapes), _unpack(m_p, shapes), _unpack(v_p, shapes)):
        outs[n] = [small_g[n], d_, m_, v_]

    return (loss, grad_x, *[outs[n][0] for n in WEIGHTS], *[outs[n][1] for n in WEIGHTS],
            *[outs[n][2] for n in WEIGHTS], *[outs[n][3] for n in WEIGHTS])
```

```python
import functools
import math

import jax
import jax.numpy as jnp
from jax import lax
from jax.experimental import pallas as pl
from jax.experimental.pallas import tpu as pltpu

F32 = jnp.float32
BF16 = jnp.bfloat16
MESH = pl.DeviceIdType.MESH
N_DEV = 8
LANES = 128
SUBLANES = 8

LN_EPS = 1e-5
LRU_C = 8.0
CONV_WIDTH = 4
POOL_HALO = 16
ADAM_LR = 0.001
ADAM_B1 = 0.9
ADAM_B2 = 0.999
ADAM_EPS = 1e-08
ADAM_WD = 0.01
ADAM_STEP = 10

VMEM_LIMIT = 48 * 1024 * 1024
SEQ_CHUNK = 256
MM_TK = 2048
MM_TK_TOKENS = 1024
GELU_C0 = math.sqrt(2.0 / math.pi)
GELU_C1 = 0.044715


def _cparams(*sem):
    return pltpu.CompilerParams(dimension_semantics=tuple(sem) if sem else None, vmem_limit_bytes=VMEM_LIMIT)


def _tile(n, pref):
    if n <= pref:
        return n
    t = pref - pref % LANES
    while t >= LANES:
        if n % t == 0:
            return t
        t -= LANES
    return n


def _row_tile(n, pref):
    if n <= pref:
        return n
    t = pref - pref % SUBLANES
    while t >= SUBLANES:
        if n % t == 0:
            return t
        t -= SUBLANES
    return n


def _mm(name, a, b, mode, out_dtypes, epi=None, extras=(), a_lead=None, b_lead=None, tm=1024, tn=512, tk=None):
    a2 = a.shape[-2:]
    b2 = b.shape[-2:]
    if mode == "nn":
        (M, K), N = a2, b2[1]
        assert b2[0] == K
    elif mode == "nt":
        (M, K), N = a2, b2[0]
        assert b2[1] == K
    else:
        (K, M), N = a2, b2[1]
        assert b2[0] == K
    if tk is None:
        tk = MM_TK_TOKENS if mode == "tn" else MM_TK
    tm, tn, tk = _tile(M, tm), _tile(N, tn), _tile(K, tk)
    nk = K // tk
    n_extra = len(extras)
    n_out = len(out_dtypes)

    def lead(shape, idx, which):
        if which is None:
            return pl.BlockSpec(shape, idx)
        return pl.BlockSpec((None,) + shape, lambda i, j, k: (which,) + idx(i, j, k))

    if mode == "nn":
        a_spec = lead((tm, tk), lambda i, j, k: (i, k), a_lead)
        b_spec = lead((tk, tn), lambda i, j, k: (k, j), b_lead)
        dims = (((1,), (0,)), ((), ()))
    elif mode == "nt":
        a_spec = lead((tm, tk), lambda i, j, k: (i, k), a_lead)
        b_spec = lead((tn, tk), lambda i, j, k: (j, k), b_lead)
        dims = (((1,), (1,)), ((), ()))
    else:
        a_spec = lead((tk, tm), lambda i, j, k: (k, i), a_lead)
        b_spec = lead((tk, tn), lambda i, j, k: (k, j), b_lead)
        dims = (((0,), (0,)), ((), ()))
    e_specs = []
    for e in extras:
        if e.shape[0] == 1:
            e_specs.append(pl.BlockSpec((1, tn), lambda i, j, k: (0, j)))
        else:
            e_specs.append(pl.BlockSpec((tm, tn), lambda i, j, k: (i, j)))

    def body(a_ref, b_ref, *rest):
        e_refs = rest[:n_extra]
        o_refs = rest[n_extra:n_extra + n_out]
        part = lax.dot_general(a_ref[...].astype(BF16), b_ref[...].astype(BF16), dims, preferred_element_type=F32)

        def finish(r):
            res = (r,) if epi is None else epi(r, *[e[...] for e in e_refs])
            for o, v in zip(o_refs, res):
                o[...] = v.astype(o.dtype)

        if nk == 1:
            finish(part)
            return
        acc = rest[n_extra + n_out]
        k = pl.program_id(2)

        @pl.when(k == 0)
        def _():
            acc[...] = part

        @pl.when(jnp.logical_and(k > 0, k < nk - 1))
        def _():
            acc[...] += part

        @pl.when(k == nk - 1)
        def _():
            finish(acc[...] + part)

    outs = pl.pallas_call(
        body,
        name=name,
        grid=(M // tm, N // tn, nk),
        in_specs=[a_spec, b_spec] + e_specs,
        out_specs=[pl.BlockSpec((tm, tn), lambda i, j, k: (i, j)) for _ in out_dtypes],
        out_shape=[jax.ShapeDtypeStruct((M, N), d) for d in out_dtypes],
        scratch_shapes=[pltpu.VMEM((tm, tn), F32)] if nk > 1 else [],
        compiler_params=_cparams("parallel", "parallel", "arbitrary"),
    )(a, b, *extras)
    return outs[0] if n_out == 1 else tuple(outs)


def _rowwise(name, fn, tiled, params, outs, accs=(), tm=256):
    S = tiled[0].shape[0]
    tm = _tile(S, tm)
    nt, npar, no = len(tiled), len(params), len(outs)

    def body(*refs):
        t_refs = refs[:nt]
        p_refs = refs[nt:nt + npar]
        o_refs = refs[nt + npar:nt + npar + no]
        a_refs = refs[nt + npar + no:]
        res = fn(*[r[...] for r in t_refs], *[r[...] for r in p_refs])
        for o, v in zip(o_refs, res[:no]):
            o[...] = v.astype(o.dtype)
        first = pl.program_id(0) == 0
        for ar, v in zip(a_refs, res[no:]):
            @pl.when(first)
            def _(ar=ar, v=v):
                ar[...] = v

            @pl.when(jnp.logical_not(first))
            def _(ar=ar, v=v):
                ar[...] += v

    full = lambda p: pl.BlockSpec(p.shape, lambda i, nd=p.ndim: (0,) * nd)
    res = pl.pallas_call(
        body,
        name=name,
        grid=(S // tm,),
        in_specs=[pl.BlockSpec((tm, t.shape[1]), lambda i: (i, 0)) for t in tiled] + [full(p) for p in params],
        out_specs=[pl.BlockSpec((tm, c), lambda i: (i, 0)) for c, _ in outs]
        + [pl.BlockSpec(s, lambda i, nd=len(s): (0,) * nd) for s in accs],
        out_shape=[jax.ShapeDtypeStruct((S, c), d) for c, d in outs] + [jax.ShapeDtypeStruct(s, F32) for s in accs],
        compiler_params=_cparams("arbitrary"),
    )(*tiled, *params)
    return res


def _ln_stats(z):
    mu = jnp.mean(z, axis=-1, keepdims=True)
    zc = z - mu
    var = jnp.mean(zc * zc, axis=-1, keepdims=True)
    return zc, lax.rsqrt(var + LN_EPS)


def _ln_fwd(name, alpha, xp, m, g, b):
    def fn(xp, m, g, b):
        zc, rstd = _ln_stats(alpha * xp + m)
        return (zc * rstd * g + b,)

    return _rowwise(name, fn, [xp, m], [g, b], [(xp.shape[1], F32)])[0]


def _ln_bwd(name, alpha, dy, xp, m, g):
    def fn(dy, xp, m, g):
        zc, rstd = _ln_stats(alpha * xp + m)
        xhat = zc * rstd
        dxh = dy * g
        m1 = jnp.mean(dxh, axis=-1, keepdims=True)
        m2 = jnp.mean(dxh * xhat, axis=-1, keepdims=True)
        dz = rstd * (dxh - m1 - xhat * m2)
        return dz, jnp.sum(dy * xhat, axis=0, keepdims=True), jnp.sum(dy, axis=0, keepdims=True)

    d = xp.shape[1]
    return _rowwise(name, fn, [dy, xp, m], [g], [(d, F32)], accs=[(1, d), (1, d)])


def _loss_and_grad(name, y, target):
    d = y.shape[1]

    def fn(y, t):
        err = y - t
        sq = jnp.sum(jnp.sum(err * err, axis=0, keepdims=True), axis=1, keepdims=True)
        return err * (1.0 / d), jnp.broadcast_to(sq, (1, LANES))

    return _rowwise(name, fn, [y, target], [], [(d, F32)], accs=[(1, LANES)])


def _ple_bwd(name, dx3, gpre, pp):
    def fn(dx3, gpre, pp):
        gate = jax.nn.sigmoid(gpre)
        dgpre = dx3 * pp * gate * (1.0 - gate)
        return dx3 * gate, dgpre, jnp.sum(dgpre, axis=0, keepdims=True)

    d = dx3.shape[1]
    return _rowwise(name, fn, [dx3, gpre, pp], [], [(d, BF16), (d, BF16)], accs=[(1, d)])


def _rows(shape):
    return lax.broadcasted_iota(jnp.int32, shape, 0)


def _gelu(y):
    t = jnp.tanh(GELU_C0 * (y + GELU_C1 * y * y * y))
    return 0.5 * y * (1.0 + t), t


def _gelu_grad(y, t):
    return 0.5 * (1.0 + t) + 0.5 * y * (1.0 - t * t) * GELU_C0 * (1.0 + 3.0 * GELU_C1 * y * y)


def _neg_expm1(x):
    series = -x * (1.0 + x * (0.5 + x * (1.0 / 6.0 + x * (1.0 / 24.0))))
    return jnp.where(x > -0.02, series, 1.0 - jnp.exp(x))


def _softplus(x):
    return jnp.maximum(x, 0.0) + jnp.log(1.0 + jnp.exp(-jnp.abs(x)))


def _conv_fwd(xs, cw, cb):
    n = xs.shape[0]
    u = cw[3:4] * xs
    for k in (1, 2, 3):
        u = u + cw[3 - k:4 - k] * pltpu.roll(xs, k, 0)
    del n
    return u[SUBLANES:] + cb


def _lru_gates(u, wa, wx, ba, bx, sp, grow):
    ub = u.astype(BF16)
    r = jax.nn.sigmoid(jnp.dot(ub, wa, preferred_element_type=F32) + ba)
    ig = jax.nn.sigmoid(jnp.dot(ub, wx, preferred_element_type=F32) + bx)
    log_a = (-LRU_C) * r * sp
    a = jnp.exp(log_a)
    mult = jnp.sqrt(_neg_expm1(2.0 * log_a))
    mult = jnp.where(grow == 0, 1.0, mult)
    return ub, r, ig, a, mult


def _scan8_fwd(a, b):
    row = _rows(a.shape)
    for k in (1, 2, 4):
        m = row >= k
        b = jnp.where(m, a * pltpu.roll(b, k, 0) + b, b)
        a = jnp.where(m, a * pltpu.roll(a, k, 0), a)
    return a, b


def _scan8_bwd(c, d):
    row = _rows(c.shape)
    for k in (1, 2, 4):
        m = row < SUBLANES - k
        d = jnp.where(m, c * pltpu.roll(d, SUBLANES - k, 0) + d, d)
        c = jnp.where(m, c * pltpu.roll(c, SUBLANES - k, 0), c)
    return c, d


def _pad_copy(dst, src, front, back):
    s, c = src.shape
    if front:
        dst[pl.ds(0, front), :] = jnp.zeros((front, c), dst.dtype)
    if back:
        dst[pl.ds(front + s, back), :] = jnp.zeros((back, c), dst.dtype)
    dst[pl.ds(front, s), :] = src[...].astype(dst.dtype)


def _lru_fwd(name, proj, par, wa, wx):
    S = proj.shape[0]
    R = proj.shape[1] // 2
    H = R // LANES
    ch = _tile(S, SEQ_CHUNK)
    nch = S // ch
    H8 = SUBLANES

    def body(up_ref, y_ref, par_ref, wa_ref, wx_ref, gh_ref, h_ref, up_pad):
        _pad_copy(up_pad, up_ref, H8, 0)
        par = par_ref[...]
        cw, cb, ba, bx = par[0:4], par[4:5], par[5:6], par[6:7]
        sp = _softplus(-par[7:8])
        wa_m, wx_m = wa_ref[...], wx_ref[...]

        def chunk(ci, carry):
            r0 = pl.multiple_of(ci * ch, ch)
            xs = up_pad[pl.ds(r0, ch + H8), :]
            u = _conv_fwd(xs, cw, cb)
            grow = _rows(u.shape) + r0
            _, _, ig, a, mult = _lru_gates(u, wa_m, wx_m, ba, bx, sp, grow)
            bt = mult * (ig * u)
            hs = []
            for j in range(ch // H8):
                aa, bb = _scan8_fwd(a[j * H8:(j + 1) * H8], bt[j * H8:(j + 1) * H8])
                hj = bb + aa * carry
                carry = hj[H8 - 1:H8]
                hs.append(hj)
            h = jnp.concatenate(hs, axis=0)
            h_ref[pl.ds(r0, ch), :] = h
            gy, _ = _gelu(y_ref[pl.ds(r0, ch), :])
            gh_ref[pl.ds(r0, ch), :] = (h * gy).astype(gh_ref.dtype)
            return carry

        lax.fori_loop(0, nch, chunk, jnp.zeros((1, LANES), F32))

    col = lambda off: pl.BlockSpec((S, LANES), lambda h: (0, h + off))
    return pl.pallas_call(
        body,
        name=name,
        grid=(H,),
        in_specs=[col(0), col(H), pl.BlockSpec((8, LANES), lambda h: (0, h)),
                  pl.BlockSpec((None, LANES, LANES), lambda h: (h, 0, 0)),
                  pl.BlockSpec((None, LANES, LANES), lambda h: (h, 0, 0))],
        out_specs=[col(0), col(0)],
        out_shape=[jax.ShapeDtypeStruct((S, R), BF16), jax.ShapeDtypeStruct((S, R), F32)],
        scratch_shapes=[pltpu.VMEM((S + H8, LANES), F32)],
        compiler_params=_cparams("parallel"),
    )(proj, proj, par, wa, wx)


def _lru_bwd(name, proj, h, dgh, par, wa, wx):
    S = proj.shape[0]
    R = proj.shape[1] // 2
    H = R // LANES
    ch = _tile(S, SEQ_CHUNK)
    nch = S // ch
    H8 = SUBLANES
    nb = ch // H8

    def body(up_ref, y_ref, h_ref, dgh_ref, par_ref, wa_ref, wx_ref,
             dup_ref, dy_ref, dwa_ref, dwx_ref, dpar_ref, up_pad, h_pad, du_pad, vec_acc):
        _pad_copy(up_pad, up_ref, H8, 0)
        _pad_copy(h_pad, h_ref, H8, 0)
        du_pad[pl.ds(S, H8), :] = jnp.zeros((H8, LANES), F32)
        par = par_ref[...]
        cw, cb, ba, bx, lam = par[0:4], par[4:5], par[5:6], par[6:7], par[7:8]
        sp = _softplus(-lam)
        wa_m, wx_m = wa_ref[...], wx_ref[...]
        dwa_ref[...] = jnp.zeros_like(dwa_ref)
        dwx_ref[...] = jnp.zeros_like(dwx_ref)
        vec_acc[...] = jnp.zeros_like(vec_acc)
        nt_dims = (((1,), (1,)), ((), ()))
        tn_dims = (((0,), (0,)), ((), ()))

        def chunk(it, carry):
            lam_next, a_next = carry
            ci = nch - 1 - it
            r0 = pl.multiple_of(ci * ch, ch)
            xs = up_pad[pl.ds(r0, ch + H8), :]
            u = _conv_fwd(xs, cw, cb)
            row = _rows(u.shape)
            grow = row + r0
            ub, r, ig, a, mult = _lru_gates(u, wa_m, wx_m, ba, bx, sp, grow)
            hs = h_pad[pl.ds(r0, ch + H8), :]
            hcur = hs[H8:]
            hprev = pltpu.roll(hs, 1, 0)[H8:]
            y = y_ref[pl.ds(r0, ch), :]
            dgh = dgh_ref[pl.ds(r0, ch), :]
            gy, t = _gelu(y)
            dy_ref[pl.ds(r0, ch), :] = (dgh * hcur * _gelu_grad(y, t)).astype(dy_ref.dtype)
            dh = dgh * gy
            c = jnp.where(row == ch - 1, a_next, pltpu.roll(a, ch - 1, 0))
            ls = [None] * nb
            for j in range(nb - 1, -1, -1):
                cc, dd = _scan8_bwd(c[j * H8:(j + 1) * H8], dh[j * H8:(j + 1) * H8])
                lj = dd + cc * lam_next
                lam_next = lj[0:1]
                ls[j] = lj
            lmb = jnp.concatenate(ls, axis=0)
            da = lmb * hprev
            gu = ig * u
            dmult = lmb * gu
            dlog_a = da * a + jnp.where(grow == 0, 0.0, dmult * (-(a * a) / mult))
            dr = dlog_a * ((-LRU_C) * sp)
            drp = dr * r * (1.0 - r)
            dip = (lmb * mult * u) * ig * (1.0 - ig)
            drb, dib = drp.astype(BF16), dip.astype(BF16)
            du = (lmb * mult * ig
                  + lax.dot_general(drb, wa_m, nt_dims, preferred_element_type=F32)
                  + lax.dot_general(dib, wx_m, nt_dims, preferred_element_type=F32))
            du_pad[pl.ds(r0, ch), :] = du
            dwa_ref[...] += lax.dot_general(ub, drb, tn_dims, preferred_element_type=F32)
            dwx_ref[...] += lax.dot_general(ub, dib, tn_dims, preferred_element_type=F32)
            ssum = lambda v: jnp.sum(v, axis=0, keepdims=True)
            vec_acc[0:1, :] += ssum(drp)
            vec_acc[1:2, :] += ssum(dip)
            vec_acc[2:3, :] += ssum(dlog_a * ((-LRU_C) * r))
            return lam_next, a[0:1]

        zero = jnp.zeros((1, LANES), F32)
        lax.fori_loop(0, nch, chunk, (zero, zero))

        def conv_chunk(ci, acc):
            r0 = pl.multiple_of(ci * ch, ch)
            ds = du_pad[pl.ds(r0, ch + H8), :]
            xs = up_pad[pl.ds(r0, ch + H8), :]
            n = ch + H8
            du = ds[:ch]
            dup = cw[3:4] * du
            new = [acc[3] + jnp.sum(du * xs[H8:], axis=0, keepdims=True)]
            for k in (1, 2, 3):
                dup = dup + cw[3 - k:4 - k] * pltpu.roll(ds, n - k, 0)[:ch]
                new.append(acc[3 - k] + jnp.sum(du * pltpu.roll(xs, k, 0)[H8:], axis=0, keepdims=True))
            dup_ref[pl.ds(r0, ch), :] = dup.astype(dup_ref.dtype)
            return (new[3], new[2], new[1], new[0], acc[4] + jnp.sum(du, axis=0, keepdims=True))

        acc = lax.fori_loop(0, nch, conv_chunk, (zero,) * 5)
        dlam = vec_acc[2:3, :] * (-jax.nn.sigmoid(-lam))
        dpar_ref[...] = jnp.concatenate(list(acc) + [vec_acc[0:1, :], vec_acc[1:2, :], dlam], axis=0)

    col = lambda off: pl.BlockSpec((S, LANES), lambda h: (0, h + off))
    head = pl.BlockSpec((None, LANES, LANES), lambda h: (h, 0, 0))
    return pl.pallas_call(
        body,
        name=name,
        grid=(H,),
        in_specs=[col(0), col(H), col(0), col(0), pl.BlockSpec((8, LANES), lambda h: (0, h)), head, head],
        out_specs=[col(0), col(0), head, head, pl.BlockSpec((8, LANES), lambda h: (0, h))],
        out_shape=[jax.ShapeDtypeStruct((S, R), BF16), jax.ShapeDtypeStruct((S, R), BF16),
                   jax.ShapeDtypeStruct((H, LANES, LANES), F32), jax.ShapeDtypeStruct((H, LANES, LANES), F32),
                   jax.ShapeDtypeStruct((8, R), F32)],
        scratch_shapes=[pltpu.VMEM((S + H8, LANES), F32), pltpu.VMEM((S + H8, LANES), F32),
                        pltpu.VMEM((S + H8, LANES), F32), pltpu.VMEM((8, LANES), F32)],
        compiler_params=_cparams("parallel"),
    )(proj, proj, h, dgh, par, wa, wx)


def _window_sum(xs, g, up):
    n = xs.shape[0]
    s = xs
    for lvl, k in enumerate((1, 2, 4, 8)):
        sh = pltpu.roll(s, (n - k) if up else k, 0)
        s = s + jnp.where(g >= lvl, sh, 0.0)
    return s


def _pool_count(grow, g):
    return jnp.minimum(grow + 1, lax.shift_left(jnp.int32(2), g)).astype(F32)


def _pool_fwd(name, u, wgrp, par):
    S, D = u.shape
    G, W = wgrp.shape[0], wgrp.shape[1]
    ch = _tile(S, SEQ_CHUNK)
    nch = S // ch
    PH = POOL_HALO

    def body(u_ref, w_ref, par_ref, zs_ref, u_pad):
        g = pl.program_id(0)
        _pad_copy(u_pad, u_ref, PH, 0)
        par = par_ref[...]
        w = w_ref[...]

        def chunk(ci, _):
            r0 = pl.multiple_of(ci * ch, ch)
            xs = u_pad[pl.ds(r0, ch + PH), :]
            ws = _window_sum(xs, g, False)[PH:]
            uc = xs[PH:]
            cnt = _pool_count(_rows(uc.shape) + r0, g)
            pooled = ws / cnt - uc
            z = jnp.dot(pooled.astype(BF16), w, preferred_element_type=F32) + par[0:1]
            zs_ref[pl.ds(r0, ch), :] = (z * par[1:2]).astype(zs_ref.dtype)
            return 0

        lax.fori_loop(0, nch, chunk, 0)

    return pl.pallas_call(
        body,
        name=name,
        grid=(G,),
        in_specs=[pl.BlockSpec((S, W), lambda g: (0, g)), pl.BlockSpec((None, W, W), lambda g: (g, 0, 0)),
                  pl.BlockSpec((2, W), lambda g: (0, g))],
        out_specs=pl.BlockSpec((S, W), lambda g: (0, g)),
        out_shape=jax.ShapeDtypeStruct((S, D), BF16),
        scratch_shapes=[pltpu.VMEM((S + PH, W), F32)],
        compiler_params=_cparams("parallel"),
    )(u, wgrp, par)


def _pool_bwd(name, u, dzs, wgrp, par):
    S, D = u.shape
    G, W = wgrp.shape[0], wgrp.shape[1]
    ch = _tile(S, SEQ_CHUNK)
    nch = S // ch
    PH = POOL_HALO

    def body(u_ref, dzs_ref, w_ref, par_ref, du_ref, dw_ref, dpar_ref, u_pad, q_pad, dw_acc):
        g = pl.program_id(0)
        _pad_copy(u_pad, u_ref, PH, 0)
        q_pad[pl.ds(S, PH), :] = jnp.zeros((PH, W), F32)
        par = par_ref[...]
        w = w_ref[...]
        dw_acc[...] = jnp.zeros_like(dw_acc)

        def chunk(ci, acc):
            db, dsc = acc
            r0 = pl.multiple_of(ci * ch, ch)
            xs = u_pad[pl.ds(r0, ch + PH), :]
            ws = _window_sum(xs, g, False)[PH:]
            uc = xs[PH:]
            cnt = _pool_count(_rows(uc.shape) + r0, g)
            pooled = (ws / cnt - uc).astype(BF16)
            z = jnp.dot(pooled, w, preferred_element_type=F32) + par[0:1]
            dzs = dzs_ref[pl.ds(r0, ch), :]
            dz = dzs * par[1:2]
            dzb = dz.astype(BF16)
            dw_acc[...] += lax.dot_general(pooled, dzb, (((0,), (0,)), ((), ())), preferred_element_type=F32)
            dpooled = lax.dot_general(dzb, w, (((1,), (1,)), ((), ())), preferred_element_type=F32)
            q_pad[pl.ds(r0, ch), :] = dpooled / cnt
            return (db + jnp.sum(dz, axis=0, keepdims=True), dsc + jnp.sum(dzs * z, axis=0, keepdims=True))

        zero = jnp.zeros((1, W), F32)
        db, dsc = lax.fori_loop(0, nch, chunk, (zero, zero))
        dpar_ref[...] = jnp.concatenate([db, dsc], axis=0)
        dw_ref[...] = dw_acc[...].astype(dw_ref.dtype)

        def back(ci, _):
            r0 = pl.multiple_of(ci * ch, ch)
            qs = q_pad[pl.ds(r0, ch + PH), :]
            ws = _window_sum(qs, g, True)[:ch]
            qc = qs[:ch]
            cnt = _pool_count(_rows(qc.shape) + r0, g)
            du_ref[pl.ds(r0, ch), :] = (ws - qc * cnt).astype(du_ref.dtype)
            return 0

        lax.fori_loop(0, nch, back, 0)

    blk = pl.BlockSpec((S, W), lambda g: (0, g))
    wspec = pl.BlockSpec((None, W, W), lambda g: (g, 0, 0))
    pspec = pl.BlockSpec((2, W), lambda g: (0, g))
    return pl.pallas_call(
        body,
        name=name,
        grid=(G,),
        in_specs=[blk, blk, wspec, pspec],
        out_specs=[blk, wspec, pspec],
        out_shape=[jax.ShapeDtypeStruct((S, D), BF16), jax.ShapeDtypeStruct((G, W, W), BF16),
                   jax.ShapeDtypeStruct((2, D), F32)],
        scratch_shapes=[pltpu.VMEM((S + PH, W), F32), pltpu.VMEM((S + PH, W), F32), pltpu.VMEM((W, W), F32)],
        compiler_params=_cparams("parallel"),
    )(u, dzs, wgrp, par)


def _my_place():
    x, y, c = lax.axis_index("x"), lax.axis_index("y"), lax.axis_index("c")
    return x, y, c, 4 * x + 2 * y + c


def _peers(x, y, c):
    out = []
    for d in range(1, N_DEV):
        px = 1 - x if d & 4 else x
        py = 1 - y if d & 2 else y
        pc = 1 - c if d & 1 else c
        out.append(((px, py, pc), 4 * px + 2 * py + pc))
    return out


def _window(ref, axis, start, size):
    idx = [slice(None)] * len(ref.shape)
    idx[axis] = pl.ds(start, size)
    return ref.at[tuple(idx)]


def _to_bf16(name, arrs):
    outs = []
    for i, a in enumerate(arrs):
        a2 = a.reshape(-1, a.shape[-1])
        tr = _tile(a2.shape[0], 512)
        o = pl.pallas_call(
            lambda a_ref, o_ref: o_ref.__setitem__(Ellipsis, a_ref[...].astype(BF16)),
            name=f"{name}_{i}",
            grid=(a2.shape[0] // tr,),
            in_specs=[pl.BlockSpec((tr, a2.shape[1]), lambda r: (r, 0))],
            out_specs=pl.BlockSpec((tr, a2.shape[1]), lambda r: (r, 0)),
            out_shape=jax.ShapeDtypeStruct(a2.shape, BF16),
            compiler_params=_cparams("parallel"),
        )(a2)
        outs.append(o.reshape(a.shape))
    return outs


def _all_gather(name, shards, axes):
    n = len(shards)
    sizes = [s.shape[ax] for s, ax in zip(shards, axes)]

    def body(*refs):
        ins, outs = refs[:n], refs[n:2 * n]
        send, recv, loc = refs[2 * n:]
        x, y, c, me = _my_place()
        peers = _peers(x, y, c)
        local = []
        for i in range(n):
            dst = _window(outs[i], axes[i], me * sizes[i], sizes[i])
            cp = pltpu.make_async_copy(ins[i], dst, loc.at[i])
            cp.start()
            local.append(cp)
            for peer, _ in peers:
                pltpu.make_async_remote_copy(src_ref=ins[i], dst_ref=dst, send_sem=send.at[i], recv_sem=recv.at[i],
                                             device_id=peer, device_id_type=MESH).start()
        for i in range(n):
            local[i].wait()
            seven = _window(outs[i], axes[i], 0, (N_DEV - 1) * sizes[i])
            pltpu.make_async_remote_copy(src_ref=seven, dst_ref=seven, send_sem=send.at[i], recv_sem=recv.at[i],
                                         device_id=(x, y, c), device_id_type=MESH).wait()

    def full_shape(s, ax):
        shp = list(s.shape)
        shp[ax] *= N_DEV
        return jax.ShapeDtypeStruct(tuple(shp), s.dtype)

    any_spec = pl.BlockSpec(memory_space=pl.ANY)
    return pl.pallas_call(
        body,
        name=name,
        in_specs=[any_spec] * n,
        out_specs=[any_spec] * n,
        out_shape=[full_shape(s, ax) for s, ax in zip(shards, axes)],
        scratch_shapes=[pltpu.SemaphoreType.DMA((n,)), pltpu.SemaphoreType.DMA((n,)), pltpu.SemaphoreType.DMA((n,))],
        compiler_params=pltpu.CompilerParams(has_side_effects=True),
    )(*shards)


def _exchange(name, groups):
    n = len(groups)
    flat = [a for arrs, _ in groups for a in arrs]
    offs = []
    o = 0
    for arrs, _ in groups:
        offs.append(o)
        o += len(arrs)
    nf = len(flat)
    sizes = [arrs[0].shape[ax] // N_DEV for arrs, ax in groups]

    def body(*refs):
        ins, outs = refs[:nf], refs[nf:nf + n]
        send, recv, loc = refs[nf + n:]
        x, y, c, me = _my_place()
        peers = _peers(x, y, c)
        local = []
        for gi, (arrs, ax) in enumerate(groups):
            for li in range(len(arrs)):
                src_ref = ins[offs[gi] + li]
                k = offs[gi] + li
                cp = pltpu.make_async_copy(_window(src_ref, ax, me * sizes[gi], sizes[gi]), outs[gi].at[me, li],
                                           loc.at[k])
                cp.start()
                local.append(cp)
                for peer, pidx in peers:
                    pltpu.make_async_remote_copy(
                        src_ref=_window(src_ref, ax, pidx * sizes[gi], sizes[gi]), dst_ref=outs[gi].at[me, li],
                        send_sem=send.at[k], recv_sem=recv.at[k], device_id=peer, device_id_type=MESH).start()
        for gi, (arrs, ax) in enumerate(groups):
            for li in range(len(arrs)):
                k = offs[gi] + li
                local[k].wait()
                seven = outs[gi].at[pl.ds(0, N_DEV - 1), li]
                pltpu.make_async_remote_copy(src_ref=seven, dst_ref=seven, send_sem=send.at[k], recv_sem=recv.at[k],
                                             device_id=(x, y, c), device_id_type=MESH).wait()

    def buf_shape(arrs, ax):
        shp = list(arrs[0].shape)
        shp[ax] //= N_DEV
        return jax.ShapeDtypeStruct((N_DEV, len(arrs)) + tuple(shp), arrs[0].dtype)

    any_spec = pl.BlockSpec(memory_space=pl.ANY)
    return pl.pallas_call(
        body,
        name=name,
        in_specs=[any_spec] * nf,
        out_specs=[any_spec] * n,
        out_shape=[buf_shape(arrs, ax) for arrs, ax in groups],
        scratch_shapes=[pltpu.SemaphoreType.DMA((nf,)), pltpu.SemaphoreType.DMA((nf,)), pltpu.SemaphoreType.DMA((nf,))],
        compiler_params=pltpu.CompilerParams(has_side_effects=True),
    )(*flat)


def _adamw_math(w, g, m, v):
    m = ADAM_B1 * m + (1.0 - ADAM_B1) * g
    v = ADAM_B2 * v + (1.0 - ADAM_B2) * jnp.square(g)
    m_hat = m / (1.0 - ADAM_B1 ** ADAM_STEP)
    v_hat = v / (1.0 - ADAM_B2 ** ADAM_STEP)
    delta = -ADAM_LR * (m_hat / (jnp.sqrt(v_hat) + ADAM_EPS) + ADAM_WD * w)
    return delta, m, v


def _sum_slots(buf_ref):
    g = buf_ref[0].astype(F32)
    for s in range(1, N_DEV):
        g = g + buf_ref[s].astype(F32)
    return g


def _adamw_big(name, buf, w, m, v):
    shape = w.shape
    L, C = shape[0], shape[-1]
    Rr = math.prod(shape[1:-1])
    buf3 = buf.reshape(N_DEV, L, Rr, C)
    w3, m3, v3 = (t.reshape(L, Rr, C) for t in (w, m, v))
    tr = _tile(Rr, 128) if Rr % 128 == 0 else Rr

    def body(buf_ref, w_ref, m_ref, v_ref, g_out, d_out, m_out, v_out):
        g = _sum_slots(buf_ref)
        d, mm, vv = _adamw_math(w_ref[...], g, m_ref[...], v_ref[...])
        g_out[...] = g
        d_out[...] = d
        m_out[...] = mm
        v_out[...] = vv

    spec = pl.BlockSpec((None, tr, C), lambda l, r: (l, r, 0))
    outs = pl.pallas_call(
        body,
        name=name,
        grid=(L, Rr // tr),
        in_specs=[pl.BlockSpec((N_DEV, None, tr, C), lambda l, r: (0, l, r, 0)), spec, spec, spec],
        out_specs=[spec] * 4,
        out_shape=[jax.ShapeDtypeStruct((L, Rr, C), F32)] * 4,
        compiler_params=_cparams("parallel", "parallel"),
    )(buf3, w3, m3, v3)
    return [o.reshape(shape) for o in outs]


def _sum8(name, buf):
    R = buf.shape[1]

    def body(buf_ref, o_ref):
        o_ref[...] = _sum_slots(buf_ref)

    return pl.pallas_call(
        body,
        name=name,
        in_specs=[pl.BlockSpec(buf.shape, lambda: (0, 0, 0))],
        out_specs=pl.BlockSpec((R, LANES), lambda: (0, 0)),
        out_shape=jax.ShapeDtypeStruct((R, LANES), F32),
        compiler_params=_cparams(),
    )(buf)


def _adamw_packed(name, w, g, m, v):
    R = w.shape[0]
    tr = _row_tile(R, 512)

    def body(w_ref, g_ref, m_ref, v_ref, d_out, m_out, v_out):
        d, mm, vv = _adamw_math(w_ref[...], g_ref[...], m_ref[...], v_ref[...])
        d_out[...] = d
        m_out[...] = mm
        v_out[...] = vv

    spec = pl.BlockSpec((tr, LANES), lambda r: (r, 0))
    return pl.pallas_call(
        body,
        name=name,
        grid=(R // tr,),
        in_specs=[spec] * 4,
        out_specs=[spec] * 3,
        out_shape=[jax.ShapeDtypeStruct((R, LANES), F32)] * 3,
        compiler_params=_cparams("parallel"),
    )(w, g, m, v)


def _pack(arrs, pad_rows_to=SUBLANES):
    parts = []
    for a in arrs:
        flat = a.reshape(-1)
        per = LANES * pad_rows_to
        padded = -(-flat.shape[0] // per) * per
        if padded != flat.shape[0]:
            flat = jnp.pad(flat, (0, padded - flat.shape[0]))
        parts.append(flat.reshape(-1, LANES))
    return jnp.concatenate(parts, axis=0)


def _unpack(packed, shapes, pad_rows_to=SUBLANES):
    out = []
    r = 0
    for shp in shapes:
        nel = math.prod(shp)
        per = LANES * pad_rows_to
        rows = -(-nel // per) * pad_rows_to
        out.append(packed[r:r + rows].reshape(-1)[:nel].reshape(shp))
        r += rows
    return out


BIG = ("lru_w_in", "lru_w_out", "pool_w_in", "pool_w_grp", "pool_w_out", "mlp_w1", "mlp_w2", "ple_w", "ple_gate_w")
BIG_AXIS = {"lru_w_in": 2, "lru_w_out": 1, "pool_w_in": 1, "pool_w_grp": 2, "pool_w_out": 1, "mlp_w1": 2,
            "mlp_w2": 1, "ple_w": 2, "ple_gate_w": 1}
SMALL_SHARDED = ("lru_conv_w", "pool_b_grp", "pool_scale")
REPLICATED = ("lru_conv_b", "lru_wa", "lru_ba", "lru_wx", "lru_bx", "lru_lambda", "ln_mix_g", "ln_mix_b",
              "ln_mlp_g", "ln_mlp_b", "ple_gate_b")
WEIGHTS = ("lru_w_in", "lru_conv_w", "lru_conv_b", "lru_wa", "lru_ba", "lru_wx", "lru_bx", "lru_lambda", "lru_w_out",
           "pool_w_in", "pool_w_grp", "pool_b_grp", "pool_scale", "pool_w_out", "ln_mix_g", "ln_mix_b", "mlp_w1",
           "mlp_w2", "ln_mlp_g", "ln_mlp_b", "ple_w", "ple_gate_w", "ple_gate_b")
INPUTS = ("x", "p") + WEIGHTS + ("loss_target",) + tuple("m_" + n for n in WEIGHTS) + tuple("v_" + n for n in WEIGHTS)


def _gather_last_axis(packed_full, shard_shape):
    nel = math.prod(shard_shape)
    blocks = packed_full.reshape(N_DEV, -1)[:, :nel].reshape((N_DEV,) + tuple(shard_shape))
    return jnp.concatenate([blocks[d] for d in range(N_DEV)], axis=-1)


def kernel(x, p, lru_w_in, lru_conv_w, lru_conv_b, lru_wa, lru_ba, lru_wx, lru_bx, lru_lambda, lru_w_out, pool_w_in, pool_w_grp, pool_b_grp, pool_scale, pool_w_out, ln_mix_g, ln_mix_b, mlp_w1, mlp_w2, ln_mlp_g, ln_mlp_b, ple_w, ple_gate_w, ple_gate_b, loss_target, m_lru_w_in, m_lru_conv_w, m_lru_conv_b, m_lru_wa, m_lru_ba, m_lru_wx, m_lru_bx, m_lru_lambda, m_lru_w_out, m_pool_w_in, m_pool_w_grp, m_pool_b_grp, m_pool_scale, m_pool_w_out, m_ln_mix_g, m_ln_mix_b, m_mlp_w1, m_mlp_w2, m_ln_mlp_g, m_ln_mlp_b, m_ple_w, m_ple_gate_w, m_ple_gate_b, v_lru_w_in, v_lru_conv_w, v_lru_conv_b, v_lru_wa, v_lru_ba, v_lru_wx, v_lru_bx, v_lru_lambda, v_lru_w_out, v_pool_w_in, v_pool_w_grp, v_pool_b_grp, v_pool_scale, v_pool_w_out, v_ln_mix_g, v_ln_mix_b, v_mlp_w1, v_mlp_w2, v_ln_mlp_g, v_ln_mlp_b, v_ple_w, v_ple_gate_w, v_ple_gate_b):
    A = dict(zip(INPUTS, (x, p, lru_w_in, lru_conv_w, lru_conv_b, lru_wa, lru_ba, lru_wx, lru_bx, lru_lambda, lru_w_out, pool_w_in, pool_w_grp, pool_b_grp, pool_scale, pool_w_out, ln_mix_g, ln_mix_b, mlp_w1, mlp_w2, ln_mlp_g, ln_mlp_b, ple_w, ple_gate_w, ple_gate_b, loss_target, m_lru_w_in, m_lru_conv_w, m_lru_conv_b, m_lru_wa, m_lru_ba, m_lru_wx, m_lru_bx, m_lru_lambda, m_lru_w_out, m_pool_w_in, m_pool_w_grp, m_pool_b_grp, m_pool_scale, m_pool_w_out, m_ln_mix_g, m_ln_mix_b, m_mlp_w1, m_mlp_w2, m_ln_mlp_g, m_ln_mlp_b, m_ple_w, m_ple_gate_w, m_ple_gate_b, v_lru_w_in, v_lru_conv_w, v_lru_conv_b, v_lru_wa, v_lru_ba, v_lru_wx, v_lru_bx, v_lru_lambda, v_lru_w_out, v_pool_w_in, v_pool_w_grp, v_pool_b_grp, v_pool_scale, v_pool_w_out, v_ln_mix_g, v_ln_mix_b, v_mlp_w1, v_mlp_w2, v_ln_mlp_g, v_ln_mlp_b, v_ple_w, v_ple_gate_w, v_ple_gate_b)))
    depth = ln_mix_g.shape[0]
    alpha = (2 * depth) ** 0.25
    S, D = x.shape[1], x.shape[2]
    xs = x.reshape(S, D)
    tgt = loss_target.reshape(S, D)
    p3 = p.reshape(depth, S, p.shape[-1])
    me = 4 * lax.axis_index("x") + 2 * lax.axis_index("y") + lax.axis_index("c")

    small_shard_shapes = [A[n].shape for n in SMALL_SHARDED]
    small_packed = _pack([A[n] for n in SMALL_SHARDED])
    gathered = _all_gather("gather_weights", _to_bf16("cast_w", [A[n] for n in BIG]) + [small_packed],
                           [BIG_AXIS[n] for n in BIG] + [0])
    W = dict(zip(BIG, gathered[:-1]))
    small_full = gathered[-1].reshape(N_DEV, -1, LANES)
    r = 0
    for n, shp in zip(SMALL_SHARDED, small_shard_shapes):
        rows = -(-math.prod(shp) // (LANES * SUBLANES)) * SUBLANES
        W[n] = _gather_last_axis(small_full[:, r:r + rows], shp)
        r += rows
    wa_b, wx_b = _to_bf16("cast_gates", [lru_wa, lru_wx])
    n_lru = lru_w_in.shape[0]
    lru_par = [jnp.concatenate([W["lru_conv_w"][s], lru_conv_b[s][None], lru_ba[s][None], lru_bx[s][None],
                                lru_lambda[s][None]], axis=0) for s in range(n_lru)]
    pool_par = [jnp.stack([W["pool_b_grp"][s], W["pool_scale"][s]], axis=0) for s in range(pool_w_in.shape[0])]

    saved = []
    h_in = xs
    for i in range(depth):
        s = i // 2
        sv = {"x0": h_in}
        if i % 2 == 0:
            sv["proj"] = _mm(f"l{i}_lru_in", h_in, W["lru_w_in"], "nn", [F32], b_lead=s)
            sv["gh"], sv["h"] = _lru_fwd(f"l{i}_lru_core", sv["proj"], lru_par[s], wa_b[s], wx_b[s])
            sv["mix"] = _mm(f"l{i}_lru_out", sv["gh"], W["lru_w_out"], "nn", [F32], b_lead=s)
        else:
            sv["u"] = _mm(f"l{i}_pool_in", h_in, W["pool_w_in"], "nn", [F32], b_lead=s)
            sv["zs"] = _pool_fwd(f"l{i}_pool_core", sv["u"], W["pool_w_grp"][s], pool_par[s])
            sv["mix"] = _mm(f"l{i}_pool_out", sv["zs"], W["pool_w_out"], "nn", [F32], b_lead=s)
        sv["x1"] = _ln_fwd(f"l{i}_ln_mix", alpha, h_in, sv["mix"], ln_mix_g[i][None], ln_mix_b[i][None])
        sv["hpre"], sv["hact"] = _mm(f"l{i}_mlp_up", sv["x1"], W["mlp_w1"], "nn", [BF16, BF16], b_lead=i,
                                     epi=lambda acc: (acc, jnp.square(jnp.maximum(acc, 0.0))))
        sv["mo"] = _mm(f"l{i}_mlp_down", sv["hact"], W["mlp_w2"], "nn", [F32], b_lead=i)
        sv["x2"] = _ln_fwd(f"l{i}_ln_mlp", alpha, sv["x1"], sv["mo"], ln_mlp_g[i][None], ln_mlp_b[i][None])
        sv["pp"] = _mm(f"l{i}_ple_up", p3, W["ple_w"], "nn", [F32], a_lead=i, b_lead=i)

        def ple_epi(acc, bg, x2t, ppt):
            gpre = acc + bg
            return x2t + ppt * jax.nn.sigmoid(gpre), gpre

        h_in, sv["gpre"] = _mm(f"l{i}_ple_gate", sv["x2"], W["ple_gate_w"], "nn", [F32, F32], b_lead=i,
                               epi=ple_epi, extras=[ple_gate_b[i][None], sv["x2"], sv["pp"]])
        saved.append(sv)

    dx, sq = _loss_and_grad("loss", h_in, tgt)
    loss = lax.psum(0.5 * sq[0, 0] / D, ("x", "y", "c"))

    dW = {n: [None] * A[n].shape[0] for n in BIG}
    dsmall = {n: [None] * A[n].shape[0] for n in REPLICATED + SMALL_SHARDED}
    for i in reversed(range(depth)):
        s = i // 2
        sv = saved[i]
        dpp, dgpre, dbg = _ple_bwd(f"l{i}_ple_bwd", dx, sv["gpre"], sv["pp"])
        dsmall["ple_gate_b"][i] = dbg[0]
        dW["ple_w"][i] = _mm(f"l{i}_d_ple_w", p3, dpp, "tn", [BF16], a_lead=i)
        dW["ple_gate_w"][i] = _mm(f"l{i}_d_ple_gate_w", sv["x2"], dgpre, "tn", [BF16])
        dx2 = _mm(f"l{i}_d_x2", dgpre, W["ple_gate_w"], "nt", [F32], b_lead=i, extras=[dx],
                  epi=lambda acc, d: (acc + d,))
        dz2, dg, db = _ln_bwd(f"l{i}_ln_mlp_bwd", alpha, dx2, sv["x1"], sv["mo"], ln_mlp_g[i][None])
        dsmall["ln_mlp_g"][i], dsmall["ln_mlp_b"][i] = dg[0], db[0]
        dhpre = _mm(f"l{i}_d_hpre", dz2, W["mlp_w2"], "nt", [BF16], b_lead=i, extras=[sv["hpre"]],
                    epi=lambda acc, hp: (acc * (2.0 * jnp.maximum(hp.astype(F32), 0.0)),))
        dW["mlp_w2"][i] = _mm(f"l{i}_d_mlp_w2", sv["hact"], dz2, "tn", [BF16])
        dW["mlp_w1"][i] = _mm(f"l{i}_d_mlp_w1", sv["x1"], dhpre, "tn", [BF16])
        dx1 = _mm(f"l{i}_d_x1", dhpre, W["mlp_w1"], "nt", [F32], b_lead=i, extras=[dz2],
                  epi=lambda acc, d: (acc + alpha * d,))
        dz1, dg, db = _ln_bwd(f"l{i}_ln_mix_bwd", alpha, dx1, sv["x0"], sv["mix"], ln_mix_g[i][None])
        dsmall["ln_mix_g"][i], dsmall["ln_mix_b"][i] = dg[0], db[0]
        if i % 2 == 0:
            dW["lru_w_out"][s] = _mm(f"l{i}_d_lru_w_out", sv["gh"], dz1, "tn", [BF16])
            dgh = _mm(f"l{i}_d_gh", dz1, W["lru_w_out"], "nt", [F32], b_lead=s)
            dup, dy, dwa, dwx, dpar = _lru_bwd(f"l{i}_lru_core_bwd", sv["proj"], sv["h"], dgh, lru_par[s],
                                               wa_b[s], wx_b[s])
            dsmall["lru_wa"][s], dsmall["lru_wx"][s] = dwa, dwx
            dsmall["lru_conv_w"][s] = dpar[0:4]
            for k, n in enumerate(("lru_conv_b", "lru_ba", "lru_bx", "lru_lambda")):
                dsmall[n][s] = dpar[4 + k]
            dmix_in = jnp.concatenate([dup, dy], axis=1)
            win = "lru_w_in"
        else:
            dW["pool_w_out"][s] = _mm(f"l{i}_d_pool_w_out", sv["zs"], dz1, "tn", [BF16])
            dzs = _mm(f"l{i}_d_zs", dz1, W["pool_w_out"], "nt", [F32], b_lead=s)
            dmix_in, dW["pool_w_grp"][s], dpar = _pool_bwd(f"l{i}_pool_core_bwd", sv["u"], dzs, W["pool_w_grp"][s],
                                                          pool_par[s])
            dsmall["pool_b_grp"][s], dsmall["pool_scale"][s] = dpar[0], dpar[1]
            win = "pool_w_in"
        dW[win][s] = _mm(f"l{i}_d_{win}", sv["x0"], dmix_in, "tn", [BF16])
        dx = _mm(f"l{i}_d_x0", dmix_in, W[win], "nt", [F32], b_lead=s, extras=[dz1],
                 epi=lambda acc, d: (acc + alpha * d,))
    grad_x = dx.reshape(x.shape)

    small_names = REPLICATED + SMALL_SHARDED
    small_grads = [jnp.stack(dsmall[n]) for n in small_names]
    small_shapes = [g.shape for g in small_grads]
    packed_g = _pack(small_grads)
    assert packed_g.shape[0] % (N_DEV * SUBLANES) == 0, packed_g.shape
    groups = [(dW[n], BIG_AXIS[n] - 1) for n in BIG] + [([packed_g], 0)]
    bufs = _exchange("exchange_grads", groups)

    outs = {}
    for n, buf in zip(BIG, bufs[:-1]):
        outs[n] = _adamw_big(f"adamw_{n}", buf, A[n], A["m_" + n], A["v_" + n])
    red = _sum8("sum_small", bufs[-1][:, 0])
    red_full = _all_gather("gather_small", [red], [0])[0]
    small_g = dict(zip(small_names, _unpack(red_full, small_shapes)))
    for n in SMALL_SHARDED:
        width = A[n].shape[-1]
        small_g[n] = lax.dynamic_slice_in_dim(small_g[n], me * width, width, axis=small_g[n].ndim - 1)
    pk = lambda pre: _pack([A[pre + n] for n in small_names])
    d_p, m_p, v_p = _adamw_packed("adamw_small", pk(""), _pack([small_g[n] for n in small_names]), pk("m_"), pk("v_"))
    shapes = [A[n].shape for n in small_names]
    for n, d_, m_, v_ in zip(small_names, _unpack(d_p, shapes), _unpack(m_p, shapes), _unpack(v_p, shapes)):
        outs[n] = [small_g[n], d_, m_, v_]

    return (loss, grad_x, *[outs[n][0] for n in WEIGHTS], *[outs[n][1] for n in WEIGHTS],
            *[outs[n][2] for n in WEIGHTS], *[outs[n][3] for n in WEIGHTS])
```

```python
import functools
import math

import jax
import jax.numpy as jnp
from jax import lax
from jax.experimental import pallas as pl
from jax.experimental.pallas import tpu as pltpu

F32 = jnp.float32
BF16 = jnp.bfloat16
MESH = pl.DeviceIdType.MESH
N_DEV = 8
LANES = 128
SUBLANES = 8

LN_EPS = 1e-5
LRU_C = 8.0
CONV_WIDTH = 4
POOL_HALO = 16
ADAM_LR = 0.001
ADAM_B1 = 0.9
ADAM_B2 = 0.999
ADAM_EPS = 1e-08
ADAM_WD = 0.01
ADAM_STEP = 10

VMEM_LIMIT = 48 * 1024 * 1024
SEQ_CHUNK = 256
MM_TK = 2048
MM_TK_TOKENS = 1024
GELU_C0 = math.sqrt(2.0 / math.pi)
GELU_C1 = 0.044715


def _cparams(*sem):
    return pltpu.CompilerParams(dimension_semantics=tuple(sem) if sem else None, vmem_limit_bytes=VMEM_LIMIT)


def _tile(n, pref):
    if n <= pref:
        return n
    t = pref - pref % LANES
    while t >= LANES:
        if n % t == 0:
            return t
        t -= LANES
    return n


def _row_tile(n, pref):
    if n <= pref:
        return n
    t = pref - pref % SUBLANES
    while t >= SUBLANES:
        if n % t == 0:
            return t
        t -= SUBLANES
    return n


def _mm(name, a, b, mode, out_dtypes, epi=None, extras=(), a_lead=None, b_lead=None, tm=1024, tn=512, tk=None,
        after=()):
    if isinstance(b, (list, tuple)):
        b, b_lead = b[b_lead], None
    a2 = a.shape[-2:]
    b2 = b.shape[-2:]
    if mode == "nn":
        (M, K), N = a2, b2[1]
        assert b2[0] == K
    elif mode == "nt":
        (M, K), N = a2, b2[0]
        assert b2[1] == K
    else:
        (K, M), N = a2, b2[1]
        assert b2[0] == K
    if tk is None:
        tk = MM_TK_TOKENS if mode == "tn" else MM_TK
    tm, tn, tk = _tile(M, tm), _tile(N, tn), _tile(K, tk)
    nk = K // tk
    n_extra = len(extras)
    n_out = len(out_dtypes)

    def lead(shape, idx, which):
        if which is None:
            return pl.BlockSpec(shape, idx)
        return pl.BlockSpec((None,) + shape, lambda i, j, k: (which,) + idx(i, j, k))

    if mode == "nn":
        a_spec = lead((tm, tk), lambda i, j, k: (i, k), a_lead)
        b_spec = lead((tk, tn), lambda i, j, k: (k, j), b_lead)
        dims = (((1,), (0,)), ((), ()))
    elif mode == "nt":
        a_spec = lead((tm, tk), lambda i, j, k: (i, k), a_lead)
        b_spec = lead((tn, tk), lambda i, j, k: (j, k), b_lead)
        dims = (((1,), (1,)), ((), ()))
    else:
        a_spec = lead((tk, tm), lambda i, j, k: (k, i), a_lead)
        b_spec = lead((tk, tn), lambda i, j, k: (k, j), b_lead)
        dims = (((0,), (0,)), ((), ()))
    e_specs = []
    for e in extras:
        if e.shape[0] == 1:
            e_specs.append(pl.BlockSpec((1, tn), lambda i, j, k: (0, j)))
        else:
            e_specs.append(pl.BlockSpec((tm, tn), lambda i, j, k: (i, j)))

    n_after = len(after)

    def body(a_ref, b_ref, *rest):
        e_refs = rest[:n_extra]
        rest = rest[:n_extra] + rest[n_extra + n_after:]
        o_refs = rest[n_extra:n_extra + n_out]
        part = lax.dot_general(a_ref[...].astype(BF16), b_ref[...].astype(BF16), dims, preferred_element_type=F32)

        def finish(r):
            res = (r,) if epi is None else epi(r, *[e[...] for e in e_refs])
            for o, v in zip(o_refs, res):
                o[...] = v.astype(o.dtype)

        if nk == 1:
            finish(part)
            return
        acc = rest[n_extra + n_out]
        k = pl.program_id(2)

        @pl.when(k == 0)
        def _():
            acc[...] = part

        @pl.when(jnp.logical_and(k > 0, k < nk - 1))
        def _():
            acc[...] += part

        @pl.when(k == nk - 1)
        def _():
            finish(acc[...] + part)

    outs = pl.pallas_call(
        body,
        name=name,
        grid=(M // tm, N // tn, nk),
        in_specs=[a_spec, b_spec] + e_specs + [pl.BlockSpec(memory_space=pl.ANY)] * n_after,
        out_specs=[pl.BlockSpec((tm, tn), lambda i, j, k: (i, j)) for _ in out_dtypes],
        out_shape=[jax.ShapeDtypeStruct((M, N), d) for d in out_dtypes],
        scratch_shapes=[pltpu.VMEM((tm, tn), F32)] if nk > 1 else [],
        compiler_params=_cparams("parallel", "parallel", "arbitrary"),
    )(a, b, *extras, *after)
    return outs[0] if n_out == 1 else tuple(outs)


def _rowwise(name, fn, tiled, params, outs, accs=(), tm=256, after=()):
    S = tiled[0].shape[0]
    tm = _tile(S, tm)
    nt, npar, no = len(tiled), len(params), len(outs)
    n_after = len(after)

    def body(*refs):
        t_refs = refs[:nt]
        p_refs = refs[nt:nt + npar]
        refs = refs[nt + npar + n_after:]
        o_refs = refs[:no]
        a_refs = refs[no:]
        res = fn(*[r[...] for r in t_refs], *[r[...] for r in p_refs])
        for o, v in zip(o_refs, res[:no]):
            o[...] = v.astype(o.dtype)
        first = pl.program_id(0) == 0
        for ar, v in zip(a_refs, res[no:]):
            @pl.when(first)
            def _(ar=ar, v=v):
                ar[...] = v

            @pl.when(jnp.logical_not(first))
            def _(ar=ar, v=v):
                ar[...] += v

    full = lambda p: pl.BlockSpec(p.shape, lambda i, nd=p.ndim: (0,) * nd)
    res = pl.pallas_call(
        body,
        name=name,
        grid=(S // tm,),
        in_specs=[pl.BlockSpec((tm, t.shape[1]), lambda i: (i, 0)) for t in tiled] + [full(p) for p in params]
        + [pl.BlockSpec(memory_space=pl.ANY)] * n_after,
        out_specs=[pl.BlockSpec((tm, c), lambda i: (i, 0)) for c, _ in outs]
        + [pl.BlockSpec(s, lambda i, nd=len(s): (0,) * nd) for s in accs],
        out_shape=[jax.ShapeDtypeStruct((S, c), d) for c, d in outs] + [jax.ShapeDtypeStruct(s, F32) for s in accs],
        compiler_params=_cparams("arbitrary"),
    )(*tiled, *params, *after)
    return res


def _ln_stats(z):
    mu = jnp.mean(z, axis=-1, keepdims=True)
    zc = z - mu
    var = jnp.mean(zc * zc, axis=-1, keepdims=True)
    return zc, lax.rsqrt(var + LN_EPS)


def _ln_fwd(name, alpha, xp, m, g, b):
    def fn(xp, m, g, b):
        zc, rstd = _ln_stats(alpha * xp + m)
        return (zc * rstd * g + b,)

    return _rowwise(name, fn, [xp, m], [g, b], [(xp.shape[1], F32)])[0]


def _ln_bwd(name, alpha, dy, xp, m, g):
    def fn(dy, xp, m, g):
        zc, rstd = _ln_stats(alpha * xp + m)
        xhat = zc * rstd
        dxh = dy * g
        m1 = jnp.mean(dxh, axis=-1, keepdims=True)
        m2 = jnp.mean(dxh * xhat, axis=-1, keepdims=True)
        dz = rstd * (dxh - m1 - xhat * m2)
        return dz, jnp.sum(dy * xhat, axis=0, keepdims=True), jnp.sum(dy, axis=0, keepdims=True)

    d = xp.shape[1]
    return _rowwise(name, fn, [dy, xp, m], [g], [(d, F32)], accs=[(1, d), (1, d)])


def _loss_and_grad(name, y, target):
    d = y.shape[1]

    def fn(y, t):
        err = y - t
        sq = jnp.sum(jnp.sum(err * err, axis=0, keepdims=True), axis=1, keepdims=True)
        return err * (1.0 / d), jnp.broadcast_to(sq, (1, LANES))

    return _rowwise(name, fn, [y, target], [], [(d, F32)], accs=[(1, LANES)])


def _ple_bwd(name, dx3, gpre, pp, after=()):
    def fn(dx3, gpre, pp):
        gate = jax.nn.sigmoid(gpre)
        dgpre = dx3 * pp * gate * (1.0 - gate)
        return dx3 * gate, dgpre, jnp.sum(dgpre, axis=0, keepdims=True)

    d = dx3.shape[1]
    return _rowwise(name, fn, [dx3, gpre, pp], [], [(d, BF16), (d, BF16)], accs=[(1, d)], after=after)


def _rows(shape):
    return lax.broadcasted_iota(jnp.int32, shape, 0)


def _gelu(y):
    t = jnp.tanh(GELU_C0 * (y + GELU_C1 * y * y * y))
    return 0.5 * y * (1.0 + t), t


def _gelu_grad(y, t):
    return 0.5 * (1.0 + t) + 0.5 * y * (1.0 - t * t) * GELU_C0 * (1.0 + 3.0 * GELU_C1 * y * y)


def _neg_expm1(x):
    series = -x * (1.0 + x * (0.5 + x * (1.0 / 6.0 + x * (1.0 / 24.0))))
    return jnp.where(x > -0.02, series, 1.0 - jnp.exp(x))


def _softplus(x):
    return jnp.maximum(x, 0.0) + jnp.log(1.0 + jnp.exp(-jnp.abs(x)))


def _conv_fwd(xs, cw, cb):
    n = xs.shape[0]
    u = cw[3:4] * xs
    for k in (1, 2, 3):
        u = u + cw[3 - k:4 - k] * pltpu.roll(xs, k, 0)
    del n
    return u[SUBLANES:] + cb


def _lru_gates(u, wa, wx, ba, bx, sp, grow):
    ub = u.astype(BF16)
    r = jax.nn.sigmoid(jnp.dot(ub, wa, preferred_element_type=F32) + ba)
    ig = jax.nn.sigmoid(jnp.dot(ub, wx, preferred_element_type=F32) + bx)
    log_a = (-LRU_C) * r * sp
    a = jnp.exp(log_a)
    mult = jnp.sqrt(_neg_expm1(2.0 * log_a))
    mult = jnp.where(grow == 0, 1.0, mult)
    return ub, r, ig, a, mult


def _scan8_fwd(a, b):
    row = _rows(a.shape)
    for k in (1, 2, 4):
        m = row >= k
        b = jnp.where(m, a * pltpu.roll(b, k, 0) + b, b)
        a = jnp.where(m, a * pltpu.roll(a, k, 0), a)
    return a, b


def _scan8_bwd(c, d):
    row = _rows(c.shape)
    for k in (1, 2, 4):
        m = row < SUBLANES - k
        d = jnp.where(m, c * pltpu.roll(d, SUBLANES - k, 0) + d, d)
        c = jnp.where(m, c * pltpu.roll(c, SUBLANES - k, 0), c)
    return c, d


def _pad_copy(dst, src, front, back):
    s, c = src.shape
    if front:
        dst[pl.ds(0, front), :] = jnp.zeros((front, c), dst.dtype)
    if back:
        dst[pl.ds(front + s, back), :] = jnp.zeros((back, c), dst.dtype)
    dst[pl.ds(front, s), :] = src[...].astype(dst.dtype)


def _lru_fwd(name, proj, par, wa, wx):
    S = proj.shape[0]
    R = proj.shape[1] // 2
    H = R // LANES
    ch = _tile(S, SEQ_CHUNK)
    nch = S // ch
    H8 = SUBLANES

    def body(up_ref, y_ref, par_ref, wa_ref, wx_ref, gh_ref, h_ref, up_pad):
        _pad_copy(up_pad, up_ref, H8, 0)
        par = par_ref[...]
        cw, cb, ba, bx = par[0:4], par[4:5], par[5:6], par[6:7]
        sp = _softplus(-par[7:8])
        wa_m, wx_m = wa_ref[...], wx_ref[...]

        def chunk(ci, carry):
            r0 = pl.multiple_of(ci * ch, ch)
            xs = up_pad[pl.ds(r0, ch + H8), :]
            u = _conv_fwd(xs, cw, cb)
            grow = _rows(u.shape) + r0
            _, _, ig, a, mult = _lru_gates(u, wa_m, wx_m, ba, bx, sp, grow)
            bt = mult * (ig * u)
            hs = []
            for j in range(ch // H8):
                aa, bb = _scan8_fwd(a[j * H8:(j + 1) * H8], bt[j * H8:(j + 1) * H8])
                hj = bb + aa * carry
                carry = hj[H8 - 1:H8]
                hs.append(hj)
            h = jnp.concatenate(hs, axis=0)
            h_ref[pl.ds(r0, ch), :] = h
            gy, _ = _gelu(y_ref[pl.ds(r0, ch), :])
            gh_ref[pl.ds(r0, ch), :] = (h * gy).astype(gh_ref.dtype)
            return carry

        lax.fori_loop(0, nch, chunk, jnp.zeros((1, LANES), F32))

    col = lambda off: pl.BlockSpec((S, LANES), lambda h: (0, h + off))
    return pl.pallas_call(
        body,
        name=name,
        grid=(H,),
        in_specs=[col(0), col(H), pl.BlockSpec((8, LANES), lambda h: (0, h)),
                  pl.BlockSpec((None, LANES, LANES), lambda h: (h, 0, 0)),
                  pl.BlockSpec((None, LANES, LANES), lambda h: (h, 0, 0))],
        out_specs=[col(0), col(0)],
        out_shape=[jax.ShapeDtypeStruct((S, R), BF16), jax.ShapeDtypeStruct((S, R), F32)],
        scratch_shapes=[pltpu.VMEM((S + H8, LANES), F32)],
        compiler_params=_cparams("parallel"),
    )(proj, proj, par, wa, wx)


def _lru_bwd(name, proj, h, dgh, par, wa, wx):
    S = proj.shape[0]
    R = proj.shape[1] // 2
    H = R // LANES
    ch = _tile(S, SEQ_CHUNK)
    nch = S // ch
    H8 = SUBLANES
    nb = ch // H8

    def body(up_ref, y_ref, h_ref, dgh_ref, par_ref, wa_ref, wx_ref,
             dup_ref, dy_ref, dwa_ref, dwx_ref, dpar_ref, up_pad, h_pad, du_pad, vec_acc):
        _pad_copy(up_pad, up_ref, H8, 0)
        _pad_copy(h_pad, h_ref, H8, 0)
        du_pad[pl.ds(S, H8), :] = jnp.zeros((H8, LANES), F32)
        par = par_ref[...]
        cw, cb, ba, bx, lam = par[0:4], par[4:5], par[5:6], par[6:7], par[7:8]
        sp = _softplus(-lam)
        wa_m, wx_m = wa_ref[...], wx_ref[...]
        dwa_ref[...] = jnp.zeros_like(dwa_ref)
        dwx_ref[...] = jnp.zeros_like(dwx_ref)
        vec_acc[...] = jnp.zeros_like(vec_acc)
        nt_dims = (((1,), (1,)), ((), ()))
        tn_dims = (((0,), (0,)), ((), ()))

        def chunk(it, carry):
            lam_next, a_next = carry
            ci = nch - 1 - it
            r0 = pl.multiple_of(ci * ch, ch)
            xs = up_pad[pl.ds(r0, ch + H8), :]
            u = _conv_fwd(xs, cw, cb)
            row = _rows(u.shape)
            grow = row + r0
            ub, r, ig, a, mult = _lru_gates(u, wa_m, wx_m, ba, bx, sp, grow)
            hs = h_pad[pl.ds(r0, ch + H8), :]
            hcur = hs[H8:]
            hprev = pltpu.roll(hs, 1, 0)[H8:]
            y = y_ref[pl.ds(r0, ch), :]
            dgh = dgh_ref[pl.ds(r0, ch), :]
            gy, t = _gelu(y)
            dy_ref[pl.ds(r0, ch), :] = (dgh * hcur * _gelu_grad(y, t)).astype(dy_ref.dtype)
            dh = dgh * gy
            c = jnp.where(row == ch - 1, a_next, pltpu.roll(a, ch - 1, 0))
            ls = [None] * nb
            for j in range(nb - 1, -1, -1):
                cc, dd = _scan8_bwd(c[j * H8:(j + 1) * H8], dh[j * H8:(j + 1) * H8])
                lj = dd + cc * lam_next
                lam_next = lj[0:1]
                ls[j] = lj
            lmb = jnp.concatenate(ls, axis=0)
            da = lmb * hprev
            gu = ig * u
            dmult = lmb * gu
            dlog_a = da * a + jnp.where(grow == 0, 0.0, dmult * (-(a * a) / mult))
            dr = dlog_a * ((-LRU_C) * sp)
            drp = dr * r * (1.0 - r)
            dip = (lmb * mult * u) * ig * (1.0 - ig)
            drb, dib = drp.astype(BF16), dip.astype(BF16)
            du = (lmb * mult * ig
                  + lax.dot_general(drb, wa_m, nt_dims, preferred_element_type=F32)
                  + lax.dot_general(dib, wx_m, nt_dims, preferred_element_type=F32))
            du_pad[pl.ds(r0, ch), :] = du
            dwa_ref[...] += lax.dot_general(ub, drb, tn_dims, preferred_element_type=F32)
            dwx_ref[...] += lax.dot_general(ub, dib, tn_dims, preferred_element_type=F32)
            ssum = lambda v: jnp.sum(v, axis=0, keepdims=True)
            vec_acc[0:1, :] += ssum(drp)
            vec_acc[1:2, :] += ssum(dip)
            vec_acc[2:3, :] += ssum(dlog_a * ((-LRU_C) * r))
            return lam_next, a[0:1]

        zero = jnp.zeros((1, LANES), F32)
        lax.fori_loop(0, nch, chunk, (zero, zero))

        def conv_chunk(ci, acc):
            r0 = pl.multiple_of(ci * ch, ch)
            ds = du_pad[pl.ds(r0, ch + H8), :]
            xs = up_pad[pl.ds(r0, ch + H8), :]
            n = ch + H8
            du = ds[:ch]
            dup = cw[3:4] * du
            new = [acc[3] + jnp.sum(du * xs[H8:], axis=0, keepdims=True)]
            for k in (1, 2, 3):
                dup = dup + cw[3 - k:4 - k] * pltpu.roll(ds, n - k, 0)[:ch]
                new.append(acc[3 - k] + jnp.sum(du * pltpu.roll(xs, k, 0)[H8:], axis=0, keepdims=True))
            dup_ref[pl.ds(r0, ch), :] = dup.astype(dup_ref.dtype)
            return (new[3], new[2], new[1], new[0], acc[4] + jnp.sum(du, axis=0, keepdims=True))

        acc = lax.fori_loop(0, nch, conv_chunk, (zero,) * 5)
        dlam = vec_acc[2:3, :] * (-jax.nn.sigmoid(-lam))
        dpar_ref[...] = jnp.concatenate(list(acc) + [vec_acc[0:1, :], vec_acc[1:2, :], dlam], axis=0)

    col = lambda off: pl.BlockSpec((S, LANES), lambda h: (0, h + off))
    head = pl.BlockSpec((None, LANES, LANES), lambda h: (h, 0, 0))
    return pl.pallas_call(
        body,
        name=name,
        grid=(H,),
        in_specs=[col(0), col(H), col(0), col(0), pl.BlockSpec((8, LANES), lambda h: (0, h)), head, head],
        out_specs=[col(0), col(0), head, head, pl.BlockSpec((8, LANES), lambda h: (0, h))],
        out_shape=[jax.ShapeDtypeStruct((S, R), BF16), jax.ShapeDtypeStruct((S, R), BF16),
                   jax.ShapeDtypeStruct((H, LANES, LANES), F32), jax.ShapeDtypeStruct((H, LANES, LANES), F32),
                   jax.ShapeDtypeStruct((8, R), F32)],
        scratch_shapes=[pltpu.VMEM((S + H8, LANES), F32), pltpu.VMEM((S + H8, LANES), F32),
                        pltpu.VMEM((S + H8, LANES), F32), pltpu.VMEM((8, LANES), F32)],
        compiler_params=_cparams("parallel"),
    )(proj, proj, h, dgh, par, wa, wx)


def _window_sum(xs, g, up):
    n = xs.shape[0]
    s = xs
    for lvl, k in enumerate((1, 2, 4, 8)):
        sh = pltpu.roll(s, (n - k) if up else k, 0)
        s = s + jnp.where(g >= lvl, sh, 0.0)
    return s


def _pool_count(grow, g):
    return jnp.minimum(grow + 1, lax.shift_left(jnp.int32(2), g)).astype(F32)


def _pool_fwd(name, u, wgrp, par):
    S, D = u.shape
    G, W = wgrp.shape[0], wgrp.shape[1]
    ch = _tile(S, SEQ_CHUNK)
    nch = S // ch
    PH = POOL_HALO

    def body(u_ref, w_ref, par_ref, zs_ref, u_pad):
        g = pl.program_id(0)
        _pad_copy(u_pad, u_ref, PH, 0)
        par = par_ref[...]
        w = w_ref[...]

        def chunk(ci, _):
            r0 = pl.multiple_of(ci * ch, ch)
            xs = u_pad[pl.ds(r0, ch + PH), :]
            ws = _window_sum(xs, g, False)[PH:]
            uc = xs[PH:]
            cnt = _pool_count(_rows(uc.shape) + r0, g)
            pooled = ws / cnt - uc
            z = jnp.dot(pooled.astype(BF16), w, preferred_element_type=F32) + par[0:1]
            zs_ref[pl.ds(r0, ch), :] = (z * par[1:2]).astype(zs_ref.dtype)
            return 0

        lax.fori_loop(0, nch, chunk, 0)

    return pl.pallas_call(
        body,
        name=name,
        grid=(G,),
        in_specs=[pl.BlockSpec((S, W), lambda g: (0, g)), pl.BlockSpec((None, W, W), lambda g: (g, 0, 0)),
                  pl.BlockSpec((2, W), lambda g: (0, g))],
        out_specs=pl.BlockSpec((S, W), lambda g: (0, g)),
        out_shape=jax.ShapeDtypeStruct((S, D), BF16),
        scratch_shapes=[pltpu.VMEM((S + PH, W), F32)],
        compiler_params=_cparams("parallel"),
    )(u, wgrp, par)


def _pool_bwd(name, u, dzs, wgrp, par):
    S, D = u.shape
    G, W = wgrp.shape[0], wgrp.shape[1]
    ch = _tile(S, SEQ_CHUNK)
    nch = S // ch
    PH = POOL_HALO

    def body(u_ref, dzs_ref, w_ref, par_ref, du_ref, dw_ref, dpar_ref, u_pad, q_pad, dw_acc):
        g = pl.program_id(0)
        _pad_copy(u_pad, u_ref, PH, 0)
        q_pad[pl.ds(S, PH), :] = jnp.zeros((PH, W), F32)
        par = par_ref[...]
        w = w_ref[...]
        dw_acc[...] = jnp.zeros_like(dw_acc)

        def chunk(ci, acc):
            db, dsc = acc
            r0 = pl.multiple_of(ci * ch, ch)
            xs = u_pad[pl.ds(r0, ch + PH), :]
            ws = _window_sum(xs, g, False)[PH:]
            uc = xs[PH:]
            cnt = _pool_count(_rows(uc.shape) + r0, g)
            pooled = (ws / cnt - uc).astype(BF16)
            z = jnp.dot(pooled, w, preferred_element_type=F32) + par[0:1]
            dzs = dzs_ref[pl.ds(r0, ch), :]
            dz = dzs * par[1:2]
            dzb = dz.astype(BF16)
            dw_acc[...] += lax.dot_general(pooled, dzb, (((0,), (0,)), ((), ())), preferred_element_type=F32)
            dpooled = lax.dot_general(dzb, w, (((1,), (1,)), ((), ())), preferred_element_type=F32)
            q_pad[pl.ds(r0, ch), :] = dpooled / cnt
            return (db + jnp.sum(dz, axis=0, keepdims=True), dsc + jnp.sum(dzs * z, axis=0, keepdims=True))

        zero = jnp.zeros((1, W), F32)
        db, dsc = lax.fori_loop(0, nch, chunk, (zero, zero))
        dpar_ref[...] = jnp.concatenate([db, dsc], axis=0)
        dw_ref[...] = dw_acc[...].astype(dw_ref.dtype)

        def back(ci, _):
            r0 = pl.multiple_of(ci * ch, ch)
            qs = q_pad[pl.ds(r0, ch + PH), :]
            ws = _window_sum(qs, g, True)[:ch]
            qc = qs[:ch]
            cnt = _pool_count(_rows(qc.shape) + r0, g)
            du_ref[pl.ds(r0, ch), :] = (ws - qc * cnt).astype(du_ref.dtype)
            return 0

        lax.fori_loop(0, nch, back, 0)

    blk = pl.BlockSpec((S, W), lambda g: (0, g))
    wspec = pl.BlockSpec((None, W, W), lambda g: (g, 0, 0))
    pspec = pl.BlockSpec((2, W), lambda g: (0, g))
    return pl.pallas_call(
        body,
        name=name,
        grid=(G,),
        in_specs=[blk, blk, wspec, pspec],
        out_specs=[blk, wspec, pspec],
        out_shape=[jax.ShapeDtypeStruct((S, D), BF16), jax.ShapeDtypeStruct((G, W, W), BF16),
                   jax.ShapeDtypeStruct((2, D), F32)],
        scratch_shapes=[pltpu.VMEM((S + PH, W), F32), pltpu.VMEM((S + PH, W), F32), pltpu.VMEM((W, W), F32)],
        compiler_params=_cparams("parallel"),
    )(u, dzs, wgrp, par)


def _my_place():
    x, y, c = lax.axis_index("x"), lax.axis_index("y"), lax.axis_index("c")
    return x, y, c, 4 * x + 2 * y + c


def _peers(x, y, c):
    out = []
    for d in range(1, N_DEV):
        px = 1 - x if d & 4 else x
        py = 1 - y if d & 2 else y
        pc = 1 - c if d & 1 else c
        out.append(((px, py, pc), 4 * px + 2 * py + pc))
    return out


def _window(ref, axis, start, size):
    idx = [slice(None)] * len(ref.shape)
    idx[axis] = pl.ds(start, size)
    return ref.at[tuple(idx)]


def _to_bf16(name, arrs):
    outs = []
    for i, a in enumerate(arrs):
        a2 = a.reshape(-1, a.shape[-1])
        tr = _tile(a2.shape[0], 512)
        o = pl.pallas_call(
            lambda a_ref, o_ref: o_ref.__setitem__(Ellipsis, a_ref[...].astype(BF16)),
            name=f"{name}_{i}",
            grid=(a2.shape[0] // tr,),
            in_specs=[pl.BlockSpec((tr, a2.shape[1]), lambda r: (r, 0))],
            out_specs=pl.BlockSpec((tr, a2.shape[1]), lambda r: (r, 0)),
            out_shape=jax.ShapeDtypeStruct(a2.shape, BF16),
            compiler_params=_cparams("parallel"),
        )(a2)
        outs.append(o.reshape(a.shape))
    return outs


def _all_gather(name, shards, axes):
    n = len(shards)
    sizes = [s.shape[ax] for s, ax in zip(shards, axes)]

    def body(*refs):
        ins, outs = refs[:n], refs[n:2 * n]
        send, recv, loc = refs[2 * n:]
        x, y, c, me = _my_place()
        peers = _peers(x, y, c)
        local = []
        for i in range(n):
            dst = _window(outs[i], axes[i], me * sizes[i], sizes[i])
            cp = pltpu.make_async_copy(ins[i], dst, loc.at[i])
            cp.start()
            local.append(cp)
            for peer, _ in peers:
                pltpu.make_async_remote_copy(src_ref=ins[i], dst_ref=dst, send_sem=send.at[i], recv_sem=recv.at[i],
                                             device_id=peer, device_id_type=MESH).start()
        for i in range(n):
            local[i].wait()
            seven = _window(outs[i], axes[i], 0, (N_DEV - 1) * sizes[i])
            pltpu.make_async_remote_copy(src_ref=seven, dst_ref=seven, send_sem=send.at[i], recv_sem=recv.at[i],
                                         device_id=(x, y, c), device_id_type=MESH).wait()

    def full_shape(s, ax):
        shp = list(s.shape)
        shp[ax] *= N_DEV
        return jax.ShapeDtypeStruct(tuple(shp), s.dtype)

    any_spec = pl.BlockSpec(memory_space=pl.ANY)
    return pl.pallas_call(
        body,
        name=name,
        in_specs=[any_spec] * n,
        out_specs=[any_spec] * n,
        out_shape=[full_shape(s, ax) for s, ax in zip(shards, axes)],
        scratch_shapes=[pltpu.SemaphoreType.DMA((n,)), pltpu.SemaphoreType.DMA((n,)), pltpu.SemaphoreType.DMA((n,))],
        compiler_params=pltpu.CompilerParams(has_side_effects=True),
    )(*shards)


def _exchange(name, groups):
    n = len(groups)
    flat = [a for arrs, _ in groups for a in arrs]
    offs = []
    o = 0
    for arrs, _ in groups:
        offs.append(o)
        o += len(arrs)
    nf = len(flat)
    sizes = [arrs[0].shape[ax] // N_DEV for arrs, ax in groups]

    def body(*refs):
        ins, outs = refs[:nf], refs[nf:nf + n]
        send, recv, loc = refs[nf + n:]
        x, y, c, me = _my_place()
        peers = _peers(x, y, c)
        local = []
        for gi, (arrs, ax) in enumerate(groups):
            for li in range(len(arrs)):
                src_ref = ins[offs[gi] + li]
                k = offs[gi] + li
                cp = pltpu.make_async_copy(_window(src_ref, ax, me * sizes[gi], sizes[gi]), outs[gi].at[me, li],
                                           loc.at[k])
                cp.start()
                local.append(cp)
                for peer, pidx in peers:
                    pltpu.make_async_remote_copy(
                        src_ref=_window(src_ref, ax, pidx * sizes[gi], sizes[gi]), dst_ref=outs[gi].at[me, li],
                        send_sem=send.at[k], recv_sem=recv.at[k], device_id=peer, device_id_type=MESH).start()
        for gi, (arrs, ax) in enumerate(groups):
            for li in range(len(arrs)):
                k = offs[gi] + li
                local[k].wait()
                seven = outs[gi].at[pl.ds(0, N_DEV - 1), li]
                pltpu.make_async_remote_copy(src_ref=seven, dst_ref=seven, send_sem=send.at[k], recv_sem=recv.at[k],
                                             device_id=(x, y, c), device_id_type=MESH).wait()

    def buf_shape(arrs, ax):
        shp = list(arrs[0].shape)
        shp[ax] //= N_DEV
        return jax.ShapeDtypeStruct((N_DEV, len(arrs)) + tuple(shp), arrs[0].dtype)

    any_spec = pl.BlockSpec(memory_space=pl.ANY)
    return pl.pallas_call(
        body,
        name=name,
        in_specs=[any_spec] * nf,
        out_specs=[any_spec] * n,
        out_shape=[buf_shape(arrs, ax) for arrs, ax in groups],
        scratch_shapes=[pltpu.SemaphoreType.DMA((nf,)), pltpu.SemaphoreType.DMA((nf,)), pltpu.SemaphoreType.DMA((nf,))],
        compiler_params=pltpu.CompilerParams(has_side_effects=True),
    )(*flat)


HBM_SPEC = pl.BlockSpec(memory_space=pltpu.HBM)
SEM_SPEC = pl.BlockSpec(memory_space=pltpu.SEMAPHORE)
SPLIT_EFFECT = pltpu.SideEffectType.DATAFLOW_SIDE_EFFECTING


def _push_all(kind, src, dst, axis, size, send_sem, recv_sem, place):
    x, y, c, me = place
    for peer, pidx in _peers(x, y, c):
        if kind == "gather":
            s, d = src, _window(dst, axis, me * size, size)
        else:
            s, d = _window(src, axis, pidx * size, size), dst.at[me]
        pltpu.make_async_remote_copy(src_ref=s, dst_ref=d, send_sem=send_sem, recv_sem=recv_sem, device_id=peer,
                                     device_id_type=MESH).start()


def _drain_all(kind, dst, axis, size, send_sem, recv_sem, place):
    x, y, c, _ = place
    seven = _window(dst, axis, 0, (N_DEV - 1) * size) if kind == "gather" else dst.at[pl.ds(0, N_DEV - 1)]
    pltpu.make_async_remote_copy(src_ref=seven, dst_ref=seven, send_sem=send_sem, recv_sem=recv_sem,
                                 device_id=(x, y, c), device_id_type=MESH).wait()


def _own_block_placed(kind, src, axis, size, me):
    if kind == "gather":
        shp = list(src.shape)
        shp[axis] *= N_DEV
        return lax.dynamic_update_slice_in_dim(lax.empty(tuple(shp), src.dtype), src, me * size, axis)
    own = lax.dynamic_slice_in_dim(src, me * size, size, axis)
    return lax.dynamic_update_slice_in_dim(lax.empty((N_DEV,) + own.shape, src.dtype), own[None], me, 0)


def _split_start(name, kind, srcs, lands, axes, sizes):
    n = len(srcs)

    def body(*refs):
        src_refs, land_refs = refs[:n], refs[n:2 * n]
        send, recv = refs[2 * n], refs[2 * n + 1]
        token = refs[-1]
        place = _my_place()
        for k in range(n):
            _push_all(kind, src_refs[k], land_refs[k], axes[k], sizes[k], send.at[k], recv.at[k], place)
        token[...] = jnp.zeros_like(token)

    hbm = lambda a: pltpu.HBM(a.shape, a.dtype)
    res = pl.pallas_call(
        body,
        name=name,
        out_shape=(pltpu.SemaphoreType.DMA((n,)), pltpu.SemaphoreType.DMA((n,)), *[hbm(a) for a in srcs],
                   *[hbm(a) for a in lands], jax.ShapeDtypeStruct((SUBLANES, LANES), F32)),
        in_specs=[HBM_SPEC] * (2 * n),
        out_specs=(SEM_SPEC, SEM_SPEC, *[HBM_SPEC] * (2 * n), pl.BlockSpec(memory_space=pltpu.VMEM)),
        input_output_aliases={k: 2 + k for k in range(2 * n)},
        compiler_params=pltpu.CompilerParams(has_side_effects=SPLIT_EFFECT),
    )(*[pltpu.with_memory_space_constraint(a, pltpu.HBM) for a in (*srcs, *lands)])
    return res[0], res[1], list(res[2:2 + n]), list(res[2 + n:2 + 2 * n]), res[-1]


def _split_wait(name, kind, handle, axes, sizes, after):
    send, recv, srcs, lands, _ = handle
    n = len(srcs)

    def body(*refs):
        land_refs = refs[n:2 * n]
        send_ref, recv_ref = refs[2 * n], refs[2 * n + 1]
        place = _my_place()
        for k in range(n):
            _drain_all(kind, land_refs[k], axes[k], sizes[k], send_ref.at[k], recv_ref.at[k], place)

    hbm = lambda a: pltpu.HBM(a.shape, a.dtype)
    res = pl.pallas_call(
        body,
        name=name,
        out_shape=tuple(hbm(a) for a in (*srcs, *lands)),
        in_specs=[HBM_SPEC] * (2 * n) + [SEM_SPEC, SEM_SPEC, pl.BlockSpec(memory_space=pl.ANY)],
        out_specs=tuple([HBM_SPEC] * (2 * n)),
        input_output_aliases={k: k for k in range(2 * n)},
        compiler_params=pltpu.CompilerParams(has_side_effects=SPLIT_EFFECT),
    )(*srcs, *lands, send, recv, after)
    return list(res[n:])


def _cast_layer(name, a, l):
    shp = a.shape[1:]
    a3 = a.reshape(a.shape[0], -1, shp[-1])
    R, C = a3.shape[1], a3.shape[2]
    tr = _tile(R, 512)
    o = pl.pallas_call(
        lambda a_ref, o_ref: o_ref.__setitem__(Ellipsis, a_ref[...].astype(BF16)),
        name=name,
        grid=(R // tr,),
        in_specs=[pl.BlockSpec((None, tr, C), lambda r: (l, r, 0))],
        out_specs=pl.BlockSpec((tr, C), lambda r: (r, 0)),
        out_shape=jax.ShapeDtypeStruct((R, C), BF16),
        compiler_params=_cparams("parallel"),
    )(a3)
    return o.reshape(shp)


def _adamw_math(w, g, m, v):
    m = ADAM_B1 * m + (1.0 - ADAM_B1) * g
    v = ADAM_B2 * v + (1.0 - ADAM_B2) * jnp.square(g)
    m_hat = m / (1.0 - ADAM_B1 ** ADAM_STEP)
    v_hat = v / (1.0 - ADAM_B2 ** ADAM_STEP)
    delta = -ADAM_LR * (m_hat / (jnp.sqrt(v_hat) + ADAM_EPS) + ADAM_WD * w)
    return delta, m, v


def _sum_slots(buf_ref):
    g = buf_ref[0].astype(F32)
    for s in range(1, N_DEV):
        g = g + buf_ref[s].astype(F32)
    return g


def _adamw_layer(name, buf, w, m, v, l, prev):
    shape = w.shape
    L, C = shape[0], shape[-1]
    Rr = math.prod(shape[1:-1])
    buf3 = buf.reshape(N_DEV, Rr, C)
    w3, m3, v3 = (t.reshape(L, Rr, C) for t in (w, m, v))
    tr = _tile(Rr, 2 * LANES) if Rr % LANES == 0 else Rr
    n_prev = 0 if prev is None else 4

    def body(buf_ref, w_ref, m_ref, v_ref, *rest):
        g_out, d_out, m_out, v_out = rest[n_prev:]
        g = _sum_slots(buf_ref)
        d, mm, vv = _adamw_math(w_ref[...], g, m_ref[...], v_ref[...])
        g_out[...] = g
        d_out[...] = d
        m_out[...] = mm
        v_out[...] = vv

    spec = pl.BlockSpec((None, tr, C), lambda r: (l, r, 0))
    outs = pl.pallas_call(
        body,
        name=name,
        grid=(Rr // tr,),
        in_specs=[pl.BlockSpec((N_DEV, tr, C), lambda r: (0, r, 0)), spec, spec, spec]
        + [pl.BlockSpec(memory_space=pl.ANY)] * n_prev,
        out_specs=[spec] * 4,
        out_shape=[jax.ShapeDtypeStruct((L, Rr, C), F32)] * 4,
        input_output_aliases={4 + k: k for k in range(n_prev)},
        compiler_params=_cparams("parallel"),
    )(buf3, w3, m3, v3, *(prev or ()))
    return list(outs)


def _sum8(name, buf):
    R = buf.shape[1]

    def body(buf_ref, o_ref):
        o_ref[...] = _sum_slots(buf_ref)

    return pl.pallas_call(
        body,
        name=name,
        in_specs=[pl.BlockSpec(buf.shape, lambda: (0, 0, 0))],
        out_specs=pl.BlockSpec((R, LANES), lambda: (0, 0)),
        out_shape=jax.ShapeDtypeStruct((R, LANES), F32),
        compiler_params=_cparams(),
    )(buf)


def _adamw_packed(name, w, g, m, v):
    R = w.shape[0]
    tr = _row_tile(R, 512)

    def body(w_ref, g_ref, m_ref, v_ref, d_out, m_out, v_out):
        d, mm, vv = _adamw_math(w_ref[...], g_ref[...], m_ref[...], v_ref[...])
        d_out[...] = d
        m_out[...] = mm
        v_out[...] = vv

    spec = pl.BlockSpec((tr, LANES), lambda r: (r, 0))
    return pl.pallas_call(
        body,
        name=name,
        grid=(R // tr,),
        in_specs=[spec] * 4,
        out_specs=[spec] * 3,
        out_shape=[jax.ShapeDtypeStruct((R, LANES), F32)] * 3,
        compiler_params=_cparams("parallel"),
    )(w, g, m, v)


def _pack(arrs, pad_rows_to=SUBLANES):
    parts = []
    for a in arrs:
        flat = a.reshape(-1)
        per = LANES * pad_rows_to
        padded = -(-flat.shape[0] // per) * per
        if padded != flat.shape[0]:
            flat = jnp.pad(flat, (0, padded - flat.shape[0]))
        parts.append(flat.reshape(-1, LANES))
    return jnp.concatenate(parts, axis=0)


def _unpack(packed, shapes, pad_rows_to=SUBLANES):
    out = []
    r = 0
    for shp in shapes:
        nel = math.prod(shp)
        per = LANES * pad_rows_to
        rows = -(-nel // per) * pad_rows_to
        out.append(packed[r:r + rows].reshape(-1)[:nel].reshape(shp))
        r += rows
    return out


BIG = ("lru_w_in", "lru_w_out", "pool_w_in", "pool_w_grp", "pool_w_out", "mlp_w1", "mlp_w2", "ple_w", "ple_gate_w")
BIG_AXIS = {"lru_w_in": 2, "lru_w_out": 1, "pool_w_in": 1, "pool_w_grp": 2, "pool_w_out": 1, "mlp_w1": 2,
            "mlp_w2": 1, "ple_w": 2, "ple_gate_w": 1}
SMALL_SHARDED = ("lru_conv_w", "pool_b_grp", "pool_scale")
REPLICATED = ("lru_conv_b", "lru_wa", "lru_ba", "lru_wx", "lru_bx", "lru_lambda", "ln_mix_g", "ln_mix_b",
              "ln_mlp_g", "ln_mlp_b", "ple_gate_b")
WEIGHTS = ("lru_w_in", "lru_conv_w", "lru_conv_b", "lru_wa", "lru_ba", "lru_wx", "lru_bx", "lru_lambda", "lru_w_out",
           "pool_w_in", "pool_w_grp", "pool_b_grp", "pool_scale", "pool_w_out", "ln_mix_g", "ln_mix_b", "mlp_w1",
           "mlp_w2", "ln_mlp_g", "ln_mlp_b", "ple_w", "ple_gate_w", "ple_gate_b")
INPUTS = ("x", "p") + WEIGHTS + ("loss_target",) + tuple("m_" + n for n in WEIGHTS) + tuple("v_" + n for n in WEIGHTS)


def _gather_last_axis(packed_full, shard_shape):
    nel = math.prod(shard_shape)
    blocks = packed_full.reshape(N_DEV, -1)[:, :nel].reshape((N_DEV,) + tuple(shard_shape))
    return jnp.concatenate([blocks[d] for d in range(N_DEV)], axis=-1)


def kernel(x, p, lru_w_in, lru_conv_w, lru_conv_b, lru_wa, lru_ba, lru_wx, lru_bx, lru_lambda, lru_w_out, pool_w_in, pool_w_grp, pool_b_grp, pool_scale, pool_w_out, ln_mix_g, ln_mix_b, mlp_w1, mlp_w2, ln_mlp_g, ln_mlp_b, ple_w, ple_gate_w, ple_gate_b, loss_target, m_lru_w_in, m_lru_conv_w, m_lru_conv_b, m_lru_wa, m_lru_ba, m_lru_wx, m_lru_bx, m_lru_lambda, m_lru_w_out, m_pool_w_in, m_pool_w_grp, m_pool_b_grp, m_pool_scale, m_pool_w_out, m_ln_mix_g, m_ln_mix_b, m_mlp_w1, m_mlp_w2, m_ln_mlp_g, m_ln_mlp_b, m_ple_w, m_ple_gate_w, m_ple_gate_b, v_lru_w_in, v_lru_conv_w, v_lru_conv_b, v_lru_wa, v_lru_ba, v_lru_wx, v_lru_bx, v_lru_lambda, v_lru_w_out, v_pool_w_in, v_pool_w_grp, v_pool_b_grp, v_pool_scale, v_pool_w_out, v_ln_mix_g, v_ln_mix_b, v_mlp_w1, v_mlp_w2, v_ln_mlp_g, v_ln_mlp_b, v_ple_w, v_ple_gate_w, v_ple_gate_b):
    A = dict(zip(INPUTS, (x, p, lru_w_in, lru_conv_w, lru_conv_b, lru_wa, lru_ba, lru_wx, lru_bx, lru_lambda, lru_w_out, pool_w_in, pool_w_grp, pool_b_grp, pool_scale, pool_w_out, ln_mix_g, ln_mix_b, mlp_w1, mlp_w2, ln_mlp_g, ln_mlp_b, ple_w, ple_gate_w, ple_gate_b, loss_target, m_lru_w_in, m_lru_conv_w, m_lru_conv_b, m_lru_wa, m_lru_ba, m_lru_wx, m_lru_bx, m_lru_lambda, m_lru_w_out, m_pool_w_in, m_pool_w_grp, m_pool_b_grp, m_pool_scale, m_pool_w_out, m_ln_mix_g, m_ln_mix_b, m_mlp_w1, m_mlp_w2, m_ln_mlp_g, m_ln_mlp_b, m_ple_w, m_ple_gate_w, m_ple_gate_b, v_lru_w_in, v_lru_conv_w, v_lru_conv_b, v_lru_wa, v_lru_ba, v_lru_wx, v_lru_bx, v_lru_lambda, v_lru_w_out, v_pool_w_in, v_pool_w_grp, v_pool_b_grp, v_pool_scale, v_pool_w_out, v_ln_mix_g, v_ln_mix_b, v_mlp_w1, v_mlp_w2, v_ln_mlp_g, v_ln_mlp_b, v_ple_w, v_ple_gate_w, v_ple_gate_b)))
    depth = ln_mix_g.shape[0]
    alpha = (2 * depth) ** 0.25
    S, D = x.shape[1], x.shape[2]
    xs = x.reshape(S, D)
    tgt = loss_target.reshape(S, D)
    p3 = p.reshape(depth, S, p.shape[-1])
    me = 4 * lax.axis_index("x") + 2 * lax.axis_index("y") + lax.axis_index("c")

    def layer_weights(i):
        s = i // 2
        mixer = ("lru_w_in", "lru_w_out") if i % 2 == 0 else ("pool_w_in", "pool_w_grp", "pool_w_out")
        return [(n, s) for n in mixer] + [(n, i) for n in ("mlp_w1", "mlp_w2", "ple_w", "ple_gate_w")]

    def push_plan(keys, arrs, kind):
        axes = [BIG_AXIS[n] - 1 for n, _ in keys]
        sizes = [a.shape[ax] // (1 if kind == "gather" else N_DEV) for a, ax in zip(arrs, axes)]
        lands = [_own_block_placed(kind, a, ax, sz, me) for a, ax, sz in zip(arrs, axes, sizes)]
        return lands, axes, sizes

    shard = {k: _cast_layer(f"cast_{k[0]}_{k[1]}", A[k[0]], k[1]) for i in range(depth) for k in layer_weights(i)}
    W = {n: [None] * A[n].shape[0] for n in BIG}
    small_shard_shapes = [A[n].shape for n in SMALL_SHARDED]
    small_packed = _pack([A[n] for n in SMALL_SHARDED])
    keys0 = layer_weights(0)
    gathered = _all_gather("gather_l0", [shard[k] for k in keys0] + [small_packed],
                           [BIG_AXIS[n] - 1 for n, _ in keys0] + [0])
    for (n, l), full in zip(keys0, gathered[:-1]):
        W[n][l] = full
    gather_pending = {}
    for i in range(1, depth):
        keys = layer_weights(i)
        srcs = [shard[k] for k in keys]
        lands, axes, sizes = push_plan(keys, srcs, "gather")
        gather_pending[i] = (keys, _split_start(f"gather_l{i}_start", "gather", srcs, lands, axes, sizes), axes, sizes)
    gather_tokens = [gather_pending[i][1][4] for i in range(1, depth)]
    small_full = gathered[-1].reshape(N_DEV, -1, LANES)
    r = 0
    for n, shp in zip(SMALL_SHARDED, small_shard_shapes):
        rows = -(-math.prod(shp) // (LANES * SUBLANES)) * SUBLANES
        W[n] = _gather_last_axis(small_full[:, r:r + rows], shp)
        r += rows
    wa_b, wx_b = _to_bf16("cast_gates", [lru_wa, lru_wx])
    n_lru = lru_w_in.shape[0]
    lru_par = [jnp.concatenate([W["lru_conv_w"][s], lru_conv_b[s][None], lru_ba[s][None], lru_bx[s][None],
                                lru_lambda[s][None]], axis=0) for s in range(n_lru)]
    pool_par = [jnp.stack([W["pool_b_grp"][s], W["pool_scale"][s]], axis=0) for s in range(pool_w_in.shape[0])]

    saved = []
    h_in = xs
    for i in range(depth):
        s = i // 2
        sv = {"x0": h_in}
        if i > 0:
            keys, handle, axes, sizes = gather_pending[i]
            landed = _split_wait(f"gather_l{i}_wait", "gather", handle, axes, sizes, h_in)
            for (n, l), full in zip(keys, landed):
                W[n][l] = full
        if i % 2 == 0:
            sv["proj"] = _mm(f"l{i}_lru_in", h_in, W["lru_w_in"], "nn", [F32], b_lead=s,
                             after=gather_tokens if i == 0 else ())
            sv["gh"], sv["h"] = _lru_fwd(f"l{i}_lru_core", sv["proj"], lru_par[s], wa_b[s], wx_b[s])
            sv["mix"] = _mm(f"l{i}_lru_out", sv["gh"], W["lru_w_out"], "nn", [F32], b_lead=s)
        else:
            sv["u"] = _mm(f"l{i}_pool_in", h_in, W["pool_w_in"], "nn", [F32], b_lead=s)
            sv["zs"] = _pool_fwd(f"l{i}_pool_core", sv["u"], W["pool_w_grp"][s], pool_par[s])
            sv["mix"] = _mm(f"l{i}_pool_out", sv["zs"], W["pool_w_out"], "nn", [F32], b_lead=s)
        sv["x1"] = _ln_fwd(f"l{i}_ln_mix", alpha, h_in, sv["mix"], ln_mix_g[i][None], ln_mix_b[i][None])
        sv["hpre"], sv["hact"] = _mm(f"l{i}_mlp_up", sv["x1"], W["mlp_w1"], "nn", [BF16, BF16], b_lead=i,
                                     epi=lambda acc: (acc, jnp.square(jnp.maximum(acc, 0.0))))
        sv["mo"] = _mm(f"l{i}_mlp_down", sv["hact"], W["mlp_w2"], "nn", [F32], b_lead=i)
        sv["x2"] = _ln_fwd(f"l{i}_ln_mlp", alpha, sv["x1"], sv["mo"], ln_mlp_g[i][None], ln_mlp_b[i][None])
        sv["pp"] = _mm(f"l{i}_ple_up", p3, W["ple_w"], "nn", [F32], a_lead=i, b_lead=i)

        def ple_epi(acc, bg, x2t, ppt):
            gpre = acc + bg
            return x2t + ppt * jax.nn.sigmoid(gpre), gpre

        h_in, sv["gpre"] = _mm(f"l{i}_ple_gate", sv["x2"], W["ple_gate_w"], "nn", [F32, F32], b_lead=i,
                               epi=ple_epi, extras=[ple_gate_b[i][None], sv["x2"], sv["pp"]])
        saved.append(sv)

    dx, sq = _loss_and_grad("loss", h_in, tgt)
    loss = lax.psum(0.5 * sq[0, 0] / D, ("x", "y", "c"))

    dW = {n: [None] * A[n].shape[0] for n in BIG}
    dsmall = {n: [None] * A[n].shape[0] for n in REPLICATED + SMALL_SHARDED}
    exchange_pending = {}
    exchange_token = ()
    for i in reversed(range(depth)):
        s = i // 2
        sv = saved[i]
        dpp, dgpre, dbg = _ple_bwd(f"l{i}_ple_bwd", dx, sv["gpre"], sv["pp"], after=exchange_token)
        dsmall["ple_gate_b"][i] = dbg[0]
        dW["ple_w"][i] = _mm(f"l{i}_d_ple_w", p3, dpp, "tn", [BF16], a_lead=i)
        dW["ple_gate_w"][i] = _mm(f"l{i}_d_ple_gate_w", sv["x2"], dgpre, "tn", [BF16])
        dx2 = _mm(f"l{i}_d_x2", dgpre, W["ple_gate_w"], "nt", [F32], b_lead=i, extras=[dx],
                  epi=lambda acc, d: (acc + d,))
        dz2, dg, db = _ln_bwd(f"l{i}_ln_mlp_bwd", alpha, dx2, sv["x1"], sv["mo"], ln_mlp_g[i][None])
        dsmall["ln_mlp_g"][i], dsmall["ln_mlp_b"][i] = dg[0], db[0]
        dhpre = _mm(f"l{i}_d_hpre", dz2, W["mlp_w2"], "nt", [BF16], b_lead=i, extras=[sv["hpre"]],
                    epi=lambda acc, hp: (acc * (2.0 * jnp.maximum(hp.astype(F32), 0.0)),))
        dW["mlp_w2"][i] = _mm(f"l{i}_d_mlp_w2", sv["hact"], dz2, "tn", [BF16])
        dW["mlp_w1"][i] = _mm(f"l{i}_d_mlp_w1", sv["x1"], dhpre, "tn", [BF16])
        dx1 = _mm(f"l{i}_d_x1", dhpre, W["mlp_w1"], "nt", [F32], b_lead=i, extras=[dz2],
                  epi=lambda acc, d: (acc + alpha * d,))
        dz1, dg, db = _ln_bwd(f"l{i}_ln_mix_bwd", alpha, dx1, sv["x0"], sv["mix"], ln_mix_g[i][None])
        dsmall["ln_mix_g"][i], dsmall["ln_mix_b"][i] = dg[0], db[0]
        if i % 2 == 0:
            dW["lru_w_out"][s] = _mm(f"l{i}_d_lru_w_out", sv["gh"], dz1, "tn", [BF16])
            dgh = _mm(f"l{i}_d_gh", dz1, W["lru_w_out"], "nt", [F32], b_lead=s)
            dup, dy, dwa, dwx, dpar = _lru_bwd(f"l{i}_lru_core_bwd", sv["proj"], sv["h"], dgh, lru_par[s],
                                               wa_b[s], wx_b[s])
            dsmall["lru_wa"][s], dsmall["lru_wx"][s] = dwa, dwx
            dsmall["lru_conv_w"][s] = dpar[0:4]
            for k, n in enumerate(("lru_conv_b", "lru_ba", "lru_bx", "lru_lambda")):
                dsmall[n][s] = dpar[4 + k]
            dmix_in = jnp.concatenate([dup, dy], axis=1)
            win = "lru_w_in"
        else:
            dW["pool_w_out"][s] = _mm(f"l{i}_d_pool_w_out", sv["zs"], dz1, "tn", [BF16])
            dzs = _mm(f"l{i}_d_zs", dz1, W["pool_w_out"], "nt", [F32], b_lead=s)
            dmix_in, dW["pool_w_grp"][s], dpar = _pool_bwd(f"l{i}_pool_core_bwd", sv["u"], dzs, W["pool_w_grp"][s],
                                                          pool_par[s])
            dsmall["pool_b_grp"][s], dsmall["pool_scale"][s] = dpar[0], dpar[1]
            win = "pool_w_in"
        dW[win][s] = _mm(f"l{i}_d_{win}", sv["x0"], dmix_in, "tn", [BF16])
        dx = _mm(f"l{i}_d_x0", dmix_in, W[win], "nt", [F32], b_lead=s, extras=[dz1],
                 epi=lambda acc, d: (acc + alpha * d,))
        if i > 0:
            keys = layer_weights(i)
            srcs = [dW[n][l] for n, l in keys]
            lands, axes, sizes = push_plan(keys, srcs, "scatter")
            handle = _split_start(f"exchange_l{i}_start", "scatter", srcs, lands, axes, sizes)
            exchange_pending[i] = (keys, handle, axes, sizes)
            exchange_token = (handle[4],)
    grad_x = dx.reshape(x.shape)

    small_names = REPLICATED + SMALL_SHARDED
    small_grads = [jnp.stack(dsmall[n]) for n in small_names]
    small_shapes = [g.shape for g in small_grads]
    packed_g = _pack(small_grads)
    assert packed_g.shape[0] % (N_DEV * SUBLANES) == 0, packed_g.shape
    groups = [([dW[n][l]], BIG_AXIS[n] - 1) for n, l in keys0] + [([packed_g], 0)]
    bufs = _exchange("exchange_l0", groups)
    partial = {k: b.reshape((N_DEV,) + b.shape[2:]) for k, b in zip(keys0, bufs[:-1])}
    for i in range(1, depth):
        keys, handle, axes, sizes = exchange_pending[i]
        landed = _split_wait(f"exchange_l{i}_wait", "scatter", handle, axes, sizes, bufs[-1])
        partial.update(zip(keys, landed))

    outs = {}
    for n in BIG:
        stacked = None
        for l in range(A[n].shape[0]):
            stacked = _adamw_layer(f"adamw_{n}_{l}", partial[(n, l)], A[n], A["m_" + n], A["v_" + n], l, stacked)
        outs[n] = [o.reshape(A[n].shape) for o in stacked]
    red = _sum8("sum_small", bufs[-1][:, 0])
    red_full = _all_gather("gather_small", [red], [0])[0]
    small_g = dict(zip(small_names, _unpack(red_full, small_shapes)))
    for n in SMALL_SHARDED:
        width = A[n].shape[-1]
        small_g[n] = lax.dynamic_slice_in_dim(small_g[n], me * width, width, axis=small_g[n].ndim - 1)
    pk = lambda pre: _pack([A[pre + n] for n in small_names])
    d_p, m_p, v_p = _adamw_packed("adamw_small", pk(""), _pack([small_g[n] for n in small_names]), pk("m_"), pk("v_"))
    shapes = [A[n].shape for n in small_names]
    for n, d_, m_, v_ in zip(small_names, _unpack(d_p, shapes), _unpack(m_p, shapes), _unpack(v_p, shapes)):
        outs[n] = [small_g[n], d_, m_, v_]

    return (loss, grad_x, *[outs[n][0] for n in WEIGHTS], *[outs[n][1] for n in WEIGHTS],
            *[outs[n][2] for n in WEIGHTS], *[outs[n][3] for n in WEIGHTS])
```

```python
import functools
import math

import jax
import jax.numpy as jnp
from jax import lax
from jax.experimental import pallas as pl
from jax.experimental.pallas import tpu as pltpu

F32 = jnp.float32
BF16 = jnp.bfloat16
MESH = pl.DeviceIdType.MESH
N_DEV = 8
LANES = 128
SUBLANES = 8

LN_EPS = 1e-5
LRU_C = 8.0
CONV_WIDTH = 4
POOL_HALO = 16
ADAM_LR = 0.001
ADAM_B1 = 0.9
ADAM_B2 = 0.999
ADAM_EPS = 1e-08
ADAM_WD = 0.01
ADAM_STEP = 10

VMEM_LIMIT = 48 * 1024 * 1024
SEQ_CHUNK = 256
MM_TK = 2048
MM_TK_TOKENS = 1024
GELU_C0 = math.sqrt(2.0 / math.pi)
GELU_C1 = 0.044715


def _cparams(*sem):
    return pltpu.CompilerParams(dimension_semantics=tuple(sem) if sem else None, vmem_limit_bytes=VMEM_LIMIT)


def _tile(n, pref):
    if n <= pref:
        return n
    t = pref - pref % LANES
    while t >= LANES:
        if n % t == 0:
            return t
        t -= LANES
    return n


def _row_tile(n, pref):
    if n <= pref:
        return n
    t = pref - pref % SUBLANES
    while t >= SUBLANES:
        if n % t == 0:
            return t
        t -= SUBLANES
    return n


def _mm(name, a, b, mode, out_dtypes, epi=None, extras=(), a_lead=None, b_lead=None, tm=1024, tn=512, tk=None,
        after=()):
    if isinstance(b, (list, tuple)):
        b, b_lead = b[b_lead], None
    a2 = a.shape[-2:]
    b2 = b.shape[-2:]
    if mode == "nn":
        (M, K), N = a2, b2[1]
        assert b2[0] == K
    elif mode == "nt":
        (M, K), N = a2, b2[0]
        assert b2[1] == K
    else:
        (K, M), N = a2, b2[1]
        assert b2[0] == K
    if tk is None:
        tk = MM_TK_TOKENS if mode == "tn" else MM_TK
    tm, tn, tk = _tile(M, tm), _tile(N, tn), _tile(K, tk)
    nk = K // tk
    n_extra = len(extras)
    n_out = len(out_dtypes)

    def lead(shape, idx, which):
        if which is None:
            return pl.BlockSpec(shape, idx)
        return pl.BlockSpec((None,) + shape, lambda i, j, k: (which,) + idx(i, j, k))

    if mode == "nn":
        a_spec = lead((tm, tk), lambda i, j, k: (i, k), a_lead)
        b_spec = lead((tk, tn), lambda i, j, k: (k, j), b_lead)
        dims = (((1,), (0,)), ((), ()))
    elif mode == "nt":
        a_spec = lead((tm, tk), lambda i, j, k: (i, k), a_lead)
        b_spec = lead((tn, tk), lambda i, j, k: (j, k), b_lead)
        dims = (((1,), (1,)), ((), ()))
    else:
        a_spec = lead((tk, tm), lambda i, j, k: (k, i), a_lead)
        b_spec = lead((tk, tn), lambda i, j, k: (k, j), b_lead)
        dims = (((0,), (0,)), ((), ()))
    e_specs = []
    for e in extras:
        if e.shape[0] == 1:
            e_specs.append(pl.BlockSpec((1, tn), lambda i, j, k: (0, j)))
        else:
            e_specs.append(pl.BlockSpec((tm, tn), lambda i, j, k: (i, j)))

    n_after = len(after)

    def body(a_ref, b_ref, *rest):
        e_refs = rest[:n_extra]
        rest = rest[:n_extra] + rest[n_extra + n_after:]
        o_refs = rest[n_extra:n_extra + n_out]
        part = lax.dot_general(a_ref[...].astype(BF16), b_ref[...].astype(BF16), dims, preferred_element_type=F32)

        def finish(r):
            res = (r,) if epi is None else epi(r, *[e[...] for e in e_refs])
            for o, v in zip(o_refs, res):
                o[...] = v.astype(o.dtype)

        if nk == 1:
            finish(part)
            return
        acc = rest[n_extra + n_out]
        k = pl.program_id(2)

        @pl.when(k == 0)
        def _():
            acc[...] = part

        @pl.when(jnp.logical_and(k > 0, k < nk - 1))
        def _():
            acc[...] += part

        @pl.when(k == nk - 1)
        def _():
            finish(acc[...] + part)

    outs = pl.pallas_call(
        body,
        name=name,
        grid=(M // tm, N // tn, nk),
        in_specs=[a_spec, b_spec] + e_specs + [pl.BlockSpec(memory_space=pl.ANY)] * n_after,
        out_specs=[pl.BlockSpec((tm, tn), lambda i, j, k: (i, j)) for _ in out_dtypes],
        out_shape=[jax.ShapeDtypeStruct((M, N), d) for d in out_dtypes],
        scratch_shapes=[pltpu.VMEM((tm, tn), F32)] if nk > 1 else [],
        compiler_params=_cparams("parallel", "parallel", "arbitrary"),
    )(a, b, *extras, *after)
    return outs[0] if n_out == 1 else tuple(outs)


def _rowwise(name, fn, tiled, params, outs, accs=(), tm=256, after=()):
    S = tiled[0].shape[0]
    tm = _tile(S, tm)
    nt, npar, no = len(tiled), len(params), len(outs)
    n_after = len(after)

    def body(*refs):
        t_refs = refs[:nt]
        p_refs = refs[nt:nt + npar]
        refs = refs[nt + npar + n_after:]
        o_refs = refs[:no]
        a_refs = refs[no:]
        res = fn(*[r[...] for r in t_refs], *[r[...] for r in p_refs])
        for o, v in zip(o_refs, res[:no]):
            o[...] = v.astype(o.dtype)
        first = pl.program_id(0) == 0
        for ar, v in zip(a_refs, res[no:]):
            @pl.when(first)
            def _(ar=ar, v=v):
                ar[...] = v

            @pl.when(jnp.logical_not(first))
            def _(ar=ar, v=v):
                ar[...] += v

    full = lambda p: pl.BlockSpec(p.shape, lambda i, nd=p.ndim: (0,) * nd)
    res = pl.pallas_call(
        body,
        name=name,
        grid=(S // tm,),
        in_specs=[pl.BlockSpec((tm, t.shape[1]), lambda i: (i, 0)) for t in tiled] + [full(p) for p in params]
        + [pl.BlockSpec(memory_space=pl.ANY)] * n_after,
        out_specs=[pl.BlockSpec((tm, c), lambda i: (i, 0)) for c, _ in outs]
        + [pl.BlockSpec(s, lambda i, nd=len(s): (0,) * nd) for s in accs],
        out_shape=[jax.ShapeDtypeStruct((S, c), d) for c, d in outs] + [jax.ShapeDtypeStruct(s, F32) for s in accs],
        compiler_params=_cparams("arbitrary"),
    )(*tiled, *params, *after)
    return res


def _ln_stats(z):
    mu = jnp.mean(z, axis=-1, keepdims=True)
    zc = z - mu
    var = jnp.mean(zc * zc, axis=-1, keepdims=True)
    return zc, lax.rsqrt(var + LN_EPS)


def _ln_fwd(name, alpha, xp, m, g, b):
    def fn(xp, m, g, b):
        zc, rstd = _ln_stats(alpha * xp + m)
        return (zc * rstd * g + b,)

    return _rowwise(name, fn, [xp, m], [g, b], [(xp.shape[1], F32)])[0]


def _ln_bwd(name, alpha, dy, xp, m, g):
    def fn(dy, xp, m, g):
        zc, rstd = _ln_stats(alpha * xp + m)
        xhat = zc * rstd
        dxh = dy * g
        m1 = jnp.mean(dxh, axis=-1, keepdims=True)
        m2 = jnp.mean(dxh * xhat, axis=-1, keepdims=True)
        dz = rstd * (dxh - m1 - xhat * m2)
        return dz, jnp.sum(dy * xhat, axis=0, keepdims=True), jnp.sum(dy, axis=0, keepdims=True)

    d = xp.shape[1]
    return _rowwise(name, fn, [dy, xp, m], [g], [(d, F32)], accs=[(1, d), (1, d)])


def _loss_and_grad(name, y, target):
    d = y.shape[1]

    def fn(y, t):
        err = y - t
        sq = jnp.sum(jnp.sum(err * err, axis=0, keepdims=True), axis=1, keepdims=True)
        return err * (1.0 / d), jnp.broadcast_to(sq, (1, LANES))

    return _rowwise(name, fn, [y, target], [], [(d, F32)], accs=[(1, LANES)])


def _ple_bwd(name, dx3, gpre, pp, after=()):
    def fn(dx3, gpre, pp):
        gate = jax.nn.sigmoid(gpre)
        dgpre = dx3 * pp * gate * (1.0 - gate)
        return dx3 * gate, dgpre, jnp.sum(dgpre, axis=0, keepdims=True)

    d = dx3.shape[1]
    return _rowwise(name, fn, [dx3, gpre, pp], [], [(d, BF16), (d, BF16)], accs=[(1, d)], after=after)


def _rows(shape):
    return lax.broadcasted_iota(jnp.int32, shape, 0)


def _gelu(y):
    t = jnp.tanh(GELU_C0 * (y + GELU_C1 * y * y * y))
    return 0.5 * y * (1.0 + t), t


def _gelu_grad(y, t):
    return 0.5 * (1.0 + t) + 0.5 * y * (1.0 - t * t) * GELU_C0 * (1.0 + 3.0 * GELU_C1 * y * y)


def _neg_expm1(x):
    series = -x * (1.0 + x * (0.5 + x * (1.0 / 6.0 + x * (1.0 / 24.0))))
    return jnp.where(x > -0.02, series, 1.0 - jnp.exp(x))


def _softplus(x):
    return jnp.maximum(x, 0.0) + jnp.log(1.0 + jnp.exp(-jnp.abs(x)))


def _conv_fwd(xs, cw, cb):
    n = xs.shape[0]
    u = cw[3:4] * xs
    for k in (1, 2, 3):
        u = u + cw[3 - k:4 - k] * pltpu.roll(xs, k, 0)
    del n
    return u[SUBLANES:] + cb


def _lru_gates(u, wa, wx, ba, bx, sp, grow):
    ub = u.astype(BF16)
    r = jax.nn.sigmoid(jnp.dot(ub, wa, preferred_element_type=F32) + ba)
    ig = jax.nn.sigmoid(jnp.dot(ub, wx, preferred_element_type=F32) + bx)
    log_a = (-LRU_C) * r * sp
    a = jnp.exp(log_a)
    mult = jnp.sqrt(_neg_expm1(2.0 * log_a))
    mult = jnp.where(grow == 0, 1.0, mult)
    return ub, r, ig, a, mult


def _scan8_fwd(a, b):
    row = _rows(a.shape)
    for k in (1, 2, 4):
        m = row >= k
        b = jnp.where(m, a * pltpu.roll(b, k, 0) + b, b)
        a = jnp.where(m, a * pltpu.roll(a, k, 0), a)
    return a, b


def _scan8_bwd(c, d):
    row = _rows(c.shape)
    for k in (1, 2, 4):
        m = row < SUBLANES - k
        d = jnp.where(m, c * pltpu.roll(d, SUBLANES - k, 0) + d, d)
        c = jnp.where(m, c * pltpu.roll(c, SUBLANES - k, 0), c)
    return c, d


def _pad_copy(dst, src, front, back):
    s, c = src.shape
    if front:
        dst[pl.ds(0, front), :] = jnp.zeros((front, c), dst.dtype)
    if back:
        dst[pl.ds(front + s, back), :] = jnp.zeros((back, c), dst.dtype)
    dst[pl.ds(front, s), :] = src[...].astype(dst.dtype)


def _lru_fwd(name, proj, par, wa, wx):
    S = proj.shape[0]
    R = proj.shape[1] // 2
    H = R // LANES
    ch = _tile(S, SEQ_CHUNK)
    nch = S // ch
    H8 = SUBLANES

    def body(up_ref, y_ref, par_ref, wa_ref, wx_ref, gh_ref, h_ref, up_pad):
        _pad_copy(up_pad, up_ref, H8, 0)
        par = par_ref[...]
        cw, cb, ba, bx = par[0:4], par[4:5], par[5:6], par[6:7]
        sp = _softplus(-par[7:8])
        wa_m, wx_m = wa_ref[...], wx_ref[...]

        def chunk(ci, carry):
            r0 = pl.multiple_of(ci * ch, ch)
            xs = up_pad[pl.ds(r0, ch + H8), :]
            u = _conv_fwd(xs, cw, cb)
            grow = _rows(u.shape) + r0
            _, _, ig, a, mult = _lru_gates(u, wa_m, wx_m, ba, bx, sp, grow)
            bt = mult * (ig * u)
            hs = []
            for j in range(ch // H8):
                aa, bb = _scan8_fwd(a[j * H8:(j + 1) * H8], bt[j * H8:(j + 1) * H8])
                hj = bb + aa * carry
                carry = hj[H8 - 1:H8]
                hs.append(hj)
            h = jnp.concatenate(hs, axis=0)
            h_ref[pl.ds(r0, ch), :] = h
            gy, _ = _gelu(y_ref[pl.ds(r0, ch), :])
            gh_ref[pl.ds(r0, ch), :] = (h * gy).astype(gh_ref.dtype)
            return carry

        lax.fori_loop(0, nch, chunk, jnp.zeros((1, LANES), F32))

    col = lambda off: pl.BlockSpec((S, LANES), lambda h: (0, h + off))
    return pl.pallas_call(
        body,
        name=name,
        grid=(H,),
        in_specs=[col(0), col(H), pl.BlockSpec((8, LANES), lambda h: (0, h)),
                  pl.BlockSpec((None, LANES, LANES), lambda h: (h, 0, 0)),
                  pl.BlockSpec((None, LANES, LANES), lambda h: (h, 0, 0))],
        out_specs=[col(0), col(0)],
        out_shape=[jax.ShapeDtypeStruct((S, R), BF16), jax.ShapeDtypeStruct((S, R), F32)],
        scratch_shapes=[pltpu.VMEM((S + H8, LANES), F32)],
        compiler_params=_cparams("parallel"),
    )(proj, proj, par, wa, wx)


def _lru_bwd(name, proj, h, dgh, par, wa, wx):
    S = proj.shape[0]
    R = proj.shape[1] // 2
    H = R // LANES
    ch = _tile(S, SEQ_CHUNK)
    nch = S // ch
    H8 = SUBLANES
    nb = ch // H8

    def body(up_ref, y_ref, h_ref, dgh_ref, par_ref, wa_ref, wx_ref,
             dup_ref, dy_ref, dwa_ref, dwx_ref, dpar_ref, up_pad, h_pad, du_pad, vec_acc):
        _pad_copy(up_pad, up_ref, H8, 0)
        _pad_copy(h_pad, h_ref, H8, 0)
        du_pad[pl.ds(S, H8), :] = jnp.zeros((H8, LANES), F32)
        par = par_ref[...]
        cw, cb, ba, bx, lam = par[0:4], par[4:5], par[5:6], par[6:7], par[7:8]
        sp = _softplus(-lam)
        wa_m, wx_m = wa_ref[...], wx_ref[...]
        dwa_ref[...] = jnp.zeros_like(dwa_ref)
        dwx_ref[...] = jnp.zeros_like(dwx_ref)
        vec_acc[...] = jnp.zeros_like(vec_acc)
        nt_dims = (((1,), (1,)), ((), ()))
        tn_dims = (((0,), (0,)), ((), ()))

        def chunk(it, carry):
            lam_next, a_next = carry
            ci = nch - 1 - it
            r0 = pl.multiple_of(ci * ch, ch)
            xs = up_pad[pl.ds(r0, ch + H8), :]
            u = _conv_fwd(xs, cw, cb)
            row = _rows(u.shape)
            grow = row + r0
            ub, r, ig, a, mult = _lru_gates(u, wa_m, wx_m, ba, bx, sp, grow)
            hs = h_pad[pl.ds(r0, ch + H8), :]
            hcur = hs[H8:]
            hprev = pltpu.roll(hs, 1, 0)[H8:]
            y = y_ref[pl.ds(r0, ch), :]
            dgh = dgh_ref[pl.ds(r0, ch), :]
            gy, t = _gelu(y)
            dy_ref[pl.ds(r0, ch), :] = (dgh * hcur * _gelu_grad(y, t)).astype(dy_ref.dtype)
            dh = dgh * gy
            c = jnp.where(row == ch - 1, a_next, pltpu.roll(a, ch - 1, 0))
            ls = [None] * nb
            for j in range(nb - 1, -1, -1):
                cc, dd = _scan8_bwd(c[j * H8:(j + 1) * H8], dh[j * H8:(j + 1) * H8])
                lj = dd + cc * lam_next
                lam_next = lj[0:1]
                ls[j] = lj
            lmb = jnp.concatenate(ls, axis=0)
            da = lmb * hprev
            gu = ig * u
            dmult = lmb * gu
            dlog_a = da * a + jnp.where(grow == 0, 0.0, dmult * (-(a * a) / mult))
            dr = dlog_a * ((-LRU_C) * sp)
            drp = dr * r * (1.0 - r)
            dip = (lmb * mult * u) * ig * (1.0 - ig)
            drb, dib = drp.astype(BF16), dip.astype(BF16)
            du = (lmb * mult * ig
                  + lax.dot_general(drb, wa_m, nt_dims, preferred_element_type=F32)
                  + lax.dot_general(dib, wx_m, nt_dims, preferred_element_type=F32))
            du_pad[pl.ds(r0, ch), :] = du
            dwa_ref[...] += lax.dot_general(ub, drb, tn_dims, preferred_element_type=F32)
            dwx_ref[...] += lax.dot_general(ub, dib, tn_dims, preferred_element_type=F32)
            ssum = lambda v: jnp.sum(v, axis=0, keepdims=True)
            vec_acc[0:1, :] += ssum(drp)
            vec_acc[1:2, :] += ssum(dip)
            vec_acc[2:3, :] += ssum(dlog_a * ((-LRU_C) * r))
            return lam_next, a[0:1]

        zero = jnp.zeros((1, LANES), F32)
        lax.fori_loop(0, nch, chunk, (zero, zero))

        def conv_chunk(ci, acc):
            r0 = pl.multiple_of(ci * ch, ch)
            ds = du_pad[pl.ds(r0, ch + H8), :]
            xs = up_pad[pl.ds(r0, ch + H8), :]
            n = ch + H8
            du = ds[:ch]
            dup = cw[3:4] * du
            new = [acc[3] + jnp.sum(du * xs[H8:], axis=0, keepdims=True)]
            for k in (1, 2, 3):
                dup = dup + cw[3 - k:4 - k] * pltpu.roll(ds, n - k, 0)[:ch]
                new.append(acc[3 - k] + jnp.sum(du * pltpu.roll(xs, k, 0)[H8:], axis=0, keepdims=True))
            dup_ref[pl.ds(r0, ch), :] = dup.astype(dup_ref.dtype)
            return (new[3], new[2], new[1], new[0], acc[4] + jnp.sum(du, axis=0, keepdims=True))

        acc = lax.fori_loop(0, nch, conv_chunk, (zero,) * 5)
        dlam = vec_acc[2:3, :] * (-jax.nn.sigmoid(-lam))
        dpar_ref[...] = jnp.concatenate(list(acc) + [vec_acc[0:1, :], vec_acc[1:2, :], dlam], axis=0)

    col = lambda off: pl.BlockSpec((S, LANES), lambda h: (0, h + off))
    head = pl.BlockSpec((None, LANES, LANES), lambda h: (h, 0, 0))
    return pl.pallas_call(
        body,
        name=name,
        grid=(H,),
        in_specs=[col(0), col(H), col(0), col(0), pl.BlockSpec((8, LANES), lambda h: (0, h)), head, head],
        out_specs=[col(0), col(0), head, head, pl.BlockSpec((8, LANES), lambda h: (0, h))],
        out_shape=[jax.ShapeDtypeStruct((S, R), BF16), jax.ShapeDtypeStruct((S, R), BF16),
                   jax.ShapeDtypeStruct((H, LANES, LANES), F32), jax.ShapeDtypeStruct((H, LANES, LANES), F32),
                   jax.ShapeDtypeStruct((8, R), F32)],
        scratch_shapes=[pltpu.VMEM((S + H8, LANES), F32), pltpu.VMEM((S + H8, LANES), F32),
                        pltpu.VMEM((S + H8, LANES), F32), pltpu.VMEM((8, LANES), F32)],
        compiler_params=_cparams("parallel"),
    )(proj, proj, h, dgh, par, wa, wx)


def _window_sum(xs, g, up):
    n = xs.shape[0]
    s = xs
    for lvl, k in enumerate((1, 2, 4, 8)):
        sh = pltpu.roll(s, (n - k) if up else k, 0)
        s = s + jnp.where(g >= lvl, sh, 0.0)
    return s


def _pool_count(grow, g):
    return jnp.minimum(grow + 1, lax.shift_left(jnp.int32(2), g)).astype(F32)


def _pool_fwd(name, u, wgrp, par):
    S, D = u.shape
    G, W = wgrp.shape[0], wgrp.shape[1]
    ch = _tile(S, SEQ_CHUNK)
    nch = S // ch
    PH = POOL_HALO

    def body(u_ref, w_ref, par_ref, zs_ref, u_pad):
        g = pl.program_id(0)
        _pad_copy(u_pad, u_ref, PH, 0)
        par = par_ref[...]
        w = w_ref[...]

        def chunk(ci, _):
            r0 = pl.multiple_of(ci * ch, ch)
            xs = u_pad[pl.ds(r0, ch + PH), :]
            ws = _window_sum(xs, g, False)[PH:]
            uc = xs[PH:]
            cnt = _pool_count(_rows(uc.shape) + r0, g)
            pooled = ws / cnt - uc
            z = jnp.dot(pooled.astype(BF16), w, preferred_element_type=F32) + par[0:1]
            zs_ref[pl.ds(r0, ch), :] = (z * par[1:2]).astype(zs_ref.dtype)
            return 0

        lax.fori_loop(0, nch, chunk, 0)

    return pl.pallas_call(
        body,
        name=name,
        grid=(G,),
        in_specs=[pl.BlockSpec((S, W), lambda g: (0, g)), pl.BlockSpec((None, W, W), lambda g: (g, 0, 0)),
                  pl.BlockSpec((2, W), lambda g: (0, g))],
        out_specs=pl.BlockSpec((S, W), lambda g: (0, g)),
        out_shape=jax.ShapeDtypeStruct((S, D), BF16),
        scratch_shapes=[pltpu.VMEM((S + PH, W), F32)],
        compiler_params=_cparams("parallel"),
    )(u, wgrp, par)


def _pool_bwd(name, u, dzs, wgrp, par):
    S, D = u.shape
    G, W = wgrp.shape[0], wgrp.shape[1]
    ch = _tile(S, SEQ_CHUNK)
    nch = S // ch
    PH = POOL_HALO

    def body(u_ref, dzs_ref, w_ref, par_ref, du_ref, dw_ref, dpar_ref, u_pad, q_pad, dw_acc):
        g = pl.program_id(0)
        _pad_copy(u_pad, u_ref, PH, 0)
        q_pad[pl.ds(S, PH), :] = jnp.zeros((PH, W), F32)
        par = par_ref[...]
        w = w_ref[...]
        dw_acc[...] = jnp.zeros_like(dw_acc)

        def chunk(ci, acc):
            db, dsc = acc
            r0 = pl.multiple_of(ci * ch, ch)
            xs = u_pad[pl.ds(r0, ch + PH), :]
            ws = _window_sum(xs, g, False)[PH:]
            uc = xs[PH:]
            cnt = _pool_count(_rows(uc.shape) + r0, g)
            pooled = (ws / cnt - uc).astype(BF16)
            z = jnp.dot(pooled, w, preferred_element_type=F32) + par[0:1]
            dzs = dzs_ref[pl.ds(r0, ch), :]
            dz = dzs * par[1:2]
            dzb = dz.astype(BF16)
            dw_acc[...] += lax.dot_general(pooled, dzb, (((0,), (0,)), ((), ())), preferred_element_type=F32)
            dpooled = lax.dot_general(dzb, w, (((1,), (1,)), ((), ())), preferred_element_type=F32)
            q_pad[pl.ds(r0, ch), :] = dpooled / cnt
            return (db + jnp.sum(dz, axis=0, keepdims=True), dsc + jnp.sum(dzs * z, axis=0, keepdims=True))

        zero = jnp.zeros((1, W), F32)
        db, dsc = lax.fori_loop(0, nch, chunk, (zero, zero))
        dpar_ref[...] = jnp.concatenate([db, dsc], axis=0)
        dw_ref[...] = dw_acc[...].astype(dw_ref.dtype)

        def back(ci, _):
            r0 = pl.multiple_of(ci * ch, ch)
            qs = q_pad[pl.ds(r0, ch + PH), :]
            ws = _window_sum(qs, g, True)[:ch]
            qc = qs[:ch]
            cnt = _pool_count(_rows(qc.shape) + r0, g)
            du_ref[pl.ds(r0, ch), :] = (ws - qc * cnt).astype(du_ref.dtype)
            return 0

        lax.fori_loop(0, nch, back, 0)

    blk = pl.BlockSpec((S, W), lambda g: (0, g))
    wspec = pl.BlockSpec((None, W, W), lambda g: (g, 0, 0))
    pspec = pl.BlockSpec((2, W), lambda g: (0, g))
    return pl.pallas_call(
        body,
        name=name,
        grid=(G,),
        in_specs=[blk, blk, wspec, pspec],
        out_specs=[blk, wspec, pspec],
        out_shape=[jax.ShapeDtypeStruct((S, D), BF16), jax.ShapeDtypeStruct((G, W, W), BF16),
                   jax.ShapeDtypeStruct((2, D), F32)],
        scratch_shapes=[pltpu.VMEM((S + PH, W), F32), pltpu.VMEM((S + PH, W), F32), pltpu.VMEM((W, W), F32)],
        compiler_params=_cparams("parallel"),
    )(u, dzs, wgrp, par)


def _my_place():
    x, y, c = lax.axis_index("x"), lax.axis_index("y"), lax.axis_index("c")
    return x, y, c, 4 * x + 2 * y + c


def _peers(x, y, c):
    out = []
    for d in range(1, N_DEV):
        px = 1 - x if d & 4 else x
        py = 1 - y if d & 2 else y
        pc = 1 - c if d & 1 else c
        out.append(((px, py, pc), 4 * px + 2 * py + pc))
    return out


def _window(ref, axis, start, size):
    idx = [slice(None)] * len(ref.shape)
    idx[axis] = pl.ds(start, size)
    return ref.at[tuple(idx)]


def _to_bf16(name, arrs):
    outs = []
    for i, a in enumerate(arrs):
        a2 = a.reshape(-1, a.shape[-1])
        tr = _tile(a2.shape[0], 512)
        o = pl.pallas_call(
            lambda a_ref, o_ref: o_ref.__setitem__(Ellipsis, a_ref[...].astype(BF16)),
            name=f"{name}_{i}",
            grid=(a2.shape[0] // tr,),
            in_specs=[pl.BlockSpec((tr, a2.shape[1]), lambda r: (r, 0))],
            out_specs=pl.BlockSpec((tr, a2.shape[1]), lambda r: (r, 0)),
            out_shape=jax.ShapeDtypeStruct(a2.shape, BF16),
            compiler_params=_cparams("parallel"),
        )(a2)
        outs.append(o.reshape(a.shape))
    return outs


def _all_gather(name, shards, axes):
    n = len(shards)
    sizes = [s.shape[ax] for s, ax in zip(shards, axes)]

    def body(*refs):
        ins, outs = refs[:n], refs[n:2 * n]
        send, recv, loc = refs[2 * n:]
        x, y, c, me = _my_place()
        peers = _peers(x, y, c)
        local = []
        for i in range(n):
            dst = _window(outs[i], axes[i], me * sizes[i], sizes[i])
            cp = pltpu.make_async_copy(ins[i], dst, loc.at[i])
            cp.start()
            local.append(cp)
            for peer, _ in peers:
                pltpu.make_async_remote_copy(src_ref=ins[i], dst_ref=dst, send_sem=send.at[i], recv_sem=recv.at[i],
                                             device_id=peer, device_id_type=MESH).start()
        for i in range(n):
            local[i].wait()
            seven = _window(outs[i], axes[i], 0, (N_DEV - 1) * sizes[i])
            pltpu.make_async_remote_copy(src_ref=seven, dst_ref=seven, send_sem=send.at[i], recv_sem=recv.at[i],
                                         device_id=(x, y, c), device_id_type=MESH).wait()

    def full_shape(s, ax):
        shp = list(s.shape)
        shp[ax] *= N_DEV
        return jax.ShapeDtypeStruct(tuple(shp), s.dtype)

    any_spec = pl.BlockSpec(memory_space=pl.ANY)
    return pl.pallas_call(
        body,
        name=name,
        in_specs=[any_spec] * n,
        out_specs=[any_spec] * n,
        out_shape=[full_shape(s, ax) for s, ax in zip(shards, axes)],
        scratch_shapes=[pltpu.SemaphoreType.DMA((n,)), pltpu.SemaphoreType.DMA((n,)), pltpu.SemaphoreType.DMA((n,))],
        compiler_params=pltpu.CompilerParams(has_side_effects=True),
    )(*shards)


HBM_SPEC = pl.BlockSpec(memory_space=pltpu.HBM)
SEM_SPEC = pl.BlockSpec(memory_space=pltpu.SEMAPHORE)
SPLIT_EFFECT = pltpu.SideEffectType.DATAFLOW_SIDE_EFFECTING


def _push_all(kind, src, dst, axis, size, send_sem, recv_sem, place):
    x, y, c, me = place
    for peer, pidx in _peers(x, y, c):
        if kind == "gather":
            s = d = _window(dst, axis, me * size, size)
        else:
            s, d = _window(src, axis, pidx * size, size), dst.at[me]
        pltpu.make_async_remote_copy(src_ref=s, dst_ref=d, send_sem=send_sem, recv_sem=recv_sem, device_id=peer,
                                     device_id_type=MESH).start()


def _drain_all(kind, dst, axis, size, send_sem, recv_sem, place):
    x, y, c, _ = place
    seven = _window(dst, axis, 0, (N_DEV - 1) * size) if kind == "gather" else dst.at[pl.ds(0, N_DEV - 1)]
    pltpu.make_async_remote_copy(src_ref=seven, dst_ref=seven, send_sem=send_sem, recv_sem=recv_sem,
                                 device_id=(x, y, c), device_id_type=MESH).wait()


def _own_block_placed(src, axis, size, me):
    own = lax.dynamic_slice_in_dim(src, me * size, size, axis)
    return lax.dynamic_update_slice_in_dim(lax.empty((N_DEV,) + own.shape, src.dtype), own[None], me, 0)


def _split_start(name, kind, srcs, lands, axes, sizes, after=()):
    n, ns, na = len(lands), len(srcs), len(after)

    def body(*refs):
        src_refs, land_refs = refs[:ns], refs[ns:ns + n]
        send, recv = refs[ns + n + na], refs[ns + n + na + 1]
        token = refs[-1]
        place = _my_place()
        for k in range(n):
            _push_all(kind, src_refs[k] if ns else None, land_refs[k], axes[k], sizes[k], send.at[k], recv.at[k], place)
        token[...] = jnp.zeros_like(token)

    hbm = lambda a: pltpu.HBM(a.shape, a.dtype)
    res = pl.pallas_call(
        body,
        name=name,
        out_shape=(pltpu.SemaphoreType.DMA((n,)), pltpu.SemaphoreType.DMA((n,)), *[hbm(a) for a in srcs],
                   *[hbm(a) for a in lands], jax.ShapeDtypeStruct((SUBLANES, LANES), F32)),
        in_specs=[HBM_SPEC] * (ns + n) + [pl.BlockSpec(memory_space=pl.ANY)] * na,
        out_specs=(SEM_SPEC, SEM_SPEC, *[HBM_SPEC] * (ns + n), pl.BlockSpec(memory_space=pltpu.VMEM)),
        input_output_aliases={k: 2 + k for k in range(ns + n)},
        compiler_params=pltpu.CompilerParams(has_side_effects=SPLIT_EFFECT),
    )(*[pltpu.with_memory_space_constraint(a, pltpu.HBM) for a in (*srcs, *lands)], *after)
    return res[0], res[1], list(res[2:2 + ns]), list(res[2 + ns:2 + ns + n]), res[-1]


def _split_wait(name, kind, handle, axes, sizes, after):
    send, recv, srcs, lands, _ = handle
    n, ns = len(lands), len(srcs)

    def body(*refs):
        land_refs = refs[ns:ns + n]
        send_ref, recv_ref = refs[ns + n], refs[ns + n + 1]
        place = _my_place()
        for k in range(n):
            _drain_all(kind, land_refs[k], axes[k], sizes[k], send_ref.at[k], recv_ref.at[k], place)

    hbm = lambda a: pltpu.HBM(a.shape, a.dtype)
    res = pl.pallas_call(
        body,
        name=name,
        out_shape=tuple(hbm(a) for a in (*srcs, *lands)),
        in_specs=[HBM_SPEC] * (ns + n) + [SEM_SPEC, SEM_SPEC, pl.BlockSpec(memory_space=pl.ANY)],
        out_specs=tuple([HBM_SPEC] * (ns + n)),
        input_output_aliases={k: k for k in range(ns + n)},
        compiler_params=pltpu.CompilerParams(has_side_effects=SPLIT_EFFECT),
    )(*srcs, *lands, send, recv, after)
    return list(res[ns:])


def _cast_into_window(name, a, l, axis, me1):
    shp = a.shape[1:]
    cast = lambda me_ref, a_ref, o_ref: o_ref.__setitem__(Ellipsis, a_ref[...].astype(BF16))
    if len(shp) == 3:
        assert axis == 1
        G, r, c = shp
        full = (G, r * N_DEV, c)
        grid = (G,)
        in_spec = pl.BlockSpec((None, None, r, c), lambda g, me: (l, g, 0, 0))
        out_spec = pl.BlockSpec((None, r, c), lambda g, me: (g, me[0], 0))
    else:
        r, c = shp
        tr = _tile(r, 512)
        nb = r // tr
        grid = (nb,)
        in_spec = pl.BlockSpec((None, tr, c), lambda i, me: (l, i, 0))
        if axis == 0:
            full = (r * N_DEV, c)
            out_spec = pl.BlockSpec((tr, c), lambda i, me: (me[0] * nb + i, 0))
        else:
            full = (r, c * N_DEV)
            out_spec = pl.BlockSpec((tr, c), lambda i, me: (i, me[0]))
    return pl.pallas_call(
        cast,
        name=name,
        grid_spec=pltpu.PrefetchScalarGridSpec(num_scalar_prefetch=1, grid=grid, in_specs=[in_spec], out_specs=out_spec),
        out_shape=jax.ShapeDtypeStruct(full, BF16),
        compiler_params=_cparams("arbitrary"),
    )(me1, a)


def _adamw_math(w, g, m, v):
    m = ADAM_B1 * m + (1.0 - ADAM_B1) * g
    v = ADAM_B2 * v + (1.0 - ADAM_B2) * jnp.square(g)
    m_hat = m / (1.0 - ADAM_B1 ** ADAM_STEP)
    v_hat = v / (1.0 - ADAM_B2 ** ADAM_STEP)
    delta = -ADAM_LR * (m_hat / (jnp.sqrt(v_hat) + ADAM_EPS) + ADAM_WD * w)
    return delta, m, v


def _sum_slots(buf_ref):
    g = buf_ref[0].astype(F32)
    for s in range(1, N_DEV):
        g = g + buf_ref[s].astype(F32)
    return g


def _adamw_layer(name, buf, w, m, v, l, prev):
    shape = w.shape
    L, C = shape[0], shape[-1]
    Rr = math.prod(shape[1:-1])
    buf3 = buf.reshape(N_DEV, Rr, C)
    w3, m3, v3 = (t.reshape(L, Rr, C) for t in (w, m, v))
    tr = _tile(Rr, 2 * LANES) if Rr % LANES == 0 else Rr
    n_prev = 0 if prev is None else 4

    def body(buf_ref, w_ref, m_ref, v_ref, *rest):
        g_out, d_out, m_out, v_out = rest[n_prev:]
        g = _sum_slots(buf_ref)
        d, mm, vv = _adamw_math(w_ref[...], g, m_ref[...], v_ref[...])
        g_out[...] = g
        d_out[...] = d
        m_out[...] = mm
        v_out[...] = vv

    spec = pl.BlockSpec((None, tr, C), lambda r: (l, r, 0))
    outs = pl.pallas_call(
        body,
        name=name,
        grid=(Rr // tr,),
        in_specs=[pl.BlockSpec((N_DEV, tr, C), lambda r: (0, r, 0)), spec, spec, spec]
        + [pl.BlockSpec(memory_space=pl.ANY)] * n_prev,
        out_specs=[spec] * 4,
        out_shape=[jax.ShapeDtypeStruct((L, Rr, C), F32)] * 4,
        input_output_aliases={4 + k: k for k in range(n_prev)},
        compiler_params=_cparams("parallel"),
    )(buf3, w3, m3, v3, *(prev or ()))
    return list(outs)


def _sum8(name, buf):
    R = buf.shape[1]

    def body(buf_ref, o_ref):
        o_ref[...] = _sum_slots(buf_ref)

    return pl.pallas_call(
        body,
        name=name,
        in_specs=[pl.BlockSpec(buf.shape, lambda: (0, 0, 0))],
        out_specs=pl.BlockSpec((R, LANES), lambda: (0, 0)),
        out_shape=jax.ShapeDtypeStruct((R, LANES), F32),
        compiler_params=_cparams(),
    )(buf)


def _adamw_packed(name, w, g, m, v):
    R = w.shape[0]
    tr = _row_tile(R, 512)

    def body(w_ref, g_ref, m_ref, v_ref, d_out, m_out, v_out):
        d, mm, vv = _adamw_math(w_ref[...], g_ref[...], m_ref[...], v_ref[...])
        d_out[...] = d
        m_out[...] = mm
        v_out[...] = vv

    spec = pl.BlockSpec((tr, LANES), lambda r: (r, 0))
    return pl.pallas_call(
        body,
        name=name,
        grid=(R // tr,),
        in_specs=[spec] * 4,
        out_specs=[spec] * 3,
        out_shape=[jax.ShapeDtypeStruct((R, LANES), F32)] * 3,
        compiler_params=_cparams("parallel"),
    )(w, g, m, v)


def _pack(arrs, pad_rows_to=SUBLANES):
    parts = []
    for a in arrs:
        flat = a.reshape(-1)
        per = LANES * pad_rows_to
        padded = -(-flat.shape[0] // per) * per
        if padded != flat.shape[0]:
            flat = jnp.pad(flat, (0, padded - flat.shape[0]))
        parts.append(flat.reshape(-1, LANES))
    return jnp.concatenate(parts, axis=0)


def _unpack(packed, shapes, pad_rows_to=SUBLANES):
    out = []
    r = 0
    for shp in shapes:
        nel = math.prod(shp)
        per = LANES * pad_rows_to
        rows = -(-nel // per) * pad_rows_to
        out.append(packed[r:r + rows].reshape(-1)[:nel].reshape(shp))
        r += rows
    return out


BIG = ("lru_w_in", "lru_w_out", "pool_w_in", "pool_w_grp", "pool_w_out", "mlp_w1", "mlp_w2", "ple_w", "ple_gate_w")
BIG_AXIS = {"lru_w_in": 2, "lru_w_out": 1, "pool_w_in": 1, "pool_w_grp": 2, "pool_w_out": 1, "mlp_w1": 2,
            "mlp_w2": 1, "ple_w": 2, "ple_gate_w": 1}
SMALL_SHARDED = ("lru_conv_w", "pool_b_grp", "pool_scale")
REPLICATED = ("lru_conv_b", "lru_wa", "lru_ba", "lru_wx", "lru_bx", "lru_lambda", "ln_mix_g", "ln_mix_b",
              "ln_mlp_g", "ln_mlp_b", "ple_gate_b")
WEIGHTS = ("lru_w_in", "lru_conv_w", "lru_conv_b", "lru_wa", "lru_ba", "lru_wx", "lru_bx", "lru_lambda", "lru_w_out",
           "pool_w_in", "pool_w_grp", "pool_b_grp", "pool_scale", "pool_w_out", "ln_mix_g", "ln_mix_b", "mlp_w1",
           "mlp_w2", "ln_mlp_g", "ln_mlp_b", "ple_w", "ple_gate_w", "ple_gate_b")
INPUTS = ("x", "p") + WEIGHTS + ("loss_target",) + tuple("m_" + n for n in WEIGHTS) + tuple("v_" + n for n in WEIGHTS)


def _gather_last_axis(packed_full, shard_shape):
    nel = math.prod(shard_shape)
    blocks = packed_full.reshape(N_DEV, -1)[:, :nel].reshape((N_DEV,) + tuple(shard_shape))
    return jnp.concatenate([blocks[d] for d in range(N_DEV)], axis=-1)


def kernel(x, p, lru_w_in, lru_conv_w, lru_conv_b, lru_wa, lru_ba, lru_wx, lru_bx, lru_lambda, lru_w_out, pool_w_in, pool_w_grp, pool_b_grp, pool_scale, pool_w_out, ln_mix_g, ln_mix_b, mlp_w1, mlp_w2, ln_mlp_g, ln_mlp_b, ple_w, ple_gate_w, ple_gate_b, loss_target, m_lru_w_in, m_lru_conv_w, m_lru_conv_b, m_lru_wa, m_lru_ba, m_lru_wx, m_lru_bx, m_lru_lambda, m_lru_w_out, m_pool_w_in, m_pool_w_grp, m_pool_b_grp, m_pool_scale, m_pool_w_out, m_ln_mix_g, m_ln_mix_b, m_mlp_w1, m_mlp_w2, m_ln_mlp_g, m_ln_mlp_b, m_ple_w, m_ple_gate_w, m_ple_gate_b, v_lru_w_in, v_lru_conv_w, v_lru_conv_b, v_lru_wa, v_lru_ba, v_lru_wx, v_lru_bx, v_lru_lambda, v_lru_w_out, v_pool_w_in, v_pool_w_grp, v_pool_b_grp, v_pool_scale, v_pool_w_out, v_ln_mix_g, v_ln_mix_b, v_mlp_w1, v_mlp_w2, v_ln_mlp_g, v_ln_mlp_b, v_ple_w, v_ple_gate_w, v_ple_gate_b):
    A = dict(zip(INPUTS, (x, p, lru_w_in, lru_conv_w, lru_conv_b, lru_wa, lru_ba, lru_wx, lru_bx, lru_lambda, lru_w_out, pool_w_in, pool_w_grp, pool_b_grp, pool_scale, pool_w_out, ln_mix_g, ln_mix_b, mlp_w1, mlp_w2, ln_mlp_g, ln_mlp_b, ple_w, ple_gate_w, ple_gate_b, loss_target, m_lru_w_in, m_lru_conv_w, m_lru_conv_b, m_lru_wa, m_lru_ba, m_lru_wx, m_lru_bx, m_lru_lambda, m_lru_w_out, m_pool_w_in, m_pool_w_grp, m_pool_b_grp, m_pool_scale, m_pool_w_out, m_ln_mix_g, m_ln_mix_b, m_mlp_w1, m_mlp_w2, m_ln_mlp_g, m_ln_mlp_b, m_ple_w, m_ple_gate_w, m_ple_gate_b, v_lru_w_in, v_lru_conv_w, v_lru_conv_b, v_lru_wa, v_lru_ba, v_lru_wx, v_lru_bx, v_lru_lambda, v_lru_w_out, v_pool_w_in, v_pool_w_grp, v_pool_b_grp, v_pool_scale, v_pool_w_out, v_ln_mix_g, v_ln_mix_b, v_mlp_w1, v_mlp_w2, v_ln_mlp_g, v_ln_mlp_b, v_ple_w, v_ple_gate_w, v_ple_gate_b)))
    depth = ln_mix_g.shape[0]
    alpha = (2 * depth) ** 0.25
    S, D = x.shape[1], x.shape[2]
    xs = x.reshape(S, D)
    tgt = loss_target.reshape(S, D)
    p3 = p.reshape(depth, S, p.shape[-1])
    me = 4 * lax.axis_index("x") + 2 * lax.axis_index("y") + lax.axis_index("c")

    def layer_weights(i):
        s = i // 2
        mixer = ("lru_w_in", "lru_w_out") if i % 2 == 0 else ("pool_w_in", "pool_w_grp", "pool_w_out")
        return [(n, s) for n in mixer] + [(n, i) for n in ("mlp_w1", "mlp_w2", "ple_w", "ple_gate_w")]

    def axis_of(key):
        return 0 if key[0] == "small" else BIG_AXIS[key[0]] - 1

    def start_gather(tag, keys, after):
        axes = [axis_of(k) for k in keys]
        lands = [land[k] for k in keys]
        sizes = [a.shape[ax] // N_DEV for a, ax in zip(lands, axes)]
        return keys, _split_start(f"gather_{tag}_start", "gather", [], lands, axes, sizes, after=after), axes, sizes

    def finish_gather(tag, pending, after):
        keys, handle, axes, sizes = pending
        for (n, l), full in zip(keys, _split_wait(f"gather_{tag}_wait", "gather", handle, axes, sizes, after)):
            W[n][l] = full

    def start_exchange(tag, keys, arrs, after):
        axes = [axis_of(k) for k in keys]
        sizes = [a.shape[ax] // N_DEV for a, ax in zip(arrs, axes)]
        lands = [_own_block_placed(a, ax, sz, me) for a, ax, sz in zip(arrs, axes, sizes)]
        return keys, _split_start(f"exchange_{tag}_start", "scatter", arrs, lands, axes, sizes, after=after), axes, sizes

    def finish_exchange(tag, pending, after):
        keys, handle, axes, sizes = pending
        partial.update(zip(keys, _split_wait(f"exchange_{tag}_wait", "scatter", handle, axes, sizes, after)))

    me1 = jnp.reshape(me, (1,)).astype(jnp.int32)
    land = {k: _cast_into_window(f"cast_{k[0]}_{k[1]}", A[k[0]], k[1], axis_of(k), me1)
            for i in range(depth) for k in layer_weights(i)}
    W = {n: [None] * A[n].shape[0] for n in BIG}
    small_shard_shapes = [A[n].shape for n in SMALL_SHARDED]
    gathered = _all_gather("gather_small_params", [_pack([A[n] for n in SMALL_SHARDED])], [0])
    first_key = layer_weights(0)[:1]
    gather_first = start_gather("first", first_key, (gathered[0],))
    gather_rest0 = start_gather("rest0", layer_weights(0)[1:], (gather_first[1][4],))
    gather_pending = {}
    token = gather_rest0[1][4]
    for i in range(1, depth):
        gather_pending[i] = start_gather(f"l{i}", layer_weights(i), (token,))
        token = gather_pending[i][1][4]
    finish_gather("first", gather_first, token)
    small_full = gathered[0].reshape(N_DEV, -1, LANES)
    r = 0
    for n, shp in zip(SMALL_SHARDED, small_shard_shapes):
        rows = -(-math.prod(shp) // (LANES * SUBLANES)) * SUBLANES
        W[n] = _gather_last_axis(small_full[:, r:r + rows], shp)
        r += rows
    wa_b, wx_b = _to_bf16("cast_gates", [lru_wa, lru_wx])
    n_lru = lru_w_in.shape[0]
    lru_par = [jnp.concatenate([W["lru_conv_w"][s], lru_conv_b[s][None], lru_ba[s][None], lru_bx[s][None],
                                lru_lambda[s][None]], axis=0) for s in range(n_lru)]
    pool_par = [jnp.stack([W["pool_b_grp"][s], W["pool_scale"][s]], axis=0) for s in range(pool_w_in.shape[0])]

    saved = []
    h_in = xs
    for i in range(depth):
        s = i // 2
        sv = {"x0": h_in}
        if i > 0:
            finish_gather(f"l{i}", gather_pending[i], h_in)
        if i % 2 == 0:
            sv["proj"] = _mm(f"l{i}_lru_in", h_in, W["lru_w_in"], "nn", [F32], b_lead=s)
            sv["gh"], sv["h"] = _lru_fwd(f"l{i}_lru_core", sv["proj"], lru_par[s], wa_b[s], wx_b[s])
            if i == 0:
                finish_gather("rest0", gather_rest0, sv["gh"])
            sv["mix"] = _mm(f"l{i}_lru_out", sv["gh"], W["lru_w_out"], "nn", [F32], b_lead=s)
        else:
            sv["u"] = _mm(f"l{i}_pool_in", h_in, W["pool_w_in"], "nn", [F32], b_lead=s)
            sv["zs"] = _pool_fwd(f"l{i}_pool_core", sv["u"], W["pool_w_grp"][s], pool_par[s])
            sv["mix"] = _mm(f"l{i}_pool_out", sv["zs"], W["pool_w_out"], "nn", [F32], b_lead=s)
        sv["x1"] = _ln_fwd(f"l{i}_ln_mix", alpha, h_in, sv["mix"], ln_mix_g[i][None], ln_mix_b[i][None])
        sv["hpre"], sv["hact"] = _mm(f"l{i}_mlp_up", sv["x1"], W["mlp_w1"], "nn", [BF16, BF16], b_lead=i,
                                     epi=lambda acc: (acc, jnp.square(jnp.maximum(acc, 0.0))))
        sv["mo"] = _mm(f"l{i}_mlp_down", sv["hact"], W["mlp_w2"], "nn", [F32], b_lead=i)
        sv["x2"] = _ln_fwd(f"l{i}_ln_mlp", alpha, sv["x1"], sv["mo"], ln_mlp_g[i][None], ln_mlp_b[i][None])
        sv["pp"] = _mm(f"l{i}_ple_up", p3, W["ple_w"], "nn", [F32], a_lead=i, b_lead=i)

        def ple_epi(acc, bg, x2t, ppt):
            gpre = acc + bg
            return x2t + ppt * jax.nn.sigmoid(gpre), gpre

        h_in, sv["gpre"] = _mm(f"l{i}_ple_gate", sv["x2"], W["ple_gate_w"], "nn", [F32, F32], b_lead=i,
                               epi=ple_epi, extras=[ple_gate_b[i][None], sv["x2"], sv["pp"]])
        saved.append(sv)

    dx, sq = _loss_and_grad("loss", h_in, tgt)
    loss = lax.psum(0.5 * sq[0, 0] / D, ("x", "y", "c"))

    dW = {n: [None] * A[n].shape[0] for n in BIG}
    dsmall = {n: [None] * A[n].shape[0] for n in REPLICATED + SMALL_SHARDED}
    small_names = REPLICATED + SMALL_SHARDED
    partial = {}
    exchange_pending = {}
    exchange_token = ()
    for i in reversed(range(depth)):
        s = i // 2
        sv = saved[i]
        dpp, dgpre, dbg = _ple_bwd(f"l{i}_ple_bwd", dx, sv["gpre"], sv["pp"], after=exchange_token)
        dsmall["ple_gate_b"][i] = dbg[0]
        dW["ple_w"][i] = _mm(f"l{i}_d_ple_w", p3, dpp, "tn", [BF16], a_lead=i)
        dW["ple_gate_w"][i] = _mm(f"l{i}_d_ple_gate_w", sv["x2"], dgpre, "tn", [BF16])
        dx2 = _mm(f"l{i}_d_x2", dgpre, W["ple_gate_w"], "nt", [F32], b_lead=i, extras=[dx],
                  epi=lambda acc, d: (acc + d,))
        dz2, dg, db = _ln_bwd(f"l{i}_ln_mlp_bwd", alpha, dx2, sv["x1"], sv["mo"], ln_mlp_g[i][None])
        dsmall["ln_mlp_g"][i], dsmall["ln_mlp_b"][i] = dg[0], db[0]
        dhpre = _mm(f"l{i}_d_hpre", dz2, W["mlp_w2"], "nt", [BF16], b_lead=i, extras=[sv["hpre"]],
                    epi=lambda acc, hp: (acc * (2.0 * jnp.maximum(hp.astype(F32), 0.0)),))
        dW["mlp_w2"][i] = _mm(f"l{i}_d_mlp_w2", sv["hact"], dz2, "tn", [BF16])
        dW["mlp_w1"][i] = _mm(f"l{i}_d_mlp_w1", sv["x1"], dhpre, "tn", [BF16])
        mlp_after = ()
        if i == 0:
            early = [(n, 0) for n in ("ple_w", "ple_gate_w", "mlp_w2", "mlp_w1")]
            exchange_early0 = start_exchange("early0", early, [dW[n][l] for n, l in early], ())
            mlp_after = (exchange_early0[1][4],)
        dx1 = _mm(f"l{i}_d_x1", dhpre, W["mlp_w1"], "nt", [F32], b_lead=i, extras=[dz2],
                  epi=lambda acc, d: (acc + alpha * d,), after=mlp_after)
        dz1, dg, db = _ln_bwd(f"l{i}_ln_mix_bwd", alpha, dx1, sv["x0"], sv["mix"], ln_mix_g[i][None])
        dsmall["ln_mix_g"][i], dsmall["ln_mix_b"][i] = dg[0], db[0]
        if i % 2 == 0:
            dW["lru_w_out"][s] = _mm(f"l{i}_d_lru_w_out", sv["gh"], dz1, "tn", [BF16])
            dgh = _mm(f"l{i}_d_gh", dz1, W["lru_w_out"], "nt", [F32], b_lead=s)
            dup, dy, dwa, dwx, dpar = _lru_bwd(f"l{i}_lru_core_bwd", sv["proj"], sv["h"], dgh, lru_par[s],
                                               wa_b[s], wx_b[s])
            dsmall["lru_wa"][s], dsmall["lru_wx"][s] = dwa, dwx
            dsmall["lru_conv_w"][s] = dpar[0:4]
            for k, n in enumerate(("lru_conv_b", "lru_ba", "lru_bx", "lru_lambda")):
                dsmall[n][s] = dpar[4 + k]
            dmix_in = jnp.concatenate([dup, dy], axis=1)
            win = "lru_w_in"
        else:
            dW["pool_w_out"][s] = _mm(f"l{i}_d_pool_w_out", sv["zs"], dz1, "tn", [BF16])
            dzs = _mm(f"l{i}_d_zs", dz1, W["pool_w_out"], "nt", [F32], b_lead=s)
            dmix_in, dW["pool_w_grp"][s], dpar = _pool_bwd(f"l{i}_pool_core_bwd", sv["u"], dzs, W["pool_w_grp"][s],
                                                          pool_par[s])
            dsmall["pool_b_grp"][s], dsmall["pool_scale"][s] = dpar[0], dpar[1]
            win = "pool_w_in"
        dW[win][s] = _mm(f"l{i}_d_{win}", sv["x0"], dmix_in, "tn", [BF16])
        x0_after = ()
        if i > 0:
            keys = layer_weights(i)
            exchange_pending[i] = start_exchange(f"l{i}", keys, [dW[n][l] for n, l in keys], ())
            exchange_token = (exchange_pending[i][1][4],)
        else:
            small_grads = [jnp.stack(dsmall[n]) for n in small_names]
            small_shapes = [g.shape for g in small_grads]
            packed_g = _pack(small_grads)
            assert packed_g.shape[0] % (N_DEV * SUBLANES) == 0, packed_g.shape
            late = [("lru_w_out", 0), ("lru_w_in", 0), ("small", 0)]
            exchange_late0 = start_exchange("late0", late, [dW["lru_w_out"][0], dW["lru_w_in"][0], packed_g], ())
            x0_after = (exchange_late0[1][4],)
        dx = _mm(f"l{i}_d_x0", dmix_in, W[win], "nt", [F32], b_lead=s, extras=[dz1],
                 epi=lambda acc, d: (acc + alpha * d,), after=x0_after)
    grad_x = dx.reshape(x.shape)

    for i in range(1, depth):
        finish_exchange(f"l{i}", exchange_pending[i], x0_after[0])
    stacked = {n: None for n in BIG}
    layer0 = layer_weights(0)

    def adamw(n, l):
        stacked[n] = _adamw_layer(f"adamw_{n}_{l}", partial[(n, l)], A[n], A["m_" + n], A["v_" + n], l, stacked[n])

    for n in BIG:
        for l in reversed(range(A[n].shape[0])):
            if (n, l) not in layer0:
                adamw(n, l)
    finish_exchange("early0", exchange_early0, stacked["ple_gate_w"][0])
    finish_exchange("late0", exchange_late0, stacked["ple_gate_w"][0])
    for n, l in layer0:
        adamw(n, l)
    outs = {n: [o.reshape(A[n].shape) for o in stacked[n]] for n in BIG}
    red = _sum8("sum_small", partial[("small", 0)])
    red_full = _all_gather("gather_small", [red], [0])[0]
    small_g = dict(zip(small_names, _unpack(red_full, small_shapes)))
    for n in SMALL_SHARDED:
        width = A[n].shape[-1]
        small_g[n] = lax.dynamic_slice_in_dim(small_g[n], me * width, width, axis=small_g[n].ndim - 1)
    pk = lambda pre: _pack([A[pre + n] for n in small_names])
    d_p, m_p, v_p = _adamw_packed("adamw_small", pk(""), _pack([small_g[n] for n in small_names]), pk("m_"), pk("v_"))
    shapes = [A[n].shape for n in small_names]
    for n, d_, m_, v_ in zip(small_names, _unpack(d_p, shapes), _unpack(m_p, shapes), _unpack(v_p, shapes)):
        outs[n] = [small_g[n], d_, m_, v_]

    return (loss, grad_x, *[outs[n][0] for n in WEIGHTS], *[outs[n][1] for n in WEIGHTS],
            *[outs[n][2] for n in WEIGHTS], *[outs[n][3] for n in WEIGHTS])
```

```python
import functools
import math

import jax
import jax.numpy as jnp
from jax import lax
from jax.experimental import pallas as pl
from jax.experimental.pallas import tpu as pltpu

F32 = jnp.float32
BF16 = jnp.bfloat16
MESH = pl.DeviceIdType.MESH
N_DEV = 8
LANES = 128
SUBLANES = 8

LN_EPS = 1e-5
LRU_C = 8.0
CONV_WIDTH = 4
POOL_HALO = 16
ADAM_LR = 0.001
ADAM_B1 = 0.9
ADAM_B2 = 0.999
ADAM_EPS = 1e-08
ADAM_WD = 0.01
ADAM_STEP = 10

VMEM_LIMIT = 48 * 1024 * 1024
SEQ_CHUNK = 256
MM_TK = 4096
MM_TK_TOKENS = 4096
GELU_C0 = math.sqrt(2.0 / math.pi)
GELU_C1 = 0.044715


def _cparams(*sem):
    return pltpu.CompilerParams(dimension_semantics=tuple(sem) if sem else None, vmem_limit_bytes=VMEM_LIMIT)


def _tile(n, pref):
    if n <= pref:
        return n
    t = pref - pref % LANES
    while t >= LANES:
        if n % t == 0:
            return t
        t -= LANES
    return n


def _row_tile(n, pref):
    if n <= pref:
        return n
    t = pref - pref % SUBLANES
    while t >= SUBLANES:
        if n % t == 0:
            return t
        t -= SUBLANES
    return n


def _mm(name, a, b, mode, out_dtypes, epi=None, extras=(), a_lead=None, b_lead=None, tm=1024, tn=512, tk=None,
        after=()):
    if isinstance(b, (list, tuple)):
        b, b_lead = b[b_lead], None
    a2 = a.shape[-2:]
    b2 = b.shape[-2:]
    if mode == "nn":
        (M, K), N = a2, b2[1]
        assert b2[0] == K
    elif mode == "nt":
        (M, K), N = a2, b2[0]
        assert b2[1] == K
    else:
        (K, M), N = a2, b2[1]
        assert b2[0] == K
    if tk is None:
        tk = MM_TK_TOKENS if mode == "tn" else MM_TK
    tm, tn, tk = _tile(M, tm), _tile(N, tn), _tile(K, tk)
    nk = K // tk
    n_extra = len(extras)
    n_out = len(out_dtypes)

    def lead(shape, idx, which):
        if which is None:
            return pl.BlockSpec(shape, idx)
        return pl.BlockSpec((None,) + shape, lambda i, j, k: (which,) + idx(i, j, k))

    if mode == "nn":
        a_spec = lead((tm, tk), lambda i, j, k: (i, k), a_lead)
        b_spec = lead((tk, tn), lambda i, j, k: (k, j), b_lead)
        dims = (((1,), (0,)), ((), ()))
    elif mode == "nt":
        a_spec = lead((tm, tk), lambda i, j, k: (i, k), a_lead)
        b_spec = lead((tn, tk), lambda i, j, k: (j, k), b_lead)
        dims = (((1,), (1,)), ((), ()))
    else:
        a_spec = lead((tk, tm), lambda i, j, k: (k, i), a_lead)
        b_spec = lead((tk, tn), lambda i, j, k: (k, j), b_lead)
        dims = (((0,), (0,)), ((), ()))
    e_specs = []
    for e in extras:
        if e.shape[0] == 1:
            e_specs.append(pl.BlockSpec((1, tn), lambda i, j, k: (0, j)))
        else:
            e_specs.append(pl.BlockSpec((tm, tn), lambda i, j, k: (i, j)))

    n_after = len(after)

    def body(a_ref, b_ref, *rest):
        e_refs = rest[:n_extra]
        rest = rest[:n_extra] + rest[n_extra + n_after:]
        o_refs = rest[n_extra:n_extra + n_out]
        part = lax.dot_general(a_ref[...].astype(BF16), b_ref[...].astype(BF16), dims, preferred_element_type=F32)

        def finish(r):
            res = (r,) if epi is None else epi(r, *[e[...] for e in e_refs])
            for o, v in zip(o_refs, res):
                o[...] = v.astype(o.dtype)

        if nk == 1:
            finish(part)
            return
        acc = rest[n_extra + n_out]
        k = pl.program_id(2)

        @pl.when(k == 0)
        def _():
            acc[...] = part

        @pl.when(jnp.logical_and(k > 0, k < nk - 1))
        def _():
            acc[...] += part

        @pl.when(k == nk - 1)
        def _():
            finish(acc[...] + part)

    outs = pl.pallas_call(
        body,
        name=name,
        grid=(M // tm, N // tn, nk),
        in_specs=[a_spec, b_spec] + e_specs + [pl.BlockSpec(memory_space=pl.ANY)] * n_after,
        out_specs=[pl.BlockSpec((tm, tn), lambda i, j, k: (i, j)) for _ in out_dtypes],
        out_shape=[jax.ShapeDtypeStruct((M, N), d) for d in out_dtypes],
        scratch_shapes=[pltpu.VMEM((tm, tn), F32)] if nk > 1 else [],
        compiler_params=_cparams("parallel", "parallel", "arbitrary"),
    )(a, b, *extras, *after)
    return outs[0] if n_out == 1 else tuple(outs)


def _rowwise(name, fn, tiled, params, outs, accs=(), tm=256, after=()):
    S = tiled[0].shape[0]
    tm = _tile(S, tm)
    nt, npar, no = len(tiled), len(params), len(outs)
    n_after = len(after)

    def body(*refs):
        t_refs = refs[:nt]
        p_refs = refs[nt:nt + npar]
        refs = refs[nt + npar + n_after:]
        o_refs = refs[:no]
        a_refs = refs[no:]
        res = fn(*[r[...] for r in t_refs], *[r[...] for r in p_refs])
        for o, v in zip(o_refs, res[:no]):
            o[...] = v.astype(o.dtype)
        first = pl.program_id(0) == 0
        for ar, v in zip(a_refs, res[no:]):
            @pl.when(first)
            def _(ar=ar, v=v):
                ar[...] = v

            @pl.when(jnp.logical_not(first))
            def _(ar=ar, v=v):
                ar[...] += v

    full = lambda p: pl.BlockSpec(p.shape, lambda i, nd=p.ndim: (0,) * nd)
    res = pl.pallas_call(
        body,
        name=name,
        grid=(S // tm,),
        in_specs=[pl.BlockSpec((tm, t.shape[1]), lambda i: (i, 0)) for t in tiled] + [full(p) for p in params]
        + [pl.BlockSpec(memory_space=pl.ANY)] * n_after,
        out_specs=[pl.BlockSpec((tm, c), lambda i: (i, 0)) for c, _ in outs]
        + [pl.BlockSpec(s, lambda i, nd=len(s): (0,) * nd) for s in accs],
        out_shape=[jax.ShapeDtypeStruct((S, c), d) for c, d in outs] + [jax.ShapeDtypeStruct(s, F32) for s in accs],
        compiler_params=_cparams("arbitrary"),
    )(*tiled, *params, *after)
    return res


def _ln_stats(z):
    mu = jnp.mean(z, axis=-1, keepdims=True)
    zc = z - mu
    var = jnp.mean(zc * zc, axis=-1, keepdims=True)
    return zc, lax.rsqrt(var + LN_EPS)


def _ln_fwd(name, alpha, xp, m, g, b):
    def fn(xp, m, g, b):
        zc, rstd = _ln_stats(alpha * xp + m)
        y = zc * rstd * g + b
        return y, y

    d = xp.shape[1]
    return _rowwise(name, fn, [xp, m], [g, b], [(d, F32), (d, BF16)])


def _ln_bwd(name, alpha, dy, xp, m, g):
    def fn(dy, xp, m, g):
        zc, rstd = _ln_stats(alpha * xp + m)
        xhat = zc * rstd
        dxh = dy * g
        m1 = jnp.mean(dxh, axis=-1, keepdims=True)
        m2 = jnp.mean(dxh * xhat, axis=-1, keepdims=True)
        dz = rstd * (dxh - m1 - xhat * m2)
        return dz, dz, jnp.sum(dy * xhat, axis=0, keepdims=True), jnp.sum(dy, axis=0, keepdims=True)

    d = xp.shape[1]
    return _rowwise(name, fn, [dy, xp, m], [g], [(d, F32), (d, BF16)], accs=[(1, d), (1, d)])


def _loss_and_grad(name, y, target):
    d = y.shape[1]

    def fn(y, t):
        err = y - t
        sq = jnp.sum(jnp.sum(err * err, axis=0, keepdims=True), axis=1, keepdims=True)
        return err * (1.0 / d), jnp.broadcast_to(sq, (1, LANES))

    return _rowwise(name, fn, [y, target], [], [(d, F32)], accs=[(1, LANES)])


def _ple_bwd(name, dx3, gpre, pp, after=()):
    def fn(dx3, gpre, pp):
        gate = jax.nn.sigmoid(gpre)
        dgpre = dx3 * pp * gate * (1.0 - gate)
        return dx3 * gate, dgpre, jnp.sum(dgpre, axis=0, keepdims=True)

    d = dx3.shape[1]
    return _rowwise(name, fn, [dx3, gpre, pp], [], [(d, BF16), (d, BF16)], accs=[(1, d)], after=after)


def _rows(shape):
    return lax.broadcasted_iota(jnp.int32, shape, 0)


def _gelu(y):
    t = jnp.tanh(GELU_C0 * (y + GELU_C1 * y * y * y))
    return 0.5 * y * (1.0 + t), t


def _gelu_grad(y, t):
    return 0.5 * (1.0 + t) + 0.5 * y * (1.0 - t * t) * GELU_C0 * (1.0 + 3.0 * GELU_C1 * y * y)


def _neg_expm1(x):
    series = -x * (1.0 + x * (0.5 + x * (1.0 / 6.0 + x * (1.0 / 24.0))))
    return jnp.where(x > -0.02, series, 1.0 - jnp.exp(x))


def _softplus(x):
    return jnp.maximum(x, 0.0) + jnp.log(1.0 + jnp.exp(-jnp.abs(x)))


def _conv_fwd(xs, cw, cb):
    n = xs.shape[0]
    u = cw[3:4] * xs
    for k in (1, 2, 3):
        u = u + cw[3 - k:4 - k] * pltpu.roll(xs, k, 0)
    del n
    return u[SUBLANES:] + cb


def _lru_gates(u, wa, wx, ba, bx, sp, grow):
    ub = u.astype(BF16)
    r = jax.nn.sigmoid(jnp.dot(ub, wa, preferred_element_type=F32) + ba)
    ig = jax.nn.sigmoid(jnp.dot(ub, wx, preferred_element_type=F32) + bx)
    log_a = (-LRU_C) * r * sp
    a = jnp.exp(log_a)
    mult = jnp.sqrt(_neg_expm1(2.0 * log_a))
    mult = jnp.where(grow == 0, 1.0, mult)
    return ub, r, ig, a, mult


def _scan8_fwd(a, b):
    row = _rows(a.shape)
    for k in (1, 2, 4):
        m = row >= k
        b = jnp.where(m, a * pltpu.roll(b, k, 0) + b, b)
        a = jnp.where(m, a * pltpu.roll(a, k, 0), a)
    return a, b


def _scan8_bwd(c, d):
    row = _rows(c.shape)
    for k in (1, 2, 4):
        m = row < SUBLANES - k
        d = jnp.where(m, c * pltpu.roll(d, SUBLANES - k, 0) + d, d)
        c = jnp.where(m, c * pltpu.roll(c, SUBLANES - k, 0), c)
    return c, d


def _pad_copy(dst, src, front, back):
    s, c = src.shape
    if front:
        dst[pl.ds(0, front), :] = jnp.zeros((front, c), dst.dtype)
    if back:
        dst[pl.ds(front + s, back), :] = jnp.zeros((back, c), dst.dtype)
    dst[pl.ds(front, s), :] = src[...].astype(dst.dtype)


def _lru_fwd(name, proj, par, wa, wx):
    S = proj.shape[0]
    R = proj.shape[1] // 2
    H = R // LANES
    ch = _tile(S, SEQ_CHUNK)
    nch = S // ch
    H8 = SUBLANES

    def body(up_ref, y_ref, par_ref, wa_ref, wx_ref, gh_ref, h_ref, up_pad):
        _pad_copy(up_pad, up_ref, H8, 0)
        par = par_ref[...]
        cw, cb, ba, bx = par[0:4], par[4:5], par[5:6], par[6:7]
        sp = _softplus(-par[7:8])
        wa_m, wx_m = wa_ref[...], wx_ref[...]

        def chunk(ci, carry):
            r0 = pl.multiple_of(ci * ch, ch)
            xs = up_pad[pl.ds(r0, ch + H8), :]
            u = _conv_fwd(xs, cw, cb)
            grow = _rows(u.shape) + r0
            _, _, ig, a, mult = _lru_gates(u, wa_m, wx_m, ba, bx, sp, grow)
            bt = mult * (ig * u)
            hs = []
            for j in range(ch // H8):
                aa, bb = _scan8_fwd(a[j * H8:(j + 1) * H8], bt[j * H8:(j + 1) * H8])
                hj = bb + aa * carry
                carry = hj[H8 - 1:H8]
                hs.append(hj)
            h = jnp.concatenate(hs, axis=0)
            h_ref[pl.ds(r0, ch), :] = h
            gy, _ = _gelu(y_ref[pl.ds(r0, ch), :])
            gh_ref[pl.ds(r0, ch), :] = (h * gy).astype(gh_ref.dtype)
            return carry

        lax.fori_loop(0, nch, chunk, jnp.zeros((1, LANES), F32))

    col = lambda off: pl.BlockSpec((S, LANES), lambda h: (0, h + off))
    return pl.pallas_call(
        body,
        name=name,
        grid=(H,),
        in_specs=[col(0), col(H), pl.BlockSpec((8, LANES), lambda h: (0, h)),
                  pl.BlockSpec((None, LANES, LANES), lambda h: (h, 0, 0)),
                  pl.BlockSpec((None, LANES, LANES), lambda h: (h, 0, 0))],
        out_specs=[col(0), col(0)],
        out_shape=[jax.ShapeDtypeStruct((S, R), BF16), jax.ShapeDtypeStruct((S, R), F32)],
        scratch_shapes=[pltpu.VMEM((S + H8, LANES), F32)],
        compiler_params=_cparams("parallel"),
    )(proj, proj, par, wa, wx)


def _lru_bwd(name, proj, h, dgh, par, wa, wx):
    S = proj.shape[0]
    R = proj.shape[1] // 2
    H = R // LANES
    ch = _tile(S, SEQ_CHUNK)
    nch = S // ch
    H8 = SUBLANES
    nb = ch // H8

    def body(up_ref, y_ref, h_ref, dgh_ref, par_ref, wa_ref, wx_ref,
             dup_ref, dy_ref, dwa_ref, dwx_ref, dpar_ref, up_pad, h_pad, du_pad, vec_acc):
        _pad_copy(up_pad, up_ref, H8, 0)
        _pad_copy(h_pad, h_ref, H8, 0)
        du_pad[pl.ds(S, H8), :] = jnp.zeros((H8, LANES), F32)
        par = par_ref[...]
        cw, cb, ba, bx, lam = par[0:4], par[4:5], par[5:6], par[6:7], par[7:8]
        sp = _softplus(-lam)
        wa_m, wx_m = wa_ref[...], wx_ref[...]
        dwa_ref[...] = jnp.zeros_like(dwa_ref)
        dwx_ref[...] = jnp.zeros_like(dwx_ref)
        vec_acc[...] = jnp.zeros_like(vec_acc)
        nt_dims = (((1,), (1,)), ((), ()))
        tn_dims = (((0,), (0,)), ((), ()))

        def chunk(it, carry):
            lam_next, a_next = carry
            ci = nch - 1 - it
            r0 = pl.multiple_of(ci * ch, ch)
            xs = up_pad[pl.ds(r0, ch + H8), :]
            u = _conv_fwd(xs, cw, cb)
            row = _rows(u.shape)
            grow = row + r0
            ub, r, ig, a, mult = _lru_gates(u, wa_m, wx_m, ba, bx, sp, grow)
            hs = h_pad[pl.ds(r0, ch + H8), :]
            hcur = hs[H8:]
            hprev = pltpu.roll(hs, 1, 0)[H8:]
            y = y_ref[pl.ds(r0, ch), :]
            dgh = dgh_ref[pl.ds(r0, ch), :]
            gy, t = _gelu(y)
            dy_ref[pl.ds(r0, ch), :] = (dgh * hcur * _gelu_grad(y, t)).astype(dy_ref.dtype)
            dh = dgh * gy
            c = jnp.where(row == ch - 1, a_next, pltpu.roll(a, ch - 1, 0))
            ls = [None] * nb
            for j in range(nb - 1, -1, -1):
                cc, dd = _scan8_bwd(c[j * H8:(j + 1) * H8], dh[j * H8:(j + 1) * H8])
                lj = dd + cc * lam_next
                lam_next = lj[0:1]
                ls[j] = lj
            lmb = jnp.concatenate(ls, axis=0)
            da = lmb * hprev
            gu = ig * u
            dmult = lmb * gu
            dlog_a = da * a + jnp.where(grow == 0, 0.0, dmult * (-(a * a) / mult))
            dr = dlog_a * ((-LRU_C) * sp)
            drp = dr * r * (1.0 - r)
            dip = (lmb * mult * u) * ig * (1.0 - ig)
            drb, dib = drp.astype(BF16), dip.astype(BF16)
            du = (lmb * mult * ig
                  + lax.dot_general(drb, wa_m, nt_dims, preferred_element_type=F32)
                  + lax.dot_general(dib, wx_m, nt_dims, preferred_element_type=F32))
            du_pad[pl.ds(r0, ch), :] = du
            dwa_ref[...] += lax.dot_general(ub, drb, tn_dims, preferred_element_type=F32)
            dwx_ref[...] += lax.dot_general(ub, dib, tn_dims, preferred_element_type=F32)
            ssum = lambda v: jnp.sum(v, axis=0, keepdims=True)
            vec_acc[0:1, :] += ssum(drp)
            vec_acc[1:2, :] += ssum(dip)
            vec_acc[2:3, :] += ssum(dlog_a * ((-LRU_C) * r))
            return lam_next, a[0:1]

        zero = jnp.zeros((1, LANES), F32)
        lax.fori_loop(0, nch, chunk, (zero, zero))

        def conv_chunk(ci, acc):
            r0 = pl.multiple_of(ci * ch, ch)
            ds = du_pad[pl.ds(r0, ch + H8), :]
            xs = up_pad[pl.ds(r0, ch + H8), :]
            n = ch + H8
            du = ds[:ch]
            dup = cw[3:4] * du
            new = [acc[3] + jnp.sum(du * xs[H8:], axis=0, keepdims=True)]
            for k in (1, 2, 3):
                dup = dup + cw[3 - k:4 - k] * pltpu.roll(ds, n - k, 0)[:ch]
                new.append(acc[3 - k] + jnp.sum(du * pltpu.roll(xs, k, 0)[H8:], axis=0, keepdims=True))
            dup_ref[pl.ds(r0, ch), :] = dup.astype(dup_ref.dtype)
            return (new[3], new[2], new[1], new[0], acc[4] + jnp.sum(du, axis=0, keepdims=True))

        acc = lax.fori_loop(0, nch, conv_chunk, (zero,) * 5)
        dlam = vec_acc[2:3, :] * (-jax.nn.sigmoid(-lam))
        dpar_ref[...] = jnp.concatenate(list(acc) + [vec_acc[0:1, :], vec_acc[1:2, :], dlam], axis=0)

    col = lambda off: pl.BlockSpec((S, LANES), lambda h: (0, h + off))
    head = pl.BlockSpec((None, LANES, LANES), lambda h: (h, 0, 0))
    return pl.pallas_call(
        body,
        name=name,
        grid=(H,),
        in_specs=[col(0), col(H), col(0), col(0), pl.BlockSpec((8, LANES), lambda h: (0, h)), head, head],
        out_specs=[col(0), col(0), head, head, pl.BlockSpec((8, LANES), lambda h: (0, h))],
        out_shape=[jax.ShapeDtypeStruct((S, R), BF16), jax.ShapeDtypeStruct((S, R), BF16),
                   jax.ShapeDtypeStruct((H, LANES, LANES), F32), jax.ShapeDtypeStruct((H, LANES, LANES), F32),
                   jax.ShapeDtypeStruct((8, R), F32)],
        scratch_shapes=[pltpu.VMEM((S + H8, LANES), F32), pltpu.VMEM((S + H8, LANES), F32),
                        pltpu.VMEM((S + H8, LANES), F32), pltpu.VMEM((8, LANES), F32)],
        compiler_params=_cparams("parallel"),
    )(proj, proj, h, dgh, par, wa, wx)


def _window_sum(xs, g, up):
    n = xs.shape[0]
    s = xs
    for lvl, k in enumerate((1, 2, 4, 8)):
        sh = pltpu.roll(s, (n - k) if up else k, 0)
        s = s + jnp.where(g >= lvl, sh, 0.0)
    return s


def _pool_count(grow, g):
    return jnp.minimum(grow + 1, lax.shift_left(jnp.int32(2), g)).astype(F32)


def _pool_fwd(name, u, wgrp, par):
    S, D = u.shape
    G, W = wgrp.shape[0], wgrp.shape[1]
    ch = _tile(S, SEQ_CHUNK)
    nch = S // ch
    PH = POOL_HALO

    def body(u_ref, w_ref, par_ref, zs_ref, u_pad):
        g = pl.program_id(0)
        _pad_copy(u_pad, u_ref, PH, 0)
        par = par_ref[...]
        w = w_ref[...]

        def chunk(ci, _):
            r0 = pl.multiple_of(ci * ch, ch)
            xs = u_pad[pl.ds(r0, ch + PH), :]
            ws = _window_sum(xs, g, False)[PH:]
            uc = xs[PH:]
            cnt = _pool_count(_rows(uc.shape) + r0, g)
            pooled = ws / cnt - uc
            z = jnp.dot(pooled.astype(BF16), w, preferred_element_type=F32) + par[0:1]
            zs_ref[pl.ds(r0, ch), :] = (z * par[1:2]).astype(zs_ref.dtype)
            return 0

        lax.fori_loop(0, nch, chunk, 0)

    return pl.pallas_call(
        body,
        name=name,
        grid=(G,),
        in_specs=[pl.BlockSpec((S, W), lambda g: (0, g)), pl.BlockSpec((None, W, W), lambda g: (g, 0, 0)),
                  pl.BlockSpec((2, W), lambda g: (0, g))],
        out_specs=pl.BlockSpec((S, W), lambda g: (0, g)),
        out_shape=jax.ShapeDtypeStruct((S, D), BF16),
        scratch_shapes=[pltpu.VMEM((S + PH, W), F32)],
        compiler_params=_cparams("parallel"),
    )(u, wgrp, par)


def _pool_bwd(name, u, dzs, wgrp, par):
    S, D = u.shape
    G, W = wgrp.shape[0], wgrp.shape[1]
    ch = _tile(S, SEQ_CHUNK)
    nch = S // ch
    PH = POOL_HALO

    def body(u_ref, dzs_ref, w_ref, par_ref, du_ref, dw_ref, dpar_ref, u_pad, q_pad, dw_acc):
        g = pl.program_id(0)
        _pad_copy(u_pad, u_ref, PH, 0)
        q_pad[pl.ds(S, PH), :] = jnp.zeros((PH, W), F32)
        par = par_ref[...]
        w = w_ref[...]
        dw_acc[...] = jnp.zeros_like(dw_acc)

        def chunk(ci, acc):
            db, dsc = acc
            r0 = pl.multiple_of(ci * ch, ch)
            xs = u_pad[pl.ds(r0, ch + PH), :]
            ws = _window_sum(xs, g, False)[PH:]
            uc = xs[PH:]
            cnt = _pool_count(_rows(uc.shape) + r0, g)
            pooled = (ws / cnt - uc).astype(BF16)
            z = jnp.dot(pooled, w, preferred_element_type=F32) + par[0:1]
            dzs = dzs_ref[pl.ds(r0, ch), :]
            dz = dzs * par[1:2]
            dzb = dz.astype(BF16)
            dw_acc[...] += lax.dot_general(pooled, dzb, (((0,), (0,)), ((), ())), preferred_element_type=F32)
            dpooled = lax.dot_general(dzb, w, (((1,), (1,)), ((), ())), preferred_element_type=F32)
            q_pad[pl.ds(r0, ch), :] = dpooled / cnt
            return (db + jnp.sum(dz, axis=0, keepdims=True), dsc + jnp.sum(dzs * z, axis=0, keepdims=True))

        zero = jnp.zeros((1, W), F32)
        db, dsc = lax.fori_loop(0, nch, chunk, (zero, zero))
        dpar_ref[...] = jnp.concatenate([db, dsc], axis=0)
        dw_ref[...] = dw_acc[...].astype(dw_ref.dtype)

        def back(ci, _):
            r0 = pl.multiple_of(ci * ch, ch)
            qs = q_pad[pl.ds(r0, ch + PH), :]
            ws = _window_sum(qs, g, True)[:ch]
            qc = qs[:ch]
            cnt = _pool_count(_rows(qc.shape) + r0, g)
            du_ref[pl.ds(r0, ch), :] = (ws - qc * cnt).astype(du_ref.dtype)
            return 0

        lax.fori_loop(0, nch, back, 0)

    blk = pl.BlockSpec((S, W), lambda g: (0, g))
    wspec = pl.BlockSpec((None, W, W), lambda g: (g, 0, 0))
    pspec = pl.BlockSpec((2, W), lambda g: (0, g))
    return pl.pallas_call(
        body,
        name=name,
        grid=(G,),
        in_specs=[blk, blk, wspec, pspec],
        out_specs=[blk, wspec, pspec],
        out_shape=[jax.ShapeDtypeStruct((S, D), BF16), jax.ShapeDtypeStruct((G, W, W), BF16),
                   jax.ShapeDtypeStruct((2, D), F32)],
        scratch_shapes=[pltpu.VMEM((S + PH, W), F32), pltpu.VMEM((S + PH, W), F32), pltpu.VMEM((W, W), F32)],
        compiler_params=_cparams("parallel"),
    )(u, dzs, wgrp, par)


def _my_place():
    x, y, c = lax.axis_index("x"), lax.axis_index("y"), lax.axis_index("c")
    return x, y, c, 4 * x + 2 * y + c


def _peers(x, y, c):
    out = []
    for d in range(1, N_DEV):
        px = 1 - x if d & 4 else x
        py = 1 - y if d & 2 else y
        pc = 1 - c if d & 1 else c
        out.append(((px, py, pc), 4 * px + 2 * py + pc))
    return out


def _window(ref, axis, start, size):
    idx = [slice(None)] * len(ref.shape)
    idx[axis] = pl.ds(start, size)
    return ref.at[tuple(idx)]


def _to_bf16(name, arrs):
    outs = []
    for i, a in enumerate(arrs):
        a2 = a.reshape(-1, a.shape[-1])
        tr = _tile(a2.shape[0], 512)
        o = pl.pallas_call(
            lambda a_ref, o_ref: o_ref.__setitem__(Ellipsis, a_ref[...].astype(BF16)),
            name=f"{name}_{i}",
            grid=(a2.shape[0] // tr,),
            in_specs=[pl.BlockSpec((tr, a2.shape[1]), lambda r: (r, 0))],
            out_specs=pl.BlockSpec((tr, a2.shape[1]), lambda r: (r, 0)),
            out_shape=jax.ShapeDtypeStruct(a2.shape, BF16),
            compiler_params=_cparams("parallel"),
        )(a2)
        outs.append(o.reshape(a.shape))
    return outs


def _all_gather(name, shards, axes):
    n = len(shards)
    sizes = [s.shape[ax] for s, ax in zip(shards, axes)]

    def body(*refs):
        ins, outs = refs[:n], refs[n:2 * n]
        send, recv, loc = refs[2 * n:]
        x, y, c, me = _my_place()
        peers = _peers(x, y, c)
        local = []
        for i in range(n):
            dst = _window(outs[i], axes[i], me * sizes[i], sizes[i])
            cp = pltpu.make_async_copy(ins[i], dst, loc.at[i])
            cp.start()
            local.append(cp)
            for peer, _ in peers:
                pltpu.make_async_remote_copy(src_ref=ins[i], dst_ref=dst, send_sem=send.at[i], recv_sem=recv.at[i],
                                             device_id=peer, device_id_type=MESH).start()
        for i in range(n):
            local[i].wait()
            seven = _window(outs[i], axes[i], 0, (N_DEV - 1) * sizes[i])
            pltpu.make_async_remote_copy(src_ref=seven, dst_ref=seven, send_sem=send.at[i], recv_sem=recv.at[i],
                                         device_id=(x, y, c), device_id_type=MESH).wait()

    def full_shape(s, ax):
        shp = list(s.shape)
        shp[ax] *= N_DEV
        return jax.ShapeDtypeStruct(tuple(shp), s.dtype)

    any_spec = pl.BlockSpec(memory_space=pl.ANY)
    return pl.pallas_call(
        body,
        name=name,
        in_specs=[any_spec] * n,
        out_specs=[any_spec] * n,
        out_shape=[full_shape(s, ax) for s, ax in zip(shards, axes)],
        scratch_shapes=[pltpu.SemaphoreType.DMA((n,)), pltpu.SemaphoreType.DMA((n,)), pltpu.SemaphoreType.DMA((n,))],
        compiler_params=pltpu.CompilerParams(has_side_effects=True),
    )(*shards)


HBM_SPEC = pl.BlockSpec(memory_space=pltpu.HBM)
SEM_SPEC = pl.BlockSpec(memory_space=pltpu.SEMAPHORE)
SPLIT_EFFECT = pltpu.SideEffectType.DATAFLOW_SIDE_EFFECTING


def _push_all(kind, src, dst, axis, size, send_sem, recv_sem, place):
    x, y, c, me = place
    for peer, pidx in _peers(x, y, c):
        if kind == "gather":
            s = d = _window(dst, axis, me * size, size)
        else:
            s, d = _window(src, axis, pidx * size, size), dst.at[me]
        pltpu.make_async_remote_copy(src_ref=s, dst_ref=d, send_sem=send_sem, recv_sem=recv_sem, device_id=peer,
                                     device_id_type=MESH).start()


def _drain_all(kind, dst, axis, size, send_sem, recv_sem, place):
    x, y, c, _ = place
    seven = _window(dst, axis, 0, (N_DEV - 1) * size) if kind == "gather" else dst.at[pl.ds(0, N_DEV - 1)]
    pltpu.make_async_remote_copy(src_ref=seven, dst_ref=seven, send_sem=send_sem, recv_sem=recv_sem,
                                 device_id=(x, y, c), device_id_type=MESH).wait()


def _own_block_placed(src, axis, size, me):
    own = lax.dynamic_slice_in_dim(src, me * size, size, axis)
    return lax.dynamic_update_slice_in_dim(lax.empty((N_DEV,) + own.shape, src.dtype), own[None], me, 0)


def _split_start(name, kind, srcs, lands, axes, sizes, after=()):
    n, ns, na = len(lands), len(srcs), len(after)

    def body(*refs):
        src_refs, land_refs = refs[:ns], refs[ns:ns + n]
        send, recv = refs[ns + n + na], refs[ns + n + na + 1]
        token = refs[-1]
        place = _my_place()
        for k in range(n):
            _push_all(kind, src_refs[k] if ns else None, land_refs[k], axes[k], sizes[k], send.at[k], recv.at[k], place)
        token[...] = jnp.zeros_like(token)

    hbm = lambda a: pltpu.HBM(a.shape, a.dtype)
    res = pl.pallas_call(
        body,
        name=name,
        out_shape=(pltpu.SemaphoreType.DMA((n,)), pltpu.SemaphoreType.DMA((n,)), *[hbm(a) for a in srcs],
                   *[hbm(a) for a in lands], jax.ShapeDtypeStruct((SUBLANES, LANES), F32)),
        in_specs=[HBM_SPEC] * (ns + n) + [pl.BlockSpec(memory_space=pl.ANY)] * na,
        out_specs=(SEM_SPEC, SEM_SPEC, *[HBM_SPEC] * (ns + n), pl.BlockSpec(memory_space=pltpu.VMEM)),
        input_output_aliases={k: 2 + k for k in range(ns + n)},
        compiler_params=pltpu.CompilerParams(has_side_effects=SPLIT_EFFECT),
    )(*[pltpu.with_memory_space_constraint(a, pltpu.HBM) for a in (*srcs, *lands)], *after)
    return res[0], res[1], list(res[2:2 + ns]), list(res[2 + ns:2 + ns + n]), res[-1]


def _split_wait(name, kind, handle, axes, sizes, after):
    send, recv, srcs, lands, _ = handle
    n, ns = len(lands), len(srcs)

    def body(*refs):
        land_refs = refs[ns:ns + n]
        send_ref, recv_ref = refs[ns + n], refs[ns + n + 1]
        place = _my_place()
        for k in range(n):
            _drain_all(kind, land_refs[k], axes[k], sizes[k], send_ref.at[k], recv_ref.at[k], place)

    hbm = lambda a: pltpu.HBM(a.shape, a.dtype)
    res = pl.pallas_call(
        body,
        name=name,
        out_shape=tuple(hbm(a) for a in (*srcs, *lands)),
        in_specs=[HBM_SPEC] * (ns + n) + [SEM_SPEC, SEM_SPEC, pl.BlockSpec(memory_space=pl.ANY)],
        out_specs=tuple([HBM_SPEC] * (ns + n)),
        input_output_aliases={k: k for k in range(ns + n)},
        compiler_params=pltpu.CompilerParams(has_side_effects=SPLIT_EFFECT),
    )(*srcs, *lands, send, recv, after)
    return list(res[ns:])


def _cast_into_window(name, a, l, axis, me1):
    shp = a.shape[1:]
    cast = lambda me_ref, a_ref, o_ref: o_ref.__setitem__(Ellipsis, a_ref[...].astype(BF16))
    if len(shp) == 3:
        assert axis == 1
        G, r, c = shp
        full = (G, r * N_DEV, c)
        grid = (G,)
        in_spec = pl.BlockSpec((None, None, r, c), lambda g, me: (l, g, 0, 0))
        out_spec = pl.BlockSpec((None, r, c), lambda g, me: (g, me[0], 0))
    else:
        r, c = shp
        tr = _tile(r, 512)
        nb = r // tr
        grid = (nb,)
        in_spec = pl.BlockSpec((None, tr, c), lambda i, me: (l, i, 0))
        if axis == 0:
            full = (r * N_DEV, c)
            out_spec = pl.BlockSpec((tr, c), lambda i, me: (me[0] * nb + i, 0))
        else:
            full = (r, c * N_DEV)
            out_spec = pl.BlockSpec((tr, c), lambda i, me: (i, me[0]))
    return pl.pallas_call(
        cast,
        name=name,
        grid_spec=pltpu.PrefetchScalarGridSpec(num_scalar_prefetch=1, grid=grid, in_specs=[in_spec], out_specs=out_spec),
        out_shape=jax.ShapeDtypeStruct(full, BF16),
        compiler_params=_cparams("arbitrary"),
    )(me1, a)


def _adamw_math(w, g, m, v):
    m = ADAM_B1 * m + (1.0 - ADAM_B1) * g
    v = ADAM_B2 * v + (1.0 - ADAM_B2) * jnp.square(g)
    m_hat = m / (1.0 - ADAM_B1 ** ADAM_STEP)
    v_hat = v / (1.0 - ADAM_B2 ** ADAM_STEP)
    delta = -ADAM_LR * (m_hat / (jnp.sqrt(v_hat) + ADAM_EPS) + ADAM_WD * w)
    return delta, m, v


def _sum_slots(buf_ref):
    g = buf_ref[0].astype(F32)
    for s in range(1, N_DEV):
        g = g + buf_ref[s].astype(F32)
    return g


def _adamw_layer(name, buf, w, m, v, l, prev):
    shape = w.shape
    L, C = shape[0], shape[-1]
    Rr = math.prod(shape[1:-1])
    buf3 = buf.reshape(N_DEV, Rr, C)
    w3, m3, v3 = (t.reshape(L, Rr, C) for t in (w, m, v))
    tr = _tile(Rr, 2 * LANES) if Rr % LANES == 0 else Rr
    n_prev = 0 if prev is None else 4

    def body(buf_ref, w_ref, m_ref, v_ref, *rest):
        g_out, d_out, m_out, v_out = rest[n_prev:]
        g = _sum_slots(buf_ref)
        d, mm, vv = _adamw_math(w_ref[...], g, m_ref[...], v_ref[...])
        g_out[...] = g
        d_out[...] = d
        m_out[...] = mm
        v_out[...] = vv

    spec = pl.BlockSpec((None, tr, C), lambda r: (l, r, 0))
    outs = pl.pallas_call(
        body,
        name=name,
        grid=(Rr // tr,),
        in_specs=[pl.BlockSpec((N_DEV, tr, C), lambda r: (0, r, 0)), spec, spec, spec]
        + [pl.BlockSpec(memory_space=pl.ANY)] * n_prev,
        out_specs=[spec] * 4,
        out_shape=[jax.ShapeDtypeStruct((L, Rr, C), F32)] * 4,
        input_output_aliases={4 + k: k for k in range(n_prev)},
        compiler_params=_cparams("parallel"),
    )(buf3, w3, m3, v3, *(prev or ()))
    return list(outs)


def _sum8(name, buf):
    R = buf.shape[1]

    def body(buf_ref, o_ref):
        o_ref[...] = _sum_slots(buf_ref)

    return pl.pallas_call(
        body,
        name=name,
        in_specs=[pl.BlockSpec(buf.shape, lambda: (0, 0, 0))],
        out_specs=pl.BlockSpec((R, LANES), lambda: (0, 0)),
        out_shape=jax.ShapeDtypeStruct((R, LANES), F32),
        compiler_params=_cparams(),
    )(buf)


def _adamw_packed(name, w, g, m, v):
    R = w.shape[0]
    tr = _row_tile(R, 512)

    def body(w_ref, g_ref, m_ref, v_ref, d_out, m_out, v_out):
        d, mm, vv = _adamw_math(w_ref[...], g_ref[...], m_ref[...], v_ref[...])
        d_out[...] = d
        m_out[...] = mm
        v_out[...] = vv

    spec = pl.BlockSpec((tr, LANES), lambda r: (r, 0))
    return pl.pallas_call(
        body,
        name=name,
        grid=(R // tr,),
        in_specs=[spec] * 4,
        out_specs=[spec] * 3,
        out_shape=[jax.ShapeDtypeStruct((R, LANES), F32)] * 3,
        compiler_params=_cparams("parallel"),
    )(w, g, m, v)


def _pack(arrs, pad_rows_to=SUBLANES):
    parts = []
    for a in arrs:
        flat = a.reshape(-1)
        per = LANES * pad_rows_to
        padded = -(-flat.shape[0] // per) * per
        if padded != flat.shape[0]:
            flat = jnp.pad(flat, (0, padded - flat.shape[0]))
        parts.append(flat.reshape(-1, LANES))
    return jnp.concatenate(parts, axis=0)


def _unpack(packed, shapes, pad_rows_to=SUBLANES):
    out = []
    r = 0
    for shp in shapes:
        nel = math.prod(shp)
        per = LANES * pad_rows_to
        rows = -(-nel // per) * pad_rows_to
        out.append(packed[r:r + rows].reshape(-1)[:nel].reshape(shp))
        r += rows
    return out


BIG = ("lru_w_in", "lru_w_out", "pool_w_in", "pool_w_grp", "pool_w_out", "mlp_w1", "mlp_w2", "ple_w", "ple_gate_w")
BIG_AXIS = {"lru_w_in": 2, "lru_w_out": 1, "pool_w_in": 1, "pool_w_grp": 2, "pool_w_out": 1, "mlp_w1": 2,
            "mlp_w2": 1, "ple_w": 2, "ple_gate_w": 1}
SMALL_SHARDED = ("lru_conv_w", "pool_b_grp", "pool_scale")
REPLICATED = ("lru_conv_b", "lru_wa", "lru_ba", "lru_wx", "lru_bx", "lru_lambda", "ln_mix_g", "ln_mix_b",
              "ln_mlp_g", "ln_mlp_b", "ple_gate_b")
WEIGHTS = ("lru_w_in", "lru_conv_w", "lru_conv_b", "lru_wa", "lru_ba", "lru_wx", "lru_bx", "lru_lambda", "lru_w_out",
           "pool_w_in", "pool_w_grp", "pool_b_grp", "pool_scale", "pool_w_out", "ln_mix_g", "ln_mix_b", "mlp_w1",
           "mlp_w2", "ln_mlp_g", "ln_mlp_b", "ple_w", "ple_gate_w", "ple_gate_b")
INPUTS = ("x", "p") + WEIGHTS + ("loss_target",) + tuple("m_" + n for n in WEIGHTS) + tuple("v_" + n for n in WEIGHTS)


def _gather_last_axis(packed_full, shard_shape):
    nel = math.prod(shard_shape)
    blocks = packed_full.reshape(N_DEV, -1)[:, :nel].reshape((N_DEV,) + tuple(shard_shape))
    return jnp.concatenate([blocks[d] for d in range(N_DEV)], axis=-1)


def kernel(x, p, lru_w_in, lru_conv_w, lru_conv_b, lru_wa, lru_ba, lru_wx, lru_bx, lru_lambda, lru_w_out, pool_w_in, pool_w_grp, pool_b_grp, pool_scale, pool_w_out, ln_mix_g, ln_mix_b, mlp_w1, mlp_w2, ln_mlp_g, ln_mlp_b, ple_w, ple_gate_w, ple_gate_b, loss_target, m_lru_w_in, m_lru_conv_w, m_lru_conv_b, m_lru_wa, m_lru_ba, m_lru_wx, m_lru_bx, m_lru_lambda, m_lru_w_out, m_pool_w_in, m_pool_w_grp, m_pool_b_grp, m_pool_scale, m_pool_w_out, m_ln_mix_g, m_ln_mix_b, m_mlp_w1, m_mlp_w2, m_ln_mlp_g, m_ln_mlp_b, m_ple_w, m_ple_gate_w, m_ple_gate_b, v_lru_w_in, v_lru_conv_w, v_lru_conv_b, v_lru_wa, v_lru_ba, v_lru_wx, v_lru_bx, v_lru_lambda, v_lru_w_out, v_pool_w_in, v_pool_w_grp, v_pool_b_grp, v_pool_scale, v_pool_w_out, v_ln_mix_g, v_ln_mix_b, v_mlp_w1, v_mlp_w2, v_ln_mlp_g, v_ln_mlp_b, v_ple_w, v_ple_gate_w, v_ple_gate_b):
    A = dict(zip(INPUTS, (x, p, lru_w_in, lru_conv_w, lru_conv_b, lru_wa, lru_ba, lru_wx, lru_bx, lru_lambda, lru_w_out, pool_w_in, pool_w_grp, pool_b_grp, pool_scale, pool_w_out, ln_mix_g, ln_mix_b, mlp_w1, mlp_w2, ln_mlp_g, ln_mlp_b, ple_w, ple_gate_w, ple_gate_b, loss_target, m_lru_w_in, m_lru_conv_w, m_lru_conv_b, m_lru_wa, m_lru_ba, m_lru_wx, m_lru_bx, m_lru_lambda, m_lru_w_out, m_pool_w_in, m_pool_w_grp, m_pool_b_grp, m_pool_scale, m_pool_w_out, m_ln_mix_g, m_ln_mix_b, m_mlp_w1, m_mlp_w2, m_ln_mlp_g, m_ln_mlp_b, m_ple_w, m_ple_gate_w, m_ple_gate_b, v_lru_w_in, v_lru_conv_w, v_lru_conv_b, v_lru_wa, v_lru_ba, v_lru_wx, v_lru_bx, v_lru_lambda, v_lru_w_out, v_pool_w_in, v_pool_w_grp, v_pool_b_grp, v_pool_scale, v_pool_w_out, v_ln_mix_g, v_ln_mix_b, v_mlp_w1, v_mlp_w2, v_ln_mlp_g, v_ln_mlp_b, v_ple_w, v_ple_gate_w, v_ple_gate_b)))
    depth = ln_mix_g.shape[0]
    alpha = (2 * depth) ** 0.25
    S, D = x.shape[1], x.shape[2]
    xs = x.reshape(S, D)
    tgt = loss_target.reshape(S, D)
    p3 = p.reshape(depth, S, p.shape[-1])
    me = 4 * lax.axis_index("x") + 2 * lax.axis_index("y") + lax.axis_index("c")

    def layer_weights(i):
        s = i // 2
        mixer = ("lru_w_in", "lru_w_out") if i % 2 == 0 else ("pool_w_in", "pool_w_grp", "pool_w_out")
        return [(n, s) for n in mixer] + [(n, i) for n in ("mlp_w1", "mlp_w2", "ple_w", "ple_gate_w")]

    def axis_of(key):
        return 0 if key[0] == "small" else BIG_AXIS[key[0]] - 1

    def start_gather(tag, keys, after):
        axes = [axis_of(k) for k in keys]
        lands = [land[k] for k in keys]
        sizes = [a.shape[ax] // N_DEV for a, ax in zip(lands, axes)]
        return keys, _split_start(f"gather_{tag}_start", "gather", [], lands, axes, sizes, after=after), axes, sizes

    def finish_gather(tag, pending, after):
        keys, handle, axes, sizes = pending
        for (n, l), full in zip(keys, _split_wait(f"gather_{tag}_wait", "gather", handle, axes, sizes, after)):
            W[n][l] = full

    def start_exchange(tag, keys, arrs, after):
        axes = [axis_of(k) for k in keys]
        sizes = [a.shape[ax] // N_DEV for a, ax in zip(arrs, axes)]
        lands = [_own_block_placed(a, ax, sz, me) for a, ax, sz in zip(arrs, axes, sizes)]
        return keys, _split_start(f"exchange_{tag}_start", "scatter", arrs, lands, axes, sizes, after=after), axes, sizes

    def finish_exchange(tag, pending, after):
        keys, handle, axes, sizes = pending
        partial.update(zip(keys, _split_wait(f"exchange_{tag}_wait", "scatter", handle, axes, sizes, after)))

    me1 = jnp.reshape(me, (1,)).astype(jnp.int32)
    land = {k: _cast_into_window(f"cast_{k[0]}_{k[1]}", A[k[0]], k[1], axis_of(k), me1)
            for i in range(depth) for k in layer_weights(i)}
    W = {n: [None] * A[n].shape[0] for n in BIG}
    small_shard_shapes = [A[n].shape for n in SMALL_SHARDED]
    gathered = _all_gather("gather_small_params", [_pack([A[n] for n in SMALL_SHARDED])], [0])
    first_key = layer_weights(0)[:1]
    gather_first = start_gather("first", first_key, (gathered[0],))
    gather_rest0 = start_gather("rest0", layer_weights(0)[1:], (gather_first[1][4],))
    gather_pending = {}
    token = gather_rest0[1][4]
    for i in range(1, depth):
        gather_pending[i] = start_gather(f"l{i}", layer_weights(i), (token,))
        token = gather_pending[i][1][4]
    finish_gather("first", gather_first, token)
    small_full = gathered[0].reshape(N_DEV, -1, LANES)
    r = 0
    for n, shp in zip(SMALL_SHARDED, small_shard_shapes):
        rows = -(-math.prod(shp) // (LANES * SUBLANES)) * SUBLANES
        W[n] = _gather_last_axis(small_full[:, r:r + rows], shp)
        r += rows
    wa_b, wx_b = _to_bf16("cast_gates", [lru_wa, lru_wx])
    n_lru = lru_w_in.shape[0]
    lru_par = [jnp.concatenate([W["lru_conv_w"][s], lru_conv_b[s][None], lru_ba[s][None], lru_bx[s][None],
                                lru_lambda[s][None]], axis=0) for s in range(n_lru)]
    pool_par = [jnp.stack([W["pool_b_grp"][s], W["pool_scale"][s]], axis=0) for s in range(pool_w_in.shape[0])]

    saved = []
    h_in = xs
    h_in_b, p3b = _to_bf16("cast_inputs", [xs, p3])
    for i in range(depth):
        s = i // 2
        sv = {"x0": h_in, "x0b": h_in_b}
        if i > 0:
            finish_gather(f"l{i}", gather_pending[i], h_in)
        if i % 2 == 0:
            sv["proj"] = _mm(f"l{i}_lru_in", h_in_b, W["lru_w_in"], "nn", [F32], b_lead=s)
            sv["gh"], sv["h"] = _lru_fwd(f"l{i}_lru_core", sv["proj"], lru_par[s], wa_b[s], wx_b[s])
            if i == 0:
                finish_gather("rest0", gather_rest0, sv["gh"])
            sv["mix"] = _mm(f"l{i}_lru_out", sv["gh"], W["lru_w_out"], "nn", [F32], b_lead=s)
        else:
            sv["u"] = _mm(f"l{i}_pool_in", h_in_b, W["pool_w_in"], "nn", [F32], b_lead=s)
            sv["zs"] = _pool_fwd(f"l{i}_pool_core", sv["u"], W["pool_w_grp"][s], pool_par[s])
            sv["mix"] = _mm(f"l{i}_pool_out", sv["zs"], W["pool_w_out"], "nn", [F32], b_lead=s)
        sv["x1"], sv["x1b"] = _ln_fwd(f"l{i}_ln_mix", alpha, h_in, sv["mix"], ln_mix_g[i][None], ln_mix_b[i][None])
        sv["hpre"], sv["hact"] = _mm(f"l{i}_mlp_up", sv["x1b"], W["mlp_w1"], "nn", [BF16, BF16], b_lead=i,
                                     epi=lambda acc: (acc, jnp.square(jnp.maximum(acc, 0.0))))
        sv["mo"] = _mm(f"l{i}_mlp_down", sv["hact"], W["mlp_w2"], "nn", [F32], b_lead=i)
        sv["x2"], sv["x2b"] = _ln_fwd(f"l{i}_ln_mlp", alpha, sv["x1"], sv["mo"], ln_mlp_g[i][None], ln_mlp_b[i][None])
        sv["pp"] = _mm(f"l{i}_ple_up", p3b, W["ple_w"], "nn", [F32], a_lead=i, b_lead=i)

        def ple_epi(acc, bg, x2t, ppt):
            gpre = acc + bg
            x3 = x2t + ppt * jax.nn.sigmoid(gpre)
            return x3, x3, gpre

        h_in, h_in_b, sv["gpre"] = _mm(f"l{i}_ple_gate", sv["x2b"], W["ple_gate_w"], "nn", [F32, BF16, F32], b_lead=i,
                                       epi=ple_epi, extras=[ple_gate_b[i][None], sv["x2"], sv["pp"]])
        saved.append(sv)

    dx, sq = _loss_and_grad("loss", h_in, tgt)
    loss = lax.psum(0.5 * sq[0, 0] / D, ("x", "y", "c"))

    dW = {n: [None] * A[n].shape[0] for n in BIG}
    dsmall = {n: [None] * A[n].shape[0] for n in REPLICATED + SMALL_SHARDED}
    small_names = REPLICATED + SMALL_SHARDED
    partial = {}
    exchange_pending = {}
    exchange_token = ()
    for i in reversed(range(depth)):
        s = i // 2
        sv = saved[i]
        dpp, dgpre, dbg = _ple_bwd(f"l{i}_ple_bwd", dx, sv["gpre"], sv["pp"], after=exchange_token)
        dsmall["ple_gate_b"][i] = dbg[0]
        dW["ple_w"][i] = _mm(f"l{i}_d_ple_w", p3b, dpp, "tn", [BF16], a_lead=i)
        dW["ple_gate_w"][i] = _mm(f"l{i}_d_ple_gate_w", sv["x2b"], dgpre, "tn", [BF16])
        dx2 = _mm(f"l{i}_d_x2", dgpre, W["ple_gate_w"], "nt", [F32], b_lead=i, extras=[dx],
                  epi=lambda acc, d: (acc + d,))
        dz2, dz2b, dg, db = _ln_bwd(f"l{i}_ln_mlp_bwd", alpha, dx2, sv["x1"], sv["mo"], ln_mlp_g[i][None])
        dsmall["ln_mlp_g"][i], dsmall["ln_mlp_b"][i] = dg[0], db[0]
        dhpre = _mm(f"l{i}_d_hpre", dz2b, W["mlp_w2"], "nt", [BF16], b_lead=i, extras=[sv["hpre"]],
                    epi=lambda acc, hp: (acc * (2.0 * jnp.maximum(hp.astype(F32), 0.0)),))
        dW["mlp_w2"][i] = _mm(f"l{i}_d_mlp_w2", sv["hact"], dz2b, "tn", [BF16])
        dW["mlp_w1"][i] = _mm(f"l{i}_d_mlp_w1", sv["x1b"], dhpre, "tn", [BF16])
        mlp_after = ()
        if i == 0:
            early = [(n, 0) for n in ("ple_w", "ple_gate_w", "mlp_w2", "mlp_w1")]
            exchange_early0 = start_exchange("early0", early, [dW[n][l] for n, l in early], ())
            mlp_after = (exchange_early0[1][4],)
        dx1 = _mm(f"l{i}_d_x1", dhpre, W["mlp_w1"], "nt", [F32], b_lead=i, extras=[dz2],
                  epi=lambda acc, d: (acc + alpha * d,), after=mlp_after)
        dz1, dz1b, dg, db = _ln_bwd(f"l{i}_ln_mix_bwd", alpha, dx1, sv["x0"], sv["mix"], ln_mix_g[i][None])
        dsmall["ln_mix_g"][i], dsmall["ln_mix_b"][i] = dg[0], db[0]
        if i % 2 == 0:
            dW["lru_w_out"][s] = _mm(f"l{i}_d_lru_w_out", sv["gh"], dz1b, "tn", [BF16])
            dgh = _mm(f"l{i}_d_gh", dz1b, W["lru_w_out"], "nt", [F32], b_lead=s)
            dup, dy, dwa, dwx, dpar = _lru_bwd(f"l{i}_lru_core_bwd", sv["proj"], sv["h"], dgh, lru_par[s],
                                               wa_b[s], wx_b[s])
            dsmall["lru_wa"][s], dsmall["lru_wx"][s] = dwa, dwx
            dsmall["lru_conv_w"][s] = dpar[0:4]
            for k, n in enumerate(("lru_conv_b", "lru_ba", "lru_bx", "lru_lambda")):
                dsmall[n][s] = dpar[4 + k]
            dmix_in = jnp.concatenate([dup, dy], axis=1)
            win = "lru_w_in"
        else:
            dW["pool_w_out"][s] = _mm(f"l{i}_d_pool_w_out", sv["zs"], dz1b, "tn", [BF16])
            dzs = _mm(f"l{i}_d_zs", dz1b, W["pool_w_out"], "nt", [F32], b_lead=s)
            dmix_in, dW["pool_w_grp"][s], dpar = _pool_bwd(f"l{i}_pool_core_bwd", sv["u"], dzs, W["pool_w_grp"][s],
                                                          pool_par[s])
            dsmall["pool_b_grp"][s], dsmall["pool_scale"][s] = dpar[0], dpar[1]
            win = "pool_w_in"
        dW[win][s] = _mm(f"l{i}_d_{win}", sv["x0b"], dmix_in, "tn", [BF16])
        x0_after = ()
        if i > 0:
            keys = layer_weights(i)
            exchange_pending[i] = start_exchange(f"l{i}", keys, [dW[n][l] for n, l in keys], ())
            exchange_token = (exchange_pending[i][1][4],)
        else:
            small_grads = [jnp.stack(dsmall[n]) for n in small_names]
            small_shapes = [g.shape for g in small_grads]
            packed_g = _pack(small_grads)
            assert packed_g.shape[0] % (N_DEV * SUBLANES) == 0, packed_g.shape
            late = [("lru_w_out", 0), ("lru_w_in", 0), ("small", 0)]
            exchange_late0 = start_exchange("late0", late, [dW["lru_w_out"][0], dW["lru_w_in"][0], packed_g], ())
            x0_after = (exchange_late0[1][4],)
        dx = _mm(f"l{i}_d_x0", dmix_in, W[win], "nt", [F32], b_lead=s, extras=[dz1],
                 epi=lambda acc, d: (acc + alpha * d,), after=x0_after)
    grad_x = dx.reshape(x.shape)

    for i in range(1, depth):
        finish_exchange(f"l{i}", exchange_pending[i], x0_after[0])
    stacked = {n: None for n in BIG}
    layer0 = layer_weights(0)

    def adamw(n, l):
        stacked[n] = _adamw_layer(f"adamw_{n}_{l}", partial[(n, l)], A[n], A["m_" + n], A["v_" + n], l, stacked[n])

    for n in BIG:
        for l in reversed(range(A[n].shape[0])):
            if (n, l) not in layer0:
                adamw(n, l)
    finish_exchange("early0", exchange_early0, stacked["ple_gate_w"][0])
    finish_exchange("late0", exchange_late0, stacked["ple_gate_w"][0])
    for n, l in layer0:
        adamw(n, l)
    outs = {n: [o.reshape(A[n].shape) for o in stacked[n]] for n in BIG}
    red = _sum8("sum_small", partial[("small", 0)])
    red_full = _all_gather("gather_small", [red], [0])[0]
    small_g = dict(zip(small_names, _unpack(red_full, small_shapes)))
    for n in SMALL_SHARDED:
        width = A[n].shape[-1]
        small_g[n] = lax.dynamic_slice_in_dim(small_g[n], me * width, width, axis=small_g[n].ndim - 1)
    pk = lambda pre: _pack([A[pre + n] for n in small_names])
    d_p, m_p, v_p = _adamw_packed("adamw_small", pk(""), _pack([small_g[n] for n in small_names]), pk("m_"), pk("v_"))
    shapes = [A[n].shape for n in small_names]
    for n, d_, m_, v_ in zip(small_names, _unpack(d_p, shapes), _unpack(m_p, shapes), _unpack(v_p, shapes)):
        outs[n] = [small_g[n], d_, m_, v_]

    return (loss, grad_x, *[outs[n][0] for n in WEIGHTS], *[outs[n][1] for n in WEIGHTS],
            *[outs[n][2] for n in WEIGHTS], *[outs[n][3] for n in WEIGHTS])
```

```python
import functools
import math

import jax
import jax.numpy as jnp
from jax import lax
from jax.experimental import pallas as pl
from jax.experimental.pallas import tpu as pltpu

F32 = jnp.float32
BF16 = jnp.bfloat16
MESH = pl.DeviceIdType.MESH
N_DEV = 8
LANES = 128
SUBLANES = 8

LN_EPS = 1e-5
LRU_C = 8.0
CONV_WIDTH = 4
POOL_HALO = 16
ADAM_LR = 0.001
ADAM_B1 = 0.9
ADAM_B2 = 0.999
ADAM_EPS = 1e-08
ADAM_WD = 0.01
ADAM_STEP = 10

VMEM_LIMIT = 48 * 1024 * 1024
SEQ_CHUNK = 256
MM_TK = 4096
MM_TK_TOKENS = 4096
MM_TN = 512
MM_TN_WIDE = 1024
MM_TN_WIDE_MAX_K = 2048
GELU_C0 = math.sqrt(2.0 / math.pi)
GELU_C1 = 0.044715


def _cparams(*sem):
    return pltpu.CompilerParams(dimension_semantics=tuple(sem) if sem else None, vmem_limit_bytes=VMEM_LIMIT)


def _tile(n, pref):
    if n <= pref:
        return n
    t = pref - pref % LANES
    while t >= LANES:
        if n % t == 0:
            return t
        t -= LANES
    return n


def _row_tile(n, pref):
    if n <= pref:
        return n
    t = pref - pref % SUBLANES
    while t >= SUBLANES:
        if n % t == 0:
            return t
        t -= SUBLANES
    return n


def _mm(name, a, b, mode, out_dtypes, epi=None, extras=(), a_lead=None, b_lead=None, tm=1024, tn=None, tk=None,
        after=()):
    if isinstance(b, (list, tuple)):
        b, b_lead = b[b_lead], None
    a2 = a.shape[-2:]
    b2 = b.shape[-2:]
    if mode == "nn":
        (M, K), N = a2, b2[1]
        assert b2[0] == K
    elif mode == "nt":
        (M, K), N = a2, b2[0]
        assert b2[1] == K
    else:
        (K, M), N = a2, b2[1]
        assert b2[0] == K
    if tk is None:
        tk = MM_TK_TOKENS if mode == "tn" else MM_TK
    if tn is None:
        tn = MM_TN_WIDE if (mode != "tn" and K <= MM_TN_WIDE_MAX_K) else MM_TN
    tm, tn, tk = _tile(M, tm), _tile(N, tn), _tile(K, tk)
    nk = K // tk
    n_extra = len(extras)
    n_out = len(out_dtypes)

    def lead(shape, idx, which):
        if which is None:
            return pl.BlockSpec(shape, idx)
        return pl.BlockSpec((None,) + shape, lambda i, j, k: (which,) + idx(i, j, k))

    if mode == "nn":
        a_spec = lead((tm, tk), lambda i, j, k: (i, k), a_lead)
        b_spec = lead((tk, tn), lambda i, j, k: (k, j), b_lead)
        dims = (((1,), (0,)), ((), ()))
    elif mode == "nt":
        a_spec = lead((tm, tk), lambda i, j, k: (i, k), a_lead)
        b_spec = lead((tn, tk), lambda i, j, k: (j, k), b_lead)
        dims = (((1,), (1,)), ((), ()))
    else:
        a_spec = lead((tk, tm), lambda i, j, k: (k, i), a_lead)
        b_spec = lead((tk, tn), lambda i, j, k: (k, j), b_lead)
        dims = (((0,), (0,)), ((), ()))
    e_specs = []
    for e in extras:
        if e.shape[0] == 1:
            e_specs.append(pl.BlockSpec((1, tn), lambda i, j, k: (0, j)))
        else:
            e_specs.append(pl.BlockSpec((tm, tn), lambda i, j, k: (i, j)))

    n_after = len(after)

    def body(a_ref, b_ref, *rest):
        e_refs = rest[:n_extra]
        rest = rest[:n_extra] + rest[n_extra + n_after:]
        o_refs = rest[n_extra:n_extra + n_out]
        part = lax.dot_general(a_ref[...].astype(BF16), b_ref[...].astype(BF16), dims, preferred_element_type=F32)

        def finish(r):
            res = (r,) if epi is None else epi(r, *[e[...] for e in e_refs])
            for o, v in zip(o_refs, res):
                o[...] = v.astype(o.dtype)

        if nk == 1:
            finish(part)
            return
        acc = rest[n_extra + n_out]
        k = pl.program_id(2)

        @pl.when(k == 0)
        def _():
            acc[...] = part

        @pl.when(jnp.logical_and(k > 0, k < nk - 1))
        def _():
            acc[...] += part

        @pl.when(k == nk - 1)
        def _():
            finish(acc[...] + part)

    outs = pl.pallas_call(
        body,
        name=name,
        grid=(M // tm, N // tn, nk),
        in_specs=[a_spec, b_spec] + e_specs + [pl.BlockSpec(memory_space=pl.ANY)] * n_after,
        out_specs=[pl.BlockSpec((tm, tn), lambda i, j, k: (i, j)) for _ in out_dtypes],
        out_shape=[jax.ShapeDtypeStruct((M, N), d) for d in out_dtypes],
        scratch_shapes=[pltpu.VMEM((tm, tn), F32)] if nk > 1 else [],
        compiler_params=_cparams("parallel", "parallel", "arbitrary"),
    )(a, b, *extras, *after)
    return outs[0] if n_out == 1 else tuple(outs)


def _rowwise(name, fn, tiled, params, outs, accs=(), tm=256, after=()):
    S = tiled[0].shape[0]
    tm = _tile(S, tm)
    nt, npar, no = len(tiled), len(params), len(outs)
    n_after = len(after)

    def body(*refs):
        t_refs = refs[:nt]
        p_refs = refs[nt:nt + npar]
        refs = refs[nt + npar + n_after:]
        o_refs = refs[:no]
        a_refs = refs[no:]
        res = fn(*[r[...] for r in t_refs], *[r[...] for r in p_refs])
        for o, v in zip(o_refs, res[:no]):
            o[...] = v.astype(o.dtype)
        first = pl.program_id(0) == 0
        for ar, v in zip(a_refs, res[no:]):
            @pl.when(first)
            def _(ar=ar, v=v):
                ar[...] = v

            @pl.when(jnp.logical_not(first))
            def _(ar=ar, v=v):
                ar[...] += v

    full = lambda p: pl.BlockSpec(p.shape, lambda i, nd=p.ndim: (0,) * nd)
    res = pl.pallas_call(
        body,
        name=name,
        grid=(S // tm,),
        in_specs=[pl.BlockSpec((tm, t.shape[1]), lambda i: (i, 0)) for t in tiled] + [full(p) for p in params]
        + [pl.BlockSpec(memory_space=pl.ANY)] * n_after,
        out_specs=[pl.BlockSpec((tm, c), lambda i: (i, 0)) for c, _ in outs]
        + [pl.BlockSpec(s, lambda i, nd=len(s): (0,) * nd) for s in accs],
        out_shape=[jax.ShapeDtypeStruct((S, c), d) for c, d in outs] + [jax.ShapeDtypeStruct(s, F32) for s in accs],
        compiler_params=_cparams("arbitrary"),
    )(*tiled, *params, *after)
    return res


def _ln_stats(z):
    mu = jnp.mean(z, axis=-1, keepdims=True)
    zc = z - mu
    var = jnp.mean(zc * zc, axis=-1, keepdims=True)
    return zc, lax.rsqrt(var + LN_EPS)


def _ln_fwd(name, alpha, xp, m, g, b):
    def fn(xp, m, g, b):
        zc, rstd = _ln_stats(alpha * xp + m)
        y = zc * rstd * g + b
        return y, y

    d = xp.shape[1]
    return _rowwise(name, fn, [xp, m], [g, b], [(d, F32), (d, BF16)])


def _ln_bwd(name, alpha, dy, xp, m, g):
    def fn(dy, xp, m, g):
        zc, rstd = _ln_stats(alpha * xp + m)
        xhat = zc * rstd
        dxh = dy * g
        m1 = jnp.mean(dxh, axis=-1, keepdims=True)
        m2 = jnp.mean(dxh * xhat, axis=-1, keepdims=True)
        dz = rstd * (dxh - m1 - xhat * m2)
        return dz, dz, jnp.sum(dy * xhat, axis=0, keepdims=True), jnp.sum(dy, axis=0, keepdims=True)

    d = xp.shape[1]
    return _rowwise(name, fn, [dy, xp, m], [g], [(d, F32), (d, BF16)], accs=[(1, d), (1, d)])


def _loss_and_grad(name, y, target):
    d = y.shape[1]

    def fn(y, t):
        err = y - t
        sq = jnp.sum(jnp.sum(err * err, axis=0, keepdims=True), axis=1, keepdims=True)
        return err * (1.0 / d), jnp.broadcast_to(sq, (1, LANES))

    return _rowwise(name, fn, [y, target], [], [(d, F32)], accs=[(1, LANES)])


def _ple_bwd(name, dx3, gpre, pp, after=()):
    def fn(dx3, gpre, pp):
        gate = jax.nn.sigmoid(gpre)
        dgpre = dx3 * pp * gate * (1.0 - gate)
        return dx3 * gate, dgpre, jnp.sum(dgpre, axis=0, keepdims=True)

    d = dx3.shape[1]
    return _rowwise(name, fn, [dx3, gpre, pp], [], [(d, BF16), (d, BF16)], accs=[(1, d)], after=after)


def _rows(shape):
    return lax.broadcasted_iota(jnp.int32, shape, 0)


def _gelu(y):
    t = jnp.tanh(GELU_C0 * (y + GELU_C1 * y * y * y))
    return 0.5 * y * (1.0 + t), t


def _gelu_grad(y, t):
    return 0.5 * (1.0 + t) + 0.5 * y * (1.0 - t * t) * GELU_C0 * (1.0 + 3.0 * GELU_C1 * y * y)


def _neg_expm1(x):
    series = -x * (1.0 + x * (0.5 + x * (1.0 / 6.0 + x * (1.0 / 24.0))))
    return jnp.where(x > -0.02, series, 1.0 - jnp.exp(x))


def _softplus(x):
    return jnp.maximum(x, 0.0) + jnp.log(1.0 + jnp.exp(-jnp.abs(x)))


def _conv_fwd(xs, cw, cb):
    n = xs.shape[0]
    u = cw[3:4] * xs
    for k in (1, 2, 3):
        u = u + cw[3 - k:4 - k] * pltpu.roll(xs, k, 0)
    del n
    return u[SUBLANES:] + cb


def _lru_gates(u, wa, wx, ba, bx, sp, grow):
    ub = u.astype(BF16)
    r = jax.nn.sigmoid(jnp.dot(ub, wa, preferred_element_type=F32) + ba)
    ig = jax.nn.sigmoid(jnp.dot(ub, wx, preferred_element_type=F32) + bx)
    log_a = (-LRU_C) * r * sp
    a = jnp.exp(log_a)
    mult = jnp.sqrt(_neg_expm1(2.0 * log_a))
    mult = jnp.where(grow == 0, 1.0, mult)
    return ub, r, ig, a, mult


def _scan8_fwd(a, b):
    row = _rows(a.shape)
    for k in (1, 2, 4):
        m = row >= k
        b = jnp.where(m, a * pltpu.roll(b, k, 0) + b, b)
        a = jnp.where(m, a * pltpu.roll(a, k, 0), a)
    return a, b


def _scan8_bwd(c, d):
    row = _rows(c.shape)
    for k in (1, 2, 4):
        m = row < SUBLANES - k
        d = jnp.where(m, c * pltpu.roll(d, SUBLANES - k, 0) + d, d)
        c = jnp.where(m, c * pltpu.roll(c, SUBLANES - k, 0), c)
    return c, d


def _pad_copy(dst, src, front, back):
    s, c = src.shape
    if front:
        dst[pl.ds(0, front), :] = jnp.zeros((front, c), dst.dtype)
    if back:
        dst[pl.ds(front + s, back), :] = jnp.zeros((back, c), dst.dtype)
    dst[pl.ds(front, s), :] = src[...].astype(dst.dtype)


def _lru_fwd(name, proj, par, wa, wx):
    S = proj.shape[0]
    R = proj.shape[1] // 2
    H = R // LANES
    ch = _tile(S, SEQ_CHUNK)
    nch = S // ch
    H8 = SUBLANES

    def body(up_ref, y_ref, par_ref, wa_ref, wx_ref, gh_ref, h_ref, up_pad):
        _pad_copy(up_pad, up_ref, H8, 0)
        par = par_ref[...]
        cw, cb, ba, bx = par[0:4], par[4:5], par[5:6], par[6:7]
        sp = _softplus(-par[7:8])
        wa_m, wx_m = wa_ref[...], wx_ref[...]

        def chunk(ci, carry):
            r0 = pl.multiple_of(ci * ch, ch)
            xs = up_pad[pl.ds(r0, ch + H8), :]
            u = _conv_fwd(xs, cw, cb)
            grow = _rows(u.shape) + r0
            _, _, ig, a, mult = _lru_gates(u, wa_m, wx_m, ba, bx, sp, grow)
            bt = mult * (ig * u)
            hs = []
            for j in range(ch // H8):
                aa, bb = _scan8_fwd(a[j * H8:(j + 1) * H8], bt[j * H8:(j + 1) * H8])
                hj = bb + aa * carry
                carry = hj[H8 - 1:H8]
                hs.append(hj)
            h = jnp.concatenate(hs, axis=0)
            h_ref[pl.ds(r0, ch), :] = h
            gy, _ = _gelu(y_ref[pl.ds(r0, ch), :])
            gh_ref[pl.ds(r0, ch), :] = (h * gy).astype(gh_ref.dtype)
            return carry

        lax.fori_loop(0, nch, chunk, jnp.zeros((1, LANES), F32))

    col = lambda off: pl.BlockSpec((S, LANES), lambda h: (0, h + off))
    return pl.pallas_call(
        body,
        name=name,
        grid=(H,),
        in_specs=[col(0), col(H), pl.BlockSpec((8, LANES), lambda h: (0, h)),
                  pl.BlockSpec((None, LANES, LANES), lambda h: (h, 0, 0)),
                  pl.BlockSpec((None, LANES, LANES), lambda h: (h, 0, 0))],
        out_specs=[col(0), col(0)],
        out_shape=[jax.ShapeDtypeStruct((S, R), BF16), jax.ShapeDtypeStruct((S, R), F32)],
        scratch_shapes=[pltpu.VMEM((S + H8, LANES), F32)],
        compiler_params=_cparams("parallel"),
    )(proj, proj, par, wa, wx)


def _lru_bwd(name, proj, h, dgh, par, wa, wx):
    S = proj.shape[0]
    R = proj.shape[1] // 2
    H = R // LANES
    ch = _tile(S, SEQ_CHUNK)
    nch = S // ch
    H8 = SUBLANES
    nb = ch // H8

    def body(up_ref, y_ref, h_ref, dgh_ref, par_ref, wa_ref, wx_ref,
             dup_ref, dy_ref, dwa_ref, dwx_ref, dpar_ref, up_pad, h_pad, du_pad, vec_acc):
        _pad_copy(up_pad, up_ref, H8, 0)
        _pad_copy(h_pad, h_ref, H8, 0)
        du_pad[pl.ds(S, H8), :] = jnp.zeros((H8, LANES), F32)
        par = par_ref[...]
        cw, cb, ba, bx, lam = par[0:4], par[4:5], par[5:6], par[6:7], par[7:8]
        sp = _softplus(-lam)
        wa_m, wx_m = wa_ref[...], wx_ref[...]
        dwa_ref[...] = jnp.zeros_like(dwa_ref)
        dwx_ref[...] = jnp.zeros_like(dwx_ref)
        vec_acc[...] = jnp.zeros_like(vec_acc)
        nt_dims = (((1,), (1,)), ((), ()))
        tn_dims = (((0,), (0,)), ((), ()))

        def chunk(it, carry):
            lam_next, a_next = carry
            ci = nch - 1 - it
            r0 = pl.multiple_of(ci * ch, ch)
            xs = up_pad[pl.ds(r0, ch + H8), :]
            u = _conv_fwd(xs, cw, cb)
            row = _rows(u.shape)
            grow = row + r0
            ub, r, ig, a, mult = _lru_gates(u, wa_m, wx_m, ba, bx, sp, grow)
            hs = h_pad[pl.ds(r0, ch + H8), :]
            hcur = hs[H8:]
            hprev = pltpu.roll(hs, 1, 0)[H8:]
            y = y_ref[pl.ds(r0, ch), :]
            dgh = dgh_ref[pl.ds(r0, ch), :]
            gy, t = _gelu(y)
            dy_ref[pl.ds(r0, ch), :] = (dgh * hcur * _gelu_grad(y, t)).astype(dy_ref.dtype)
            dh = dgh * gy
            c = jnp.where(row == ch - 1, a_next, pltpu.roll(a, ch - 1, 0))
            ls = [None] * nb
            for j in range(nb - 1, -1, -1):
                cc, dd = _scan8_bwd(c[j * H8:(j + 1) * H8], dh[j * H8:(j + 1) * H8])
                lj = dd + cc * lam_next
                lam_next = lj[0:1]
                ls[j] = lj
            lmb = jnp.concatenate(ls, axis=0)
            da = lmb * hprev
            gu = ig * u
            dmult = lmb * gu
            dlog_a = da * a + jnp.where(grow == 0, 0.0, dmult * (-(a * a) / mult))
            dr = dlog_a * ((-LRU_C) * sp)
            drp = dr * r * (1.0 - r)
            dip = (lmb * mult * u) * ig * (1.0 - ig)
            drb, dib = drp.astype(BF16), dip.astype(BF16)
            du = (lmb * mult * ig
                  + lax.dot_general(drb, wa_m, nt_dims, preferred_element_type=F32)
                  + lax.dot_general(dib, wx_m, nt_dims, preferred_element_type=F32))
            du_pad[pl.ds(r0, ch), :] = du
            dwa_ref[...] += lax.dot_general(ub, drb, tn_dims, preferred_element_type=F32)
            dwx_ref[...] += lax.dot_general(ub, dib, tn_dims, preferred_element_type=F32)
            ssum = lambda v: jnp.sum(v, axis=0, keepdims=True)
            vec_acc[0:1, :] += ssum(drp)
            vec_acc[1:2, :] += ssum(dip)
            vec_acc[2:3, :] += ssum(dlog_a * ((-LRU_C) * r))
            return lam_next, a[0:1]

        zero = jnp.zeros((1, LANES), F32)
        lax.fori_loop(0, nch, chunk, (zero, zero))

        def conv_chunk(ci, acc):
            r0 = pl.multiple_of(ci * ch, ch)
            ds = du_pad[pl.ds(r0, ch + H8), :]
            xs = up_pad[pl.ds(r0, ch + H8), :]
            n = ch + H8
            du = ds[:ch]
            dup = cw[3:4] * du
            new = [acc[3] + jnp.sum(du * xs[H8:], axis=0, keepdims=True)]
            for k in (1, 2, 3):
                dup = dup + cw[3 - k:4 - k] * pltpu.roll(ds, n - k, 0)[:ch]
                new.append(acc[3 - k] + jnp.sum(du * pltpu.roll(xs, k, 0)[H8:], axis=0, keepdims=True))
            dup_ref[pl.ds(r0, ch), :] = dup.astype(dup_ref.dtype)
            return (new[3], new[2], new[1], new[0], acc[4] + jnp.sum(du, axis=0, keepdims=True))

        acc = lax.fori_loop(0, nch, conv_chunk, (zero,) * 5)
        dlam = vec_acc[2:3, :] * (-jax.nn.sigmoid(-lam))
        dpar_ref[...] = jnp.concatenate(list(acc) + [vec_acc[0:1, :], vec_acc[1:2, :], dlam], axis=0)

    col = lambda off: pl.BlockSpec((S, LANES), lambda h: (0, h + off))
    head = pl.BlockSpec((None, LANES, LANES), lambda h: (h, 0, 0))
    return pl.pallas_call(
        body,
        name=name,
        grid=(H,),
        in_specs=[col(0), col(H), col(0), col(0), pl.BlockSpec((8, LANES), lambda h: (0, h)), head, head],
        out_specs=[col(0), col(0), head, head, pl.BlockSpec((8, LANES), lambda h: (0, h))],
        out_shape=[jax.ShapeDtypeStruct((S, R), BF16), jax.ShapeDtypeStruct((S, R), BF16),
                   jax.ShapeDtypeStruct((H, LANES, LANES), F32), jax.ShapeDtypeStruct((H, LANES, LANES), F32),
                   jax.ShapeDtypeStruct((8, R), F32)],
        scratch_shapes=[pltpu.VMEM((S + H8, LANES), F32), pltpu.VMEM((S + H8, LANES), F32),
                        pltpu.VMEM((S + H8, LANES), F32), pltpu.VMEM((8, LANES), F32)],
        compiler_params=_cparams("parallel"),
    )(proj, proj, h, dgh, par, wa, wx)


def _window_sum(xs, g, up):
    n = xs.shape[0]
    s = xs
    for lvl, k in enumerate((1, 2, 4, 8)):
        sh = pltpu.roll(s, (n - k) if up else k, 0)
        s = s + jnp.where(g >= lvl, sh, 0.0)
    return s


def _pool_count(grow, g):
    return jnp.minimum(grow + 1, lax.shift_left(jnp.int32(2), g)).astype(F32)


def _pool_fwd(name, u, wgrp, par):
    S, D = u.shape
    G, W = wgrp.shape[0], wgrp.shape[1]
    ch = _tile(S, SEQ_CHUNK)
    nch = S // ch
    PH = POOL_HALO

    def body(u_ref, w_ref, par_ref, zs_ref, u_pad):
        g = pl.program_id(0)
        _pad_copy(u_pad, u_ref, PH, 0)
        par = par_ref[...]
        w = w_ref[...]

        def chunk(ci, _):
            r0 = pl.multiple_of(ci * ch, ch)
            xs = u_pad[pl.ds(r0, ch + PH), :]
            ws = _window_sum(xs, g, False)[PH:]
            uc = xs[PH:]
            cnt = _pool_count(_rows(uc.shape) + r0, g)
            pooled = ws / cnt - uc
            z = jnp.dot(pooled.astype(BF16), w, preferred_element_type=F32) + par[0:1]
            zs_ref[pl.ds(r0, ch), :] = (z * par[1:2]).astype(zs_ref.dtype)
            return 0

        lax.fori_loop(0, nch, chunk, 0)

    return pl.pallas_call(
        body,
        name=name,
        grid=(G,),
        in_specs=[pl.BlockSpec((S, W), lambda g: (0, g)), pl.BlockSpec((None, W, W), lambda g: (g, 0, 0)),
                  pl.BlockSpec((2, W), lambda g: (0, g))],
        out_specs=pl.BlockSpec((S, W), lambda g: (0, g)),
        out_shape=jax.ShapeDtypeStruct((S, D), BF16),
        scratch_shapes=[pltpu.VMEM((S + PH, W), F32)],
        compiler_params=_cparams("parallel"),
    )(u, wgrp, par)


def _pool_bwd(name, u, dzs, wgrp, par):
    S, D = u.shape
    G, W = wgrp.shape[0], wgrp.shape[1]
    ch = _tile(S, SEQ_CHUNK)
    nch = S // ch
    PH = POOL_HALO

    def body(u_ref, dzs_ref, w_ref, par_ref, du_ref, dw_ref, dpar_ref, u_pad, q_pad, dw_acc):
        g = pl.program_id(0)
        _pad_copy(u_pad, u_ref, PH, 0)
        q_pad[pl.ds(S, PH), :] = jnp.zeros((PH, W), F32)
        par = par_ref[...]
        w = w_ref[...]
        dw_acc[...] = jnp.zeros_like(dw_acc)

        def chunk(ci, acc):
            db, dsc = acc
            r0 = pl.multiple_of(ci * ch, ch)
            xs = u_pad[pl.ds(r0, ch + PH), :]
            ws = _window_sum(xs, g, False)[PH:]
            uc = xs[PH:]
            cnt = _pool_count(_rows(uc.shape) + r0, g)
            pooled = (ws / cnt - uc).astype(BF16)
            z = jnp.dot(pooled, w, preferred_element_type=F32) + par[0:1]
            dzs = dzs_ref[pl.ds(r0, ch), :]
            dz = dzs * par[1:2]
            dzb = dz.astype(BF16)
            dw_acc[...] += lax.dot_general(pooled, dzb, (((0,), (0,)), ((), ())), preferred_element_type=F32)
            dpooled = lax.dot_general(dzb, w, (((1,), (1,)), ((), ())), preferred_element_type=F32)
            q_pad[pl.ds(r0, ch), :] = dpooled / cnt
            return (db + jnp.sum(dz, axis=0, keepdims=True), dsc + jnp.sum(dzs * z, axis=0, keepdims=True))

        zero = jnp.zeros((1, W), F32)
        db, dsc = lax.fori_loop(0, nch, chunk, (zero, zero))
        dpar_ref[...] = jnp.concatenate([db, dsc], axis=0)
        dw_ref[...] = dw_acc[...].astype(dw_ref.dtype)

        def back(ci, _):
            r0 = pl.multiple_of(ci * ch, ch)
            qs = q_pad[pl.ds(r0, ch + PH), :]
            ws = _window_sum(qs, g, True)[:ch]
            qc = qs[:ch]
            cnt = _pool_count(_rows(qc.shape) + r0, g)
            du_ref[pl.ds(r0, ch), :] = (ws - qc * cnt).astype(du_ref.dtype)
            return 0

        lax.fori_loop(0, nch, back, 0)

    blk = pl.BlockSpec((S, W), lambda g: (0, g))
    wspec = pl.BlockSpec((None, W, W), lambda g: (g, 0, 0))
    pspec = pl.BlockSpec((2, W), lambda g: (0, g))
    return pl.pallas_call(
        body,
        name=name,
        grid=(G,),
        in_specs=[blk, blk, wspec, pspec],
        out_specs=[blk, wspec, pspec],
        out_shape=[jax.ShapeDtypeStruct((S, D), BF16), jax.ShapeDtypeStruct((G, W, W), BF16),
                   jax.ShapeDtypeStruct((2, D), F32)],
        scratch_shapes=[pltpu.VMEM((S + PH, W), F32), pltpu.VMEM((S + PH, W), F32), pltpu.VMEM((W, W), F32)],
        compiler_params=_cparams("parallel"),
    )(u, dzs, wgrp, par)


def _my_place():
    x, y, c = lax.axis_index("x"), lax.axis_index("y"), lax.axis_index("c")
    return x, y, c, 4 * x + 2 * y + c


def _peers(x, y, c):
    out = []
    for d in range(1, N_DEV):
        px = 1 - x if d & 4 else x
        py = 1 - y if d & 2 else y
        pc = 1 - c if d & 1 else c
        out.append(((px, py, pc), 4 * px + 2 * py + pc))
    return out


def _window(ref, axis, start, size):
    idx = [slice(None)] * len(ref.shape)
    idx[axis] = pl.ds(start, size)
    return ref.at[tuple(idx)]


def _to_bf16(name, arrs):
    outs = []
    for i, a in enumerate(arrs):
        a2 = a.reshape(-1, a.shape[-1])
        tr = _tile(a2.shape[0], 512)
        o = pl.pallas_call(
            lambda a_ref, o_ref: o_ref.__setitem__(Ellipsis, a_ref[...].astype(BF16)),
            name=f"{name}_{i}",
            grid=(a2.shape[0] // tr,),
            in_specs=[pl.BlockSpec((tr, a2.shape[1]), lambda r: (r, 0))],
            out_specs=pl.BlockSpec((tr, a2.shape[1]), lambda r: (r, 0)),
            out_shape=jax.ShapeDtypeStruct(a2.shape, BF16),
            compiler_params=_cparams("parallel"),
        )(a2)
        outs.append(o.reshape(a.shape))
    return outs


def _all_gather(name, shards, axes):
    n = len(shards)
    sizes = [s.shape[ax] for s, ax in zip(shards, axes)]

    def body(*refs):
        ins, outs = refs[:n], refs[n:2 * n]
        send, recv, loc = refs[2 * n:]
        x, y, c, me = _my_place()
        peers = _peers(x, y, c)
        local = []
        for i in range(n):
            dst = _window(outs[i], axes[i], me * sizes[i], sizes[i])
            cp = pltpu.make_async_copy(ins[i], dst, loc.at[i])
            cp.start()
            local.append(cp)
            for peer, _ in peers:
                pltpu.make_async_remote_copy(src_ref=ins[i], dst_ref=dst, send_sem=send.at[i], recv_sem=recv.at[i],
                                             device_id=peer, device_id_type=MESH).start()
        for i in range(n):
            local[i].wait()
            seven = _window(outs[i], axes[i], 0, (N_DEV - 1) * sizes[i])
            pltpu.make_async_remote_copy(src_ref=seven, dst_ref=seven, send_sem=send.at[i], recv_sem=recv.at[i],
                                         device_id=(x, y, c), device_id_type=MESH).wait()

    def full_shape(s, ax):
        shp = list(s.shape)
        shp[ax] *= N_DEV
        return jax.ShapeDtypeStruct(tuple(shp), s.dtype)

    any_spec = pl.BlockSpec(memory_space=pl.ANY)
    return pl.pallas_call(
        body,
        name=name,
        in_specs=[any_spec] * n,
        out_specs=[any_spec] * n,
        out_shape=[full_shape(s, ax) for s, ax in zip(shards, axes)],
        scratch_shapes=[pltpu.SemaphoreType.DMA((n,)), pltpu.SemaphoreType.DMA((n,)), pltpu.SemaphoreType.DMA((n,))],
        compiler_params=pltpu.CompilerParams(has_side_effects=True),
    )(*shards)


HBM_SPEC = pl.BlockSpec(memory_space=pltpu.HBM)
SEM_SPEC = pl.BlockSpec(memory_space=pltpu.SEMAPHORE)
SPLIT_EFFECT = pltpu.SideEffectType.DATAFLOW_SIDE_EFFECTING


def _push_all(kind, src, dst, axis, size, send_sem, recv_sem, place):
    x, y, c, me = place
    for peer, pidx in _peers(x, y, c):
        if kind == "gather":
            s = d = _window(dst, axis, me * size, size)
        else:
            s, d = _window(src, axis, pidx * size, size), dst.at[me]
        pltpu.make_async_remote_copy(src_ref=s, dst_ref=d, send_sem=send_sem, recv_sem=recv_sem, device_id=peer,
                                     device_id_type=MESH).start()


def _drain_all(kind, dst, axis, size, send_sem, recv_sem, place):
    x, y, c, _ = place
    seven = _window(dst, axis, 0, (N_DEV - 1) * size) if kind == "gather" else dst.at[pl.ds(0, N_DEV - 1)]
    pltpu.make_async_remote_copy(src_ref=seven, dst_ref=seven, send_sem=send_sem, recv_sem=recv_sem,
                                 device_id=(x, y, c), device_id_type=MESH).wait()


def _own_block_placed(src, axis, size, me):
    own = lax.dynamic_slice_in_dim(src, me * size, size, axis)
    return lax.dynamic_update_slice_in_dim(lax.empty((N_DEV,) + own.shape, src.dtype), own[None], me, 0)


def _split_start(name, kind, srcs, lands, axes, sizes, after=()):
    n, ns, na = len(lands), len(srcs), len(after)

    def body(*refs):
        src_refs, land_refs = refs[:ns], refs[ns:ns + n]
        send, recv = refs[ns + n + na], refs[ns + n + na + 1]
        token = refs[-1]
        place = _my_place()
        for k in range(n):
            _push_all(kind, src_refs[k] if ns else None, land_refs[k], axes[k], sizes[k], send.at[k], recv.at[k], place)
        token[...] = jnp.zeros_like(token)

    hbm = lambda a: pltpu.HBM(a.shape, a.dtype)
    res = pl.pallas_call(
        body,
        name=name,
        out_shape=(pltpu.SemaphoreType.DMA((n,)), pltpu.SemaphoreType.DMA((n,)), *[hbm(a) for a in srcs],
                   *[hbm(a) for a in lands], jax.ShapeDtypeStruct((SUBLANES, LANES), F32)),
        in_specs=[HBM_SPEC] * (ns + n) + [pl.BlockSpec(memory_space=pl.ANY)] * na,
        out_specs=(SEM_SPEC, SEM_SPEC, *[HBM_SPEC] * (ns + n), pl.BlockSpec(memory_space=pltpu.VMEM)),
        input_output_aliases={k: 2 + k for k in range(ns + n)},
        compiler_params=pltpu.CompilerParams(has_side_effects=SPLIT_EFFECT),
    )(*[pltpu.with_memory_space_constraint(a, pltpu.HBM) for a in (*srcs, *lands)], *after)
    return res[0], res[1], list(res[2:2 + ns]), list(res[2 + ns:2 + ns + n]), res[-1]


def _split_wait(name, kind, handle, axes, sizes, after):
    send, recv, srcs, lands, _ = handle
    n, ns = len(lands), len(srcs)
    after = list(after) if isinstance(after, (list, tuple)) else [after]

    def body(*refs):
        land_refs = refs[ns:ns + n]
        send_ref, recv_ref = refs[ns + n], refs[ns + n + 1]
        place = _my_place()
        for k in range(n):
            _drain_all(kind, land_refs[k], axes[k], sizes[k], send_ref.at[k], recv_ref.at[k], place)

    hbm = lambda a: pltpu.HBM(a.shape, a.dtype)
    res = pl.pallas_call(
        body,
        name=name,
        out_shape=tuple(hbm(a) for a in (*srcs, *lands)),
        in_specs=[HBM_SPEC] * (ns + n) + [SEM_SPEC, SEM_SPEC] + [pl.BlockSpec(memory_space=pl.ANY)] * len(after),
        out_specs=tuple([HBM_SPEC] * (ns + n)),
        input_output_aliases={k: k for k in range(ns + n)},
        compiler_params=pltpu.CompilerParams(has_side_effects=SPLIT_EFFECT),
    )(*srcs, *lands, send, recv, *after)
    return list(res[ns:])


def _cast_into_window(name, a, l, axis, me1):
    shp = a.shape[1:]
    cast = lambda me_ref, a_ref, o_ref: o_ref.__setitem__(Ellipsis, a_ref[...].astype(BF16))
    if len(shp) == 3:
        assert axis == 1
        G, r, c = shp
        full = (G, r * N_DEV, c)
        grid = (G,)
        in_spec = pl.BlockSpec((None, None, r, c), lambda g, me: (l, g, 0, 0))
        out_spec = pl.BlockSpec((None, r, c), lambda g, me: (g, me[0], 0))
    else:
        r, c = shp
        tr = _tile(r, 512)
        nb = r // tr
        grid = (nb,)
        in_spec = pl.BlockSpec((None, tr, c), lambda i, me: (l, i, 0))
        if axis == 0:
            full = (r * N_DEV, c)
            out_spec = pl.BlockSpec((tr, c), lambda i, me: (me[0] * nb + i, 0))
        else:
            full = (r, c * N_DEV)
            out_spec = pl.BlockSpec((tr, c), lambda i, me: (i, me[0]))
    return pl.pallas_call(
        cast,
        name=name,
        grid_spec=pltpu.PrefetchScalarGridSpec(num_scalar_prefetch=1, grid=grid, in_specs=[in_spec], out_specs=out_spec),
        out_shape=jax.ShapeDtypeStruct(full, BF16),
        compiler_params=_cparams("arbitrary"),
    )(me1, a)


def _adamw_math(w, g, m, v):
    m = ADAM_B1 * m + (1.0 - ADAM_B1) * g
    v = ADAM_B2 * v + (1.0 - ADAM_B2) * jnp.square(g)
    m_hat = m / (1.0 - ADAM_B1 ** ADAM_STEP)
    v_hat = v / (1.0 - ADAM_B2 ** ADAM_STEP)
    delta = -ADAM_LR * (m_hat / (jnp.sqrt(v_hat) + ADAM_EPS) + ADAM_WD * w)
    return delta, m, v


def _sum_slots(buf_ref):
    g = buf_ref[0].astype(F32)
    for s in range(1, N_DEV):
        g = g + buf_ref[s].astype(F32)
    return g


def _adamw_layer(name, buf, w, m, v, l, prev):
    shape = w.shape
    L, C = shape[0], shape[-1]
    Rr = math.prod(shape[1:-1])
    buf3 = buf.reshape(N_DEV, Rr, C)
    w3, m3, v3 = (t.reshape(L, Rr, C) for t in (w, m, v))
    tr = _tile(Rr, 2 * LANES) if Rr % LANES == 0 else Rr
    n_prev = 0 if prev is None else 4

    def body(buf_ref, w_ref, m_ref, v_ref, *rest):
        g_out, d_out, m_out, v_out = rest[n_prev:]
        g = _sum_slots(buf_ref)
        d, mm, vv = _adamw_math(w_ref[...], g, m_ref[...], v_ref[...])
        g_out[...] = g
        d_out[...] = d
        m_out[...] = mm
        v_out[...] = vv

    spec = pl.BlockSpec((None, tr, C), lambda r: (l, r, 0))
    outs = pl.pallas_call(
        body,
        name=name,
        grid=(Rr // tr,),
        in_specs=[pl.BlockSpec((N_DEV, tr, C), lambda r: (0, r, 0)), spec, spec, spec]
        + [pl.BlockSpec(memory_space=pl.ANY)] * n_prev,
        out_specs=[spec] * 4,
        out_shape=[jax.ShapeDtypeStruct((L, Rr, C), F32)] * 4,
        input_output_aliases={4 + k: k for k in range(n_prev)},
        compiler_params=_cparams("parallel"),
    )(buf3, w3, m3, v3, *(prev or ()))
    return list(outs)


def _sum8(name, buf):
    R = buf.shape[1]

    def body(buf_ref, o_ref):
        o_ref[...] = _sum_slots(buf_ref)

    return pl.pallas_call(
        body,
        name=name,
        in_specs=[pl.BlockSpec(buf.shape, lambda: (0, 0, 0))],
        out_specs=pl.BlockSpec((R, LANES), lambda: (0, 0)),
        out_shape=jax.ShapeDtypeStruct((R, LANES), F32),
        compiler_params=_cparams(),
    )(buf)


def _adamw_packed(name, w, g, m, v):
    R = w.shape[0]
    tr = _row_tile(R, 512)

    def body(w_ref, g_ref, m_ref, v_ref, d_out, m_out, v_out):
        d, mm, vv = _adamw_math(w_ref[...], g_ref[...], m_ref[...], v_ref[...])
        d_out[...] = d
        m_out[...] = mm
        v_out[...] = vv

    spec = pl.BlockSpec((tr, LANES), lambda r: (r, 0))
    return pl.pallas_call(
        body,
        name=name,
        grid=(R // tr,),
        in_specs=[spec] * 4,
        out_specs=[spec] * 3,
        out_shape=[jax.ShapeDtypeStruct((R, LANES), F32)] * 3,
        compiler_params=_cparams("parallel"),
    )(w, g, m, v)


def _pack(arrs, pad_rows_to=SUBLANES):
    parts = []
    for a in arrs:
        flat = a.reshape(-1)
        per = LANES * pad_rows_to
        padded = -(-flat.shape[0] // per) * per
        if padded != flat.shape[0]:
            flat = jnp.pad(flat, (0, padded - flat.shape[0]))
        parts.append(flat.reshape(-1, LANES))
    return jnp.concatenate(parts, axis=0)


def _unpack(packed, shapes, pad_rows_to=SUBLANES):
    out = []
    r = 0
    for shp in shapes:
        nel = math.prod(shp)
        per = LANES * pad_rows_to
        rows = -(-nel // per) * pad_rows_to
        out.append(packed[r:r + rows].reshape(-1)[:nel].reshape(shp))
        r += rows
    return out


BIG = ("lru_w_in", "lru_w_out", "pool_w_in", "pool_w_grp", "pool_w_out", "mlp_w1", "mlp_w2", "ple_w", "ple_gate_w")
BIG_AXIS = {"lru_w_in": 2, "lru_w_out": 1, "pool_w_in": 1, "pool_w_grp": 2, "pool_w_out": 1, "mlp_w1": 2,
            "mlp_w2": 1, "ple_w": 2, "ple_gate_w": 1}
SMALL_SHARDED = ("lru_conv_w", "pool_b_grp", "pool_scale")
REPLICATED = ("lru_conv_b", "lru_wa", "lru_ba", "lru_wx", "lru_bx", "lru_lambda", "ln_mix_g", "ln_mix_b",
              "ln_mlp_g", "ln_mlp_b", "ple_gate_b")
WEIGHTS = ("lru_w_in", "lru_conv_w", "lru_conv_b", "lru_wa", "lru_ba", "lru_wx", "lru_bx", "lru_lambda", "lru_w_out",
           "pool_w_in", "pool_w_grp", "pool_b_grp", "pool_scale", "pool_w_out", "ln_mix_g", "ln_mix_b", "mlp_w1",
           "mlp_w2", "ln_mlp_g", "ln_mlp_b", "ple_w", "ple_gate_w", "ple_gate_b")
INPUTS = ("x", "p") + WEIGHTS + ("loss_target",) + tuple("m_" + n for n in WEIGHTS) + tuple("v_" + n for n in WEIGHTS)


def _gather_last_axis(packed_full, shard_shape):
    nel = math.prod(shard_shape)
    blocks = packed_full.reshape(N_DEV, -1)[:, :nel].reshape((N_DEV,) + tuple(shard_shape))
    return jnp.concatenate([blocks[d] for d in range(N_DEV)], axis=-1)


def kernel(x, p, lru_w_in, lru_conv_w, lru_conv_b, lru_wa, lru_ba, lru_wx, lru_bx, lru_lambda, lru_w_out, pool_w_in, pool_w_grp, pool_b_grp, pool_scale, pool_w_out, ln_mix_g, ln_mix_b, mlp_w1, mlp_w2, ln_mlp_g, ln_mlp_b, ple_w, ple_gate_w, ple_gate_b, loss_target, m_lru_w_in, m_lru_conv_w, m_lru_conv_b, m_lru_wa, m_lru_ba, m_lru_wx, m_lru_bx, m_lru_lambda, m_lru_w_out, m_pool_w_in, m_pool_w_grp, m_pool_b_grp, m_pool_scale, m_pool_w_out, m_ln_mix_g, m_ln_mix_b, m_mlp_w1, m_mlp_w2, m_ln_mlp_g, m_ln_mlp_b, m_ple_w, m_ple_gate_w, m_ple_gate_b, v_lru_w_in, v_lru_conv_w, v_lru_conv_b, v_lru_wa, v_lru_ba, v_lru_wx, v_lru_bx, v_lru_lambda, v_lru_w_out, v_pool_w_in, v_pool_w_grp, v_pool_b_grp, v_pool_scale, v_pool_w_out, v_ln_mix_g, v_ln_mix_b, v_mlp_w1, v_mlp_w2, v_ln_mlp_g, v_ln_mlp_b, v_ple_w, v_ple_gate_w, v_ple_gate_b):
    A = dict(zip(INPUTS, (x, p, lru_w_in, lru_conv_w, lru_conv_b, lru_wa, lru_ba, lru_wx, lru_bx, lru_lambda, lru_w_out, pool_w_in, pool_w_grp, pool_b_grp, pool_scale, pool_w_out, ln_mix_g, ln_mix_b, mlp_w1, mlp_w2, ln_mlp_g, ln_mlp_b, ple_w, ple_gate_w, ple_gate_b, loss_target, m_lru_w_in, m_lru_conv_w, m_lru_conv_b, m_lru_wa, m_lru_ba, m_lru_wx, m_lru_bx, m_lru_lambda, m_lru_w_out, m_pool_w_in, m_pool_w_grp, m_pool_b_grp, m_pool_scale, m_pool_w_out, m_ln_mix_g, m_ln_mix_b, m_mlp_w1, m_mlp_w2, m_ln_mlp_g, m_ln_mlp_b, m_ple_w, m_ple_gate_w, m_ple_gate_b, v_lru_w_in, v_lru_conv_w, v_lru_conv_b, v_lru_wa, v_lru_ba, v_lru_wx, v_lru_bx, v_lru_lambda, v_lru_w_out, v_pool_w_in, v_pool_w_grp, v_pool_b_grp, v_pool_scale, v_pool_w_out, v_ln_mix_g, v_ln_mix_b, v_mlp_w1, v_mlp_w2, v_ln_mlp_g, v_ln_mlp_b, v_ple_w, v_ple_gate_w, v_ple_gate_b)))
    depth = ln_mix_g.shape[0]
    alpha = (2 * depth) ** 0.25
    S, D = x.shape[1], x.shape[2]
    xs = x.reshape(S, D)
    tgt = loss_target.reshape(S, D)
    p3 = p.reshape(depth, S, p.shape[-1])
    me = 4 * lax.axis_index("x") + 2 * lax.axis_index("y") + lax.axis_index("c")

    def layer_weights(i):
        s = i // 2
        mixer = ("lru_w_in", "lru_w_out") if i % 2 == 0 else ("pool_w_in", "pool_w_grp", "pool_w_out")
        return [(n, s) for n in mixer] + [(n, i) for n in ("mlp_w1", "mlp_w2", "ple_w", "ple_gate_w")]

    def axis_of(key):
        return 0 if key[0] == "small" else BIG_AXIS[key[0]] - 1

    def start_gather(tag, keys, after):
        axes = [axis_of(k) for k in keys]
        lands = [land[k] for k in keys]
        sizes = [a.shape[ax] // N_DEV for a, ax in zip(lands, axes)]
        return keys, _split_start(f"gather_{tag}_start", "gather", [], lands, axes, sizes, after=after), axes, sizes

    def finish_gather(tag, pending, after):
        keys, handle, axes, sizes = pending
        for (n, l), full in zip(keys, _split_wait(f"gather_{tag}_wait", "gather", handle, axes, sizes, after)):
            W[n][l] = full

    def start_exchange(tag, keys, arrs, after):
        axes = [axis_of(k) for k in keys]
        sizes = [a.shape[ax] // N_DEV for a, ax in zip(arrs, axes)]
        lands = [_own_block_placed(a, ax, sz, me) for a, ax, sz in zip(arrs, axes, sizes)]
        return keys, _split_start(f"exchange_{tag}_start", "scatter", arrs, lands, axes, sizes, after=after), axes, sizes

    def finish_exchange(tag, pending, after):
        keys, handle, axes, sizes = pending
        partial.update(zip(keys, _split_wait(f"exchange_{tag}_wait", "scatter", handle, axes, sizes, after)))

    me1 = jnp.reshape(me, (1,)).astype(jnp.int32)
    land = {k: _cast_into_window(f"cast_{k[0]}_{k[1]}", A[k[0]], k[1], axis_of(k), me1)
            for i in range(depth) for k in layer_weights(i)}
    W = {n: [None] * A[n].shape[0] for n in BIG}
    small_shard_shapes = [A[n].shape for n in SMALL_SHARDED]
    gathered = _all_gather("gather_small_params", [_pack([A[n] for n in SMALL_SHARDED])], [0])
    def gather_groups(i):
        keys = layer_weights(i)
        mixer, (w1, w2, pw, pg) = keys[:-4], keys[-4:]
        if i == 0:
            return [("in", mixer[:1]), ("out", mixer[1:]), ("up", [w1]), ("rest", [w2, pw, pg])]
        return [("in", mixer + [w1]), ("rest", [w2, pw, pg])]

    gather_pending = {}
    token = gathered[0]
    for i in range(depth):
        for tag, keys in gather_groups(i):
            gather_pending[(i, tag)] = start_gather(f"l{i}_{tag}", keys, (token,))
            token = gather_pending[(i, tag)][1][4]
        if i == 0:
            finish_gather("l0_in", gather_pending[(0, "in")], token)
    all_started = token
    small_full = gathered[0].reshape(N_DEV, -1, LANES)
    r = 0
    for n, shp in zip(SMALL_SHARDED, small_shard_shapes):
        rows = -(-math.prod(shp) // (LANES * SUBLANES)) * SUBLANES
        W[n] = _gather_last_axis(small_full[:, r:r + rows], shp)
        r += rows
    wa_b, wx_b = _to_bf16("cast_gates", [lru_wa, lru_wx])
    n_lru = lru_w_in.shape[0]
    lru_par = [jnp.concatenate([W["lru_conv_w"][s], lru_conv_b[s][None], lru_ba[s][None], lru_bx[s][None],
                                lru_lambda[s][None]], axis=0) for s in range(n_lru)]
    pool_par = [jnp.stack([W["pool_b_grp"][s], W["pool_scale"][s]], axis=0) for s in range(pool_w_in.shape[0])]

    saved = []
    h_in = xs
    h_in_b, p3b = _to_bf16("cast_inputs", [xs, p3])
    for i in range(depth):
        s = i // 2
        sv = {"x0": h_in, "x0b": h_in_b}
        if i > 0:
            finish_gather(f"l{i}_in", gather_pending[(i, "in")], h_in)
        if i % 2 == 0:
            sv["proj"] = _mm(f"l{i}_lru_in", h_in_b, W["lru_w_in"], "nn", [F32], b_lead=s)
            sv["gh"], sv["h"] = _lru_fwd(f"l{i}_lru_core", sv["proj"], lru_par[s], wa_b[s], wx_b[s])
            if i == 0:
                finish_gather("l0_out", gather_pending[(0, "out")], [sv["gh"], all_started])
            sv["mix"] = _mm(f"l{i}_lru_out", sv["gh"], W["lru_w_out"], "nn", [F32], b_lead=s)
        else:
            sv["u"] = _mm(f"l{i}_pool_in", h_in_b, W["pool_w_in"], "nn", [F32], b_lead=s)
            sv["zs"] = _pool_fwd(f"l{i}_pool_core", sv["u"], W["pool_w_grp"][s], pool_par[s])
            sv["mix"] = _mm(f"l{i}_pool_out", sv["zs"], W["pool_w_out"], "nn", [F32], b_lead=s)
        sv["x1"], sv["x1b"] = _ln_fwd(f"l{i}_ln_mix", alpha, h_in, sv["mix"], ln_mix_g[i][None], ln_mix_b[i][None])
        if i == 0:
            finish_gather("l0_up", gather_pending[(0, "up")], sv["x1b"])
        sv["hpre"], sv["hact"] = _mm(f"l{i}_mlp_up", sv["x1b"], W["mlp_w1"], "nn", [BF16, BF16], b_lead=i,
                                     epi=lambda acc: (acc, jnp.square(jnp.maximum(acc, 0.0))))
        finish_gather(f"l{i}_rest", gather_pending[(i, "rest")], sv["hact"])
        sv["mo"] = _mm(f"l{i}_mlp_down", sv["hact"], W["mlp_w2"], "nn", [F32], b_lead=i)
        sv["x2"], sv["x2b"] = _ln_fwd(f"l{i}_ln_mlp", alpha, sv["x1"], sv["mo"], ln_mlp_g[i][None], ln_mlp_b[i][None])
        sv["pp"] = _mm(f"l{i}_ple_up", p3b, W["ple_w"], "nn", [F32], a_lead=i, b_lead=i)

        def ple_epi(acc, bg, x2t, ppt):
            gpre = acc + bg
            x3 = x2t + ppt * jax.nn.sigmoid(gpre)
            return x3, x3, gpre

        h_in, h_in_b, sv["gpre"] = _mm(f"l{i}_ple_gate", sv["x2b"], W["ple_gate_w"], "nn", [F32, BF16, F32], b_lead=i,
                                       epi=ple_epi, extras=[ple_gate_b[i][None], sv["x2"], sv["pp"]], tn=MM_TN)
        saved.append(sv)

    dx, sq = _loss_and_grad("loss", h_in, tgt)
    loss = lax.psum(0.5 * sq[0, 0] / D, ("x", "y", "c"))

    dW = {n: [None] * A[n].shape[0] for n in BIG}
    dsmall = {n: [None] * A[n].shape[0] for n in REPLICATED + SMALL_SHARDED}
    small_names = REPLICATED + SMALL_SHARDED
    partial = {}
    exchange_pending = {}
    exchange_token = ()
    for i in reversed(range(depth)):
        s = i // 2
        sv = saved[i]
        dpp, dgpre, dbg = _ple_bwd(f"l{i}_ple_bwd", dx, sv["gpre"], sv["pp"], after=exchange_token)
        dsmall["ple_gate_b"][i] = dbg[0]
        dW["ple_w"][i] = _mm(f"l{i}_d_ple_w", p3b, dpp, "tn", [BF16], a_lead=i)
        dW["ple_gate_w"][i] = _mm(f"l{i}_d_ple_gate_w", sv["x2b"], dgpre, "tn", [BF16])
        dx2 = _mm(f"l{i}_d_x2", dgpre, W["ple_gate_w"], "nt", [F32], b_lead=i, extras=[dx],
                  epi=lambda acc, d: (acc + d,))
        dz2, dz2b, dg, db = _ln_bwd(f"l{i}_ln_mlp_bwd", alpha, dx2, sv["x1"], sv["mo"], ln_mlp_g[i][None])
        dsmall["ln_mlp_g"][i], dsmall["ln_mlp_b"][i] = dg[0], db[0]
        dhpre = _mm(f"l{i}_d_hpre", dz2b, W["mlp_w2"], "nt", [BF16], b_lead=i, extras=[sv["hpre"]],
                    epi=lambda acc, hp: (acc * (2.0 * jnp.maximum(hp.astype(F32), 0.0)),))
        dW["mlp_w2"][i] = _mm(f"l{i}_d_mlp_w2", sv["hact"], dz2b, "tn", [BF16])
        dW["mlp_w1"][i] = _mm(f"l{i}_d_mlp_w1", sv["x1b"], dhpre, "tn", [BF16])
        mlp_after = ()
        if i == 0:
            early = [(n, 0) for n in ("ple_w", "ple_gate_w", "mlp_w2", "mlp_w1")]
            exchange_early0 = start_exchange("early0", early, [dW[n][l] for n, l in early], ())
            mlp_after = (exchange_early0[1][4],)
        dx1 = _mm(f"l{i}_d_x1", dhpre, W["mlp_w1"], "nt", [F32], b_lead=i, extras=[dz2],
                  epi=lambda acc, d: (acc + alpha * d,), after=mlp_after)
        dz1, dz1b, dg, db = _ln_bwd(f"l{i}_ln_mix_bwd", alpha, dx1, sv["x0"], sv["mix"], ln_mix_g[i][None])
        dsmall["ln_mix_g"][i], dsmall["ln_mix_b"][i] = dg[0], db[0]
        if i % 2 == 0:
            dW["lru_w_out"][s] = _mm(f"l{i}_d_lru_w_out", sv["gh"], dz1b, "tn", [BF16])
            dgh = _mm(f"l{i}_d_gh", dz1b, W["lru_w_out"], "nt", [F32], b_lead=s)
            dup, dy, dwa, dwx, dpar = _lru_bwd(f"l{i}_lru_core_bwd", sv["proj"], sv["h"], dgh, lru_par[s],
                                               wa_b[s], wx_b[s])
            dsmall["lru_wa"][s], dsmall["lru_wx"][s] = dwa, dwx
            dsmall["lru_conv_w"][s] = dpar[0:4]
            for k, n in enumerate(("lru_conv_b", "lru_ba", "lru_bx", "lru_lambda")):
                dsmall[n][s] = dpar[4 + k]
            dmix_in = jnp.concatenate([dup, dy], axis=1)
            win = "lru_w_in"
        else:
            dW["pool_w_out"][s] = _mm(f"l{i}_d_pool_w_out", sv["zs"], dz1b, "tn", [BF16])
            dzs = _mm(f"l{i}_d_zs", dz1b, W["pool_w_out"], "nt", [F32], b_lead=s)
            dmix_in, dW["pool_w_grp"][s], dpar = _pool_bwd(f"l{i}_pool_core_bwd", sv["u"], dzs, W["pool_w_grp"][s],
                                                          pool_par[s])
            dsmall["pool_b_grp"][s], dsmall["pool_scale"][s] = dpar[0], dpar[1]
            win = "pool_w_in"
        dW[win][s] = _mm(f"l{i}_d_{win}", sv["x0b"], dmix_in, "tn", [BF16])
        x0_after = ()
        if i > 0:
            keys = layer_weights(i)
            exchange_pending[i] = start_exchange(f"l{i}", keys, [dW[n][l] for n, l in keys], ())
            exchange_token = (exchange_pending[i][1][4],)
        else:
            small_grads = [jnp.stack(dsmall[n]) for n in small_names]
            small_shapes = [g.shape for g in small_grads]
            packed_g = _pack(small_grads)
            assert packed_g.shape[0] % (N_DEV * SUBLANES) == 0, packed_g.shape
            late = [("lru_w_out", 0), ("lru_w_in", 0), ("small", 0)]
            exchange_late0 = start_exchange("late0", late, [dW["lru_w_out"][0], dW["lru_w_in"][0], packed_g], ())
            x0_after = (exchange_late0[1][4],)
        dx = _mm(f"l{i}_d_x0", dmix_in, W[win], "nt", [F32], b_lead=s, extras=[dz1],
                 epi=lambda acc, d: (acc + alpha * d,), after=x0_after)
    grad_x = dx.reshape(x.shape)

    for i in range(1, depth):
        finish_exchange(f"l{i}", exchange_pending[i], x0_after[0])
    stacked = {n: None for n in BIG}
    layer0 = layer_weights(0)

    def adamw(n, l):
        stacked[n] = _adamw_layer(f"adamw_{n}_{l}", partial[(n, l)], A[n], A["m_" + n], A["v_" + n], l, stacked[n])

    for n in BIG:
        for l in reversed(range(A[n].shape[0])):
            if (n, l) not in layer0:
                adamw(n, l)
    behind = [dx] + [stacked[n][0] for n in BIG if stacked[n] is not None]
    finish_exchange("early0", exchange_early0, behind)
    finish_exchange("late0", exchange_late0, behind)
    for n, l in layer0:
        adamw(n, l)
    outs = {n: [o.reshape(A[n].shape) for o in stacked[n]] for n in BIG}
    red = _sum8("sum_small", partial[("small", 0)])
    red_full = _all_gather("gather_small", [red], [0])[0]
    small_g = dict(zip(small_names, _unpack(red_full, small_shapes)))
    for n in SMALL_SHARDED:
        width = A[n].shape[-1]
        small_g[n] = lax.dynamic_slice_in_dim(small_g[n], me * width, width, axis=small_g[n].ndim - 1)
    pk = lambda pre: _pack([A[pre + n] for n in small_names])
    d_p, m_p, v_p = _adamw_packed("adamw_small", pk(""), _pack([small_g[n] for n in small_names]), pk("m_"), pk("v_"))
    shapes = [A[n].shape for n in small_names]
    for n, d_, m_, v_ in zip(small_names, _unpack(d_p, shapes), _unpack(m_p, shapes), _unpack(v_p, shapes)):
        outs[n] = [small_g[n], d_, m_, v_]

    return (loss, grad_x, *[outs[n][0] for n in WEIGHTS], *[outs[n][1] for n in WEIGHTS],
            *[outs[n][2] for n in WEIGHTS], *[outs[n][3] for n in WEIGHTS])
```

```python
import functools
import math

import jax
import jax.numpy as jnp
from jax import lax
from jax.experimental import pallas as pl
from jax.experimental.pallas import tpu as pltpu

F32 = jnp.float32
BF16 = jnp.bfloat16
MESH = pl.DeviceIdType.MESH
N_DEV = 8
LANES = 128
SUBLANES = 8

LN_EPS = 1e-5
LRU_C = 8.0
CONV_WIDTH = 4
POOL_HALO = 16
ADAM_LR = 0.001
ADAM_B1 = 0.9
ADAM_B2 = 0.999
ADAM_EPS = 1e-08
ADAM_WD = 0.01
ADAM_STEP = 10

VMEM_LIMIT = 48 * 1024 * 1024
SEQ_CHUNK = 256
MM_TK = 4096
MM_TK_TOKENS = 4096
MM_TM_ROWS = 512
MM_TN = 512
MM_TN_WIDE = 1024
MM_TN_WIDE_MAX_K = 2048
GELU_C0 = math.sqrt(2.0 / math.pi)
GELU_C1 = 0.044715


def _cparams(*sem):
    return pltpu.CompilerParams(dimension_semantics=tuple(sem) if sem else None, vmem_limit_bytes=VMEM_LIMIT)


def _tile(n, pref):
    if n <= pref:
        return n
    t = pref - pref % LANES
    while t >= LANES:
        if n % t == 0:
            return t
        t -= LANES
    return n


def _row_tile(n, pref):
    if n <= pref:
        return n
    t = pref - pref % SUBLANES
    while t >= SUBLANES:
        if n % t == 0:
            return t
        t -= SUBLANES
    return n


def _mm(name, a, b, mode, out_dtypes, epi=None, extras=(), a_lead=None, b_lead=None, tm=1024, tn=None, tk=None,
        after=(), n_sums=0):
    if isinstance(b, (list, tuple)):
        b, b_lead = b[b_lead], None
    a2 = a.shape[-2:]
    b2 = b.shape[-2:]
    if mode == "nn":
        (M, K), N = a2, b2[1]
        assert b2[0] == K
    elif mode == "nt":
        (M, K), N = a2, b2[0]
        assert b2[1] == K
    else:
        (K, M), N = a2, b2[1]
        assert b2[0] == K
    if tk is None:
        tk = MM_TK_TOKENS if mode == "tn" else MM_TK
    if tn is None:
        tn = MM_TN_WIDE if (mode != "tn" and K <= MM_TN_WIDE_MAX_K) else MM_TN
    tm, tn, tk = _tile(M, tm), _tile(N, tn), _tile(K, tk)
    nk = K // tk
    n_extra = len(extras)
    n_out = len(out_dtypes)
    assert n_sums == 0 or tn == N
    resident = {"pipeline_mode": pl.Buffered(1)} if (tn == N and nk == 1) else {}

    def lead(shape, idx, which, **kw):
        if which is None:
            return pl.BlockSpec(shape, idx, **kw)
        return pl.BlockSpec((None,) + shape, lambda i, j, k: (which,) + idx(i, j, k), **kw)

    if mode == "nn":
        a_spec = lead((tm, tk), lambda i, j, k: (i, k), a_lead)
        b_spec = lead((tk, tn), lambda i, j, k: (k, j), b_lead, **resident)
        dims = (((1,), (0,)), ((), ()))
    elif mode == "nt":
        a_spec = lead((tm, tk), lambda i, j, k: (i, k), a_lead)
        b_spec = lead((tn, tk), lambda i, j, k: (j, k), b_lead, **resident)
        dims = (((1,), (1,)), ((), ()))
    else:
        a_spec = lead((tk, tm), lambda i, j, k: (k, i), a_lead)
        b_spec = lead((tk, tn), lambda i, j, k: (k, j), b_lead, **resident)
        dims = (((0,), (0,)), ((), ()))
    e_specs = []
    for e in extras:
        if e.shape[0] == 1:
            e_specs.append(pl.BlockSpec((1, tn), lambda i, j, k: (0, j)))
        else:
            e_specs.append(pl.BlockSpec((tm, tn), lambda i, j, k: (i, j)))

    n_after = len(after)

    def body(a_ref, b_ref, *rest):
        e_refs = rest[:n_extra]
        rest = rest[:n_extra] + rest[n_extra + n_after:]
        o_refs = rest[n_extra:n_extra + n_out]
        s_refs = rest[n_extra + n_out:n_extra + n_out + n_sums]
        part = lax.dot_general(a_ref[...].astype(BF16), b_ref[...].astype(BF16), dims, preferred_element_type=F32)

        def finish(r):
            res = (r,) if epi is None else epi(r, *[e[...] for e in e_refs])
            for o, v in zip(o_refs, res[:n_out]):
                o[...] = v.astype(o.dtype)
            first = pl.program_id(0) == 0
            for sr, v in zip(s_refs, res[n_out:]):
                @pl.when(first)
                def _(sr=sr, v=v):
                    sr[...] = v

                @pl.when(jnp.logical_not(first))
                def _(sr=sr, v=v):
                    sr[...] += v

        if nk == 1:
            finish(part)
            return
        acc = rest[n_extra + n_out + n_sums]
        k = pl.program_id(2)

        @pl.when(k == 0)
        def _():
            acc[...] = part

        @pl.when(jnp.logical_and(k > 0, k < nk - 1))
        def _():
            acc[...] += part

        @pl.when(k == nk - 1)
        def _():
            finish(acc[...] + part)

    outs = pl.pallas_call(
        body,
        name=name,
        grid=(M // tm, N // tn, nk),
        in_specs=[a_spec, b_spec] + e_specs + [pl.BlockSpec(memory_space=pl.ANY)] * n_after,
        out_specs=[pl.BlockSpec((tm, tn), lambda i, j, k: (i, j)) for _ in out_dtypes]
        + [pl.BlockSpec((1, tn), lambda i, j, k: (0, 0))] * n_sums,
        out_shape=[jax.ShapeDtypeStruct((M, N), d) for d in out_dtypes] + [jax.ShapeDtypeStruct((1, N), F32)] * n_sums,
        scratch_shapes=[pltpu.VMEM((tm, tn), F32)] if nk > 1 else [],
        compiler_params=_cparams(*(("arbitrary",) * 3 if n_sums else ("parallel", "parallel", "arbitrary"))),
    )(a, b, *extras, *after)
    return outs[0] if n_out + n_sums == 1 else tuple(outs)


def _rowwise(name, fn, tiled, params, outs, accs=(), tm=256, after=()):
    S = tiled[0].shape[0]
    tm = _tile(S, tm)
    nt, npar, no = len(tiled), len(params), len(outs)
    n_after = len(after)

    def body(*refs):
        t_refs = refs[:nt]
        p_refs = refs[nt:nt + npar]
        refs = refs[nt + npar + n_after:]
        o_refs = refs[:no]
        a_refs = refs[no:]
        res = fn(*[r[...] for r in t_refs], *[r[...] for r in p_refs])
        for o, v in zip(o_refs, res[:no]):
            o[...] = v.astype(o.dtype)
        first = pl.program_id(0) == 0
        for ar, v in zip(a_refs, res[no:]):
            @pl.when(first)
            def _(ar=ar, v=v):
                ar[...] = v

            @pl.when(jnp.logical_not(first))
            def _(ar=ar, v=v):
                ar[...] += v

    full = lambda p: pl.BlockSpec(p.shape, lambda i, nd=p.ndim: (0,) * nd)
    res = pl.pallas_call(
        body,
        name=name,
        grid=(S // tm,),
        in_specs=[pl.BlockSpec((tm, t.shape[1]), lambda i: (i, 0)) for t in tiled] + [full(p) for p in params]
        + [pl.BlockSpec(memory_space=pl.ANY)] * n_after,
        out_specs=[pl.BlockSpec((tm, c), lambda i: (i, 0)) for c, _ in outs]
        + [pl.BlockSpec(s, lambda i, nd=len(s): (0,) * nd) for s in accs],
        out_shape=[jax.ShapeDtypeStruct((S, c), d) for c, d in outs] + [jax.ShapeDtypeStruct(s, F32) for s in accs],
        compiler_params=_cparams("arbitrary"),
    )(*tiled, *params, *after)
    return res


def _ln_stats(z):
    mu = jnp.mean(z, axis=-1, keepdims=True)
    zc = z - mu
    var = jnp.mean(zc * zc, axis=-1, keepdims=True)
    return zc, lax.rsqrt(var + LN_EPS)


def _ln_apply(z, g, b):
    zc, rstd = _ln_stats(z)
    y = zc * rstd * g + b
    return z, y, y


def _ln_grad(dy, z, g):
    zc, rstd = _ln_stats(z)
    xhat = zc * rstd
    dxh = dy * g
    m1 = jnp.mean(dxh, axis=-1, keepdims=True)
    m2 = jnp.mean(dxh * xhat, axis=-1, keepdims=True)
    dz = rstd * (dxh - m1 - xhat * m2)
    return dz, dz, jnp.sum(dy * xhat, axis=0, keepdims=True), jnp.sum(dy, axis=0, keepdims=True)


def _ple_grad(dx3, gpre, pp):
    gate = jax.nn.sigmoid(gpre)
    dgpre = dx3 * pp * gate * (1.0 - gate)
    return dx3 * gate, dgpre, jnp.sum(dgpre, axis=0, keepdims=True)


def _loss_and_grad(name, y, target, gpre, pp):
    d = y.shape[1]

    def fn(y, t, gpre, pp):
        err = y - t
        sq = jnp.sum(jnp.sum(err * err, axis=0, keepdims=True), axis=1, keepdims=True)
        dy = err * (1.0 / d)
        dpp, dgpre, dbg = _ple_grad(dy, gpre, pp)
        return dy, dpp, dgpre, jnp.broadcast_to(sq, (1, LANES)), dbg

    return _rowwise(name, fn, [y, target, gpre, pp], [], [(d, F32), (d, BF16), (d, BF16)], accs=[(1, LANES), (1, d)])


def _rows(shape):
    return lax.broadcasted_iota(jnp.int32, shape, 0)


def _gelu(y):
    t = jnp.tanh(GELU_C0 * (y + GELU_C1 * y * y * y))
    return 0.5 * y * (1.0 + t), t


def _gelu_grad(y, t):
    return 0.5 * (1.0 + t) + 0.5 * y * (1.0 - t * t) * GELU_C0 * (1.0 + 3.0 * GELU_C1 * y * y)


def _neg_expm1(x):
    series = -x * (1.0 + x * (0.5 + x * (1.0 / 6.0 + x * (1.0 / 24.0))))
    return jnp.where(x > -0.02, series, 1.0 - jnp.exp(x))


def _softplus(x):
    return jnp.maximum(x, 0.0) + jnp.log(1.0 + jnp.exp(-jnp.abs(x)))


def _conv_fwd(xs, cw, cb):
    n = xs.shape[0]
    u = cw[3:4] * xs
    for k in (1, 2, 3):
        u = u + cw[3 - k:4 - k] * pltpu.roll(xs, k, 0)
    del n
    return u[SUBLANES:] + cb


def _lru_gates(u, wa, wx, ba, bx, sp, grow):
    ub = u.astype(BF16)
    r = jax.nn.sigmoid(jnp.dot(ub, wa, preferred_element_type=F32) + ba)
    ig = jax.nn.sigmoid(jnp.dot(ub, wx, preferred_element_type=F32) + bx)
    log_a = (-LRU_C) * r * sp
    a = jnp.exp(log_a)
    mult = jnp.sqrt(_neg_expm1(2.0 * log_a))
    mult = jnp.where(grow == 0, 1.0, mult)
    return ub, r, ig, a, mult


def _scan8_fwd(a, b):
    row = _rows(a.shape)
    for k in (1, 2, 4):
        m = row >= k
        b = jnp.where(m, a * pltpu.roll(b, k, 0) + b, b)
        a = jnp.where(m, a * pltpu.roll(a, k, 0), a)
    return a, b


def _scan8_bwd(c, d):
    row = _rows(c.shape)
    for k in (1, 2, 4):
        m = row < SUBLANES - k
        d = jnp.where(m, c * pltpu.roll(d, SUBLANES - k, 0) + d, d)
        c = jnp.where(m, c * pltpu.roll(c, SUBLANES - k, 0), c)
    return c, d


def _pad_copy(dst, src, front, back):
    s, c = src.shape
    if front:
        dst[pl.ds(0, front), :] = jnp.zeros((front, c), dst.dtype)
    if back:
        dst[pl.ds(front + s, back), :] = jnp.zeros((back, c), dst.dtype)
    dst[pl.ds(front, s), :] = src[...].astype(dst.dtype)


def _lru_fwd(name, proj, par, wa, wx):
    S = proj.shape[0]
    R = proj.shape[1] // 2
    H = R // LANES
    ch = _tile(S, SEQ_CHUNK)
    nch = S // ch
    H8 = SUBLANES

    def body(up_ref, y_ref, par_ref, wa_ref, wx_ref, gh_ref, h_ref, up_pad):
        _pad_copy(up_pad, up_ref, H8, 0)
        par = par_ref[...]
        cw, cb, ba, bx = par[0:4], par[4:5], par[5:6], par[6:7]
        sp = _softplus(-par[7:8])
        wa_m, wx_m = wa_ref[...], wx_ref[...]

        def chunk(ci, carry):
            r0 = pl.multiple_of(ci * ch, ch)
            xs = up_pad[pl.ds(r0, ch + H8), :]
            u = _conv_fwd(xs, cw, cb)
            grow = _rows(u.shape) + r0
            _, _, ig, a, mult = _lru_gates(u, wa_m, wx_m, ba, bx, sp, grow)
            bt = mult * (ig * u)
            hs = []
            for j in range(ch // H8):
                aa, bb = _scan8_fwd(a[j * H8:(j + 1) * H8], bt[j * H8:(j + 1) * H8])
                hj = bb + aa * carry
                carry = hj[H8 - 1:H8]
                hs.append(hj)
            h = jnp.concatenate(hs, axis=0)
            h_ref[pl.ds(r0, ch), :] = h
            gy, _ = _gelu(y_ref[pl.ds(r0, ch), :])
            gh_ref[pl.ds(r0, ch), :] = (h * gy).astype(gh_ref.dtype)
            return carry

        lax.fori_loop(0, nch, chunk, jnp.zeros((1, LANES), F32))

    col = lambda off: pl.BlockSpec((S, LANES), lambda h: (0, h + off))
    return pl.pallas_call(
        body,
        name=name,
        grid=(H,),
        in_specs=[col(0), col(H), pl.BlockSpec((8, LANES), lambda h: (0, h)),
                  pl.BlockSpec((None, LANES, LANES), lambda h: (h, 0, 0)),
                  pl.BlockSpec((None, LANES, LANES), lambda h: (h, 0, 0))],
        out_specs=[col(0), col(0)],
        out_shape=[jax.ShapeDtypeStruct((S, R), BF16), jax.ShapeDtypeStruct((S, R), F32)],
        scratch_shapes=[pltpu.VMEM((S + H8, LANES), F32)],
        compiler_params=_cparams("parallel"),
    )(proj, proj, par, wa, wx)


def _lru_bwd(name, proj, h, dgh, par, wa, wx):
    S = proj.shape[0]
    R = proj.shape[1] // 2
    H = R // LANES
    ch = _tile(S, SEQ_CHUNK)
    nch = S // ch
    H8 = SUBLANES
    nb = ch // H8

    def body(up_ref, y_ref, h_ref, dgh_ref, par_ref, wa_ref, wx_ref,
             dup_ref, dy_ref, dwa_ref, dwx_ref, dpar_ref, up_pad, h_pad, du_pad, vec_acc):
        _pad_copy(up_pad, up_ref, H8, 0)
        _pad_copy(h_pad, h_ref, H8, 0)
        du_pad[pl.ds(S, H8), :] = jnp.zeros((H8, LANES), F32)
        par = par_ref[...]
        cw, cb, ba, bx, lam = par[0:4], par[4:5], par[5:6], par[6:7], par[7:8]
        sp = _softplus(-lam)
        wa_m, wx_m = wa_ref[...], wx_ref[...]
        dwa_ref[...] = jnp.zeros_like(dwa_ref)
        dwx_ref[...] = jnp.zeros_like(dwx_ref)
        vec_acc[...] = jnp.zeros_like(vec_acc)
        nt_dims = (((1,), (1,)), ((), ()))
        tn_dims = (((0,), (0,)), ((), ()))

        def chunk(it, carry):
            lam_next, a_next = carry
            ci = nch - 1 - it
            r0 = pl.multiple_of(ci * ch, ch)
            xs = up_pad[pl.ds(r0, ch + H8), :]
            u = _conv_fwd(xs, cw, cb)
            row = _rows(u.shape)
            grow = row + r0
            ub, r, ig, a, mult = _lru_gates(u, wa_m, wx_m, ba, bx, sp, grow)
            hs = h_pad[pl.ds(r0, ch + H8), :]
            hcur = hs[H8:]
            hprev = pltpu.roll(hs, 1, 0)[H8:]
            y = y_ref[pl.ds(r0, ch), :]
            dgh = dgh_ref[pl.ds(r0, ch), :]
            gy, t = _gelu(y)
            dy_ref[pl.ds(r0, ch), :] = (dgh * hcur * _gelu_grad(y, t)).astype(dy_ref.dtype)
            dh = dgh * gy
            c = jnp.where(row == ch - 1, a_next, pltpu.roll(a, ch - 1, 0))
            ls = [None] * nb
            for j in range(nb - 1, -1, -1):
                cc, dd = _scan8_bwd(c[j * H8:(j + 1) * H8], dh[j * H8:(j + 1) * H8])
                lj = dd + cc * lam_next
                lam_next = lj[0:1]
                ls[j] = lj
            lmb = jnp.concatenate(ls, axis=0)
            da = lmb * hprev
            gu = ig * u
            dmult = lmb * gu
            dlog_a = da * a + jnp.where(grow == 0, 0.0, dmult * (-(a * a) / mult))
            dr = dlog_a * ((-LRU_C) * sp)
            drp = dr * r * (1.0 - r)
            dip = (lmb * mult * u) * ig * (1.0 - ig)
            drb, dib = drp.astype(BF16), dip.astype(BF16)
            du = (lmb * mult * ig
                  + lax.dot_general(drb, wa_m, nt_dims, preferred_element_type=F32)
                  + lax.dot_general(dib, wx_m, nt_dims, preferred_element_type=F32))
            du_pad[pl.ds(r0, ch), :] = du
            dwa_ref[...] += lax.dot_general(ub, drb, tn_dims, preferred_element_type=F32)
            dwx_ref[...] += lax.dot_general(ub, dib, tn_dims, preferred_element_type=F32)
            ssum = lambda v: jnp.sum(v, axis=0, keepdims=True)
            vec_acc[0:1, :] += ssum(drp)
            vec_acc[1:2, :] += ssum(dip)
            vec_acc[2:3, :] += ssum(dlog_a * ((-LRU_C) * r))
            return lam_next, a[0:1]

        zero = jnp.zeros((1, LANES), F32)
        lax.fori_loop(0, nch, chunk, (zero, zero))

        def conv_chunk(ci, acc):
            r0 = pl.multiple_of(ci * ch, ch)
            ds = du_pad[pl.ds(r0, ch + H8), :]
            xs = up_pad[pl.ds(r0, ch + H8), :]
            n = ch + H8
            du = ds[:ch]
            dup = cw[3:4] * du
            new = [acc[3] + jnp.sum(du * xs[H8:], axis=0, keepdims=True)]
            for k in (1, 2, 3):
                dup = dup + cw[3 - k:4 - k] * pltpu.roll(ds, n - k, 0)[:ch]
                new.append(acc[3 - k] + jnp.sum(du * pltpu.roll(xs, k, 0)[H8:], axis=0, keepdims=True))
            dup_ref[pl.ds(r0, ch), :] = dup.astype(dup_ref.dtype)
            return (new[3], new[2], new[1], new[0], acc[4] + jnp.sum(du, axis=0, keepdims=True))

        acc = lax.fori_loop(0, nch, conv_chunk, (zero,) * 5)
        dlam = vec_acc[2:3, :] * (-jax.nn.sigmoid(-lam))
        dpar_ref[...] = jnp.concatenate(list(acc) + [vec_acc[0:1, :], vec_acc[1:2, :], dlam], axis=0)

    col = lambda off: pl.BlockSpec((S, LANES), lambda h: (0, h + off))
    head = pl.BlockSpec((None, LANES, LANES), lambda h: (h, 0, 0))
    return pl.pallas_call(
        body,
        name=name,
        grid=(H,),
        in_specs=[col(0), col(H), col(0), col(0), pl.BlockSpec((8, LANES), lambda h: (0, h)), head, head],
        out_specs=[col(0), col(0), head, head, pl.BlockSpec((8, LANES), lambda h: (0, h))],
        out_shape=[jax.ShapeDtypeStruct((S, R), BF16), jax.ShapeDtypeStruct((S, R), BF16),
                   jax.ShapeDtypeStruct((H, LANES, LANES), F32), jax.ShapeDtypeStruct((H, LANES, LANES), F32),
                   jax.ShapeDtypeStruct((8, R), F32)],
        scratch_shapes=[pltpu.VMEM((S + H8, LANES), F32), pltpu.VMEM((S + H8, LANES), F32),
                        pltpu.VMEM((S + H8, LANES), F32), pltpu.VMEM((8, LANES), F32)],
        compiler_params=_cparams("parallel"),
    )(proj, proj, h, dgh, par, wa, wx)


def _window_sum(xs, g, up):
    n = xs.shape[0]
    s = xs
    for lvl, k in enumerate((1, 2, 4, 8)):
        sh = pltpu.roll(s, (n - k) if up else k, 0)
        s = s + jnp.where(g >= lvl, sh, 0.0)
    return s


def _pool_count(grow, g):
    return jnp.minimum(grow + 1, lax.shift_left(jnp.int32(2), g)).astype(F32)


def _pool_fwd(name, u, wgrp, par):
    S, D = u.shape
    G, W = wgrp.shape[0], wgrp.shape[1]
    ch = _tile(S, SEQ_CHUNK)
    nch = S // ch
    PH = POOL_HALO

    def body(u_ref, w_ref, par_ref, zs_ref, u_pad):
        g = pl.program_id(0)
        _pad_copy(u_pad, u_ref, PH, 0)
        par = par_ref[...]
        w = w_ref[...]

        def chunk(ci, _):
            r0 = pl.multiple_of(ci * ch, ch)
            xs = u_pad[pl.ds(r0, ch + PH), :]
            ws = _window_sum(xs, g, False)[PH:]
            uc = xs[PH:]
            cnt = _pool_count(_rows(uc.shape) + r0, g)
            pooled = ws / cnt - uc
            z = jnp.dot(pooled.astype(BF16), w, preferred_element_type=F32) + par[0:1]
            zs_ref[pl.ds(r0, ch), :] = (z * par[1:2]).astype(zs_ref.dtype)
            return 0

        lax.fori_loop(0, nch, chunk, 0)

    return pl.pallas_call(
        body,
        name=name,
        grid=(G,),
        in_specs=[pl.BlockSpec((S, W), lambda g: (0, g)), pl.BlockSpec((None, W, W), lambda g: (g, 0, 0)),
                  pl.BlockSpec((2, W), lambda g: (0, g))],
        out_specs=pl.BlockSpec((S, W), lambda g: (0, g)),
        out_shape=jax.ShapeDtypeStruct((S, D), BF16),
        scratch_shapes=[pltpu.VMEM((S + PH, W), F32)],
        compiler_params=_cparams("parallel"),
    )(u, wgrp, par)


def _pool_bwd(name, u, dzs, wgrp, par):
    S, D = u.shape
    G, W = wgrp.shape[0], wgrp.shape[1]
    ch = _tile(S, SEQ_CHUNK)
    nch = S // ch
    PH = POOL_HALO

    def body(u_ref, dzs_ref, w_ref, par_ref, du_ref, dw_ref, dpar_ref, u_pad, q_pad, dw_acc):
        g = pl.program_id(0)
        _pad_copy(u_pad, u_ref, PH, 0)
        q_pad[pl.ds(S, PH), :] = jnp.zeros((PH, W), F32)
        par = par_ref[...]
        w = w_ref[...]
        dw_acc[...] = jnp.zeros_like(dw_acc)

        def chunk(ci, acc):
            db, dsc = acc
            r0 = pl.multiple_of(ci * ch, ch)
            xs = u_pad[pl.ds(r0, ch + PH), :]
            ws = _window_sum(xs, g, False)[PH:]
            uc = xs[PH:]
            cnt = _pool_count(_rows(uc.shape) + r0, g)
            pooled = (ws / cnt - uc).astype(BF16)
            z = jnp.dot(pooled, w, preferred_element_type=F32) + par[0:1]
            dzs = dzs_ref[pl.ds(r0, ch), :]
            dz = dzs * par[1:2]
            dzb = dz.astype(BF16)
            dw_acc[...] += lax.dot_general(pooled, dzb, (((0,), (0,)), ((), ())), preferred_element_type=F32)
            dpooled = lax.dot_general(dzb, w, (((1,), (1,)), ((), ())), preferred_element_type=F32)
            q_pad[pl.ds(r0, ch), :] = dpooled / cnt
            return (db + jnp.sum(dz, axis=0, keepdims=True), dsc + jnp.sum(dzs * z, axis=0, keepdims=True))

        zero = jnp.zeros((1, W), F32)
        db, dsc = lax.fori_loop(0, nch, chunk, (zero, zero))
        dpar_ref[...] = jnp.concatenate([db, dsc], axis=0)
        dw_ref[...] = dw_acc[...].astype(dw_ref.dtype)

        def back(ci, _):
            r0 = pl.multiple_of(ci * ch, ch)
            qs = q_pad[pl.ds(r0, ch + PH), :]
            ws = _window_sum(qs, g, True)[:ch]
            qc = qs[:ch]
            cnt = _pool_count(_rows(qc.shape) + r0, g)
            du_ref[pl.ds(r0, ch), :] = (ws - qc * cnt).astype(du_ref.dtype)
            return 0

        lax.fori_loop(0, nch, back, 0)

    blk = pl.BlockSpec((S, W), lambda g: (0, g))
    wspec = pl.BlockSpec((None, W, W), lambda g: (g, 0, 0))
    pspec = pl.BlockSpec((2, W), lambda g: (0, g))
    return pl.pallas_call(
        body,
        name=name,
        grid=(G,),
        in_specs=[blk, blk, wspec, pspec],
        out_specs=[blk, wspec, pspec],
        out_shape=[jax.ShapeDtypeStruct((S, D), BF16), jax.ShapeDtypeStruct((G, W, W), BF16),
                   jax.ShapeDtypeStruct((2, D), F32)],
        scratch_shapes=[pltpu.VMEM((S + PH, W), F32), pltpu.VMEM((S + PH, W), F32), pltpu.VMEM((W, W), F32)],
        compiler_params=_cparams("parallel"),
    )(u, dzs, wgrp, par)


def _my_place():
    x, y, c = lax.axis_index("x"), lax.axis_index("y"), lax.axis_index("c")
    return x, y, c, 4 * x + 2 * y + c


def _peers(x, y, c):
    out = []
    for d in range(1, N_DEV):
        px = 1 - x if d & 4 else x
        py = 1 - y if d & 2 else y
        pc = 1 - c if d & 1 else c
        out.append(((px, py, pc), 4 * px + 2 * py + pc))
    return out


def _window(ref, axis, start, size):
    idx = [slice(None)] * len(ref.shape)
    idx[axis] = pl.ds(start, size)
    return ref.at[tuple(idx)]


def _to_bf16(name, arrs):
    outs = []
    for i, a in enumerate(arrs):
        a2 = a.reshape(-1, a.shape[-1])
        tr = _tile(a2.shape[0], 512)
        o = pl.pallas_call(
            lambda a_ref, o_ref: o_ref.__setitem__(Ellipsis, a_ref[...].astype(BF16)),
            name=f"{name}_{i}",
            grid=(a2.shape[0] // tr,),
            in_specs=[pl.BlockSpec((tr, a2.shape[1]), lambda r: (r, 0))],
            out_specs=pl.BlockSpec((tr, a2.shape[1]), lambda r: (r, 0)),
            out_shape=jax.ShapeDtypeStruct(a2.shape, BF16),
            compiler_params=_cparams("parallel"),
        )(a2)
        outs.append(o.reshape(a.shape))
    return outs


def _all_gather(name, shards, axes):
    n = len(shards)
    sizes = [s.shape[ax] for s, ax in zip(shards, axes)]

    def body(*refs):
        ins, outs = refs[:n], refs[n:2 * n]
        send, recv, loc = refs[2 * n:]
        x, y, c, me = _my_place()
        peers = _peers(x, y, c)
        local = []
        for i in range(n):
            dst = _window(outs[i], axes[i], me * sizes[i], sizes[i])
            cp = pltpu.make_async_copy(ins[i], dst, loc.at[i])
            cp.start()
            local.append(cp)
            for peer, _ in peers:
                pltpu.make_async_remote_copy(src_ref=ins[i], dst_ref=dst, send_sem=send.at[i], recv_sem=recv.at[i],
                                             device_id=peer, device_id_type=MESH).start()
        for i in range(n):
            local[i].wait()
            seven = _window(outs[i], axes[i], 0, (N_DEV - 1) * sizes[i])
            pltpu.make_async_remote_copy(src_ref=seven, dst_ref=seven, send_sem=send.at[i], recv_sem=recv.at[i],
                                         device_id=(x, y, c), device_id_type=MESH).wait()

    def full_shape(s, ax):
        shp = list(s.shape)
        shp[ax] *= N_DEV
        return jax.ShapeDtypeStruct(tuple(shp), s.dtype)

    any_spec = pl.BlockSpec(memory_space=pl.ANY)
    return pl.pallas_call(
        body,
        name=name,
        in_specs=[any_spec] * n,
        out_specs=[any_spec] * n,
        out_shape=[full_shape(s, ax) for s, ax in zip(shards, axes)],
        scratch_shapes=[pltpu.SemaphoreType.DMA((n,)), pltpu.SemaphoreType.DMA((n,)), pltpu.SemaphoreType.DMA((n,))],
        compiler_params=pltpu.CompilerParams(has_side_effects=True),
    )(*shards)


HBM_SPEC = pl.BlockSpec(memory_space=pltpu.HBM)
SEM_SPEC = pl.BlockSpec(memory_space=pltpu.SEMAPHORE)
SPLIT_EFFECT = pltpu.SideEffectType.DATAFLOW_SIDE_EFFECTING


def _push_all(kind, src, dst, axis, size, send_sem, recv_sem, place):
    x, y, c, me = place
    for peer, pidx in _peers(x, y, c):
        if kind == "gather":
            s = d = _window(dst, axis, me * size, size)
        else:
            s, d = _window(src, axis, pidx * size, size), dst.at[me]
        pltpu.make_async_remote_copy(src_ref=s, dst_ref=d, send_sem=send_sem, recv_sem=recv_sem, device_id=peer,
                                     device_id_type=MESH).start()


def _drain_all(kind, dst, axis, size, send_sem, recv_sem, place):
    x, y, c, _ = place
    seven = _window(dst, axis, 0, (N_DEV - 1) * size) if kind == "gather" else dst.at[pl.ds(0, N_DEV - 1)]
    pltpu.make_async_remote_copy(src_ref=seven, dst_ref=seven, send_sem=send_sem, recv_sem=recv_sem,
                                 device_id=(x, y, c), device_id_type=MESH).wait()


def _own_block_placed(src, axis, size, me):
    own = lax.dynamic_slice_in_dim(src, me * size, size, axis)
    return lax.dynamic_update_slice_in_dim(lax.empty((N_DEV,) + own.shape, src.dtype), own[None], me, 0)


def _split_start(name, kind, srcs, lands, axes, sizes, after=()):
    n, ns, na = len(lands), len(srcs), len(after)

    def body(*refs):
        src_refs, land_refs = refs[:ns], refs[ns:ns + n]
        send, recv = refs[ns + n + na], refs[ns + n + na + 1]
        token = refs[-1]
        place = _my_place()
        for k in range(n):
            _push_all(kind, src_refs[k] if ns else None, land_refs[k], axes[k], sizes[k], send.at[k], recv.at[k], place)
        token[...] = jnp.zeros_like(token)

    hbm = lambda a: pltpu.HBM(a.shape, a.dtype)
    res = pl.pallas_call(
        body,
        name=name,
        out_shape=(pltpu.SemaphoreType.DMA((n,)), pltpu.SemaphoreType.DMA((n,)), *[hbm(a) for a in srcs],
                   *[hbm(a) for a in lands], jax.ShapeDtypeStruct((SUBLANES, LANES), F32)),
        in_specs=[HBM_SPEC] * (ns + n) + [pl.BlockSpec(memory_space=pl.ANY)] * na,
        out_specs=(SEM_SPEC, SEM_SPEC, *[HBM_SPEC] * (ns + n), pl.BlockSpec(memory_space=pltpu.VMEM)),
        input_output_aliases={k: 2 + k for k in range(ns + n)},
        compiler_params=pltpu.CompilerParams(has_side_effects=SPLIT_EFFECT),
    )(*[pltpu.with_memory_space_constraint(a, pltpu.HBM) for a in (*srcs, *lands)], *after)
    return res[0], res[1], list(res[2:2 + ns]), list(res[2 + ns:2 + ns + n]), res[-1]


def _split_wait(name, kind, handle, axes, sizes, after):
    send, recv, srcs, lands, _ = handle
    n, ns = len(lands), len(srcs)
    after = list(after) if isinstance(after, (list, tuple)) else [after]

    def body(*refs):
        land_refs = refs[ns:ns + n]
        send_ref, recv_ref = refs[ns + n], refs[ns + n + 1]
        place = _my_place()
        for k in range(n):
            _drain_all(kind, land_refs[k], axes[k], sizes[k], send_ref.at[k], recv_ref.at[k], place)

    hbm = lambda a: pltpu.HBM(a.shape, a.dtype)
    res = pl.pallas_call(
        body,
        name=name,
        out_shape=tuple(hbm(a) for a in (*srcs, *lands)),
        in_specs=[HBM_SPEC] * (ns + n) + [SEM_SPEC, SEM_SPEC] + [pl.BlockSpec(memory_space=pl.ANY)] * len(after),
        out_specs=tuple([HBM_SPEC] * (ns + n)),
        input_output_aliases={k: k for k in range(ns + n)},
        compiler_params=pltpu.CompilerParams(has_side_effects=SPLIT_EFFECT),
    )(*srcs, *lands, send, recv, *after)
    return list(res[ns:])


def _cast_into_window(name, a, l, axis, me1):
    shp = a.shape[1:]
    cast = lambda me_ref, a_ref, o_ref: o_ref.__setitem__(Ellipsis, a_ref[...].astype(BF16))
    if len(shp) == 3:
        assert axis == 1
        G, r, c = shp
        full = (G, r * N_DEV, c)
        grid = (G,)
        in_spec = pl.BlockSpec((None, None, r, c), lambda g, me: (l, g, 0, 0))
        out_spec = pl.BlockSpec((None, r, c), lambda g, me: (g, me[0], 0))
    else:
        r, c = shp
        tr = _tile(r, 512)
        nb = r // tr
        grid = (nb,)
        in_spec = pl.BlockSpec((None, tr, c), lambda i, me: (l, i, 0))
        if axis == 0:
            full = (r * N_DEV, c)
            out_spec = pl.BlockSpec((tr, c), lambda i, me: (me[0] * nb + i, 0))
        else:
            full = (r, c * N_DEV)
            out_spec = pl.BlockSpec((tr, c), lambda i, me: (i, me[0]))
    return pl.pallas_call(
        cast,
        name=name,
        grid_spec=pltpu.PrefetchScalarGridSpec(num_scalar_prefetch=1, grid=grid, in_specs=[in_spec], out_specs=out_spec),
        out_shape=jax.ShapeDtypeStruct(full, BF16),
        compiler_params=_cparams("arbitrary"),
    )(me1, a)


def _adamw_math(w, g, m, v):
    m = ADAM_B1 * m + (1.0 - ADAM_B1) * g
    v = ADAM_B2 * v + (1.0 - ADAM_B2) * jnp.square(g)
    m_hat = m / (1.0 - ADAM_B1 ** ADAM_STEP)
    v_hat = v / (1.0 - ADAM_B2 ** ADAM_STEP)
    delta = -ADAM_LR * (m_hat / (jnp.sqrt(v_hat) + ADAM_EPS) + ADAM_WD * w)
    return delta, m, v


def _sum_slots(buf_ref):
    g = buf_ref[0].astype(F32)
    for s in range(1, N_DEV):
        g = g + buf_ref[s].astype(F32)
    return g


def _adamw_layer(name, buf, w, m, v, l, prev):
    shape = w.shape
    L, C = shape[0], shape[-1]
    Rr = math.prod(shape[1:-1])
    buf3 = buf.reshape(N_DEV, Rr, C)
    w3, m3, v3 = (t.reshape(L, Rr, C) for t in (w, m, v))
    tr = _tile(Rr, 2 * LANES) if Rr % LANES == 0 else Rr
    n_prev = 0 if prev is None else 4

    def body(buf_ref, w_ref, m_ref, v_ref, *rest):
        g_out, d_out, m_out, v_out = rest[n_prev:]
        g = _sum_slots(buf_ref)
        d, mm, vv = _adamw_math(w_ref[...], g, m_ref[...], v_ref[...])
        g_out[...] = g
        d_out[...] = d
        m_out[...] = mm
        v_out[...] = vv

    spec = pl.BlockSpec((None, tr, C), lambda r: (l, r, 0))
    outs = pl.pallas_call(
        body,
        name=name,
        grid=(Rr // tr,),
        in_specs=[pl.BlockSpec((N_DEV, tr, C), lambda r: (0, r, 0)), spec, spec, spec]
        + [pl.BlockSpec(memory_space=pl.ANY)] * n_prev,
        out_specs=[spec] * 4,
        out_shape=[jax.ShapeDtypeStruct((L, Rr, C), F32)] * 4,
        input_output_aliases={4 + k: k for k in range(n_prev)},
        compiler_params=_cparams("parallel"),
    )(buf3, w3, m3, v3, *(prev or ()))
    return list(outs)


def _sum8(name, buf):
    R = buf.shape[1]

    def body(buf_ref, o_ref):
        o_ref[...] = _sum_slots(buf_ref)

    return pl.pallas_call(
        body,
        name=name,
        in_specs=[pl.BlockSpec(buf.shape, lambda: (0, 0, 0))],
        out_specs=pl.BlockSpec((R, LANES), lambda: (0, 0)),
        out_shape=jax.ShapeDtypeStruct((R, LANES), F32),
        compiler_params=_cparams(),
    )(buf)


def _adamw_packed(name, w, g, m, v):
    R = w.shape[0]
    tr = _row_tile(R, 512)

    def body(w_ref, g_ref, m_ref, v_ref, d_out, m_out, v_out):
        d, mm, vv = _adamw_math(w_ref[...], g_ref[...], m_ref[...], v_ref[...])
        d_out[...] = d
        m_out[...] = mm
        v_out[...] = vv

    spec = pl.BlockSpec((tr, LANES), lambda r: (r, 0))
    return pl.pallas_call(
        body,
        name=name,
        grid=(R // tr,),
        in_specs=[spec] * 4,
        out_specs=[spec] * 3,
        out_shape=[jax.ShapeDtypeStruct((R, LANES), F32)] * 3,
        compiler_params=_cparams("parallel"),
    )(w, g, m, v)


def _pack(arrs, pad_rows_to=SUBLANES):
    parts = []
    for a in arrs:
        flat = a.reshape(-1)
        per = LANES * pad_rows_to
        padded = -(-flat.shape[0] // per) * per
        if padded != flat.shape[0]:
            flat = jnp.pad(flat, (0, padded - flat.shape[0]))
        parts.append(flat.reshape(-1, LANES))
    return jnp.concatenate(parts, axis=0)


def _unpack(packed, shapes, pad_rows_to=SUBLANES):
    out = []
    r = 0
    for shp in shapes:
        nel = math.prod(shp)
        per = LANES * pad_rows_to
        rows = -(-nel // per) * pad_rows_to
        out.append(packed[r:r + rows].reshape(-1)[:nel].reshape(shp))
        r += rows
    return out


BIG = ("lru_w_in", "lru_w_out", "pool_w_in", "pool_w_grp", "pool_w_out", "mlp_w1", "mlp_w2", "ple_w", "ple_gate_w")
BIG_AXIS = {"lru_w_in": 2, "lru_w_out": 1, "pool_w_in": 1, "pool_w_grp": 2, "pool_w_out": 1, "mlp_w1": 2,
            "mlp_w2": 1, "ple_w": 2, "ple_gate_w": 1}
SMALL_SHARDED = ("lru_conv_w", "pool_b_grp", "pool_scale")
REPLICATED = ("lru_conv_b", "lru_wa", "lru_ba", "lru_wx", "lru_bx", "lru_lambda", "ln_mix_g", "ln_mix_b",
              "ln_mlp_g", "ln_mlp_b", "ple_gate_b")
WEIGHTS = ("lru_w_in", "lru_conv_w", "lru_conv_b", "lru_wa", "lru_ba", "lru_wx", "lru_bx", "lru_lambda", "lru_w_out",
           "pool_w_in", "pool_w_grp", "pool_b_grp", "pool_scale", "pool_w_out", "ln_mix_g", "ln_mix_b", "mlp_w1",
           "mlp_w2", "ln_mlp_g", "ln_mlp_b", "ple_w", "ple_gate_w", "ple_gate_b")
INPUTS = ("x", "p") + WEIGHTS + ("loss_target",) + tuple("m_" + n for n in WEIGHTS) + tuple("v_" + n for n in WEIGHTS)


def _gather_last_axis(packed_full, shard_shape):
    nel = math.prod(shard_shape)
    blocks = packed_full.reshape(N_DEV, -1)[:, :nel].reshape((N_DEV,) + tuple(shard_shape))
    return jnp.concatenate([blocks[d] for d in range(N_DEV)], axis=-1)


def kernel(x, p, lru_w_in, lru_conv_w, lru_conv_b, lru_wa, lru_ba, lru_wx, lru_bx, lru_lambda, lru_w_out, pool_w_in, pool_w_grp, pool_b_grp, pool_scale, pool_w_out, ln_mix_g, ln_mix_b, mlp_w1, mlp_w2, ln_mlp_g, ln_mlp_b, ple_w, ple_gate_w, ple_gate_b, loss_target, m_lru_w_in, m_lru_conv_w, m_lru_conv_b, m_lru_wa, m_lru_ba, m_lru_wx, m_lru_bx, m_lru_lambda, m_lru_w_out, m_pool_w_in, m_pool_w_grp, m_pool_b_grp, m_pool_scale, m_pool_w_out, m_ln_mix_g, m_ln_mix_b, m_mlp_w1, m_mlp_w2, m_ln_mlp_g, m_ln_mlp_b, m_ple_w, m_ple_gate_w, m_ple_gate_b, v_lru_w_in, v_lru_conv_w, v_lru_conv_b, v_lru_wa, v_lru_ba, v_lru_wx, v_lru_bx, v_lru_lambda, v_lru_w_out, v_pool_w_in, v_pool_w_grp, v_pool_b_grp, v_pool_scale, v_pool_w_out, v_ln_mix_g, v_ln_mix_b, v_mlp_w1, v_mlp_w2, v_ln_mlp_g, v_ln_mlp_b, v_ple_w, v_ple_gate_w, v_ple_gate_b):
    A = dict(zip(INPUTS, (x, p, lru_w_in, lru_conv_w, lru_conv_b, lru_wa, lru_ba, lru_wx, lru_bx, lru_lambda, lru_w_out, pool_w_in, pool_w_grp, pool_b_grp, pool_scale, pool_w_out, ln_mix_g, ln_mix_b, mlp_w1, mlp_w2, ln_mlp_g, ln_mlp_b, ple_w, ple_gate_w, ple_gate_b, loss_target, m_lru_w_in, m_lru_conv_w, m_lru_conv_b, m_lru_wa, m_lru_ba, m_lru_wx, m_lru_bx, m_lru_lambda, m_lru_w_out, m_pool_w_in, m_pool_w_grp, m_pool_b_grp, m_pool_scale, m_pool_w_out, m_ln_mix_g, m_ln_mix_b, m_mlp_w1, m_mlp_w2, m_ln_mlp_g, m_ln_mlp_b, m_ple_w, m_ple_gate_w, m_ple_gate_b, v_lru_w_in, v_lru_conv_w, v_lru_conv_b, v_lru_wa, v_lru_ba, v_lru_wx, v_lru_bx, v_lru_lambda, v_lru_w_out, v_pool_w_in, v_pool_w_grp, v_pool_b_grp, v_pool_scale, v_pool_w_out, v_ln_mix_g, v_ln_mix_b, v_mlp_w1, v_mlp_w2, v_ln_mlp_g, v_ln_mlp_b, v_ple_w, v_ple_gate_w, v_ple_gate_b)))
    depth = ln_mix_g.shape[0]
    alpha = (2 * depth) ** 0.25
    S, D = x.shape[1], x.shape[2]
    xs = x.reshape(S, D)
    tgt = loss_target.reshape(S, D)
    p3 = p.reshape(depth, S, p.shape[-1])
    me = 4 * lax.axis_index("x") + 2 * lax.axis_index("y") + lax.axis_index("c")

    def layer_weights(i):
        s = i // 2
        mixer = ("lru_w_in", "lru_w_out") if i % 2 == 0 else ("pool_w_in", "pool_w_grp", "pool_w_out")
        return [(n, s) for n in mixer] + [(n, i) for n in ("mlp_w1", "mlp_w2", "ple_w", "ple_gate_w")]

    def axis_of(key):
        return 0 if key[0] == "small" else BIG_AXIS[key[0]] - 1

    def start_gather(tag, keys, after):
        axes = [axis_of(k) for k in keys]
        lands = [land[k] for k in keys]
        sizes = [a.shape[ax] // N_DEV for a, ax in zip(lands, axes)]
        return keys, _split_start(f"gather_{tag}_start", "gather", [], lands, axes, sizes, after=after), axes, sizes

    def finish_gather(tag, pending, after):
        keys, handle, axes, sizes = pending
        for (n, l), full in zip(keys, _split_wait(f"gather_{tag}_wait", "gather", handle, axes, sizes, after)):
            W[n][l] = full

    def start_exchange(tag, keys, arrs, after):
        axes = [axis_of(k) for k in keys]
        sizes = [a.shape[ax] // N_DEV for a, ax in zip(arrs, axes)]
        lands = [_own_block_placed(a, ax, sz, me) for a, ax, sz in zip(arrs, axes, sizes)]
        return keys, _split_start(f"exchange_{tag}_start", "scatter", arrs, lands, axes, sizes, after=after), axes, sizes

    def finish_exchange(tag, pending, after):
        keys, handle, axes, sizes = pending
        partial.update(zip(keys, _split_wait(f"exchange_{tag}_wait", "scatter", handle, axes, sizes, after)))

    me1 = jnp.reshape(me, (1,)).astype(jnp.int32)
    land = {k: _cast_into_window(f"cast_{k[0]}_{k[1]}", A[k[0]], k[1], axis_of(k), me1)
            for i in range(depth) for k in layer_weights(i)}
    W = {n: [None] * A[n].shape[0] for n in BIG}
    small_shard_shapes = [A[n].shape for n in SMALL_SHARDED]
    gathered = _all_gather("gather_small_params", [_pack([A[n] for n in SMALL_SHARDED])], [0])
    def gather_groups(i):
        keys = layer_weights(i)
        mixer, (w1, w2, pw, pg) = keys[:-4], keys[-4:]
        if i == 0:
            return [("in", mixer[:1]), ("out", mixer[1:]), ("up", [w1]), ("rest", [w2, pw, pg])]
        return [("in", mixer + [w1]), ("rest", [w2, pw, pg])]

    gather_pending = {}
    token = gathered[0]
    for i in range(depth):
        for tag, keys in gather_groups(i):
            gather_pending[(i, tag)] = start_gather(f"l{i}_{tag}", keys, (token,))
            token = gather_pending[(i, tag)][1][4]
        if i == 0:
            finish_gather("l0_in", gather_pending[(0, "in")], token)
    all_started = token
    small_full = gathered[0].reshape(N_DEV, -1, LANES)
    r = 0
    for n, shp in zip(SMALL_SHARDED, small_shard_shapes):
        rows = -(-math.prod(shp) // (LANES * SUBLANES)) * SUBLANES
        W[n] = _gather_last_axis(small_full[:, r:r + rows], shp)
        r += rows
    wa_b, wx_b = _to_bf16("cast_gates", [lru_wa, lru_wx])
    n_lru = lru_w_in.shape[0]
    lru_par = [jnp.concatenate([W["lru_conv_w"][s], lru_conv_b[s][None], lru_ba[s][None], lru_bx[s][None],
                                lru_lambda[s][None]], axis=0) for s in range(n_lru)]
    pool_par = [jnp.stack([W["pool_b_grp"][s], W["pool_scale"][s]], axis=0) for s in range(pool_w_in.shape[0])]

    saved = []
    h_in = xs
    h_in_b, p3b = _to_bf16("cast_inputs", [xs, p3])
    for i in range(depth):
        s = i // 2
        sv = {"x0b": h_in_b}
        if i > 0:
            finish_gather(f"l{i}_in", gather_pending[(i, "in")], h_in)
        ln_out = dict(out_dtypes=[F32, F32, BF16], tm=MM_TM_ROWS, tn=D,
                      epi=lambda acc, xp, g, b: _ln_apply(alpha * xp + acc, g, b))
        if i % 2 == 0:
            sv["proj"] = _mm(f"l{i}_lru_in", h_in_b, W["lru_w_in"], "nn", [F32], b_lead=s)
            sv["gh"], sv["h"] = _lru_fwd(f"l{i}_lru_core", sv["proj"], lru_par[s], wa_b[s], wx_b[s])
            if i == 0:
                finish_gather("l0_out", gather_pending[(0, "out")], [sv["gh"], all_started])
            mix_in, mix_w = sv["gh"], W["lru_w_out"]
        else:
            sv["u"] = _mm(f"l{i}_pool_in", h_in_b, W["pool_w_in"], "nn", [F32], b_lead=s)
            sv["zs"] = _pool_fwd(f"l{i}_pool_core", sv["u"], W["pool_w_grp"][s], pool_par[s])
            mix_in, mix_w = sv["zs"], W["pool_w_out"]
        sv["z1"], sv["x1"], sv["x1b"] = _mm(f"l{i}_mix_out_ln", mix_in, mix_w, "nn", b_lead=s,
                                            extras=[h_in, ln_mix_g[i][None], ln_mix_b[i][None]], **ln_out)
        if i == 0:
            finish_gather("l0_up", gather_pending[(0, "up")], sv["x1b"])
        sv["hpre"], sv["hact"] = _mm(f"l{i}_mlp_up", sv["x1b"], W["mlp_w1"], "nn", [BF16, BF16], b_lead=i,
                                     epi=lambda acc: (acc, jnp.square(jnp.maximum(acc, 0.0))))
        finish_gather(f"l{i}_rest", gather_pending[(i, "rest")], sv["hact"])
        sv["z2"], sv["x2"], sv["x2b"] = _mm(f"l{i}_mlp_down_ln", sv["hact"], W["mlp_w2"], "nn", b_lead=i,
                                            extras=[sv["x1"], ln_mlp_g[i][None], ln_mlp_b[i][None]], **ln_out)
        sv["pp"] = _mm(f"l{i}_ple_up", p3b, W["ple_w"], "nn", [F32], a_lead=i, b_lead=i)

        def ple_epi(acc, bg, x2t, ppt):
            gpre = acc + bg
            x3 = x2t + ppt * jax.nn.sigmoid(gpre)
            return x3, x3, gpre

        h_in, h_in_b, sv["gpre"] = _mm(f"l{i}_ple_gate", sv["x2b"], W["ple_gate_w"], "nn", [F32, BF16, F32], b_lead=i,
                                       epi=ple_epi, extras=[ple_gate_b[i][None], sv["x2"], sv["pp"]], tn=MM_TN)
        saved.append(sv)

    dx, dpp, dgpre, sq, dbg = _loss_and_grad("loss", h_in, tgt, saved[-1]["gpre"], saved[-1]["pp"])
    loss = lax.psum(0.5 * sq[0, 0] / D, ("x", "y", "c"))

    dW = {n: [None] * A[n].shape[0] for n in BIG}
    dsmall = {n: [None] * A[n].shape[0] for n in REPLICATED + SMALL_SHARDED}
    small_names = REPLICATED + SMALL_SHARDED
    partial = {}
    exchange_pending = {}
    exchange_token = ()
    for i in reversed(range(depth)):
        s = i // 2
        sv = saved[i]
        dsmall["ple_gate_b"][i] = dbg[0]
        dW["ple_w"][i] = _mm(f"l{i}_d_ple_w", p3b, dpp, "tn", [BF16], a_lead=i, after=exchange_token)
        dW["ple_gate_w"][i] = _mm(f"l{i}_d_ple_gate_w", sv["x2b"], dgpre, "tn", [BF16])
        ln_back = dict(out_dtypes=[F32, BF16], tm=MM_TM_ROWS, tn=D, n_sums=2)
        dz2, dz2b, dg, db = _mm(f"l{i}_d_x2_ln", dgpre, W["ple_gate_w"], "nt", b_lead=i,
                                extras=[dx, sv["z2"], ln_mlp_g[i][None]],
                                epi=lambda acc, d, z, g: _ln_grad(acc + d, z, g), **ln_back)
        dsmall["ln_mlp_g"][i], dsmall["ln_mlp_b"][i] = dg[0], db[0]
        dhpre = _mm(f"l{i}_d_hpre", dz2b, W["mlp_w2"], "nt", [BF16], b_lead=i, extras=[sv["hpre"]],
                    epi=lambda acc, hp: (acc * (2.0 * jnp.maximum(hp.astype(F32), 0.0)),))
        dW["mlp_w2"][i] = _mm(f"l{i}_d_mlp_w2", sv["hact"], dz2b, "tn", [BF16])
        dW["mlp_w1"][i] = _mm(f"l{i}_d_mlp_w1", sv["x1b"], dhpre, "tn", [BF16])
        mlp_after = ()
        if i == 0:
            early = [(n, 0) for n in ("ple_w", "ple_gate_w", "mlp_w2", "mlp_w1")]
            exchange_early0 = start_exchange("early0", early, [dW[n][l] for n, l in early], ())
            mlp_after = (exchange_early0[1][4],)
        dz1, dz1b, dg, db = _mm(f"l{i}_d_x1_ln", dhpre, W["mlp_w1"], "nt", b_lead=i,
                                extras=[dz2, sv["z1"], ln_mix_g[i][None]],
                                epi=lambda acc, d, z, g: _ln_grad(acc + alpha * d, z, g), after=mlp_after, **ln_back)
        dsmall["ln_mix_g"][i], dsmall["ln_mix_b"][i] = dg[0], db[0]
        if i % 2 == 0:
            dW["lru_w_out"][s] = _mm(f"l{i}_d_lru_w_out", sv["gh"], dz1b, "tn", [BF16])
            dgh = _mm(f"l{i}_d_gh", dz1b, W["lru_w_out"], "nt", [F32], b_lead=s)
            dup, dy, dwa, dwx, dpar = _lru_bwd(f"l{i}_lru_core_bwd", sv["proj"], sv["h"], dgh, lru_par[s],
                                               wa_b[s], wx_b[s])
            dsmall["lru_wa"][s], dsmall["lru_wx"][s] = dwa, dwx
            dsmall["lru_conv_w"][s] = dpar[0:4]
            for k, n in enumerate(("lru_conv_b", "lru_ba", "lru_bx", "lru_lambda")):
                dsmall[n][s] = dpar[4 + k]
            dmix_in = jnp.concatenate([dup, dy], axis=1)
            win = "lru_w_in"
        else:
            dW["pool_w_out"][s] = _mm(f"l{i}_d_pool_w_out", sv["zs"], dz1b, "tn", [BF16])
            dzs = _mm(f"l{i}_d_zs", dz1b, W["pool_w_out"], "nt", [F32], b_lead=s)
            dmix_in, dW["pool_w_grp"][s], dpar = _pool_bwd(f"l{i}_pool_core_bwd", sv["u"], dzs, W["pool_w_grp"][s],
                                                          pool_par[s])
            dsmall["pool_b_grp"][s], dsmall["pool_scale"][s] = dpar[0], dpar[1]
            win = "pool_w_in"
        dW[win][s] = _mm(f"l{i}_d_{win}", sv["x0b"], dmix_in, "tn", [BF16])
        x0_after = ()
        if i > 0:
            keys = layer_weights(i)
            exchange_pending[i] = start_exchange(f"l{i}", keys, [dW[n][l] for n, l in keys], ())
            exchange_token = (exchange_pending[i][1][4],)
        else:
            small_grads = [jnp.stack(dsmall[n]) for n in small_names]
            small_shapes = [g.shape for g in small_grads]
            packed_g = _pack(small_grads)
            assert packed_g.shape[0] % (N_DEV * SUBLANES) == 0, packed_g.shape
            late = [("lru_w_out", 0), ("lru_w_in", 0), ("small", 0)]
            exchange_late0 = start_exchange("late0", late, [dW["lru_w_out"][0], dW["lru_w_in"][0], packed_g], ())
            x0_after = (exchange_late0[1][4],)
        if i > 0:
            def x0_epi(acc, d, gpre, pp):
                dxv = acc + alpha * d
                return (dxv,) + _ple_grad(dxv, gpre, pp)

            dx, dpp, dgpre, dbg = _mm(f"l{i}_d_x0_ple", dmix_in, W[win], "nt", [F32, BF16, BF16], b_lead=s,
                                      extras=[dz1, saved[i - 1]["gpre"], saved[i - 1]["pp"]], epi=x0_epi,
                                      tm=MM_TM_ROWS, tn=D, n_sums=1)
        else:
            dx = _mm(f"l{i}_d_x0", dmix_in, W[win], "nt", [F32], b_lead=s, extras=[dz1],
                     epi=lambda acc, d: (acc + alpha * d,), after=x0_after)
    grad_x = dx.reshape(x.shape)

    for i in range(1, depth):
        finish_exchange(f"l{i}", exchange_pending[i], x0_after[0])
    stacked = {n: None for n in BIG}
    layer0 = layer_weights(0)

    def adamw(n, l):
        stacked[n] = _adamw_layer(f"adamw_{n}_{l}", partial[(n, l)], A[n], A["m_" + n], A["v_" + n], l, stacked[n])

    for n in BIG:
        for l in reversed(range(A[n].shape[0])):
            if (n, l) not in layer0:
                adamw(n, l)
    behind = [dx] + [stacked[n][0] for n in BIG if stacked[n] is not None]
    finish_exchange("early0", exchange_early0, behind)
    finish_exchange("late0", exchange_late0, behind)
    for n, l in layer0:
        adamw(n, l)
    outs = {n: [o.reshape(A[n].shape) for o in stacked[n]] for n in BIG}
    red = _sum8("sum_small", partial[("small", 0)])
    red_full = _all_gather("gather_small", [red], [0])[0]
    small_g = dict(zip(small_names, _unpack(red_full, small_shapes)))
    for n in SMALL_SHARDED:
        width = A[n].shape[-1]
        small_g[n] = lax.dynamic_slice_in_dim(small_g[n], me * width, width, axis=small_g[n].ndim - 1)
    pk = lambda pre: _pack([A[pre + n] for n in small_names])
    d_p, m_p, v_p = _adamw_packed("adamw_small", pk(""), _pack([small_g[n] for n in small_names]), pk("m_"), pk("v_"))
    shapes = [A[n].shape for n in small_names]
    for n, d_, m_, v_ in zip(small_names, _unpack(d_p, shapes), _unpack(m_p, shapes), _unpack(v_p, shapes)):
        outs[n] = [small_g[n], d_, m_, v_]

    return (loss, grad_x, *[outs[n][0] for n in WEIGHTS], *[outs[n][1] for n in WEIGHTS],
            *[outs[n][2] for n in WEIGHTS], *[outs[n][3] for n in WEIGHTS])
```

```python
import functools
import math

import jax
import jax.numpy as jnp
from jax import lax
from jax.experimental import pallas as pl
from jax.experimental.pallas import tpu as pltpu

F32 = jnp.float32
BF16 = jnp.bfloat16
MESH = pl.DeviceIdType.MESH
N_DEV = 8
LANES = 128
SUBLANES = 8

LN_EPS = 1e-5
LRU_C = 8.0
CONV_WIDTH = 4
POOL_HALO = 16
ADAM_LR = 0.001
ADAM_B1 = 0.9
ADAM_B2 = 0.999
ADAM_EPS = 1e-08
ADAM_WD = 0.01
ADAM_STEP = 10

VMEM_LIMIT = 48 * 1024 * 1024
SEQ_CHUNK = 256
MM_TK = 4096
MM_TK_TOKENS = 4096
MM_TM_ROWS = 512
MM_TN = 512
MM_TN_WIDE = 1024
MM_TN_WIDE_MAX_K = 2048
GELU_C0 = math.sqrt(2.0 / math.pi)
GELU_C1 = 0.044715


def _cparams(*sem):
    return pltpu.CompilerParams(dimension_semantics=tuple(sem) if sem else None, vmem_limit_bytes=VMEM_LIMIT)


def _tile(n, pref):
    if n <= pref:
        return n
    t = pref - pref % LANES
    while t >= LANES:
        if n % t == 0:
            return t
        t -= LANES
    return n


def _row_tile(n, pref):
    if n <= pref:
        return n
    t = pref - pref % SUBLANES
    while t >= SUBLANES:
        if n % t == 0:
            return t
        t -= SUBLANES
    return n


def _mm(name, a, b, mode, out_dtypes, epi=None, extras=(), a_lead=None, b_lead=None, tm=1024, tn=None, tk=None,
        after=(), n_sums=0):
    if isinstance(b, (list, tuple)):
        b, b_lead = b[b_lead], None
    a2 = a.shape[-2:]
    b2 = b.shape[-2:]
    if mode == "nn":
        (M, K), N = a2, b2[1]
        assert b2[0] == K
    elif mode == "nt":
        (M, K), N = a2, b2[0]
        assert b2[1] == K
    else:
        (K, M), N = a2, b2[1]
        assert b2[0] == K
    if tk is None:
        tk = MM_TK_TOKENS if mode == "tn" else MM_TK
    if tn is None:
        tn = MM_TN_WIDE if (mode != "tn" and K <= MM_TN_WIDE_MAX_K) else MM_TN
    tm, tn, tk = _tile(M, tm), _tile(N, tn), _tile(K, tk)
    nk = K // tk
    n_extra = len(extras)
    n_out = len(out_dtypes)
    assert n_sums == 0 or tn == N
    resident = {"pipeline_mode": pl.Buffered(1)} if (tn == N and nk == 1) else {}

    def lead(shape, idx, which, **kw):
        if which is None:
            return pl.BlockSpec(shape, idx, **kw)
        return pl.BlockSpec((None,) + shape, lambda i, j, k: (which,) + idx(i, j, k), **kw)

    if mode == "nn":
        a_spec = lead((tm, tk), lambda i, j, k: (i, k), a_lead)
        b_spec = lead((tk, tn), lambda i, j, k: (k, j), b_lead, **resident)
        dims = (((1,), (0,)), ((), ()))
    elif mode == "nt":
        a_spec = lead((tm, tk), lambda i, j, k: (i, k), a_lead)
        b_spec = lead((tn, tk), lambda i, j, k: (j, k), b_lead, **resident)
        dims = (((1,), (1,)), ((), ()))
    else:
        a_spec = lead((tk, tm), lambda i, j, k: (k, i), a_lead)
        b_spec = lead((tk, tn), lambda i, j, k: (k, j), b_lead, **resident)
        dims = (((0,), (0,)), ((), ()))
    e_specs = []
    for e in extras:
        if e.shape[0] == 1:
            e_specs.append(pl.BlockSpec((1, tn), lambda i, j, k: (0, j)))
        else:
            e_specs.append(pl.BlockSpec((tm, tn), lambda i, j, k: (i, j)))

    n_after = len(after)

    def body(a_ref, b_ref, *rest):
        e_refs = rest[:n_extra]
        rest = rest[:n_extra] + rest[n_extra + n_after:]
        o_refs = rest[n_extra:n_extra + n_out]
        s_refs = rest[n_extra + n_out:n_extra + n_out + n_sums]
        part = lax.dot_general(a_ref[...].astype(BF16), b_ref[...].astype(BF16), dims, preferred_element_type=F32)

        def finish(r):
            res = (r,) if epi is None else epi(r, *[e[...] for e in e_refs])
            for o, v in zip(o_refs, res[:n_out]):
                o[...] = v.astype(o.dtype)
            first = pl.program_id(0) == 0
            for sr, v in zip(s_refs, res[n_out:]):
                @pl.when(first)
                def _(sr=sr, v=v):
                    sr[...] = v

                @pl.when(jnp.logical_not(first))
                def _(sr=sr, v=v):
                    sr[...] += v

        if nk == 1:
            finish(part)
            return
        acc = rest[n_extra + n_out + n_sums]
        k = pl.program_id(2)

        @pl.when(k == 0)
        def _():
            acc[...] = part

        @pl.when(jnp.logical_and(k > 0, k < nk - 1))
        def _():
            acc[...] += part

        @pl.when(k == nk - 1)
        def _():
            finish(acc[...] + part)

    outs = pl.pallas_call(
        body,
        name=name,
        grid=(M // tm, N // tn, nk),
        in_specs=[a_spec, b_spec] + e_specs + [pl.BlockSpec(memory_space=pl.ANY)] * n_after,
        out_specs=[pl.BlockSpec((tm, tn), lambda i, j, k: (i, j)) for _ in out_dtypes]
        + [pl.BlockSpec((1, tn), lambda i, j, k: (0, 0))] * n_sums,
        out_shape=[jax.ShapeDtypeStruct((M, N), d) for d in out_dtypes] + [jax.ShapeDtypeStruct((1, N), F32)] * n_sums,
        scratch_shapes=[pltpu.VMEM((tm, tn), F32)] if nk > 1 else [],
        compiler_params=_cparams(*(("arbitrary",) * 3 if n_sums else ("parallel", "parallel", "arbitrary"))),
    )(a, b, *extras, *after)
    return outs[0] if n_out + n_sums == 1 else tuple(outs)


def _rowwise(name, fn, tiled, params, outs, accs=(), tm=256, after=()):
    S = tiled[0].shape[0]
    tm = _tile(S, tm)
    nt, npar, no = len(tiled), len(params), len(outs)
    n_after = len(after)

    def body(*refs):
        t_refs = refs[:nt]
        p_refs = refs[nt:nt + npar]
        refs = refs[nt + npar + n_after:]
        o_refs = refs[:no]
        a_refs = refs[no:]
        res = fn(*[r[...] for r in t_refs], *[r[...] for r in p_refs])
        for o, v in zip(o_refs, res[:no]):
            o[...] = v.astype(o.dtype)
        first = pl.program_id(0) == 0
        for ar, v in zip(a_refs, res[no:]):
            @pl.when(first)
            def _(ar=ar, v=v):
                ar[...] = v

            @pl.when(jnp.logical_not(first))
            def _(ar=ar, v=v):
                ar[...] += v

    full = lambda p: pl.BlockSpec(p.shape, lambda i, nd=p.ndim: (0,) * nd)
    res = pl.pallas_call(
        body,
        name=name,
        grid=(S // tm,),
        in_specs=[pl.BlockSpec((tm, t.shape[1]), lambda i: (i, 0)) for t in tiled] + [full(p) for p in params]
        + [pl.BlockSpec(memory_space=pl.ANY)] * n_after,
        out_specs=[pl.BlockSpec((tm, c), lambda i: (i, 0)) for c, _ in outs]
        + [pl.BlockSpec(s, lambda i, nd=len(s): (0,) * nd) for s in accs],
        out_shape=[jax.ShapeDtypeStruct((S, c), d) for c, d in outs] + [jax.ShapeDtypeStruct(s, F32) for s in accs],
        compiler_params=_cparams("arbitrary"),
    )(*tiled, *params, *after)
    return res


def _ln_stats(z):
    mu = jnp.mean(z, axis=-1, keepdims=True)
    zc = z - mu
    var = jnp.mean(zc * zc, axis=-1, keepdims=True)
    return zc, lax.rsqrt(var + LN_EPS)


def _ln_apply(z, g, b):
    zc, rstd = _ln_stats(z)
    y = zc * rstd * g + b
    return z, y, y


def _ln_grad(dy, z, g):
    zc, rstd = _ln_stats(z)
    xhat = zc * rstd
    dxh = dy * g
    m1 = jnp.mean(dxh, axis=-1, keepdims=True)
    m2 = jnp.mean(dxh * xhat, axis=-1, keepdims=True)
    dz = rstd * (dxh - m1 - xhat * m2)
    return dz, dz, jnp.sum(dy * xhat, axis=0, keepdims=True), jnp.sum(dy, axis=0, keepdims=True)


def _ple_grad(dx3, gpre, pp):
    gate = jax.nn.sigmoid(gpre)
    dgpre = dx3 * pp * gate * (1.0 - gate)
    return dx3 * gate, dgpre, jnp.sum(dgpre, axis=0, keepdims=True)


def _loss_and_grad(name, y, target, gpre, pp):
    d = y.shape[1]

    def fn(y, t, gpre, pp):
        err = y - t
        sq = jnp.sum(jnp.sum(err * err, axis=0, keepdims=True), axis=1, keepdims=True)
        dy = err * (1.0 / d)
        dpp, dgpre, dbg = _ple_grad(dy, gpre, pp)
        return dy, dpp, dgpre, jnp.broadcast_to(sq, (1, LANES)), dbg

    return _rowwise(name, fn, [y, target, gpre, pp], [], [(d, F32), (d, BF16), (d, BF16)], accs=[(1, LANES), (1, d)])


def _rows(shape):
    return lax.broadcasted_iota(jnp.int32, shape, 0)


def _gelu(y):
    t = jnp.tanh(GELU_C0 * (y + GELU_C1 * y * y * y))
    return 0.5 * y * (1.0 + t), t


def _gelu_grad(y, t):
    return 0.5 * (1.0 + t) + 0.5 * y * (1.0 - t * t) * GELU_C0 * (1.0 + 3.0 * GELU_C1 * y * y)


def _neg_expm1(x):
    series = -x * (1.0 + x * (0.5 + x * (1.0 / 6.0 + x * (1.0 / 24.0))))
    return jnp.where(x > -0.02, series, 1.0 - jnp.exp(x))


def _softplus(x):
    return jnp.maximum(x, 0.0) + jnp.log(1.0 + jnp.exp(-jnp.abs(x)))


def _conv_fwd(xs, cw, cb):
    n = xs.shape[0]
    u = cw[3:4] * xs
    for k in (1, 2, 3):
        u = u + cw[3 - k:4 - k] * pltpu.roll(xs, k, 0)
    del n
    return u[SUBLANES:] + cb


def _lru_gates(u, wa, wx, ba, bx, sp, grow):
    ub = u.astype(BF16)
    r = jax.nn.sigmoid(jnp.dot(ub, wa, preferred_element_type=F32) + ba)
    ig = jax.nn.sigmoid(jnp.dot(ub, wx, preferred_element_type=F32) + bx)
    log_a = (-LRU_C) * r * sp
    a = jnp.exp(log_a)
    mult = jnp.sqrt(_neg_expm1(2.0 * log_a))
    mult = jnp.where(grow == 0, 1.0, mult)
    return ub, r, ig, a, mult


def _scan8_fwd(a, b):
    row = _rows(a.shape)
    for k in (1, 2, 4):
        m = row >= k
        b = jnp.where(m, a * pltpu.roll(b, k, 0) + b, b)
        a = jnp.where(m, a * pltpu.roll(a, k, 0), a)
    return a, b


def _scan8_bwd(c, d):
    row = _rows(c.shape)
    for k in (1, 2, 4):
        m = row < SUBLANES - k
        d = jnp.where(m, c * pltpu.roll(d, SUBLANES - k, 0) + d, d)
        c = jnp.where(m, c * pltpu.roll(c, SUBLANES - k, 0), c)
    return c, d


def _pad_copy(dst, src, front, back):
    s, c = src.shape
    if front:
        dst[pl.ds(0, front), :] = jnp.zeros((front, c), dst.dtype)
    if back:
        dst[pl.ds(front + s, back), :] = jnp.zeros((back, c), dst.dtype)
    dst[pl.ds(front, s), :] = src[...].astype(dst.dtype)


def _lru_fwd(name, proj, par, wa, wx):
    S = proj.shape[0]
    R = proj.shape[1] // 2
    H = R // LANES
    ch = _tile(S, SEQ_CHUNK)
    nch = S // ch
    H8 = SUBLANES

    def body(up_ref, y_ref, par_ref, wa_ref, wx_ref, gh_ref, h_ref, up_pad):
        _pad_copy(up_pad, up_ref, H8, 0)
        par = par_ref[...]
        cw, cb, ba, bx = par[0:4], par[4:5], par[5:6], par[6:7]
        sp = _softplus(-par[7:8])
        wa_m, wx_m = wa_ref[...], wx_ref[...]

        def chunk(ci, carry):
            r0 = pl.multiple_of(ci * ch, ch)
            xs = up_pad[pl.ds(r0, ch + H8), :]
            u = _conv_fwd(xs, cw, cb)
            grow = _rows(u.shape) + r0
            _, _, ig, a, mult = _lru_gates(u, wa_m, wx_m, ba, bx, sp, grow)
            bt = mult * (ig * u)
            hs = []
            for j in range(ch // H8):
                aa, bb = _scan8_fwd(a[j * H8:(j + 1) * H8], bt[j * H8:(j + 1) * H8])
                hj = bb + aa * carry
                carry = hj[H8 - 1:H8]
                hs.append(hj)
            h = jnp.concatenate(hs, axis=0)
            h_ref[pl.ds(r0, ch), :] = h
            gy, _ = _gelu(y_ref[pl.ds(r0, ch), :])
            gh_ref[pl.ds(r0, ch), :] = (h * gy).astype(gh_ref.dtype)
            return carry

        lax.fori_loop(0, nch, chunk, jnp.zeros((1, LANES), F32))

    col = lambda off: pl.BlockSpec((S, LANES), lambda h: (0, h + off))
    return pl.pallas_call(
        body,
        name=name,
        grid=(H,),
        in_specs=[col(0), col(H), pl.BlockSpec((8, LANES), lambda h: (0, h)),
                  pl.BlockSpec((None, LANES, LANES), lambda h: (h, 0, 0)),
                  pl.BlockSpec((None, LANES, LANES), lambda h: (h, 0, 0))],
        out_specs=[col(0), col(0)],
        out_shape=[jax.ShapeDtypeStruct((S, R), BF16), jax.ShapeDtypeStruct((S, R), F32)],
        scratch_shapes=[pltpu.VMEM((S + H8, LANES), F32)],
        compiler_params=_cparams("parallel"),
    )(proj, proj, par, wa, wx)


def _lru_bwd(name, proj, h, dgh, par, wa, wx):
    S = proj.shape[0]
    R = proj.shape[1] // 2
    H = R // LANES
    ch = _tile(S, SEQ_CHUNK)
    nch = S // ch
    H8 = SUBLANES
    nb = ch // H8

    def body(up_ref, y_ref, h_ref, dgh_ref, par_ref, wa_ref, wx_ref,
             dup_ref, dy_ref, dwa_ref, dwx_ref, dpar_ref, up_pad, h_pad, du_pad, vec_acc):
        _pad_copy(up_pad, up_ref, H8, 0)
        _pad_copy(h_pad, h_ref, H8, 0)
        du_pad[pl.ds(S, H8), :] = jnp.zeros((H8, LANES), F32)
        par = par_ref[...]
        cw, cb, ba, bx, lam = par[0:4], par[4:5], par[5:6], par[6:7], par[7:8]
        sp = _softplus(-lam)
        wa_m, wx_m = wa_ref[...], wx_ref[...]
        dwa_ref[...] = jnp.zeros_like(dwa_ref)
        dwx_ref[...] = jnp.zeros_like(dwx_ref)
        vec_acc[...] = jnp.zeros_like(vec_acc)
        nt_dims = (((1,), (1,)), ((), ()))
        tn_dims = (((0,), (0,)), ((), ()))

        def chunk(it, carry):
            lam_next, a_next = carry
            ci = nch - 1 - it
            r0 = pl.multiple_of(ci * ch, ch)
            xs = up_pad[pl.ds(r0, ch + H8), :]
            u = _conv_fwd(xs, cw, cb)
            row = _rows(u.shape)
            grow = row + r0
            ub, r, ig, a, mult = _lru_gates(u, wa_m, wx_m, ba, bx, sp, grow)
            hs = h_pad[pl.ds(r0, ch + H8), :]
            hcur = hs[H8:]
            hprev = pltpu.roll(hs, 1, 0)[H8:]
            y = y_ref[pl.ds(r0, ch), :]
            dgh = dgh_ref[pl.ds(r0, ch), :]
            gy, t = _gelu(y)
            dy_ref[pl.ds(r0, ch), :] = (dgh * hcur * _gelu_grad(y, t)).astype(dy_ref.dtype)
            dh = dgh * gy
            c = jnp.where(row == ch - 1, a_next, pltpu.roll(a, ch - 1, 0))
            ls = [None] * nb
            for j in range(nb - 1, -1, -1):
                cc, dd = _scan8_bwd(c[j * H8:(j + 1) * H8], dh[j * H8:(j + 1) * H8])
                lj = dd + cc * lam_next
                lam_next = lj[0:1]
                ls[j] = lj
            lmb = jnp.concatenate(ls, axis=0)
            da = lmb * hprev
            gu = ig * u
            dmult = lmb * gu
            dlog_a = da * a + jnp.where(grow == 0, 0.0, dmult * (-(a * a) / mult))
            dr = dlog_a * ((-LRU_C) * sp)
            drp = dr * r * (1.0 - r)
            dip = (lmb * mult * u) * ig * (1.0 - ig)
            drb, dib = drp.astype(BF16), dip.astype(BF16)
            du = (lmb * mult * ig
                  + lax.dot_general(drb, wa_m, nt_dims, preferred_element_type=F32)
                  + lax.dot_general(dib, wx_m, nt_dims, preferred_element_type=F32))
            du_pad[pl.ds(r0, ch), :] = du
            dwa_ref[...] += lax.dot_general(ub, drb, tn_dims, preferred_element_type=F32)
            dwx_ref[...] += lax.dot_general(ub, dib, tn_dims, preferred_element_type=F32)
            ssum = lambda v: jnp.sum(v, axis=0, keepdims=True)
            vec_acc[0:1, :] += ssum(drp)
            vec_acc[1:2, :] += ssum(dip)
            vec_acc[2:3, :] += ssum(dlog_a * ((-LRU_C) * r))
            return lam_next, a[0:1]

        zero = jnp.zeros((1, LANES), F32)
        lax.fori_loop(0, nch, chunk, (zero, zero))

        def conv_chunk(ci, acc):
            r0 = pl.multiple_of(ci * ch, ch)
            ds = du_pad[pl.ds(r0, ch + H8), :]
            xs = up_pad[pl.ds(r0, ch + H8), :]
            n = ch + H8
            du = ds[:ch]
            dup = cw[3:4] * du
            new = [acc[3] + jnp.sum(du * xs[H8:], axis=0, keepdims=True)]
            for k in (1, 2, 3):
                dup = dup + cw[3 - k:4 - k] * pltpu.roll(ds, n - k, 0)[:ch]
                new.append(acc[3 - k] + jnp.sum(du * pltpu.roll(xs, k, 0)[H8:], axis=0, keepdims=True))
            dup_ref[pl.ds(r0, ch), :] = dup.astype(dup_ref.dtype)
            return (new[3], new[2], new[1], new[0], acc[4] + jnp.sum(du, axis=0, keepdims=True))

        acc = lax.fori_loop(0, nch, conv_chunk, (zero,) * 5)
        dlam = vec_acc[2:3, :] * (-jax.nn.sigmoid(-lam))
        dpar_ref[...] = jnp.concatenate(list(acc) + [vec_acc[0:1, :], vec_acc[1:2, :], dlam], axis=0)

    col = lambda off: pl.BlockSpec((S, LANES), lambda h: (0, h + off))
    head = pl.BlockSpec((None, LANES, LANES), lambda h: (h, 0, 0))
    return pl.pallas_call(
        body,
        name=name,
        grid=(H,),
        in_specs=[col(0), col(H), col(0), col(0), pl.BlockSpec((8, LANES), lambda h: (0, h)), head, head],
        out_specs=[col(0), col(0), head, head, pl.BlockSpec((8, LANES), lambda h: (0, h))],
        out_shape=[jax.ShapeDtypeStruct((S, R), BF16), jax.ShapeDtypeStruct((S, R), BF16),
                   jax.ShapeDtypeStruct((H, LANES, LANES), F32), jax.ShapeDtypeStruct((H, LANES, LANES), F32),
                   jax.ShapeDtypeStruct((8, R), F32)],
        scratch_shapes=[pltpu.VMEM((S + H8, LANES), F32), pltpu.VMEM((S + H8, LANES), F32),
                        pltpu.VMEM((S + H8, LANES), F32), pltpu.VMEM((8, LANES), F32)],
        compiler_params=_cparams("parallel"),
    )(proj, proj, h, dgh, par, wa, wx)


def _window_sum(xs, g, up):
    n = xs.shape[0]
    s = xs
    for lvl, k in enumerate((1, 2, 4, 8)):
        sh = pltpu.roll(s, (n - k) if up else k, 0)
        s = s + jnp.where(g >= lvl, sh, 0.0)
    return s


def _pool_count(grow, g):
    return jnp.minimum(grow + 1, lax.shift_left(jnp.int32(2), g)).astype(F32)


def _pool_fwd(name, u, wgrp, par):
    S, D = u.shape
    G, W = wgrp.shape[0], wgrp.shape[1]
    ch = _tile(S, SEQ_CHUNK)
    nch = S // ch
    PH = POOL_HALO

    def body(u_ref, w_ref, par_ref, zs_ref, u_pad):
        g = pl.program_id(0)
        _pad_copy(u_pad, u_ref, PH, 0)
        par = par_ref[...]
        w = w_ref[...]

        def chunk(ci, _):
            r0 = pl.multiple_of(ci * ch, ch)
            xs = u_pad[pl.ds(r0, ch + PH), :]
            ws = _window_sum(xs, g, False)[PH:]
            uc = xs[PH:]
            cnt = _pool_count(_rows(uc.shape) + r0, g)
            pooled = ws / cnt - uc
            z = jnp.dot(pooled.astype(BF16), w, preferred_element_type=F32) + par[0:1]
            zs_ref[pl.ds(r0, ch), :] = (z * par[1:2]).astype(zs_ref.dtype)
            return 0

        lax.fori_loop(0, nch, chunk, 0)

    return pl.pallas_call(
        body,
        name=name,
        grid=(G,),
        in_specs=[pl.BlockSpec((S, W), lambda g: (0, g)), pl.BlockSpec((None, W, W), lambda g: (g, 0, 0)),
                  pl.BlockSpec((2, W), lambda g: (0, g))],
        out_specs=pl.BlockSpec((S, W), lambda g: (0, g)),
        out_shape=jax.ShapeDtypeStruct((S, D), BF16),
        scratch_shapes=[pltpu.VMEM((S + PH, W), F32)],
        compiler_params=_cparams("parallel"),
    )(u, wgrp, par)


def _pool_bwd(name, u, dzs, wgrp, par):
    S, D = u.shape
    G, W = wgrp.shape[0], wgrp.shape[1]
    ch = _tile(S, SEQ_CHUNK)
    nch = S // ch
    PH = POOL_HALO

    def body(u_ref, dzs_ref, w_ref, par_ref, du_ref, dw_ref, dpar_ref, u_pad, q_pad, dw_acc):
        g = pl.program_id(0)
        _pad_copy(u_pad, u_ref, PH, 0)
        q_pad[pl.ds(S, PH), :] = jnp.zeros((PH, W), F32)
        par = par_ref[...]
        w = w_ref[...]
        dw_acc[...] = jnp.zeros_like(dw_acc)

        def chunk(ci, acc):
            db, dsc = acc
            r0 = pl.multiple_of(ci * ch, ch)
            xs = u_pad[pl.ds(r0, ch + PH), :]
            ws = _window_sum(xs, g, False)[PH:]
            uc = xs[PH:]
            cnt = _pool_count(_rows(uc.shape) + r0, g)
            pooled = (ws / cnt - uc).astype(BF16)
            z = jnp.dot(pooled, w, preferred_element_type=F32) + par[0:1]
            dzs = dzs_ref[pl.ds(r0, ch), :]
            dz = dzs * par[1:2]
            dzb = dz.astype(BF16)
            dw_acc[...] += lax.dot_general(pooled, dzb, (((0,), (0,)), ((), ())), preferred_element_type=F32)
            dpooled = lax.dot_general(dzb, w, (((1,), (1,)), ((), ())), preferred_element_type=F32)
            q_pad[pl.ds(r0, ch), :] = dpooled / cnt
            return (db + jnp.sum(dz, axis=0, keepdims=True), dsc + jnp.sum(dzs * z, axis=0, keepdims=True))

        zero = jnp.zeros((1, W), F32)
        db, dsc = lax.fori_loop(0, nch, chunk, (zero, zero))
        dpar_ref[...] = jnp.concatenate([db, dsc], axis=0)
        dw_ref[...] = dw_acc[...].astype(dw_ref.dtype)

        def back(ci, _):
            r0 = pl.multiple_of(ci * ch, ch)
            qs = q_pad[pl.ds(r0, ch + PH), :]
            ws = _window_sum(qs, g, True)[:ch]
            qc = qs[:ch]
            cnt = _pool_count(_rows(qc.shape) + r0, g)
            du_ref[pl.ds(r0, ch), :] = (ws - qc * cnt).astype(du_ref.dtype)
            return 0

        lax.fori_loop(0, nch, back, 0)

    blk = pl.BlockSpec((S, W), lambda g: (0, g))
    wspec = pl.BlockSpec((None, W, W), lambda g: (g, 0, 0))
    pspec = pl.BlockSpec((2, W), lambda g: (0, g))
    return pl.pallas_call(
        body,
        name=name,
        grid=(G,),
        in_specs=[blk, blk, wspec, pspec],
        out_specs=[blk, wspec, pspec],
        out_shape=[jax.ShapeDtypeStruct((S, D), BF16), jax.ShapeDtypeStruct((G, W, W), BF16),
                   jax.ShapeDtypeStruct((2, D), F32)],
        scratch_shapes=[pltpu.VMEM((S + PH, W), F32), pltpu.VMEM((S + PH, W), F32), pltpu.VMEM((W, W), F32)],
        compiler_params=_cparams("parallel"),
    )(u, dzs, wgrp, par)


def _my_place():
    x, y, c = lax.axis_index("x"), lax.axis_index("y"), lax.axis_index("c")
    return x, y, c, 4 * x + 2 * y + c


def _peers(x, y, c):
    out = []
    for d in range(1, N_DEV):
        px = 1 - x if d & 4 else x
        py = 1 - y if d & 2 else y
        pc = 1 - c if d & 1 else c
        out.append(((px, py, pc), 4 * px + 2 * py + pc))
    return out


def _window(ref, axis, start, size):
    idx = [slice(None)] * len(ref.shape)
    idx[axis] = pl.ds(start, size)
    return ref.at[tuple(idx)]


def _to_bf16(name, arrs):
    outs = []
    for i, a in enumerate(arrs):
        a2 = a.reshape(-1, a.shape[-1])
        tr = _tile(a2.shape[0], 512)
        o = pl.pallas_call(
            lambda a_ref, o_ref: o_ref.__setitem__(Ellipsis, a_ref[...].astype(BF16)),
            name=f"{name}_{i}",
            grid=(a2.shape[0] // tr,),
            in_specs=[pl.BlockSpec((tr, a2.shape[1]), lambda r: (r, 0))],
            out_specs=pl.BlockSpec((tr, a2.shape[1]), lambda r: (r, 0)),
            out_shape=jax.ShapeDtypeStruct(a2.shape, BF16),
            compiler_params=_cparams("parallel"),
        )(a2)
        outs.append(o.reshape(a.shape))
    return outs


def _all_gather(name, shards, axes):
    n = len(shards)
    sizes = [s.shape[ax] for s, ax in zip(shards, axes)]

    def body(*refs):
        ins, outs = refs[:n], refs[n:2 * n]
        send, recv, loc = refs[2 * n:]
        x, y, c, me = _my_place()
        peers = _peers(x, y, c)
        local = []
        for i in range(n):
            dst = _window(outs[i], axes[i], me * sizes[i], sizes[i])
            cp = pltpu.make_async_copy(ins[i], dst, loc.at[i])
            cp.start()
            local.append(cp)
            for peer, _ in peers:
                pltpu.make_async_remote_copy(src_ref=ins[i], dst_ref=dst, send_sem=send.at[i], recv_sem=recv.at[i],
                                             device_id=peer, device_id_type=MESH).start()
        for i in range(n):
            local[i].wait()
            seven = _window(outs[i], axes[i], 0, (N_DEV - 1) * sizes[i])
            pltpu.make_async_remote_copy(src_ref=seven, dst_ref=seven, send_sem=send.at[i], recv_sem=recv.at[i],
                                         device_id=(x, y, c), device_id_type=MESH).wait()

    def full_shape(s, ax):
        shp = list(s.shape)
        shp[ax] *= N_DEV
        return jax.ShapeDtypeStruct(tuple(shp), s.dtype)

    any_spec = pl.BlockSpec(memory_space=pl.ANY)
    return pl.pallas_call(
        body,
        name=name,
        in_specs=[any_spec] * n,
        out_specs=[any_spec] * n,
        out_shape=[full_shape(s, ax) for s, ax in zip(shards, axes)],
        scratch_shapes=[pltpu.SemaphoreType.DMA((n,)), pltpu.SemaphoreType.DMA((n,)), pltpu.SemaphoreType.DMA((n,))],
        compiler_params=pltpu.CompilerParams(has_side_effects=True),
    )(*shards)


HBM_SPEC = pl.BlockSpec(memory_space=pltpu.HBM)
SEM_SPEC = pl.BlockSpec(memory_space=pltpu.SEMAPHORE)
SPLIT_EFFECT = pltpu.SideEffectType.DATAFLOW_SIDE_EFFECTING


def _push_all(kind, src, dst, axis, size, send_sem, recv_sem, place):
    x, y, c, me = place
    for peer, pidx in _peers(x, y, c):
        if kind == "gather":
            s = d = _window(dst, axis, me * size, size)
        else:
            s, d = _window(src, axis, pidx * size, size), dst.at[me]
        pltpu.make_async_remote_copy(src_ref=s, dst_ref=d, send_sem=send_sem, recv_sem=recv_sem, device_id=peer,
                                     device_id_type=MESH).start()


def _drain_all(kind, dst, axis, size, send_sem, recv_sem, place):
    x, y, c, _ = place
    seven = _window(dst, axis, 0, (N_DEV - 1) * size) if kind == "gather" else dst.at[pl.ds(0, N_DEV - 1)]
    pltpu.make_async_remote_copy(src_ref=seven, dst_ref=seven, send_sem=send_sem, recv_sem=recv_sem,
                                 device_id=(x, y, c), device_id_type=MESH).wait()


def _own_block_placed(src, axis, size, me):
    own = lax.dynamic_slice_in_dim(src, me * size, size, axis)
    return lax.dynamic_update_slice_in_dim(lax.empty((N_DEV,) + own.shape, src.dtype), own[None], me, 0)


def _split_start(name, kind, srcs, lands, axes, sizes, after=()):
    n, ns, na = len(lands), len(srcs), len(after)

    def body(*refs):
        src_refs, land_refs = refs[:ns], refs[ns:ns + n]
        send, recv = refs[ns + n + na], refs[ns + n + na + 1]
        token = refs[-1]
        place = _my_place()
        for k in range(n):
            _push_all(kind, src_refs[k] if ns else None, land_refs[k], axes[k], sizes[k], send.at[k], recv.at[k], place)
        token[...] = jnp.zeros_like(token)

    hbm = lambda a: pltpu.HBM(a.shape, a.dtype)
    res = pl.pallas_call(
        body,
        name=name,
        out_shape=(pltpu.SemaphoreType.DMA((n,)), pltpu.SemaphoreType.DMA((n,)), *[hbm(a) for a in srcs],
                   *[hbm(a) for a in lands], jax.ShapeDtypeStruct((SUBLANES, LANES), F32)),
        in_specs=[HBM_SPEC] * (ns + n) + [pl.BlockSpec(memory_space=pl.ANY)] * na,
        out_specs=(SEM_SPEC, SEM_SPEC, *[HBM_SPEC] * (ns + n), pl.BlockSpec(memory_space=pltpu.VMEM)),
        input_output_aliases={k: 2 + k for k in range(ns + n)},
        compiler_params=pltpu.CompilerParams(has_side_effects=SPLIT_EFFECT),
    )(*[pltpu.with_memory_space_constraint(a, pltpu.HBM) for a in (*srcs, *lands)], *after)
    return res[0], res[1], list(res[2:2 + ns]), list(res[2 + ns:2 + ns + n]), res[-1]


def _split_wait(name, kind, handle, axes, sizes, after):
    send, recv, srcs, lands, _ = handle
    n, ns = len(lands), len(srcs)
    after = list(after) if isinstance(after, (list, tuple)) else [after]

    def body(*refs):
        land_refs = refs[ns:ns + n]
        send_ref, recv_ref = refs[ns + n], refs[ns + n + 1]
        place = _my_place()
        for k in range(n):
            _drain_all(kind, land_refs[k], axes[k], sizes[k], send_ref.at[k], recv_ref.at[k], place)

    hbm = lambda a: pltpu.HBM(a.shape, a.dtype)
    res = pl.pallas_call(
        body,
        name=name,
        out_shape=tuple(hbm(a) for a in (*srcs, *lands)),
        in_specs=[HBM_SPEC] * (ns + n) + [SEM_SPEC, SEM_SPEC] + [pl.BlockSpec(memory_space=pl.ANY)] * len(after),
        out_specs=tuple([HBM_SPEC] * (ns + n)),
        input_output_aliases={k: k for k in range(ns + n)},
        compiler_params=pltpu.CompilerParams(has_side_effects=SPLIT_EFFECT),
    )(*srcs, *lands, send, recv, *after)
    return list(res[ns:])


def _cast_into_window(name, a, l, axis, me1):
    shp = a.shape[1:]
    cast = lambda me_ref, a_ref, o_ref: o_ref.__setitem__(Ellipsis, a_ref[...].astype(BF16))
    if len(shp) == 3:
        assert axis == 1
        G, r, c = shp
        full = (G, r * N_DEV, c)
        grid = (G,)
        in_spec = pl.BlockSpec((None, None, r, c), lambda g, me: (l, g, 0, 0))
        out_spec = pl.BlockSpec((None, r, c), lambda g, me: (g, me[0], 0))
    else:
        r, c = shp
        tr = _tile(r, 512)
        nb = r // tr
        grid = (nb,)
        in_spec = pl.BlockSpec((None, tr, c), lambda i, me: (l, i, 0))
        if axis == 0:
            full = (r * N_DEV, c)
            out_spec = pl.BlockSpec((tr, c), lambda i, me: (me[0] * nb + i, 0))
        else:
            full = (r, c * N_DEV)
            out_spec = pl.BlockSpec((tr, c), lambda i, me: (i, me[0]))
    return pl.pallas_call(
        cast,
        name=name,
        grid_spec=pltpu.PrefetchScalarGridSpec(num_scalar_prefetch=1, grid=grid, in_specs=[in_spec], out_specs=out_spec),
        out_shape=jax.ShapeDtypeStruct(full, BF16),
        compiler_params=_cparams("arbitrary"),
    )(me1, a)


def _adamw_math(w, g, m, v):
    m = ADAM_B1 * m + (1.0 - ADAM_B1) * g
    v = ADAM_B2 * v + (1.0 - ADAM_B2) * jnp.square(g)
    m_hat = m / (1.0 - ADAM_B1 ** ADAM_STEP)
    v_hat = v / (1.0 - ADAM_B2 ** ADAM_STEP)
    delta = -ADAM_LR * (m_hat / (jnp.sqrt(v_hat) + ADAM_EPS) + ADAM_WD * w)
    return delta, m, v


def _sum_slots(buf_ref):
    g = buf_ref[0].astype(F32)
    for s in range(1, N_DEV):
        g = g + buf_ref[s].astype(F32)
    return g


def _adamw_layer(name, buf, w, m, v, l, prev):
    shape = w.shape
    L, C = shape[0], shape[-1]
    Rr = math.prod(shape[1:-1])
    buf3 = buf.reshape(N_DEV, Rr, C)
    w3, m3, v3 = (t.reshape(L, Rr, C) for t in (w, m, v))
    tr = _tile(Rr, 2 * LANES) if Rr % LANES == 0 else Rr
    n_prev = 0 if prev is None else 4

    def body(buf_ref, w_ref, m_ref, v_ref, *rest):
        g_out, d_out, m_out, v_out = rest[n_prev:]
        g = _sum_slots(buf_ref)
        d, mm, vv = _adamw_math(w_ref[...], g, m_ref[...], v_ref[...])
        g_out[...] = g
        d_out[...] = d
        m_out[...] = mm
        v_out[...] = vv

    spec = pl.BlockSpec((None, tr, C), lambda r: (l, r, 0))
    outs = pl.pallas_call(
        body,
        name=name,
        grid=(Rr // tr,),
        in_specs=[pl.BlockSpec((N_DEV, tr, C), lambda r: (0, r, 0)), spec, spec, spec]
        + [pl.BlockSpec(memory_space=pl.ANY)] * n_prev,
        out_specs=[spec] * 4,
        out_shape=[jax.ShapeDtypeStruct((L, Rr, C), F32)] * 4,
        input_output_aliases={4 + k: k for k in range(n_prev)},
        compiler_params=_cparams("parallel"),
    )(buf3, w3, m3, v3, *(prev or ()))
    return list(outs)


def _sum8(name, buf):
    R = buf.shape[1]

    def body(buf_ref, o_ref):
        o_ref[...] = _sum_slots(buf_ref)

    return pl.pallas_call(
        body,
        name=name,
        in_specs=[pl.BlockSpec(buf.shape, lambda: (0, 0, 0))],
        out_specs=pl.BlockSpec((R, LANES), lambda: (0, 0)),
        out_shape=jax.ShapeDtypeStruct((R, LANES), F32),
        compiler_params=_cparams(),
    )(buf)


def _adamw_packed(name, w, g, m, v):
    R = w.shape[0]
    tr = _row_tile(R, 512)

    def body(w_ref, g_ref, m_ref, v_ref, d_out, m_out, v_out):
        d, mm, vv = _adamw_math(w_ref[...], g_ref[...], m_ref[...], v_ref[...])
        d_out[...] = d
        m_out[...] = mm
        v_out[...] = vv

    spec = pl.BlockSpec((tr, LANES), lambda r: (r, 0))
    return pl.pallas_call(
        body,
        name=name,
        grid=(R // tr,),
        in_specs=[spec] * 4,
        out_specs=[spec] * 3,
        out_shape=[jax.ShapeDtypeStruct((R, LANES), F32)] * 3,
        compiler_params=_cparams("parallel"),
    )(w, g, m, v)


def _pack(arrs, pad_rows_to=SUBLANES):
    parts = []
    for a in arrs:
        flat = a.reshape(-1)
        per = LANES * pad_rows_to
        padded = -(-flat.shape[0] // per) * per
        if padded != flat.shape[0]:
            flat = jnp.pad(flat, (0, padded - flat.shape[0]))
        parts.append(flat.reshape(-1, LANES))
    return jnp.concatenate(parts, axis=0)


def _unpack(packed, shapes, pad_rows_to=SUBLANES):
    out = []
    r = 0
    for shp in shapes:
        nel = math.prod(shp)
        per = LANES * pad_rows_to
        rows = -(-nel // per) * pad_rows_to
        out.append(packed[r:r + rows].reshape(-1)[:nel].reshape(shp))
        r += rows
    return out


BIG = ("lru_w_in", "lru_w_out", "pool_w_in", "pool_w_grp", "pool_w_out", "mlp_w1", "mlp_w2", "ple_w", "ple_gate_w")
BIG_AXIS = {"lru_w_in": 2, "lru_w_out": 1, "pool_w_in": 1, "pool_w_grp": 2, "pool_w_out": 1, "mlp_w1": 2,
            "mlp_w2": 1, "ple_w": 2, "ple_gate_w": 1}
SMALL_SHARDED = ("lru_conv_w", "pool_b_grp", "pool_scale")
REPLICATED = ("lru_conv_b", "lru_wa", "lru_ba", "lru_wx", "lru_bx", "lru_lambda", "ln_mix_g", "ln_mix_b",
              "ln_mlp_g", "ln_mlp_b", "ple_gate_b")
WEIGHTS = ("lru_w_in", "lru_conv_w", "lru_conv_b", "lru_wa", "lru_ba", "lru_wx", "lru_bx", "lru_lambda", "lru_w_out",
           "pool_w_in", "pool_w_grp", "pool_b_grp", "pool_scale", "pool_w_out", "ln_mix_g", "ln_mix_b", "mlp_w1",
           "mlp_w2", "ln_mlp_g", "ln_mlp_b", "ple_w", "ple_gate_w", "ple_gate_b")
INPUTS = ("x", "p") + WEIGHTS + ("loss_target",) + tuple("m_" + n for n in WEIGHTS) + tuple("v_" + n for n in WEIGHTS)


def _gather_last_axis(packed_full, shard_shape):
    nel = math.prod(shard_shape)
    blocks = packed_full.reshape(N_DEV, -1)[:, :nel].reshape((N_DEV,) + tuple(shard_shape))
    return jnp.concatenate([blocks[d] for d in range(N_DEV)], axis=-1)


def kernel(x, p, lru_w_in, lru_conv_w, lru_conv_b, lru_wa, lru_ba, lru_wx, lru_bx, lru_lambda, lru_w_out, pool_w_in, pool_w_grp, pool_b_grp, pool_scale, pool_w_out, ln_mix_g, ln_mix_b, mlp_w1, mlp_w2, ln_mlp_g, ln_mlp_b, ple_w, ple_gate_w, ple_gate_b, loss_target, m_lru_w_in, m_lru_conv_w, m_lru_conv_b, m_lru_wa, m_lru_ba, m_lru_wx, m_lru_bx, m_lru_lambda, m_lru_w_out, m_pool_w_in, m_pool_w_grp, m_pool_b_grp, m_pool_scale, m_pool_w_out, m_ln_mix_g, m_ln_mix_b, m_mlp_w1, m_mlp_w2, m_ln_mlp_g, m_ln_mlp_b, m_ple_w, m_ple_gate_w, m_ple_gate_b, v_lru_w_in, v_lru_conv_w, v_lru_conv_b, v_lru_wa, v_lru_ba, v_lru_wx, v_lru_bx, v_lru_lambda, v_lru_w_out, v_pool_w_in, v_pool_w_grp, v_pool_b_grp, v_pool_scale, v_pool_w_out, v_ln_mix_g, v_ln_mix_b, v_mlp_w1, v_mlp_w2, v_ln_mlp_g, v_ln_mlp_b, v_ple_w, v_ple_gate_w, v_ple_gate_b):
    A = dict(zip(INPUTS, (x, p, lru_w_in, lru_conv_w, lru_conv_b, lru_wa, lru_ba, lru_wx, lru_bx, lru_lambda, lru_w_out, pool_w_in, pool_w_grp, pool_b_grp, pool_scale, pool_w_out, ln_mix_g, ln_mix_b, mlp_w1, mlp_w2, ln_mlp_g, ln_mlp_b, ple_w, ple_gate_w, ple_gate_b, loss_target, m_lru_w_in, m_lru_conv_w, m_lru_conv_b, m_lru_wa, m_lru_ba, m_lru_wx, m_lru_bx, m_lru_lambda, m_lru_w_out, m_pool_w_in, m_pool_w_grp, m_pool_b_grp, m_pool_scale, m_pool_w_out, m_ln_mix_g, m_ln_mix_b, m_mlp_w1, m_mlp_w2, m_ln_mlp_g, m_ln_mlp_b, m_ple_w, m_ple_gate_w, m_ple_gate_b, v_lru_w_in, v_lru_conv_w, v_lru_conv_b, v_lru_wa, v_lru_ba, v_lru_wx, v_lru_bx, v_lru_lambda, v_lru_w_out, v_pool_w_in, v_pool_w_grp, v_pool_b_grp, v_pool_scale, v_pool_w_out, v_ln_mix_g, v_ln_mix_b, v_mlp_w1, v_mlp_w2, v_ln_mlp_g, v_ln_mlp_b, v_ple_w, v_ple_gate_w, v_ple_gate_b)))
    depth = ln_mix_g.shape[0]
    alpha = (2 * depth) ** 0.25
    S, D = x.shape[1], x.shape[2]
    xs = x.reshape(S, D)
    tgt = loss_target.reshape(S, D)
    p3 = p.reshape(depth, S, p.shape[-1])
    me = 4 * lax.axis_index("x") + 2 * lax.axis_index("y") + lax.axis_index("c")

    def layer_weights(i):
        s = i // 2
        mixer = ("lru_w_in", "lru_w_out") if i % 2 == 0 else ("pool_w_in", "pool_w_grp", "pool_w_out")
        return [(n, s) for n in mixer] + [(n, i) for n in ("mlp_w1", "mlp_w2", "ple_w", "ple_gate_w")]

    def axis_of(key):
        return 0 if key[0] == "small" else BIG_AXIS[key[0]] - 1

    def start_gather(tag, keys, after):
        axes = [axis_of(k) for k in keys]
        lands = [land[k] for k in keys]
        sizes = [a.shape[ax] // N_DEV for a, ax in zip(lands, axes)]
        return keys, _split_start(f"gather_{tag}_start", "gather", [], lands, axes, sizes, after=after), axes, sizes

    def finish_gather(tag, pending, after):
        keys, handle, axes, sizes = pending
        for (n, l), full in zip(keys, _split_wait(f"gather_{tag}_wait", "gather", handle, axes, sizes, after)):
            W[n][l] = full

    def start_exchange(tag, keys, arrs, after):
        axes = [axis_of(k) for k in keys]
        sizes = [a.shape[ax] // N_DEV for a, ax in zip(arrs, axes)]
        lands = [_own_block_placed(a, ax, sz, me) for a, ax, sz in zip(arrs, axes, sizes)]
        return keys, _split_start(f"exchange_{tag}_start", "scatter", arrs, lands, axes, sizes, after=after), axes, sizes

    def finish_exchange(tag, pending, after):
        keys, handle, axes, sizes = pending
        partial.update(zip(keys, _split_wait(f"exchange_{tag}_wait", "scatter", handle, axes, sizes, after)))

    me1 = jnp.reshape(me, (1,)).astype(jnp.int32)
    land = {k: _cast_into_window(f"cast_{k[0]}_{k[1]}", A[k[0]], k[1], axis_of(k), me1)
            for i in range(depth) for k in layer_weights(i)}
    W = {n: [None] * A[n].shape[0] for n in BIG}
    small_shard_shapes = [A[n].shape for n in SMALL_SHARDED]
    gathered = _all_gather("gather_small_params", [_pack([A[n] for n in SMALL_SHARDED])], [0])
    def gather_groups(i):
        keys = layer_weights(i)
        mixer, (w1, w2, pw, pg) = keys[:-4], keys[-4:]
        if i == 0:
            return [("in", mixer[:1]), ("out", mixer[1:]), ("up", [w1]), ("rest", [w2, pw, pg])]
        return [("in", mixer + [w1]), ("rest", [w2, pw, pg])]

    gather_pending = {}
    token = gathered[0]
    for i in range(depth):
        for tag, keys in gather_groups(i):
            gather_pending[(i, tag)] = start_gather(f"l{i}_{tag}", keys, (token,))
            token = gather_pending[(i, tag)][1][4]
        if i == 0:
            finish_gather("l0_in", gather_pending[(0, "in")], token)
    all_started = token
    small_full = gathered[0].reshape(N_DEV, -1, LANES)
    r = 0
    for n, shp in zip(SMALL_SHARDED, small_shard_shapes):
        rows = -(-math.prod(shp) // (LANES * SUBLANES)) * SUBLANES
        W[n] = _gather_last_axis(small_full[:, r:r + rows], shp)
        r += rows
    wa_b, wx_b = _to_bf16("cast_gates", [lru_wa, lru_wx])
    n_lru = lru_w_in.shape[0]
    lru_par = [jnp.concatenate([W["lru_conv_w"][s], lru_conv_b[s][None], lru_ba[s][None], lru_bx[s][None],
                                lru_lambda[s][None]], axis=0) for s in range(n_lru)]
    pool_par = [jnp.stack([W["pool_b_grp"][s], W["pool_scale"][s]], axis=0) for s in range(pool_w_in.shape[0])]

    saved = []
    h_in = xs
    h_in_b, p3b = _to_bf16("cast_inputs", [xs, p3])
    for i in range(depth):
        s = i // 2
        sv = {"x0b": h_in_b}
        if i > 0:
            finish_gather(f"l{i}_in", gather_pending[(i, "in")], h_in)
        ln_out = dict(out_dtypes=[F32, F32, BF16], tm=MM_TM_ROWS, tn=D,
                      epi=lambda acc, xp, g, b: _ln_apply(alpha * xp + acc, g, b))
        if i % 2 == 0:
            sv["proj"] = _mm(f"l{i}_lru_in", h_in_b, W["lru_w_in"], "nn", [F32], b_lead=s)
            sv["gh"], sv["h"] = _lru_fwd(f"l{i}_lru_core", sv["proj"], lru_par[s], wa_b[s], wx_b[s])
            if i == 0:
                finish_gather("l0_out", gather_pending[(0, "out")], [sv["gh"], all_started])
            mix_in, mix_w = sv["gh"], W["lru_w_out"]
        else:
            sv["u"] = _mm(f"l{i}_pool_in", h_in_b, W["pool_w_in"], "nn", [F32], b_lead=s)
            sv["zs"] = _pool_fwd(f"l{i}_pool_core", sv["u"], W["pool_w_grp"][s], pool_par[s])
            mix_in, mix_w = sv["zs"], W["pool_w_out"]
        sv["z1"], sv["x1"], sv["x1b"] = _mm(f"l{i}_mix_out_ln", mix_in, mix_w, "nn", b_lead=s,
                                            extras=[h_in, ln_mix_g[i][None], ln_mix_b[i][None]], **ln_out)
        if i == 0:
            finish_gather("l0_up", gather_pending[(0, "up")], sv["x1b"])
        sv["hpre"], sv["hact"] = _mm(f"l{i}_mlp_up", sv["x1b"], W["mlp_w1"], "nn", [BF16, BF16], b_lead=i,
                                     epi=lambda acc: (acc, jnp.square(jnp.maximum(acc, 0.0))))
        finish_gather(f"l{i}_rest", gather_pending[(i, "rest")], sv["hact"])
        sv["z2"], sv["x2"], sv["x2b"] = _mm(f"l{i}_mlp_down_ln", sv["hact"], W["mlp_w2"], "nn", b_lead=i,
                                            extras=[sv["x1"], ln_mlp_g[i][None], ln_mlp_b[i][None]], **ln_out)
        sv["pp"] = _mm(f"l{i}_ple_up", p3b, W["ple_w"], "nn", [F32], a_lead=i, b_lead=i)

        def ple_epi(acc, bg, x2t, ppt):
            gpre = acc + bg
            x3 = x2t + ppt * jax.nn.sigmoid(gpre)
            return x3, x3, gpre

        h_in, h_in_b, sv["gpre"] = _mm(f"l{i}_ple_gate", sv["x2b"], W["ple_gate_w"], "nn", [F32, BF16, F32], b_lead=i,
                                       epi=ple_epi, extras=[ple_gate_b[i][None], sv["x2"], sv["pp"]], tn=MM_TN)
        saved.append(sv)

    dx, dpp, dgpre, sq, dbg = _loss_and_grad("loss", h_in, tgt, saved[-1]["gpre"], saved[-1]["pp"])
    loss = lax.psum(0.5 * sq[0, 0] / D, ("x", "y", "c"))

    dW = {n: [None] * A[n].shape[0] for n in BIG}
    dsmall = {n: [None] * A[n].shape[0] for n in REPLICATED + SMALL_SHARDED}
    small_names = REPLICATED + SMALL_SHARDED
    partial = {}
    exchange_pending = {}
    exchange_token = ()
    for i in reversed(range(depth)):
        s = i // 2
        sv = saved[i]
        dsmall["ple_gate_b"][i] = dbg[0]
        dW["ple_w"][i] = _mm(f"l{i}_d_ple_w", p3b, dpp, "tn", [BF16], a_lead=i, after=exchange_token)
        dW["ple_gate_w"][i] = _mm(f"l{i}_d_ple_gate_w", sv["x2b"], dgpre, "tn", [BF16])
        ln_back = dict(out_dtypes=[F32, BF16], tm=MM_TM_ROWS, tn=D, n_sums=2)
        dz2, dz2b, dg, db = _mm(f"l{i}_d_x2_ln", dgpre, W["ple_gate_w"], "nt", b_lead=i,
                                extras=[dx, sv["z2"], ln_mlp_g[i][None]],
                                epi=lambda acc, d, z, g: _ln_grad(acc + d, z, g), after=exchange_token, **ln_back)
        dsmall["ln_mlp_g"][i], dsmall["ln_mlp_b"][i] = dg[0], db[0]
        dhpre = _mm(f"l{i}_d_hpre", dz2b, W["mlp_w2"], "nt", [BF16], b_lead=i, extras=[sv["hpre"]],
                    epi=lambda acc, hp: (acc * (2.0 * jnp.maximum(hp.astype(F32), 0.0)),))
        dW["mlp_w2"][i] = _mm(f"l{i}_d_mlp_w2", sv["hact"], dz2b, "tn", [BF16])
        dW["mlp_w1"][i] = _mm(f"l{i}_d_mlp_w1", sv["x1b"], dhpre, "tn", [BF16])
        mlp_after = ()
        if i == 0:
            early = [(n, 0) for n in ("ple_w", "ple_gate_w", "mlp_w2", "mlp_w1")]
            exchange_early0 = start_exchange("early0", early, [dW[n][l] for n, l in early], ())
            mlp_after = (exchange_early0[1][4],)
        dz1, dz1b, dg, db = _mm(f"l{i}_d_x1_ln", dhpre, W["mlp_w1"], "nt", b_lead=i,
                                extras=[dz2, sv["z1"], ln_mix_g[i][None]],
                                epi=lambda acc, d, z, g: _ln_grad(acc + alpha * d, z, g), after=mlp_after, **ln_back)
        dsmall["ln_mix_g"][i], dsmall["ln_mix_b"][i] = dg[0], db[0]
        if i % 2 == 0:
            dW["lru_w_out"][s] = _mm(f"l{i}_d_lru_w_out", sv["gh"], dz1b, "tn", [BF16])
            dgh = _mm(f"l{i}_d_gh", dz1b, W["lru_w_out"], "nt", [F32], b_lead=s)
            dup, dy, dwa, dwx, dpar = _lru_bwd(f"l{i}_lru_core_bwd", sv["proj"], sv["h"], dgh, lru_par[s],
                                               wa_b[s], wx_b[s])
            dsmall["lru_wa"][s], dsmall["lru_wx"][s] = dwa, dwx
            dsmall["lru_conv_w"][s] = dpar[0:4]
            for k, n in enumerate(("lru_conv_b", "lru_ba", "lru_bx", "lru_lambda")):
                dsmall[n][s] = dpar[4 + k]
            dmix_in = jnp.concatenate([dup, dy], axis=1)
            win = "lru_w_in"
        else:
            dW["pool_w_out"][s] = _mm(f"l{i}_d_pool_w_out", sv["zs"], dz1b, "tn", [BF16])
            dzs = _mm(f"l{i}_d_zs", dz1b, W["pool_w_out"], "nt", [F32], b_lead=s)
            dmix_in, dW["pool_w_grp"][s], dpar = _pool_bwd(f"l{i}_pool_core_bwd", sv["u"], dzs, W["pool_w_grp"][s],
                                                          pool_par[s])
            dsmall["pool_b_grp"][s], dsmall["pool_scale"][s] = dpar[0], dpar[1]
            win = "pool_w_in"
        dW[win][s] = _mm(f"l{i}_d_{win}", sv["x0b"], dmix_in, "tn", [BF16])
        x0_after = ()
        if i > 0:
            keys = layer_weights(i)
            exchange_pending[i] = start_exchange(f"l{i}", keys, [dW[n][l] for n, l in keys], ())
            exchange_token = (exchange_pending[i][1][4],)
        else:
            small_grads = [jnp.stack(dsmall[n]) for n in small_names]
            small_shapes = [g.shape for g in small_grads]
            packed_g = _pack(small_grads)
            assert packed_g.shape[0] % (N_DEV * SUBLANES) == 0, packed_g.shape
            late = [("lru_w_out", 0), ("lru_w_in", 0), ("small", 0)]
            exchange_late0 = start_exchange("late0", late, [dW["lru_w_out"][0], dW["lru_w_in"][0], packed_g], ())
            x0_after = (exchange_late0[1][4],)
        if i > 0:
            def x0_epi(acc, d, gpre, pp):
                dxv = acc + alpha * d
                return (dxv,) + _ple_grad(dxv, gpre, pp)

            dx, dpp, dgpre, dbg = _mm(f"l{i}_d_x0_ple", dmix_in, W[win], "nt", [F32, BF16, BF16], b_lead=s,
                                      extras=[dz1, saved[i - 1]["gpre"], saved[i - 1]["pp"]], epi=x0_epi,
                                      tm=MM_TM_ROWS, tn=D, n_sums=1)
        else:
            dx = _mm(f"l{i}_d_x0", dmix_in, W[win], "nt", [F32], b_lead=s, extras=[dz1],
                     epi=lambda acc, d: (acc + alpha * d,), after=x0_after)
    grad_x = dx.reshape(x.shape)

    for i in range(1, depth):
        finish_exchange(f"l{i}", exchange_pending[i], x0_after[0])
    stacked = {n: None for n in BIG}
    layer0 = layer_weights(0)

    def adamw(n, l):
        stacked[n] = _adamw_layer(f"adamw_{n}_{l}", partial[(n, l)], A[n], A["m_" + n], A["v_" + n], l, stacked[n])

    for n in BIG:
        for l in reversed(range(A[n].shape[0])):
            if (n, l) not in layer0:
                adamw(n, l)
    behind = [dx] + [stacked[n][0] for n in BIG if stacked[n] is not None]
    finish_exchange("early0", exchange_early0, behind)
    finish_exchange("late0", exchange_late0, behind)
    for n, l in layer0:
        adamw(n, l)
    outs = {n: [o.reshape(A[n].shape) for o in stacked[n]] for n in BIG}
    red = _sum8("sum_small", partial[("small", 0)])
    red_full = _all_gather("gather_small", [red], [0])[0]
    small_g = dict(zip(small_names, _unpack(red_full, small_shapes)))
    for n in SMALL_SHARDED:
        width = A[n].shape[-1]
        small_g[n] = lax.dynamic_slice_in_dim(small_g[n], me * width, width, axis=small_g[n].ndim - 1)
    pk = lambda pre: _pack([A[pre + n] for n in small_names])
    d_p, m_p, v_p = _adamw_packed("adamw_small", pk(""), _pack([small_g[n] for n in small_names]), pk("m_"), pk("v_"))
    shapes = [A[n].shape for n in small_names]
    for n, d_, m_, v_ in zip(small_names, _unpack(d_p, shapes), _unpack(m_p, shapes), _unpack(v_p, shapes)):
        outs[n] = [small_g[n], d_, m_, v_]

    return (loss, grad_x, *[outs[n][0] for n in WEIGHTS], *[outs[n][1] for n in WEIGHTS],
            *[outs[n][2] for n in WEIGHTS], *[outs[n][3] for n in WEIGHTS])
```

```python
import functools
import math

import jax
import jax.numpy as jnp
from jax import lax
from jax.experimental import pallas as pl
from jax.experimental.pallas import tpu as pltpu

F32 = jnp.float32
BF16 = jnp.bfloat16
MESH = pl.DeviceIdType.MESH
N_DEV = 8
LANES = 128
SUBLANES = 8

LN_EPS = 1e-5
LRU_C = 8.0
CONV_WIDTH = 4
POOL_HALO = 16
ADAM_LR = 0.001
ADAM_B1 = 0.9
ADAM_B2 = 0.999
ADAM_EPS = 1e-08
ADAM_WD = 0.01
ADAM_STEP = 10

VMEM_LIMIT = 48 * 1024 * 1024
SEQ_CHUNK = 256
MM_TK = 4096
MM_TK_TOKENS = 4096
MM_TM_ROWS = 512
MM_TN = 512
MM_TN_WIDE = 1024
MM_TN_WIDE_MAX_K = 2048
GELU_C0 = math.sqrt(2.0 / math.pi)
GELU_C1 = 0.044715


def _cparams(*sem):
    return pltpu.CompilerParams(dimension_semantics=tuple(sem) if sem else None, vmem_limit_bytes=VMEM_LIMIT)


def _tile(n, pref):
    if n <= pref:
        return n
    t = pref - pref % LANES
    while t >= LANES:
        if n % t == 0:
            return t
        t -= LANES
    return n


def _row_tile(n, pref):
    if n <= pref:
        return n
    t = pref - pref % SUBLANES
    while t >= SUBLANES:
        if n % t == 0:
            return t
        t -= SUBLANES
    return n


def _mm(name, a, b, mode, out_dtypes, epi=None, extras=(), a_lead=None, b_lead=None, tm=1024, tn=None, tk=None,
        after=(), n_sums=0):
    if isinstance(b, (list, tuple)):
        b, b_lead = b[b_lead], None
    a2 = a.shape[-2:]
    b2 = b.shape[-2:]
    if mode == "nn":
        (M, K), N = a2, b2[1]
        assert b2[0] == K
    elif mode == "nt":
        (M, K), N = a2, b2[0]
        assert b2[1] == K
    else:
        (K, M), N = a2, b2[1]
        assert b2[0] == K
    if tk is None:
        tk = MM_TK_TOKENS if mode == "tn" else MM_TK
    if tn is None:
        tn = MM_TN_WIDE if (mode != "tn" and K <= MM_TN_WIDE_MAX_K) else MM_TN
    tm, tn, tk = _tile(M, tm), _tile(N, tn), _tile(K, tk)
    nk = K // tk
    n_extra = len(extras)
    n_out = len(out_dtypes)
    assert n_sums == 0 or tn == N
    resident = {"pipeline_mode": pl.Buffered(1)} if (tn == N and nk == 1) else {}

    def lead(shape, idx, which, **kw):
        if which is None:
            return pl.BlockSpec(shape, idx, **kw)
        return pl.BlockSpec((None,) + shape, lambda i, j, k: (which,) + idx(i, j, k), **kw)

    if mode == "nn":
        a_spec = lead((tm, tk), lambda i, j, k: (i, k), a_lead)
        b_spec = lead((tk, tn), lambda i, j, k: (k, j), b_lead, **resident)
        dims = (((1,), (0,)), ((), ()))
    elif mode == "nt":
        a_spec = lead((tm, tk), lambda i, j, k: (i, k), a_lead)
        b_spec = lead((tn, tk), lambda i, j, k: (j, k), b_lead, **resident)
        dims = (((1,), (1,)), ((), ()))
    else:
        a_spec = lead((tk, tm), lambda i, j, k: (k, i), a_lead)
        b_spec = lead((tk, tn), lambda i, j, k: (k, j), b_lead, **resident)
        dims = (((0,), (0,)), ((), ()))
    e_specs = []
    for e in extras:
        if e.shape[0] == 1:
            e_specs.append(pl.BlockSpec((1, tn), lambda i, j, k: (0, j)))
        else:
            e_specs.append(pl.BlockSpec((tm, tn), lambda i, j, k: (i, j)))

    n_after = len(after)

    def body(a_ref, b_ref, *rest):
        e_refs = rest[:n_extra]
        rest = rest[:n_extra] + rest[n_extra + n_after:]
        o_refs = rest[n_extra:n_extra + n_out]
        s_refs = rest[n_extra + n_out:n_extra + n_out + n_sums]
        part = lax.dot_general(a_ref[...].astype(BF16), b_ref[...].astype(BF16), dims, preferred_element_type=F32)

        def finish(r):
            res = (r,) if epi is None else epi(r, *[e[...] for e in e_refs])
            for o, v in zip(o_refs, res[:n_out]):
                o[...] = v.astype(o.dtype)
            first = pl.program_id(0) == 0
            for sr, v in zip(s_refs, res[n_out:]):
                @pl.when(first)
                def _(sr=sr, v=v):
                    sr[...] = v

                @pl.when(jnp.logical_not(first))
                def _(sr=sr, v=v):
                    sr[...] += v

        if nk == 1:
            finish(part)
            return
        acc = rest[n_extra + n_out + n_sums]
        k = pl.program_id(2)

        @pl.when(k == 0)
        def _():
            acc[...] = part

        @pl.when(jnp.logical_and(k > 0, k < nk - 1))
        def _():
            acc[...] += part

        @pl.when(k == nk - 1)
        def _():
            finish(acc[...] + part)

    outs = pl.pallas_call(
        body,
        name=name,
        grid=(M // tm, N // tn, nk),
        in_specs=[a_spec, b_spec] + e_specs + [pl.BlockSpec(memory_space=pl.ANY)] * n_after,
        out_specs=[pl.BlockSpec((tm, tn), lambda i, j, k: (i, j)) for _ in out_dtypes]
        + [pl.BlockSpec((1, tn), lambda i, j, k: (0, 0))] * n_sums,
        out_shape=[jax.ShapeDtypeStruct((M, N), d) for d in out_dtypes] + [jax.ShapeDtypeStruct((1, N), F32)] * n_sums,
        scratch_shapes=[pltpu.VMEM((tm, tn), F32)] if nk > 1 else [],
        compiler_params=_cparams(*(("arbitrary",) * 3 if n_sums else ("parallel", "parallel", "arbitrary"))),
    )(a, b, *extras, *after)
    return outs[0] if n_out + n_sums == 1 else tuple(outs)


def _rowwise(name, fn, tiled, params, outs, accs=(), tm=256, after=()):
    S = tiled[0].shape[0]
    tm = _tile(S, tm)
    nt, npar, no = len(tiled), len(params), len(outs)
    n_after = len(after)

    def body(*refs):
        t_refs = refs[:nt]
        p_refs = refs[nt:nt + npar]
        refs = refs[nt + npar + n_after:]
        o_refs = refs[:no]
        a_refs = refs[no:]
        res = fn(*[r[...] for r in t_refs], *[r[...] for r in p_refs])
        for o, v in zip(o_refs, res[:no]):
            o[...] = v.astype(o.dtype)
        first = pl.program_id(0) == 0
        for ar, v in zip(a_refs, res[no:]):
            @pl.when(first)
            def _(ar=ar, v=v):
                ar[...] = v

            @pl.when(jnp.logical_not(first))
            def _(ar=ar, v=v):
                ar[...] += v

    full = lambda p: pl.BlockSpec(p.shape, lambda i, nd=p.ndim: (0,) * nd)
    res = pl.pallas_call(
        body,
        name=name,
        grid=(S // tm,),
        in_specs=[pl.BlockSpec((tm, t.shape[1]), lambda i: (i, 0)) for t in tiled] + [full(p) for p in params]
        + [pl.BlockSpec(memory_space=pl.ANY)] * n_after,
        out_specs=[pl.BlockSpec((tm, c), lambda i: (i, 0)) for c, _ in outs]
        + [pl.BlockSpec(s, lambda i, nd=len(s): (0,) * nd) for s in accs],
        out_shape=[jax.ShapeDtypeStruct((S, c), d) for c, d in outs] + [jax.ShapeDtypeStruct(s, F32) for s in accs],
        compiler_params=_cparams("arbitrary"),
    )(*tiled, *params, *after)
    return res


def _ln_stats(z):
    mu = jnp.mean(z, axis=-1, keepdims=True)
    zc = z - mu
    var = jnp.mean(zc * zc, axis=-1, keepdims=True)
    return zc, lax.rsqrt(var + LN_EPS)


def _ln_apply(z, g, b):
    zc, rstd = _ln_stats(z)
    y = zc * rstd * g + b
    return z, y, y


def _ln_grad(dy, z, g):
    zc, rstd = _ln_stats(z)
    xhat = zc * rstd
    dxh = dy * g
    m1 = jnp.mean(dxh, axis=-1, keepdims=True)
    m2 = jnp.mean(dxh * xhat, axis=-1, keepdims=True)
    dz = rstd * (dxh - m1 - xhat * m2)
    return dz, dz, jnp.sum(dy * xhat, axis=0, keepdims=True), jnp.sum(dy, axis=0, keepdims=True)


def _ple_grad(dx3, gpre, pp):
    gate = jax.nn.sigmoid(gpre)
    dgpre = dx3 * pp * gate * (1.0 - gate)
    return dx3 * gate, dgpre, jnp.sum(dgpre, axis=0, keepdims=True)


def _loss_and_grad(name, y, target, gpre, pp):
    d = y.shape[1]

    def fn(y, t, gpre, pp):
        err = y - t
        sq = jnp.sum(jnp.sum(err * err, axis=0, keepdims=True), axis=1, keepdims=True)
        dy = err * (1.0 / d)
        dpp, dgpre, dbg = _ple_grad(dy, gpre, pp)
        return dy, dpp, dgpre, jnp.broadcast_to(sq, (1, LANES)), dbg

    return _rowwise(name, fn, [y, target, gpre, pp], [], [(d, F32), (d, BF16), (d, BF16)], accs=[(1, LANES), (1, d)])


def _rows(shape):
    return lax.broadcasted_iota(jnp.int32, shape, 0)


def _gelu(y):
    t = jnp.tanh(GELU_C0 * (y + GELU_C1 * y * y * y))
    return 0.5 * y * (1.0 + t), t


def _gelu_grad(y, t):
    return 0.5 * (1.0 + t) + 0.5 * y * (1.0 - t * t) * GELU_C0 * (1.0 + 3.0 * GELU_C1 * y * y)


def _neg_expm1(x):
    series = -x * (1.0 + x * (0.5 + x * (1.0 / 6.0 + x * (1.0 / 24.0))))
    return jnp.where(x > -0.02, series, 1.0 - jnp.exp(x))


def _softplus(x):
    return jnp.maximum(x, 0.0) + jnp.log(1.0 + jnp.exp(-jnp.abs(x)))


def _conv_fwd(xs, cw, cb):
    n = xs.shape[0]
    u = cw[3:4] * xs
    for k in (1, 2, 3):
        u = u + cw[3 - k:4 - k] * pltpu.roll(xs, k, 0)
    del n
    return u[SUBLANES:] + cb


def _lru_gates(u, wa, wx, ba, bx, sp, grow):
    ub = u.astype(BF16)
    r = jax.nn.sigmoid(jnp.dot(ub, wa, preferred_element_type=F32) + ba)
    ig = jax.nn.sigmoid(jnp.dot(ub, wx, preferred_element_type=F32) + bx)
    log_a = (-LRU_C) * r * sp
    a = jnp.exp(log_a)
    mult = jnp.sqrt(_neg_expm1(2.0 * log_a))
    mult = jnp.where(grow == 0, 1.0, mult)
    return ub, r, ig, a, mult


def _scan8_fwd(a, b):
    row = _rows(a.shape)
    for k in (1, 2, 4):
        m = row >= k
        b = jnp.where(m, a * pltpu.roll(b, k, 0) + b, b)
        a = jnp.where(m, a * pltpu.roll(a, k, 0), a)
    return a, b


def _scan8_bwd(c, d):
    row = _rows(c.shape)
    for k in (1, 2, 4):
        m = row < SUBLANES - k
        d = jnp.where(m, c * pltpu.roll(d, SUBLANES - k, 0) + d, d)
        c = jnp.where(m, c * pltpu.roll(c, SUBLANES - k, 0), c)
    return c, d


def _pad_copy(dst, src, front, back):
    s, c = src.shape
    if front:
        dst[pl.ds(0, front), :] = jnp.zeros((front, c), dst.dtype)
    if back:
        dst[pl.ds(front + s, back), :] = jnp.zeros((back, c), dst.dtype)
    dst[pl.ds(front, s), :] = src[...].astype(dst.dtype)


def _lru_fwd(name, proj, par, wa, wx):
    S = proj.shape[0]
    R = proj.shape[1] // 2
    H = R // LANES
    ch = _tile(S, SEQ_CHUNK)
    nch = S // ch
    H8 = SUBLANES

    def body(up_ref, y_ref, par_ref, wa_ref, wx_ref, gh_ref, h_ref, up_pad):
        _pad_copy(up_pad, up_ref, H8, 0)
        par = par_ref[...]
        cw, cb, ba, bx = par[0:4], par[4:5], par[5:6], par[6:7]
        sp = _softplus(-par[7:8])
        wa_m, wx_m = wa_ref[...], wx_ref[...]

        def chunk(ci, carry):
            r0 = pl.multiple_of(ci * ch, ch)
            xs = up_pad[pl.ds(r0, ch + H8), :]
            u = _conv_fwd(xs, cw, cb)
            grow = _rows(u.shape) + r0
            _, _, ig, a, mult = _lru_gates(u, wa_m, wx_m, ba, bx, sp, grow)
            bt = mult * (ig * u)
            hs = []
            for j in range(ch // H8):
                aa, bb = _scan8_fwd(a[j * H8:(j + 1) * H8], bt[j * H8:(j + 1) * H8])
                hj = bb + aa * carry
                carry = hj[H8 - 1:H8]
                hs.append(hj)
            h = jnp.concatenate(hs, axis=0)
            h_ref[pl.ds(r0, ch), :] = h
            gy, _ = _gelu(y_ref[pl.ds(r0, ch), :])
            gh_ref[pl.ds(r0, ch), :] = (h * gy).astype(gh_ref.dtype)
            return carry

        lax.fori_loop(0, nch, chunk, jnp.zeros((1, LANES), F32))

    col = lambda off: pl.BlockSpec((S, LANES), lambda h: (0, h + off))
    return pl.pallas_call(
        body,
        name=name,
        grid=(H,),
        in_specs=[col(0), col(H), pl.BlockSpec((8, LANES), lambda h: (0, h)),
                  pl.BlockSpec((None, LANES, LANES), lambda h: (h, 0, 0)),
                  pl.BlockSpec((None, LANES, LANES), lambda h: (h, 0, 0))],
        out_specs=[col(0), col(0)],
        out_shape=[jax.ShapeDtypeStruct((S, R), BF16), jax.ShapeDtypeStruct((S, R), F32)],
        scratch_shapes=[pltpu.VMEM((S + H8, LANES), F32)],
        compiler_params=_cparams("parallel"),
    )(proj, proj, par, wa, wx)


def _lru_bwd(name, proj, h, dgh, par, wa, wx):
    S = proj.shape[0]
    R = proj.shape[1] // 2
    H = R // LANES
    ch = _tile(S, SEQ_CHUNK)
    nch = S // ch
    H8 = SUBLANES
    nb = ch // H8

    def body(up_ref, y_ref, h_ref, dgh_ref, par_ref, wa_ref, wx_ref,
             dup_ref, dy_ref, dwa_ref, dwx_ref, dpar_ref, up_pad, h_pad, du_pad, vec_acc):
        _pad_copy(up_pad, up_ref, H8, 0)
        _pad_copy(h_pad, h_ref, H8, 0)
        du_pad[pl.ds(S, H8), :] = jnp.zeros((H8, LANES), F32)
        par = par_ref[...]
        cw, cb, ba, bx, lam = par[0:4], par[4:5], par[5:6], par[6:7], par[7:8]
        sp = _softplus(-lam)
        wa_m, wx_m = wa_ref[...], wx_ref[...]
        dwa_ref[...] = jnp.zeros_like(dwa_ref)
        dwx_ref[...] = jnp.zeros_like(dwx_ref)
        vec_acc[...] = jnp.zeros_like(vec_acc)
        nt_dims = (((1,), (1,)), ((), ()))
        tn_dims = (((0,), (0,)), ((), ()))

        def chunk(it, carry):
            lam_next, a_next = carry
            ci = nch - 1 - it
            r0 = pl.multiple_of(ci * ch, ch)
            xs = up_pad[pl.ds(r0, ch + H8), :]
            u = _conv_fwd(xs, cw, cb)
            row = _rows(u.shape)
            grow = row + r0
            ub, r, ig, a, mult = _lru_gates(u, wa_m, wx_m, ba, bx, sp, grow)
            hs = h_pad[pl.ds(r0, ch + H8), :]
            hcur = hs[H8:]
            hprev = pltpu.roll(hs, 1, 0)[H8:]
            y = y_ref[pl.ds(r0, ch), :]
            dgh = dgh_ref[pl.ds(r0, ch), :]
            gy, t = _gelu(y)
            dy_ref[pl.ds(r0, ch), :] = (dgh * hcur * _gelu_grad(y, t)).astype(dy_ref.dtype)
            dh = dgh * gy
            c = jnp.where(row == ch - 1, a_next, pltpu.roll(a, ch - 1, 0))
            ls = [None] * nb
            for j in range(nb - 1, -1, -1):
                cc, dd = _scan8_bwd(c[j * H8:(j + 1) * H8], dh[j * H8:(j + 1) * H8])
                lj = dd + cc * lam_next
                lam_next = lj[0:1]
                ls[j] = lj
            lmb = jnp.concatenate(ls, axis=0)
            da = lmb * hprev
            gu = ig * u
            dmult = lmb * gu
            dlog_a = da * a + jnp.where(grow == 0, 0.0, dmult * (-(a * a) / mult))
            dr = dlog_a * ((-LRU_C) * sp)
            drp = dr * r * (1.0 - r)
            dip = (lmb * mult * u) * ig * (1.0 - ig)
            drb, dib = drp.astype(BF16), dip.astype(BF16)
            du = (lmb * mult * ig
                  + lax.dot_general(drb, wa_m, nt_dims, preferred_element_type=F32)
                  + lax.dot_general(dib, wx_m, nt_dims, preferred_element_type=F32))
            du_pad[pl.ds(r0, ch), :] = du
            dwa_ref[...] += lax.dot_general(ub, drb, tn_dims, preferred_element_type=F32)
            dwx_ref[...] += lax.dot_general(ub, dib, tn_dims, preferred_element_type=F32)
            ssum = lambda v: jnp.sum(v, axis=0, keepdims=True)
            vec_acc[0:1, :] += ssum(drp)
            vec_acc[1:2, :] += ssum(dip)
            vec_acc[2:3, :] += ssum(dlog_a * ((-LRU_C) * r))
            return lam_next, a[0:1]

        zero = jnp.zeros((1, LANES), F32)
        lax.fori_loop(0, nch, chunk, (zero, zero))

        def conv_chunk(ci, acc):
            r0 = pl.multiple_of(ci * ch, ch)
            ds = du_pad[pl.ds(r0, ch + H8), :]
            xs = up_pad[pl.ds(r0, ch + H8), :]
            n = ch + H8
            du = ds[:ch]
            dup = cw[3:4] * du
            new = [acc[3] + jnp.sum(du * xs[H8:], axis=0, keepdims=True)]
            for k in (1, 2, 3):
                dup = dup + cw[3 - k:4 - k] * pltpu.roll(ds, n - k, 0)[:ch]
                new.append(acc[3 - k] + jnp.sum(du * pltpu.roll(xs, k, 0)[H8:], axis=0, keepdims=True))
            dup_ref[pl.ds(r0, ch), :] = dup.astype(dup_ref.dtype)
            return (new[3], new[2], new[1], new[0], acc[4] + jnp.sum(du, axis=0, keepdims=True))

        acc = lax.fori_loop(0, nch, conv_chunk, (zero,) * 5)
        dlam = vec_acc[2:3, :] * (-jax.nn.sigmoid(-lam))
        dpar_ref[...] = jnp.concatenate(list(acc) + [vec_acc[0:1, :], vec_acc[1:2, :], dlam], axis=0)

    col = lambda off: pl.BlockSpec((S, LANES), lambda h: (0, h + off))
    head = pl.BlockSpec((None, LANES, LANES), lambda h: (h, 0, 0))
    return pl.pallas_call(
        body,
        name=name,
        grid=(H,),
        in_specs=[col(0), col(H), col(0), col(0), pl.BlockSpec((8, LANES), lambda h: (0, h)), head, head],
        out_specs=[col(0), col(0), head, head, pl.BlockSpec((8, LANES), lambda h: (0, h))],
        out_shape=[jax.ShapeDtypeStruct((S, R), BF16), jax.ShapeDtypeStruct((S, R), BF16),
                   jax.ShapeDtypeStruct((H, LANES, LANES), F32), jax.ShapeDtypeStruct((H, LANES, LANES), F32),
                   jax.ShapeDtypeStruct((8, R), F32)],
        scratch_shapes=[pltpu.VMEM((S + H8, LANES), F32), pltpu.VMEM((S + H8, LANES), F32),
                        pltpu.VMEM((S + H8, LANES), F32), pltpu.VMEM((8, LANES), F32)],
        compiler_params=_cparams("parallel"),
    )(proj, proj, h, dgh, par, wa, wx)


def _window_sum(xs, g, up):
    n = xs.shape[0]
    s = xs
    for lvl, k in enumerate((1, 2, 4, 8)):
        sh = pltpu.roll(s, (n - k) if up else k, 0)
        s = s + jnp.where(g >= lvl, sh, 0.0)
    return s


def _pool_count(grow, g):
    return jnp.minimum(grow + 1, lax.shift_left(jnp.int32(2), g)).astype(F32)


def _pool_fwd(name, u, wgrp, par):
    S, D = u.shape
    G, W = wgrp.shape[0], wgrp.shape[1]
    ch = _tile(S, SEQ_CHUNK)
    nch = S // ch
    PH = POOL_HALO

    def body(u_ref, w_ref, par_ref, zs_ref, u_pad):
        g = pl.program_id(0)
        _pad_copy(u_pad, u_ref, PH, 0)
        par = par_ref[...]
        w = w_ref[...]

        def chunk(ci, _):
            r0 = pl.multiple_of(ci * ch, ch)
            xs = u_pad[pl.ds(r0, ch + PH), :]
            ws = _window_sum(xs, g, False)[PH:]
            uc = xs[PH:]
            cnt = _pool_count(_rows(uc.shape) + r0, g)
            pooled = ws / cnt - uc
            z = jnp.dot(pooled.astype(BF16), w, preferred_element_type=F32) + par[0:1]
            zs_ref[pl.ds(r0, ch), :] = (z * par[1:2]).astype(zs_ref.dtype)
            return 0

        lax.fori_loop(0, nch, chunk, 0)

    return pl.pallas_call(
        body,
        name=name,
        grid=(G,),
        in_specs=[pl.BlockSpec((S, W), lambda g: (0, g)), pl.BlockSpec((None, W, W), lambda g: (g, 0, 0)),
                  pl.BlockSpec((2, W), lambda g: (0, g))],
        out_specs=pl.BlockSpec((S, W), lambda g: (0, g)),
        out_shape=jax.ShapeDtypeStruct((S, D), BF16),
        scratch_shapes=[pltpu.VMEM((S + PH, W), F32)],
        compiler_params=_cparams("parallel"),
    )(u, wgrp, par)


def _pool_bwd(name, u, dzs, wgrp, par):
    S, D = u.shape
    G, W = wgrp.shape[0], wgrp.shape[1]
    ch = _tile(S, SEQ_CHUNK)
    nch = S // ch
    PH = POOL_HALO

    def body(u_ref, dzs_ref, w_ref, par_ref, du_ref, dw_ref, dpar_ref, u_pad, q_pad, dw_acc):
        g = pl.program_id(0)
        _pad_copy(u_pad, u_ref, PH, 0)
        q_pad[pl.ds(S, PH), :] = jnp.zeros((PH, W), F32)
        par = par_ref[...]
        w = w_ref[...]
        dw_acc[...] = jnp.zeros_like(dw_acc)

        def chunk(ci, acc):
            db, dsc = acc
            r0 = pl.multiple_of(ci * ch, ch)
            xs = u_pad[pl.ds(r0, ch + PH), :]
            ws = _window_sum(xs, g, False)[PH:]
            uc = xs[PH:]
            cnt = _pool_count(_rows(uc.shape) + r0, g)
            pooled = (ws / cnt - uc).astype(BF16)
            z = jnp.dot(pooled, w, preferred_element_type=F32) + par[0:1]
            dzs = dzs_ref[pl.ds(r0, ch), :]
            dz = dzs * par[1:2]
            dzb = dz.astype(BF16)
            dw_acc[...] += lax.dot_general(pooled, dzb, (((0,), (0,)), ((), ())), preferred_element_type=F32)
            dpooled = lax.dot_general(dzb, w, (((1,), (1,)), ((), ())), preferred_element_type=F32)
            q_pad[pl.ds(r0, ch), :] = dpooled / cnt
            return (db + jnp.sum(dz, axis=0, keepdims=True), dsc + jnp.sum(dzs * z, axis=0, keepdims=True))

        zero = jnp.zeros((1, W), F32)
        db, dsc = lax.fori_loop(0, nch, chunk, (zero, zero))
        dpar_ref[...] = jnp.concatenate([db, dsc], axis=0)
        dw_ref[...] = dw_acc[...].astype(dw_ref.dtype)

        def back(ci, _):
            r0 = pl.multiple_of(ci * ch, ch)
            qs = q_pad[pl.ds(r0, ch + PH), :]
            ws = _window_sum(qs, g, True)[:ch]
            qc = qs[:ch]
            cnt = _pool_count(_rows(qc.shape) + r0, g)
            du_ref[pl.ds(r0, ch), :] = (ws - qc * cnt).astype(du_ref.dtype)
            return 0

        lax.fori_loop(0, nch, back, 0)

    blk = pl.BlockSpec((S, W), lambda g: (0, g))
    wspec = pl.BlockSpec((None, W, W), lambda g: (g, 0, 0))
    pspec = pl.BlockSpec((2, W), lambda g: (0, g))
    return pl.pallas_call(
        body,
        name=name,
        grid=(G,),
        in_specs=[blk, blk, wspec, pspec],
        out_specs=[blk, wspec, pspec],
        out_shape=[jax.ShapeDtypeStruct((S, D), BF16), jax.ShapeDtypeStruct((G, W, W), BF16),
                   jax.ShapeDtypeStruct((2, D), F32)],
        scratch_shapes=[pltpu.VMEM((S + PH, W), F32), pltpu.VMEM((S + PH, W), F32), pltpu.VMEM((W, W), F32)],
        compiler_params=_cparams("parallel"),
    )(u, dzs, wgrp, par)


def _my_place():
    x, y, c = lax.axis_index("x"), lax.axis_index("y"), lax.axis_index("c")
    return x, y, c, 4 * x + 2 * y + c


def _peers(x, y, c):
    out = []
    for d in range(1, N_DEV):
        px = 1 - x if d & 4 else x
        py = 1 - y if d & 2 else y
        pc = 1 - c if d & 1 else c
        out.append(((px, py, pc), 4 * px + 2 * py + pc))
    return out


def _window(ref, axis, start, size):
    idx = [slice(None)] * len(ref.shape)
    idx[axis] = pl.ds(start, size)
    return ref.at[tuple(idx)]


def _to_bf16(name, arrs, after=()):
    outs = []
    for i, a in enumerate(arrs):
        a2 = a.reshape(-1, a.shape[-1])
        tr = _tile(a2.shape[0], 512)
        o = pl.pallas_call(
            lambda a_ref, *rest: rest[-1].__setitem__(Ellipsis, a_ref[...].astype(BF16)),
            name=f"{name}_{i}",
            grid=(a2.shape[0] // tr,),
            in_specs=[pl.BlockSpec((tr, a2.shape[1]), lambda r: (r, 0))] + [pl.BlockSpec(memory_space=pl.ANY)] * len(after),
            out_specs=pl.BlockSpec((tr, a2.shape[1]), lambda r: (r, 0)),
            out_shape=jax.ShapeDtypeStruct(a2.shape, BF16),
            compiler_params=_cparams("parallel"),
        )(a2, *after)
        outs.append(o.reshape(a.shape))
    return outs


def _all_gather(name, shards, axes):
    n = len(shards)
    sizes = [s.shape[ax] for s, ax in zip(shards, axes)]

    def body(*refs):
        ins, outs = refs[:n], refs[n:2 * n]
        send, recv, loc = refs[2 * n:]
        x, y, c, me = _my_place()
        peers = _peers(x, y, c)
        local = []
        for i in range(n):
            dst = _window(outs[i], axes[i], me * sizes[i], sizes[i])
            cp = pltpu.make_async_copy(ins[i], dst, loc.at[i])
            cp.start()
            local.append(cp)
            for peer, _ in peers:
                pltpu.make_async_remote_copy(src_ref=ins[i], dst_ref=dst, send_sem=send.at[i], recv_sem=recv.at[i],
                                             device_id=peer, device_id_type=MESH).start()
        for i in range(n):
            local[i].wait()
            seven = _window(outs[i], axes[i], 0, (N_DEV - 1) * sizes[i])
            pltpu.make_async_remote_copy(src_ref=seven, dst_ref=seven, send_sem=send.at[i], recv_sem=recv.at[i],
                                         device_id=(x, y, c), device_id_type=MESH).wait()

    def full_shape(s, ax):
        shp = list(s.shape)
        shp[ax] *= N_DEV
        return jax.ShapeDtypeStruct(tuple(shp), s.dtype)

    any_spec = pl.BlockSpec(memory_space=pl.ANY)
    return pl.pallas_call(
        body,
        name=name,
        in_specs=[any_spec] * n,
        out_specs=[any_spec] * n,
        out_shape=[full_shape(s, ax) for s, ax in zip(shards, axes)],
        scratch_shapes=[pltpu.SemaphoreType.DMA((n,)), pltpu.SemaphoreType.DMA((n,)), pltpu.SemaphoreType.DMA((n,))],
        compiler_params=pltpu.CompilerParams(has_side_effects=True),
    )(*shards)


HBM_SPEC = pl.BlockSpec(memory_space=pltpu.HBM)
SEM_SPEC = pl.BlockSpec(memory_space=pltpu.SEMAPHORE)
SPLIT_EFFECT = pltpu.SideEffectType.DATAFLOW_SIDE_EFFECTING


def _push_all(kind, src, dst, axis, size, send_sem, recv_sem, place):
    x, y, c, me = place
    for peer, pidx in _peers(x, y, c):
        if kind == "gather":
            s = d = _window(dst, axis, me * size, size)
        else:
            s, d = _window(src, axis, pidx * size, size), dst.at[me]
        pltpu.make_async_remote_copy(src_ref=s, dst_ref=d, send_sem=send_sem, recv_sem=recv_sem, device_id=peer,
                                     device_id_type=MESH).start()


def _drain_all(kind, dst, axis, size, send_sem, recv_sem, place):
    x, y, c, _ = place
    seven = _window(dst, axis, 0, (N_DEV - 1) * size) if kind == "gather" else dst.at[pl.ds(0, N_DEV - 1)]
    pltpu.make_async_remote_copy(src_ref=seven, dst_ref=seven, send_sem=send_sem, recv_sem=recv_sem,
                                 device_id=(x, y, c), device_id_type=MESH).wait()


def _own_block_placed(src, axis, size, me):
    own = lax.dynamic_slice_in_dim(src, me * size, size, axis)
    return lax.dynamic_update_slice_in_dim(lax.empty((N_DEV,) + own.shape, src.dtype), own[None], me, 0)


def _split_start(name, kind, srcs, lands, axes, sizes, after=()):
    n, ns, na = len(lands), len(srcs), len(after)

    def body(*refs):
        src_refs, land_refs = refs[:ns], refs[ns:ns + n]
        send, recv = refs[ns + n + na], refs[ns + n + na + 1]
        token = refs[-1]
        place = _my_place()
        for k in range(n):
            _push_all(kind, src_refs[k] if ns else None, land_refs[k], axes[k], sizes[k], send.at[k], recv.at[k], place)
        token[...] = jnp.zeros_like(token)

    hbm = lambda a: pltpu.HBM(a.shape, a.dtype)
    res = pl.pallas_call(
        body,
        name=name,
        out_shape=(pltpu.SemaphoreType.DMA((n,)), pltpu.SemaphoreType.DMA((n,)), *[hbm(a) for a in srcs],
                   *[hbm(a) for a in lands], jax.ShapeDtypeStruct((SUBLANES, LANES), F32)),
        in_specs=[HBM_SPEC] * (ns + n) + [pl.BlockSpec(memory_space=pl.ANY)] * na,
        out_specs=(SEM_SPEC, SEM_SPEC, *[HBM_SPEC] * (ns + n), pl.BlockSpec(memory_space=pltpu.VMEM)),
        input_output_aliases={k: 2 + k for k in range(ns + n)},
        compiler_params=pltpu.CompilerParams(has_side_effects=SPLIT_EFFECT),
    )(*[pltpu.with_memory_space_constraint(a, pltpu.HBM) for a in (*srcs, *lands)], *after)
    return res[0], res[1], list(res[2:2 + ns]), list(res[2 + ns:2 + ns + n]), res[-1]


def _split_wait(name, kind, handle, axes, sizes, after):
    send, recv, srcs, lands, _ = handle
    n, ns = len(lands), len(srcs)
    after = list(after) if isinstance(after, (list, tuple)) else [after]

    def body(*refs):
        land_refs = refs[ns:ns + n]
        send_ref, recv_ref = refs[ns + n], refs[ns + n + 1]
        place = _my_place()
        for k in range(n):
            _drain_all(kind, land_refs[k], axes[k], sizes[k], send_ref.at[k], recv_ref.at[k], place)

    hbm = lambda a: pltpu.HBM(a.shape, a.dtype)
    res = pl.pallas_call(
        body,
        name=name,
        out_shape=tuple(hbm(a) for a in (*srcs, *lands)),
        in_specs=[HBM_SPEC] * (ns + n) + [SEM_SPEC, SEM_SPEC] + [pl.BlockSpec(memory_space=pl.ANY)] * len(after),
        out_specs=tuple([HBM_SPEC] * (ns + n)),
        input_output_aliases={k: k for k in range(ns + n)},
        compiler_params=pltpu.CompilerParams(has_side_effects=SPLIT_EFFECT),
    )(*srcs, *lands, send, recv, *after)
    return list(res[ns:])


def _cast_into_window(name, a, l, axis, me1, after=()):
    shp = a.shape[1:]
    cast = lambda me_ref, a_ref, *rest: rest[-1].__setitem__(Ellipsis, a_ref[...].astype(BF16))
    if len(shp) == 3:
        assert axis == 1
        G, r, c = shp
        full = (G, r * N_DEV, c)
        grid = (G,)
        in_spec = pl.BlockSpec((None, None, r, c), lambda g, me: (l, g, 0, 0))
        out_spec = pl.BlockSpec((None, r, c), lambda g, me: (g, me[0], 0))
    else:
        r, c = shp
        tr = _tile(r, 512)
        nb = r // tr
        grid = (nb,)
        in_spec = pl.BlockSpec((None, tr, c), lambda i, me: (l, i, 0))
        if axis == 0:
            full = (r * N_DEV, c)
            out_spec = pl.BlockSpec((tr, c), lambda i, me: (me[0] * nb + i, 0))
        else:
            full = (r, c * N_DEV)
            out_spec = pl.BlockSpec((tr, c), lambda i, me: (i, me[0]))
    return pl.pallas_call(
        cast,
        name=name,
        grid_spec=pltpu.PrefetchScalarGridSpec(
            num_scalar_prefetch=1, grid=grid,
            in_specs=[in_spec] + [pl.BlockSpec(memory_space=pl.ANY)] * len(after), out_specs=out_spec),
        out_shape=jax.ShapeDtypeStruct(full, BF16),
        compiler_params=_cparams("arbitrary"),
    )(me1, a, *after)


def _adamw_math(w, g, m, v):
    m = ADAM_B1 * m + (1.0 - ADAM_B1) * g
    v = ADAM_B2 * v + (1.0 - ADAM_B2) * jnp.square(g)
    m_hat = m / (1.0 - ADAM_B1 ** ADAM_STEP)
    v_hat = v / (1.0 - ADAM_B2 ** ADAM_STEP)
    delta = -ADAM_LR * (m_hat / (jnp.sqrt(v_hat) + ADAM_EPS) + ADAM_WD * w)
    return delta, m, v


def _sum_slots(buf_ref):
    g = buf_ref[0].astype(F32)
    for s in range(1, N_DEV):
        g = g + buf_ref[s].astype(F32)
    return g


def _adamw_layer(name, buf, w, m, v, l, prev):
    shape = w.shape
    L, C = shape[0], shape[-1]
    Rr = math.prod(shape[1:-1])
    buf3 = buf.reshape(N_DEV, Rr, C)
    w3, m3, v3 = (t.reshape(L, Rr, C) for t in (w, m, v))
    tr = _tile(Rr, 2 * LANES) if Rr % LANES == 0 else Rr
    n_prev = 0 if prev is None else 4

    def body(buf_ref, w_ref, m_ref, v_ref, *rest):
        g_out, d_out, m_out, v_out = rest[n_prev:]
        g = _sum_slots(buf_ref)
        d, mm, vv = _adamw_math(w_ref[...], g, m_ref[...], v_ref[...])
        g_out[...] = g
        d_out[...] = d
        m_out[...] = mm
        v_out[...] = vv

    spec = pl.BlockSpec((None, tr, C), lambda r: (l, r, 0))
    outs = pl.pallas_call(
        body,
        name=name,
        grid=(Rr // tr,),
        in_specs=[pl.BlockSpec((N_DEV, tr, C), lambda r: (0, r, 0)), spec, spec, spec]
        + [pl.BlockSpec(memory_space=pl.ANY)] * n_prev,
        out_specs=[spec] * 4,
        out_shape=[jax.ShapeDtypeStruct((L, Rr, C), F32)] * 4,
        input_output_aliases={4 + k: k for k in range(n_prev)},
        compiler_params=_cparams("parallel"),
    )(buf3, w3, m3, v3, *(prev or ()))
    return list(outs)


def _sum8(name, buf):
    R = buf.shape[1]

    def body(buf_ref, o_ref):
        o_ref[...] = _sum_slots(buf_ref)

    return pl.pallas_call(
        body,
        name=name,
        in_specs=[pl.BlockSpec(buf.shape, lambda: (0, 0, 0))],
        out_specs=pl.BlockSpec((R, LANES), lambda: (0, 0)),
        out_shape=jax.ShapeDtypeStruct((R, LANES), F32),
        compiler_params=_cparams(),
    )(buf)


def _adamw_small(name, w, g, m, v):
    shape = w.shape
    w2, g2, m2, v2 = (t.reshape(-1, shape[-1]) for t in (w, g, m, v))
    R, C = w2.shape
    tr = _row_tile(R, 512)

    def body(w_ref, g_ref, m_ref, v_ref, d_out, m_out, v_out):
        d, mm, vv = _adamw_math(w_ref[...], g_ref[...], m_ref[...], v_ref[...])
        d_out[...] = d
        m_out[...] = mm
        v_out[...] = vv

    spec = pl.BlockSpec((tr, C), lambda r: (r, 0))
    outs = pl.pallas_call(
        body,
        name=name,
        grid=(R // tr,),
        in_specs=[spec] * 4,
        out_specs=[spec] * 3,
        out_shape=[jax.ShapeDtypeStruct((R, C), F32)] * 3,
        compiler_params=_cparams("parallel"),
    )(w2, g2, m2, v2)
    return [o.reshape(shape) for o in outs]


def _pack(arrs, pad_rows_to=SUBLANES):
    parts = []
    for a in arrs:
        flat = a.reshape(-1)
        per = LANES * pad_rows_to
        padded = -(-flat.shape[0] // per) * per
        if padded != flat.shape[0]:
            flat = jnp.pad(flat, (0, padded - flat.shape[0]))
        parts.append(flat.reshape(-1, LANES))
    return jnp.concatenate(parts, axis=0)


def _unpack(packed, shapes, pad_rows_to=SUBLANES):
    out = []
    r = 0
    for shp in shapes:
        nel = math.prod(shp)
        per = LANES * pad_rows_to
        rows = -(-nel // per) * pad_rows_to
        out.append(packed[r:r + rows].reshape(-1)[:nel].reshape(shp))
        r += rows
    return out


BIG = ("lru_w_in", "lru_w_out", "pool_w_in", "pool_w_grp", "pool_w_out", "mlp_w1", "mlp_w2", "ple_w", "ple_gate_w")
BIG_AXIS = {"lru_w_in": 2, "lru_w_out": 1, "pool_w_in": 1, "pool_w_grp": 2, "pool_w_out": 1, "mlp_w1": 2,
            "mlp_w2": 1, "ple_w": 2, "ple_gate_w": 1}
SMALL_SHARDED = ("lru_conv_w", "pool_b_grp", "pool_scale")
REPLICATED = ("lru_conv_b", "lru_wa", "lru_ba", "lru_wx", "lru_bx", "lru_lambda", "ln_mix_g", "ln_mix_b",
              "ln_mlp_g", "ln_mlp_b", "ple_gate_b")
WEIGHTS = ("lru_w_in", "lru_conv_w", "lru_conv_b", "lru_wa", "lru_ba", "lru_wx", "lru_bx", "lru_lambda", "lru_w_out",
           "pool_w_in", "pool_w_grp", "pool_b_grp", "pool_scale", "pool_w_out", "ln_mix_g", "ln_mix_b", "mlp_w1",
           "mlp_w2", "ln_mlp_g", "ln_mlp_b", "ple_w", "ple_gate_w", "ple_gate_b")
INPUTS = ("x", "p") + WEIGHTS + ("loss_target",) + tuple("m_" + n for n in WEIGHTS) + tuple("v_" + n for n in WEIGHTS)


def _gather_last_axis(packed_full, shard_shape):
    nel = math.prod(shard_shape)
    blocks = packed_full.reshape(N_DEV, -1)[:, :nel].reshape((N_DEV,) + tuple(shard_shape))
    return jnp.concatenate([blocks[d] for d in range(N_DEV)], axis=-1)


def kernel(x, p, lru_w_in, lru_conv_w, lru_conv_b, lru_wa, lru_ba, lru_wx, lru_bx, lru_lambda, lru_w_out, pool_w_in, pool_w_grp, pool_b_grp, pool_scale, pool_w_out, ln_mix_g, ln_mix_b, mlp_w1, mlp_w2, ln_mlp_g, ln_mlp_b, ple_w, ple_gate_w, ple_gate_b, loss_target, m_lru_w_in, m_lru_conv_w, m_lru_conv_b, m_lru_wa, m_lru_ba, m_lru_wx, m_lru_bx, m_lru_lambda, m_lru_w_out, m_pool_w_in, m_pool_w_grp, m_pool_b_grp, m_pool_scale, m_pool_w_out, m_ln_mix_g, m_ln_mix_b, m_mlp_w1, m_mlp_w2, m_ln_mlp_g, m_ln_mlp_b, m_ple_w, m_ple_gate_w, m_ple_gate_b, v_lru_w_in, v_lru_conv_w, v_lru_conv_b, v_lru_wa, v_lru_ba, v_lru_wx, v_lru_bx, v_lru_lambda, v_lru_w_out, v_pool_w_in, v_pool_w_grp, v_pool_b_grp, v_pool_scale, v_pool_w_out, v_ln_mix_g, v_ln_mix_b, v_mlp_w1, v_mlp_w2, v_ln_mlp_g, v_ln_mlp_b, v_ple_w, v_ple_gate_w, v_ple_gate_b):
    A = dict(zip(INPUTS, (x, p, lru_w_in, lru_conv_w, lru_conv_b, lru_wa, lru_ba, lru_wx, lru_bx, lru_lambda, lru_w_out, pool_w_in, pool_w_grp, pool_b_grp, pool_scale, pool_w_out, ln_mix_g, ln_mix_b, mlp_w1, mlp_w2, ln_mlp_g, ln_mlp_b, ple_w, ple_gate_w, ple_gate_b, loss_target, m_lru_w_in, m_lru_conv_w, m_lru_conv_b, m_lru_wa, m_lru_ba, m_lru_wx, m_lru_bx, m_lru_lambda, m_lru_w_out, m_pool_w_in, m_pool_w_grp, m_pool_b_grp, m_pool_scale, m_pool_w_out, m_ln_mix_g, m_ln_mix_b, m_mlp_w1, m_mlp_w2, m_ln_mlp_g, m_ln_mlp_b, m_ple_w, m_ple_gate_w, m_ple_gate_b, v_lru_w_in, v_lru_conv_w, v_lru_conv_b, v_lru_wa, v_lru_ba, v_lru_wx, v_lru_bx, v_lru_lambda, v_lru_w_out, v_pool_w_in, v_pool_w_grp, v_pool_b_grp, v_pool_scale, v_pool_w_out, v_ln_mix_g, v_ln_mix_b, v_mlp_w1, v_mlp_w2, v_ln_mlp_g, v_ln_mlp_b, v_ple_w, v_ple_gate_w, v_ple_gate_b)))
    depth = ln_mix_g.shape[0]
    alpha = (2 * depth) ** 0.25
    S, D = x.shape[1], x.shape[2]
    xs = x.reshape(S, D)
    tgt = loss_target.reshape(S, D)
    p3 = p.reshape(depth, S, p.shape[-1])
    me = 4 * lax.axis_index("x") + 2 * lax.axis_index("y") + lax.axis_index("c")

    def layer_weights(i):
        s = i // 2
        mixer = ("lru_w_in", "lru_w_out") if i % 2 == 0 else ("pool_w_in", "pool_w_grp", "pool_w_out")
        return [(n, s) for n in mixer] + [(n, i) for n in ("mlp_w1", "mlp_w2", "ple_w", "ple_gate_w")]

    def axis_of(key):
        return 0 if key[0] == "small" else BIG_AXIS[key[0]] - 1

    def start_gather(tag, keys, after):
        axes = [axis_of(k) for k in keys]
        lands = [land[k] for k in keys]
        sizes = [a.shape[ax] // N_DEV for a, ax in zip(lands, axes)]
        return keys, _split_start(f"gather_{tag}_start", "gather", [], lands, axes, sizes, after=after), axes, sizes

    def finish_gather(tag, pending, after):
        keys, handle, axes, sizes = pending
        for (n, l), full in zip(keys, _split_wait(f"gather_{tag}_wait", "gather", handle, axes, sizes, after)):
            W[n][l] = full

    def start_exchange(tag, keys, arrs, after):
        axes = [axis_of(k) for k in keys]
        sizes = [a.shape[ax] // N_DEV for a, ax in zip(arrs, axes)]
        lands = [_own_block_placed(a, ax, sz, me) for a, ax, sz in zip(arrs, axes, sizes)]
        return keys, _split_start(f"exchange_{tag}_start", "scatter", arrs, lands, axes, sizes, after=after), axes, sizes

    def finish_exchange(tag, pending, after):
        keys, handle, axes, sizes = pending
        partial.update(zip(keys, _split_wait(f"exchange_{tag}_wait", "scatter", handle, axes, sizes, after)))

    me1 = jnp.reshape(me, (1,)).astype(jnp.int32)
    land = {}
    W = {n: [None] * A[n].shape[0] for n in BIG}
    small_shard_shapes = [A[n].shape for n in SMALL_SHARDED]
    gathered = _all_gather("gather_small_params", [_pack([A[n] for n in SMALL_SHARDED])], [0])
    def gather_groups(i):
        keys = layer_weights(i)
        mixer, (w1, w2, pw, pg) = keys[:-4], keys[-4:]
        if i == 0:
            return [("in", mixer[:1]), ("out", mixer[1:]), ("up", [w1]), ("rest", [w2, pw, pg])]
        return [("in", mixer + [w1]), ("rest", [w2, pw, pg])]

    gather_pending = {}

    def send_layer(i, behind):
        for tag, keys in gather_groups(i):
            for k in keys:
                land[k] = _cast_into_window(f"cast_{k[0]}_{k[1]}", A[k[0]], k[1], axis_of(k), me1, after=behind)
            gather_pending[(i, tag)] = start_gather(f"l{i}_{tag}", keys, behind)
            behind = (gather_pending[(i, tag)][1][4],)
        return behind

    layer0_started = send_layer(0, (gathered[0],))
    small_full = gathered[0].reshape(N_DEV, -1, LANES)
    r = 0
    for n, shp in zip(SMALL_SHARDED, small_shard_shapes):
        rows = -(-math.prod(shp) // (LANES * SUBLANES)) * SUBLANES
        W[n] = _gather_last_axis(small_full[:, r:r + rows], shp)
        r += rows
    wa_b, wx_b = _to_bf16("cast_gates", [lru_wa, lru_wx], after=layer0_started)
    n_lru = lru_w_in.shape[0]
    lru_par = [jnp.concatenate([W["lru_conv_w"][s], lru_conv_b[s][None], lru_ba[s][None], lru_bx[s][None],
                                lru_lambda[s][None]], axis=0) for s in range(n_lru)]
    pool_par = [jnp.stack([W["pool_b_grp"][s], W["pool_scale"][s]], axis=0) for s in range(pool_w_in.shape[0])]

    saved = []
    h_in = xs
    h_in_b, p3b = _to_bf16("cast_inputs", [xs, p3], after=layer0_started)
    finish_gather("l0_in", gather_pending[(0, "in")], [h_in_b, wa_b])
    for i in range(depth):
        s = i // 2
        sv = {"x0b": h_in_b}
        if i > 0:
            finish_gather(f"l{i}_in", gather_pending[(i, "in")], h_in)
        ln_out = dict(out_dtypes=[F32, F32, BF16], tm=MM_TM_ROWS, tn=D,
                      epi=lambda acc, xp, g, b: _ln_apply(alpha * xp + acc, g, b))
        if i % 2 == 0:
            sv["proj"] = _mm(f"l{i}_lru_in", h_in_b, W["lru_w_in"], "nn", [F32], b_lead=s)
            if i == 0:
                behind = (layer0_started[0], sv["proj"])
                for later in range(1, depth):
                    behind = send_layer(later, behind)
                all_started = behind[0]
            sv["gh"], sv["h"] = _lru_fwd(f"l{i}_lru_core", sv["proj"], lru_par[s], wa_b[s], wx_b[s])
            if i == 0:
                finish_gather("l0_out", gather_pending[(0, "out")], [sv["gh"], all_started])
            mix_in, mix_w = sv["gh"], W["lru_w_out"]
        else:
            sv["u"] = _mm(f"l{i}_pool_in", h_in_b, W["pool_w_in"], "nn", [F32], b_lead=s)
            sv["zs"] = _pool_fwd(f"l{i}_pool_core", sv["u"], W["pool_w_grp"][s], pool_par[s])
            mix_in, mix_w = sv["zs"], W["pool_w_out"]
        sv["z1"], sv["x1"], sv["x1b"] = _mm(f"l{i}_mix_out_ln", mix_in, mix_w, "nn", b_lead=s,
                                            extras=[h_in, ln_mix_g[i][None], ln_mix_b[i][None]], **ln_out)
        if i == 0:
            finish_gather("l0_up", gather_pending[(0, "up")], sv["x1b"])
        sv["hpre"], sv["hact"] = _mm(f"l{i}_mlp_up", sv["x1b"], W["mlp_w1"], "nn", [BF16, BF16], b_lead=i,
                                     epi=lambda acc: (acc, jnp.square(jnp.maximum(acc, 0.0))))
        finish_gather(f"l{i}_rest", gather_pending[(i, "rest")], sv["hact"])
        sv["z2"], sv["x2"], sv["x2b"] = _mm(f"l{i}_mlp_down_ln", sv["hact"], W["mlp_w2"], "nn", b_lead=i,
                                            extras=[sv["x1"], ln_mlp_g[i][None], ln_mlp_b[i][None]], **ln_out)
        sv["pp"] = _mm(f"l{i}_ple_up", p3b, W["ple_w"], "nn", [F32], a_lead=i, b_lead=i)

        def ple_epi(acc, bg, x2t, ppt):
            gpre = acc + bg
            x3 = x2t + ppt * jax.nn.sigmoid(gpre)
            return x3, x3, gpre

        h_in, h_in_b, sv["gpre"] = _mm(f"l{i}_ple_gate", sv["x2b"], W["ple_gate_w"], "nn", [F32, BF16, F32], b_lead=i,
                                       epi=ple_epi, extras=[ple_gate_b[i][None], sv["x2"], sv["pp"]], tn=MM_TN)
        saved.append(sv)

    dx, dpp, dgpre, sq, dbg = _loss_and_grad("loss", h_in, tgt, saved[-1]["gpre"], saved[-1]["pp"])
    loss = lax.psum(0.5 * sq[0, 0] / D, ("x", "y", "c"))

    dW = {n: [None] * A[n].shape[0] for n in BIG}
    dsmall = {n: [None] * A[n].shape[0] for n in REPLICATED + SMALL_SHARDED}
    small_names = REPLICATED + SMALL_SHARDED
    partial = {}
    exchange_pending = {}
    exchange_token = ()
    for i in reversed(range(depth)):
        s = i // 2
        sv = saved[i]
        dsmall["ple_gate_b"][i] = dbg[0]
        dW["ple_w"][i] = _mm(f"l{i}_d_ple_w", p3b, dpp, "tn", [BF16], a_lead=i, after=exchange_token)
        dW["ple_gate_w"][i] = _mm(f"l{i}_d_ple_gate_w", sv["x2b"], dgpre, "tn", [BF16])
        ln_back = dict(out_dtypes=[F32, BF16], tm=MM_TM_ROWS, tn=D, n_sums=2)
        dz2, dz2b, dg, db = _mm(f"l{i}_d_x2_ln", dgpre, W["ple_gate_w"], "nt", b_lead=i,
                                extras=[dx, sv["z2"], ln_mlp_g[i][None]],
                                epi=lambda acc, d, z, g: _ln_grad(acc + d, z, g), after=exchange_token, **ln_back)
        dsmall["ln_mlp_g"][i], dsmall["ln_mlp_b"][i] = dg[0], db[0]
        dhpre = _mm(f"l{i}_d_hpre", dz2b, W["mlp_w2"], "nt", [BF16], b_lead=i, extras=[sv["hpre"]],
                    epi=lambda acc, hp: (acc * (2.0 * jnp.maximum(hp.astype(F32), 0.0)),))
        dW["mlp_w2"][i] = _mm(f"l{i}_d_mlp_w2", sv["hact"], dz2b, "tn", [BF16])
        dW["mlp_w1"][i] = _mm(f"l{i}_d_mlp_w1", sv["x1b"], dhpre, "tn", [BF16])
        mlp_after = ()
        if i == 0:
            early = [(n, 0) for n in ("ple_w", "ple_gate_w", "mlp_w2", "mlp_w1")]
            exchange_early0 = start_exchange("early0", early, [dW[n][l] for n, l in early], ())
            mlp_after = (exchange_early0[1][4],)
        dz1, dz1b, dg, db = _mm(f"l{i}_d_x1_ln", dhpre, W["mlp_w1"], "nt", b_lead=i,
                                extras=[dz2, sv["z1"], ln_mix_g[i][None]],
                                epi=lambda acc, d, z, g: _ln_grad(acc + alpha * d, z, g), after=mlp_after, **ln_back)
        dsmall["ln_mix_g"][i], dsmall["ln_mix_b"][i] = dg[0], db[0]
        if i % 2 == 0:
            dW["lru_w_out"][s] = _mm(f"l{i}_d_lru_w_out", sv["gh"], dz1b, "tn", [BF16])
            dgh = _mm(f"l{i}_d_gh", dz1b, W["lru_w_out"], "nt", [F32], b_lead=s)
            dup, dy, dwa, dwx, dpar = _lru_bwd(f"l{i}_lru_core_bwd", sv["proj"], sv["h"], dgh, lru_par[s],
                                               wa_b[s], wx_b[s])
            dsmall["lru_wa"][s], dsmall["lru_wx"][s] = dwa, dwx
            dsmall["lru_conv_w"][s] = dpar[0:4]
            for k, n in enumerate(("lru_conv_b", "lru_ba", "lru_bx", "lru_lambda")):
                dsmall[n][s] = dpar[4 + k]
            dmix_in = jnp.concatenate([dup, dy], axis=1)
            win = "lru_w_in"
        else:
            dW["pool_w_out"][s] = _mm(f"l{i}_d_pool_w_out", sv["zs"], dz1b, "tn", [BF16])
            dzs = _mm(f"l{i}_d_zs", dz1b, W["pool_w_out"], "nt", [F32], b_lead=s)
            dmix_in, dW["pool_w_grp"][s], dpar = _pool_bwd(f"l{i}_pool_core_bwd", sv["u"], dzs, W["pool_w_grp"][s],
                                                          pool_par[s])
            dsmall["pool_b_grp"][s], dsmall["pool_scale"][s] = dpar[0], dpar[1]
            win = "pool_w_in"
        dW[win][s] = _mm(f"l{i}_d_{win}", sv["x0b"], dmix_in, "tn", [BF16])
        x0_after = ()
        if i > 0:
            keys = layer_weights(i)
            exchange_pending[i] = start_exchange(f"l{i}", keys, [dW[n][l] for n, l in keys], ())
            exchange_token = (exchange_pending[i][1][4],)
        else:
            small_grads = [jnp.stack(dsmall[n]) for n in small_names]
            small_shapes = [g.shape for g in small_grads]
            packed_g = _pack(small_grads)
            assert packed_g.shape[0] % (N_DEV * SUBLANES) == 0, packed_g.shape
            late = [("lru_w_out", 0), ("lru_w_in", 0), ("small", 0)]
            exchange_late0 = start_exchange("late0", late, [dW["lru_w_out"][0], dW["lru_w_in"][0], packed_g], ())
            x0_after = (exchange_late0[1][4],)
        if i > 0:
            def x0_epi(acc, d, gpre, pp):
                dxv = acc + alpha * d
                return (dxv,) + _ple_grad(dxv, gpre, pp)

            dx, dpp, dgpre, dbg = _mm(f"l{i}_d_x0_ple", dmix_in, W[win], "nt", [F32, BF16, BF16], b_lead=s,
                                      extras=[dz1, saved[i - 1]["gpre"], saved[i - 1]["pp"]], epi=x0_epi,
                                      tm=MM_TM_ROWS, tn=D, n_sums=1)
        else:
            dx = _mm(f"l{i}_d_x0", dmix_in, W[win], "nt", [F32], b_lead=s, extras=[dz1],
                     epi=lambda acc, d: (acc + alpha * d,), after=x0_after)
    grad_x = dx.reshape(x.shape)

    for i in range(1, depth):
        finish_exchange(f"l{i}", exchange_pending[i], x0_after[0])
    stacked = {n: None for n in BIG}
    layer0 = layer_weights(0)

    def adamw(n, l):
        stacked[n] = _adamw_layer(f"adamw_{n}_{l}", partial[(n, l)], A[n], A["m_" + n], A["v_" + n], l, stacked[n])

    for n in BIG:
        for l in reversed(range(A[n].shape[0])):
            if (n, l) not in layer0:
                adamw(n, l)
    behind = [dx] + [stacked[n][0] for n in BIG if stacked[n] is not None]
    finish_exchange("early0", exchange_early0, behind)
    finish_exchange("late0", exchange_late0, behind)
    red = _sum8("sum_small", partial[("small", 0)])
    rows = red.shape[0]
    red_land = lax.dynamic_update_slice_in_dim(lax.empty((N_DEV * rows, LANES), F32), red, me * rows, 0)
    small_handle = _split_start("gather_small_start", "gather", [], [red_land], [0], [rows])
    for n, l in layer0:
        adamw(n, l)
    outs = {n: [o.reshape(A[n].shape) for o in stacked[n]] for n in BIG}
    red_full = _split_wait("gather_small_wait", "gather", small_handle, [0], [rows], [stacked[n][0] for n, _ in layer0])[0]
    small_g = dict(zip(small_names, _unpack(red_full, small_shapes)))
    for n in SMALL_SHARDED:
        width = A[n].shape[-1]
        small_g[n] = lax.dynamic_slice_in_dim(small_g[n], me * width, width, axis=small_g[n].ndim - 1)
    for n in small_names:
        outs[n] = [small_g[n]] + _adamw_small(f"adamw_{n}", A[n], small_g[n], A["m_" + n], A["v_" + n])

    return (loss, grad_x, *[outs[n][0] for n in WEIGHTS], *[outs[n][1] for n in WEIGHTS],
            *[outs[n][2] for n in WEIGHTS], *[outs[n][3] for n in WEIGHTS])
```

```python
import functools
import math

import jax
import jax.numpy as jnp
from jax import lax
from jax.experimental import pallas as pl
from jax.experimental.pallas import tpu as pltpu

F32 = jnp.float32
BF16 = jnp.bfloat16
MESH = pl.DeviceIdType.MESH
N_DEV = 8
LANES = 128
SUBLANES = 8

LN_EPS = 1e-5
LRU_C = 8.0
CONV_WIDTH = 4
POOL_HALO = 16
ADAM_LR = 0.001
ADAM_B1 = 0.9
ADAM_B2 = 0.999
ADAM_EPS = 1e-08
ADAM_WD = 0.01
ADAM_STEP = 10

VMEM_LIMIT = 48 * 1024 * 1024
SEQ_CHUNK = 256
MM_TK = 4096
MM_TK_TOKENS = 4096
MM_TM_ROWS = 512
MM_TN = 512
MM_TN_WIDE = 1024
MM_TN_WIDE_MAX_K = 2048
GELU_C0 = math.sqrt(2.0 / math.pi)
GELU_C1 = 0.044715


def _cparams(*sem):
    return pltpu.CompilerParams(dimension_semantics=tuple(sem) if sem else None, vmem_limit_bytes=VMEM_LIMIT)


def _tile(n, pref):
    if n <= pref:
        return n
    t = pref - pref % LANES
    while t >= LANES:
        if n % t == 0:
            return t
        t -= LANES
    return n


def _row_tile(n, pref):
    if n <= pref:
        return n
    t = pref - pref % SUBLANES
    while t >= SUBLANES:
        if n % t == 0:
            return t
        t -= SUBLANES
    return n


def _mm(name, a, b, mode, out_dtypes, epi=None, extras=(), a_lead=None, b_lead=None, tm=1024, tn=None, tk=None,
        after=(), n_sums=0):
    if isinstance(b, (list, tuple)):
        b, b_lead = b[b_lead], None
    a2 = a.shape[-2:]
    b2 = b.shape[-2:]
    if mode == "nn":
        (M, K), N = a2, b2[1]
        assert b2[0] == K
    elif mode == "nt":
        (M, K), N = a2, b2[0]
        assert b2[1] == K
    else:
        (K, M), N = a2, b2[1]
        assert b2[0] == K
    if tk is None:
        tk = MM_TK_TOKENS if mode == "tn" else MM_TK
    if tn is None:
        tn = MM_TN_WIDE if (mode != "tn" and K <= MM_TN_WIDE_MAX_K) else MM_TN
    tm, tn, tk = _tile(M, tm), _tile(N, tn), _tile(K, tk)
    nk = K // tk
    n_extra = len(extras)
    n_out = len(out_dtypes)
    assert n_sums == 0 or tn == N
    resident = {"pipeline_mode": pl.Buffered(1)} if (tn == N and nk == 1) else {}

    def lead(shape, idx, which, **kw):
        if which is None:
            return pl.BlockSpec(shape, idx, **kw)
        return pl.BlockSpec((None,) + shape, lambda i, j, k: (which,) + idx(i, j, k), **kw)

    if mode == "nn":
        a_spec = lead((tm, tk), lambda i, j, k: (i, k), a_lead)
        b_spec = lead((tk, tn), lambda i, j, k: (k, j), b_lead, **resident)
        dims = (((1,), (0,)), ((), ()))
    elif mode == "nt":
        a_spec = lead((tm, tk), lambda i, j, k: (i, k), a_lead)
        b_spec = lead((tn, tk), lambda i, j, k: (j, k), b_lead, **resident)
        dims = (((1,), (1,)), ((), ()))
    else:
        a_spec = lead((tk, tm), lambda i, j, k: (k, i), a_lead)
        b_spec = lead((tk, tn), lambda i, j, k: (k, j), b_lead, **resident)
        dims = (((0,), (0,)), ((), ()))
    e_specs = []
    for e in extras:
        if e.shape[0] == 1:
            e_specs.append(pl.BlockSpec((1, tn), lambda i, j, k: (0, j)))
        else:
            e_specs.append(pl.BlockSpec((tm, tn), lambda i, j, k: (i, j)))

    n_after = len(after)

    def body(a_ref, b_ref, *rest):
        e_refs = rest[:n_extra]
        rest = rest[:n_extra] + rest[n_extra + n_after:]
        o_refs = rest[n_extra:n_extra + n_out]
        s_refs = rest[n_extra + n_out:n_extra + n_out + n_sums]
        part = lax.dot_general(a_ref[...].astype(BF16), b_ref[...].astype(BF16), dims, preferred_element_type=F32)

        def finish(r):
            res = (r,) if epi is None else epi(r, *[e[...] for e in e_refs])
            for o, v in zip(o_refs, res[:n_out]):
                o[...] = v.astype(o.dtype)
            first = pl.program_id(0) == 0
            for sr, v in zip(s_refs, res[n_out:]):
                @pl.when(first)
                def _(sr=sr, v=v):
                    sr[...] = v

                @pl.when(jnp.logical_not(first))
                def _(sr=sr, v=v):
                    sr[...] += v

        if nk == 1:
            finish(part)
            return
        acc = rest[n_extra + n_out + n_sums]
        k = pl.program_id(2)

        @pl.when(k == 0)
        def _():
            acc[...] = part

        @pl.when(jnp.logical_and(k > 0, k < nk - 1))
        def _():
            acc[...] += part

        @pl.when(k == nk - 1)
        def _():
            finish(acc[...] + part)

    outs = pl.pallas_call(
        body,
        name=name,
        grid=(M // tm, N // tn, nk),
        in_specs=[a_spec, b_spec] + e_specs + [pl.BlockSpec(memory_space=pl.ANY)] * n_after,
        out_specs=[pl.BlockSpec((tm, tn), lambda i, j, k: (i, j)) for _ in out_dtypes]
        + [pl.BlockSpec((1, tn), lambda i, j, k: (0, 0))] * n_sums,
        out_shape=[jax.ShapeDtypeStruct((M, N), d) for d in out_dtypes] + [jax.ShapeDtypeStruct((1, N), F32)] * n_sums,
        scratch_shapes=[pltpu.VMEM((tm, tn), F32)] if nk > 1 else [],
        compiler_params=_cparams(*(("arbitrary",) * 3 if n_sums else ("parallel", "parallel", "arbitrary"))),
    )(a, b, *extras, *after)
    return outs[0] if n_out + n_sums == 1 else tuple(outs)


def _rowwise(name, fn, tiled, params, outs, accs=(), tm=256, after=()):
    S = tiled[0].shape[0]
    tm = _tile(S, tm)
    nt, npar, no = len(tiled), len(params), len(outs)
    n_after = len(after)

    def body(*refs):
        t_refs = refs[:nt]
        p_refs = refs[nt:nt + npar]
        refs = refs[nt + npar + n_after:]
        o_refs = refs[:no]
        a_refs = refs[no:]
        res = fn(*[r[...] for r in t_refs], *[r[...] for r in p_refs])
        for o, v in zip(o_refs, res[:no]):
            o[...] = v.astype(o.dtype)
        first = pl.program_id(0) == 0
        for ar, v in zip(a_refs, res[no:]):
            @pl.when(first)
            def _(ar=ar, v=v):
                ar[...] = v

            @pl.when(jnp.logical_not(first))
            def _(ar=ar, v=v):
                ar[...] += v

    full = lambda p: pl.BlockSpec(p.shape, lambda i, nd=p.ndim: (0,) * nd)
    res = pl.pallas_call(
        body,
        name=name,
        grid=(S // tm,),
        in_specs=[pl.BlockSpec((tm, t.shape[1]), lambda i: (i, 0)) for t in tiled] + [full(p) for p in params]
        + [pl.BlockSpec(memory_space=pl.ANY)] * n_after,
        out_specs=[pl.BlockSpec((tm, c), lambda i: (i, 0)) for c, _ in outs]
        + [pl.BlockSpec(s, lambda i, nd=len(s): (0,) * nd) for s in accs],
        out_shape=[jax.ShapeDtypeStruct((S, c), d) for c, d in outs] + [jax.ShapeDtypeStruct(s, F32) for s in accs],
        compiler_params=_cparams("arbitrary"),
    )(*tiled, *params, *after)
    return res


def _ln_stats(z):
    mu = jnp.mean(z, axis=-1, keepdims=True)
    zc = z - mu
    var = jnp.mean(zc * zc, axis=-1, keepdims=True)
    return zc, lax.rsqrt(var + LN_EPS)


def _ln_apply(z, g, b):
    zc, rstd = _ln_stats(z)
    y = zc * rstd * g + b
    return z, y, y


def _ln_grad(dy, z, g):
    zc, rstd = _ln_stats(z)
    xhat = zc * rstd
    dxh = dy * g
    m1 = jnp.mean(dxh, axis=-1, keepdims=True)
    m2 = jnp.mean(dxh * xhat, axis=-1, keepdims=True)
    dz = rstd * (dxh - m1 - xhat * m2)
    return dz, dz, jnp.sum(dy * xhat, axis=0, keepdims=True), jnp.sum(dy, axis=0, keepdims=True)


def _ple_grad(dx3, gpre, pp):
    gate = jax.nn.sigmoid(gpre)
    dgpre = dx3 * pp * gate * (1.0 - gate)
    return dx3 * gate, dgpre, jnp.sum(dgpre, axis=0, keepdims=True)


def _loss_and_grad(name, y, target, gpre, pp):
    d = y.shape[1]

    def fn(y, t, gpre, pp):
        err = y - t
        sq = jnp.sum(jnp.sum(err * err, axis=0, keepdims=True), axis=1, keepdims=True)
        dy = err * (1.0 / d)
        dpp, dgpre, dbg = _ple_grad(dy, gpre, pp)
        return dy, dpp, dgpre, jnp.broadcast_to(sq, (1, LANES)), dbg

    return _rowwise(name, fn, [y, target, gpre, pp], [], [(d, F32), (d, BF16), (d, BF16)], accs=[(1, LANES), (1, d)])


def _rows(shape):
    return lax.broadcasted_iota(jnp.int32, shape, 0)


def _gelu(y):
    t = jnp.tanh(GELU_C0 * (y + GELU_C1 * y * y * y))
    return 0.5 * y * (1.0 + t), t


def _gelu_grad(y, t):
    return 0.5 * (1.0 + t) + 0.5 * y * (1.0 - t * t) * GELU_C0 * (1.0 + 3.0 * GELU_C1 * y * y)


def _softplus(x):
    return jnp.maximum(x, 0.0) + jnp.log(1.0 + jnp.exp(-jnp.abs(x)))


def _conv_fwd(xs, cw, cb):
    n = xs.shape[0]
    u = cw[3:4] * xs
    for k in (1, 2, 3):
        u = u + cw[3 - k:4 - k] * pltpu.roll(xs, k, 0)
    del n
    return u[SUBLANES:] + cb


def _lru_gates(u, wa, wx, ba, bx, sp, grow):
    ub = u.astype(BF16)
    r = jax.nn.sigmoid(jnp.dot(ub, wa, preferred_element_type=F32) + ba)
    ig = jax.nn.sigmoid(jnp.dot(ub, wx, preferred_element_type=F32) + bx)
    log_a = (-LRU_C) * r * sp
    a = jnp.exp(log_a)
    mult = jnp.sqrt(jnp.tanh(-log_a) * (1.0 + a * a))
    mult = jnp.where(grow == 0, 1.0, mult)
    return ub, r, ig, a, mult


def _scan8_fwd(a, b):
    row = _rows(a.shape)
    for k in (1, 2, 4):
        m = row >= k
        b = jnp.where(m, a * pltpu.roll(b, k, 0) + b, b)
        a = jnp.where(m, a * pltpu.roll(a, k, 0), a)
    return a, b


def _scan8_bwd(c, d):
    row = _rows(c.shape)
    for k in (1, 2, 4):
        m = row < SUBLANES - k
        d = jnp.where(m, c * pltpu.roll(d, SUBLANES - k, 0) + d, d)
        c = jnp.where(m, c * pltpu.roll(c, SUBLANES - k, 0), c)
    return c, d


def _pad_copy(dst, src, front, back):
    s, c = src.shape
    if front:
        dst[pl.ds(0, front), :] = jnp.zeros((front, c), dst.dtype)
    if back:
        dst[pl.ds(front + s, back), :] = jnp.zeros((back, c), dst.dtype)
    dst[pl.ds(front, s), :] = src[...].astype(dst.dtype)


def _lru_fwd(name, proj, par, wa, wx):
    S = proj.shape[0]
    R = proj.shape[1] // 2
    H = R // LANES
    ch = _tile(S, SEQ_CHUNK)
    nch = S // ch
    H8 = SUBLANES

    def body(up_ref, y_ref, par_ref, wa_ref, wx_ref, gh_ref, h_ref, up_pad):
        _pad_copy(up_pad, up_ref, H8, 0)
        par = par_ref[...]
        cw, cb, ba, bx = par[0:4], par[4:5], par[5:6], par[6:7]
        sp = _softplus(-par[7:8])
        wa_m, wx_m = wa_ref[...], wx_ref[...]

        def chunk(ci, carry):
            r0 = pl.multiple_of(ci * ch, ch)
            xs = up_pad[pl.ds(r0, ch + H8), :]
            u = _conv_fwd(xs, cw, cb)
            grow = _rows(u.shape) + r0
            _, _, ig, a, mult = _lru_gates(u, wa_m, wx_m, ba, bx, sp, grow)
            bt = mult * (ig * u)
            hs = []
            for j in range(ch // H8):
                aa, bb = _scan8_fwd(a[j * H8:(j + 1) * H8], bt[j * H8:(j + 1) * H8])
                hj = bb + aa * carry
                carry = hj[H8 - 1:H8]
                hs.append(hj)
            h = jnp.concatenate(hs, axis=0)
            h_ref[pl.ds(r0, ch), :] = h
            gy, _ = _gelu(y_ref[pl.ds(r0, ch), :])
            gh_ref[pl.ds(r0, ch), :] = (h * gy).astype(gh_ref.dtype)
            return carry

        lax.fori_loop(0, nch, chunk, jnp.zeros((1, LANES), F32))

    col = lambda off: pl.BlockSpec((S, LANES), lambda h: (0, h + off))
    return pl.pallas_call(
        body,
        name=name,
        grid=(H,),
        in_specs=[col(0), col(H), pl.BlockSpec((8, LANES), lambda h: (0, h)),
                  pl.BlockSpec((None, LANES, LANES), lambda h: (h, 0, 0)),
                  pl.BlockSpec((None, LANES, LANES), lambda h: (h, 0, 0))],
        out_specs=[col(0), col(0)],
        out_shape=[jax.ShapeDtypeStruct((S, R), BF16), jax.ShapeDtypeStruct((S, R), F32)],
        scratch_shapes=[pltpu.VMEM((S + H8, LANES), F32)],
        compiler_params=_cparams("parallel"),
    )(proj, proj, par, wa, wx)


def _lru_bwd(name, proj, h, dgh, par, wa, wx):
    S = proj.shape[0]
    R = proj.shape[1] // 2
    H = R // LANES
    ch = _tile(S, SEQ_CHUNK)
    nch = S // ch
    H8 = SUBLANES
    nb = ch // H8

    def body(up_ref, y_ref, h_ref, dgh_ref, par_ref, wa_ref, wx_ref,
             dup_ref, dy_ref, dwa_ref, dwx_ref, dpar_ref, up_pad, h_pad, du_pad, vec_acc):
        _pad_copy(up_pad, up_ref, H8, 0)
        _pad_copy(h_pad, h_ref, H8, 0)
        du_pad[pl.ds(S, H8), :] = jnp.zeros((H8, LANES), F32)
        par = par_ref[...]
        cw, cb, ba, bx, lam = par[0:4], par[4:5], par[5:6], par[6:7], par[7:8]
        sp = _softplus(-lam)
        wa_m, wx_m = wa_ref[...], wx_ref[...]
        dwa_ref[...] = jnp.zeros_like(dwa_ref)
        dwx_ref[...] = jnp.zeros_like(dwx_ref)
        vec_acc[...] = jnp.zeros_like(vec_acc)
        nt_dims = (((1,), (1,)), ((), ()))
        tn_dims = (((0,), (0,)), ((), ()))

        def chunk(it, carry):
            lam_next, a_next = carry
            ci = nch - 1 - it
            r0 = pl.multiple_of(ci * ch, ch)
            xs = up_pad[pl.ds(r0, ch + H8), :]
            u = _conv_fwd(xs, cw, cb)
            row = _rows(u.shape)
            grow = row + r0
            ub, r, ig, a, mult = _lru_gates(u, wa_m, wx_m, ba, bx, sp, grow)
            hs = h_pad[pl.ds(r0, ch + H8), :]
            hcur = hs[H8:]
            hprev = pltpu.roll(hs, 1, 0)[H8:]
            y = y_ref[pl.ds(r0, ch), :]
            dgh = dgh_ref[pl.ds(r0, ch), :]
            gy, t = _gelu(y)
            dy_ref[pl.ds(r0, ch), :] = (dgh * hcur * _gelu_grad(y, t)).astype(dy_ref.dtype)
            dh = dgh * gy
            c = jnp.where(row == ch - 1, a_next, pltpu.roll(a, ch - 1, 0))
            ls = [None] * nb
            for j in range(nb - 1, -1, -1):
                cc, dd = _scan8_bwd(c[j * H8:(j + 1) * H8], dh[j * H8:(j + 1) * H8])
                lj = dd + cc * lam_next
                lam_next = lj[0:1]
                ls[j] = lj
            lmb = jnp.concatenate(ls, axis=0)
            da = lmb * hprev
            gu = ig * u
            dmult = lmb * gu
            dlog_a = da * a + jnp.where(grow == 0, 0.0, dmult * (-(a * a) / mult))
            dr = dlog_a * ((-LRU_C) * sp)
            drp = dr * r * (1.0 - r)
            dip = (lmb * mult * u) * ig * (1.0 - ig)
            drb, dib = drp.astype(BF16), dip.astype(BF16)
            du = (lmb * mult * ig
                  + lax.dot_general(drb, wa_m, nt_dims, preferred_element_type=F32)
                  + lax.dot_general(dib, wx_m, nt_dims, preferred_element_type=F32))
            du_pad[pl.ds(r0, ch), :] = du
            dwa_ref[...] += lax.dot_general(ub, drb, tn_dims, preferred_element_type=F32)
            dwx_ref[...] += lax.dot_general(ub, dib, tn_dims, preferred_element_type=F32)
            ssum = lambda v: jnp.sum(v, axis=0, keepdims=True)
            vec_acc[0:1, :] += ssum(drp)
            vec_acc[1:2, :] += ssum(dip)
            vec_acc[2:3, :] += ssum(dlog_a * ((-LRU_C) * r))
            return lam_next, a[0:1]

        zero = jnp.zeros((1, LANES), F32)
        lax.fori_loop(0, nch, chunk, (zero, zero))

        def conv_chunk(ci, acc):
            r0 = pl.multiple_of(ci * ch, ch)
            ds = du_pad[pl.ds(r0, ch + H8), :]
            xs = up_pad[pl.ds(r0, ch + H8), :]
            n = ch + H8
            du = ds[:ch]
            dup = cw[3:4] * du
            new = [acc[3] + jnp.sum(du * xs[H8:], axis=0, keepdims=True)]
            for k in (1, 2, 3):
                dup = dup + cw[3 - k:4 - k] * pltpu.roll(ds, n - k, 0)[:ch]
                new.append(acc[3 - k] + jnp.sum(du * pltpu.roll(xs, k, 0)[H8:], axis=0, keepdims=True))
            dup_ref[pl.ds(r0, ch), :] = dup.astype(dup_ref.dtype)
            return (new[3], new[2], new[1], new[0], acc[4] + jnp.sum(du, axis=0, keepdims=True))

        acc = lax.fori_loop(0, nch, conv_chunk, (zero,) * 5)
        dlam = vec_acc[2:3, :] * (-jax.nn.sigmoid(-lam))
        dpar_ref[...] = jnp.concatenate(list(acc) + [vec_acc[0:1, :], vec_acc[1:2, :], dlam], axis=0)

    col = lambda off: pl.BlockSpec((S, LANES), lambda h: (0, h + off))
    head = pl.BlockSpec((None, LANES, LANES), lambda h: (h, 0, 0))
    return pl.pallas_call(
        body,
        name=name,
        grid=(H,),
        in_specs=[col(0), col(H), col(0), col(0), pl.BlockSpec((8, LANES), lambda h: (0, h)), head, head],
        out_specs=[col(0), col(0), head, head, pl.BlockSpec((8, LANES), lambda h: (0, h))],
        out_shape=[jax.ShapeDtypeStruct((S, R), BF16), jax.ShapeDtypeStruct((S, R), BF16),
                   jax.ShapeDtypeStruct((H, LANES, LANES), F32), jax.ShapeDtypeStruct((H, LANES, LANES), F32),
                   jax.ShapeDtypeStruct((8, R), F32)],
        scratch_shapes=[pltpu.VMEM((S + H8, LANES), F32), pltpu.VMEM((S + H8, LANES), F32),
                        pltpu.VMEM((S + H8, LANES), F32), pltpu.VMEM((8, LANES), F32)],
        compiler_params=_cparams("parallel"),
    )(proj, proj, h, dgh, par, wa, wx)


def _window_sum(xs, g, up):
    n = xs.shape[0]
    s = xs
    for lvl, k in enumerate((1, 2, 4, 8)):
        sh = pltpu.roll(s, (n - k) if up else k, 0)
        s = s + jnp.where(g >= lvl, sh, 0.0)
    return s


def _pool_count(grow, g):
    return jnp.minimum(grow + 1, lax.shift_left(jnp.int32(2), g)).astype(F32)


def _pool_fwd(name, u, wgrp, par):
    S, D = u.shape
    G, W = wgrp.shape[0], wgrp.shape[1]
    ch = _tile(S, SEQ_CHUNK)
    nch = S // ch
    PH = POOL_HALO

    def body(u_ref, w_ref, par_ref, zs_ref, u_pad):
        g = pl.program_id(0)
        _pad_copy(u_pad, u_ref, PH, 0)
        par = par_ref[...]
        w = w_ref[...]

        def chunk(ci, _):
            r0 = pl.multiple_of(ci * ch, ch)
            xs = u_pad[pl.ds(r0, ch + PH), :]
            ws = _window_sum(xs, g, False)[PH:]
            uc = xs[PH:]
            cnt = _pool_count(_rows(uc.shape) + r0, g)
            pooled = ws / cnt - uc
            z = jnp.dot(pooled.astype(BF16), w, preferred_element_type=F32) + par[0:1]
            zs_ref[pl.ds(r0, ch), :] = (z * par[1:2]).astype(zs_ref.dtype)
            return 0

        lax.fori_loop(0, nch, chunk, 0)

    return pl.pallas_call(
        body,
        name=name,
        grid=(G,),
        in_specs=[pl.BlockSpec((S, W), lambda g: (0, g)), pl.BlockSpec((None, W, W), lambda g: (g, 0, 0)),
                  pl.BlockSpec((2, W), lambda g: (0, g))],
        out_specs=pl.BlockSpec((S, W), lambda g: (0, g)),
        out_shape=jax.ShapeDtypeStruct((S, D), BF16),
        scratch_shapes=[pltpu.VMEM((S + PH, W), F32)],
        compiler_params=_cparams("parallel"),
    )(u, wgrp, par)


def _pool_bwd(name, u, dzs, wgrp, par):
    S, D = u.shape
    G, W = wgrp.shape[0], wgrp.shape[1]
    ch = _tile(S, SEQ_CHUNK)
    nch = S // ch
    PH = POOL_HALO

    def body(u_ref, dzs_ref, w_ref, par_ref, du_ref, dw_ref, dpar_ref, u_pad, q_pad, dw_acc):
        g = pl.program_id(0)
        _pad_copy(u_pad, u_ref, PH, 0)
        q_pad[pl.ds(S, PH), :] = jnp.zeros((PH, W), F32)
        par = par_ref[...]
        w = w_ref[...]
        dw_acc[...] = jnp.zeros_like(dw_acc)

        def chunk(ci, acc):
            db, dsc = acc
            r0 = pl.multiple_of(ci * ch, ch)
            xs = u_pad[pl.ds(r0, ch + PH), :]
            ws = _window_sum(xs, g, False)[PH:]
            uc = xs[PH:]
            cnt = _pool_count(_rows(uc.shape) + r0, g)
            pooled = (ws / cnt - uc).astype(BF16)
            z = jnp.dot(pooled, w, preferred_element_type=F32) + par[0:1]
            dzs = dzs_ref[pl.ds(r0, ch), :]
            dz = dzs * par[1:2]
            dzb = dz.astype(BF16)
            dw_acc[...] += lax.dot_general(pooled, dzb, (((0,), (0,)), ((), ())), preferred_element_type=F32)
            dpooled = lax.dot_general(dzb, w, (((1,), (1,)), ((), ())), preferred_element_type=F32)
            q_pad[pl.ds(r0, ch), :] = dpooled / cnt
            return (db + jnp.sum(dz, axis=0, keepdims=True), dsc + jnp.sum(dzs * z, axis=0, keepdims=True))

        zero = jnp.zeros((1, W), F32)
        db, dsc = lax.fori_loop(0, nch, chunk, (zero, zero))
        dpar_ref[...] = jnp.concatenate([db, dsc], axis=0)
        dw_ref[...] = dw_acc[...].astype(dw_ref.dtype)

        def back(ci, _):
            r0 = pl.multiple_of(ci * ch, ch)
            qs = q_pad[pl.ds(r0, ch + PH), :]
            ws = _window_sum(qs, g, True)[:ch]
            qc = qs[:ch]
            cnt = _pool_count(_rows(qc.shape) + r0, g)
            du_ref[pl.ds(r0, ch), :] = (ws - qc * cnt).astype(du_ref.dtype)
            return 0

        lax.fori_loop(0, nch, back, 0)

    blk = pl.BlockSpec((S, W), lambda g: (0, g))
    wspec = pl.BlockSpec((None, W, W), lambda g: (g, 0, 0))
    pspec = pl.BlockSpec((2, W), lambda g: (0, g))
    return pl.pallas_call(
        body,
        name=name,
        grid=(G,),
        in_specs=[blk, blk, wspec, pspec],
        out_specs=[blk, wspec, pspec],
        out_shape=[jax.ShapeDtypeStruct((S, D), BF16), jax.ShapeDtypeStruct((G, W, W), BF16),
                   jax.ShapeDtypeStruct((2, D), F32)],
        scratch_shapes=[pltpu.VMEM((S + PH, W), F32), pltpu.VMEM((S + PH, W), F32), pltpu.VMEM((W, W), F32)],
        compiler_params=_cparams("parallel"),
    )(u, dzs, wgrp, par)


def _my_place():
    x, y, c = lax.axis_index("x"), lax.axis_index("y"), lax.axis_index("c")
    return x, y, c, 4 * x + 2 * y + c


def _peers(x, y, c):
    out = []
    for d in range(1, N_DEV):
        px = 1 - x if d & 4 else x
        py = 1 - y if d & 2 else y
        pc = 1 - c if d & 1 else c
        out.append(((px, py, pc), 4 * px + 2 * py + pc))
    return out


def _window(ref, axis, start, size):
    idx = [slice(None)] * len(ref.shape)
    idx[axis] = pl.ds(start, size)
    return ref.at[tuple(idx)]


def _to_bf16(name, arrs, after=()):
    outs = []
    for i, a in enumerate(arrs):
        a2 = a.reshape(-1, a.shape[-1])
        tr = _tile(a2.shape[0], 512)
        o = pl.pallas_call(
            lambda a_ref, *rest: rest[-1].__setitem__(Ellipsis, a_ref[...].astype(BF16)),
            name=f"{name}_{i}",
            grid=(a2.shape[0] // tr,),
            in_specs=[pl.BlockSpec((tr, a2.shape[1]), lambda r: (r, 0))] + [pl.BlockSpec(memory_space=pl.ANY)] * len(after),
            out_specs=pl.BlockSpec((tr, a2.shape[1]), lambda r: (r, 0)),
            out_shape=jax.ShapeDtypeStruct(a2.shape, BF16),
            compiler_params=_cparams("parallel"),
        )(a2, *after)
        outs.append(o.reshape(a.shape))
    return outs


def _all_gather(name, shards, axes):
    n = len(shards)
    sizes = [s.shape[ax] for s, ax in zip(shards, axes)]

    def body(*refs):
        ins, outs = refs[:n], refs[n:2 * n]
        send, recv, loc = refs[2 * n:]
        x, y, c, me = _my_place()
        peers = _peers(x, y, c)
        local = []
        for i in range(n):
            dst = _window(outs[i], axes[i], me * sizes[i], sizes[i])
            cp = pltpu.make_async_copy(ins[i], dst, loc.at[i])
            cp.start()
            local.append(cp)
            for peer, _ in peers:
                pltpu.make_async_remote_copy(src_ref=ins[i], dst_ref=dst, send_sem=send.at[i], recv_sem=recv.at[i],
                                             device_id=peer, device_id_type=MESH).start()
        for i in range(n):
            local[i].wait()
            seven = _window(outs[i], axes[i], 0, (N_DEV - 1) * sizes[i])
            pltpu.make_async_remote_copy(src_ref=seven, dst_ref=seven, send_sem=send.at[i], recv_sem=recv.at[i],
                                         device_id=(x, y, c), device_id_type=MESH).wait()

    def full_shape(s, ax):
        shp = list(s.shape)
        shp[ax] *= N_DEV
        return jax.ShapeDtypeStruct(tuple(shp), s.dtype)

    any_spec = pl.BlockSpec(memory_space=pl.ANY)
    return pl.pallas_call(
        body,
        name=name,
        in_specs=[any_spec] * n,
        out_specs=[any_spec] * n,
        out_shape=[full_shape(s, ax) for s, ax in zip(shards, axes)],
        scratch_shapes=[pltpu.SemaphoreType.DMA((n,)), pltpu.SemaphoreType.DMA((n,)), pltpu.SemaphoreType.DMA((n,))],
        compiler_params=pltpu.CompilerParams(has_side_effects=True),
    )(*shards)


HBM_SPEC = pl.BlockSpec(memory_space=pltpu.HBM)
SEM_SPEC = pl.BlockSpec(memory_space=pltpu.SEMAPHORE)
SPLIT_EFFECT = pltpu.SideEffectType.DATAFLOW_SIDE_EFFECTING


def _push_all(kind, src, dst, axis, size, send_sem, recv_sem, place):
    x, y, c, me = place
    for peer, pidx in _peers(x, y, c):
        if kind == "gather":
            s = d = _window(dst, axis, me * size, size)
        else:
            s, d = _window(src, axis, pidx * size, size), dst.at[me]
        pltpu.make_async_remote_copy(src_ref=s, dst_ref=d, send_sem=send_sem, recv_sem=recv_sem, device_id=peer,
                                     device_id_type=MESH).start()


def _drain_all(kind, dst, axis, size, send_sem, recv_sem, place):
    x, y, c, _ = place
    seven = _window(dst, axis, 0, (N_DEV - 1) * size) if kind == "gather" else dst.at[pl.ds(0, N_DEV - 1)]
    pltpu.make_async_remote_copy(src_ref=seven, dst_ref=seven, send_sem=send_sem, recv_sem=recv_sem,
                                 device_id=(x, y, c), device_id_type=MESH).wait()


def _own_block_placed(src, axis, size, me):
    own = lax.dynamic_slice_in_dim(src, me * size, size, axis)
    return lax.dynamic_update_slice_in_dim(lax.empty((N_DEV,) + own.shape, src.dtype), own[None], me, 0)


def _split_start(name, kind, srcs, lands, axes, sizes, after=()):
    n, ns, na = len(lands), len(srcs), len(after)

    def body(*refs):
        src_refs, land_refs = refs[:ns], refs[ns:ns + n]
        send, recv = refs[ns + n + na], refs[ns + n + na + 1]
        token = refs[-1]
        place = _my_place()
        for k in range(n):
            _push_all(kind, src_refs[k] if ns else None, land_refs[k], axes[k], sizes[k], send.at[k], recv.at[k], place)
        token[...] = jnp.zeros_like(token)

    hbm = lambda a: pltpu.HBM(a.shape, a.dtype)
    res = pl.pallas_call(
        body,
        name=name,
        out_shape=(pltpu.SemaphoreType.DMA((n,)), pltpu.SemaphoreType.DMA((n,)), *[hbm(a) for a in srcs],
                   *[hbm(a) for a in lands], jax.ShapeDtypeStruct((SUBLANES, LANES), F32)),
        in_specs=[HBM_SPEC] * (ns + n) + [pl.BlockSpec(memory_space=pl.ANY)] * na,
        out_specs=(SEM_SPEC, SEM_SPEC, *[HBM_SPEC] * (ns + n), pl.BlockSpec(memory_space=pltpu.VMEM)),
        input_output_aliases={k: 2 + k for k in range(ns + n)},
        compiler_params=pltpu.CompilerParams(has_side_effects=SPLIT_EFFECT),
    )(*[pltpu.with_memory_space_constraint(a, pltpu.HBM) for a in (*srcs, *lands)], *after)
    return res[0], res[1], list(res[2:2 + ns]), list(res[2 + ns:2 + ns + n]), res[-1]


def _split_wait(name, kind, handle, axes, sizes, after):
    send, recv, srcs, lands, _ = handle
    n, ns = len(lands), len(srcs)
    after = list(after) if isinstance(after, (list, tuple)) else [after]

    def body(*refs):
        land_refs = refs[ns:ns + n]
        send_ref, recv_ref = refs[ns + n], refs[ns + n + 1]
        place = _my_place()
        for k in range(n):
            _drain_all(kind, land_refs[k], axes[k], sizes[k], send_ref.at[k], recv_ref.at[k], place)

    hbm = lambda a: pltpu.HBM(a.shape, a.dtype)
    res = pl.pallas_call(
        body,
        name=name,
        out_shape=tuple(hbm(a) for a in (*srcs, *lands)),
        in_specs=[HBM_SPEC] * (ns + n) + [SEM_SPEC, SEM_SPEC] + [pl.BlockSpec(memory_space=pl.ANY)] * len(after),
        out_specs=tuple([HBM_SPEC] * (ns + n)),
        input_output_aliases={k: k for k in range(ns + n)},
        compiler_params=pltpu.CompilerParams(has_side_effects=SPLIT_EFFECT),
    )(*srcs, *lands, send, recv, *after)
    return list(res[ns:])


def _cast_into_window(name, a, l, axis, me1, after=()):
    shp = a.shape[1:]
    cast = lambda me_ref, a_ref, *rest: rest[-1].__setitem__(Ellipsis, a_ref[...].astype(BF16))
    if len(shp) == 3:
        assert axis == 1
        G, r, c = shp
        full = (G, r * N_DEV, c)
        grid = (G,)
        in_spec = pl.BlockSpec((None, None, r, c), lambda g, me: (l, g, 0, 0))
        out_spec = pl.BlockSpec((None, r, c), lambda g, me: (g, me[0], 0))
    else:
        r, c = shp
        tr = _tile(r, 512)
        nb = r // tr
        grid = (nb,)
        in_spec = pl.BlockSpec((None, tr, c), lambda i, me: (l, i, 0))
        if axis == 0:
            full = (r * N_DEV, c)
            out_spec = pl.BlockSpec((tr, c), lambda i, me: (me[0] * nb + i, 0))
        else:
            full = (r, c * N_DEV)
            out_spec = pl.BlockSpec((tr, c), lambda i, me: (i, me[0]))
    return pl.pallas_call(
        cast,
        name=name,
        grid_spec=pltpu.PrefetchScalarGridSpec(
            num_scalar_prefetch=1, grid=grid,
            in_specs=[in_spec] + [pl.BlockSpec(memory_space=pl.ANY)] * len(after), out_specs=out_spec),
        out_shape=jax.ShapeDtypeStruct(full, BF16),
        compiler_params=_cparams("arbitrary"),
    )(me1, a, *after)


def _adamw_math(w, g, m, v):
    m = ADAM_B1 * m + (1.0 - ADAM_B1) * g
    v = ADAM_B2 * v + (1.0 - ADAM_B2) * jnp.square(g)
    m_hat = m / (1.0 - ADAM_B1 ** ADAM_STEP)
    v_hat = v / (1.0 - ADAM_B2 ** ADAM_STEP)
    delta = -ADAM_LR * (m_hat / (jnp.sqrt(v_hat) + ADAM_EPS) + ADAM_WD * w)
    return delta, m, v


def _sum_slots(buf_ref):
    g = buf_ref[0].astype(F32)
    for s in range(1, N_DEV):
        g = g + buf_ref[s].astype(F32)
    return g


def _adamw_layer(name, buf, w, m, v, l, prev, after=()):
    shape = w.shape
    L, C = shape[0], shape[-1]
    Rr = math.prod(shape[1:-1])
    buf3 = buf.reshape(N_DEV, Rr, C)
    w3, m3, v3 = (t.reshape(L, Rr, C) for t in (w, m, v))
    tr = _tile(Rr, 2 * LANES) if Rr % LANES == 0 else Rr
    n_prev = 0 if prev is None else 4

    def body(buf_ref, w_ref, m_ref, v_ref, *rest):
        g_out, d_out, m_out, v_out = rest[n_prev + len(after):]
        g = _sum_slots(buf_ref)
        d, mm, vv = _adamw_math(w_ref[...], g, m_ref[...], v_ref[...])
        g_out[...] = g
        d_out[...] = d
        m_out[...] = mm
        v_out[...] = vv

    spec = pl.BlockSpec((None, tr, C), lambda r: (l, r, 0))
    outs = pl.pallas_call(
        body,
        name=name,
        grid=(Rr // tr,),
        in_specs=[pl.BlockSpec((N_DEV, tr, C), lambda r: (0, r, 0)), spec, spec, spec]
        + [pl.BlockSpec(memory_space=pl.ANY)] * (n_prev + len(after)),
        out_specs=[spec] * 4,
        out_shape=[jax.ShapeDtypeStruct((L, Rr, C), F32)] * 4,
        input_output_aliases={4 + k: k for k in range(n_prev)},
        compiler_params=_cparams("parallel"),
    )(buf3, w3, m3, v3, *(prev or ()), *after)
    return list(outs)


def _sum8(name, buf):
    R = buf.shape[1]

    def body(buf_ref, o_ref):
        o_ref[...] = _sum_slots(buf_ref)

    return pl.pallas_call(
        body,
        name=name,
        in_specs=[pl.BlockSpec(buf.shape, lambda: (0, 0, 0))],
        out_specs=pl.BlockSpec((R, LANES), lambda: (0, 0)),
        out_shape=jax.ShapeDtypeStruct((R, LANES), F32),
        compiler_params=_cparams(),
    )(buf)


def _adamw_small(name, w, g, m, v):
    shape = w.shape
    w2, g2, m2, v2 = (t.reshape(-1, shape[-1]) for t in (w, g, m, v))
    R, C = w2.shape
    tr = _row_tile(R, 512)

    def body(w_ref, g_ref, m_ref, v_ref, d_out, m_out, v_out):
        d, mm, vv = _adamw_math(w_ref[...], g_ref[...], m_ref[...], v_ref[...])
        d_out[...] = d
        m_out[...] = mm
        v_out[...] = vv

    spec = pl.BlockSpec((tr, C), lambda r: (r, 0))
    outs = pl.pallas_call(
        body,
        name=name,
        grid=(R // tr,),
        in_specs=[spec] * 4,
        out_specs=[spec] * 3,
        out_shape=[jax.ShapeDtypeStruct((R, C), F32)] * 3,
        compiler_params=_cparams("parallel"),
    )(w2, g2, m2, v2)
    return [o.reshape(shape) for o in outs]


def _pack(arrs, pad_rows_to=SUBLANES):
    parts = []
    for a in arrs:
        flat = a.reshape(-1)
        per = LANES * pad_rows_to
        padded = -(-flat.shape[0] // per) * per
        if padded != flat.shape[0]:
            flat = jnp.pad(flat, (0, padded - flat.shape[0]))
        parts.append(flat.reshape(-1, LANES))
    return jnp.concatenate(parts, axis=0)


def _unpack(packed, shapes, pad_rows_to=SUBLANES):
    out = []
    r = 0
    for shp in shapes:
        nel = math.prod(shp)
        per = LANES * pad_rows_to
        rows = -(-nel // per) * pad_rows_to
        out.append(packed[r:r + rows].reshape(-1)[:nel].reshape(shp))
        r += rows
    return out


BIG = ("lru_w_in", "lru_w_out", "pool_w_in", "pool_w_grp", "pool_w_out", "mlp_w1", "mlp_w2", "ple_w", "ple_gate_w")
BIG_AXIS = {"lru_w_in": 2, "lru_w_out": 1, "pool_w_in": 1, "pool_w_grp": 2, "pool_w_out": 1, "mlp_w1": 2,
            "mlp_w2": 1, "ple_w": 2, "ple_gate_w": 1}
SMALL_SHARDED = ("lru_conv_w", "pool_b_grp", "pool_scale")
REPLICATED = ("lru_conv_b", "lru_wa", "lru_ba", "lru_wx", "lru_bx", "lru_lambda", "ln_mix_g", "ln_mix_b",
              "ln_mlp_g", "ln_mlp_b", "ple_gate_b")
WEIGHTS = ("lru_w_in", "lru_conv_w", "lru_conv_b", "lru_wa", "lru_ba", "lru_wx", "lru_bx", "lru_lambda", "lru_w_out",
           "pool_w_in", "pool_w_grp", "pool_b_grp", "pool_scale", "pool_w_out", "ln_mix_g", "ln_mix_b", "mlp_w1",
           "mlp_w2", "ln_mlp_g", "ln_mlp_b", "ple_w", "ple_gate_w", "ple_gate_b")
INPUTS = ("x", "p") + WEIGHTS + ("loss_target",) + tuple("m_" + n for n in WEIGHTS) + tuple("v_" + n for n in WEIGHTS)


def _gather_last_axis(packed_full, shard_shape):
    nel = math.prod(shard_shape)
    blocks = packed_full.reshape(N_DEV, -1)[:, :nel].reshape((N_DEV,) + tuple(shard_shape))
    return jnp.concatenate([blocks[d] for d in range(N_DEV)], axis=-1)


def kernel(x, p, lru_w_in, lru_conv_w, lru_conv_b, lru_wa, lru_ba, lru_wx, lru_bx, lru_lambda, lru_w_out, pool_w_in, pool_w_grp, pool_b_grp, pool_scale, pool_w_out, ln_mix_g, ln_mix_b, mlp_w1, mlp_w2, ln_mlp_g, ln_mlp_b, ple_w, ple_gate_w, ple_gate_b, loss_target, m_lru_w_in, m_lru_conv_w, m_lru_conv_b, m_lru_wa, m_lru_ba, m_lru_wx, m_lru_bx, m_lru_lambda, m_lru_w_out, m_pool_w_in, m_pool_w_grp, m_pool_b_grp, m_pool_scale, m_pool_w_out, m_ln_mix_g, m_ln_mix_b, m_mlp_w1, m_mlp_w2, m_ln_mlp_g, m_ln_mlp_b, m_ple_w, m_ple_gate_w, m_ple_gate_b, v_lru_w_in, v_lru_conv_w, v_lru_conv_b, v_lru_wa, v_lru_ba, v_lru_wx, v_lru_bx, v_lru_lambda, v_lru_w_out, v_pool_w_in, v_pool_w_grp, v_pool_b_grp, v_pool_scale, v_pool_w_out, v_ln_mix_g, v_ln_mix_b, v_mlp_w1, v_mlp_w2, v_ln_mlp_g, v_ln_mlp_b, v_ple_w, v_ple_gate_w, v_ple_gate_b):
    A = dict(zip(INPUTS, (x, p, lru_w_in, lru_conv_w, lru_conv_b, lru_wa, lru_ba, lru_wx, lru_bx, lru_lambda, lru_w_out, pool_w_in, pool_w_grp, pool_b_grp, pool_scale, pool_w_out, ln_mix_g, ln_mix_b, mlp_w1, mlp_w2, ln_mlp_g, ln_mlp_b, ple_w, ple_gate_w, ple_gate_b, loss_target, m_lru_w_in, m_lru_conv_w, m_lru_conv_b, m_lru_wa, m_lru_ba, m_lru_wx, m_lru_bx, m_lru_lambda, m_lru_w_out, m_pool_w_in, m_pool_w_grp, m_pool_b_grp, m_pool_scale, m_pool_w_out, m_ln_mix_g, m_ln_mix_b, m_mlp_w1, m_mlp_w2, m_ln_mlp_g, m_ln_mlp_b, m_ple_w, m_ple_gate_w, m_ple_gate_b, v_lru_w_in, v_lru_conv_w, v_lru_conv_b, v_lru_wa, v_lru_ba, v_lru_wx, v_lru_bx, v_lru_lambda, v_lru_w_out, v_pool_w_in, v_pool_w_grp, v_pool_b_grp, v_pool_scale, v_pool_w_out, v_ln_mix_g, v_ln_mix_b, v_mlp_w1, v_mlp_w2, v_ln_mlp_g, v_ln_mlp_b, v_ple_w, v_ple_gate_w, v_ple_gate_b)))
    depth = ln_mix_g.shape[0]
    alpha = (2 * depth) ** 0.25
    S, D = x.shape[1], x.shape[2]
    xs = x.reshape(S, D)
    tgt = loss_target.reshape(S, D)
    p3 = p.reshape(depth, S, p.shape[-1])
    me = 4 * lax.axis_index("x") + 2 * lax.axis_index("y") + lax.axis_index("c")

    def layer_weights(i):
        s = i // 2
        mixer = ("lru_w_in", "lru_w_out") if i % 2 == 0 else ("pool_w_in", "pool_w_grp", "pool_w_out")
        return [(n, s) for n in mixer] + [(n, i) for n in ("mlp_w1", "mlp_w2", "ple_w", "ple_gate_w")]

    def axis_of(key):
        return 0 if key[0] == "small" else BIG_AXIS[key[0]] - 1

    def start_gather(tag, keys, after):
        axes = [axis_of(k) for k in keys]
        lands = [land[k] for k in keys]
        sizes = [a.shape[ax] // N_DEV for a, ax in zip(lands, axes)]
        return keys, _split_start(f"gather_{tag}_start", "gather", [], lands, axes, sizes, after=after), axes, sizes

    def finish_gather(tag, pending, after):
        keys, handle, axes, sizes = pending
        for (n, l), full in zip(keys, _split_wait(f"gather_{tag}_wait", "gather", handle, axes, sizes, after)):
            W[n][l] = full

    def start_exchange(tag, keys, arrs, after):
        axes = [axis_of(k) for k in keys]
        sizes = [a.shape[ax] // N_DEV for a, ax in zip(arrs, axes)]
        lands = [_own_block_placed(a, ax, sz, me) for a, ax, sz in zip(arrs, axes, sizes)]
        return keys, _split_start(f"exchange_{tag}_start", "scatter", arrs, lands, axes, sizes, after=after), axes, sizes

    def finish_exchange(tag, pending, after):
        keys, handle, axes, sizes = pending
        partial.update(zip(keys, _split_wait(f"exchange_{tag}_wait", "scatter", handle, axes, sizes, after)))

    me1 = jnp.reshape(me, (1,)).astype(jnp.int32)
    land = {}
    W = {n: [None] * A[n].shape[0] for n in BIG}
    small_shard_shapes = [A[n].shape for n in SMALL_SHARDED]
    gathered = _all_gather("gather_small_params", [_pack([A[n] for n in SMALL_SHARDED])], [0])
    def gather_groups(i):
        keys = layer_weights(i)
        mixer, (w1, w2, pw, pg) = keys[:-4], keys[-4:]
        if i % 2 == 0:
            return [("in", mixer[:1]), ("out", mixer[1:]), ("up", [w1]), ("rest", [w2, pw, pg])]
        return [("in", mixer + [w1]), ("rest", [w2, pw, pg])]

    gather_pending = {}

    def send_layer(i, behind):
        for tag, keys in gather_groups(i):
            for k in keys:
                land[k] = _cast_into_window(f"cast_{k[0]}_{k[1]}", A[k[0]], k[1], axis_of(k), me1, after=behind)
            gather_pending[(i, tag)] = start_gather(f"l{i}_{tag}", keys, behind)
            behind = (gather_pending[(i, tag)][1][4],)
        return behind

    layer0_started = send_layer(0, (gathered[0],))
    small_full = gathered[0].reshape(N_DEV, -1, LANES)
    r = 0
    for n, shp in zip(SMALL_SHARDED, small_shard_shapes):
        rows = -(-math.prod(shp) // (LANES * SUBLANES)) * SUBLANES
        W[n] = _gather_last_axis(small_full[:, r:r + rows], shp)
        r += rows
    wa_b, wx_b = _to_bf16("cast_gates", [lru_wa, lru_wx], after=layer0_started)
    n_lru = lru_w_in.shape[0]
    lru_par = [jnp.concatenate([W["lru_conv_w"][s], lru_conv_b[s][None], lru_ba[s][None], lru_bx[s][None],
                                lru_lambda[s][None]], axis=0) for s in range(n_lru)]
    pool_par = [jnp.stack([W["pool_b_grp"][s], W["pool_scale"][s]], axis=0) for s in range(pool_w_in.shape[0])]

    saved = []
    h_in = xs
    h_in_b, p3b = _to_bf16("cast_inputs", [xs, p3], after=layer0_started)
    layer1_started = send_layer(1, (layer0_started[0], h_in_b))
    finish_gather("l0_in", gather_pending[(0, "in")], [h_in_b, wa_b, layer1_started[0]])
    for i in range(depth):
        s = i // 2
        sv = {"x0b": h_in_b}
        if i > 0:
            finish_gather(f"l{i}_in", gather_pending[(i, "in")], h_in)
        ln_out = dict(out_dtypes=[F32, F32, BF16], tm=MM_TM_ROWS, tn=D,
                      epi=lambda acc, xp, g, b: _ln_apply(alpha * xp + acc, g, b))
        if i % 2 == 0:
            sv["proj"] = _mm(f"l{i}_lru_in", h_in_b, W["lru_w_in"], "nn", [F32], b_lead=s)
            if i == 0:
                behind = (layer1_started[0], sv["proj"])
                for later in range(2, depth):
                    behind = send_layer(later, behind)
                all_started = behind[0]
            sv["gh"], sv["h"] = _lru_fwd(f"l{i}_lru_core", sv["proj"], lru_par[s], wa_b[s], wx_b[s])
            finish_gather(f"l{i}_out", gather_pending[(i, "out")], [sv["gh"], all_started])
            mix_in, mix_w = sv["gh"], W["lru_w_out"]
        else:
            sv["u"] = _mm(f"l{i}_pool_in", h_in_b, W["pool_w_in"], "nn", [F32], b_lead=s)
            sv["zs"] = _pool_fwd(f"l{i}_pool_core", sv["u"], W["pool_w_grp"][s], pool_par[s])
            mix_in, mix_w = sv["zs"], W["pool_w_out"]
        sv["z1"], sv["x1"], sv["x1b"] = _mm(f"l{i}_mix_out_ln", mix_in, mix_w, "nn", b_lead=s,
                                            extras=[h_in, ln_mix_g[i][None], ln_mix_b[i][None]], **ln_out)
        if i % 2 == 0:
            finish_gather(f"l{i}_up", gather_pending[(i, "up")], sv["x1b"])
        sv["hpre"], sv["hact"] = _mm(f"l{i}_mlp_up", sv["x1b"], W["mlp_w1"], "nn", [BF16, BF16], b_lead=i,
                                     epi=lambda acc: (acc, jnp.square(jnp.maximum(acc, 0.0))))
        finish_gather(f"l{i}_rest", gather_pending[(i, "rest")], sv["hact"])
        sv["z2"], sv["x2"], sv["x2b"] = _mm(f"l{i}_mlp_down_ln", sv["hact"], W["mlp_w2"], "nn", b_lead=i,
                                            extras=[sv["x1"], ln_mlp_g[i][None], ln_mlp_b[i][None]], **ln_out)
        sv["pp"] = _mm(f"l{i}_ple_up", p3b, W["ple_w"], "nn", [F32], a_lead=i, b_lead=i)

        def ple_epi(acc, bg, x2t, ppt):
            gpre = acc + bg
            x3 = x2t + ppt * jax.nn.sigmoid(gpre)
            return x3, x3, gpre

        h_in, h_in_b, sv["gpre"] = _mm(f"l{i}_ple_gate", sv["x2b"], W["ple_gate_w"], "nn", [F32, BF16, F32], b_lead=i,
                                       epi=ple_epi, extras=[ple_gate_b[i][None], sv["x2"], sv["pp"]], tn=MM_TN)
        saved.append(sv)

    dx, dpp, dgpre, sq, dbg = _loss_and_grad("loss", h_in, tgt, saved[-1]["gpre"], saved[-1]["pp"])
    loss = lax.psum(0.5 * sq[0, 0] / D, ("x", "y", "c"))

    dW = {n: [None] * A[n].shape[0] for n in BIG}
    dsmall = {n: [None] * A[n].shape[0] for n in REPLICATED + SMALL_SHARDED}
    small_names = REPLICATED + SMALL_SHARDED
    partial = {}
    exchange_pending = {}
    exchange_token = ()
    for i in reversed(range(depth)):
        s = i // 2
        sv = saved[i]
        dsmall["ple_gate_b"][i] = dbg[0]
        dW["ple_w"][i] = _mm(f"l{i}_d_ple_w", p3b, dpp, "tn", [BF16], a_lead=i, after=exchange_token)
        dW["ple_gate_w"][i] = _mm(f"l{i}_d_ple_gate_w", sv["x2b"], dgpre, "tn", [BF16])
        ln_back = dict(out_dtypes=[F32, BF16], tm=MM_TM_ROWS, tn=D, n_sums=2)
        dz2, dz2b, dg, db = _mm(f"l{i}_d_x2_ln", dgpre, W["ple_gate_w"], "nt", b_lead=i,
                                extras=[dx, sv["z2"], ln_mlp_g[i][None]],
                                epi=lambda acc, d, z, g: _ln_grad(acc + d, z, g), after=exchange_token, **ln_back)
        dsmall["ln_mlp_g"][i], dsmall["ln_mlp_b"][i] = dg[0], db[0]
        dhpre = _mm(f"l{i}_d_hpre", dz2b, W["mlp_w2"], "nt", [BF16], b_lead=i, extras=[sv["hpre"]],
                    epi=lambda acc, hp: (acc * (2.0 * jnp.maximum(hp.astype(F32), 0.0)),))
        dW["mlp_w2"][i] = _mm(f"l{i}_d_mlp_w2", sv["hact"], dz2b, "tn", [BF16])
        dW["mlp_w1"][i] = _mm(f"l{i}_d_mlp_w1", sv["x1b"], dhpre, "tn", [BF16])
        mlp_after = ()
        if i == 0:
            early = [(n, 0) for n in ("ple_w", "ple_gate_w", "mlp_w2", "mlp_w1")]
            exchange_early0 = start_exchange("early0", early, [dW[n][l] for n, l in early], ())
            mlp_after = (exchange_early0[1][4],)
        dz1, dz1b, dg, db = _mm(f"l{i}_d_x1_ln", dhpre, W["mlp_w1"], "nt", b_lead=i,
                                extras=[dz2, sv["z1"], ln_mix_g[i][None]],
                                epi=lambda acc, d, z, g: _ln_grad(acc + alpha * d, z, g), after=mlp_after, **ln_back)
        dsmall["ln_mix_g"][i], dsmall["ln_mix_b"][i] = dg[0], db[0]
        if i % 2 == 0:
            dW["lru_w_out"][s] = _mm(f"l{i}_d_lru_w_out", sv["gh"], dz1b, "tn", [BF16])
            dgh = _mm(f"l{i}_d_gh", dz1b, W["lru_w_out"], "nt", [F32], b_lead=s)
            dup, dy, dwa, dwx, dpar = _lru_bwd(f"l{i}_lru_core_bwd", sv["proj"], sv["h"], dgh, lru_par[s],
                                               wa_b[s], wx_b[s])
            dsmall["lru_wa"][s], dsmall["lru_wx"][s] = dwa, dwx
            dsmall["lru_conv_w"][s] = dpar[0:4]
            for k, n in enumerate(("lru_conv_b", "lru_ba", "lru_bx", "lru_lambda")):
                dsmall[n][s] = dpar[4 + k]
            dmix_in = jnp.concatenate([dup, dy], axis=1)
            win = "lru_w_in"
        else:
            dW["pool_w_out"][s] = _mm(f"l{i}_d_pool_w_out", sv["zs"], dz1b, "tn", [BF16])
            dzs = _mm(f"l{i}_d_zs", dz1b, W["pool_w_out"], "nt", [F32], b_lead=s)
            dmix_in, dW["pool_w_grp"][s], dpar = _pool_bwd(f"l{i}_pool_core_bwd", sv["u"], dzs, W["pool_w_grp"][s],
                                                          pool_par[s])
            dsmall["pool_b_grp"][s], dsmall["pool_scale"][s] = dpar[0], dpar[1]
            win = "pool_w_in"
        dW[win][s] = _mm(f"l{i}_d_{win}", sv["x0b"], dmix_in, "tn", [BF16])
        x0_after = ()
        if i > 0:
            keys = layer_weights(i)
            exchange_pending[i] = start_exchange(f"l{i}", keys, [dW[n][l] for n, l in keys], ())
            exchange_token = (exchange_pending[i][1][4],)
        else:
            small_grads = [jnp.stack(dsmall[n]) for n in small_names]
            small_shapes = [g.shape for g in small_grads]
            packed_g = _pack(small_grads)
            assert packed_g.shape[0] % (N_DEV * SUBLANES) == 0, packed_g.shape
            late = [("lru_w_out", 0), ("lru_w_in", 0), ("small", 0)]
            exchange_late0 = start_exchange("late0", late, [dW["lru_w_out"][0], dW["lru_w_in"][0], packed_g], ())
            x0_after = (exchange_late0[1][4],)
        if i > 0:
            def x0_epi(acc, d, gpre, pp):
                dxv = acc + alpha * d
                return (dxv,) + _ple_grad(dxv, gpre, pp)

            dx, dpp, dgpre, dbg = _mm(f"l{i}_d_x0_ple", dmix_in, W[win], "nt", [F32, BF16, BF16], b_lead=s,
                                      extras=[dz1, saved[i - 1]["gpre"], saved[i - 1]["pp"]], epi=x0_epi,
                                      tm=MM_TM_ROWS, tn=D, n_sums=1)
        else:
            dx = _mm(f"l{i}_d_x0", dmix_in, W[win], "nt", [F32], b_lead=s, extras=[dz1],
                     epi=lambda acc, d: (acc + alpha * d,), after=x0_after)
    grad_x = dx.reshape(x.shape)

    for i in range(1, depth):
        finish_exchange(f"l{i}", exchange_pending[i], x0_after[0])
    stacked = {n: None for n in BIG}
    layer0 = layer_weights(0)

    def adamw(n, l, after=()):
        stacked[n] = _adamw_layer(f"adamw_{n}_{l}", partial[(n, l)], A[n], A["m_" + n], A["v_" + n], l, stacked[n],
                                  after=after)

    for n in BIG:
        for l in reversed(range(A[n].shape[0])):
            if (n, l) not in layer0:
                adamw(n, l)
    behind = [dx] + [stacked[n][0] for n in BIG if stacked[n] is not None]
    finish_exchange("early0", exchange_early0, behind)
    finish_exchange("late0", exchange_late0, behind)
    red = _sum8("sum_small", partial[("small", 0)])
    rows = red.shape[0]
    red_land = lax.dynamic_update_slice_in_dim(lax.empty((N_DEV * rows, LANES), F32), red, me * rows, 0)
    small_handle = _split_start("gather_small_start", "gather", [], [red_land], [0], [rows])
    for n, l in layer0:
        adamw(n, l, after=(small_handle[4],))
    outs = {n: [o.reshape(A[n].shape) for o in stacked[n]] for n in BIG}
    red_full = _split_wait("gather_small_wait", "gather", small_handle, [0], [rows], [stacked[n][0] for n, _ in layer0])[0]
    small_g = dict(zip(small_names, _unpack(red_full, small_shapes)))
    for n in SMALL_SHARDED:
        width = A[n].shape[-1]
        small_g[n] = lax.dynamic_slice_in_dim(small_g[n], me * width, width, axis=small_g[n].ndim - 1)
    for n in small_names:
        outs[n] = [small_g[n]] + _adamw_small(f"adamw_{n}", A[n], small_g[n], A["m_" + n], A["v_" + n])

    return (loss, grad_x, *[outs[n][0] for n in WEIGHTS], *[outs[n][1] for n in WEIGHTS],
            *[outs[n][2] for n in WEIGHTS], *[outs[n][3] for n in WEIGHTS])
```

```python
import functools
import math

import jax
import jax.numpy as jnp
from jax import lax
from jax.experimental import pallas as pl
from jax.experimental.pallas import tpu as pltpu

F32 = jnp.float32
BF16 = jnp.bfloat16
MESH = pl.DeviceIdType.MESH
N_DEV = 8
LANES = 128
SUBLANES = 8

LN_EPS = 1e-5
LRU_C = 8.0
CONV_WIDTH = 4
POOL_HALO = 16
ADAM_LR = 0.001
ADAM_B1 = 0.9
ADAM_B2 = 0.999
ADAM_EPS = 1e-08
ADAM_WD = 0.01
ADAM_STEP = 10

VMEM_LIMIT = 48 * 1024 * 1024
SEQ_CHUNK = 256
MM_TK = 4096
MM_TK_TOKENS = 4096
MM_TM_ROWS = 512
MM_TN = 512
MM_TN_WIDE = 1024
MM_TN_WIDE_MAX_K = 2048
GELU_C0 = math.sqrt(2.0 / math.pi)
GELU_C1 = 0.044715


def _in_hbm(*arrays):
    return [pltpu.with_memory_space_constraint(a, pltpu.HBM) for a in arrays]


def _cparams(*sem):
    return pltpu.CompilerParams(dimension_semantics=tuple(sem) if sem else None, vmem_limit_bytes=VMEM_LIMIT)


def _tile(n, pref):
    if n <= pref:
        return n
    t = pref - pref % LANES
    while t >= LANES:
        if n % t == 0:
            return t
        t -= LANES
    return n


def _row_tile(n, pref):
    if n <= pref:
        return n
    t = pref - pref % SUBLANES
    while t >= SUBLANES:
        if n % t == 0:
            return t
        t -= SUBLANES
    return n


def _mm(name, a, b, mode, out_dtypes, epi=None, extras=(), a_lead=None, b_lead=None, tm=1024, tn=None, tk=None,
        after=(), n_sums=0):
    if isinstance(b, (list, tuple)):
        b, b_lead = b[b_lead], None
    a2 = a.shape[-2:]
    b2 = b.shape[-2:]
    if mode == "nn":
        (M, K), N = a2, b2[1]
        assert b2[0] == K
    elif mode == "nt":
        (M, K), N = a2, b2[0]
        assert b2[1] == K
    else:
        (K, M), N = a2, b2[1]
        assert b2[0] == K
    if tk is None:
        tk = MM_TK_TOKENS if mode == "tn" else MM_TK
    if tn is None:
        tn = MM_TN_WIDE if (mode != "tn" and K <= MM_TN_WIDE_MAX_K) else MM_TN
    tm, tn, tk = _tile(M, tm), _tile(N, tn), _tile(K, tk)
    nk = K // tk
    n_extra = len(extras)
    n_out = len(out_dtypes)
    assert n_sums == 0 or tn == N
    resident = {"pipeline_mode": pl.Buffered(1)} if (tn == N and nk == 1) else {}

    def lead(shape, idx, which, **kw):
        if which is None:
            return pl.BlockSpec(shape, idx, **kw)
        return pl.BlockSpec((None,) + shape, lambda i, j, k: (which,) + idx(i, j, k), **kw)

    if mode == "nn":
        a_spec = lead((tm, tk), lambda i, j, k: (i, k), a_lead)
        b_spec = lead((tk, tn), lambda i, j, k: (k, j), b_lead, **resident)
        dims = (((1,), (0,)), ((), ()))
    elif mode == "nt":
        a_spec = lead((tm, tk), lambda i, j, k: (i, k), a_lead)
        b_spec = lead((tn, tk), lambda i, j, k: (j, k), b_lead, **resident)
        dims = (((1,), (1,)), ((), ()))
    else:
        a_spec = lead((tk, tm), lambda i, j, k: (k, i), a_lead)
        b_spec = lead((tk, tn), lambda i, j, k: (k, j), b_lead, **resident)
        dims = (((0,), (0,)), ((), ()))
    e_specs = []
    for e in extras:
        if e.shape[0] == 1:
            e_specs.append(pl.BlockSpec((1, tn), lambda i, j, k: (0, j)))
        else:
            e_specs.append(pl.BlockSpec((tm, tn), lambda i, j, k: (i, j)))

    n_after = len(after)

    def body(a_ref, b_ref, *rest):
        e_refs = rest[:n_extra]
        rest = rest[:n_extra] + rest[n_extra + n_after:]
        o_refs = rest[n_extra:n_extra + n_out]
        s_refs = rest[n_extra + n_out:n_extra + n_out + n_sums]
        part = lax.dot_general(a_ref[...].astype(BF16), b_ref[...].astype(BF16), dims, preferred_element_type=F32)

        def finish(r):
            res = (r,) if epi is None else epi(r, *[e[...] for e in e_refs])
            for o, v in zip(o_refs, res[:n_out]):
                o[...] = v.astype(o.dtype)
            first = pl.program_id(0) == 0
            for sr, v in zip(s_refs, res[n_out:]):
                @pl.when(first)
                def _(sr=sr, v=v):
                    sr[...] = v

                @pl.when(jnp.logical_not(first))
                def _(sr=sr, v=v):
                    sr[...] += v

        if nk == 1:
            finish(part)
            return
        acc = rest[n_extra + n_out + n_sums]
        k = pl.program_id(2)

        @pl.when(k == 0)
        def _():
            acc[...] = part

        @pl.when(jnp.logical_and(k > 0, k < nk - 1))
        def _():
            acc[...] += part

        @pl.when(k == nk - 1)
        def _():
            finish(acc[...] + part)

    outs = pl.pallas_call(
        body,
        name=name,
        grid=(M // tm, N // tn, nk),
        in_specs=[a_spec, b_spec] + e_specs + [pl.BlockSpec(memory_space=pl.ANY)] * n_after,
        out_specs=[pl.BlockSpec((tm, tn), lambda i, j, k: (i, j)) for _ in out_dtypes]
        + [pl.BlockSpec((1, tn), lambda i, j, k: (0, 0))] * n_sums,
        out_shape=[jax.ShapeDtypeStruct((M, N), d) for d in out_dtypes] + [jax.ShapeDtypeStruct((1, N), F32)] * n_sums,
        scratch_shapes=[pltpu.VMEM((tm, tn), F32)] if nk > 1 else [],
        compiler_params=_cparams(*(("arbitrary",) * 3 if n_sums else ("parallel", "parallel", "arbitrary"))),
    )(*_in_hbm(a, b, *extras), *after)
    return outs[0] if n_out + n_sums == 1 else tuple(outs)


def _rowwise(name, fn, tiled, params, outs, accs=(), tm=256, after=()):
    S = tiled[0].shape[0]
    tm = _tile(S, tm)
    nt, npar, no = len(tiled), len(params), len(outs)
    n_after = len(after)

    def body(*refs):
        t_refs = refs[:nt]
        p_refs = refs[nt:nt + npar]
        refs = refs[nt + npar + n_after:]
        o_refs = refs[:no]
        a_refs = refs[no:]
        res = fn(*[r[...] for r in t_refs], *[r[...] for r in p_refs])
        for o, v in zip(o_refs, res[:no]):
            o[...] = v.astype(o.dtype)
        first = pl.program_id(0) == 0
        for ar, v in zip(a_refs, res[no:]):
            @pl.when(first)
            def _(ar=ar, v=v):
                ar[...] = v

            @pl.when(jnp.logical_not(first))
            def _(ar=ar, v=v):
                ar[...] += v

    full = lambda p: pl.BlockSpec(p.shape, lambda i, nd=p.ndim: (0,) * nd)
    res = pl.pallas_call(
        body,
        name=name,
        grid=(S // tm,),
        in_specs=[pl.BlockSpec((tm, t.shape[1]), lambda i: (i, 0)) for t in tiled] + [full(p) for p in params]
        + [pl.BlockSpec(memory_space=pl.ANY)] * n_after,
        out_specs=[pl.BlockSpec((tm, c), lambda i: (i, 0)) for c, _ in outs]
        + [pl.BlockSpec(s, lambda i, nd=len(s): (0,) * nd) for s in accs],
        out_shape=[jax.ShapeDtypeStruct((S, c), d) for c, d in outs] + [jax.ShapeDtypeStruct(s, F32) for s in accs],
        compiler_params=_cparams("arbitrary"),
    )(*_in_hbm(*tiled, *params), *after)
    return res


def _ln_stats(z):
    mu = jnp.mean(z, axis=-1, keepdims=True)
    zc = z - mu
    var = jnp.mean(zc * zc, axis=-1, keepdims=True)
    return zc, lax.rsqrt(var + LN_EPS)


def _ln_apply(z, g, b):
    zc, rstd = _ln_stats(z)
    y = zc * rstd * g + b
    return z, y, y


def _ln_grad(dy, z, g):
    zc, rstd = _ln_stats(z)
    xhat = zc * rstd
    dxh = dy * g
    m1 = jnp.mean(dxh, axis=-1, keepdims=True)
    m2 = jnp.mean(dxh * xhat, axis=-1, keepdims=True)
    dz = rstd * (dxh - m1 - xhat * m2)
    return dz, dz, jnp.sum(dy * xhat, axis=0, keepdims=True), jnp.sum(dy, axis=0, keepdims=True)


def _ple_grad(dx3, gpre, pp):
    gate = jax.nn.sigmoid(gpre)
    dgpre = dx3 * pp * gate * (1.0 - gate)
    return dx3 * gate, dgpre, jnp.sum(dgpre, axis=0, keepdims=True)


def _loss_and_grad(name, y, target, gpre, pp):
    d = y.shape[1]

    def fn(y, t, gpre, pp):
        err = y - t
        sq = jnp.sum(jnp.sum(err * err, axis=0, keepdims=True), axis=1, keepdims=True)
        dy = err * (1.0 / d)
        dpp, dgpre, dbg = _ple_grad(dy, gpre, pp)
        return dy, dpp, dgpre, jnp.broadcast_to(sq, (1, LANES)), dbg

    return _rowwise(name, fn, [y, target, gpre, pp], [], [(d, F32), (d, BF16), (d, BF16)], accs=[(1, LANES), (1, d)])


def _rows(shape):
    return lax.broadcasted_iota(jnp.int32, shape, 0)


def _gelu(y):
    t = jnp.tanh(GELU_C0 * (y + GELU_C1 * y * y * y))
    return 0.5 * y * (1.0 + t), t


def _gelu_grad(y, t):
    return 0.5 * (1.0 + t) + 0.5 * y * (1.0 - t * t) * GELU_C0 * (1.0 + 3.0 * GELU_C1 * y * y)


def _softplus(x):
    return jnp.maximum(x, 0.0) + jnp.log(1.0 + jnp.exp(-jnp.abs(x)))


def _conv_fwd(xs, cw, cb):
    n = xs.shape[0]
    u = cw[3:4] * xs
    for k in (1, 2, 3):
        u = u + cw[3 - k:4 - k] * pltpu.roll(xs, k, 0)
    del n
    return u[SUBLANES:] + cb


def _lru_gates(u, wa, wx, ba, bx, sp, grow):
    ub = u.astype(BF16)
    r = jax.nn.sigmoid(jnp.dot(ub, wa, preferred_element_type=F32) + ba)
    ig = jax.nn.sigmoid(jnp.dot(ub, wx, preferred_element_type=F32) + bx)
    log_a = (-LRU_C) * r * sp
    a = jnp.exp(log_a)
    mult = jnp.sqrt(jnp.tanh(-log_a) * (1.0 + a * a))
    mult = jnp.where(grow == 0, 1.0, mult)
    return ub, r, ig, a, mult


def _scan8_fwd(a, b):
    row = _rows(a.shape)
    for k in (1, 2, 4):
        m = row >= k
        b = jnp.where(m, a * pltpu.roll(b, k, 0) + b, b)
        a = jnp.where(m, a * pltpu.roll(a, k, 0), a)
    return a, b


def _scan8_bwd(c, d):
    row = _rows(c.shape)
    for k in (1, 2, 4):
        m = row < SUBLANES - k
        d = jnp.where(m, c * pltpu.roll(d, SUBLANES - k, 0) + d, d)
        c = jnp.where(m, c * pltpu.roll(c, SUBLANES - k, 0), c)
    return c, d


def _pad_copy(dst, src, front, back):
    s, c = src.shape
    if front:
        dst[pl.ds(0, front), :] = jnp.zeros((front, c), dst.dtype)
    if back:
        dst[pl.ds(front + s, back), :] = jnp.zeros((back, c), dst.dtype)
    dst[pl.ds(front, s), :] = src[...].astype(dst.dtype)


def _lru_fwd(name, proj, par, wa, wx):
    S = proj.shape[0]
    R = proj.shape[1] // 2
    H = R // LANES
    ch = _tile(S, SEQ_CHUNK)
    nch = S // ch
    H8 = SUBLANES

    def body(up_ref, y_ref, par_ref, wa_ref, wx_ref, gh_ref, h_ref, up_pad):
        _pad_copy(up_pad, up_ref, H8, 0)
        par = par_ref[...]
        cw, cb, ba, bx = par[0:4], par[4:5], par[5:6], par[6:7]
        sp = _softplus(-par[7:8])
        wa_m, wx_m = wa_ref[...], wx_ref[...]

        def chunk(ci, carry):
            r0 = pl.multiple_of(ci * ch, ch)
            xs = up_pad[pl.ds(r0, ch + H8), :]
            u = _conv_fwd(xs, cw, cb)
            grow = _rows(u.shape) + r0
            _, _, ig, a, mult = _lru_gates(u, wa_m, wx_m, ba, bx, sp, grow)
            bt = mult * (ig * u)
            hs = []
            for j in range(ch // H8):
                aa, bb = _scan8_fwd(a[j * H8:(j + 1) * H8], bt[j * H8:(j + 1) * H8])
                hj = bb + aa * carry
                carry = hj[H8 - 1:H8]
                hs.append(hj)
            h = jnp.concatenate(hs, axis=0)
            h_ref[pl.ds(r0, ch), :] = h
            gy, _ = _gelu(y_ref[pl.ds(r0, ch), :])
            gh_ref[pl.ds(r0, ch), :] = (h * gy).astype(gh_ref.dtype)
            return carry

        lax.fori_loop(0, nch, chunk, jnp.zeros((1, LANES), F32))

    col = lambda off: pl.BlockSpec((S, LANES), lambda h: (0, h + off))
    return pl.pallas_call(
        body,
        name=name,
        grid=(H,),
        in_specs=[col(0), col(H), pl.BlockSpec((8, LANES), lambda h: (0, h)),
                  pl.BlockSpec((None, LANES, LANES), lambda h: (h, 0, 0)),
                  pl.BlockSpec((None, LANES, LANES), lambda h: (h, 0, 0))],
        out_specs=[col(0), col(0)],
        out_shape=[jax.ShapeDtypeStruct((S, R), BF16), jax.ShapeDtypeStruct((S, R), F32)],
        scratch_shapes=[pltpu.VMEM((S + H8, LANES), F32)],
        compiler_params=_cparams("parallel"),
    )(proj, proj, par, wa, wx)


def _lru_bwd(name, proj, h, dgh, par, wa, wx):
    S = proj.shape[0]
    R = proj.shape[1] // 2
    H = R // LANES
    ch = _tile(S, SEQ_CHUNK)
    nch = S // ch
    H8 = SUBLANES
    nb = ch // H8

    def body(up_ref, y_ref, h_ref, dgh_ref, par_ref, wa_ref, wx_ref,
             dup_ref, dy_ref, dwa_ref, dwx_ref, dpar_ref, up_pad, h_pad, du_pad, vec_acc):
        _pad_copy(up_pad, up_ref, H8, 0)
        _pad_copy(h_pad, h_ref, H8, 0)
        du_pad[pl.ds(S, H8), :] = jnp.zeros((H8, LANES), F32)
        par = par_ref[...]
        cw, cb, ba, bx, lam = par[0:4], par[4:5], par[5:6], par[6:7], par[7:8]
        sp = _softplus(-lam)
        wa_m, wx_m = wa_ref[...], wx_ref[...]
        dwa_ref[...] = jnp.zeros_like(dwa_ref)
        dwx_ref[...] = jnp.zeros_like(dwx_ref)
        vec_acc[...] = jnp.zeros_like(vec_acc)
        nt_dims = (((1,), (1,)), ((), ()))
        tn_dims = (((0,), (0,)), ((), ()))

        def chunk(it, carry):
            lam_next, a_next = carry
            ci = nch - 1 - it
            r0 = pl.multiple_of(ci * ch, ch)
            xs = up_pad[pl.ds(r0, ch + H8), :]
            u = _conv_fwd(xs, cw, cb)
            row = _rows(u.shape)
            grow = row + r0
            ub, r, ig, a, mult = _lru_gates(u, wa_m, wx_m, ba, bx, sp, grow)
            hs = h_pad[pl.ds(r0, ch + H8), :]
            hcur = hs[H8:]
            hprev = pltpu.roll(hs, 1, 0)[H8:]
            y = y_ref[pl.ds(r0, ch), :]
            dgh = dgh_ref[pl.ds(r0, ch), :]
            gy, t = _gelu(y)
            dy_ref[pl.ds(r0, ch), :] = (dgh * hcur * _gelu_grad(y, t)).astype(dy_ref.dtype)
            dh = dgh * gy
            c = jnp.where(row == ch - 1, a_next, pltpu.roll(a, ch - 1, 0))
            ls = [None] * nb
            for j in range(nb - 1, -1, -1):
                cc, dd = _scan8_bwd(c[j * H8:(j + 1) * H8], dh[j * H8:(j + 1) * H8])
                lj = dd + cc * lam_next
                lam_next = lj[0:1]
                ls[j] = lj
            lmb = jnp.concatenate(ls, axis=0)
            da = lmb * hprev
            gu = ig * u
            dmult = lmb * gu
            dlog_a = da * a + jnp.where(grow == 0, 0.0, dmult * (-(a * a) / mult))
            dr = dlog_a * ((-LRU_C) * sp)
            drp = dr * r * (1.0 - r)
            dip = (lmb * mult * u) * ig * (1.0 - ig)
            drb, dib = drp.astype(BF16), dip.astype(BF16)
            du = (lmb * mult * ig
                  + lax.dot_general(drb, wa_m, nt_dims, preferred_element_type=F32)
                  + lax.dot_general(dib, wx_m, nt_dims, preferred_element_type=F32))
            du_pad[pl.ds(r0, ch), :] = du
            dwa_ref[...] += lax.dot_general(ub, drb, tn_dims, preferred_element_type=F32)
            dwx_ref[...] += lax.dot_general(ub, dib, tn_dims, preferred_element_type=F32)
            ssum = lambda v: jnp.sum(v, axis=0, keepdims=True)
            vec_acc[0:1, :] += ssum(drp)
            vec_acc[1:2, :] += ssum(dip)
            vec_acc[2:3, :] += ssum(dlog_a * ((-LRU_C) * r))
            return lam_next, a[0:1]

        zero = jnp.zeros((1, LANES), F32)
        lax.fori_loop(0, nch, chunk, (zero, zero))

        def conv_chunk(ci, acc):
            r0 = pl.multiple_of(ci * ch, ch)
            ds = du_pad[pl.ds(r0, ch + H8), :]
            xs = up_pad[pl.ds(r0, ch + H8), :]
            n = ch + H8
            du = ds[:ch]
            dup = cw[3:4] * du
            new = [acc[3] + jnp.sum(du * xs[H8:], axis=0, keepdims=True)]
            for k in (1, 2, 3):
                dup = dup + cw[3 - k:4 - k] * pltpu.roll(ds, n - k, 0)[:ch]
                new.append(acc[3 - k] + jnp.sum(du * pltpu.roll(xs, k, 0)[H8:], axis=0, keepdims=True))
            dup_ref[pl.ds(r0, ch), :] = dup.astype(dup_ref.dtype)
            return (new[3], new[2], new[1], new[0], acc[4] + jnp.sum(du, axis=0, keepdims=True))

        acc = lax.fori_loop(0, nch, conv_chunk, (zero,) * 5)
        dlam = vec_acc[2:3, :] * (-jax.nn.sigmoid(-lam))
        dpar_ref[...] = jnp.concatenate(list(acc) + [vec_acc[0:1, :], vec_acc[1:2, :], dlam], axis=0)

    col = lambda off: pl.BlockSpec((S, LANES), lambda h: (0, h + off))
    head = pl.BlockSpec((None, LANES, LANES), lambda h: (h, 0, 0))
    return pl.pallas_call(
        body,
        name=name,
        grid=(H,),
        in_specs=[col(0), col(H), col(0), col(0), pl.BlockSpec((8, LANES), lambda h: (0, h)), head, head],
        out_specs=[col(0), col(0), head, head, pl.BlockSpec((8, LANES), lambda h: (0, h))],
        out_shape=[jax.ShapeDtypeStruct((S, R), BF16), jax.ShapeDtypeStruct((S, R), BF16),
                   jax.ShapeDtypeStruct((H, LANES, LANES), F32), jax.ShapeDtypeStruct((H, LANES, LANES), F32),
                   jax.ShapeDtypeStruct((8, R), F32)],
        scratch_shapes=[pltpu.VMEM((S + H8, LANES), F32), pltpu.VMEM((S + H8, LANES), F32),
                        pltpu.VMEM((S + H8, LANES), F32), pltpu.VMEM((8, LANES), F32)],
        compiler_params=_cparams("parallel"),
    )(proj, proj, h, dgh, par, wa, wx)


def _window_sum(xs, g, up):
    n = xs.shape[0]
    s = xs
    for lvl, k in enumerate((1, 2, 4, 8)):
        sh = pltpu.roll(s, (n - k) if up else k, 0)
        s = s + jnp.where(g >= lvl, sh, 0.0)
    return s


def _pool_count(grow, g):
    return jnp.minimum(grow + 1, lax.shift_left(jnp.int32(2), g)).astype(F32)


def _pool_fwd(name, u, wgrp, par):
    S, D = u.shape
    G, W = wgrp.shape[0], wgrp.shape[1]
    ch = _tile(S, SEQ_CHUNK)
    nch = S // ch
    PH = POOL_HALO

    def body(u_ref, w_ref, par_ref, zs_ref, u_pad):
        g = pl.program_id(0)
        _pad_copy(u_pad, u_ref, PH, 0)
        par = par_ref[...]
        w = w_ref[...]

        def chunk(ci, _):
            r0 = pl.multiple_of(ci * ch, ch)
            xs = u_pad[pl.ds(r0, ch + PH), :]
            ws = _window_sum(xs, g, False)[PH:]
            uc = xs[PH:]
            cnt = _pool_count(_rows(uc.shape) + r0, g)
            pooled = ws / cnt - uc
            z = jnp.dot(pooled.astype(BF16), w, preferred_element_type=F32) + par[0:1]
            zs_ref[pl.ds(r0, ch), :] = (z * par[1:2]).astype(zs_ref.dtype)
            return 0

        lax.fori_loop(0, nch, chunk, 0)

    return pl.pallas_call(
        body,
        name=name,
        grid=(G,),
        in_specs=[pl.BlockSpec((S, W), lambda g: (0, g)), pl.BlockSpec((None, W, W), lambda g: (g, 0, 0)),
                  pl.BlockSpec((2, W), lambda g: (0, g))],
        out_specs=pl.BlockSpec((S, W), lambda g: (0, g)),
        out_shape=jax.ShapeDtypeStruct((S, D), BF16),
        scratch_shapes=[pltpu.VMEM((S + PH, W), F32)],
        compiler_params=_cparams("parallel"),
    )(u, wgrp, par)


def _pool_bwd(name, u, dzs, wgrp, par):
    S, D = u.shape
    G, W = wgrp.shape[0], wgrp.shape[1]
    ch = _tile(S, SEQ_CHUNK)
    nch = S // ch
    PH = POOL_HALO

    def body(u_ref, dzs_ref, w_ref, par_ref, du_ref, dw_ref, dpar_ref, u_pad, q_pad, dw_acc):
        g = pl.program_id(0)
        _pad_copy(u_pad, u_ref, PH, 0)
        q_pad[pl.ds(S, PH), :] = jnp.zeros((PH, W), F32)
        par = par_ref[...]
        w = w_ref[...]
        dw_acc[...] = jnp.zeros_like(dw_acc)

        def chunk(ci, acc):
            db, dsc = acc
            r0 = pl.multiple_of(ci * ch, ch)
            xs = u_pad[pl.ds(r0, ch + PH), :]
            ws = _window_sum(xs, g, False)[PH:]
            uc = xs[PH:]
            cnt = _pool_count(_rows(uc.shape) + r0, g)
            pooled = (ws / cnt - uc).astype(BF16)
            z = jnp.dot(pooled, w, preferred_element_type=F32) + par[0:1]
            dzs = dzs_ref[pl.ds(r0, ch), :]
            dz = dzs * par[1:2]
            dzb = dz.astype(BF16)
            dw_acc[...] += lax.dot_general(pooled, dzb, (((0,), (0,)), ((), ())), preferred_element_type=F32)
            dpooled = lax.dot_general(dzb, w, (((1,), (1,)), ((), ())), preferred_element_type=F32)
            q_pad[pl.ds(r0, ch), :] = dpooled / cnt
            return (db + jnp.sum(dz, axis=0, keepdims=True), dsc + jnp.sum(dzs * z, axis=0, keepdims=True))

        zero = jnp.zeros((1, W), F32)
        db, dsc = lax.fori_loop(0, nch, chunk, (zero, zero))
        dpar_ref[...] = jnp.concatenate([db, dsc], axis=0)
        dw_ref[...] = dw_acc[...].astype(dw_ref.dtype)

        def back(ci, _):
            r0 = pl.multiple_of(ci * ch, ch)
            qs = q_pad[pl.ds(r0, ch + PH), :]
            ws = _window_sum(qs, g, True)[:ch]
            qc = qs[:ch]
            cnt = _pool_count(_rows(qc.shape) + r0, g)
            du_ref[pl.ds(r0, ch), :] = (ws - qc * cnt).astype(du_ref.dtype)
            return 0

        lax.fori_loop(0, nch, back, 0)

    blk = pl.BlockSpec((S, W), lambda g: (0, g))
    wspec = pl.BlockSpec((None, W, W), lambda g: (g, 0, 0))
    pspec = pl.BlockSpec((2, W), lambda g: (0, g))
    return pl.pallas_call(
        body,
        name=name,
        grid=(G,),
        in_specs=[blk, blk, wspec, pspec],
        out_specs=[blk, wspec, pspec],
        out_shape=[jax.ShapeDtypeStruct((S, D), BF16), jax.ShapeDtypeStruct((G, W, W), BF16),
                   jax.ShapeDtypeStruct((2, D), F32)],
        scratch_shapes=[pltpu.VMEM((S + PH, W), F32), pltpu.VMEM((S + PH, W), F32), pltpu.VMEM((W, W), F32)],
        compiler_params=_cparams("parallel"),
    )(u, dzs, wgrp, par)


def _my_place():
    x, y, c = lax.axis_index("x"), lax.axis_index("y"), lax.axis_index("c")
    return x, y, c, 4 * x + 2 * y + c


def _peers(x, y, c):
    out = []
    for d in range(1, N_DEV):
        px = 1 - x if d & 4 else x
        py = 1 - y if d & 2 else y
        pc = 1 - c if d & 1 else c
        out.append(((px, py, pc), 4 * px + 2 * py + pc))
    return out


def _window(ref, axis, start, size):
    idx = [slice(None)] * len(ref.shape)
    idx[axis] = pl.ds(start, size)
    return ref.at[tuple(idx)]


def _to_bf16(name, arrs, after=()):
    outs = []
    for i, a in enumerate(arrs):
        a2 = a.reshape(-1, a.shape[-1])
        tr = _tile(a2.shape[0], 512)
        o = pl.pallas_call(
            lambda a_ref, *rest: rest[-1].__setitem__(Ellipsis, a_ref[...].astype(BF16)),
            name=f"{name}_{i}",
            grid=(a2.shape[0] // tr,),
            in_specs=[pl.BlockSpec((tr, a2.shape[1]), lambda r: (r, 0))] + [pl.BlockSpec(memory_space=pl.ANY)] * len(after),
            out_specs=pl.BlockSpec((tr, a2.shape[1]), lambda r: (r, 0)),
            out_shape=jax.ShapeDtypeStruct(a2.shape, BF16),
            compiler_params=_cparams("parallel"),
        )(a2, *after)
        outs.append(o.reshape(a.shape))
    return outs


def _all_gather(name, shards, axes):
    n = len(shards)
    sizes = [s.shape[ax] for s, ax in zip(shards, axes)]

    def body(*refs):
        ins, outs = refs[:n], refs[n:2 * n]
        send, recv, loc = refs[2 * n:]
        x, y, c, me = _my_place()
        peers = _peers(x, y, c)
        local = []
        for i in range(n):
            dst = _window(outs[i], axes[i], me * sizes[i], sizes[i])
            cp = pltpu.make_async_copy(ins[i], dst, loc.at[i])
            cp.start()
            local.append(cp)
            for peer, _ in peers:
                pltpu.make_async_remote_copy(src_ref=ins[i], dst_ref=dst, send_sem=send.at[i], recv_sem=recv.at[i],
                                             device_id=peer, device_id_type=MESH).start()
        for i in range(n):
            local[i].wait()
            seven = _window(outs[i], axes[i], 0, (N_DEV - 1) * sizes[i])
            pltpu.make_async_remote_copy(src_ref=seven, dst_ref=seven, send_sem=send.at[i], recv_sem=recv.at[i],
                                         device_id=(x, y, c), device_id_type=MESH).wait()

    def full_shape(s, ax):
        shp = list(s.shape)
        shp[ax] *= N_DEV
        return jax.ShapeDtypeStruct(tuple(shp), s.dtype)

    any_spec = pl.BlockSpec(memory_space=pl.ANY)
    return pl.pallas_call(
        body,
        name=name,
        in_specs=[any_spec] * n,
        out_specs=[any_spec] * n,
        out_shape=[full_shape(s, ax) for s, ax in zip(shards, axes)],
        scratch_shapes=[pltpu.SemaphoreType.DMA((n,)), pltpu.SemaphoreType.DMA((n,)), pltpu.SemaphoreType.DMA((n,))],
        compiler_params=pltpu.CompilerParams(has_side_effects=True),
    )(*shards)


HBM_SPEC = pl.BlockSpec(memory_space=pltpu.HBM)
SEM_SPEC = pl.BlockSpec(memory_space=pltpu.SEMAPHORE)
SPLIT_EFFECT = pltpu.SideEffectType.DATAFLOW_SIDE_EFFECTING


def _push_all(kind, src, dst, axis, size, send_sem, recv_sem, place):
    x, y, c, me = place
    for peer, pidx in _peers(x, y, c):
        if kind == "gather":
            s = d = _window(dst, axis, me * size, size)
        else:
            s, d = _window(src, axis, pidx * size, size), dst.at[me]
        pltpu.make_async_remote_copy(src_ref=s, dst_ref=d, send_sem=send_sem, recv_sem=recv_sem, device_id=peer,
                                     device_id_type=MESH).start()


def _drain_all(kind, dst, axis, size, send_sem, recv_sem, place):
    x, y, c, _ = place
    seven = _window(dst, axis, 0, (N_DEV - 1) * size) if kind == "gather" else dst.at[pl.ds(0, N_DEV - 1)]
    pltpu.make_async_remote_copy(src_ref=seven, dst_ref=seven, send_sem=send_sem, recv_sem=recv_sem,
                                 device_id=(x, y, c), device_id_type=MESH).wait()


def _own_block_placed(src, axis, size, me):
    own = lax.dynamic_slice_in_dim(src, me * size, size, axis)
    return lax.dynamic_update_slice_in_dim(lax.empty((N_DEV,) + own.shape, src.dtype), own[None], me, 0)


def _split_start(name, kind, srcs, lands, axes, sizes, after=()):
    n, ns, na = len(lands), len(srcs), len(after)

    def body(*refs):
        src_refs, land_refs = refs[:ns], refs[ns:ns + n]
        send, recv = refs[ns + n + na], refs[ns + n + na + 1]
        token = refs[-1]
        place = _my_place()
        for k in range(n):
            _push_all(kind, src_refs[k] if ns else None, land_refs[k], axes[k], sizes[k], send.at[k], recv.at[k], place)
        token[...] = jnp.zeros_like(token)

    hbm = lambda a: pltpu.HBM(a.shape, a.dtype)
    res = pl.pallas_call(
        body,
        name=name,
        out_shape=(pltpu.SemaphoreType.DMA((n,)), pltpu.SemaphoreType.DMA((n,)), *[hbm(a) for a in srcs],
                   *[hbm(a) for a in lands], jax.ShapeDtypeStruct((SUBLANES, LANES), F32)),
        in_specs=[HBM_SPEC] * (ns + n) + [pl.BlockSpec(memory_space=pl.ANY)] * na,
        out_specs=(SEM_SPEC, SEM_SPEC, *[HBM_SPEC] * (ns + n), pl.BlockSpec(memory_space=pltpu.VMEM)),
        input_output_aliases={k: 2 + k for k in range(ns + n)},
        compiler_params=pltpu.CompilerParams(has_side_effects=SPLIT_EFFECT),
    )(*[pltpu.with_memory_space_constraint(a, pltpu.HBM) for a in (*srcs, *lands)], *after)
    return res[0], res[1], list(res[2:2 + ns]), list(res[2 + ns:2 + ns + n]), res[-1]


def _split_wait(name, kind, handle, axes, sizes, after):
    send, recv, srcs, lands, _ = handle
    n, ns = len(lands), len(srcs)
    after = list(after) if isinstance(after, (list, tuple)) else [after]

    def body(*refs):
        land_refs = refs[ns:ns + n]
        send_ref, recv_ref = refs[ns + n], refs[ns + n + 1]
        place = _my_place()
        for k in range(n):
            _drain_all(kind, land_refs[k], axes[k], sizes[k], send_ref.at[k], recv_ref.at[k], place)

    hbm = lambda a: pltpu.HBM(a.shape, a.dtype)
    res = pl.pallas_call(
        body,
        name=name,
        out_shape=tuple(hbm(a) for a in (*srcs, *lands)),
        in_specs=[HBM_SPEC] * (ns + n) + [SEM_SPEC, SEM_SPEC] + [pl.BlockSpec(memory_space=pl.ANY)] * len(after),
        out_specs=tuple([HBM_SPEC] * (ns + n)),
        input_output_aliases={k: k for k in range(ns + n)},
        compiler_params=pltpu.CompilerParams(has_side_effects=SPLIT_EFFECT),
    )(*srcs, *lands, send, recv, *after)
    return list(res[ns:])


def _cast_into_window(name, a, l, axis, me1, after=()):
    shp = a.shape[1:]
    cast = lambda me_ref, a_ref, *rest: rest[-1].__setitem__(Ellipsis, a_ref[...].astype(BF16))
    if len(shp) == 3:
        assert axis == 1
        G, r, c = shp
        full = (G, r * N_DEV, c)
        grid = (G,)
        in_spec = pl.BlockSpec((None, None, r, c), lambda g, me: (l, g, 0, 0))
        out_spec = pl.BlockSpec((None, r, c), lambda g, me: (g, me[0], 0))
    else:
        r, c = shp
        tr = _tile(r, 512)
        nb = r // tr
        grid = (nb,)
        in_spec = pl.BlockSpec((None, tr, c), lambda i, me: (l, i, 0))
        if axis == 0:
            full = (r * N_DEV, c)
            out_spec = pl.BlockSpec((tr, c), lambda i, me: (me[0] * nb + i, 0))
        else:
            full = (r, c * N_DEV)
            out_spec = pl.BlockSpec((tr, c), lambda i, me: (i, me[0]))
    return pl.pallas_call(
        cast,
        name=name,
        grid_spec=pltpu.PrefetchScalarGridSpec(
            num_scalar_prefetch=1, grid=grid,
            in_specs=[in_spec] + [pl.BlockSpec(memory_space=pl.ANY)] * len(after), out_specs=out_spec),
        out_shape=jax.ShapeDtypeStruct(full, BF16),
        compiler_params=_cparams("arbitrary"),
    )(me1, a, *after)


def _adamw_math(w, g, m, v):
    m = ADAM_B1 * m + (1.0 - ADAM_B1) * g
    v = ADAM_B2 * v + (1.0 - ADAM_B2) * jnp.square(g)
    m_hat = m / (1.0 - ADAM_B1 ** ADAM_STEP)
    v_hat = v / (1.0 - ADAM_B2 ** ADAM_STEP)
    delta = -ADAM_LR * (m_hat / (jnp.sqrt(v_hat) + ADAM_EPS) + ADAM_WD * w)
    return delta, m, v


def _sum_slots(buf_ref):
    g = buf_ref[0].astype(F32)
    for s in range(1, N_DEV):
        g = g + buf_ref[s].astype(F32)
    return g


def _adamw_layer(name, buf, w, m, v, l, prev, after=()):
    shape = w.shape
    L, C = shape[0], shape[-1]
    Rr = math.prod(shape[1:-1])
    buf3 = buf.reshape(N_DEV, Rr, C)
    w3, m3, v3 = (t.reshape(L, Rr, C) for t in (w, m, v))
    tr = _tile(Rr, 2 * LANES) if Rr % LANES == 0 else Rr
    n_prev = 0 if prev is None else 4

    def body(buf_ref, w_ref, m_ref, v_ref, *rest):
        g_out, d_out, m_out, v_out = rest[n_prev + len(after):]
        g = _sum_slots(buf_ref)
        d, mm, vv = _adamw_math(w_ref[...], g, m_ref[...], v_ref[...])
        g_out[...] = g
        d_out[...] = d
        m_out[...] = mm
        v_out[...] = vv

    spec = pl.BlockSpec((None, tr, C), lambda r: (l, r, 0))
    outs = pl.pallas_call(
        body,
        name=name,
        grid=(Rr // tr,),
        in_specs=[pl.BlockSpec((N_DEV, tr, C), lambda r: (0, r, 0)), spec, spec, spec]
        + [pl.BlockSpec(memory_space=pl.ANY)] * (n_prev + len(after)),
        out_specs=[spec] * 4,
        out_shape=[jax.ShapeDtypeStruct((L, Rr, C), F32)] * 4,
        input_output_aliases={4 + k: k for k in range(n_prev)},
        compiler_params=_cparams("parallel"),
    )(buf3, w3, m3, v3, *(prev or ()), *after)
    return list(outs)


def _sum8(name, buf):
    R = buf.shape[1]

    def body(buf_ref, o_ref):
        o_ref[...] = _sum_slots(buf_ref)

    return pl.pallas_call(
        body,
        name=name,
        in_specs=[pl.BlockSpec(buf.shape, lambda: (0, 0, 0))],
        out_specs=pl.BlockSpec((R, LANES), lambda: (0, 0)),
        out_shape=jax.ShapeDtypeStruct((R, LANES), F32),
        compiler_params=_cparams(),
    )(buf)


def _adamw_small(name, w, g, m, v):
    shape = w.shape
    w2, g2, m2, v2 = (t.reshape(-1, shape[-1]) for t in (w, g, m, v))
    R, C = w2.shape
    tr = _row_tile(R, 512)

    def body(w_ref, g_ref, m_ref, v_ref, d_out, m_out, v_out):
        d, mm, vv = _adamw_math(w_ref[...], g_ref[...], m_ref[...], v_ref[...])
        d_out[...] = d
        m_out[...] = mm
        v_out[...] = vv

    spec = pl.BlockSpec((tr, C), lambda r: (r, 0))
    outs = pl.pallas_call(
        body,
        name=name,
        grid=(R // tr,),
        in_specs=[spec] * 4,
        out_specs=[spec] * 3,
        out_shape=[jax.ShapeDtypeStruct((R, C), F32)] * 3,
        compiler_params=_cparams("parallel"),
    )(w2, g2, m2, v2)
    return [o.reshape(shape) for o in outs]


def _pack(arrs, pad_rows_to=SUBLANES):
    parts = []
    for a in arrs:
        flat = a.reshape(-1)
        per = LANES * pad_rows_to
        padded = -(-flat.shape[0] // per) * per
        if padded != flat.shape[0]:
            flat = jnp.pad(flat, (0, padded - flat.shape[0]))
        parts.append(flat.reshape(-1, LANES))
    return jnp.concatenate(parts, axis=0)


def _unpack(packed, shapes, pad_rows_to=SUBLANES):
    out = []
    r = 0
    for shp in shapes:
        nel = math.prod(shp)
        per = LANES * pad_rows_to
        rows = -(-nel // per) * pad_rows_to
        out.append(packed[r:r + rows].reshape(-1)[:nel].reshape(shp))
        r += rows
    return out


BIG = ("lru_w_in", "lru_w_out", "pool_w_in", "pool_w_grp", "pool_w_out", "mlp_w1", "mlp_w2", "ple_w", "ple_gate_w")
BIG_AXIS = {"lru_w_in": 2, "lru_w_out": 1, "pool_w_in": 1, "pool_w_grp": 2, "pool_w_out": 1, "mlp_w1": 2,
            "mlp_w2": 1, "ple_w": 2, "ple_gate_w": 1}
SMALL_SHARDED = ("lru_conv_w", "pool_b_grp", "pool_scale")
REPLICATED = ("lru_conv_b", "lru_wa", "lru_ba", "lru_wx", "lru_bx", "lru_lambda", "ln_mix_g", "ln_mix_b",
              "ln_mlp_g", "ln_mlp_b", "ple_gate_b")
WEIGHTS = ("lru_w_in", "lru_conv_w", "lru_conv_b", "lru_wa", "lru_ba", "lru_wx", "lru_bx", "lru_lambda", "lru_w_out",
           "pool_w_in", "pool_w_grp", "pool_b_grp", "pool_scale", "pool_w_out", "ln_mix_g", "ln_mix_b", "mlp_w1",
           "mlp_w2", "ln_mlp_g", "ln_mlp_b", "ple_w", "ple_gate_w", "ple_gate_b")
INPUTS = ("x", "p") + WEIGHTS + ("loss_target",) + tuple("m_" + n for n in WEIGHTS) + tuple("v_" + n for n in WEIGHTS)


def _gather_last_axis(packed_full, shard_shape):
    nel = math.prod(shard_shape)
    blocks = packed_full.reshape(N_DEV, -1)[:, :nel].reshape((N_DEV,) + tuple(shard_shape))
    return jnp.concatenate([blocks[d] for d in range(N_DEV)], axis=-1)


def kernel(x, p, lru_w_in, lru_conv_w, lru_conv_b, lru_wa, lru_ba, lru_wx, lru_bx, lru_lambda, lru_w_out, pool_w_in, pool_w_grp, pool_b_grp, pool_scale, pool_w_out, ln_mix_g, ln_mix_b, mlp_w1, mlp_w2, ln_mlp_g, ln_mlp_b, ple_w, ple_gate_w, ple_gate_b, loss_target, m_lru_w_in, m_lru_conv_w, m_lru_conv_b, m_lru_wa, m_lru_ba, m_lru_wx, m_lru_bx, m_lru_lambda, m_lru_w_out, m_pool_w_in, m_pool_w_grp, m_pool_b_grp, m_pool_scale, m_pool_w_out, m_ln_mix_g, m_ln_mix_b, m_mlp_w1, m_mlp_w2, m_ln_mlp_g, m_ln_mlp_b, m_ple_w, m_ple_gate_w, m_ple_gate_b, v_lru_w_in, v_lru_conv_w, v_lru_conv_b, v_lru_wa, v_lru_ba, v_lru_wx, v_lru_bx, v_lru_lambda, v_lru_w_out, v_pool_w_in, v_pool_w_grp, v_pool_b_grp, v_pool_scale, v_pool_w_out, v_ln_mix_g, v_ln_mix_b, v_mlp_w1, v_mlp_w2, v_ln_mlp_g, v_ln_mlp_b, v_ple_w, v_ple_gate_w, v_ple_gate_b):
    A = dict(zip(INPUTS, (x, p, lru_w_in, lru_conv_w, lru_conv_b, lru_wa, lru_ba, lru_wx, lru_bx, lru_lambda, lru_w_out, pool_w_in, pool_w_grp, pool_b_grp, pool_scale, pool_w_out, ln_mix_g, ln_mix_b, mlp_w1, mlp_w2, ln_mlp_g, ln_mlp_b, ple_w, ple_gate_w, ple_gate_b, loss_target, m_lru_w_in, m_lru_conv_w, m_lru_conv_b, m_lru_wa, m_lru_ba, m_lru_wx, m_lru_bx, m_lru_lambda, m_lru_w_out, m_pool_w_in, m_pool_w_grp, m_pool_b_grp, m_pool_scale, m_pool_w_out, m_ln_mix_g, m_ln_mix_b, m_mlp_w1, m_mlp_w2, m_ln_mlp_g, m_ln_mlp_b, m_ple_w, m_ple_gate_w, m_ple_gate_b, v_lru_w_in, v_lru_conv_w, v_lru_conv_b, v_lru_wa, v_lru_ba, v_lru_wx, v_lru_bx, v_lru_lambda, v_lru_w_out, v_pool_w_in, v_pool_w_grp, v_pool_b_grp, v_pool_scale, v_pool_w_out, v_ln_mix_g, v_ln_mix_b, v_mlp_w1, v_mlp_w2, v_ln_mlp_g, v_ln_mlp_b, v_ple_w, v_ple_gate_w, v_ple_gate_b)))
    depth = ln_mix_g.shape[0]
    alpha = (2 * depth) ** 0.25
    S, D = x.shape[1], x.shape[2]
    xs = x.reshape(S, D)
    tgt = loss_target.reshape(S, D)
    p3 = p.reshape(depth, S, p.shape[-1])
    me = 4 * lax.axis_index("x") + 2 * lax.axis_index("y") + lax.axis_index("c")

    def layer_weights(i):
        s = i // 2
        mixer = ("lru_w_in", "lru_w_out") if i % 2 == 0 else ("pool_w_in", "pool_w_grp", "pool_w_out")
        return [(n, s) for n in mixer] + [(n, i) for n in ("mlp_w1", "mlp_w2", "ple_w", "ple_gate_w")]

    def axis_of(key):
        return 0 if key[0] == "small" else BIG_AXIS[key[0]] - 1

    def start_gather(tag, keys, after):
        axes = [axis_of(k) for k in keys]
        lands = [land[k] for k in keys]
        sizes = [a.shape[ax] // N_DEV for a, ax in zip(lands, axes)]
        return keys, _split_start(f"gather_{tag}_start", "gather", [], lands, axes, sizes, after=after), axes, sizes

    def finish_gather(tag, pending, after):
        keys, handle, axes, sizes = pending
        for (n, l), full in zip(keys, _split_wait(f"gather_{tag}_wait", "gather", handle, axes, sizes, after)):
            W[n][l] = full

    def start_exchange(tag, keys, arrs, after):
        axes = [axis_of(k) for k in keys]
        sizes = [a.shape[ax] // N_DEV for a, ax in zip(arrs, axes)]
        lands = [_own_block_placed(a, ax, sz, me) for a, ax, sz in zip(arrs, axes, sizes)]
        return keys, _split_start(f"exchange_{tag}_start", "scatter", arrs, lands, axes, sizes, after=after), axes, sizes

    def finish_exchange(tag, pending, after):
        keys, handle, axes, sizes = pending
        partial.update(zip(keys, _split_wait(f"exchange_{tag}_wait", "scatter", handle, axes, sizes, after)))

    me1 = jnp.reshape(me, (1,)).astype(jnp.int32)
    land = {}
    W = {n: [None] * A[n].shape[0] for n in BIG}
    small_shard_shapes = [A[n].shape for n in SMALL_SHARDED]
    gathered = _all_gather("gather_small_params", [_pack([A[n] for n in SMALL_SHARDED])], [0])
    def gather_groups(i):
        keys = layer_weights(i)
        mixer, (w1, w2, pw, pg) = keys[:-4], keys[-4:]
        if i % 2 == 0:
            return [("in", mixer[:1]), ("out", mixer[1:]), ("up", [w1]), ("rest", [w2, pw, pg])]
        return [("in", mixer + [w1]), ("rest", [w2, pw, pg])]

    gather_pending = {}

    def send_layer(i, behind):
        for tag, keys in gather_groups(i):
            for k in keys:
                land[k] = _cast_into_window(f"cast_{k[0]}_{k[1]}", A[k[0]], k[1], axis_of(k), me1, after=behind)
            gather_pending[(i, tag)] = start_gather(f"l{i}_{tag}", keys, behind)
            behind = (gather_pending[(i, tag)][1][4],)
        return behind

    layer0_started = send_layer(0, (gathered[0],))
    small_full = gathered[0].reshape(N_DEV, -1, LANES)
    r = 0
    for n, shp in zip(SMALL_SHARDED, small_shard_shapes):
        rows = -(-math.prod(shp) // (LANES * SUBLANES)) * SUBLANES
        W[n] = _gather_last_axis(small_full[:, r:r + rows], shp)
        r += rows
    wa_b, wx_b = _to_bf16("cast_gates", [lru_wa, lru_wx], after=layer0_started)
    n_lru = lru_w_in.shape[0]
    lru_par = [jnp.concatenate([W["lru_conv_w"][s], lru_conv_b[s][None], lru_ba[s][None], lru_bx[s][None],
                                lru_lambda[s][None]], axis=0) for s in range(n_lru)]
    pool_par = [jnp.stack([W["pool_b_grp"][s], W["pool_scale"][s]], axis=0) for s in range(pool_w_in.shape[0])]

    saved = []
    h_in = xs
    (h_in_b,) = _to_bf16("cast_inputs", [xs], after=layer0_started)
    layer1_started = send_layer(1, (layer0_started[0], h_in_b))
    finish_gather("l0_in", gather_pending[(0, "in")], [h_in_b, wa_b, layer1_started[0]])
    for i in range(depth):
        s = i // 2
        sv = {"x0b": h_in_b}
        if i > 0:
            finish_gather(f"l{i}_in", gather_pending[(i, "in")], h_in)
        ln_out = dict(out_dtypes=[F32, F32, BF16], tm=MM_TM_ROWS, tn=D,
                      epi=lambda acc, xp, g, b: _ln_apply(alpha * xp + acc, g, b))
        if i % 2 == 0:
            sv["proj"] = _mm(f"l{i}_lru_in", h_in_b, W["lru_w_in"], "nn", [F32], b_lead=s)
            if i == 0:
                behind = (layer1_started[0], sv["proj"])
                for later in range(2, depth):
                    behind = send_layer(later, behind)
                all_started = behind[0]
            sv["gh"], sv["h"] = _lru_fwd(f"l{i}_lru_core", sv["proj"], lru_par[s], wa_b[s], wx_b[s])
            finish_gather(f"l{i}_out", gather_pending[(i, "out")], [sv["gh"], all_started])
            mix_in, mix_w = sv["gh"], W["lru_w_out"]
        else:
            sv["u"] = _mm(f"l{i}_pool_in", h_in_b, W["pool_w_in"], "nn", [F32], b_lead=s)
            sv["zs"] = _pool_fwd(f"l{i}_pool_core", sv["u"], W["pool_w_grp"][s], pool_par[s])
            mix_in, mix_w = sv["zs"], W["pool_w_out"]
        sv["z1"], sv["x1"], sv["x1b"] = _mm(f"l{i}_mix_out_ln", mix_in, mix_w, "nn", b_lead=s,
                                            extras=[h_in, ln_mix_g[i][None], ln_mix_b[i][None]], **ln_out)
        if i % 2 == 0:
            finish_gather(f"l{i}_up", gather_pending[(i, "up")], sv["x1b"])
        sv["hpre"], sv["hact"] = _mm(f"l{i}_mlp_up", sv["x1b"], W["mlp_w1"], "nn", [BF16, BF16], b_lead=i,
                                     epi=lambda acc: (acc, jnp.square(jnp.maximum(acc, 0.0))))
        finish_gather(f"l{i}_rest", gather_pending[(i, "rest")], sv["hact"])
        sv["z2"], sv["x2"], sv["x2b"] = _mm(f"l{i}_mlp_down_ln", sv["hact"], W["mlp_w2"], "nn", b_lead=i,
                                            extras=[sv["x1"], ln_mlp_g[i][None], ln_mlp_b[i][None]], **ln_out)
        sv["pp"] = _mm(f"l{i}_ple_up", p3, W["ple_w"], "nn", [F32], a_lead=i, b_lead=i)

        def ple_epi(acc, bg, x2t, ppt):
            gpre = acc + bg
            x3 = x2t + ppt * jax.nn.sigmoid(gpre)
            return x3, x3, gpre

        h_in, h_in_b, sv["gpre"] = _mm(f"l{i}_ple_gate", sv["x2b"], W["ple_gate_w"], "nn", [F32, BF16, F32], b_lead=i,
                                       epi=ple_epi, extras=[ple_gate_b[i][None], sv["x2"], sv["pp"]], tn=MM_TN)
        saved.append(sv)

    dx, dpp, dgpre, sq, dbg = _loss_and_grad("loss", h_in, tgt, saved[-1]["gpre"], saved[-1]["pp"])
    loss = lax.psum(0.5 * sq[0, 0] / D, ("x", "y", "c"))

    dW = {n: [None] * A[n].shape[0] for n in BIG}
    dsmall = {n: [None] * A[n].shape[0] for n in REPLICATED + SMALL_SHARDED}
    small_names = REPLICATED + SMALL_SHARDED
    partial = {}
    exchange_pending = {}
    exchange_token = ()
    for i in reversed(range(depth)):
        s = i // 2
        sv = saved[i]
        dsmall["ple_gate_b"][i] = dbg[0]
        dW["ple_w"][i] = _mm(f"l{i}_d_ple_w", p3, dpp, "tn", [BF16], a_lead=i, after=exchange_token)
        dW["ple_gate_w"][i] = _mm(f"l{i}_d_ple_gate_w", sv["x2b"], dgpre, "tn", [BF16])
        ln_back = dict(out_dtypes=[F32, BF16], tm=MM_TM_ROWS, tn=D, n_sums=2)
        dz2, dz2b, dg, db = _mm(f"l{i}_d_x2_ln", dgpre, W["ple_gate_w"], "nt", b_lead=i,
                                extras=[dx, sv["z2"], ln_mlp_g[i][None]],
                                epi=lambda acc, d, z, g: _ln_grad(acc + d, z, g), after=exchange_token, **ln_back)
        dsmall["ln_mlp_g"][i], dsmall["ln_mlp_b"][i] = dg[0], db[0]
        dhpre = _mm(f"l{i}_d_hpre", dz2b, W["mlp_w2"], "nt", [BF16], b_lead=i, extras=[sv["hpre"]],
                    epi=lambda acc, hp: (acc * (2.0 * jnp.maximum(hp.astype(F32), 0.0)),))
        dW["mlp_w2"][i] = _mm(f"l{i}_d_mlp_w2", sv["hact"], dz2b, "tn", [BF16])
        dW["mlp_w1"][i] = _mm(f"l{i}_d_mlp_w1", sv["x1b"], dhpre, "tn", [BF16])
        mlp_after = ()
        if i == 0:
            early = [(n, 0) for n in ("ple_w", "ple_gate_w", "mlp_w2", "mlp_w1")]
            exchange_early0 = start_exchange("early0", early, [dW[n][l] for n, l in early], ())
            mlp_after = (exchange_early0[1][4],)
        dz1, dz1b, dg, db = _mm(f"l{i}_d_x1_ln", dhpre, W["mlp_w1"], "nt", b_lead=i,
                                extras=[dz2, sv["z1"], ln_mix_g[i][None]],
                                epi=lambda acc, d, z, g: _ln_grad(acc + alpha * d, z, g), after=mlp_after, **ln_back)
        dsmall["ln_mix_g"][i], dsmall["ln_mix_b"][i] = dg[0], db[0]
        if i % 2 == 0:
            dW["lru_w_out"][s] = _mm(f"l{i}_d_lru_w_out", sv["gh"], dz1b, "tn", [BF16])
            dgh = _mm(f"l{i}_d_gh", dz1b, W["lru_w_out"], "nt", [F32], b_lead=s)
            dup, dy, dwa, dwx, dpar = _lru_bwd(f"l{i}_lru_core_bwd", sv["proj"], sv["h"], dgh, lru_par[s],
                                               wa_b[s], wx_b[s])
            dsmall["lru_wa"][s], dsmall["lru_wx"][s] = dwa, dwx
            dsmall["lru_conv_w"][s] = dpar[0:4]
            for k, n in enumerate(("lru_conv_b", "lru_ba", "lru_bx", "lru_lambda")):
                dsmall[n][s] = dpar[4 + k]
            dmix_in = jnp.concatenate([dup, dy], axis=1)
            win = "lru_w_in"
        else:
            dW["pool_w_out"][s] = _mm(f"l{i}_d_pool_w_out", sv["zs"], dz1b, "tn", [BF16])
            dzs = _mm(f"l{i}_d_zs", dz1b, W["pool_w_out"], "nt", [F32], b_lead=s)
            dmix_in, dW["pool_w_grp"][s], dpar = _pool_bwd(f"l{i}_pool_core_bwd", sv["u"], dzs, W["pool_w_grp"][s],
                                                          pool_par[s])
            dsmall["pool_b_grp"][s], dsmall["pool_scale"][s] = dpar[0], dpar[1]
            win = "pool_w_in"
        dW[win][s] = _mm(f"l{i}_d_{win}", sv["x0b"], dmix_in, "tn", [BF16])
        x0_after = ()
        if i > 0:
            keys = layer_weights(i)
            exchange_pending[i] = start_exchange(f"l{i}", keys, [dW[n][l] for n, l in keys], ())
            exchange_token = (exchange_pending[i][1][4],)
        else:
            small_grads = [jnp.stack(dsmall[n]) for n in small_names]
            small_shapes = [g.shape for g in small_grads]
            packed_g = _pack(small_grads)
            assert packed_g.shape[0] % (N_DEV * SUBLANES) == 0, packed_g.shape
            late = [("lru_w_out", 0), ("lru_w_in", 0), ("small", 0)]
            exchange_late0 = start_exchange("late0", late, [dW["lru_w_out"][0], dW["lru_w_in"][0], packed_g], ())
            x0_after = (exchange_late0[1][4],)
        if i > 0:
            def x0_epi(acc, d, gpre, pp):
                dxv = acc + alpha * d
                return (dxv,) + _ple_grad(dxv, gpre, pp)

            dx, dpp, dgpre, dbg = _mm(f"l{i}_d_x0_ple", dmix_in, W[win], "nt", [F32, BF16, BF16], b_lead=s,
                                      extras=[dz1, saved[i - 1]["gpre"], saved[i - 1]["pp"]], epi=x0_epi,
                                      tm=MM_TM_ROWS, tn=D, n_sums=1)
        else:
            dx = _mm(f"l{i}_d_x0", dmix_in, W[win], "nt", [F32], b_lead=s, extras=[dz1],
                     epi=lambda acc, d: (acc + alpha * d,), after=x0_after)
    grad_x = dx.reshape(x.shape)

    for i in range(1, depth):
        finish_exchange(f"l{i}", exchange_pending[i], x0_after[0])
    stacked = {n: None for n in BIG}
    layer0 = layer_weights(0)

    def adamw(n, l, after=()):
        stacked[n] = _adamw_layer(f"adamw_{n}_{l}", partial[(n, l)], A[n], A["m_" + n], A["v_" + n], l, stacked[n],
                                  after=after)

    for n in BIG:
        for l in reversed(range(A[n].shape[0])):
            if (n, l) not in layer0:
                adamw(n, l)
    behind = [dx] + [stacked[n][0] for n in BIG if stacked[n] is not None]
    finish_exchange("early0", exchange_early0, behind)
    finish_exchange("late0", exchange_late0, behind)
    red = _sum8("sum_small", partial[("small", 0)])
    rows = red.shape[0]
    red_land = lax.dynamic_update_slice_in_dim(lax.empty((N_DEV * rows, LANES), F32), red, me * rows, 0)
    small_handle = _split_start("gather_small_start", "gather", [], [red_land], [0], [rows])
    for n, l in layer0:
        adamw(n, l, after=(small_handle[4],))
    outs = {n: [o.reshape(A[n].shape) for o in stacked[n]] for n in BIG}
    red_full = _split_wait("gather_small_wait", "gather", small_handle, [0], [rows], [stacked[n][0] for n, _ in layer0])[0]
    small_g = dict(zip(small_names, _unpack(red_full, small_shapes)))
    for n in SMALL_SHARDED:
        width = A[n].shape[-1]
        small_g[n] = lax.dynamic_slice_in_dim(small_g[n], me * width, width, axis=small_g[n].ndim - 1)
    for n in small_names:
        outs[n] = [small_g[n]] + _adamw_small(f"adamw_{n}", A[n], small_g[n], A["m_" + n], A["v_" + n])

    return (loss, grad_x, *[outs[n][0] for n in WEIGHTS], *[outs[n][1] for n in WEIGHTS],
            *[outs[n][2] for n in WEIGHTS], *[outs[n][3] for n in WEIGHTS])
```

```python
import functools
import math

import jax
import jax.numpy as jnp
from jax import lax
from jax.experimental import pallas as pl
from jax.experimental.pallas import tpu as pltpu

F32 = jnp.float32
BF16 = jnp.bfloat16
MESH = pl.DeviceIdType.MESH
N_DEV = 8
LANES = 128
SUBLANES = 8

LN_EPS = 1e-5
LRU_C = 8.0
CONV_WIDTH = 4
POOL_HALO = 16
ADAM_LR = 0.001
ADAM_B1 = 0.9
ADAM_B2 = 0.999
ADAM_EPS = 1e-08
ADAM_WD = 0.01
ADAM_STEP = 10

VMEM_LIMIT = 48 * 1024 * 1024
SEQ_CHUNK = 256
MM_TK = 4096
MM_TK_TOKENS = 4096
MM_TM_ROWS = 512
MM_TN = 512
MM_TN_WIDE = 1024
MM_TN_WIDE_MAX_K = 2048
GELU_C0 = math.sqrt(2.0 / math.pi)
GELU_C1 = 0.044715


def _in_hbm(*arrays):
    return [pltpu.with_memory_space_constraint(a, pltpu.HBM) for a in arrays]


def _cparams(*sem):
    return pltpu.CompilerParams(dimension_semantics=tuple(sem) if sem else None, vmem_limit_bytes=VMEM_LIMIT)


def _tile(n, pref):
    if n <= pref:
        return n
    t = pref - pref % LANES
    while t >= LANES:
        if n % t == 0:
            return t
        t -= LANES
    return n


def _row_tile(n, pref):
    if n <= pref:
        return n
    t = pref - pref % SUBLANES
    while t >= SUBLANES:
        if n % t == 0:
            return t
        t -= SUBLANES
    return n


def _mm(name, a, b, mode, out_dtypes, epi=None, extras=(), a_lead=None, b_lead=None, tm=1024, tn=None, tk=None,
        after=(), n_sums=0):
    if isinstance(b, (list, tuple)):
        b, b_lead = b[b_lead], None
    a2 = a.shape[-2:]
    b2 = b.shape[-2:]
    if mode == "nn":
        (M, K), N = a2, b2[1]
        assert b2[0] == K
    elif mode == "nt":
        (M, K), N = a2, b2[0]
        assert b2[1] == K
    else:
        (K, M), N = a2, b2[1]
        assert b2[0] == K
    if tk is None:
        tk = MM_TK_TOKENS if mode == "tn" else MM_TK
    if tn is None:
        tn = MM_TN_WIDE if (mode != "tn" and K <= MM_TN_WIDE_MAX_K) else MM_TN
    tm, tn, tk = _tile(M, tm), _tile(N, tn), _tile(K, tk)
    nk = K // tk
    n_extra = len(extras)
    n_out = len(out_dtypes)
    assert n_sums == 0 or tn == N
    resident = {"pipeline_mode": pl.Buffered(1)} if (tn == N and nk == 1) else {}

    def lead(shape, idx, which, **kw):
        if which is None:
            return pl.BlockSpec(shape, idx, **kw)
        return pl.BlockSpec((None,) + shape, lambda i, j, k: (which,) + idx(i, j, k), **kw)

    if mode == "nn":
        a_spec = lead((tm, tk), lambda i, j, k: (i, k), a_lead)
        b_spec = lead((tk, tn), lambda i, j, k: (k, j), b_lead, **resident)
        dims = (((1,), (0,)), ((), ()))
    elif mode == "nt":
        a_spec = lead((tm, tk), lambda i, j, k: (i, k), a_lead)
        b_spec = lead((tn, tk), lambda i, j, k: (j, k), b_lead, **resident)
        dims = (((1,), (1,)), ((), ()))
    else:
        a_spec = lead((tk, tm), lambda i, j, k: (k, i), a_lead)
        b_spec = lead((tk, tn), lambda i, j, k: (k, j), b_lead, **resident)
        dims = (((0,), (0,)), ((), ()))
    e_specs = []
    for e in extras:
        if e.shape[0] == 1:
            e_specs.append(pl.BlockSpec((1, tn), lambda i, j, k: (0, j)))
        else:
            e_specs.append(pl.BlockSpec((tm, tn), lambda i, j, k: (i, j)))

    n_after = len(after)

    def body(a_ref, b_ref, *rest):
        e_refs = rest[:n_extra]
        rest = rest[:n_extra] + rest[n_extra + n_after:]
        o_refs = rest[n_extra:n_extra + n_out]
        s_refs = rest[n_extra + n_out:n_extra + n_out + n_sums]
        part = lax.dot_general(a_ref[...].astype(BF16), b_ref[...].astype(BF16), dims, preferred_element_type=F32)

        def finish(r):
            res = (r,) if epi is None else epi(r, *[e[...] for e in e_refs])
            for o, v in zip(o_refs, res[:n_out]):
                o[...] = v.astype(o.dtype)
            first = pl.program_id(0) == 0
            for sr, v in zip(s_refs, res[n_out:]):
                @pl.when(first)
                def _(sr=sr, v=v):
                    sr[...] = v

                @pl.when(jnp.logical_not(first))
                def _(sr=sr, v=v):
                    sr[...] += v

        if nk == 1:
            finish(part)
            return
        acc = rest[n_extra + n_out + n_sums]
        k = pl.program_id(2)

        @pl.when(k == 0)
        def _():
            acc[...] = part

        @pl.when(jnp.logical_and(k > 0, k < nk - 1))
        def _():
            acc[...] += part

        @pl.when(k == nk - 1)
        def _():
            finish(acc[...] + part)

    outs = pl.pallas_call(
        body,
        name=name,
        grid=(M // tm, N // tn, nk),
        in_specs=[a_spec, b_spec] + e_specs + [pl.BlockSpec(memory_space=pl.ANY)] * n_after,
        out_specs=[pl.BlockSpec((tm, tn), lambda i, j, k: (i, j)) for _ in out_dtypes]
        + [pl.BlockSpec((1, tn), lambda i, j, k: (0, 0))] * n_sums,
        out_shape=[jax.ShapeDtypeStruct((M, N), d) for d in out_dtypes] + [jax.ShapeDtypeStruct((1, N), F32)] * n_sums,
        scratch_shapes=[pltpu.VMEM((tm, tn), F32)] if nk > 1 else [],
        compiler_params=_cparams(*(("arbitrary",) * 3 if n_sums else ("parallel", "parallel", "arbitrary"))),
    )(*_in_hbm(a, b, *extras), *after)
    return outs[0] if n_out + n_sums == 1 else tuple(outs)


def _rowwise(name, fn, tiled, params, outs, accs=(), tm=256, after=()):
    S = tiled[0].shape[0]
    tm = _tile(S, tm)
    nt, npar, no = len(tiled), len(params), len(outs)
    n_after = len(after)

    def body(*refs):
        t_refs = refs[:nt]
        p_refs = refs[nt:nt + npar]
        refs = refs[nt + npar + n_after:]
        o_refs = refs[:no]
        a_refs = refs[no:]
        res = fn(*[r[...] for r in t_refs], *[r[...] for r in p_refs])
        for o, v in zip(o_refs, res[:no]):
            o[...] = v.astype(o.dtype)
        first = pl.program_id(0) == 0
        for ar, v in zip(a_refs, res[no:]):
            @pl.when(first)
            def _(ar=ar, v=v):
                ar[...] = v

            @pl.when(jnp.logical_not(first))
            def _(ar=ar, v=v):
                ar[...] += v

    full = lambda p: pl.BlockSpec(p.shape, lambda i, nd=p.ndim: (0,) * nd)
    res = pl.pallas_call(
        body,
        name=name,
        grid=(S // tm,),
        in_specs=[pl.BlockSpec((tm, t.shape[1]), lambda i: (i, 0)) for t in tiled] + [full(p) for p in params]
        + [pl.BlockSpec(memory_space=pl.ANY)] * n_after,
        out_specs=[pl.BlockSpec((tm, c), lambda i: (i, 0)) for c, _ in outs]
        + [pl.BlockSpec(s, lambda i, nd=len(s): (0,) * nd) for s in accs],
        out_shape=[jax.ShapeDtypeStruct((S, c), d) for c, d in outs] + [jax.ShapeDtypeStruct(s, F32) for s in accs],
        compiler_params=_cparams("arbitrary"),
    )(*_in_hbm(*tiled, *params), *after)
    return res


def _ln_stats(z):
    mu = jnp.mean(z, axis=-1, keepdims=True)
    zc = z - mu
    var = jnp.mean(zc * zc, axis=-1, keepdims=True)
    return zc, lax.rsqrt(var + LN_EPS)


def _ln_apply(z, g, b):
    zc, rstd = _ln_stats(z)
    y = zc * rstd * g + b
    return z, y, y


def _ln_grad(dy, z, g):
    zc, rstd = _ln_stats(z)
    xhat = zc * rstd
    dxh = dy * g
    m1 = jnp.mean(dxh, axis=-1, keepdims=True)
    m2 = jnp.mean(dxh * xhat, axis=-1, keepdims=True)
    dz = rstd * (dxh - m1 - xhat * m2)
    return dz, dz, jnp.sum(dy * xhat, axis=0, keepdims=True), jnp.sum(dy, axis=0, keepdims=True)


def _ple_grad(dx3, gpre, pp):
    gate = jax.nn.sigmoid(gpre)
    dgpre = dx3 * pp * gate * (1.0 - gate)
    return dx3 * gate, dgpre, jnp.sum(dgpre, axis=0, keepdims=True)


def _loss_and_grad(name, y, target, gpre, pp):
    d = y.shape[1]

    def fn(y, t, gpre, pp):
        err = y - t
        sq = jnp.sum(jnp.sum(err * err, axis=0, keepdims=True), axis=1, keepdims=True)
        dy = err * (1.0 / d)
        dpp, dgpre, dbg = _ple_grad(dy, gpre, pp)
        return dy, dpp, dgpre, jnp.broadcast_to(sq, (1, LANES)), dbg

    return _rowwise(name, fn, [y, target, gpre, pp], [], [(d, F32), (d, BF16), (d, BF16)], accs=[(1, LANES), (1, d)])


def _rows(shape):
    return lax.broadcasted_iota(jnp.int32, shape, 0)


def _gelu(y):
    t = jnp.tanh(GELU_C0 * (y + GELU_C1 * y * y * y))
    return 0.5 * y * (1.0 + t), t


def _gelu_grad(y, t):
    return 0.5 * (1.0 + t) + 0.5 * y * (1.0 - t * t) * GELU_C0 * (1.0 + 3.0 * GELU_C1 * y * y)


def _softplus(x):
    return jnp.maximum(x, 0.0) + jnp.log(1.0 + jnp.exp(-jnp.abs(x)))


def _conv_fwd(xs, cw, cb):
    n = xs.shape[0]
    u = cw[3:4] * xs
    for k in (1, 2, 3):
        u = u + cw[3 - k:4 - k] * pltpu.roll(xs, k, 0)
    del n
    return u[SUBLANES:] + cb


def _lru_gates(u, wa, wx, ba, bx, sp, grow):
    ub = u.astype(BF16)
    r = jax.nn.sigmoid(jnp.dot(ub, wa, preferred_element_type=F32) + ba)
    ig = jax.nn.sigmoid(jnp.dot(ub, wx, preferred_element_type=F32) + bx)
    log_a = (-LRU_C) * r * sp
    a = jnp.exp(log_a)
    mult = jnp.sqrt(jnp.tanh(-log_a) * (1.0 + a * a))
    mult = jnp.where(grow == 0, 1.0, mult)
    return ub, r, ig, a, mult


def _scan8_fwd(a, b):
    row = _rows(a.shape)
    for k in (1, 2, 4):
        m = row >= k
        b = jnp.where(m, a * pltpu.roll(b, k, 0) + b, b)
        a = jnp.where(m, a * pltpu.roll(a, k, 0), a)
    return a, b


def _scan8_bwd(c, d):
    row = _rows(c.shape)
    for k in (1, 2, 4):
        m = row < SUBLANES - k
        d = jnp.where(m, c * pltpu.roll(d, SUBLANES - k, 0) + d, d)
        c = jnp.where(m, c * pltpu.roll(c, SUBLANES - k, 0), c)
    return c, d


def _pad_copy(dst, src, front, back):
    s, c = src.shape
    if front:
        dst[pl.ds(0, front), :] = jnp.zeros((front, c), dst.dtype)
    if back:
        dst[pl.ds(front + s, back), :] = jnp.zeros((back, c), dst.dtype)
    dst[pl.ds(front, s), :] = src[...].astype(dst.dtype)


def _lru_fwd(name, proj, par, wa, wx):
    S = proj.shape[0]
    R = proj.shape[1] // 2
    H = R // LANES
    ch = _tile(S, SEQ_CHUNK)
    nch = S // ch
    H8 = SUBLANES

    def body(up_ref, y_ref, par_ref, wa_ref, wx_ref, gh_ref, h_ref, r_ref, ig_ref, a_ref, mult_ref, up_pad):
        _pad_copy(up_pad, up_ref, H8, 0)
        par = par_ref[...]
        cw, cb, ba, bx = par[0:4], par[4:5], par[5:6], par[6:7]
        sp = _softplus(-par[7:8])
        wa_m, wx_m = wa_ref[...], wx_ref[...]

        def chunk(ci, carry):
            r0 = pl.multiple_of(ci * ch, ch)
            xs = up_pad[pl.ds(r0, ch + H8), :]
            u = _conv_fwd(xs, cw, cb)
            grow = _rows(u.shape) + r0
            _, r, ig, a, mult = _lru_gates(u, wa_m, wx_m, ba, bx, sp, grow)
            for ref, val in ((r_ref, r), (ig_ref, ig), (a_ref, a), (mult_ref, mult)):
                ref[pl.ds(r0, ch), :] = val
            bt = mult * (ig * u)
            hs = []
            for j in range(ch // H8):
                aa, bb = _scan8_fwd(a[j * H8:(j + 1) * H8], bt[j * H8:(j + 1) * H8])
                hj = bb + aa * carry
                carry = hj[H8 - 1:H8]
                hs.append(hj)
            h = jnp.concatenate(hs, axis=0)
            h_ref[pl.ds(r0, ch), :] = h
            gy, _ = _gelu(y_ref[pl.ds(r0, ch), :])
            gh_ref[pl.ds(r0, ch), :] = (h * gy).astype(gh_ref.dtype)
            return carry

        lax.fori_loop(0, nch, chunk, jnp.zeros((1, LANES), F32))

    col = lambda off: pl.BlockSpec((S, LANES), lambda h: (0, h + off))
    return pl.pallas_call(
        body,
        name=name,
        grid=(H,),
        in_specs=[col(0), col(H), pl.BlockSpec((8, LANES), lambda h: (0, h)),
                  pl.BlockSpec((None, LANES, LANES), lambda h: (h, 0, 0)),
                  pl.BlockSpec((None, LANES, LANES), lambda h: (h, 0, 0))],
        out_specs=[col(0)] * 6,
        out_shape=[jax.ShapeDtypeStruct((S, R), BF16)] + [jax.ShapeDtypeStruct((S, R), F32)] * 5,
        scratch_shapes=[pltpu.VMEM((S + H8, LANES), F32)],
        compiler_params=_cparams("parallel"),
    )(proj, proj, par, wa, wx)


def _lru_bwd(name, proj, h, dgh, gates, par, wa, wx):
    S = proj.shape[0]
    R = proj.shape[1] // 2
    H = R // LANES
    ch = _tile(S, SEQ_CHUNK)
    nch = S // ch
    H8 = SUBLANES
    nb = ch // H8

    def body(up_ref, y_ref, h_ref, dgh_ref, r_ref, ig_ref, a_ref, mult_ref, par_ref, wa_ref, wx_ref,
             dup_ref, dy_ref, dwa_ref, dwx_ref, dpar_ref, up_pad, du_pad, vec_acc):
        _pad_copy(up_pad, up_ref, H8, 0)
        du_pad[pl.ds(S, H8), :] = jnp.zeros((H8, LANES), F32)
        par = par_ref[...]
        cw, cb, lam = par[0:4], par[4:5], par[7:8]
        sp = _softplus(-lam)
        wa_m, wx_m = wa_ref[...], wx_ref[...]
        dwa_ref[...] = jnp.zeros_like(dwa_ref)
        dwx_ref[...] = jnp.zeros_like(dwx_ref)
        vec_acc[...] = jnp.zeros_like(vec_acc)
        nt_dims = (((1,), (1,)), ((), ()))
        tn_dims = (((0,), (0,)), ((), ()))

        def chunk(it, carry):
            lam_next, a_next = carry
            ci = nch - 1 - it
            r0 = pl.multiple_of(ci * ch, ch)
            xs = up_pad[pl.ds(r0, ch + H8), :]
            u = _conv_fwd(xs, cw, cb)
            row = _rows(u.shape)
            grow = row + r0
            ub = u.astype(BF16)
            here = pl.ds(r0, ch)
            r, ig, a, mult = r_ref[here, :], ig_ref[here, :], a_ref[here, :], mult_ref[here, :]
            hcur = h_ref[here, :]
            before = h_ref[pl.ds(pl.multiple_of(jnp.maximum(r0 - H8, 0), H8), H8), :][H8 - 1:H8]
            hprev = jnp.where(row == 0, jnp.where(ci > 0, before, 0.0), pltpu.roll(hcur, 1, 0))
            y = y_ref[pl.ds(r0, ch), :]
            dgh = dgh_ref[pl.ds(r0, ch), :]
            gy, t = _gelu(y)
            dy_ref[pl.ds(r0, ch), :] = (dgh * hcur * _gelu_grad(y, t)).astype(dy_ref.dtype)
            dh = dgh * gy
            c = jnp.where(row == ch - 1, a_next, pltpu.roll(a, ch - 1, 0))
            ls = [None] * nb
            for j in range(nb - 1, -1, -1):
                cc, dd = _scan8_bwd(c[j * H8:(j + 1) * H8], dh[j * H8:(j + 1) * H8])
                lj = dd + cc * lam_next
                lam_next = lj[0:1]
                ls[j] = lj
            lmb = jnp.concatenate(ls, axis=0)
            da = lmb * hprev
            gu = ig * u
            dmult = lmb * gu
            dlog_a = da * a + jnp.where(grow == 0, 0.0, dmult * (-(a * a) / mult))
            dr = dlog_a * ((-LRU_C) * sp)
            drp = dr * r * (1.0 - r)
            dip = (lmb * mult * u) * ig * (1.0 - ig)
            drb, dib = drp.astype(BF16), dip.astype(BF16)
            du = (lmb * mult * ig
                  + lax.dot_general(drb, wa_m, nt_dims, preferred_element_type=F32)
                  + lax.dot_general(dib, wx_m, nt_dims, preferred_element_type=F32))
            du_pad[pl.ds(r0, ch), :] = du
            dwa_ref[...] += lax.dot_general(ub, drb, tn_dims, preferred_element_type=F32)
            dwx_ref[...] += lax.dot_general(ub, dib, tn_dims, preferred_element_type=F32)
            ssum = lambda v: jnp.sum(v, axis=0, keepdims=True)
            vec_acc[0:1, :] += ssum(drp)
            vec_acc[1:2, :] += ssum(dip)
            vec_acc[2:3, :] += ssum(dlog_a * ((-LRU_C) * r))
            return lam_next, a[0:1]

        zero = jnp.zeros((1, LANES), F32)
        lax.fori_loop(0, nch, chunk, (zero, zero))

        def conv_chunk(ci, acc):
            r0 = pl.multiple_of(ci * ch, ch)
            ds = du_pad[pl.ds(r0, ch + H8), :]
            xs = up_pad[pl.ds(r0, ch + H8), :]
            n = ch + H8
            du = ds[:ch]
            dup = cw[3:4] * du
            new = [acc[3] + jnp.sum(du * xs[H8:], axis=0, keepdims=True)]
            for k in (1, 2, 3):
                dup = dup + cw[3 - k:4 - k] * pltpu.roll(ds, n - k, 0)[:ch]
                new.append(acc[3 - k] + jnp.sum(du * pltpu.roll(xs, k, 0)[H8:], axis=0, keepdims=True))
            dup_ref[pl.ds(r0, ch), :] = dup.astype(dup_ref.dtype)
            return (new[3], new[2], new[1], new[0], acc[4] + jnp.sum(du, axis=0, keepdims=True))

        acc = lax.fori_loop(0, nch, conv_chunk, (zero,) * 5)
        dlam = vec_acc[2:3, :] * (-jax.nn.sigmoid(-lam))
        dpar_ref[...] = jnp.concatenate(list(acc) + [vec_acc[0:1, :], vec_acc[1:2, :], dlam], axis=0)

    col = lambda off: pl.BlockSpec((S, LANES), lambda h: (0, h + off))
    head = pl.BlockSpec((None, LANES, LANES), lambda h: (h, 0, 0))
    return pl.pallas_call(
        body,
        name=name,
        grid=(H,),
        in_specs=[col(0), col(H)] + [col(0)] * 6 + [pl.BlockSpec((8, LANES), lambda h: (0, h)), head, head],
        out_specs=[col(0), col(0), head, head, pl.BlockSpec((8, LANES), lambda h: (0, h))],
        out_shape=[jax.ShapeDtypeStruct((S, R), BF16), jax.ShapeDtypeStruct((S, R), BF16),
                   jax.ShapeDtypeStruct((H, LANES, LANES), F32), jax.ShapeDtypeStruct((H, LANES, LANES), F32),
                   jax.ShapeDtypeStruct((8, R), F32)],
        scratch_shapes=[pltpu.VMEM((S + H8, LANES), F32), pltpu.VMEM((S + H8, LANES), F32),
                        pltpu.VMEM((8, LANES), F32)],
        compiler_params=_cparams("parallel"),
    )(proj, proj, h, dgh, *gates, par, wa, wx)


def _window_sum(xs, g, up):
    n = xs.shape[0]
    s = xs
    for lvl, k in enumerate((1, 2, 4, 8)):
        sh = pltpu.roll(s, (n - k) if up else k, 0)
        s = s + jnp.where(g >= lvl, sh, 0.0)
    return s


def _pool_count(grow, g):
    return jnp.minimum(grow + 1, lax.shift_left(jnp.int32(2), g)).astype(F32)


def _pool_fwd(name, u, wgrp, par):
    S, D = u.shape
    G, W = wgrp.shape[0], wgrp.shape[1]
    ch = _tile(S, SEQ_CHUNK)
    nch = S // ch
    PH = POOL_HALO

    def body(u_ref, w_ref, par_ref, zs_ref, u_pad):
        g = pl.program_id(0)
        _pad_copy(u_pad, u_ref, PH, 0)
        par = par_ref[...]
        w = w_ref[...]

        def chunk(ci, _):
            r0 = pl.multiple_of(ci * ch, ch)
            xs = u_pad[pl.ds(r0, ch + PH), :]
            ws = _window_sum(xs, g, False)[PH:]
            uc = xs[PH:]
            cnt = _pool_count(_rows(uc.shape) + r0, g)
            pooled = ws / cnt - uc
            z = jnp.dot(pooled.astype(BF16), w, preferred_element_type=F32) + par[0:1]
            zs_ref[pl.ds(r0, ch), :] = (z * par[1:2]).astype(zs_ref.dtype)
            return 0

        lax.fori_loop(0, nch, chunk, 0)

    return pl.pallas_call(
        body,
        name=name,
        grid=(G,),
        in_specs=[pl.BlockSpec((S, W), lambda g: (0, g)), pl.BlockSpec((None, W, W), lambda g: (g, 0, 0)),
                  pl.BlockSpec((2, W), lambda g: (0, g))],
        out_specs=pl.BlockSpec((S, W), lambda g: (0, g)),
        out_shape=jax.ShapeDtypeStruct((S, D), BF16),
        scratch_shapes=[pltpu.VMEM((S + PH, W), F32)],
        compiler_params=_cparams("parallel"),
    )(u, wgrp, par)


def _pool_bwd(name, u, dzs, wgrp, par):
    S, D = u.shape
    G, W = wgrp.shape[0], wgrp.shape[1]
    ch = _tile(S, SEQ_CHUNK)
    nch = S // ch
    PH = POOL_HALO

    def body(u_ref, dzs_ref, w_ref, par_ref, du_ref, dw_ref, dpar_ref, u_pad, q_pad, dw_acc):
        g = pl.program_id(0)
        _pad_copy(u_pad, u_ref, PH, 0)
        q_pad[pl.ds(S, PH), :] = jnp.zeros((PH, W), F32)
        par = par_ref[...]
        w = w_ref[...]
        dw_acc[...] = jnp.zeros_like(dw_acc)

        def chunk(ci, acc):
            db, dsc = acc
            r0 = pl.multiple_of(ci * ch, ch)
            xs = u_pad[pl.ds(r0, ch + PH), :]
            ws = _window_sum(xs, g, False)[PH:]
            uc = xs[PH:]
            cnt = _pool_count(_rows(uc.shape) + r0, g)
            pooled = (ws / cnt - uc).astype(BF16)
            z = jnp.dot(pooled, w, preferred_element_type=F32) + par[0:1]
            dzs = dzs_ref[pl.ds(r0, ch), :]
            dz = dzs * par[1:2]
            dzb = dz.astype(BF16)
            dw_acc[...] += lax.dot_general(pooled, dzb, (((0,), (0,)), ((), ())), preferred_element_type=F32)
            dpooled = lax.dot_general(dzb, w, (((1,), (1,)), ((), ())), preferred_element_type=F32)
            q_pad[pl.ds(r0, ch), :] = dpooled / cnt
            return (db + jnp.sum(dz, axis=0, keepdims=True), dsc + jnp.sum(dzs * z, axis=0, keepdims=True))

        zero = jnp.zeros((1, W), F32)
        db, dsc = lax.fori_loop(0, nch, chunk, (zero, zero))
        dpar_ref[...] = jnp.concatenate([db, dsc], axis=0)
        dw_ref[...] = dw_acc[...].astype(dw_ref.dtype)

        def back(ci, _):
            r0 = pl.multiple_of(ci * ch, ch)
            qs = q_pad[pl.ds(r0, ch + PH), :]
            ws = _window_sum(qs, g, True)[:ch]
            qc = qs[:ch]
            cnt = _pool_count(_rows(qc.shape) + r0, g)
            du_ref[pl.ds(r0, ch), :] = (ws - qc * cnt).astype(du_ref.dtype)
            return 0

        lax.fori_loop(0, nch, back, 0)

    blk = pl.BlockSpec((S, W), lambda g: (0, g))
    wspec = pl.BlockSpec((None, W, W), lambda g: (g, 0, 0))
    pspec = pl.BlockSpec((2, W), lambda g: (0, g))
    return pl.pallas_call(
        body,
        name=name,
        grid=(G,),
        in_specs=[blk, blk, wspec, pspec],
        out_specs=[blk, wspec, pspec],
        out_shape=[jax.ShapeDtypeStruct((S, D), BF16), jax.ShapeDtypeStruct((G, W, W), BF16),
                   jax.ShapeDtypeStruct((2, D), F32)],
        scratch_shapes=[pltpu.VMEM((S + PH, W), F32), pltpu.VMEM((S + PH, W), F32), pltpu.VMEM((W, W), F32)],
        compiler_params=_cparams("parallel"),
    )(u, dzs, wgrp, par)


def _my_place():
    x, y, c = lax.axis_index("x"), lax.axis_index("y"), lax.axis_index("c")
    return x, y, c, 4 * x + 2 * y + c


def _peers(x, y, c):
    out = []
    for d in range(1, N_DEV):
        px = 1 - x if d & 4 else x
        py = 1 - y if d & 2 else y
        pc = 1 - c if d & 1 else c
        out.append(((px, py, pc), 4 * px + 2 * py + pc))
    return out


def _window(ref, axis, start, size):
    idx = [slice(None)] * len(ref.shape)
    idx[axis] = pl.ds(start, size)
    return ref.at[tuple(idx)]


def _to_bf16(name, arrs, after=()):
    outs = []
    for i, a in enumerate(arrs):
        a2 = a.reshape(-1, a.shape[-1])
        tr = _tile(a2.shape[0], 512)
        o = pl.pallas_call(
            lambda a_ref, *rest: rest[-1].__setitem__(Ellipsis, a_ref[...].astype(BF16)),
            name=f"{name}_{i}",
            grid=(a2.shape[0] // tr,),
            in_specs=[pl.BlockSpec((tr, a2.shape[1]), lambda r: (r, 0))] + [pl.BlockSpec(memory_space=pl.ANY)] * len(after),
            out_specs=pl.BlockSpec((tr, a2.shape[1]), lambda r: (r, 0)),
            out_shape=jax.ShapeDtypeStruct(a2.shape, BF16),
            compiler_params=_cparams("parallel"),
        )(a2, *after)
        outs.append(o.reshape(a.shape))
    return outs


def _all_gather(name, shards, axes):
    n = len(shards)
    sizes = [s.shape[ax] for s, ax in zip(shards, axes)]

    def body(*refs):
        ins, outs = refs[:n], refs[n:2 * n]
        send, recv, loc = refs[2 * n:]
        x, y, c, me = _my_place()
        peers = _peers(x, y, c)
        local = []
        for i in range(n):
            dst = _window(outs[i], axes[i], me * sizes[i], sizes[i])
            cp = pltpu.make_async_copy(ins[i], dst, loc.at[i])
            cp.start()
            local.append(cp)
            for peer, _ in peers:
                pltpu.make_async_remote_copy(src_ref=ins[i], dst_ref=dst, send_sem=send.at[i], recv_sem=recv.at[i],
                                             device_id=peer, device_id_type=MESH).start()
        for i in range(n):
            local[i].wait()
            seven = _window(outs[i], axes[i], 0, (N_DEV - 1) * sizes[i])
            pltpu.make_async_remote_copy(src_ref=seven, dst_ref=seven, send_sem=send.at[i], recv_sem=recv.at[i],
                                         device_id=(x, y, c), device_id_type=MESH).wait()

    def full_shape(s, ax):
        shp = list(s.shape)
        shp[ax] *= N_DEV
        return jax.ShapeDtypeStruct(tuple(shp), s.dtype)

    any_spec = pl.BlockSpec(memory_space=pl.ANY)
    return pl.pallas_call(
        body,
        name=name,
        in_specs=[any_spec] * n,
        out_specs=[any_spec] * n,
        out_shape=[full_shape(s, ax) for s, ax in zip(shards, axes)],
        scratch_shapes=[pltpu.SemaphoreType.DMA((n,)), pltpu.SemaphoreType.DMA((n,)), pltpu.SemaphoreType.DMA((n,))],
        compiler_params=pltpu.CompilerParams(has_side_effects=True),
    )(*shards)


HBM_SPEC = pl.BlockSpec(memory_space=pltpu.HBM)
SEM_SPEC = pl.BlockSpec(memory_space=pltpu.SEMAPHORE)
SPLIT_EFFECT = pltpu.SideEffectType.DATAFLOW_SIDE_EFFECTING


def _push_all(kind, src, dst, axis, size, send_sem, recv_sem, place):
    x, y, c, me = place
    for peer, pidx in _peers(x, y, c):
        if kind == "gather":
            s = d = _window(dst, axis, me * size, size)
        else:
            s, d = _window(src, axis, pidx * size, size), dst.at[me]
        pltpu.make_async_remote_copy(src_ref=s, dst_ref=d, send_sem=send_sem, recv_sem=recv_sem, device_id=peer,
                                     device_id_type=MESH).start()


def _drain_all(kind, dst, axis, size, send_sem, recv_sem, place):
    x, y, c, _ = place
    seven = _window(dst, axis, 0, (N_DEV - 1) * size) if kind == "gather" else dst.at[pl.ds(0, N_DEV - 1)]
    pltpu.make_async_remote_copy(src_ref=seven, dst_ref=seven, send_sem=send_sem, recv_sem=recv_sem,
                                 device_id=(x, y, c), device_id_type=MESH).wait()


def _own_block_placed(src, axis, size, me):
    own = lax.dynamic_slice_in_dim(src, me * size, size, axis)
    return lax.dynamic_update_slice_in_dim(lax.empty((N_DEV,) + own.shape, src.dtype), own[None], me, 0)


def _split_start(name, kind, srcs, lands, axes, sizes, after=()):
    n, ns, na = len(lands), len(srcs), len(after)

    def body(*refs):
        src_refs, land_refs = refs[:ns], refs[ns:ns + n]
        send, recv = refs[ns + n + na], refs[ns + n + na + 1]
        token = refs[-1]
        place = _my_place()
        for k in range(n):
            _push_all(kind, src_refs[k] if ns else None, land_refs[k], axes[k], sizes[k], send.at[k], recv.at[k], place)
        token[...] = jnp.zeros_like(token)

    hbm = lambda a: pltpu.HBM(a.shape, a.dtype)
    res = pl.pallas_call(
        body,
        name=name,
        out_shape=(pltpu.SemaphoreType.DMA((n,)), pltpu.SemaphoreType.DMA((n,)), *[hbm(a) for a in srcs],
                   *[hbm(a) for a in lands], jax.ShapeDtypeStruct((SUBLANES, LANES), F32)),
        in_specs=[HBM_SPEC] * (ns + n) + [pl.BlockSpec(memory_space=pl.ANY)] * na,
        out_specs=(SEM_SPEC, SEM_SPEC, *[HBM_SPEC] * (ns + n), pl.BlockSpec(memory_space=pltpu.VMEM)),
        input_output_aliases={k: 2 + k for k in range(ns + n)},
        compiler_params=pltpu.CompilerParams(has_side_effects=SPLIT_EFFECT),
    )(*[pltpu.with_memory_space_constraint(a, pltpu.HBM) for a in (*srcs, *lands)], *after)
    return res[0], res[1], list(res[2:2 + ns]), list(res[2 + ns:2 + ns + n]), res[-1]


def _split_wait(name, kind, handle, axes, sizes, after):
    send, recv, srcs, lands, _ = handle
    n, ns = len(lands), len(srcs)
    after = list(after) if isinstance(after, (list, tuple)) else [after]

    def body(*refs):
        land_refs = refs[ns:ns + n]
        send_ref, recv_ref = refs[ns + n], refs[ns + n + 1]
        place = _my_place()
        for k in range(n):
            _drain_all(kind, land_refs[k], axes[k], sizes[k], send_ref.at[k], recv_ref.at[k], place)

    hbm = lambda a: pltpu.HBM(a.shape, a.dtype)
    res = pl.pallas_call(
        body,
        name=name,
        out_shape=tuple(hbm(a) for a in (*srcs, *lands)),
        in_specs=[HBM_SPEC] * (ns + n) + [SEM_SPEC, SEM_SPEC] + [pl.BlockSpec(memory_space=pl.ANY)] * len(after),
        out_specs=tuple([HBM_SPEC] * (ns + n)),
        input_output_aliases={k: k for k in range(ns + n)},
        compiler_params=pltpu.CompilerParams(has_side_effects=SPLIT_EFFECT),
    )(*srcs, *lands, send, recv, *after)
    return list(res[ns:])


def _cast_into_window(name, a, l, axis, me1, after=()):
    shp = a.shape[1:]
    cast = lambda me_ref, a_ref, *rest: rest[-1].__setitem__(Ellipsis, a_ref[...].astype(BF16))
    if len(shp) == 3:
        assert axis == 1
        G, r, c = shp
        full = (G, r * N_DEV, c)
        grid = (G,)
        in_spec = pl.BlockSpec((None, None, r, c), lambda g, me: (l, g, 0, 0))
        out_spec = pl.BlockSpec((None, r, c), lambda g, me: (g, me[0], 0))
    else:
        r, c = shp
        tr = _tile(r, 512)
        nb = r // tr
        grid = (nb,)
        in_spec = pl.BlockSpec((None, tr, c), lambda i, me: (l, i, 0))
        if axis == 0:
            full = (r * N_DEV, c)
            out_spec = pl.BlockSpec((tr, c), lambda i, me: (me[0] * nb + i, 0))
        else:
            full = (r, c * N_DEV)
            out_spec = pl.BlockSpec((tr, c), lambda i, me: (i, me[0]))
    return pl.pallas_call(
        cast,
        name=name,
        grid_spec=pltpu.PrefetchScalarGridSpec(
            num_scalar_prefetch=1, grid=grid,
            in_specs=[in_spec] + [pl.BlockSpec(memory_space=pl.ANY)] * len(after), out_specs=out_spec),
        out_shape=jax.ShapeDtypeStruct(full, BF16),
        compiler_params=_cparams("arbitrary"),
    )(me1, a, *after)


def _adamw_math(w, g, m, v):
    m = ADAM_B1 * m + (1.0 - ADAM_B1) * g
    v = ADAM_B2 * v + (1.0 - ADAM_B2) * jnp.square(g)
    m_hat = m / (1.0 - ADAM_B1 ** ADAM_STEP)
    v_hat = v / (1.0 - ADAM_B2 ** ADAM_STEP)
    delta = -ADAM_LR * (m_hat / (jnp.sqrt(v_hat) + ADAM_EPS) + ADAM_WD * w)
    return delta, m, v


def _sum_slots(buf_ref):
    g = buf_ref[0].astype(F32)
    for s in range(1, N_DEV):
        g = g + buf_ref[s].astype(F32)
    return g


def _adamw_layer(name, buf, w, m, v, l, prev, after=()):
    shape = w.shape
    L, C = shape[0], shape[-1]
    Rr = math.prod(shape[1:-1])
    buf3 = buf.reshape(N_DEV, Rr, C)
    w3, m3, v3 = (t.reshape(L, Rr, C) for t in (w, m, v))
    tr = _tile(Rr, 2 * LANES) if Rr % LANES == 0 else Rr
    n_prev = 0 if prev is None else 4

    def body(buf_ref, w_ref, m_ref, v_ref, *rest):
        g_out, d_out, m_out, v_out = rest[n_prev + len(after):]
        g = _sum_slots(buf_ref)
        d, mm, vv = _adamw_math(w_ref[...], g, m_ref[...], v_ref[...])
        g_out[...] = g
        d_out[...] = d
        m_out[...] = mm
        v_out[...] = vv

    spec = pl.BlockSpec((None, tr, C), lambda r: (l, r, 0))
    outs = pl.pallas_call(
        body,
        name=name,
        grid=(Rr // tr,),
        in_specs=[pl.BlockSpec((N_DEV, tr, C), lambda r: (0, r, 0)), spec, spec, spec]
        + [pl.BlockSpec(memory_space=pl.ANY)] * (n_prev + len(after)),
        out_specs=[spec] * 4,
        out_shape=[jax.ShapeDtypeStruct((L, Rr, C), F32)] * 4,
        input_output_aliases={4 + k: k for k in range(n_prev)},
        compiler_params=_cparams("parallel"),
    )(buf3, w3, m3, v3, *(prev or ()), *after)
    return list(outs)


def _sum8(name, buf):
    R = buf.shape[1]

    def body(buf_ref, o_ref):
        o_ref[...] = _sum_slots(buf_ref)

    return pl.pallas_call(
        body,
        name=name,
        in_specs=[pl.BlockSpec(buf.shape, lambda: (0, 0, 0))],
        out_specs=pl.BlockSpec((R, LANES), lambda: (0, 0)),
        out_shape=jax.ShapeDtypeStruct((R, LANES), F32),
        compiler_params=_cparams(),
    )(buf)


def _adamw_small(name, w, g, m, v):
    shape = w.shape
    w2, g2, m2, v2 = (t.reshape(-1, shape[-1]) for t in (w, g, m, v))
    R, C = w2.shape
    tr = _row_tile(R, 512)

    def body(w_ref, g_ref, m_ref, v_ref, d_out, m_out, v_out):
        d, mm, vv = _adamw_math(w_ref[...], g_ref[...], m_ref[...], v_ref[...])
        d_out[...] = d
        m_out[...] = mm
        v_out[...] = vv

    spec = pl.BlockSpec((tr, C), lambda r: (r, 0))
    outs = pl.pallas_call(
        body,
        name=name,
        grid=(R // tr,),
        in_specs=[spec] * 4,
        out_specs=[spec] * 3,
        out_shape=[jax.ShapeDtypeStruct((R, C), F32)] * 3,
        compiler_params=_cparams("parallel"),
    )(w2, g2, m2, v2)
    return [o.reshape(shape) for o in outs]


def _pack(arrs, pad_rows_to=SUBLANES):
    parts = []
    for a in arrs:
        flat = a.reshape(-1)
        per = LANES * pad_rows_to
        padded = -(-flat.shape[0] // per) * per
        if padded != flat.shape[0]:
            flat = jnp.pad(flat, (0, padded - flat.shape[0]))
        parts.append(flat.reshape(-1, LANES))
    return jnp.concatenate(parts, axis=0)


def _unpack(packed, shapes, pad_rows_to=SUBLANES):
    out = []
    r = 0
    for shp in shapes:
        nel = math.prod(shp)
        per = LANES * pad_rows_to
        rows = -(-nel // per) * pad_rows_to
        out.append(packed[r:r + rows].reshape(-1)[:nel].reshape(shp))
        r += rows
    return out


BIG = ("lru_w_in", "lru_w_out", "pool_w_in", "pool_w_grp", "pool_w_out", "mlp_w1", "mlp_w2", "ple_w", "ple_gate_w")
BIG_AXIS = {"lru_w_in": 2, "lru_w_out": 1, "pool_w_in": 1, "pool_w_grp": 2, "pool_w_out": 1, "mlp_w1": 2,
            "mlp_w2": 1, "ple_w": 2, "ple_gate_w": 1}
SMALL_SHARDED = ("lru_conv_w", "pool_b_grp", "pool_scale")
REPLICATED = ("lru_conv_b", "lru_wa", "lru_ba", "lru_wx", "lru_bx", "lru_lambda", "ln_mix_g", "ln_mix_b",
              "ln_mlp_g", "ln_mlp_b", "ple_gate_b")
WEIGHTS = ("lru_w_in", "lru_conv_w", "lru_conv_b", "lru_wa", "lru_ba", "lru_wx", "lru_bx", "lru_lambda", "lru_w_out",
           "pool_w_in", "pool_w_grp", "pool_b_grp", "pool_scale", "pool_w_out", "ln_mix_g", "ln_mix_b", "mlp_w1",
           "mlp_w2", "ln_mlp_g", "ln_mlp_b", "ple_w", "ple_gate_w", "ple_gate_b")
INPUTS = ("x", "p") + WEIGHTS + ("loss_target",) + tuple("m_" + n for n in WEIGHTS) + tuple("v_" + n for n in WEIGHTS)


def _gather_last_axis(packed_full, shard_shape):
    nel = math.prod(shard_shape)
    blocks = packed_full.reshape(N_DEV, -1)[:, :nel].reshape((N_DEV,) + tuple(shard_shape))
    return jnp.concatenate([blocks[d] for d in range(N_DEV)], axis=-1)


def kernel(x, p, lru_w_in, lru_conv_w, lru_conv_b, lru_wa, lru_ba, lru_wx, lru_bx, lru_lambda, lru_w_out, pool_w_in, pool_w_grp, pool_b_grp, pool_scale, pool_w_out, ln_mix_g, ln_mix_b, mlp_w1, mlp_w2, ln_mlp_g, ln_mlp_b, ple_w, ple_gate_w, ple_gate_b, loss_target, m_lru_w_in, m_lru_conv_w, m_lru_conv_b, m_lru_wa, m_lru_ba, m_lru_wx, m_lru_bx, m_lru_lambda, m_lru_w_out, m_pool_w_in, m_pool_w_grp, m_pool_b_grp, m_pool_scale, m_pool_w_out, m_ln_mix_g, m_ln_mix_b, m_mlp_w1, m_mlp_w2, m_ln_mlp_g, m_ln_mlp_b, m_ple_w, m_ple_gate_w, m_ple_gate_b, v_lru_w_in, v_lru_conv_w, v_lru_conv_b, v_lru_wa, v_lru_ba, v_lru_wx, v_lru_bx, v_lru_lambda, v_lru_w_out, v_pool_w_in, v_pool_w_grp, v_pool_b_grp, v_pool_scale, v_pool_w_out, v_ln_mix_g, v_ln_mix_b, v_mlp_w1, v_mlp_w2, v_ln_mlp_g, v_ln_mlp_b, v_ple_w, v_ple_gate_w, v_ple_gate_b):
    A = dict(zip(INPUTS, (x, p, lru_w_in, lru_conv_w, lru_conv_b, lru_wa, lru_ba, lru_wx, lru_bx, lru_lambda, lru_w_out, pool_w_in, pool_w_grp, pool_b_grp, pool_scale, pool_w_out, ln_mix_g, ln_mix_b, mlp_w1, mlp_w2, ln_mlp_g, ln_mlp_b, ple_w, ple_gate_w, ple_gate_b, loss_target, m_lru_w_in, m_lru_conv_w, m_lru_conv_b, m_lru_wa, m_lru_ba, m_lru_wx, m_lru_bx, m_lru_lambda, m_lru_w_out, m_pool_w_in, m_pool_w_grp, m_pool_b_grp, m_pool_scale, m_pool_w_out, m_ln_mix_g, m_ln_mix_b, m_mlp_w1, m_mlp_w2, m_ln_mlp_g, m_ln_mlp_b, m_ple_w, m_ple_gate_w, m_ple_gate_b, v_lru_w_in, v_lru_conv_w, v_lru_conv_b, v_lru_wa, v_lru_ba, v_lru_wx, v_lru_bx, v_lru_lambda, v_lru_w_out, v_pool_w_in, v_pool_w_grp, v_pool_b_grp, v_pool_scale, v_pool_w_out, v_ln_mix_g, v_ln_mix_b, v_mlp_w1, v_mlp_w2, v_ln_mlp_g, v_ln_mlp_b, v_ple_w, v_ple_gate_w, v_ple_gate_b)))
    depth = ln_mix_g.shape[0]
    alpha = (2 * depth) ** 0.25
    S, D = x.shape[1], x.shape[2]
    xs = x.reshape(S, D)
    tgt = loss_target.reshape(S, D)
    p3 = p.reshape(depth, S, p.shape[-1])
    me = 4 * lax.axis_index("x") + 2 * lax.axis_index("y") + lax.axis_index("c")

    def layer_weights(i):
        s = i // 2
        mixer = ("lru_w_in", "lru_w_out") if i % 2 == 0 else ("pool_w_in", "pool_w_grp", "pool_w_out")
        return [(n, s) for n in mixer] + [(n, i) for n in ("mlp_w1", "mlp_w2", "ple_w", "ple_gate_w")]

    def axis_of(key):
        return 0 if key[0] == "small" else BIG_AXIS[key[0]] - 1

    def start_gather(tag, keys, after):
        axes = [axis_of(k) for k in keys]
        lands = [land[k] for k in keys]
        sizes = [a.shape[ax] // N_DEV for a, ax in zip(lands, axes)]
        return keys, _split_start(f"gather_{tag}_start", "gather", [], lands, axes, sizes, after=after), axes, sizes

    def finish_gather(tag, pending, after):
        keys, handle, axes, sizes = pending
        for (n, l), full in zip(keys, _split_wait(f"gather_{tag}_wait", "gather", handle, axes, sizes, after)):
            W[n][l] = full

    def start_exchange(tag, keys, arrs, after):
        axes = [axis_of(k) for k in keys]
        sizes = [a.shape[ax] // N_DEV for a, ax in zip(arrs, axes)]
        lands = [_own_block_placed(a, ax, sz, me) for a, ax, sz in zip(arrs, axes, sizes)]
        return keys, _split_start(f"exchange_{tag}_start", "scatter", arrs, lands, axes, sizes, after=after), axes, sizes

    def finish_exchange(tag, pending, after):
        keys, handle, axes, sizes = pending
        partial.update(zip(keys, _split_wait(f"exchange_{tag}_wait", "scatter", handle, axes, sizes, after)))

    me1 = jnp.reshape(me, (1,)).astype(jnp.int32)
    land = {}
    W = {n: [None] * A[n].shape[0] for n in BIG}
    small_shard_shapes = [A[n].shape for n in SMALL_SHARDED]
    gathered = _all_gather("gather_small_params", [_pack([A[n] for n in SMALL_SHARDED])], [0])
    def gather_groups(i):
        keys = layer_weights(i)
        mixer, (w1, w2, pw, pg) = keys[:-4], keys[-4:]
        if i % 2 == 0:
            return [("in", mixer[:1]), ("out", mixer[1:]), ("up", [w1]), ("rest", [w2, pw, pg])]
        return [("in", mixer + [w1]), ("rest", [w2, pw, pg])]

    gather_pending = {}

    def send_layer(i, behind):
        for tag, keys in gather_groups(i):
            for k in keys:
                land[k] = _cast_into_window(f"cast_{k[0]}_{k[1]}", A[k[0]], k[1], axis_of(k), me1, after=behind)
            gather_pending[(i, tag)] = start_gather(f"l{i}_{tag}", keys, behind)
            behind = (gather_pending[(i, tag)][1][4],)
        return behind

    layer0_started = send_layer(0, (gathered[0],))
    small_full = gathered[0].reshape(N_DEV, -1, LANES)
    r = 0
    for n, shp in zip(SMALL_SHARDED, small_shard_shapes):
        rows = -(-math.prod(shp) // (LANES * SUBLANES)) * SUBLANES
        W[n] = _gather_last_axis(small_full[:, r:r + rows], shp)
        r += rows
    wa_b, wx_b = _to_bf16("cast_gates", [lru_wa, lru_wx], after=layer0_started)
    n_lru = lru_w_in.shape[0]
    lru_par = [jnp.concatenate([W["lru_conv_w"][s], lru_conv_b[s][None], lru_ba[s][None], lru_bx[s][None],
                                lru_lambda[s][None]], axis=0) for s in range(n_lru)]
    pool_par = [jnp.stack([W["pool_b_grp"][s], W["pool_scale"][s]], axis=0) for s in range(pool_w_in.shape[0])]

    saved = []
    h_in = xs
    (h_in_b,) = _to_bf16("cast_inputs", [xs], after=layer0_started)
    layer1_started = send_layer(1, (layer0_started[0], h_in_b))
    finish_gather("l0_in", gather_pending[(0, "in")], [h_in_b, wa_b, layer1_started[0]])
    for i in range(depth):
        s = i // 2
        sv = {"x0b": h_in_b}
        if i > 0:
            finish_gather(f"l{i}_in", gather_pending[(i, "in")], h_in)
        ln_out = dict(out_dtypes=[F32, F32, BF16], tm=MM_TM_ROWS, tn=D,
                      epi=lambda acc, xp, g, b: _ln_apply(alpha * xp + acc, g, b))
        if i % 2 == 0:
            sv["proj"] = _mm(f"l{i}_lru_in", h_in_b, W["lru_w_in"], "nn", [F32], b_lead=s)
            if i == 0:
                behind = (layer1_started[0], sv["proj"])
                for later in range(2, depth):
                    behind = send_layer(later, behind)
                all_started = behind[0]
            sv["gh"], sv["h"], *sv["gates"] = _lru_fwd(f"l{i}_lru_core", sv["proj"], lru_par[s], wa_b[s], wx_b[s])
            finish_gather(f"l{i}_out", gather_pending[(i, "out")], [sv["gh"], all_started])
            mix_in, mix_w = sv["gh"], W["lru_w_out"]
        else:
            sv["u"] = _mm(f"l{i}_pool_in", h_in_b, W["pool_w_in"], "nn", [F32], b_lead=s)
            sv["zs"] = _pool_fwd(f"l{i}_pool_core", sv["u"], W["pool_w_grp"][s], pool_par[s])
            mix_in, mix_w = sv["zs"], W["pool_w_out"]
        sv["z1"], sv["x1"], sv["x1b"] = _mm(f"l{i}_mix_out_ln", mix_in, mix_w, "nn", b_lead=s,
                                            extras=[h_in, ln_mix_g[i][None], ln_mix_b[i][None]], **ln_out)
        if i % 2 == 0:
            finish_gather(f"l{i}_up", gather_pending[(i, "up")], sv["x1b"])
        sv["hpre"], sv["hact"] = _mm(f"l{i}_mlp_up", sv["x1b"], W["mlp_w1"], "nn", [BF16, BF16], b_lead=i,
                                     epi=lambda acc: (acc, jnp.square(jnp.maximum(acc, 0.0))))
        finish_gather(f"l{i}_rest", gather_pending[(i, "rest")], sv["hact"])
        sv["z2"], sv["x2"], sv["x2b"] = _mm(f"l{i}_mlp_down_ln", sv["hact"], W["mlp_w2"], "nn", b_lead=i,
                                            extras=[sv["x1"], ln_mlp_g[i][None], ln_mlp_b[i][None]], **ln_out)
        sv["pp"] = _mm(f"l{i}_ple_up", p3, W["ple_w"], "nn", [F32], a_lead=i, b_lead=i)

        def ple_epi(acc, bg, x2t, ppt):
            gpre = acc + bg
            x3 = x2t + ppt * jax.nn.sigmoid(gpre)
            return x3, x3, gpre

        h_in, h_in_b, sv["gpre"] = _mm(f"l{i}_ple_gate", sv["x2b"], W["ple_gate_w"], "nn", [F32, BF16, F32], b_lead=i,
                                       epi=ple_epi, extras=[ple_gate_b[i][None], sv["x2"], sv["pp"]], tn=MM_TN)
        saved.append(sv)

    dx, dpp, dgpre, sq, dbg = _loss_and_grad("loss", h_in, tgt, saved[-1]["gpre"], saved[-1]["pp"])
    loss = lax.psum(0.5 * sq[0, 0] / D, ("x", "y", "c"))

    dW = {n: [None] * A[n].shape[0] for n in BIG}
    dsmall = {n: [None] * A[n].shape[0] for n in REPLICATED + SMALL_SHARDED}
    small_names = REPLICATED + SMALL_SHARDED
    partial = {}
    exchange_pending = {}
    exchange_token = ()
    for i in reversed(range(depth)):
        s = i // 2
        sv = saved[i]
        dsmall["ple_gate_b"][i] = dbg[0]
        dW["ple_w"][i] = _mm(f"l{i}_d_ple_w", p3, dpp, "tn", [BF16], a_lead=i, after=exchange_token)
        dW["ple_gate_w"][i] = _mm(f"l{i}_d_ple_gate_w", sv["x2b"], dgpre, "tn", [BF16])
        ln_back = dict(out_dtypes=[F32, BF16], tm=MM_TM_ROWS, tn=D, n_sums=2)
        dz2, dz2b, dg, db = _mm(f"l{i}_d_x2_ln", dgpre, W["ple_gate_w"], "nt", b_lead=i,
                                extras=[dx, sv["z2"], ln_mlp_g[i][None]],
                                epi=lambda acc, d, z, g: _ln_grad(acc + d, z, g), after=exchange_token, **ln_back)
        dsmall["ln_mlp_g"][i], dsmall["ln_mlp_b"][i] = dg[0], db[0]
        dhpre = _mm(f"l{i}_d_hpre", dz2b, W["mlp_w2"], "nt", [BF16], b_lead=i, extras=[sv["hpre"]],
                    epi=lambda acc, hp: (acc * (2.0 * jnp.maximum(hp.astype(F32), 0.0)),))
        dW["mlp_w2"][i] = _mm(f"l{i}_d_mlp_w2", sv["hact"], dz2b, "tn", [BF16])
        dW["mlp_w1"][i] = _mm(f"l{i}_d_mlp_w1", sv["x1b"], dhpre, "tn", [BF16])
        mlp_after = ()
        if i == 0:
            early = [(n, 0) for n in ("ple_w", "ple_gate_w", "mlp_w2", "mlp_w1")]
            exchange_early0 = start_exchange("early0", early, [dW[n][l] for n, l in early], ())
            mlp_after = (exchange_early0[1][4],)
        dz1, dz1b, dg, db = _mm(f"l{i}_d_x1_ln", dhpre, W["mlp_w1"], "nt", b_lead=i,
                                extras=[dz2, sv["z1"], ln_mix_g[i][None]],
                                epi=lambda acc, d, z, g: _ln_grad(acc + alpha * d, z, g), after=mlp_after, **ln_back)
        dsmall["ln_mix_g"][i], dsmall["ln_mix_b"][i] = dg[0], db[0]
        if i % 2 == 0:
            dW["lru_w_out"][s] = _mm(f"l{i}_d_lru_w_out", sv["gh"], dz1b, "tn", [BF16])
            dgh = _mm(f"l{i}_d_gh", dz1b, W["lru_w_out"], "nt", [F32], b_lead=s)
            dup, dy, dwa, dwx, dpar = _lru_bwd(f"l{i}_lru_core_bwd", sv["proj"], sv["h"], dgh, sv["gates"],
                                               lru_par[s], wa_b[s], wx_b[s])
            dsmall["lru_wa"][s], dsmall["lru_wx"][s] = dwa, dwx
            dsmall["lru_conv_w"][s] = dpar[0:4]
            for k, n in enumerate(("lru_conv_b", "lru_ba", "lru_bx", "lru_lambda")):
                dsmall[n][s] = dpar[4 + k]
            dmix_in = jnp.concatenate([dup, dy], axis=1)
            win = "lru_w_in"
        else:
            dW["pool_w_out"][s] = _mm(f"l{i}_d_pool_w_out", sv["zs"], dz1b, "tn", [BF16])
            dzs = _mm(f"l{i}_d_zs", dz1b, W["pool_w_out"], "nt", [F32], b_lead=s)
            dmix_in, dW["pool_w_grp"][s], dpar = _pool_bwd(f"l{i}_pool_core_bwd", sv["u"], dzs, W["pool_w_grp"][s],
                                                          pool_par[s])
            dsmall["pool_b_grp"][s], dsmall["pool_scale"][s] = dpar[0], dpar[1]
            win = "pool_w_in"
        dW[win][s] = _mm(f"l{i}_d_{win}", sv["x0b"], dmix_in, "tn", [BF16])
        x0_after = ()
        if i > 0:
            keys = layer_weights(i)
            exchange_pending[i] = start_exchange(f"l{i}", keys, [dW[n][l] for n, l in keys], ())
            exchange_token = (exchange_pending[i][1][4],)
        else:
            small_grads = [jnp.stack(dsmall[n]) for n in small_names]
            small_shapes = [g.shape for g in small_grads]
            packed_g = _pack(small_grads)
            assert packed_g.shape[0] % (N_DEV * SUBLANES) == 0, packed_g.shape
            late = [("lru_w_out", 0), ("lru_w_in", 0), ("small", 0)]
            exchange_late0 = start_exchange("late0", late, [dW["lru_w_out"][0], dW["lru_w_in"][0], packed_g], ())
            x0_after = (exchange_late0[1][4],)
        if i > 0:
            def x0_epi(acc, d, gpre, pp):
                dxv = acc + alpha * d
                return (dxv,) + _ple_grad(dxv, gpre, pp)

            dx, dpp, dgpre, dbg = _mm(f"l{i}_d_x0_ple", dmix_in, W[win], "nt", [F32, BF16, BF16], b_lead=s,
                                      extras=[dz1, saved[i - 1]["gpre"], saved[i - 1]["pp"]], epi=x0_epi,
                                      tm=MM_TM_ROWS, tn=D, n_sums=1)
        else:
            dx = _mm(f"l{i}_d_x0", dmix_in, W[win], "nt", [F32], b_lead=s, extras=[dz1],
                     epi=lambda acc, d: (acc + alpha * d,), after=x0_after)
    grad_x = dx.reshape(x.shape)

    for i in range(1, depth):
        finish_exchange(f"l{i}", exchange_pending[i], x0_after[0])
    stacked = {n: None for n in BIG}
    layer0 = layer_weights(0)

    def adamw(n, l, after=()):
        stacked[n] = _adamw_layer(f"adamw_{n}_{l}", partial[(n, l)], A[n], A["m_" + n], A["v_" + n], l, stacked[n],
                                  after=after)

    for n in BIG:
        for l in reversed(range(A[n].shape[0])):
            if (n, l) not in layer0:
                adamw(n, l)
    behind = [dx] + [stacked[n][0] for n in BIG if stacked[n] is not None]
    finish_exchange("early0", exchange_early0, behind)
    finish_exchange("late0", exchange_late0, behind)
    red = _sum8("sum_small", partial[("small", 0)])
    rows = red.shape[0]
    red_land = lax.dynamic_update_slice_in_dim(lax.empty((N_DEV * rows, LANES), F32), red, me * rows, 0)
    small_handle = _split_start("gather_small_start", "gather", [], [red_land], [0], [rows])
    for n, l in layer0:
        adamw(n, l, after=(small_handle[4],))
    outs = {n: [o.reshape(A[n].shape) for o in stacked[n]] for n in BIG}
    red_full = _split_wait("gather_small_wait", "gather", small_handle, [0], [rows], [stacked[n][0] for n, _ in layer0])[0]
    small_g = dict(zip(small_names, _unpack(red_full, small_shapes)))
    for n in SMALL_SHARDED:
        width = A[n].shape[-1]
        small_g[n] = lax.dynamic_slice_in_dim(small_g[n], me * width, width, axis=small_g[n].ndim - 1)
    for n in small_names:
        outs[n] = [small_g[n]] + _adamw_small(f"adamw_{n}", A[n], small_g[n], A["m_" + n], A["v_" + n])

    return (loss, grad_x, *[outs[n][0] for n in WEIGHTS], *[outs[n][1] for n in WEIGHTS],
            *[outs[n][2] for n in WEIGHTS], *[outs[n][3] for n in WEIGHTS])
```

```python
import functools
import math

import jax
import jax.numpy as jnp
from jax import lax
from jax.experimental import pallas as pl
from jax.experimental.pallas import tpu as pltpu

F32 = jnp.float32
BF16 = jnp.bfloat16
MESH = pl.DeviceIdType.MESH
N_DEV = 8
LANES = 128
SUBLANES = 8

LN_EPS = 1e-5
LRU_C = 8.0
CONV_WIDTH = 4
POOL_HALO = 16
ADAM_LR = 0.001
ADAM_B1 = 0.9
ADAM_B2 = 0.999
ADAM_EPS = 1e-08
ADAM_WD = 0.01
ADAM_STEP = 10

VMEM_LIMIT = 48 * 1024 * 1024
VMEM_LIMIT_SEQ = 56 * 1024 * 1024
SEQ_CHUNK = 256
MM_TK = 4096
MM_TK_TOKENS = 4096
MM_TM_ROWS = 512
MM_TN = 512
MM_TN_WIDE = 1024
MM_TN_WIDE_MAX_K = 2048
GELU_C0 = math.sqrt(2.0 / math.pi)
GELU_C1 = 0.044715


def _in_hbm(*arrays):
    return [pltpu.with_memory_space_constraint(a, pltpu.HBM) for a in arrays]


def _cparams(*sem, limit=None):
    return pltpu.CompilerParams(dimension_semantics=tuple(sem) if sem else None,
                                vmem_limit_bytes=VMEM_LIMIT if limit is None else limit)


def _tile(n, pref):
    if n <= pref:
        return n
    t = pref - pref % LANES
    while t >= LANES:
        if n % t == 0:
            return t
        t -= LANES
    return n


def _row_tile(n, pref):
    if n <= pref:
        return n
    t = pref - pref % SUBLANES
    while t >= SUBLANES:
        if n % t == 0:
            return t
        t -= SUBLANES
    return n


def _mm(name, a, b, mode, out_dtypes, epi=None, extras=(), a_lead=None, b_lead=None, tm=1024, tn=None, tk=None,
        after=(), n_sums=0):
    if isinstance(b, (list, tuple)):
        b, b_lead = b[b_lead], None
    a2 = a.shape[-2:]
    b2 = b.shape[-2:]
    if mode == "nn":
        (M, K), N = a2, b2[1]
        assert b2[0] == K
    elif mode == "nt":
        (M, K), N = a2, b2[0]
        assert b2[1] == K
    else:
        (K, M), N = a2, b2[1]
        assert b2[0] == K
    if tk is None:
        tk = MM_TK_TOKENS if mode == "tn" else MM_TK
    if tn is None:
        tn = MM_TN_WIDE if (mode != "tn" and K <= MM_TN_WIDE_MAX_K) else MM_TN
    tm, tn, tk = _tile(M, tm), _tile(N, tn), _tile(K, tk)
    nk = K // tk
    n_extra = len(extras)
    n_out = len(out_dtypes)
    assert n_sums == 0 or tn == N
    resident = {"pipeline_mode": pl.Buffered(1)} if (tn == N and nk == 1) else {}

    def lead(shape, idx, which, **kw):
        if which is None:
            return pl.BlockSpec(shape, idx, **kw)
        return pl.BlockSpec((None,) + shape, lambda i, j, k: (which,) + idx(i, j, k), **kw)

    if mode == "nn":
        a_spec = lead((tm, tk), lambda i, j, k: (i, k), a_lead)
        b_spec = lead((tk, tn), lambda i, j, k: (k, j), b_lead, **resident)
        dims = (((1,), (0,)), ((), ()))
    elif mode == "nt":
        a_spec = lead((tm, tk), lambda i, j, k: (i, k), a_lead)
        b_spec = lead((tn, tk), lambda i, j, k: (j, k), b_lead, **resident)
        dims = (((1,), (1,)), ((), ()))
    else:
        a_spec = lead((tk, tm), lambda i, j, k: (k, i), a_lead)
        b_spec = lead((tk, tn), lambda i, j, k: (k, j), b_lead, **resident)
        dims = (((0,), (0,)), ((), ()))
    e_specs = []
    for e in extras:
        if e.shape[0] == 1:
            e_specs.append(pl.BlockSpec((1, tn), lambda i, j, k: (0, j)))
        else:
            e_specs.append(pl.BlockSpec((tm, tn), lambda i, j, k: (i, j)))

    n_after = len(after)

    def body(a_ref, b_ref, *rest):
        e_refs = rest[:n_extra]
        rest = rest[:n_extra] + rest[n_extra + n_after:]
        o_refs = rest[n_extra:n_extra + n_out]
        s_refs = rest[n_extra + n_out:n_extra + n_out + n_sums]
        part = lax.dot_general(a_ref[...].astype(BF16), b_ref[...].astype(BF16), dims, preferred_element_type=F32)

        def finish(r):
            res = (r,) if epi is None else epi(r, *[e[...] for e in e_refs])
            for o, v in zip(o_refs, res[:n_out]):
                o[...] = v.astype(o.dtype)
            first = pl.program_id(0) == 0
            for sr, v in zip(s_refs, res[n_out:]):
                @pl.when(first)
                def _(sr=sr, v=v):
                    sr[...] = v

                @pl.when(jnp.logical_not(first))
                def _(sr=sr, v=v):
                    sr[...] += v

        if nk == 1:
            finish(part)
            return
        acc = rest[n_extra + n_out + n_sums]
        k = pl.program_id(2)

        @pl.when(k == 0)
        def _():
            acc[...] = part

        @pl.when(jnp.logical_and(k > 0, k < nk - 1))
        def _():
            acc[...] += part

        @pl.when(k == nk - 1)
        def _():
            finish(acc[...] + part)

    outs = pl.pallas_call(
        body,
        name=name,
        grid=(M // tm, N // tn, nk),
        in_specs=[a_spec, b_spec] + e_specs + [pl.BlockSpec(memory_space=pl.ANY)] * n_after,
        out_specs=[pl.BlockSpec((tm, tn), lambda i, j, k: (i, j)) for _ in out_dtypes]
        + [pl.BlockSpec((1, tn), lambda i, j, k: (0, 0))] * n_sums,
        out_shape=[jax.ShapeDtypeStruct((M, N), d) for d in out_dtypes] + [jax.ShapeDtypeStruct((1, N), F32)] * n_sums,
        scratch_shapes=[pltpu.VMEM((tm, tn), F32)] if nk > 1 else [],
        compiler_params=_cparams(*(("arbitrary",) * 3 if n_sums else ("parallel", "parallel", "arbitrary"))),
    )(*_in_hbm(a, b, *extras), *after)
    return outs[0] if n_out + n_sums == 1 else tuple(outs)


def _rowwise(name, fn, tiled, params, outs, accs=(), tm=256, after=()):
    S = tiled[0].shape[0]
    tm = _tile(S, tm)
    nt, npar, no = len(tiled), len(params), len(outs)
    n_after = len(after)

    def body(*refs):
        t_refs = refs[:nt]
        p_refs = refs[nt:nt + npar]
        refs = refs[nt + npar + n_after:]
        o_refs = refs[:no]
        a_refs = refs[no:]
        res = fn(*[r[...] for r in t_refs], *[r[...] for r in p_refs])
        for o, v in zip(o_refs, res[:no]):
            o[...] = v.astype(o.dtype)
        first = pl.program_id(0) == 0
        for ar, v in zip(a_refs, res[no:]):
            @pl.when(first)
            def _(ar=ar, v=v):
                ar[...] = v

            @pl.when(jnp.logical_not(first))
            def _(ar=ar, v=v):
                ar[...] += v

    full = lambda p: pl.BlockSpec(p.shape, lambda i, nd=p.ndim: (0,) * nd)
    res = pl.pallas_call(
        body,
        name=name,
        grid=(S // tm,),
        in_specs=[pl.BlockSpec((tm, t.shape[1]), lambda i: (i, 0)) for t in tiled] + [full(p) for p in params]
        + [pl.BlockSpec(memory_space=pl.ANY)] * n_after,
        out_specs=[pl.BlockSpec((tm, c), lambda i: (i, 0)) for c, _ in outs]
        + [pl.BlockSpec(s, lambda i, nd=len(s): (0,) * nd) for s in accs],
        out_shape=[jax.ShapeDtypeStruct((S, c), d) for c, d in outs] + [jax.ShapeDtypeStruct(s, F32) for s in accs],
        compiler_params=_cparams("arbitrary"),
    )(*_in_hbm(*tiled, *params), *after)
    return res


def _ln_stats(z):
    mu = jnp.mean(z, axis=-1, keepdims=True)
    zc = z - mu
    var = jnp.mean(zc * zc, axis=-1, keepdims=True)
    return zc, lax.rsqrt(var + LN_EPS)


def _ln_apply(z, g, b):
    zc, rstd = _ln_stats(z)
    y = zc * rstd * g + b
    return z, y, y


def _ln_grad(dy, z, g):
    zc, rstd = _ln_stats(z)
    xhat = zc * rstd
    dxh = dy * g
    m1 = jnp.mean(dxh, axis=-1, keepdims=True)
    m2 = jnp.mean(dxh * xhat, axis=-1, keepdims=True)
    dz = rstd * (dxh - m1 - xhat * m2)
    return dz, dz, jnp.sum(dy * xhat, axis=0, keepdims=True), jnp.sum(dy, axis=0, keepdims=True)


def _ple_grad(dx3, gpre, pp):
    gate = jax.nn.sigmoid(gpre)
    dgpre = dx3 * pp * gate * (1.0 - gate)
    return dx3 * gate, dgpre, jnp.sum(dgpre, axis=0, keepdims=True)


def _loss_and_grad(name, y, target, gpre, pp):
    d = y.shape[1]

    def fn(y, t, gpre, pp):
        err = y - t
        sq = jnp.sum(jnp.sum(err * err, axis=0, keepdims=True), axis=1, keepdims=True)
        dy = err * (1.0 / d)
        dpp, dgpre, dbg = _ple_grad(dy, gpre, pp)
        return dy, dpp, dgpre, jnp.broadcast_to(sq, (1, LANES)), dbg

    return _rowwise(name, fn, [y, target, gpre, pp], [], [(d, F32), (d, BF16), (d, BF16)], accs=[(1, LANES), (1, d)])


def _rows(shape):
    return lax.broadcasted_iota(jnp.int32, shape, 0)


def _gelu(y):
    t = jnp.tanh(GELU_C0 * (y + GELU_C1 * y * y * y))
    return 0.5 * y * (1.0 + t), t


def _gelu_grad(y, t):
    return 0.5 * (1.0 + t) + 0.5 * y * (1.0 - t * t) * GELU_C0 * (1.0 + 3.0 * GELU_C1 * y * y)


def _softplus(x):
    return jnp.maximum(x, 0.0) + jnp.log(1.0 + jnp.exp(-jnp.abs(x)))


def _conv_fwd(xs, cw, cb):
    n = xs.shape[0]
    u = cw[3:4] * xs
    for k in (1, 2, 3):
        u = u + cw[3 - k:4 - k] * pltpu.roll(xs, k, 0)
    del n
    return u[SUBLANES:] + cb


def _lru_gates(u, wa, wx, ba, bx, sp, grow):
    ub = u.astype(BF16)
    r = jax.nn.sigmoid(jnp.dot(ub, wa, preferred_element_type=F32) + ba)
    ig = jax.nn.sigmoid(jnp.dot(ub, wx, preferred_element_type=F32) + bx)
    log_a = (-LRU_C) * r * sp
    a = jnp.exp(log_a)
    mult = jnp.sqrt(jnp.tanh(-log_a) * (1.0 + a * a))
    mult = jnp.where(grow == 0, 1.0, mult)
    return ub, r, ig, a, mult


def _scan8_fwd(a, b):
    row = _rows(a.shape)
    for k in (1, 2, 4):
        m = row >= k
        b = jnp.where(m, a * pltpu.roll(b, k, 0) + b, b)
        a = jnp.where(m, a * pltpu.roll(a, k, 0), a)
    return a, b


def _scan8_bwd(c, d):
    row = _rows(c.shape)
    for k in (1, 2, 4):
        m = row < SUBLANES - k
        d = jnp.where(m, c * pltpu.roll(d, SUBLANES - k, 0) + d, d)
        c = jnp.where(m, c * pltpu.roll(c, SUBLANES - k, 0), c)
    return c, d


def _pad_copy(dst, src, front, back):
    s, c = src.shape
    if front:
        dst[pl.ds(0, front), :] = jnp.zeros((front, c), dst.dtype)
    if back:
        dst[pl.ds(front + s, back), :] = jnp.zeros((back, c), dst.dtype)
    dst[pl.ds(front, s), :] = src[...].astype(dst.dtype)


def _lru_fwd(name, proj, par, wa, wx):
    S = proj.shape[0]
    R = proj.shape[1] // 2
    H = R // LANES
    ch = _tile(S, SEQ_CHUNK)
    nch = S // ch
    H8 = SUBLANES

    def body(up_ref, y_ref, par_ref, wa_ref, wx_ref, gh_ref, h_ref, r_ref, ig_ref, a_ref, mult_ref, gy_ref, dgy_ref,
             up_pad):
        _pad_copy(up_pad, up_ref, H8, 0)
        par = par_ref[...]
        cw, cb, ba, bx = par[0:4], par[4:5], par[5:6], par[6:7]
        sp = _softplus(-par[7:8])
        wa_m, wx_m = wa_ref[...], wx_ref[...]

        def chunk(ci, carry):
            r0 = pl.multiple_of(ci * ch, ch)
            xs = up_pad[pl.ds(r0, ch + H8), :]
            u = _conv_fwd(xs, cw, cb)
            grow = _rows(u.shape) + r0
            _, r, ig, a, mult = _lru_gates(u, wa_m, wx_m, ba, bx, sp, grow)
            for ref, val in ((r_ref, r), (ig_ref, ig), (a_ref, a), (mult_ref, mult)):
                ref[pl.ds(r0, ch), :] = val
            bt = mult * (ig * u)
            hs = []
            for j in range(ch // H8):
                aa, bb = _scan8_fwd(a[j * H8:(j + 1) * H8], bt[j * H8:(j + 1) * H8])
                hj = bb + aa * carry
                carry = hj[H8 - 1:H8]
                hs.append(hj)
            h = jnp.concatenate(hs, axis=0)
            h_ref[pl.ds(r0, ch), :] = h
            y = y_ref[pl.ds(r0, ch), :]
            gy, t = _gelu(y)
            gy_ref[pl.ds(r0, ch), :] = gy
            dgy_ref[pl.ds(r0, ch), :] = _gelu_grad(y, t)
            gh_ref[pl.ds(r0, ch), :] = (h * gy).astype(gh_ref.dtype)
            return carry

        lax.fori_loop(0, nch, chunk, jnp.zeros((1, LANES), F32))

    col = lambda off: pl.BlockSpec((S, LANES), lambda h: (0, h + off))
    return pl.pallas_call(
        body,
        name=name,
        grid=(H,),
        in_specs=[col(0), col(H), pl.BlockSpec((8, LANES), lambda h: (0, h)),
                  pl.BlockSpec((None, LANES, LANES), lambda h: (h, 0, 0)),
                  pl.BlockSpec((None, LANES, LANES), lambda h: (h, 0, 0))],
        out_specs=[col(0)] * 8,
        out_shape=[jax.ShapeDtypeStruct((S, R), BF16)] + [jax.ShapeDtypeStruct((S, R), F32)] * 7,
        scratch_shapes=[pltpu.VMEM((S + H8, LANES), F32)],
        compiler_params=_cparams("parallel"),
    )(proj, proj, par, wa, wx)


def _lru_bwd(name, proj, h, dgh, gates, par, wa, wx):
    S = proj.shape[0]
    R = proj.shape[1] // 2
    H = R // LANES
    ch = _tile(S, SEQ_CHUNK)
    nch = S // ch
    H8 = SUBLANES
    nb = ch // H8

    def body(up_ref, h_ref, dgh_ref, r_ref, ig_ref, a_ref, mult_ref, gy_ref, dgy_ref, par_ref, wa_ref, wx_ref,
             dup_ref, dy_ref, dwa_ref, dwx_ref, dpar_ref, up_pad, du_pad, vec_acc):
        _pad_copy(up_pad, up_ref, H8, 0)
        du_pad[pl.ds(S, H8), :] = jnp.zeros((H8, LANES), F32)
        par = par_ref[...]
        cw, cb, lam = par[0:4], par[4:5], par[7:8]
        sp = _softplus(-lam)
        wa_m, wx_m = wa_ref[...], wx_ref[...]
        dwa_ref[...] = jnp.zeros_like(dwa_ref)
        dwx_ref[...] = jnp.zeros_like(dwx_ref)
        vec_acc[...] = jnp.zeros_like(vec_acc)
        nt_dims = (((1,), (1,)), ((), ()))
        tn_dims = (((0,), (0,)), ((), ()))

        def chunk(it, carry):
            lam_next, a_next = carry
            ci = nch - 1 - it
            r0 = pl.multiple_of(ci * ch, ch)
            xs = up_pad[pl.ds(r0, ch + H8), :]
            u = _conv_fwd(xs, cw, cb)
            row = _rows(u.shape)
            grow = row + r0
            ub = u.astype(BF16)
            here = pl.ds(r0, ch)
            r, ig, a, mult = r_ref[here, :], ig_ref[here, :], a_ref[here, :], mult_ref[here, :]
            hcur = h_ref[here, :]
            before = h_ref[pl.ds(pl.multiple_of(jnp.maximum(r0 - H8, 0), H8), H8), :][H8 - 1:H8]
            hprev = jnp.where(row == 0, jnp.where(ci > 0, before, 0.0), pltpu.roll(hcur, 1, 0))
            dgh = dgh_ref[here, :]
            dy_ref[here, :] = (dgh * hcur * dgy_ref[here, :]).astype(dy_ref.dtype)
            dh = dgh * gy_ref[here, :]
            c = jnp.where(row == ch - 1, a_next, pltpu.roll(a, ch - 1, 0))
            ls = [None] * nb
            for j in range(nb - 1, -1, -1):
                cc, dd = _scan8_bwd(c[j * H8:(j + 1) * H8], dh[j * H8:(j + 1) * H8])
                lj = dd + cc * lam_next
                lam_next = lj[0:1]
                ls[j] = lj
            lmb = jnp.concatenate(ls, axis=0)
            da = lmb * hprev
            gu = ig * u
            dmult = lmb * gu
            dlog_a = da * a + jnp.where(grow == 0, 0.0, dmult * (-(a * a) / mult))
            dr = dlog_a * ((-LRU_C) * sp)
            drp = dr * r * (1.0 - r)
            dip = (lmb * mult * u) * ig * (1.0 - ig)
            drb, dib = drp.astype(BF16), dip.astype(BF16)
            du = (lmb * mult * ig
                  + lax.dot_general(drb, wa_m, nt_dims, preferred_element_type=F32)
                  + lax.dot_general(dib, wx_m, nt_dims, preferred_element_type=F32))
            du_pad[pl.ds(r0, ch), :] = du
            dwa_ref[...] += lax.dot_general(ub, drb, tn_dims, preferred_element_type=F32)
            dwx_ref[...] += lax.dot_general(ub, dib, tn_dims, preferred_element_type=F32)
            ssum = lambda v: jnp.sum(v, axis=0, keepdims=True)
            vec_acc[0:1, :] += ssum(drp)
            vec_acc[1:2, :] += ssum(dip)
            vec_acc[2:3, :] += ssum(dlog_a * ((-LRU_C) * r))
            return lam_next, a[0:1]

        zero = jnp.zeros((1, LANES), F32)
        lax.fori_loop(0, nch, chunk, (zero, zero))

        def conv_chunk(ci, acc):
            r0 = pl.multiple_of(ci * ch, ch)
            ds = du_pad[pl.ds(r0, ch + H8), :]
            xs = up_pad[pl.ds(r0, ch + H8), :]
            n = ch + H8
            du = ds[:ch]
            dup = cw[3:4] * du
            new = [acc[3] + jnp.sum(du * xs[H8:], axis=0, keepdims=True)]
            for k in (1, 2, 3):
                dup = dup + cw[3 - k:4 - k] * pltpu.roll(ds, n - k, 0)[:ch]
                new.append(acc[3 - k] + jnp.sum(du * pltpu.roll(xs, k, 0)[H8:], axis=0, keepdims=True))
            dup_ref[pl.ds(r0, ch), :] = dup.astype(dup_ref.dtype)
            return (new[3], new[2], new[1], new[0], acc[4] + jnp.sum(du, axis=0, keepdims=True))

        acc = lax.fori_loop(0, nch, conv_chunk, (zero,) * 5)
        dlam = vec_acc[2:3, :] * (-jax.nn.sigmoid(-lam))
        dpar_ref[...] = jnp.concatenate(list(acc) + [vec_acc[0:1, :], vec_acc[1:2, :], dlam], axis=0)

    col = lambda off: pl.BlockSpec((S, LANES), lambda h: (0, h + off))
    head = pl.BlockSpec((None, LANES, LANES), lambda h: (h, 0, 0))
    return pl.pallas_call(
        body,
        name=name,
        grid=(H,),
        in_specs=[col(0)] * 9 + [pl.BlockSpec((8, LANES), lambda h: (0, h)), head, head],
        out_specs=[col(0), col(0), head, head, pl.BlockSpec((8, LANES), lambda h: (0, h))],
        out_shape=[jax.ShapeDtypeStruct((S, R), BF16), jax.ShapeDtypeStruct((S, R), BF16),
                   jax.ShapeDtypeStruct((H, LANES, LANES), F32), jax.ShapeDtypeStruct((H, LANES, LANES), F32),
                   jax.ShapeDtypeStruct((8, R), F32)],
        scratch_shapes=[pltpu.VMEM((S + H8, LANES), F32), pltpu.VMEM((S + H8, LANES), F32),
                        pltpu.VMEM((8, LANES), F32)],
        compiler_params=_cparams("parallel", limit=VMEM_LIMIT_SEQ),
    )(proj, h, dgh, *gates, par, wa, wx)


def _window_sum(xs, g, up):
    n = xs.shape[0]
    s = xs
    for lvl, k in enumerate((1, 2, 4, 8)):
        sh = pltpu.roll(s, (n - k) if up else k, 0)
        s = s + jnp.where(g >= lvl, sh, 0.0)
    return s


def _pool_count(grow, g):
    return jnp.minimum(grow + 1, lax.shift_left(jnp.int32(2), g)).astype(F32)


def _pool_fwd(name, u, wgrp, par):
    S, D = u.shape
    G, W = wgrp.shape[0], wgrp.shape[1]
    ch = _tile(S, SEQ_CHUNK)
    nch = S // ch
    PH = POOL_HALO

    def body(u_ref, w_ref, par_ref, zs_ref, u_pad):
        g = pl.program_id(0)
        _pad_copy(u_pad, u_ref, PH, 0)
        par = par_ref[...]
        w = w_ref[...]

        def chunk(ci, _):
            r0 = pl.multiple_of(ci * ch, ch)
            xs = u_pad[pl.ds(r0, ch + PH), :]
            ws = _window_sum(xs, g, False)[PH:]
            uc = xs[PH:]
            cnt = _pool_count(_rows(uc.shape) + r0, g)
            pooled = ws / cnt - uc
            z = jnp.dot(pooled.astype(BF16), w, preferred_element_type=F32) + par[0:1]
            zs_ref[pl.ds(r0, ch), :] = (z * par[1:2]).astype(zs_ref.dtype)
            return 0

        lax.fori_loop(0, nch, chunk, 0)

    return pl.pallas_call(
        body,
        name=name,
        grid=(G,),
        in_specs=[pl.BlockSpec((S, W), lambda g: (0, g)), pl.BlockSpec((None, W, W), lambda g: (g, 0, 0)),
                  pl.BlockSpec((2, W), lambda g: (0, g))],
        out_specs=pl.BlockSpec((S, W), lambda g: (0, g)),
        out_shape=jax.ShapeDtypeStruct((S, D), BF16),
        scratch_shapes=[pltpu.VMEM((S + PH, W), F32)],
        compiler_params=_cparams("parallel"),
    )(u, wgrp, par)


def _pool_bwd(name, u, dzs, wgrp, par):
    S, D = u.shape
    G, W = wgrp.shape[0], wgrp.shape[1]
    ch = _tile(S, SEQ_CHUNK)
    nch = S // ch
    PH = POOL_HALO

    def body(u_ref, dzs_ref, w_ref, par_ref, du_ref, dw_ref, dpar_ref, u_pad, q_pad, dw_acc):
        g = pl.program_id(0)
        _pad_copy(u_pad, u_ref, PH, 0)
        q_pad[pl.ds(S, PH), :] = jnp.zeros((PH, W), F32)
        par = par_ref[...]
        w = w_ref[...]
        dw_acc[...] = jnp.zeros_like(dw_acc)

        def chunk(ci, acc):
            db, dsc = acc
            r0 = pl.multiple_of(ci * ch, ch)
            xs = u_pad[pl.ds(r0, ch + PH), :]
            ws = _window_sum(xs, g, False)[PH:]
            uc = xs[PH:]
            cnt = _pool_count(_rows(uc.shape) + r0, g)
            pooled = (ws / cnt - uc).astype(BF16)
            z = jnp.dot(pooled, w, preferred_element_type=F32) + par[0:1]
            dzs = dzs_ref[pl.ds(r0, ch), :]
            dz = dzs * par[1:2]
            dzb = dz.astype(BF16)
            dw_acc[...] += lax.dot_general(pooled, dzb, (((0,), (0,)), ((), ())), preferred_element_type=F32)
            dpooled = lax.dot_general(dzb, w, (((1,), (1,)), ((), ())), preferred_element_type=F32)
            q_pad[pl.ds(r0, ch), :] = dpooled / cnt
            return (db + jnp.sum(dz, axis=0, keepdims=True), dsc + jnp.sum(dzs * z, axis=0, keepdims=True))

        zero = jnp.zeros((1, W), F32)
        db, dsc = lax.fori_loop(0, nch, chunk, (zero, zero))
        dpar_ref[...] = jnp.concatenate([db, dsc], axis=0)
        dw_ref[...] = dw_acc[...].astype(dw_ref.dtype)

        def back(ci, _):
            r0 = pl.multiple_of(ci * ch, ch)
            qs = q_pad[pl.ds(r0, ch + PH), :]
            ws = _window_sum(qs, g, True)[:ch]
            qc = qs[:ch]
            cnt = _pool_count(_rows(qc.shape) + r0, g)
            du_ref[pl.ds(r0, ch), :] = (ws - qc * cnt).astype(du_ref.dtype)
            return 0

        lax.fori_loop(0, nch, back, 0)

    blk = pl.BlockSpec((S, W), lambda g: (0, g))
    wspec = pl.BlockSpec((None, W, W), lambda g: (g, 0, 0))
    pspec = pl.BlockSpec((2, W), lambda g: (0, g))
    return pl.pallas_call(
        body,
        name=name,
        grid=(G,),
        in_specs=[blk, blk, wspec, pspec],
        out_specs=[blk, wspec, pspec],
        out_shape=[jax.ShapeDtypeStruct((S, D), BF16), jax.ShapeDtypeStruct((G, W, W), BF16),
                   jax.ShapeDtypeStruct((2, D), F32)],
        scratch_shapes=[pltpu.VMEM((S + PH, W), F32), pltpu.VMEM((S + PH, W), F32), pltpu.VMEM((W, W), F32)],
        compiler_params=_cparams("parallel"),
    )(u, dzs, wgrp, par)


def _my_place():
    x, y, c = lax.axis_index("x"), lax.axis_index("y"), lax.axis_index("c")
    return x, y, c, 4 * x + 2 * y + c


def _peers(x, y, c):
    out = []
    for d in range(1, N_DEV):
        px = 1 - x if d & 4 else x
        py = 1 - y if d & 2 else y
        pc = 1 - c if d & 1 else c
        out.append(((px, py, pc), 4 * px + 2 * py + pc))
    return out


def _window(ref, axis, start, size):
    idx = [slice(None)] * len(ref.shape)
    idx[axis] = pl.ds(start, size)
    return ref.at[tuple(idx)]


def _to_bf16(name, arrs, after=()):
    outs = []
    for i, a in enumerate(arrs):
        a2 = a.reshape(-1, a.shape[-1])
        tr = _tile(a2.shape[0], 512)
        o = pl.pallas_call(
            lambda a_ref, *rest: rest[-1].__setitem__(Ellipsis, a_ref[...].astype(BF16)),
            name=f"{name}_{i}",
            grid=(a2.shape[0] // tr,),
            in_specs=[pl.BlockSpec((tr, a2.shape[1]), lambda r: (r, 0))] + [pl.BlockSpec(memory_space=pl.ANY)] * len(after),
            out_specs=pl.BlockSpec((tr, a2.shape[1]), lambda r: (r, 0)),
            out_shape=jax.ShapeDtypeStruct(a2.shape, BF16),
            compiler_params=_cparams("parallel"),
        )(a2, *after)
        outs.append(o.reshape(a.shape))
    return outs


def _all_gather(name, shards, axes):
    n = len(shards)
    sizes = [s.shape[ax] for s, ax in zip(shards, axes)]

    def body(*refs):
        ins, outs = refs[:n], refs[n:2 * n]
        send, recv, loc = refs[2 * n:]
        x, y, c, me = _my_place()
        peers = _peers(x, y, c)
        local = []
        for i in range(n):
            dst = _window(outs[i], axes[i], me * sizes[i], sizes[i])
            cp = pltpu.make_async_copy(ins[i], dst, loc.at[i])
            cp.start()
            local.append(cp)
            for peer, _ in peers:
                pltpu.make_async_remote_copy(src_ref=ins[i], dst_ref=dst, send_sem=send.at[i], recv_sem=recv.at[i],
                                             device_id=peer, device_id_type=MESH).start()
        for i in range(n):
            local[i].wait()
            seven = _window(outs[i], axes[i], 0, (N_DEV - 1) * sizes[i])
            pltpu.make_async_remote_copy(src_ref=seven, dst_ref=seven, send_sem=send.at[i], recv_sem=recv.at[i],
                                         device_id=(x, y, c), device_id_type=MESH).wait()

    def full_shape(s, ax):
        shp = list(s.shape)
        shp[ax] *= N_DEV
        return jax.ShapeDtypeStruct(tuple(shp), s.dtype)

    any_spec = pl.BlockSpec(memory_space=pl.ANY)
    return pl.pallas_call(
        body,
        name=name,
        in_specs=[any_spec] * n,
        out_specs=[any_spec] * n,
        out_shape=[full_shape(s, ax) for s, ax in zip(shards, axes)],
        scratch_shapes=[pltpu.SemaphoreType.DMA((n,)), pltpu.SemaphoreType.DMA((n,)), pltpu.SemaphoreType.DMA((n,))],
        compiler_params=pltpu.CompilerParams(has_side_effects=True),
    )(*shards)


HBM_SPEC = pl.BlockSpec(memory_space=pltpu.HBM)
SEM_SPEC = pl.BlockSpec(memory_space=pltpu.SEMAPHORE)
SPLIT_EFFECT = pltpu.SideEffectType.DATAFLOW_SIDE_EFFECTING


def _push_all(kind, src, dst, axis, size, send_sem, recv_sem, place):
    x, y, c, me = place
    for peer, pidx in _peers(x, y, c):
        if kind == "gather":
            s = d = _window(dst, axis, me * size, size)
        else:
            s, d = _window(src, axis, pidx * size, size), dst.at[me]
        pltpu.make_async_remote_copy(src_ref=s, dst_ref=d, send_sem=send_sem, recv_sem=recv_sem, device_id=peer,
                                     device_id_type=MESH).start()


def _drain_all(kind, dst, axis, size, send_sem, recv_sem, place):
    x, y, c, _ = place
    seven = _window(dst, axis, 0, (N_DEV - 1) * size) if kind == "gather" else dst.at[pl.ds(0, N_DEV - 1)]
    pltpu.make_async_remote_copy(src_ref=seven, dst_ref=seven, send_sem=send_sem, recv_sem=recv_sem,
                                 device_id=(x, y, c), device_id_type=MESH).wait()


def _own_block_placed(src, axis, size, me):
    own = lax.dynamic_slice_in_dim(src, me * size, size, axis)
    return lax.dynamic_update_slice_in_dim(lax.empty((N_DEV,) + own.shape, src.dtype), own[None], me, 0)


def _split_start(name, kind, srcs, lands, axes, sizes, after=()):
    n, ns, na = len(lands), len(srcs), len(after)

    def body(*refs):
        src_refs, land_refs = refs[:ns], refs[ns:ns + n]
        send, recv = refs[ns + n + na], refs[ns + n + na + 1]
        token = refs[-1]
        place = _my_place()
        for k in range(n):
            _push_all(kind, src_refs[k] if ns else None, land_refs[k], axes[k], sizes[k], send.at[k], recv.at[k], place)
        token[...] = jnp.zeros_like(token)

    hbm = lambda a: pltpu.HBM(a.shape, a.dtype)
    res = pl.pallas_call(
        body,
        name=name,
        out_shape=(pltpu.SemaphoreType.DMA((n,)), pltpu.SemaphoreType.DMA((n,)), *[hbm(a) for a in srcs],
                   *[hbm(a) for a in lands], jax.ShapeDtypeStruct((SUBLANES, LANES), F32)),
        in_specs=[HBM_SPEC] * (ns + n) + [pl.BlockSpec(memory_space=pl.ANY)] * na,
        out_specs=(SEM_SPEC, SEM_SPEC, *[HBM_SPEC] * (ns + n), pl.BlockSpec(memory_space=pltpu.VMEM)),
        input_output_aliases={k: 2 + k for k in range(ns + n)},
        compiler_params=pltpu.CompilerParams(has_side_effects=SPLIT_EFFECT),
    )(*[pltpu.with_memory_space_constraint(a, pltpu.HBM) for a in (*srcs, *lands)], *after)
    return res[0], res[1], list(res[2:2 + ns]), list(res[2 + ns:2 + ns + n]), res[-1]


def _split_wait(name, kind, handle, axes, sizes, after):
    send, recv, srcs, lands, _ = handle
    n, ns = len(lands), len(srcs)
    after = list(after) if isinstance(after, (list, tuple)) else [after]

    def body(*refs):
        land_refs = refs[ns:ns + n]
        send_ref, recv_ref = refs[ns + n], refs[ns + n + 1]
        place = _my_place()
        for k in range(n):
            _drain_all(kind, land_refs[k], axes[k], sizes[k], send_ref.at[k], recv_ref.at[k], place)

    hbm = lambda a: pltpu.HBM(a.shape, a.dtype)
    res = pl.pallas_call(
        body,
        name=name,
        out_shape=tuple(hbm(a) for a in (*srcs, *lands)),
        in_specs=[HBM_SPEC] * (ns + n) + [SEM_SPEC, SEM_SPEC] + [pl.BlockSpec(memory_space=pl.ANY)] * len(after),
        out_specs=tuple([HBM_SPEC] * (ns + n)),
        input_output_aliases={k: k for k in range(ns + n)},
        compiler_params=pltpu.CompilerParams(has_side_effects=SPLIT_EFFECT),
    )(*srcs, *lands, send, recv, *after)
    return list(res[ns:])


def _cast_into_window(name, a, l, axis, me1, after=()):
    shp = a.shape[1:]
    cast = lambda me_ref, a_ref, *rest: rest[-1].__setitem__(Ellipsis, a_ref[...].astype(BF16))
    if len(shp) == 3:
        assert axis == 1
        G, r, c = shp
        full = (G, r * N_DEV, c)
        grid = (G,)
        in_spec = pl.BlockSpec((None, None, r, c), lambda g, me: (l, g, 0, 0))
        out_spec = pl.BlockSpec((None, r, c), lambda g, me: (g, me[0], 0))
    else:
        r, c = shp
        tr = _tile(r, 512)
        nb = r // tr
        grid = (nb,)
        in_spec = pl.BlockSpec((None, tr, c), lambda i, me: (l, i, 0))
        if axis == 0:
            full = (r * N_DEV, c)
            out_spec = pl.BlockSpec((tr, c), lambda i, me: (me[0] * nb + i, 0))
        else:
            full = (r, c * N_DEV)
            out_spec = pl.BlockSpec((tr, c), lambda i, me: (i, me[0]))
    return pl.pallas_call(
        cast,
        name=name,
        grid_spec=pltpu.PrefetchScalarGridSpec(
            num_scalar_prefetch=1, grid=grid,
            in_specs=[in_spec] + [pl.BlockSpec(memory_space=pl.ANY)] * len(after), out_specs=out_spec),
        out_shape=jax.ShapeDtypeStruct(full, BF16),
        compiler_params=_cparams("arbitrary"),
    )(me1, a, *after)


def _adamw_math(w, g, m, v):
    m = ADAM_B1 * m + (1.0 - ADAM_B1) * g
    v = ADAM_B2 * v + (1.0 - ADAM_B2) * jnp.square(g)
    m_hat = m / (1.0 - ADAM_B1 ** ADAM_STEP)
    v_hat = v / (1.0 - ADAM_B2 ** ADAM_STEP)
    delta = -ADAM_LR * (m_hat / (jnp.sqrt(v_hat) + ADAM_EPS) + ADAM_WD * w)
    return delta, m, v


def _sum_slots(buf_ref):
    g = buf_ref[0].astype(F32)
    for s in range(1, N_DEV):
        g = g + buf_ref[s].astype(F32)
    return g


def _adamw_layer(name, buf, w, m, v, l, prev, after=()):
    shape = w.shape
    L, C = shape[0], shape[-1]
    Rr = math.prod(shape[1:-1])
    buf3 = buf.reshape(N_DEV, Rr, C)
    w3, m3, v3 = (t.reshape(L, Rr, C) for t in (w, m, v))
    tr = _tile(Rr, 2 * LANES) if Rr % LANES == 0 else Rr
    n_prev = 0 if prev is None else 4

    def body(buf_ref, w_ref, m_ref, v_ref, *rest):
        g_out, d_out, m_out, v_out = rest[n_prev + len(after):]
        g = _sum_slots(buf_ref)
        d, mm, vv = _adamw_math(w_ref[...], g, m_ref[...], v_ref[...])
        g_out[...] = g
        d_out[...] = d
        m_out[...] = mm
        v_out[...] = vv

    spec = pl.BlockSpec((None, tr, C), lambda r: (l, r, 0))
    outs = pl.pallas_call(
        body,
        name=name,
        grid=(Rr // tr,),
        in_specs=[pl.BlockSpec((N_DEV, tr, C), lambda r: (0, r, 0)), spec, spec, spec]
        + [pl.BlockSpec(memory_space=pl.ANY)] * (n_prev + len(after)),
        out_specs=[spec] * 4,
        out_shape=[jax.ShapeDtypeStruct((L, Rr, C), F32)] * 4,
        input_output_aliases={4 + k: k for k in range(n_prev)},
        compiler_params=_cparams("parallel"),
    )(buf3, w3, m3, v3, *(prev or ()), *after)
    return list(outs)


def _sum8(name, buf):
    R = buf.shape[1]

    def body(buf_ref, o_ref):
        o_ref[...] = _sum_slots(buf_ref)

    return pl.pallas_call(
        body,
        name=name,
        in_specs=[pl.BlockSpec(buf.shape, lambda: (0, 0, 0))],
        out_specs=pl.BlockSpec((R, LANES), lambda: (0, 0)),
        out_shape=jax.ShapeDtypeStruct((R, LANES), F32),
        compiler_params=_cparams(),
    )(buf)


def _adamw_small(name, w, g, m, v):
    shape = w.shape
    w2, g2, m2, v2 = (t.reshape(-1, shape[-1]) for t in (w, g, m, v))
    R, C = w2.shape
    tr = _row_tile(R, 512)

    def body(w_ref, g_ref, m_ref, v_ref, d_out, m_out, v_out):
        d, mm, vv = _adamw_math(w_ref[...], g_ref[...], m_ref[...], v_ref[...])
        d_out[...] = d
        m_out[...] = mm
        v_out[...] = vv

    spec = pl.BlockSpec((tr, C), lambda r: (r, 0))
    outs = pl.pallas_call(
        body,
        name=name,
        grid=(R // tr,),
        in_specs=[spec] * 4,
        out_specs=[spec] * 3,
        out_shape=[jax.ShapeDtypeStruct((R, C), F32)] * 3,
        compiler_params=_cparams("parallel"),
    )(w2, g2, m2, v2)
    return [o.reshape(shape) for o in outs]


def _pack(arrs, pad_rows_to=SUBLANES):
    parts = []
    for a in arrs:
        flat = a.reshape(-1)
        per = LANES * pad_rows_to
        padded = -(-flat.shape[0] // per) * per
        if padded != flat.shape[0]:
            flat = jnp.pad(flat, (0, padded - flat.shape[0]))
        parts.append(flat.reshape(-1, LANES))
    return jnp.concatenate(parts, axis=0)


def _unpack(packed, shapes, pad_rows_to=SUBLANES):
    out = []
    r = 0
    for shp in shapes:
        nel = math.prod(shp)
        per = LANES * pad_rows_to
        rows = -(-nel // per) * pad_rows_to
        out.append(packed[r:r + rows].reshape(-1)[:nel].reshape(shp))
        r += rows
    return out


BIG = ("lru_w_in", "lru_w_out", "pool_w_in", "pool_w_grp", "pool_w_out", "mlp_w1", "mlp_w2", "ple_w", "ple_gate_w")
BIG_AXIS = {"lru_w_in": 2, "lru_w_out": 1, "pool_w_in": 1, "pool_w_grp": 2, "pool_w_out": 1, "mlp_w1": 2,
            "mlp_w2": 1, "ple_w": 2, "ple_gate_w": 1}
SMALL_SHARDED = ("lru_conv_w", "pool_b_grp", "pool_scale")
REPLICATED = ("lru_conv_b", "lru_wa", "lru_ba", "lru_wx", "lru_bx", "lru_lambda", "ln_mix_g", "ln_mix_b",
              "ln_mlp_g", "ln_mlp_b", "ple_gate_b")
WEIGHTS = ("lru_w_in", "lru_conv_w", "lru_conv_b", "lru_wa", "lru_ba", "lru_wx", "lru_bx", "lru_lambda", "lru_w_out",
           "pool_w_in", "pool_w_grp", "pool_b_grp", "pool_scale", "pool_w_out", "ln_mix_g", "ln_mix_b", "mlp_w1",
           "mlp_w2", "ln_mlp_g", "ln_mlp_b", "ple_w", "ple_gate_w", "ple_gate_b")
INPUTS = ("x", "p") + WEIGHTS + ("loss_target",) + tuple("m_" + n for n in WEIGHTS) + tuple("v_" + n for n in WEIGHTS)


def _gather_last_axis(packed_full, shard_shape):
    nel = math.prod(shard_shape)
    blocks = packed_full.reshape(N_DEV, -1)[:, :nel].reshape((N_DEV,) + tuple(shard_shape))
    return jnp.concatenate([blocks[d] for d in range(N_DEV)], axis=-1)


def kernel(x, p, lru_w_in, lru_conv_w, lru_conv_b, lru_wa, lru_ba, lru_wx, lru_bx, lru_lambda, lru_w_out, pool_w_in, pool_w_grp, pool_b_grp, pool_scale, pool_w_out, ln_mix_g, ln_mix_b, mlp_w1, mlp_w2, ln_mlp_g, ln_mlp_b, ple_w, ple_gate_w, ple_gate_b, loss_target, m_lru_w_in, m_lru_conv_w, m_lru_conv_b, m_lru_wa, m_lru_ba, m_lru_wx, m_lru_bx, m_lru_lambda, m_lru_w_out, m_pool_w_in, m_pool_w_grp, m_pool_b_grp, m_pool_scale, m_pool_w_out, m_ln_mix_g, m_ln_mix_b, m_mlp_w1, m_mlp_w2, m_ln_mlp_g, m_ln_mlp_b, m_ple_w, m_ple_gate_w, m_ple_gate_b, v_lru_w_in, v_lru_conv_w, v_lru_conv_b, v_lru_wa, v_lru_ba, v_lru_wx, v_lru_bx, v_lru_lambda, v_lru_w_out, v_pool_w_in, v_pool_w_grp, v_pool_b_grp, v_pool_scale, v_pool_w_out, v_ln_mix_g, v_ln_mix_b, v_mlp_w1, v_mlp_w2, v_ln_mlp_g, v_ln_mlp_b, v_ple_w, v_ple_gate_w, v_ple_gate_b):
    A = dict(zip(INPUTS, (x, p, lru_w_in, lru_conv_w, lru_conv_b, lru_wa, lru_ba, lru_wx, lru_bx, lru_lambda, lru_w_out, pool_w_in, pool_w_grp, pool_b_grp, pool_scale, pool_w_out, ln_mix_g, ln_mix_b, mlp_w1, mlp_w2, ln_mlp_g, ln_mlp_b, ple_w, ple_gate_w, ple_gate_b, loss_target, m_lru_w_in, m_lru_conv_w, m_lru_conv_b, m_lru_wa, m_lru_ba, m_lru_wx, m_lru_bx, m_lru_lambda, m_lru_w_out, m_pool_w_in, m_pool_w_grp, m_pool_b_grp, m_pool_scale, m_pool_w_out, m_ln_mix_g, m_ln_mix_b, m_mlp_w1, m_mlp_w2, m_ln_mlp_g, m_ln_mlp_b, m_ple_w, m_ple_gate_w, m_ple_gate_b, v_lru_w_in, v_lru_conv_w, v_lru_conv_b, v_lru_wa, v_lru_ba, v_lru_wx, v_lru_bx, v_lru_lambda, v_lru_w_out, v_pool_w_in, v_pool_w_grp, v_pool_b_grp, v_pool_scale, v_pool_w_out, v_ln_mix_g, v_ln_mix_b, v_mlp_w1, v_mlp_w2, v_ln_mlp_g, v_ln_mlp_b, v_ple_w, v_ple_gate_w, v_ple_gate_b)))
    depth = ln_mix_g.shape[0]
    alpha = (2 * depth) ** 0.25
    S, D = x.shape[1], x.shape[2]
    xs = x.reshape(S, D)
    tgt = loss_target.reshape(S, D)
    p3 = p.reshape(depth, S, p.shape[-1])
    me = 4 * lax.axis_index("x") + 2 * lax.axis_index("y") + lax.axis_index("c")

    def layer_weights(i):
        s = i // 2
        mixer = ("lru_w_in", "lru_w_out") if i % 2 == 0 else ("pool_w_in", "pool_w_grp", "pool_w_out")
        return [(n, s) for n in mixer] + [(n, i) for n in ("mlp_w1", "mlp_w2", "ple_w", "ple_gate_w")]

    def axis_of(key):
        return 0 if key[0] == "small" else BIG_AXIS[key[0]] - 1

    def start_gather(tag, keys, after):
        axes = [axis_of(k) for k in keys]
        lands = [land[k] for k in keys]
        sizes = [a.shape[ax] // N_DEV for a, ax in zip(lands, axes)]
        return keys, _split_start(f"gather_{tag}_start", "gather", [], lands, axes, sizes, after=after), axes, sizes

    def finish_gather(tag, pending, after):
        keys, handle, axes, sizes = pending
        for (n, l), full in zip(keys, _split_wait(f"gather_{tag}_wait", "gather", handle, axes, sizes, after)):
            W[n][l] = full

    def start_exchange(tag, keys, arrs, after):
        axes = [axis_of(k) for k in keys]
        sizes = [a.shape[ax] // N_DEV for a, ax in zip(arrs, axes)]
        lands = [_own_block_placed(a, ax, sz, me) for a, ax, sz in zip(arrs, axes, sizes)]
        return keys, _split_start(f"exchange_{tag}_start", "scatter", arrs, lands, axes, sizes, after=after), axes, sizes

    def finish_exchange(tag, pending, after):
        keys, handle, axes, sizes = pending
        partial.update(zip(keys, _split_wait(f"exchange_{tag}_wait", "scatter", handle, axes, sizes, after)))

    me1 = jnp.reshape(me, (1,)).astype(jnp.int32)
    land = {}
    W = {n: [None] * A[n].shape[0] for n in BIG}
    small_shard_shapes = [A[n].shape for n in SMALL_SHARDED]
    gathered = _all_gather("gather_small_params", [_pack([A[n] for n in SMALL_SHARDED])], [0])
    def gather_groups(i):
        keys = layer_weights(i)
        mixer, (w1, w2, pw, pg) = keys[:-4], keys[-4:]
        if i % 2 == 0:
            return [("in", mixer[:1]), ("out", mixer[1:]), ("up", [w1]), ("rest", [w2, pw, pg])]
        return [("in", mixer + [w1]), ("rest", [w2, pw, pg])]

    gather_pending = {}

    def send_layer(i, behind):
        for tag, keys in gather_groups(i):
            for k in keys:
                land[k] = _cast_into_window(f"cast_{k[0]}_{k[1]}", A[k[0]], k[1], axis_of(k), me1, after=behind)
            gather_pending[(i, tag)] = start_gather(f"l{i}_{tag}", keys, behind)
            behind = (gather_pending[(i, tag)][1][4],)
        return behind

    layer0_started = send_layer(0, (gathered[0],))
    small_full = gathered[0].reshape(N_DEV, -1, LANES)
    r = 0
    for n, shp in zip(SMALL_SHARDED, small_shard_shapes):
        rows = -(-math.prod(shp) // (LANES * SUBLANES)) * SUBLANES
        W[n] = _gather_last_axis(small_full[:, r:r + rows], shp)
        r += rows
    wa_b, wx_b = _to_bf16("cast_gates", [lru_wa, lru_wx], after=layer0_started)
    n_lru = lru_w_in.shape[0]
    lru_par = [jnp.concatenate([W["lru_conv_w"][s], lru_conv_b[s][None], lru_ba[s][None], lru_bx[s][None],
                                lru_lambda[s][None]], axis=0) for s in range(n_lru)]
    pool_par = [jnp.stack([W["pool_b_grp"][s], W["pool_scale"][s]], axis=0) for s in range(pool_w_in.shape[0])]

    saved = []
    h_in = xs
    (h_in_b,) = _to_bf16("cast_inputs", [xs], after=layer0_started)
    layer1_started = send_layer(1, (layer0_started[0], h_in_b))
    finish_gather("l0_in", gather_pending[(0, "in")], [h_in_b, wa_b, layer1_started[0]])
    for i in range(depth):
        s = i // 2
        sv = {"x0b": h_in_b}
        if i > 0:
            finish_gather(f"l{i}_in", gather_pending[(i, "in")], h_in)
        ln_out = dict(out_dtypes=[F32, F32, BF16], tm=MM_TM_ROWS, tn=D,
                      epi=lambda acc, xp, g, b: _ln_apply(alpha * xp + acc, g, b))
        if i % 2 == 0:
            sv["proj"] = _mm(f"l{i}_lru_in", h_in_b, W["lru_w_in"], "nn", [F32], b_lead=s)
            if i == 0:
                behind = (layer1_started[0], sv["proj"])
                for later in range(2, depth):
                    behind = send_layer(later, behind)
                all_started = behind[0]
            sv["gh"], sv["h"], *sv["gates"] = _lru_fwd(f"l{i}_lru_core", sv["proj"], lru_par[s], wa_b[s], wx_b[s])
            finish_gather(f"l{i}_out", gather_pending[(i, "out")], [sv["gh"], all_started])
            mix_in, mix_w = sv["gh"], W["lru_w_out"]
        else:
            sv["u"] = _mm(f"l{i}_pool_in", h_in_b, W["pool_w_in"], "nn", [F32], b_lead=s)
            sv["zs"] = _pool_fwd(f"l{i}_pool_core", sv["u"], W["pool_w_grp"][s], pool_par[s])
            mix_in, mix_w = sv["zs"], W["pool_w_out"]
        sv["z1"], sv["x1"], sv["x1b"] = _mm(f"l{i}_mix_out_ln", mix_in, mix_w, "nn", b_lead=s,
                                            extras=[h_in, ln_mix_g[i][None], ln_mix_b[i][None]], **ln_out)
        if i % 2 == 0:
            finish_gather(f"l{i}_up", gather_pending[(i, "up")], sv["x1b"])
        sv["hpre"], sv["hact"] = _mm(f"l{i}_mlp_up", sv["x1b"], W["mlp_w1"], "nn", [BF16, BF16], b_lead=i,
                                     epi=lambda acc: (acc, jnp.square(jnp.maximum(acc, 0.0))))
        finish_gather(f"l{i}_rest", gather_pending[(i, "rest")], sv["hact"])
        sv["z2"], sv["x2"], sv["x2b"] = _mm(f"l{i}_mlp_down_ln", sv["hact"], W["mlp_w2"], "nn", b_lead=i,
                                            extras=[sv["x1"], ln_mlp_g[i][None], ln_mlp_b[i][None]], **ln_out)
        sv["pp"] = _mm(f"l{i}_ple_up", p3, W["ple_w"], "nn", [F32], a_lead=i, b_lead=i)

        def ple_epi(acc, bg, x2t, ppt):
            gpre = acc + bg
            x3 = x2t + ppt * jax.nn.sigmoid(gpre)
            return x3, x3, gpre

        h_in, h_in_b, sv["gpre"] = _mm(f"l{i}_ple_gate", sv["x2b"], W["ple_gate_w"], "nn", [F32, BF16, F32], b_lead=i,
                                       epi=ple_epi, extras=[ple_gate_b[i][None], sv["x2"], sv["pp"]], tn=MM_TN)
        saved.append(sv)

    dx, dpp, dgpre, sq, dbg = _loss_and_grad("loss", h_in, tgt, saved[-1]["gpre"], saved[-1]["pp"])
    loss = lax.psum(0.5 * sq[0, 0] / D, ("x", "y", "c"))

    dW = {n: [None] * A[n].shape[0] for n in BIG}
    dsmall = {n: [None] * A[n].shape[0] for n in REPLICATED + SMALL_SHARDED}
    small_names = REPLICATED + SMALL_SHARDED
    partial = {}
    exchange_pending = {}
    exchange_token = ()
    for i in reversed(range(depth)):
        s = i // 2
        sv = saved[i]
        dsmall["ple_gate_b"][i] = dbg[0]
        dW["ple_w"][i] = _mm(f"l{i}_d_ple_w", p3, dpp, "tn", [BF16], a_lead=i, after=exchange_token)
        dW["ple_gate_w"][i] = _mm(f"l{i}_d_ple_gate_w", sv["x2b"], dgpre, "tn", [BF16])
        ln_back = dict(out_dtypes=[F32, BF16], tm=MM_TM_ROWS, tn=D, n_sums=2)
        dz2, dz2b, dg, db = _mm(f"l{i}_d_x2_ln", dgpre, W["ple_gate_w"], "nt", b_lead=i,
                                extras=[dx, sv["z2"], ln_mlp_g[i][None]],
                                epi=lambda acc, d, z, g: _ln_grad(acc + d, z, g), after=exchange_token, **ln_back)
        dsmall["ln_mlp_g"][i], dsmall["ln_mlp_b"][i] = dg[0], db[0]
        dhpre = _mm(f"l{i}_d_hpre", dz2b, W["mlp_w2"], "nt", [BF16], b_lead=i, extras=[sv["hpre"]],
                    epi=lambda acc, hp: (acc * (2.0 * jnp.maximum(hp.astype(F32), 0.0)),))
        dW["mlp_w2"][i] = _mm(f"l{i}_d_mlp_w2", sv["hact"], dz2b, "tn", [BF16])
        dW["mlp_w1"][i] = _mm(f"l{i}_d_mlp_w1", sv["x1b"], dhpre, "tn", [BF16])
        mlp_after = ()
        if i == 0:
            early = [(n, 0) for n in ("ple_w", "ple_gate_w", "mlp_w2", "mlp_w1")]
            exchange_early0 = start_exchange("early0", early, [dW[n][l] for n, l in early], ())
            mlp_after = (exchange_early0[1][4],)
        dz1, dz1b, dg, db = _mm(f"l{i}_d_x1_ln", dhpre, W["mlp_w1"], "nt", b_lead=i,
                                extras=[dz2, sv["z1"], ln_mix_g[i][None]],
                                epi=lambda acc, d, z, g: _ln_grad(acc + alpha * d, z, g), after=mlp_after, **ln_back)
        dsmall["ln_mix_g"][i], dsmall["ln_mix_b"][i] = dg[0], db[0]
        if i % 2 == 0:
            dW["lru_w_out"][s] = _mm(f"l{i}_d_lru_w_out", sv["gh"], dz1b, "tn", [BF16])
            dgh = _mm(f"l{i}_d_gh", dz1b, W["lru_w_out"], "nt", [F32], b_lead=s)
            dup, dy, dwa, dwx, dpar = _lru_bwd(f"l{i}_lru_core_bwd", sv["proj"], sv["h"], dgh, sv["gates"],
                                               lru_par[s], wa_b[s], wx_b[s])
            dsmall["lru_wa"][s], dsmall["lru_wx"][s] = dwa, dwx
            dsmall["lru_conv_w"][s] = dpar[0:4]
            for k, n in enumerate(("lru_conv_b", "lru_ba", "lru_bx", "lru_lambda")):
                dsmall[n][s] = dpar[4 + k]
            dmix_in = jnp.concatenate([dup, dy], axis=1)
            win = "lru_w_in"
        else:
            dW["pool_w_out"][s] = _mm(f"l{i}_d_pool_w_out", sv["zs"], dz1b, "tn", [BF16])
            dzs = _mm(f"l{i}_d_zs", dz1b, W["pool_w_out"], "nt", [F32], b_lead=s)
            dmix_in, dW["pool_w_grp"][s], dpar = _pool_bwd(f"l{i}_pool_core_bwd", sv["u"], dzs, W["pool_w_grp"][s],
                                                          pool_par[s])
            dsmall["pool_b_grp"][s], dsmall["pool_scale"][s] = dpar[0], dpar[1]
            win = "pool_w_in"
        dW[win][s] = _mm(f"l{i}_d_{win}", sv["x0b"], dmix_in, "tn", [BF16])
        x0_after = ()
        if i > 0:
            keys = layer_weights(i)
            exchange_pending[i] = start_exchange(f"l{i}", keys, [dW[n][l] for n, l in keys], ())
            exchange_token = (exchange_pending[i][1][4],)
        else:
            small_grads = [jnp.stack(dsmall[n]) for n in small_names]
            small_shapes = [g.shape for g in small_grads]
            packed_g = _pack(small_grads)
            assert packed_g.shape[0] % (N_DEV * SUBLANES) == 0, packed_g.shape
            late = [("lru_w_out", 0), ("lru_w_in", 0), ("small", 0)]
            exchange_late0 = start_exchange("late0", late, [dW["lru_w_out"][0], dW["lru_w_in"][0], packed_g], ())
            x0_after = (exchange_late0[1][4],)
        if i > 0:
            def x0_epi(acc, d, gpre, pp):
                dxv = acc + alpha * d
                return (dxv,) + _ple_grad(dxv, gpre, pp)

            dx, dpp, dgpre, dbg = _mm(f"l{i}_d_x0_ple", dmix_in, W[win], "nt", [F32, BF16, BF16], b_lead=s,
                                      extras=[dz1, saved[i - 1]["gpre"], saved[i - 1]["pp"]], epi=x0_epi,
                                      tm=MM_TM_ROWS, tn=D, n_sums=1)
        else:
            dx = _mm(f"l{i}_d_x0", dmix_in, W[win], "nt", [F32], b_lead=s, extras=[dz1],
                     epi=lambda acc, d: (acc + alpha * d,), after=x0_after)
    grad_x = dx.reshape(x.shape)

    for i in range(1, depth):
        finish_exchange(f"l{i}", exchange_pending[i], x0_after[0])
    stacked = {n: None for n in BIG}
    layer0 = layer_weights(0)

    def adamw(n, l, after=()):
        stacked[n] = _adamw_layer(f"adamw_{n}_{l}", partial[(n, l)], A[n], A["m_" + n], A["v_" + n], l, stacked[n],
                                  after=after)

    for n in BIG:
        for l in reversed(range(A[n].shape[0])):
            if (n, l) not in layer0:
                adamw(n, l)
    behind = [dx] + [stacked[n][0] for n in BIG if stacked[n] is not None]
    finish_exchange("early0", exchange_early0, behind)
    finish_exchange("late0", exchange_late0, behind)
    red = _sum8("sum_small", partial[("small", 0)])
    rows = red.shape[0]
    red_land = lax.dynamic_update_slice_in_dim(lax.empty((N_DEV * rows, LANES), F32), red, me * rows, 0)
    small_handle = _split_start("gather_small_start", "gather", [], [red_land], [0], [rows])
    for n, l in layer0:
        adamw(n, l, after=(small_handle[4],))
    outs = {n: [o.reshape(A[n].shape) for o in stacked[n]] for n in BIG}
    red_full = _split_wait("gather_small_wait", "gather", small_handle, [0], [rows], [stacked[n][0] for n, _ in layer0])[0]
    small_g = dict(zip(small_names, _unpack(red_full, small_shapes)))
    for n in SMALL_SHARDED:
        width = A[n].shape[-1]
        small_g[n] = lax.dynamic_slice_in_dim(small_g[n], me * width, width, axis=small_g[n].ndim - 1)
    for n in small_names:
        outs[n] = [small_g[n]] + _adamw_small(f"adamw_{n}", A[n], small_g[n], A["m_" + n], A["v_" + n])

    return (loss, grad_x, *[outs[n][0] for n in WEIGHTS], *[outs[n][1] for n in WEIGHTS],
            *[outs[n][2] for n in WEIGHTS], *[outs[n][3] for n in WEIGHTS])
```

```python
import functools
import math

import jax
import jax.numpy as jnp
from jax import lax
from jax.experimental import pallas as pl
from jax.experimental.pallas import tpu as pltpu

F32 = jnp.float32
BF16 = jnp.bfloat16
MESH = pl.DeviceIdType.MESH
N_DEV = 8
LANES = 128
SUBLANES = 8

LN_EPS = 1e-5
LRU_C = 8.0
CONV_WIDTH = 4
POOL_HALO = 16
ADAM_LR = 0.001
ADAM_B1 = 0.9
ADAM_B2 = 0.999
ADAM_EPS = 1e-08
ADAM_WD = 0.01
ADAM_STEP = 10

VMEM_LIMIT = 48 * 1024 * 1024
VMEM_LIMIT_SEQ = 56 * 1024 * 1024
SEQ_CHUNK = 256
MM_TK = 4096
MM_TK_TOKENS = 4096
MM_TM_ROWS = 512
MM_TN = 512
MM_TN_WIDE = 1024
MM_TN_WIDE_MAX_K = 2048
GELU_C0 = math.sqrt(2.0 / math.pi)
GELU_C1 = 0.044715


def _in_hbm(*arrays):
    return [pltpu.with_memory_space_constraint(a, pltpu.HBM) for a in arrays]


def _cparams(*sem, limit=None):
    return pltpu.CompilerParams(dimension_semantics=tuple(sem) if sem else None,
                                vmem_limit_bytes=VMEM_LIMIT if limit is None else limit)


def _tile(n, pref):
    if n <= pref:
        return n
    t = pref - pref % LANES
    while t >= LANES:
        if n % t == 0:
            return t
        t -= LANES
    return n


def _row_tile(n, pref):
    if n <= pref:
        return n
    t = pref - pref % SUBLANES
    while t >= SUBLANES:
        if n % t == 0:
            return t
        t -= SUBLANES
    return n


def _mm(name, a, b, mode, out_dtypes, epi=None, extras=(), a_lead=None, b_lead=None, tm=1024, tn=None, tk=None,
        after=(), n_sums=0):
    if isinstance(b, (list, tuple)):
        b, b_lead = b[b_lead], None
    a2 = a.shape[-2:]
    b2 = b.shape[-2:]
    if mode == "nn":
        (M, K), N = a2, b2[1]
        assert b2[0] == K
    elif mode == "nt":
        (M, K), N = a2, b2[0]
        assert b2[1] == K
    else:
        (K, M), N = a2, b2[1]
        assert b2[0] == K
    if tk is None:
        tk = MM_TK_TOKENS if mode == "tn" else MM_TK
    if tn is None:
        tn = MM_TN_WIDE if (mode != "tn" and K <= MM_TN_WIDE_MAX_K) else MM_TN
    tm, tn, tk = _tile(M, tm), _tile(N, tn), _tile(K, tk)
    nk = K // tk
    n_extra = len(extras)
    n_out = len(out_dtypes)
    assert n_sums == 0 or tn == N
    resident = {"pipeline_mode": pl.Buffered(1)} if (tn == N and nk == 1) else {}

    def lead(shape, idx, which, **kw):
        if which is None:
            return pl.BlockSpec(shape, idx, **kw)
        return pl.BlockSpec((None,) + shape, lambda i, j, k: (which,) + idx(i, j, k), **kw)

    if mode == "nn":
        a_spec = lead((tm, tk), lambda i, j, k: (i, k), a_lead)
        b_spec = lead((tk, tn), lambda i, j, k: (k, j), b_lead, **resident)
        dims = (((1,), (0,)), ((), ()))
    elif mode == "nt":
        a_spec = lead((tm, tk), lambda i, j, k: (i, k), a_lead)
        b_spec = lead((tn, tk), lambda i, j, k: (j, k), b_lead, **resident)
        dims = (((1,), (1,)), ((), ()))
    else:
        a_spec = lead((tk, tm), lambda i, j, k: (k, i), a_lead)
        b_spec = lead((tk, tn), lambda i, j, k: (k, j), b_lead, **resident)
        dims = (((0,), (0,)), ((), ()))
    e_specs = []
    for e in extras:
        if e.shape[0] == 1:
            e_specs.append(pl.BlockSpec((1, tn), lambda i, j, k: (0, j)))
        else:
            e_specs.append(pl.BlockSpec((tm, tn), lambda i, j, k: (i, j)))

    n_after = len(after)

    def body(a_ref, b_ref, *rest):
        e_refs = rest[:n_extra]
        rest = rest[:n_extra] + rest[n_extra + n_after:]
        o_refs = rest[n_extra:n_extra + n_out]
        s_refs = rest[n_extra + n_out:n_extra + n_out + n_sums]
        part = lax.dot_general(a_ref[...].astype(BF16), b_ref[...].astype(BF16), dims, preferred_element_type=F32)

        def finish(r):
            res = (r,) if epi is None else epi(r, *[e[...] for e in e_refs])
            for o, v in zip(o_refs, res[:n_out]):
                o[...] = v.astype(o.dtype)
            first = pl.program_id(0) == 0
            for sr, v in zip(s_refs, res[n_out:]):
                @pl.when(first)
                def _(sr=sr, v=v):
                    sr[...] = v

                @pl.when(jnp.logical_not(first))
                def _(sr=sr, v=v):
                    sr[...] += v

        if nk == 1:
            finish(part)
            return
        acc = rest[n_extra + n_out + n_sums]
        k = pl.program_id(2)

        @pl.when(k == 0)
        def _():
            acc[...] = part

        @pl.when(jnp.logical_and(k > 0, k < nk - 1))
        def _():
            acc[...] += part

        @pl.when(k == nk - 1)
        def _():
            finish(acc[...] + part)

    outs = pl.pallas_call(
        body,
        name=name,
        grid=(M // tm, N // tn, nk),
        in_specs=[a_spec, b_spec] + e_specs + [pl.BlockSpec(memory_space=pl.ANY)] * n_after,
        out_specs=[pl.BlockSpec((tm, tn), lambda i, j, k: (i, j)) for _ in out_dtypes]
        + [pl.BlockSpec((1, tn), lambda i, j, k: (0, 0))] * n_sums,
        out_shape=[jax.ShapeDtypeStruct((M, N), d) for d in out_dtypes] + [jax.ShapeDtypeStruct((1, N), F32)] * n_sums,
        scratch_shapes=[pltpu.VMEM((tm, tn), F32)] if nk > 1 else [],
        compiler_params=_cparams(*(("arbitrary",) * 3 if n_sums else ("parallel", "parallel", "arbitrary"))),
    )(*_in_hbm(a, b, *extras), *after)
    return outs[0] if n_out + n_sums == 1 else tuple(outs)


def _rowwise(name, fn, tiled, params, outs, accs=(), tm=256, after=()):
    S = tiled[0].shape[0]
    tm = _tile(S, tm)
    nt, npar, no = len(tiled), len(params), len(outs)
    n_after = len(after)

    def body(*refs):
        t_refs = refs[:nt]
        p_refs = refs[nt:nt + npar]
        refs = refs[nt + npar + n_after:]
        o_refs = refs[:no]
        a_refs = refs[no:]
        res = fn(*[r[...] for r in t_refs], *[r[...] for r in p_refs])
        for o, v in zip(o_refs, res[:no]):
            o[...] = v.astype(o.dtype)
        first = pl.program_id(0) == 0
        for ar, v in zip(a_refs, res[no:]):
            @pl.when(first)
            def _(ar=ar, v=v):
                ar[...] = v

            @pl.when(jnp.logical_not(first))
            def _(ar=ar, v=v):
                ar[...] += v

    full = lambda p: pl.BlockSpec(p.shape, lambda i, nd=p.ndim: (0,) * nd)
    res = pl.pallas_call(
        body,
        name=name,
        grid=(S // tm,),
        in_specs=[pl.BlockSpec((tm, t.shape[1]), lambda i: (i, 0)) for t in tiled] + [full(p) for p in params]
        + [pl.BlockSpec(memory_space=pl.ANY)] * n_after,
        out_specs=[pl.BlockSpec((tm, c), lambda i: (i, 0)) for c, _ in outs]
        + [pl.BlockSpec(s, lambda i, nd=len(s): (0,) * nd) for s in accs],
        out_shape=[jax.ShapeDtypeStruct((S, c), d) for c, d in outs] + [jax.ShapeDtypeStruct(s, F32) for s in accs],
        compiler_params=_cparams("arbitrary"),
    )(*_in_hbm(*tiled, *params), *after)
    return res


def _ln_stats(z):
    mu = jnp.mean(z, axis=-1, keepdims=True)
    zc = z - mu
    var = jnp.mean(zc * zc, axis=-1, keepdims=True)
    return zc, lax.rsqrt(var + LN_EPS)


def _ln_apply(z, g, b):
    zc, rstd = _ln_stats(z)
    y = zc * rstd * g + b
    return z, y, y


def _ln_grad(dy, z, g):
    zc, rstd = _ln_stats(z)
    xhat = zc * rstd
    dxh = dy * g
    m1 = jnp.mean(dxh, axis=-1, keepdims=True)
    m2 = jnp.mean(dxh * xhat, axis=-1, keepdims=True)
    dz = rstd * (dxh - m1 - xhat * m2)
    return dz, dz, jnp.sum(dy * xhat, axis=0, keepdims=True), jnp.sum(dy, axis=0, keepdims=True)


def _ple_grad(dx3, gpre, pp):
    gate = jax.nn.sigmoid(gpre)
    dgpre = dx3 * pp * gate * (1.0 - gate)
    return dx3 * gate, dgpre, jnp.sum(dgpre, axis=0, keepdims=True)


def _loss_and_grad(name, y, target, gpre, pp):
    d = y.shape[1]

    def fn(y, t, gpre, pp):
        err = y - t
        sq = jnp.sum(jnp.sum(err * err, axis=0, keepdims=True), axis=1, keepdims=True)
        dy = err * (1.0 / d)
        dpp, dgpre, dbg = _ple_grad(dy, gpre, pp)
        return dy, dpp, dgpre, jnp.broadcast_to(sq, (1, LANES)), dbg

    return _rowwise(name, fn, [y, target, gpre, pp], [], [(d, F32), (d, BF16), (d, BF16)], accs=[(1, LANES), (1, d)])


def _rows(shape):
    return lax.broadcasted_iota(jnp.int32, shape, 0)


def _gelu(y):
    t = jnp.tanh(GELU_C0 * (y + GELU_C1 * y * y * y))
    return 0.5 * y * (1.0 + t), t


def _gelu_grad(y, t):
    return 0.5 * (1.0 + t) + 0.5 * y * (1.0 - t * t) * GELU_C0 * (1.0 + 3.0 * GELU_C1 * y * y)


def _softplus(x):
    return jnp.maximum(x, 0.0) + jnp.log(1.0 + jnp.exp(-jnp.abs(x)))


def _conv_fwd(xs, cw, cb):
    n = xs.shape[0]
    u = cw[3:4] * xs
    for k in (1, 2, 3):
        u = u + cw[3 - k:4 - k] * pltpu.roll(xs, k, 0)
    del n
    return u[SUBLANES:] + cb


def _lru_gates(u, wa, wx, ba, bx, sp, grow):
    ub = u.astype(BF16)
    r = jax.nn.sigmoid(jnp.dot(ub, wa, preferred_element_type=F32) + ba)
    ig = jax.nn.sigmoid(jnp.dot(ub, wx, preferred_element_type=F32) + bx)
    log_a = (-LRU_C) * r * sp
    a = jnp.exp(log_a)
    mult = jnp.sqrt(jnp.tanh(-log_a) * (1.0 + a * a))
    mult = jnp.where(grow == 0, 1.0, mult)
    return ub, r, ig, a, mult


def _scan8_fwd(a, b):
    row = _rows(a.shape)
    for k in (1, 2, 4):
        m = row >= k
        b = jnp.where(m, a * pltpu.roll(b, k, 0) + b, b)
        a = jnp.where(m, a * pltpu.roll(a, k, 0), a)
    return a, b


def _scan8_bwd(c, d):
    row = _rows(c.shape)
    for k in (1, 2, 4):
        m = row < SUBLANES - k
        d = jnp.where(m, c * pltpu.roll(d, SUBLANES - k, 0) + d, d)
        c = jnp.where(m, c * pltpu.roll(c, SUBLANES - k, 0), c)
    return c, d


def _pad_copy(dst, src, front, back):
    s, c = src.shape
    if front:
        dst[pl.ds(0, front), :] = jnp.zeros((front, c), dst.dtype)
    if back:
        dst[pl.ds(front + s, back), :] = jnp.zeros((back, c), dst.dtype)
    dst[pl.ds(front, s), :] = src[...].astype(dst.dtype)


def _lru_fwd(name, proj, par, wa, wx):
    S = proj.shape[0]
    R = proj.shape[1] // 2
    H = R // LANES
    ch = _tile(S, SEQ_CHUNK)
    nch = S // ch
    H8 = SUBLANES

    def body(up_ref, y_ref, par_ref, wa_ref, wx_ref, gh_ref, h_ref, r_ref, ig_ref, a_ref, mult_ref, gy_ref, dgy_ref,
             u_ref, up_pad):
        _pad_copy(up_pad, up_ref, H8, 0)
        par = par_ref[...]
        cw, cb, ba, bx = par[0:4], par[4:5], par[5:6], par[6:7]
        sp = _softplus(-par[7:8])
        wa_m, wx_m = wa_ref[...], wx_ref[...]

        def chunk(ci, carry):
            r0 = pl.multiple_of(ci * ch, ch)
            xs = up_pad[pl.ds(r0, ch + H8), :]
            u = _conv_fwd(xs, cw, cb)
            grow = _rows(u.shape) + r0
            _, r, ig, a, mult = _lru_gates(u, wa_m, wx_m, ba, bx, sp, grow)
            for ref, val in ((r_ref, r), (ig_ref, ig), (a_ref, a), (mult_ref, mult), (u_ref, u)):
                ref[pl.ds(r0, ch), :] = val
            bt = mult * (ig * u)
            hs = []
            for j in range(ch // H8):
                aa, bb = _scan8_fwd(a[j * H8:(j + 1) * H8], bt[j * H8:(j + 1) * H8])
                hj = bb + aa * carry
                carry = hj[H8 - 1:H8]
                hs.append(hj)
            h = jnp.concatenate(hs, axis=0)
            h_ref[pl.ds(r0, ch), :] = h
            y = y_ref[pl.ds(r0, ch), :]
            gy, t = _gelu(y)
            gy_ref[pl.ds(r0, ch), :] = gy
            dgy_ref[pl.ds(r0, ch), :] = _gelu_grad(y, t)
            gh_ref[pl.ds(r0, ch), :] = (h * gy).astype(gh_ref.dtype)
            return carry

        lax.fori_loop(0, nch, chunk, jnp.zeros((1, LANES), F32))

    col = lambda off: pl.BlockSpec((S, LANES), lambda h: (0, h + off))
    return pl.pallas_call(
        body,
        name=name,
        grid=(H,),
        in_specs=[col(0), col(H), pl.BlockSpec((8, LANES), lambda h: (0, h)),
                  pl.BlockSpec((None, LANES, LANES), lambda h: (h, 0, 0)),
                  pl.BlockSpec((None, LANES, LANES), lambda h: (h, 0, 0))],
        out_specs=[col(0)] * 9,
        out_shape=[jax.ShapeDtypeStruct((S, R), BF16)] + [jax.ShapeDtypeStruct((S, R), F32)] * 8,
        scratch_shapes=[pltpu.VMEM((S + H8, LANES), F32)],
        compiler_params=_cparams("parallel"),
    )(proj, proj, par, wa, wx)


def _lru_bwd(name, proj, h, dgh, gates, par, wa, wx):
    S = proj.shape[0]
    R = proj.shape[1] // 2
    H = R // LANES
    ch = _tile(S, SEQ_CHUNK)
    nch = S // ch
    H8 = SUBLANES
    nb = ch // H8

    def body(up_ref, h_ref, dgh_ref, r_ref, ig_ref, a_ref, mult_ref, gy_ref, dgy_ref, u_ref, par_ref, wa_ref, wx_ref,
             dup_ref, dy_ref, dwa_ref, dwx_ref, dpar_ref, up_pad, du_pad, vec_acc):
        _pad_copy(up_pad, up_ref, H8, 0)
        du_pad[pl.ds(S, H8), :] = jnp.zeros((H8, LANES), F32)
        par = par_ref[...]
        cw, lam = par[0:4], par[7:8]
        sp = _softplus(-lam)
        wa_m, wx_m = wa_ref[...], wx_ref[...]
        dwa_ref[...] = jnp.zeros_like(dwa_ref)
        dwx_ref[...] = jnp.zeros_like(dwx_ref)
        vec_acc[...] = jnp.zeros_like(vec_acc)
        nt_dims = (((1,), (1,)), ((), ()))
        tn_dims = (((0,), (0,)), ((), ()))

        def chunk(it, carry):
            lam_next, a_next = carry
            ci = nch - 1 - it
            r0 = pl.multiple_of(ci * ch, ch)
            here = pl.ds(r0, ch)
            u = u_ref[here, :]
            row = _rows(u.shape)
            grow = row + r0
            ub = u.astype(BF16)
            r, ig, a, mult = r_ref[here, :], ig_ref[here, :], a_ref[here, :], mult_ref[here, :]
            hcur = h_ref[here, :]
            before = h_ref[pl.ds(pl.multiple_of(jnp.maximum(r0 - H8, 0), H8), H8), :][H8 - 1:H8]
            hprev = jnp.where(row == 0, jnp.where(ci > 0, before, 0.0), pltpu.roll(hcur, 1, 0))
            dgh = dgh_ref[here, :]
            dy_ref[here, :] = (dgh * hcur * dgy_ref[here, :]).astype(dy_ref.dtype)
            dh = dgh * gy_ref[here, :]
            c = jnp.where(row == ch - 1, a_next, pltpu.roll(a, ch - 1, 0))
            ls = [None] * nb
            for j in range(nb - 1, -1, -1):
                cc, dd = _scan8_bwd(c[j * H8:(j + 1) * H8], dh[j * H8:(j + 1) * H8])
                lj = dd + cc * lam_next
                lam_next = lj[0:1]
                ls[j] = lj
            lmb = jnp.concatenate(ls, axis=0)
            da = lmb * hprev
            gu = ig * u
            dmult = lmb * gu
            dlog_a = da * a + jnp.where(grow == 0, 0.0, dmult * (-(a * a) / mult))
            dr = dlog_a * ((-LRU_C) * sp)
            drp = dr * r * (1.0 - r)
            dip = (lmb * mult * u) * ig * (1.0 - ig)
            drb, dib = drp.astype(BF16), dip.astype(BF16)
            du = (lmb * mult * ig
                  + lax.dot_general(drb, wa_m, nt_dims, preferred_element_type=F32)
                  + lax.dot_general(dib, wx_m, nt_dims, preferred_element_type=F32))
            du_pad[pl.ds(r0, ch), :] = du
            dwa_ref[...] += lax.dot_general(ub, drb, tn_dims, preferred_element_type=F32)
            dwx_ref[...] += lax.dot_general(ub, dib, tn_dims, preferred_element_type=F32)
            ssum = lambda v: jnp.sum(v, axis=0, keepdims=True)
            vec_acc[0:1, :] += ssum(drp)
            vec_acc[1:2, :] += ssum(dip)
            vec_acc[2:3, :] += ssum(dlog_a * ((-LRU_C) * r))
            return lam_next, a[0:1]

        zero = jnp.zeros((1, LANES), F32)
        lax.fori_loop(0, nch, chunk, (zero, zero))

        def conv_chunk(ci, acc):
            r0 = pl.multiple_of(ci * ch, ch)
            ds = du_pad[pl.ds(r0, ch + H8), :]
            xs = up_pad[pl.ds(r0, ch + H8), :]
            n = ch + H8
            du = ds[:ch]
            dup = cw[3:4] * du
            new = [acc[3] + jnp.sum(du * xs[H8:], axis=0, keepdims=True)]
            for k in (1, 2, 3):
                dup = dup + cw[3 - k:4 - k] * pltpu.roll(ds, n - k, 0)[:ch]
                new.append(acc[3 - k] + jnp.sum(du * pltpu.roll(xs, k, 0)[H8:], axis=0, keepdims=True))
            dup_ref[pl.ds(r0, ch), :] = dup.astype(dup_ref.dtype)
            return (new[3], new[2], new[1], new[0], acc[4] + jnp.sum(du, axis=0, keepdims=True))

        acc = lax.fori_loop(0, nch, conv_chunk, (zero,) * 5)
        dlam = vec_acc[2:3, :] * (-jax.nn.sigmoid(-lam))
        dpar_ref[...] = jnp.concatenate(list(acc) + [vec_acc[0:1, :], vec_acc[1:2, :], dlam], axis=0)

    col = lambda off: pl.BlockSpec((S, LANES), lambda h: (0, h + off))
    head = pl.BlockSpec((None, LANES, LANES), lambda h: (h, 0, 0))
    return pl.pallas_call(
        body,
        name=name,
        grid=(H,),
        in_specs=[col(0)] * 10 + [pl.BlockSpec((8, LANES), lambda h: (0, h)), head, head],
        out_specs=[col(0), col(0), head, head, pl.BlockSpec((8, LANES), lambda h: (0, h))],
        out_shape=[jax.ShapeDtypeStruct((S, R), BF16), jax.ShapeDtypeStruct((S, R), BF16),
                   jax.ShapeDtypeStruct((H, LANES, LANES), F32), jax.ShapeDtypeStruct((H, LANES, LANES), F32),
                   jax.ShapeDtypeStruct((8, R), F32)],
        scratch_shapes=[pltpu.VMEM((S + H8, LANES), F32), pltpu.VMEM((S + H8, LANES), F32),
                        pltpu.VMEM((8, LANES), F32)],
        compiler_params=_cparams("parallel", limit=VMEM_LIMIT_SEQ),
    )(proj, h, dgh, *gates, par, wa, wx)


def _window_sum(xs, g, up):
    n = xs.shape[0]
    s = xs
    for lvl, k in enumerate((1, 2, 4, 8)):
        sh = pltpu.roll(s, (n - k) if up else k, 0)
        s = s + jnp.where(g >= lvl, sh, 0.0)
    return s


def _pool_count(grow, g):
    return jnp.minimum(grow + 1, lax.shift_left(jnp.int32(2), g)).astype(F32)


def _pool_fwd(name, u, wgrp, par):
    S, D = u.shape
    G, W = wgrp.shape[0], wgrp.shape[1]
    ch = _tile(S, SEQ_CHUNK)
    nch = S // ch
    PH = POOL_HALO

    def body(u_ref, w_ref, par_ref, zs_ref, pooled_ref, z_ref, u_pad):
        g = pl.program_id(0)
        _pad_copy(u_pad, u_ref, PH, 0)
        par = par_ref[...]
        w = w_ref[...]

        def chunk(ci, _):
            r0 = pl.multiple_of(ci * ch, ch)
            xs = u_pad[pl.ds(r0, ch + PH), :]
            ws = _window_sum(xs, g, False)[PH:]
            uc = xs[PH:]
            cnt = _pool_count(_rows(uc.shape) + r0, g)
            pooled = (ws / cnt - uc).astype(BF16)
            z = jnp.dot(pooled, w, preferred_element_type=F32) + par[0:1]
            pooled_ref[pl.ds(r0, ch), :] = pooled
            z_ref[pl.ds(r0, ch), :] = z
            zs_ref[pl.ds(r0, ch), :] = (z * par[1:2]).astype(zs_ref.dtype)
            return 0

        lax.fori_loop(0, nch, chunk, 0)

    blk = pl.BlockSpec((S, W), lambda g: (0, g))
    return pl.pallas_call(
        body,
        name=name,
        grid=(G,),
        in_specs=[blk, pl.BlockSpec((None, W, W), lambda g: (g, 0, 0)), pl.BlockSpec((2, W), lambda g: (0, g))],
        out_specs=[blk, blk, blk],
        out_shape=[jax.ShapeDtypeStruct((S, D), BF16), jax.ShapeDtypeStruct((S, D), BF16),
                   jax.ShapeDtypeStruct((S, D), F32)],
        scratch_shapes=[pltpu.VMEM((S + PH, W), F32)],
        compiler_params=_cparams("parallel"),
    )(u, wgrp, par)


def _pool_bwd(name, pooled, z, dzs, wgrp, par):
    S, D = z.shape
    G, W = wgrp.shape[0], wgrp.shape[1]
    ch = _tile(S, SEQ_CHUNK)
    nch = S // ch
    PH = POOL_HALO

    def body(pooled_ref, z_ref, dzs_ref, w_ref, par_ref, du_ref, dw_ref, dpar_ref, q_pad, dw_acc):
        g = pl.program_id(0)
        q_pad[pl.ds(S, PH), :] = jnp.zeros((PH, W), F32)
        par = par_ref[...]
        w = w_ref[...]
        dw_acc[...] = jnp.zeros_like(dw_acc)

        def chunk(ci, acc):
            db, dsc = acc
            r0 = pl.multiple_of(ci * ch, ch)
            cnt = _pool_count(_rows((ch, W)) + r0, g)
            pooled = pooled_ref[pl.ds(r0, ch), :]
            z = z_ref[pl.ds(r0, ch), :]
            dzs = dzs_ref[pl.ds(r0, ch), :]
            dz = dzs * par[1:2]
            dzb = dz.astype(BF16)
            dw_acc[...] += lax.dot_general(pooled, dzb, (((0,), (0,)), ((), ())), preferred_element_type=F32)
            dpooled = lax.dot_general(dzb, w, (((1,), (1,)), ((), ())), preferred_element_type=F32)
            q_pad[pl.ds(r0, ch), :] = dpooled / cnt
            return (db + jnp.sum(dz, axis=0, keepdims=True), dsc + jnp.sum(dzs * z, axis=0, keepdims=True))

        zero = jnp.zeros((1, W), F32)
        db, dsc = lax.fori_loop(0, nch, chunk, (zero, zero))
        dpar_ref[...] = jnp.concatenate([db, dsc], axis=0)
        dw_ref[...] = dw_acc[...].astype(dw_ref.dtype)

        def back(ci, _):
            r0 = pl.multiple_of(ci * ch, ch)
            qs = q_pad[pl.ds(r0, ch + PH), :]
            ws = _window_sum(qs, g, True)[:ch]
            qc = qs[:ch]
            cnt = _pool_count(_rows(qc.shape) + r0, g)
            du_ref[pl.ds(r0, ch), :] = (ws - qc * cnt).astype(du_ref.dtype)
            return 0

        lax.fori_loop(0, nch, back, 0)

    blk = pl.BlockSpec((S, W), lambda g: (0, g))
    wspec = pl.BlockSpec((None, W, W), lambda g: (g, 0, 0))
    pspec = pl.BlockSpec((2, W), lambda g: (0, g))
    return pl.pallas_call(
        body,
        name=name,
        grid=(G,),
        in_specs=[blk, blk, blk, wspec, pspec],
        out_specs=[blk, wspec, pspec],
        out_shape=[jax.ShapeDtypeStruct((S, D), BF16), jax.ShapeDtypeStruct((G, W, W), BF16),
                   jax.ShapeDtypeStruct((2, D), F32)],
        scratch_shapes=[pltpu.VMEM((S + PH, W), F32), pltpu.VMEM((W, W), F32)],
        compiler_params=_cparams("parallel"),
    )(pooled, z, dzs, wgrp, par)


def _my_place():
    x, y, c = lax.axis_index("x"), lax.axis_index("y"), lax.axis_index("c")
    return x, y, c, 4 * x + 2 * y + c


def _peers(x, y, c):
    out = []
    for d in range(1, N_DEV):
        px = 1 - x if d & 4 else x
        py = 1 - y if d & 2 else y
        pc = 1 - c if d & 1 else c
        out.append(((px, py, pc), 4 * px + 2 * py + pc))
    return out


def _window(ref, axis, start, size):
    idx = [slice(None)] * len(ref.shape)
    idx[axis] = pl.ds(start, size)
    return ref.at[tuple(idx)]


def _to_bf16(name, arrs, after=()):
    outs = []
    for i, a in enumerate(arrs):
        a2 = a.reshape(-1, a.shape[-1])
        tr = _tile(a2.shape[0], 512)
        o = pl.pallas_call(
            lambda a_ref, *rest: rest[-1].__setitem__(Ellipsis, a_ref[...].astype(BF16)),
            name=f"{name}_{i}",
            grid=(a2.shape[0] // tr,),
            in_specs=[pl.BlockSpec((tr, a2.shape[1]), lambda r: (r, 0))] + [pl.BlockSpec(memory_space=pl.ANY)] * len(after),
            out_specs=pl.BlockSpec((tr, a2.shape[1]), lambda r: (r, 0)),
            out_shape=jax.ShapeDtypeStruct(a2.shape, BF16),
            compiler_params=_cparams("parallel"),
        )(a2, *after)
        outs.append(o.reshape(a.shape))
    return outs


def _all_gather(name, shards, axes):
    n = len(shards)
    sizes = [s.shape[ax] for s, ax in zip(shards, axes)]

    def body(*refs):
        ins, outs = refs[:n], refs[n:2 * n]
        send, recv, loc = refs[2 * n:]
        x, y, c, me = _my_place()
        peers = _peers(x, y, c)
        local = []
        for i in range(n):
            dst = _window(outs[i], axes[i], me * sizes[i], sizes[i])
            cp = pltpu.make_async_copy(ins[i], dst, loc.at[i])
            cp.start()
            local.append(cp)
            for peer, _ in peers:
                pltpu.make_async_remote_copy(src_ref=ins[i], dst_ref=dst, send_sem=send.at[i], recv_sem=recv.at[i],
                                             device_id=peer, device_id_type=MESH).start()
        for i in range(n):
            local[i].wait()
            seven = _window(outs[i], axes[i], 0, (N_DEV - 1) * sizes[i])
            pltpu.make_async_remote_copy(src_ref=seven, dst_ref=seven, send_sem=send.at[i], recv_sem=recv.at[i],
                                         device_id=(x, y, c), device_id_type=MESH).wait()

    def full_shape(s, ax):
        shp = list(s.shape)
        shp[ax] *= N_DEV
        return jax.ShapeDtypeStruct(tuple(shp), s.dtype)

    any_spec = pl.BlockSpec(memory_space=pl.ANY)
    return pl.pallas_call(
        body,
        name=name,
        in_specs=[any_spec] * n,
        out_specs=[any_spec] * n,
        out_shape=[full_shape(s, ax) for s, ax in zip(shards, axes)],
        scratch_shapes=[pltpu.SemaphoreType.DMA((n,)), pltpu.SemaphoreType.DMA((n,)), pltpu.SemaphoreType.DMA((n,))],
        compiler_params=pltpu.CompilerParams(has_side_effects=True),
    )(*shards)


HBM_SPEC = pl.BlockSpec(memory_space=pltpu.HBM)
SEM_SPEC = pl.BlockSpec(memory_space=pltpu.SEMAPHORE)
SPLIT_EFFECT = pltpu.SideEffectType.DATAFLOW_SIDE_EFFECTING


def _push_all(kind, src, dst, axis, size, send_sem, recv_sem, place):
    x, y, c, me = place
    for peer, pidx in _peers(x, y, c):
        if kind == "gather":
            s = d = _window(dst, axis, me * size, size)
        else:
            s, d = _window(src, axis, pidx * size, size), dst.at[me]
        pltpu.make_async_remote_copy(src_ref=s, dst_ref=d, send_sem=send_sem, recv_sem=recv_sem, device_id=peer,
                                     device_id_type=MESH).start()


def _drain_all(kind, dst, axis, size, send_sem, recv_sem, place):
    x, y, c, _ = place
    seven = _window(dst, axis, 0, (N_DEV - 1) * size) if kind == "gather" else dst.at[pl.ds(0, N_DEV - 1)]
    pltpu.make_async_remote_copy(src_ref=seven, dst_ref=seven, send_sem=send_sem, recv_sem=recv_sem,
                                 device_id=(x, y, c), device_id_type=MESH).wait()


def _own_block_placed(src, axis, size, me):
    own = lax.dynamic_slice_in_dim(src, me * size, size, axis)
    return lax.dynamic_update_slice_in_dim(lax.empty((N_DEV,) + own.shape, src.dtype), own[None], me, 0)


def _split_start(name, kind, srcs, lands, axes, sizes, after=()):
    n, ns, na = len(lands), len(srcs), len(after)

    def body(*refs):
        src_refs, land_refs = refs[:ns], refs[ns:ns + n]
        send, recv = refs[ns + n + na], refs[ns + n + na + 1]
        token = refs[-1]
        place = _my_place()
        for k in range(n):
            _push_all(kind, src_refs[k] if ns else None, land_refs[k], axes[k], sizes[k], send.at[k], recv.at[k], place)
        token[...] = jnp.zeros_like(token)

    hbm = lambda a: pltpu.HBM(a.shape, a.dtype)
    res = pl.pallas_call(
        body,
        name=name,
        out_shape=(pltpu.SemaphoreType.DMA((n,)), pltpu.SemaphoreType.DMA((n,)), *[hbm(a) for a in srcs],
                   *[hbm(a) for a in lands], jax.ShapeDtypeStruct((SUBLANES, LANES), F32)),
        in_specs=[HBM_SPEC] * (ns + n) + [pl.BlockSpec(memory_space=pl.ANY)] * na,
        out_specs=(SEM_SPEC, SEM_SPEC, *[HBM_SPEC] * (ns + n), pl.BlockSpec(memory_space=pltpu.VMEM)),
        input_output_aliases={k: 2 + k for k in range(ns + n)},
        compiler_params=pltpu.CompilerParams(has_side_effects=SPLIT_EFFECT),
    )(*[pltpu.with_memory_space_constraint(a, pltpu.HBM) for a in (*srcs, *lands)], *after)
    return res[0], res[1], list(res[2:2 + ns]), list(res[2 + ns:2 + ns + n]), res[-1]


def _split_wait(name, kind, handle, axes, sizes, after):
    send, recv, srcs, lands, _ = handle
    n, ns = len(lands), len(srcs)
    after = list(after) if isinstance(after, (list, tuple)) else [after]

    def body(*refs):
        land_refs = refs[ns:ns + n]
        send_ref, recv_ref = refs[ns + n], refs[ns + n + 1]
        place = _my_place()
        for k in range(n):
            _drain_all(kind, land_refs[k], axes[k], sizes[k], send_ref.at[k], recv_ref.at[k], place)

    hbm = lambda a: pltpu.HBM(a.shape, a.dtype)
    res = pl.pallas_call(
        body,
        name=name,
        out_shape=tuple(hbm(a) for a in (*srcs, *lands)),
        in_specs=[HBM_SPEC] * (ns + n) + [SEM_SPEC, SEM_SPEC] + [pl.BlockSpec(memory_space=pl.ANY)] * len(after),
        out_specs=tuple([HBM_SPEC] * (ns + n)),
        input_output_aliases={k: k for k in range(ns + n)},
        compiler_params=pltpu.CompilerParams(has_side_effects=SPLIT_EFFECT),
    )(*srcs, *lands, send, recv, *after)
    return list(res[ns:])


def _cast_into_window(name, a, l, axis, me1, after=()):
    shp = a.shape[1:]
    cast = lambda me_ref, a_ref, *rest: rest[-1].__setitem__(Ellipsis, a_ref[...].astype(BF16))
    if len(shp) == 3:
        assert axis == 1
        G, r, c = shp
        full = (G, r * N_DEV, c)
        grid = (G,)
        in_spec = pl.BlockSpec((None, None, r, c), lambda g, me: (l, g, 0, 0))
        out_spec = pl.BlockSpec((None, r, c), lambda g, me: (g, me[0], 0))
    else:
        r, c = shp
        tr = _tile(r, 512)
        nb = r // tr
        grid = (nb,)
        in_spec = pl.BlockSpec((None, tr, c), lambda i, me: (l, i, 0))
        if axis == 0:
            full = (r * N_DEV, c)
            out_spec = pl.BlockSpec((tr, c), lambda i, me: (me[0] * nb + i, 0))
        else:
            full = (r, c * N_DEV)
            out_spec = pl.BlockSpec((tr, c), lambda i, me: (i, me[0]))
    return pl.pallas_call(
        cast,
        name=name,
        grid_spec=pltpu.PrefetchScalarGridSpec(
            num_scalar_prefetch=1, grid=grid,
            in_specs=[in_spec] + [pl.BlockSpec(memory_space=pl.ANY)] * len(after), out_specs=out_spec),
        out_shape=jax.ShapeDtypeStruct(full, BF16),
        compiler_params=_cparams("arbitrary"),
    )(me1, a, *after)


def _adamw_math(w, g, m, v):
    m = ADAM_B1 * m + (1.0 - ADAM_B1) * g
    v = ADAM_B2 * v + (1.0 - ADAM_B2) * jnp.square(g)
    m_hat = m / (1.0 - ADAM_B1 ** ADAM_STEP)
    v_hat = v / (1.0 - ADAM_B2 ** ADAM_STEP)
    delta = -ADAM_LR * (m_hat / (jnp.sqrt(v_hat) + ADAM_EPS) + ADAM_WD * w)
    return delta, m, v


def _sum_slots(buf_ref):
    g = buf_ref[0].astype(F32)
    for s in range(1, N_DEV):
        g = g + buf_ref[s].astype(F32)
    return g


def _adamw_layer(name, buf, w, m, v, l, prev, after=()):
    shape = w.shape
    L, C = shape[0], shape[-1]
    Rr = math.prod(shape[1:-1])
    buf3 = buf.reshape(N_DEV, Rr, C)
    w3, m3, v3 = (t.reshape(L, Rr, C) for t in (w, m, v))
    tr = _tile(Rr, 2 * LANES) if Rr % LANES == 0 else Rr
    n_prev = 0 if prev is None else 4

    def body(buf_ref, w_ref, m_ref, v_ref, *rest):
        g_out, d_out, m_out, v_out = rest[n_prev + len(after):]
        g = _sum_slots(buf_ref)
        d, mm, vv = _adamw_math(w_ref[...], g, m_ref[...], v_ref[...])
        g_out[...] = g
        d_out[...] = d
        m_out[...] = mm
        v_out[...] = vv

    spec = pl.BlockSpec((None, tr, C), lambda r: (l, r, 0))
    outs = pl.pallas_call(
        body,
        name=name,
        grid=(Rr // tr,),
        in_specs=[pl.BlockSpec((N_DEV, tr, C), lambda r: (0, r, 0)), spec, spec, spec]
        + [pl.BlockSpec(memory_space=pl.ANY)] * (n_prev + len(after)),
        out_specs=[spec] * 4,
        out_shape=[jax.ShapeDtypeStruct((L, Rr, C), F32)] * 4,
        input_output_aliases={4 + k: k for k in range(n_prev)},
        compiler_params=_cparams("parallel"),
    )(buf3, w3, m3, v3, *(prev or ()), *after)
    return list(outs)


def _sum8(name, buf):
    R = buf.shape[1]

    def body(buf_ref, o_ref):
        o_ref[...] = _sum_slots(buf_ref)

    return pl.pallas_call(
        body,
        name=name,
        in_specs=[pl.BlockSpec(buf.shape, lambda: (0, 0, 0))],
        out_specs=pl.BlockSpec((R, LANES), lambda: (0, 0)),
        out_shape=jax.ShapeDtypeStruct((R, LANES), F32),
        compiler_params=_cparams(),
    )(buf)


def _adamw_small(name, w, g, m, v):
    shape = w.shape
    w2, g2, m2, v2 = (t.reshape(-1, shape[-1]) for t in (w, g, m, v))
    R, C = w2.shape
    tr = _row_tile(R, 512)

    def body(w_ref, g_ref, m_ref, v_ref, d_out, m_out, v_out):
        d, mm, vv = _adamw_math(w_ref[...], g_ref[...], m_ref[...], v_ref[...])
        d_out[...] = d
        m_out[...] = mm
        v_out[...] = vv

    spec = pl.BlockSpec((tr, C), lambda r: (r, 0))
    outs = pl.pallas_call(
        body,
        name=name,
        grid=(R // tr,),
        in_specs=[spec] * 4,
        out_specs=[spec] * 3,
        out_shape=[jax.ShapeDtypeStruct((R, C), F32)] * 3,
        compiler_params=_cparams("parallel"),
    )(w2, g2, m2, v2)
    return [o.reshape(shape) for o in outs]


def _pack(arrs, pad_rows_to=SUBLANES):
    parts = []
    for a in arrs:
        flat = a.reshape(-1)
        per = LANES * pad_rows_to
        padded = -(-flat.shape[0] // per) * per
        if padded != flat.shape[0]:
            flat = jnp.pad(flat, (0, padded - flat.shape[0]))
        parts.append(flat.reshape(-1, LANES))
    return jnp.concatenate(parts, axis=0)


def _unpack(packed, shapes, pad_rows_to=SUBLANES):
    out = []
    r = 0
    for shp in shapes:
        nel = math.prod(shp)
        per = LANES * pad_rows_to
        rows = -(-nel // per) * pad_rows_to
        out.append(packed[r:r + rows].reshape(-1)[:nel].reshape(shp))
        r += rows
    return out


BIG = ("lru_w_in", "lru_w_out", "pool_w_in", "pool_w_grp", "pool_w_out", "mlp_w1", "mlp_w2", "ple_w", "ple_gate_w")
BIG_AXIS = {"lru_w_in": 2, "lru_w_out": 1, "pool_w_in": 1, "pool_w_grp": 2, "pool_w_out": 1, "mlp_w1": 2,
            "mlp_w2": 1, "ple_w": 2, "ple_gate_w": 1}
SMALL_SHARDED = ("lru_conv_w", "pool_b_grp", "pool_scale")
REPLICATED = ("lru_conv_b", "lru_wa", "lru_ba", "lru_wx", "lru_bx", "lru_lambda", "ln_mix_g", "ln_mix_b",
              "ln_mlp_g", "ln_mlp_b", "ple_gate_b")
WEIGHTS = ("lru_w_in", "lru_conv_w", "lru_conv_b", "lru_wa", "lru_ba", "lru_wx", "lru_bx", "lru_lambda", "lru_w_out",
           "pool_w_in", "pool_w_grp", "pool_b_grp", "pool_scale", "pool_w_out", "ln_mix_g", "ln_mix_b", "mlp_w1",
           "mlp_w2", "ln_mlp_g", "ln_mlp_b", "ple_w", "ple_gate_w", "ple_gate_b")
INPUTS = ("x", "p") + WEIGHTS + ("loss_target",) + tuple("m_" + n for n in WEIGHTS) + tuple("v_" + n for n in WEIGHTS)


def _gather_last_axis(packed_full, shard_shape):
    nel = math.prod(shard_shape)
    blocks = packed_full.reshape(N_DEV, -1)[:, :nel].reshape((N_DEV,) + tuple(shard_shape))
    return jnp.concatenate([blocks[d] for d in range(N_DEV)], axis=-1)


def kernel(x, p, lru_w_in, lru_conv_w, lru_conv_b, lru_wa, lru_ba, lru_wx, lru_bx, lru_lambda, lru_w_out, pool_w_in, pool_w_grp, pool_b_grp, pool_scale, pool_w_out, ln_mix_g, ln_mix_b, mlp_w1, mlp_w2, ln_mlp_g, ln_mlp_b, ple_w, ple_gate_w, ple_gate_b, loss_target, m_lru_w_in, m_lru_conv_w, m_lru_conv_b, m_lru_wa, m_lru_ba, m_lru_wx, m_lru_bx, m_lru_lambda, m_lru_w_out, m_pool_w_in, m_pool_w_grp, m_pool_b_grp, m_pool_scale, m_pool_w_out, m_ln_mix_g, m_ln_mix_b, m_mlp_w1, m_mlp_w2, m_ln_mlp_g, m_ln_mlp_b, m_ple_w, m_ple_gate_w, m_ple_gate_b, v_lru_w_in, v_lru_conv_w, v_lru_conv_b, v_lru_wa, v_lru_ba, v_lru_wx, v_lru_bx, v_lru_lambda, v_lru_w_out, v_pool_w_in, v_pool_w_grp, v_pool_b_grp, v_pool_scale, v_pool_w_out, v_ln_mix_g, v_ln_mix_b, v_mlp_w1, v_mlp_w2, v_ln_mlp_g, v_ln_mlp_b, v_ple_w, v_ple_gate_w, v_ple_gate_b):
    A = dict(zip(INPUTS, (x, p, lru_w_in, lru_conv_w, lru_conv_b, lru_wa, lru_ba, lru_wx, lru_bx, lru_lambda, lru_w_out, pool_w_in, pool_w_grp, pool_b_grp, pool_scale, pool_w_out, ln_mix_g, ln_mix_b, mlp_w1, mlp_w2, ln_mlp_g, ln_mlp_b, ple_w, ple_gate_w, ple_gate_b, loss_target, m_lru_w_in, m_lru_conv_w, m_lru_conv_b, m_lru_wa, m_lru_ba, m_lru_wx, m_lru_bx, m_lru_lambda, m_lru_w_out, m_pool_w_in, m_pool_w_grp, m_pool_b_grp, m_pool_scale, m_pool_w_out, m_ln_mix_g, m_ln_mix_b, m_mlp_w1, m_mlp_w2, m_ln_mlp_g, m_ln_mlp_b, m_ple_w, m_ple_gate_w, m_ple_gate_b, v_lru_w_in, v_lru_conv_w, v_lru_conv_b, v_lru_wa, v_lru_ba, v_lru_wx, v_lru_bx, v_lru_lambda, v_lru_w_out, v_pool_w_in, v_pool_w_grp, v_pool_b_grp, v_pool_scale, v_pool_w_out, v_ln_mix_g, v_ln_mix_b, v_mlp_w1, v_mlp_w2, v_ln_mlp_g, v_ln_mlp_b, v_ple_w, v_ple_gate_w, v_ple_gate_b)))
    depth = ln_mix_g.shape[0]
    alpha = (2 * depth) ** 0.25
    S, D = x.shape[1], x.shape[2]
    xs = x.reshape(S, D)
    tgt = loss_target.reshape(S, D)
    p3 = p.reshape(depth, S, p.shape[-1])
    me = 4 * lax.axis_index("x") + 2 * lax.axis_index("y") + lax.axis_index("c")

    def layer_weights(i):
        s = i // 2
        mixer = ("lru_w_in", "lru_w_out") if i % 2 == 0 else ("pool_w_in", "pool_w_grp", "pool_w_out")
        return [(n, s) for n in mixer] + [(n, i) for n in ("mlp_w1", "mlp_w2", "ple_w", "ple_gate_w")]

    def axis_of(key):
        return 0 if key[0] == "small" else BIG_AXIS[key[0]] - 1

    def start_gather(tag, keys, after):
        axes = [axis_of(k) for k in keys]
        lands = [land[k] for k in keys]
        sizes = [a.shape[ax] // N_DEV for a, ax in zip(lands, axes)]
        return keys, _split_start(f"gather_{tag}_start", "gather", [], lands, axes, sizes, after=after), axes, sizes

    def finish_gather(tag, pending, after):
        keys, handle, axes, sizes = pending
        for (n, l), full in zip(keys, _split_wait(f"gather_{tag}_wait", "gather", handle, axes, sizes, after)):
            W[n][l] = full

    def start_exchange(tag, keys, arrs, after):
        axes = [axis_of(k) for k in keys]
        sizes = [a.shape[ax] // N_DEV for a, ax in zip(arrs, axes)]
        lands = [_own_block_placed(a, ax, sz, me) for a, ax, sz in zip(arrs, axes, sizes)]
        return keys, _split_start(f"exchange_{tag}_start", "scatter", arrs, lands, axes, sizes, after=after), axes, sizes

    def finish_exchange(tag, pending, after):
        keys, handle, axes, sizes = pending
        partial.update(zip(keys, _split_wait(f"exchange_{tag}_wait", "scatter", handle, axes, sizes, after)))

    me1 = jnp.reshape(me, (1,)).astype(jnp.int32)
    land = {}
    W = {n: [None] * A[n].shape[0] for n in BIG}
    small_shard_shapes = [A[n].shape for n in SMALL_SHARDED]
    gathered = _all_gather("gather_small_params", [_pack([A[n] for n in SMALL_SHARDED])], [0])
    def gather_groups(i):
        keys = layer_weights(i)
        mixer, (w1, w2, pw, pg) = keys[:-4], keys[-4:]
        if i % 2 == 0:
            return [("in", mixer[:1]), ("out", mixer[1:]), ("up", [w1]), ("rest", [w2, pw, pg])]
        return [("in", mixer + [w1]), ("rest", [w2, pw, pg])]

    gather_pending = {}

    def send_layer(i, behind):
        for tag, keys in gather_groups(i):
            for k in keys:
                land[k] = _cast_into_window(f"cast_{k[0]}_{k[1]}", A[k[0]], k[1], axis_of(k), me1, after=behind)
            gather_pending[(i, tag)] = start_gather(f"l{i}_{tag}", keys, behind)
            behind = (gather_pending[(i, tag)][1][4],)
        return behind

    layer0_started = send_layer(0, (gathered[0],))
    small_full = gathered[0].reshape(N_DEV, -1, LANES)
    r = 0
    for n, shp in zip(SMALL_SHARDED, small_shard_shapes):
        rows = -(-math.prod(shp) // (LANES * SUBLANES)) * SUBLANES
        W[n] = _gather_last_axis(small_full[:, r:r + rows], shp)
        r += rows
    wa_b, wx_b = _to_bf16("cast_gates", [lru_wa, lru_wx], after=layer0_started)
    n_lru = lru_w_in.shape[0]
    lru_par = [jnp.concatenate([W["lru_conv_w"][s], lru_conv_b[s][None], lru_ba[s][None], lru_bx[s][None],
                                lru_lambda[s][None]], axis=0) for s in range(n_lru)]
    pool_par = [jnp.stack([W["pool_b_grp"][s], W["pool_scale"][s]], axis=0) for s in range(pool_w_in.shape[0])]

    saved = []
    h_in = xs
    (h_in_b,) = _to_bf16("cast_inputs", [xs], after=layer0_started)
    layer1_started = send_layer(1, (layer0_started[0], h_in_b))
    finish_gather("l0_in", gather_pending[(0, "in")], [h_in_b, wa_b, layer1_started[0]])
    for i in range(depth):
        s = i // 2
        sv = {"x0b": h_in_b}
        if i > 0:
            finish_gather(f"l{i}_in", gather_pending[(i, "in")], h_in)
        ln_out = dict(out_dtypes=[F32, F32, BF16], tm=MM_TM_ROWS, tn=D,
                      epi=lambda acc, xp, g, b: _ln_apply(alpha * xp + acc, g, b))
        if i % 2 == 0:
            sv["proj"] = _mm(f"l{i}_lru_in", h_in_b, W["lru_w_in"], "nn", [F32], b_lead=s)
            if i == 0:
                behind = (layer1_started[0], sv["proj"])
                for later in range(2, depth):
                    behind = send_layer(later, behind)
                all_started = behind[0]
            sv["gh"], sv["h"], *sv["gates"] = _lru_fwd(f"l{i}_lru_core", sv["proj"], lru_par[s], wa_b[s], wx_b[s])
            finish_gather(f"l{i}_out", gather_pending[(i, "out")], [sv["gh"], all_started])
            mix_in, mix_w = sv["gh"], W["lru_w_out"]
        else:
            u_pool = _mm(f"l{i}_pool_in", h_in_b, W["pool_w_in"], "nn", [F32], b_lead=s)
            sv["zs"], sv["pooled"], sv["z"] = _pool_fwd(f"l{i}_pool_core", u_pool, W["pool_w_grp"][s], pool_par[s])
            mix_in, mix_w = sv["zs"], W["pool_w_out"]
        sv["z1"], sv["x1"], sv["x1b"] = _mm(f"l{i}_mix_out_ln", mix_in, mix_w, "nn", b_lead=s,
                                            extras=[h_in, ln_mix_g[i][None], ln_mix_b[i][None]], **ln_out)
        if i % 2 == 0:
            finish_gather(f"l{i}_up", gather_pending[(i, "up")], sv["x1b"])
        sv["hpre"], sv["hact"] = _mm(f"l{i}_mlp_up", sv["x1b"], W["mlp_w1"], "nn", [BF16, BF16], b_lead=i,
                                     epi=lambda acc: (acc, jnp.square(jnp.maximum(acc, 0.0))))
        finish_gather(f"l{i}_rest", gather_pending[(i, "rest")], sv["hact"])
        sv["z2"], sv["x2"], sv["x2b"] = _mm(f"l{i}_mlp_down_ln", sv["hact"], W["mlp_w2"], "nn", b_lead=i,
                                            extras=[sv["x1"], ln_mlp_g[i][None], ln_mlp_b[i][None]], **ln_out)
        sv["pp"] = _mm(f"l{i}_ple_up", p3, W["ple_w"], "nn", [F32], a_lead=i, b_lead=i)

        def ple_epi(acc, bg, x2t, ppt):
            gpre = acc + bg
            x3 = x2t + ppt * jax.nn.sigmoid(gpre)
            return x3, x3, gpre

        h_in, h_in_b, sv["gpre"] = _mm(f"l{i}_ple_gate", sv["x2b"], W["ple_gate_w"], "nn", [F32, BF16, F32], b_lead=i,
                                       epi=ple_epi, extras=[ple_gate_b[i][None], sv["x2"], sv["pp"]], tn=MM_TN)
        saved.append(sv)

    dx, dpp, dgpre, sq, dbg = _loss_and_grad("loss", h_in, tgt, saved[-1]["gpre"], saved[-1]["pp"])
    loss = lax.psum(0.5 * sq[0, 0] / D, ("x", "y", "c"))

    dW = {n: [None] * A[n].shape[0] for n in BIG}
    dsmall = {n: [None] * A[n].shape[0] for n in REPLICATED + SMALL_SHARDED}
    small_names = REPLICATED + SMALL_SHARDED
    partial = {}
    exchange_pending = {}
    exchange_token = ()
    for i in reversed(range(depth)):
        s = i // 2
        sv = saved[i]
        dsmall["ple_gate_b"][i] = dbg[0]
        dW["ple_w"][i] = _mm(f"l{i}_d_ple_w", p3, dpp, "tn", [BF16], a_lead=i, after=exchange_token)
        dW["ple_gate_w"][i] = _mm(f"l{i}_d_ple_gate_w", sv["x2b"], dgpre, "tn", [BF16])
        ln_back = dict(out_dtypes=[F32, BF16], tm=MM_TM_ROWS, tn=D, n_sums=2)
        dz2, dz2b, dg, db = _mm(f"l{i}_d_x2_ln", dgpre, W["ple_gate_w"], "nt", b_lead=i,
                                extras=[dx, sv["z2"], ln_mlp_g[i][None]],
                                epi=lambda acc, d, z, g: _ln_grad(acc + d, z, g), after=exchange_token, **ln_back)
        dsmall["ln_mlp_g"][i], dsmall["ln_mlp_b"][i] = dg[0], db[0]
        dhpre = _mm(f"l{i}_d_hpre", dz2b, W["mlp_w2"], "nt", [BF16], b_lead=i, extras=[sv["hpre"]],
                    epi=lambda acc, hp: (acc * (2.0 * jnp.maximum(hp.astype(F32), 0.0)),))
        dW["mlp_w2"][i] = _mm(f"l{i}_d_mlp_w2", sv["hact"], dz2b, "tn", [BF16])
        dW["mlp_w1"][i] = _mm(f"l{i}_d_mlp_w1", sv["x1b"], dhpre, "tn", [BF16])
        mlp_after = ()
        if i == 0:
            early = [(n, 0) for n in ("ple_w", "ple_gate_w", "mlp_w2", "mlp_w1")]
            exchange_early0 = start_exchange("early0", early, [dW[n][l] for n, l in early], ())
            mlp_after = (exchange_early0[1][4],)
        dz1, dz1b, dg, db = _mm(f"l{i}_d_x1_ln", dhpre, W["mlp_w1"], "nt", b_lead=i,
                                extras=[dz2, sv["z1"], ln_mix_g[i][None]],
                                epi=lambda acc, d, z, g: _ln_grad(acc + alpha * d, z, g), after=mlp_after, **ln_back)
        dsmall["ln_mix_g"][i], dsmall["ln_mix_b"][i] = dg[0], db[0]
        if i % 2 == 0:
            dW["lru_w_out"][s] = _mm(f"l{i}_d_lru_w_out", sv["gh"], dz1b, "tn", [BF16])
            dgh = _mm(f"l{i}_d_gh", dz1b, W["lru_w_out"], "nt", [F32], b_lead=s)
            dup, dy, dwa, dwx, dpar = _lru_bwd(f"l{i}_lru_core_bwd", sv["proj"], sv["h"], dgh, sv["gates"],
                                               lru_par[s], wa_b[s], wx_b[s])
            dsmall["lru_wa"][s], dsmall["lru_wx"][s] = dwa, dwx
            dsmall["lru_conv_w"][s] = dpar[0:4]
            for k, n in enumerate(("lru_conv_b", "lru_ba", "lru_bx", "lru_lambda")):
                dsmall[n][s] = dpar[4 + k]
            dmix_in = jnp.concatenate([dup, dy], axis=1)
            win = "lru_w_in"
        else:
            dW["pool_w_out"][s] = _mm(f"l{i}_d_pool_w_out", sv["zs"], dz1b, "tn", [BF16])
            dzs = _mm(f"l{i}_d_zs", dz1b, W["pool_w_out"], "nt", [F32], b_lead=s)
            dmix_in, dW["pool_w_grp"][s], dpar = _pool_bwd(f"l{i}_pool_core_bwd", sv["pooled"], sv["z"], dzs,
                                                          W["pool_w_grp"][s], pool_par[s])
            dsmall["pool_b_grp"][s], dsmall["pool_scale"][s] = dpar[0], dpar[1]
            win = "pool_w_in"
        dW[win][s] = _mm(f"l{i}_d_{win}", sv["x0b"], dmix_in, "tn", [BF16])
        x0_after = ()
        if i > 0:
            keys = layer_weights(i)
            exchange_pending[i] = start_exchange(f"l{i}", keys, [dW[n][l] for n, l in keys], ())
            exchange_token = (exchange_pending[i][1][4],)
        else:
            small_grads = [jnp.stack(dsmall[n]) for n in small_names]
            small_shapes = [g.shape for g in small_grads]
            packed_g = _pack(small_grads)
            assert packed_g.shape[0] % (N_DEV * SUBLANES) == 0, packed_g.shape
            late = [("lru_w_out", 0), ("lru_w_in", 0), ("small", 0)]
            exchange_late0 = start_exchange("late0", late, [dW["lru_w_out"][0], dW["lru_w_in"][0], packed_g], ())
            x0_after = (exchange_late0[1][4],)
        if i > 0:
            def x0_epi(acc, d, gpre, pp):
                dxv = acc + alpha * d
                return (dxv,) + _ple_grad(dxv, gpre, pp)

            dx, dpp, dgpre, dbg = _mm(f"l{i}_d_x0_ple", dmix_in, W[win], "nt", [F32, BF16, BF16], b_lead=s,
                                      extras=[dz1, saved[i - 1]["gpre"], saved[i - 1]["pp"]], epi=x0_epi,
                                      tm=MM_TM_ROWS, tn=D, n_sums=1)
        else:
            dx = _mm(f"l{i}_d_x0", dmix_in, W[win], "nt", [F32], b_lead=s, extras=[dz1],
                     epi=lambda acc, d: (acc + alpha * d,), after=x0_after)
    grad_x = dx.reshape(x.shape)

    for i in range(1, depth):
        finish_exchange(f"l{i}", exchange_pending[i], x0_after[0])
    stacked = {n: None for n in BIG}
    layer0 = layer_weights(0)

    def adamw(n, l, after=()):
        stacked[n] = _adamw_layer(f"adamw_{n}_{l}", partial[(n, l)], A[n], A["m_" + n], A["v_" + n], l, stacked[n],
                                  after=after)

    for n in BIG:
        for l in reversed(range(A[n].shape[0])):
            if (n, l) not in layer0:
                adamw(n, l)
    behind = [dx] + [stacked[n][0] for n in BIG if stacked[n] is not None]
    finish_exchange("early0", exchange_early0, behind)
    finish_exchange("late0", exchange_late0, behind)
    red = _sum8("sum_small", partial[("small", 0)])
    rows = red.shape[0]
    red_land = lax.dynamic_update_slice_in_dim(lax.empty((N_DEV * rows, LANES), F32), red, me * rows, 0)
    small_handle = _split_start("gather_small_start", "gather", [], [red_land], [0], [rows])
    for n, l in layer0:
        adamw(n, l, after=(small_handle[4],))
    outs = {n: [o.reshape(A[n].shape) for o in stacked[n]] for n in BIG}
    red_full = _split_wait("gather_small_wait", "gather", small_handle, [0], [rows], [stacked[n][0] for n, _ in layer0])[0]
    small_g = dict(zip(small_names, _unpack(red_full, small_shapes)))
    for n in SMALL_SHARDED:
        width = A[n].shape[-1]
        small_g[n] = lax.dynamic_slice_in_dim(small_g[n], me * width, width, axis=small_g[n].ndim - 1)
    for n in small_names:
        outs[n] = [small_g[n]] + _adamw_small(f"adamw_{n}", A[n], small_g[n], A["m_" + n], A["v_" + n])

    return (loss, grad_x, *[outs[n][0] for n in WEIGHTS], *[outs[n][1] for n in WEIGHTS],
            *[outs[n][2] for n in WEIGHTS], *[outs[n][3] for n in WEIGHTS])
```

```python
import functools
import math

import jax
import jax.numpy as jnp
from jax import lax
from jax.experimental import pallas as pl
from jax.experimental.pallas import tpu as pltpu

F32 = jnp.float32
BF16 = jnp.bfloat16
MESH = pl.DeviceIdType.MESH
N_DEV = 8
LANES = 128
SUBLANES = 8

LN_EPS = 1e-5
LRU_C = 8.0
CONV_WIDTH = 4
POOL_HALO = 16
ADAM_LR = 0.001
ADAM_B1 = 0.9
ADAM_B2 = 0.999
ADAM_EPS = 1e-08
ADAM_WD = 0.01
ADAM_STEP = 10

VMEM_LIMIT = 48 * 1024 * 1024
VMEM_LIMIT_SEQ = 56 * 1024 * 1024
SEQ_CHUNK = 256
MM_TK = 4096
MM_TK_TOKENS = 4096
MM_TM_ROWS = 512
MM_TN = 512
MM_TN_WIDE = 1024
MM_TN_WIDE_MAX_K = 2048
GELU_C0 = math.sqrt(2.0 / math.pi)
GELU_C1 = 0.044715


def _in_hbm(*arrays):
    return [pltpu.with_memory_space_constraint(a, pltpu.HBM) for a in arrays]


def _cparams(*sem, limit=None):
    return pltpu.CompilerParams(dimension_semantics=tuple(sem) if sem else None,
                                vmem_limit_bytes=VMEM_LIMIT if limit is None else limit)


def _tile(n, pref):
    if n <= pref:
        return n
    t = pref - pref % LANES
    while t >= LANES:
        if n % t == 0:
            return t
        t -= LANES
    return n


def _row_tile(n, pref):
    if n <= pref:
        return n
    t = pref - pref % SUBLANES
    while t >= SUBLANES:
        if n % t == 0:
            return t
        t -= SUBLANES
    return n


def _mm(name, a, b, mode, out_dtypes, epi=None, extras=(), a_lead=None, b_lead=None, tm=1024, tn=None, tk=None,
        after=(), n_sums=0):
    if isinstance(b, (list, tuple)):
        b, b_lead = b[b_lead], None
    a2 = a.shape[-2:]
    b2 = b.shape[-2:]
    if mode == "nn":
        (M, K), N = a2, b2[1]
        assert b2[0] == K
    elif mode == "nt":
        (M, K), N = a2, b2[0]
        assert b2[1] == K
    else:
        (K, M), N = a2, b2[1]
        assert b2[0] == K
    if tk is None:
        tk = MM_TK_TOKENS if mode == "tn" else MM_TK
    if tn is None:
        tn = MM_TN_WIDE if (mode != "tn" and K <= MM_TN_WIDE_MAX_K) else MM_TN
    tm, tn, tk = _tile(M, tm), _tile(N, tn), _tile(K, tk)
    nk = K // tk
    n_extra = len(extras)
    n_out = len(out_dtypes)
    assert n_sums == 0 or tn == N
    resident = {"pipeline_mode": pl.Buffered(1)} if (tn == N and nk == 1) else {}

    def lead(shape, idx, which, **kw):
        if which is None:
            return pl.BlockSpec(shape, idx, **kw)
        return pl.BlockSpec((None,) + shape, lambda i, j, k: (which,) + idx(i, j, k), **kw)

    if mode == "nn":
        a_spec = lead((tm, tk), lambda i, j, k: (i, k), a_lead)
        b_spec = lead((tk, tn), lambda i, j, k: (k, j), b_lead, **resident)
        dims = (((1,), (0,)), ((), ()))
    elif mode == "nt":
        a_spec = lead((tm, tk), lambda i, j, k: (i, k), a_lead)
        b_spec = lead((tn, tk), lambda i, j, k: (j, k), b_lead, **resident)
        dims = (((1,), (1,)), ((), ()))
    else:
        a_spec = lead((tk, tm), lambda i, j, k: (k, i), a_lead)
        b_spec = lead((tk, tn), lambda i, j, k: (k, j), b_lead, **resident)
        dims = (((0,), (0,)), ((), ()))
    e_specs = []
    for e in extras:
        if e.shape[0] == 1:
            e_specs.append(pl.BlockSpec((1, tn), lambda i, j, k: (0, j)))
        else:
            e_specs.append(pl.BlockSpec((tm, tn), lambda i, j, k: (i, j)))

    n_after = len(after)

    def body(a_ref, b_ref, *rest):
        e_refs = rest[:n_extra]
        rest = rest[:n_extra] + rest[n_extra + n_after:]
        o_refs = rest[n_extra:n_extra + n_out]
        s_refs = rest[n_extra + n_out:n_extra + n_out + n_sums]
        part = lax.dot_general(a_ref[...].astype(BF16), b_ref[...].astype(BF16), dims, preferred_element_type=F32)

        def finish(r):
            res = (r,) if epi is None else epi(r, *[e[...] for e in e_refs])
            for o, v in zip(o_refs, res[:n_out]):
                o[...] = v.astype(o.dtype)
            first = pl.program_id(0) == 0
            for sr, v in zip(s_refs, res[n_out:]):
                @pl.when(first)
                def _(sr=sr, v=v):
                    sr[...] = v

                @pl.when(jnp.logical_not(first))
                def _(sr=sr, v=v):
                    sr[...] += v

        if nk == 1:
            finish(part)
            return
        acc = rest[n_extra + n_out + n_sums]
        k = pl.program_id(2)

        @pl.when(k == 0)
        def _():
            acc[...] = part

        @pl.when(jnp.logical_and(k > 0, k < nk - 1))
        def _():
            acc[...] += part

        @pl.when(k == nk - 1)
        def _():
            finish(acc[...] + part)

    outs = pl.pallas_call(
        body,
        name=name,
        grid=(M // tm, N // tn, nk),
        in_specs=[a_spec, b_spec] + e_specs + [pl.BlockSpec(memory_space=pl.ANY)] * n_after,
        out_specs=[pl.BlockSpec((tm, tn), lambda i, j, k: (i, j)) for _ in out_dtypes]
        + [pl.BlockSpec((1, tn), lambda i, j, k: (0, 0))] * n_sums,
        out_shape=[jax.ShapeDtypeStruct((M, N), d) for d in out_dtypes] + [jax.ShapeDtypeStruct((1, N), F32)] * n_sums,
        scratch_shapes=[pltpu.VMEM((tm, tn), F32)] if nk > 1 else [],
        compiler_params=_cparams(*(("arbitrary",) * 3 if n_sums else ("parallel", "parallel", "arbitrary"))),
    )(*_in_hbm(a, b, *extras), *after)
    return outs[0] if n_out + n_sums == 1 else tuple(outs)


def _rowwise(name, fn, tiled, params, outs, accs=(), tm=256, after=()):
    S = tiled[0].shape[0]
    tm = _tile(S, tm)
    nt, npar, no = len(tiled), len(params), len(outs)
    n_after = len(after)

    def body(*refs):
        t_refs = refs[:nt]
        p_refs = refs[nt:nt + npar]
        refs = refs[nt + npar + n_after:]
        o_refs = refs[:no]
        a_refs = refs[no:]
        res = fn(*[r[...] for r in t_refs], *[r[...] for r in p_refs])
        for o, v in zip(o_refs, res[:no]):
            o[...] = v.astype(o.dtype)
        first = pl.program_id(0) == 0
        for ar, v in zip(a_refs, res[no:]):
            @pl.when(first)
            def _(ar=ar, v=v):
                ar[...] = v

            @pl.when(jnp.logical_not(first))
            def _(ar=ar, v=v):
                ar[...] += v

    full = lambda p: pl.BlockSpec(p.shape, lambda i, nd=p.ndim: (0,) * nd)
    res = pl.pallas_call(
        body,
        name=name,
        grid=(S // tm,),
        in_specs=[pl.BlockSpec((tm, t.shape[1]), lambda i: (i, 0)) for t in tiled] + [full(p) for p in params]
        + [pl.BlockSpec(memory_space=pl.ANY)] * n_after,
        out_specs=[pl.BlockSpec((tm, c), lambda i: (i, 0)) for c, _ in outs]
        + [pl.BlockSpec(s, lambda i, nd=len(s): (0,) * nd) for s in accs],
        out_shape=[jax.ShapeDtypeStruct((S, c), d) for c, d in outs] + [jax.ShapeDtypeStruct(s, F32) for s in accs],
        compiler_params=_cparams("arbitrary"),
    )(*_in_hbm(*tiled, *params), *after)
    return res


def _ln_stats(z):
    mu = jnp.mean(z, axis=-1, keepdims=True)
    zc = z - mu
    var = jnp.mean(zc * zc, axis=-1, keepdims=True)
    return zc, lax.rsqrt(var + LN_EPS)


def _ln_apply(z, g, b):
    zc, rstd = _ln_stats(z)
    y = zc * rstd * g + b
    return z, y, y


def _ln_grad(dy, z, g):
    zc, rstd = _ln_stats(z)
    xhat = zc * rstd
    dxh = dy * g
    m1 = jnp.mean(dxh, axis=-1, keepdims=True)
    m2 = jnp.mean(dxh * xhat, axis=-1, keepdims=True)
    dz = rstd * (dxh - m1 - xhat * m2)
    return dz, dz, jnp.sum(dy * xhat, axis=0, keepdims=True), jnp.sum(dy, axis=0, keepdims=True)


def _ple_grad(dx3, gpre, pp):
    gate = jax.nn.sigmoid(gpre)
    dgpre = dx3 * pp * gate * (1.0 - gate)
    return dx3 * gate, dgpre, jnp.sum(dgpre, axis=0, keepdims=True)


def _loss_and_grad(name, y, target, gpre, pp):
    d = y.shape[1]

    def fn(y, t, gpre, pp):
        err = y - t
        sq = jnp.sum(jnp.sum(err * err, axis=0, keepdims=True), axis=1, keepdims=True)
        dy = err * (1.0 / d)
        dpp, dgpre, dbg = _ple_grad(dy, gpre, pp)
        return dy, dpp, dgpre, jnp.broadcast_to(sq, (1, LANES)), dbg

    return _rowwise(name, fn, [y, target, gpre, pp], [], [(d, F32), (d, BF16), (d, BF16)], accs=[(1, LANES), (1, d)])


def _rows(shape):
    return lax.broadcasted_iota(jnp.int32, shape, 0)


def _gelu(y):
    t = jnp.tanh(GELU_C0 * (y + GELU_C1 * y * y * y))
    return 0.5 * y * (1.0 + t), t


def _gelu_grad(y, t):
    return 0.5 * (1.0 + t) + 0.5 * y * (1.0 - t * t) * GELU_C0 * (1.0 + 3.0 * GELU_C1 * y * y)


def _softplus(x):
    return jnp.maximum(x, 0.0) + jnp.log(1.0 + jnp.exp(-jnp.abs(x)))


def _conv_fwd(xs, cw, cb):
    n = xs.shape[0]
    u = cw[3:4] * xs
    for k in (1, 2, 3):
        u = u + cw[3 - k:4 - k] * pltpu.roll(xs, k, 0)
    del n
    return u[SUBLANES:] + cb


def _lru_gates(u, wa, wx, ba, bx, sp, grow):
    ub = u.astype(BF16)
    r = jax.nn.sigmoid(jnp.dot(ub, wa, preferred_element_type=F32) + ba)
    ig = jax.nn.sigmoid(jnp.dot(ub, wx, preferred_element_type=F32) + bx)
    log_a = (-LRU_C) * r * sp
    a = jnp.exp(log_a)
    mult = jnp.sqrt(jnp.tanh(-log_a) * (1.0 + a * a))
    mult = jnp.where(grow == 0, 1.0, mult)
    return ub, r, ig, a, mult


def _scan8_fwd(a, b):
    row = _rows(a.shape)
    for k in (1, 2, 4):
        m = row >= k
        b = jnp.where(m, a * pltpu.roll(b, k, 0) + b, b)
        a = jnp.where(m, a * pltpu.roll(a, k, 0), a)
    return a, b


def _scan8_bwd(c, d):
    row = _rows(c.shape)
    for k in (1, 2, 4):
        m = row < SUBLANES - k
        d = jnp.where(m, c * pltpu.roll(d, SUBLANES - k, 0) + d, d)
        c = jnp.where(m, c * pltpu.roll(c, SUBLANES - k, 0), c)
    return c, d


def _pad_copy(dst, src, front, back):
    s, c = src.shape
    if front:
        dst[pl.ds(0, front), :] = jnp.zeros((front, c), dst.dtype)
    if back:
        dst[pl.ds(front + s, back), :] = jnp.zeros((back, c), dst.dtype)
    dst[pl.ds(front, s), :] = src[...].astype(dst.dtype)


def _lru_fwd(name, proj, par, wa, wx):
    S = proj.shape[0]
    R = proj.shape[1] // 2
    H = R // LANES
    ch = _tile(S, SEQ_CHUNK)
    nch = S // ch
    H8 = SUBLANES

    def body(up_ref, y_ref, par_ref, wa_ref, wx_ref, gh_ref, h_ref, r_ref, ig_ref, a_ref, mult_ref, gy_ref, dgy_ref,
             u_ref, up_pad):
        _pad_copy(up_pad, up_ref, H8, 0)
        par = par_ref[...]
        cw, cb, ba, bx = par[0:4], par[4:5], par[5:6], par[6:7]
        sp = _softplus(-par[7:8])
        wa_m, wx_m = wa_ref[...], wx_ref[...]

        def chunk(ci, carry):
            r0 = pl.multiple_of(ci * ch, ch)
            xs = up_pad[pl.ds(r0, ch + H8), :]
            u = _conv_fwd(xs, cw, cb)
            grow = _rows(u.shape) + r0
            _, r, ig, a, mult = _lru_gates(u, wa_m, wx_m, ba, bx, sp, grow)
            for ref, val in ((r_ref, r), (ig_ref, ig), (a_ref, a), (mult_ref, mult), (u_ref, u)):
                ref[pl.ds(r0, ch), :] = val
            bt = mult * (ig * u)
            hs = []
            for j in range(ch // H8):
                aa, bb = _scan8_fwd(a[j * H8:(j + 1) * H8], bt[j * H8:(j + 1) * H8])
                hj = bb + aa * carry
                carry = hj[H8 - 1:H8]
                hs.append(hj)
            h = jnp.concatenate(hs, axis=0)
            h_ref[pl.ds(r0, ch), :] = h
            y = y_ref[pl.ds(r0, ch), :]
            gy, t = _gelu(y)
            gy_ref[pl.ds(r0, ch), :] = gy
            dgy_ref[pl.ds(r0, ch), :] = _gelu_grad(y, t)
            gh_ref[pl.ds(r0, ch), :] = (h * gy).astype(gh_ref.dtype)
            return carry

        lax.fori_loop(0, nch, chunk, jnp.zeros((1, LANES), F32))

    col = lambda off: pl.BlockSpec((S, LANES), lambda h: (0, h + off))
    return pl.pallas_call(
        body,
        name=name,
        grid=(H,),
        in_specs=[col(0), col(H), pl.BlockSpec((8, LANES), lambda h: (0, h)),
                  pl.BlockSpec((None, LANES, LANES), lambda h: (h, 0, 0)),
                  pl.BlockSpec((None, LANES, LANES), lambda h: (h, 0, 0))],
        out_specs=[col(0)] * 9,
        out_shape=[jax.ShapeDtypeStruct((S, R), BF16)] + [jax.ShapeDtypeStruct((S, R), F32)] * 8,
        scratch_shapes=[pltpu.VMEM((S + H8, LANES), F32)],
        compiler_params=_cparams("parallel"),
    )(proj, proj, par, wa, wx)


def _lru_bwd(name, proj, h, dgh, gates, par, wa, wx):
    S = proj.shape[0]
    R = proj.shape[1] // 2
    H = R // LANES
    ch = _tile(S, SEQ_CHUNK)
    nch = S // ch
    H8 = SUBLANES
    nb = ch // H8

    def body(up_ref, h_ref, dgh_ref, r_ref, ig_ref, a_ref, mult_ref, gy_ref, dgy_ref, u_ref, par_ref, wa_ref, wx_ref,
             dup_ref, dy_ref, dwa_ref, dwx_ref, dpar_ref, up_pad, du_pad, vec_acc):
        _pad_copy(up_pad, up_ref, H8, 0)
        du_pad[pl.ds(S, H8), :] = jnp.zeros((H8, LANES), F32)
        par = par_ref[...]
        cw, lam = par[0:4], par[7:8]
        sp = _softplus(-lam)
        wa_m, wx_m = wa_ref[...], wx_ref[...]
        dwa_ref[...] = jnp.zeros_like(dwa_ref)
        dwx_ref[...] = jnp.zeros_like(dwx_ref)
        vec_acc[...] = jnp.zeros_like(vec_acc)
        nt_dims = (((1,), (1,)), ((), ()))
        tn_dims = (((0,), (0,)), ((), ()))

        def chunk(it, carry):
            lam_next, a_next = carry
            ci = nch - 1 - it
            r0 = pl.multiple_of(ci * ch, ch)
            here = pl.ds(r0, ch)
            u = u_ref[here, :]
            row = _rows(u.shape)
            grow = row + r0
            ub = u.astype(BF16)
            r, ig, a, mult = r_ref[here, :], ig_ref[here, :], a_ref[here, :], mult_ref[here, :]
            hcur = h_ref[here, :]
            before = h_ref[pl.ds(pl.multiple_of(jnp.maximum(r0 - H8, 0), H8), H8), :][H8 - 1:H8]
            hprev = jnp.where(row == 0, jnp.where(ci > 0, before, 0.0), pltpu.roll(hcur, 1, 0))
            dgh = dgh_ref[here, :]
            dy_ref[here, :] = (dgh * hcur * dgy_ref[here, :]).astype(dy_ref.dtype)
            dh = dgh * gy_ref[here, :]
            c = jnp.where(row == ch - 1, a_next, pltpu.roll(a, ch - 1, 0))
            ls = [None] * nb
            for j in range(nb - 1, -1, -1):
                cc, dd = _scan8_bwd(c[j * H8:(j + 1) * H8], dh[j * H8:(j + 1) * H8])
                lj = dd + cc * lam_next
                lam_next = lj[0:1]
                ls[j] = lj
            lmb = jnp.concatenate(ls, axis=0)
            da = lmb * hprev
            gu = ig * u
            dmult = lmb * gu
            dlog_a = da * a + jnp.where(grow == 0, 0.0, dmult * (-(a * a) / mult))
            dr = dlog_a * ((-LRU_C) * sp)
            drp = dr * r * (1.0 - r)
            dip = (lmb * mult * u) * ig * (1.0 - ig)
            drb, dib = drp.astype(BF16), dip.astype(BF16)
            du = (lmb * mult * ig
                  + lax.dot_general(drb, wa_m, nt_dims, preferred_element_type=F32)
                  + lax.dot_general(dib, wx_m, nt_dims, preferred_element_type=F32))
            du_pad[pl.ds(r0, ch), :] = du
            dwa_ref[...] += lax.dot_general(ub, drb, tn_dims, preferred_element_type=F32)
            dwx_ref[...] += lax.dot_general(ub, dib, tn_dims, preferred_element_type=F32)
            ssum = lambda v: jnp.sum(v, axis=0, keepdims=True)
            vec_acc[0:1, :] += ssum(drp)
            vec_acc[1:2, :] += ssum(dip)
            vec_acc[2:3, :] += ssum(dlog_a * ((-LRU_C) * r))
            return lam_next, a[0:1]

        zero = jnp.zeros((1, LANES), F32)
        lax.fori_loop(0, nch, chunk, (zero, zero))

        def conv_chunk(ci, acc):
            r0 = pl.multiple_of(ci * ch, ch)
            ds = du_pad[pl.ds(r0, ch + H8), :]
            xs = up_pad[pl.ds(r0, ch + H8), :]
            n = ch + H8
            du = ds[:ch]
            dup = cw[3:4] * du
            new = [acc[3] + jnp.sum(du * xs[H8:], axis=0, keepdims=True)]
            for k in (1, 2, 3):
                dup = dup + cw[3 - k:4 - k] * pltpu.roll(ds, n - k, 0)[:ch]
                new.append(acc[3 - k] + jnp.sum(du * pltpu.roll(xs, k, 0)[H8:], axis=0, keepdims=True))
            dup_ref[pl.ds(r0, ch), :] = dup.astype(dup_ref.dtype)
            return (new[3], new[2], new[1], new[0], acc[4] + jnp.sum(du, axis=0, keepdims=True))

        acc = lax.fori_loop(0, nch, conv_chunk, (zero,) * 5)
        dlam = vec_acc[2:3, :] * (-jax.nn.sigmoid(-lam))
        dpar_ref[...] = jnp.concatenate(list(acc) + [vec_acc[0:1, :], vec_acc[1:2, :], dlam], axis=0)

    col = lambda off: pl.BlockSpec((S, LANES), lambda h: (0, h + off))
    head = pl.BlockSpec((None, LANES, LANES), lambda h: (h, 0, 0))
    return pl.pallas_call(
        body,
        name=name,
        grid=(H,),
        in_specs=[col(0)] * 10 + [pl.BlockSpec((8, LANES), lambda h: (0, h)), head, head],
        out_specs=[col(0), col(0), head, head, pl.BlockSpec((8, LANES), lambda h: (0, h))],
        out_shape=[jax.ShapeDtypeStruct((S, R), BF16), jax.ShapeDtypeStruct((S, R), BF16),
                   jax.ShapeDtypeStruct((H, LANES, LANES), F32), jax.ShapeDtypeStruct((H, LANES, LANES), F32),
                   jax.ShapeDtypeStruct((8, R), F32)],
        scratch_shapes=[pltpu.VMEM((S + H8, LANES), F32), pltpu.VMEM((S + H8, LANES), F32),
                        pltpu.VMEM((8, LANES), F32)],
        compiler_params=_cparams("parallel", limit=VMEM_LIMIT_SEQ),
    )(proj, h, dgh, *gates, par, wa, wx)


def _window_sum(xs, g, up):
    n = xs.shape[0]
    s = xs
    for lvl, k in enumerate((1, 2, 4, 8)):
        sh = pltpu.roll(s, (n - k) if up else k, 0)
        s = s + jnp.where(g >= lvl, sh, 0.0)
    return s


def _pool_count(grow, g):
    return jnp.minimum(grow + 1, lax.shift_left(jnp.int32(2), g)).astype(F32)


def _pool_fwd(name, u, wgrp, par):
    S, D = u.shape
    G, W = wgrp.shape[0], wgrp.shape[1]
    ch = _tile(S, SEQ_CHUNK)
    nch = S // ch
    PH = POOL_HALO

    def body(u_ref, w_ref, par_ref, zs_ref, pooled_ref, z_ref, u_pad):
        g = pl.program_id(0)
        _pad_copy(u_pad, u_ref, PH, 0)
        par = par_ref[...]
        w = w_ref[...]

        def chunk(ci, _):
            r0 = pl.multiple_of(ci * ch, ch)
            xs = u_pad[pl.ds(r0, ch + PH), :]
            ws = _window_sum(xs, g, False)[PH:]
            uc = xs[PH:]
            cnt = _pool_count(_rows(uc.shape) + r0, g)
            pooled = (ws / cnt - uc).astype(BF16)
            z = jnp.dot(pooled, w, preferred_element_type=F32) + par[0:1]
            pooled_ref[pl.ds(r0, ch), :] = pooled
            z_ref[pl.ds(r0, ch), :] = z
            zs_ref[pl.ds(r0, ch), :] = (z * par[1:2]).astype(zs_ref.dtype)
            return 0

        lax.fori_loop(0, nch, chunk, 0)

    blk = pl.BlockSpec((S, W), lambda g: (0, g))
    return pl.pallas_call(
        body,
        name=name,
        grid=(G,),
        in_specs=[blk, pl.BlockSpec((None, W, W), lambda g: (g, 0, 0)), pl.BlockSpec((2, W), lambda g: (0, g))],
        out_specs=[blk, blk, blk],
        out_shape=[jax.ShapeDtypeStruct((S, D), BF16), jax.ShapeDtypeStruct((S, D), BF16),
                   jax.ShapeDtypeStruct((S, D), F32)],
        scratch_shapes=[pltpu.VMEM((S + PH, W), F32)],
        compiler_params=_cparams("parallel"),
    )(u, wgrp, par)


def _pool_bwd(name, pooled, z, dzs, wgrp, par):
    S, D = z.shape
    G, W = wgrp.shape[0], wgrp.shape[1]
    ch = _tile(S, SEQ_CHUNK)
    nch = S // ch
    PH = POOL_HALO

    def body(pooled_ref, z_ref, dzs_ref, w_ref, par_ref, du_ref, dw_ref, dpar_ref, q_pad, dw_acc):
        g = pl.program_id(0)
        q_pad[pl.ds(S, PH), :] = jnp.zeros((PH, W), F32)
        par = par_ref[...]
        w = w_ref[...]
        dw_acc[...] = jnp.zeros_like(dw_acc)

        def chunk(ci, acc):
            db, dsc = acc
            r0 = pl.multiple_of(ci * ch, ch)
            cnt = _pool_count(_rows((ch, W)) + r0, g)
            pooled = pooled_ref[pl.ds(r0, ch), :]
            z = z_ref[pl.ds(r0, ch), :]
            dzs = dzs_ref[pl.ds(r0, ch), :]
            dz = dzs * par[1:2]
            dzb = dz.astype(BF16)
            dw_acc[...] += lax.dot_general(pooled, dzb, (((0,), (0,)), ((), ())), preferred_element_type=F32)
            dpooled = lax.dot_general(dzb, w, (((1,), (1,)), ((), ())), preferred_element_type=F32)
            q_pad[pl.ds(r0, ch), :] = dpooled / cnt
            return (db + jnp.sum(dz, axis=0, keepdims=True), dsc + jnp.sum(dzs * z, axis=0, keepdims=True))

        zero = jnp.zeros((1, W), F32)
        db, dsc = lax.fori_loop(0, nch, chunk, (zero, zero))
        dpar_ref[...] = jnp.concatenate([db, dsc], axis=0)
        dw_ref[...] = dw_acc[...].astype(dw_ref.dtype)

        def back(ci, _):
            r0 = pl.multiple_of(ci * ch, ch)
            qs = q_pad[pl.ds(r0, ch + PH), :]
            ws = _window_sum(qs, g, True)[:ch]
            qc = qs[:ch]
            cnt = _pool_count(_rows(qc.shape) + r0, g)
            du_ref[pl.ds(r0, ch), :] = (ws - qc * cnt).astype(du_ref.dtype)
            return 0

        lax.fori_loop(0, nch, back, 0)

    blk = pl.BlockSpec((S, W), lambda g: (0, g))
    wspec = pl.BlockSpec((None, W, W), lambda g: (g, 0, 0))
    pspec = pl.BlockSpec((2, W), lambda g: (0, g))
    return pl.pallas_call(
        body,
        name=name,
        grid=(G,),
        in_specs=[blk, blk, blk, wspec, pspec],
        out_specs=[blk, wspec, pspec],
        out_shape=[jax.ShapeDtypeStruct((S, D), BF16), jax.ShapeDtypeStruct((G, W, W), BF16),
                   jax.ShapeDtypeStruct((2, D), F32)],
        scratch_shapes=[pltpu.VMEM((S + PH, W), F32), pltpu.VMEM((W, W), F32)],
        compiler_params=_cparams("parallel"),
    )(pooled, z, dzs, wgrp, par)


def _my_place():
    x, y, c = lax.axis_index("x"), lax.axis_index("y"), lax.axis_index("c")
    return x, y, c, 4 * x + 2 * y + c


def _peers(x, y, c):
    out = []
    for d in range(1, N_DEV):
        px = 1 - x if d & 4 else x
        py = 1 - y if d & 2 else y
        pc = 1 - c if d & 1 else c
        out.append(((px, py, pc), 4 * px + 2 * py + pc))
    return out


def _window(ref, axis, start, size):
    idx = [slice(None)] * len(ref.shape)
    idx[axis] = pl.ds(start, size)
    return ref.at[tuple(idx)]


def _to_bf16(name, arrs, after=()):
    outs = []
    for i, a in enumerate(arrs):
        a2 = a.reshape(-1, a.shape[-1])
        tr = _tile(a2.shape[0], 512)
        o = pl.pallas_call(
            lambda a_ref, *rest: rest[-1].__setitem__(Ellipsis, a_ref[...].astype(BF16)),
            name=f"{name}_{i}",
            grid=(a2.shape[0] // tr,),
            in_specs=[pl.BlockSpec((tr, a2.shape[1]), lambda r: (r, 0))] + [pl.BlockSpec(memory_space=pl.ANY)] * len(after),
            out_specs=pl.BlockSpec((tr, a2.shape[1]), lambda r: (r, 0)),
            out_shape=jax.ShapeDtypeStruct(a2.shape, BF16),
            compiler_params=_cparams("parallel"),
        )(a2, *after)
        outs.append(o.reshape(a.shape))
    return outs


def _all_gather(name, shards, axes):
    n = len(shards)
    sizes = [s.shape[ax] for s, ax in zip(shards, axes)]

    def body(*refs):
        ins, outs = refs[:n], refs[n:2 * n]
        send, recv, loc = refs[2 * n:]
        x, y, c, me = _my_place()
        peers = _peers(x, y, c)
        local = []
        for i in range(n):
            dst = _window(outs[i], axes[i], me * sizes[i], sizes[i])
            cp = pltpu.make_async_copy(ins[i], dst, loc.at[i])
            cp.start()
            local.append(cp)
            for peer, _ in peers:
                pltpu.make_async_remote_copy(src_ref=ins[i], dst_ref=dst, send_sem=send.at[i], recv_sem=recv.at[i],
                                             device_id=peer, device_id_type=MESH).start()
        for i in range(n):
            local[i].wait()
            seven = _window(outs[i], axes[i], 0, (N_DEV - 1) * sizes[i])
            pltpu.make_async_remote_copy(src_ref=seven, dst_ref=seven, send_sem=send.at[i], recv_sem=recv.at[i],
                                         device_id=(x, y, c), device_id_type=MESH).wait()

    def full_shape(s, ax):
        shp = list(s.shape)
        shp[ax] *= N_DEV
        return jax.ShapeDtypeStruct(tuple(shp), s.dtype)

    any_spec = pl.BlockSpec(memory_space=pl.ANY)
    return pl.pallas_call(
        body,
        name=name,
        in_specs=[any_spec] * n,
        out_specs=[any_spec] * n,
        out_shape=[full_shape(s, ax) for s, ax in zip(shards, axes)],
        scratch_shapes=[pltpu.SemaphoreType.DMA((n,)), pltpu.SemaphoreType.DMA((n,)), pltpu.SemaphoreType.DMA((n,))],
        compiler_params=pltpu.CompilerParams(has_side_effects=True),
    )(*shards)


HBM_SPEC = pl.BlockSpec(memory_space=pltpu.HBM)
SEM_SPEC = pl.BlockSpec(memory_space=pltpu.SEMAPHORE)
SPLIT_EFFECT = pltpu.SideEffectType.DATAFLOW_SIDE_EFFECTING


def _push_all(kind, src, dst, axis, size, send_sem, recv_sem, place):
    x, y, c, me = place
    for peer, pidx in _peers(x, y, c):
        if kind == "gather":
            s = d = _window(dst, axis, me * size, size)
        else:
            s, d = _window(src, axis, pidx * size, size), dst.at[me]
        pltpu.make_async_remote_copy(src_ref=s, dst_ref=d, send_sem=send_sem, recv_sem=recv_sem, device_id=peer,
                                     device_id_type=MESH).start()


def _drain_all(kind, dst, axis, size, send_sem, recv_sem, place):
    x, y, c, _ = place
    seven = _window(dst, axis, 0, (N_DEV - 1) * size) if kind == "gather" else dst.at[pl.ds(0, N_DEV - 1)]
    pltpu.make_async_remote_copy(src_ref=seven, dst_ref=seven, send_sem=send_sem, recv_sem=recv_sem,
                                 device_id=(x, y, c), device_id_type=MESH).wait()


def _own_block_placed(src, axis, size, me):
    own = lax.dynamic_slice_in_dim(src, me * size, size, axis)
    return lax.dynamic_update_slice_in_dim(lax.empty((N_DEV,) + own.shape, src.dtype), own[None], me, 0)


def _split_start(name, kind, srcs, lands, axes, sizes, after=()):
    n, ns, na = len(lands), len(srcs), len(after)

    def body(*refs):
        src_refs, land_refs = refs[:ns], refs[ns:ns + n]
        send, recv = refs[ns + n + na], refs[ns + n + na + 1]
        token = refs[-1]
        place = _my_place()
        for k in range(n):
            _push_all(kind, src_refs[k] if ns else None, land_refs[k], axes[k], sizes[k], send.at[k], recv.at[k], place)
        token[...] = jnp.zeros_like(token)

    hbm = lambda a: pltpu.HBM(a.shape, a.dtype)
    res = pl.pallas_call(
        body,
        name=name,
        out_shape=(pltpu.SemaphoreType.DMA((n,)), pltpu.SemaphoreType.DMA((n,)), *[hbm(a) for a in srcs],
                   *[hbm(a) for a in lands], jax.ShapeDtypeStruct((SUBLANES, LANES), F32)),
        in_specs=[HBM_SPEC] * (ns + n) + [pl.BlockSpec(memory_space=pl.ANY)] * na,
        out_specs=(SEM_SPEC, SEM_SPEC, *[HBM_SPEC] * (ns + n), pl.BlockSpec(memory_space=pltpu.VMEM)),
        input_output_aliases={k: 2 + k for k in range(ns + n)},
        compiler_params=pltpu.CompilerParams(has_side_effects=SPLIT_EFFECT),
    )(*[pltpu.with_memory_space_constraint(a, pltpu.HBM) for a in (*srcs, *lands)], *after)
    return res[0], res[1], list(res[2:2 + ns]), list(res[2 + ns:2 + ns + n]), res[-1]


def _split_wait(name, kind, handle, axes, sizes, after):
    send, recv, srcs, lands, _ = handle
    n, ns = len(lands), len(srcs)
    after = list(after) if isinstance(after, (list, tuple)) else [after]

    def body(*refs):
        land_refs = refs[ns:ns + n]
        send_ref, recv_ref = refs[ns + n], refs[ns + n + 1]
        place = _my_place()
        for k in range(n):
            _drain_all(kind, land_refs[k], axes[k], sizes[k], send_ref.at[k], recv_ref.at[k], place)

    hbm = lambda a: pltpu.HBM(a.shape, a.dtype)
    res = pl.pallas_call(
        body,
        name=name,
        out_shape=tuple(hbm(a) for a in (*srcs, *lands)),
        in_specs=[HBM_SPEC] * (ns + n) + [SEM_SPEC, SEM_SPEC] + [pl.BlockSpec(memory_space=pl.ANY)] * len(after),
        out_specs=tuple([HBM_SPEC] * (ns + n)),
        input_output_aliases={k: k for k in range(ns + n)},
        compiler_params=pltpu.CompilerParams(has_side_effects=SPLIT_EFFECT),
    )(*srcs, *lands, send, recv, *after)
    return list(res[:ns]), list(res[ns:])


def _cast_into_window(name, a, l, axis, me1, after=()):
    shp = a.shape[1:]
    cast = lambda me_ref, a_ref, *rest: rest[-1].__setitem__(Ellipsis, a_ref[...].astype(BF16))
    if len(shp) == 3:
        assert axis == 1
        G, r, c = shp
        full = (G, r * N_DEV, c)
        grid = (G,)
        in_spec = pl.BlockSpec((None, None, r, c), lambda g, me: (l, g, 0, 0))
        out_spec = pl.BlockSpec((None, r, c), lambda g, me: (g, me[0], 0))
    else:
        r, c = shp
        tr = _tile(r, 512)
        nb = r // tr
        grid = (nb,)
        in_spec = pl.BlockSpec((None, tr, c), lambda i, me: (l, i, 0))
        if axis == 0:
            full = (r * N_DEV, c)
            out_spec = pl.BlockSpec((tr, c), lambda i, me: (me[0] * nb + i, 0))
        else:
            full = (r, c * N_DEV)
            out_spec = pl.BlockSpec((tr, c), lambda i, me: (i, me[0]))
    return pl.pallas_call(
        cast,
        name=name,
        grid_spec=pltpu.PrefetchScalarGridSpec(
            num_scalar_prefetch=1, grid=grid,
            in_specs=[in_spec] + [pl.BlockSpec(memory_space=pl.ANY)] * len(after), out_specs=out_spec),
        out_shape=jax.ShapeDtypeStruct(full, BF16),
        compiler_params=_cparams("arbitrary"),
    )(me1, a, *after)


def _adamw_math(w, g, m, v):
    m = ADAM_B1 * m + (1.0 - ADAM_B1) * g
    v = ADAM_B2 * v + (1.0 - ADAM_B2) * jnp.square(g)
    m_hat = m / (1.0 - ADAM_B1 ** ADAM_STEP)
    v_hat = v / (1.0 - ADAM_B2 ** ADAM_STEP)
    delta = -ADAM_LR * (m_hat / (jnp.sqrt(v_hat) + ADAM_EPS) + ADAM_WD * w)
    return delta, m, v


def _sum_slots(buf_ref):
    g = buf_ref[0].astype(F32)
    for s in range(1, N_DEV):
        g = g + buf_ref[s].astype(F32)
    return g


def _adamw_layer(name, buf, w, m, v, l, prev, after=(), own=None, own_axis=0, me1=None):
    shape = w.shape
    L, C = shape[0], shape[-1]
    Rr = math.prod(shape[1:-1])
    buf3 = buf.reshape(N_DEV, Rr, C)
    w3, m3, v3 = (t.reshape(L, Rr, C) for t in (w, m, v))
    tr = _tile(Rr, 2 * LANES) if Rr % LANES == 0 else Rr
    nb = Rr // tr
    n_pass = (0 if prev is None else 4) + len(after)
    n_lead = 1 if own is None else 3

    def body(*refs):
        buf_ref = refs[0] if own is None else refs[1]
        w_ref, m_ref, v_ref = refs[n_lead], refs[n_lead + 1], refs[n_lead + 2]
        g_out, d_out, m_out, v_out = refs[n_lead + 3 + n_pass:]
        if own is None:
            g = _sum_slots(buf_ref)
        else:
            me_ref, mine = refs[0], refs[2][...].astype(F32)
            g = None
            for s in range(N_DEV):
                term = jnp.where(me_ref[0] == s, mine, buf_ref[s].astype(F32))
                g = term if g is None else g + term
        d, mm, vv = _adamw_math(w_ref[...], g, m_ref[...], v_ref[...])
        g_out[...] = g
        d_out[...] = d
        m_out[...] = mm
        v_out[...] = vv

    out_shape = [jax.ShapeDtypeStruct((L, Rr, C), F32)] * 4
    passed = [pl.BlockSpec(memory_space=pl.ANY)] * n_pass
    aliases = {} if prev is None else {n_lead + 3 + k: k for k in range(4)}
    if own is None:
        spec = pl.BlockSpec((None, tr, C), lambda r: (l, r, 0))
        outs = pl.pallas_call(
            body, name=name, grid=(nb,),
            in_specs=[pl.BlockSpec((N_DEV, tr, C), lambda r: (0, r, 0)), spec, spec, spec] + passed,
            out_specs=[spec] * 4, out_shape=out_shape, input_output_aliases=aliases,
            compiler_params=_cparams("parallel"),
        )(buf3, w3, m3, v3, *(prev or ()), *after)
    else:
        spec = pl.BlockSpec((None, tr, C), lambda r, me: (l, r, 0))
        own_idx = (lambda r, me: (me[0] * nb + r, 0)) if own_axis == 0 else (lambda r, me: (r, me[0]))
        outs = pl.pallas_call(
            body, name=name,
            grid_spec=pltpu.PrefetchScalarGridSpec(
                num_scalar_prefetch=1, grid=(nb,),
                in_specs=[pl.BlockSpec((N_DEV, tr, C), lambda r, me: (0, r, 0)), pl.BlockSpec((tr, C), own_idx),
                          spec, spec, spec] + passed,
                out_specs=[spec] * 4),
            out_shape=out_shape, input_output_aliases=aliases,
            compiler_params=_cparams("arbitrary"),
        )(me1, buf3, own, w3, m3, v3, *(prev or ()), *after)
    return list(outs)


def _sum8(name, buf):
    R = buf.shape[1]

    def body(buf_ref, o_ref):
        o_ref[...] = _sum_slots(buf_ref)

    return pl.pallas_call(
        body,
        name=name,
        in_specs=[pl.BlockSpec(buf.shape, lambda: (0, 0, 0))],
        out_specs=pl.BlockSpec((R, LANES), lambda: (0, 0)),
        out_shape=jax.ShapeDtypeStruct((R, LANES), F32),
        compiler_params=_cparams(),
    )(buf)


def _adamw_small(name, w, g, m, v):
    shape = w.shape
    w2, g2, m2, v2 = (t.reshape(-1, shape[-1]) for t in (w, g, m, v))
    R, C = w2.shape
    tr = _row_tile(R, 512)

    def body(w_ref, g_ref, m_ref, v_ref, d_out, m_out, v_out):
        d, mm, vv = _adamw_math(w_ref[...], g_ref[...], m_ref[...], v_ref[...])
        d_out[...] = d
        m_out[...] = mm
        v_out[...] = vv

    spec = pl.BlockSpec((tr, C), lambda r: (r, 0))
    outs = pl.pallas_call(
        body,
        name=name,
        grid=(R // tr,),
        in_specs=[spec] * 4,
        out_specs=[spec] * 3,
        out_shape=[jax.ShapeDtypeStruct((R, C), F32)] * 3,
        compiler_params=_cparams("parallel"),
    )(w2, g2, m2, v2)
    return [o.reshape(shape) for o in outs]


def _pack(arrs, pad_rows_to=SUBLANES):
    parts = []
    for a in arrs:
        flat = a.reshape(-1)
        per = LANES * pad_rows_to
        padded = -(-flat.shape[0] // per) * per
        if padded != flat.shape[0]:
            flat = jnp.pad(flat, (0, padded - flat.shape[0]))
        parts.append(flat.reshape(-1, LANES))
    return jnp.concatenate(parts, axis=0)


def _unpack(packed, shapes, pad_rows_to=SUBLANES):
    out = []
    r = 0
    for shp in shapes:
        nel = math.prod(shp)
        per = LANES * pad_rows_to
        rows = -(-nel // per) * pad_rows_to
        out.append(packed[r:r + rows].reshape(-1)[:nel].reshape(shp))
        r += rows
    return out


BIG = ("lru_w_in", "lru_w_out", "pool_w_in", "pool_w_grp", "pool_w_out", "mlp_w1", "mlp_w2", "ple_w", "ple_gate_w")
BIG_AXIS = {"lru_w_in": 2, "lru_w_out": 1, "pool_w_in": 1, "pool_w_grp": 2, "pool_w_out": 1, "mlp_w1": 2,
            "mlp_w2": 1, "ple_w": 2, "ple_gate_w": 1}
SMALL_SHARDED = ("lru_conv_w", "pool_b_grp", "pool_scale")
REPLICATED = ("lru_conv_b", "lru_wa", "lru_ba", "lru_wx", "lru_bx", "lru_lambda", "ln_mix_g", "ln_mix_b",
              "ln_mlp_g", "ln_mlp_b", "ple_gate_b")
WEIGHTS = ("lru_w_in", "lru_conv_w", "lru_conv_b", "lru_wa", "lru_ba", "lru_wx", "lru_bx", "lru_lambda", "lru_w_out",
           "pool_w_in", "pool_w_grp", "pool_b_grp", "pool_scale", "pool_w_out", "ln_mix_g", "ln_mix_b", "mlp_w1",
           "mlp_w2", "ln_mlp_g", "ln_mlp_b", "ple_w", "ple_gate_w", "ple_gate_b")
INPUTS = ("x", "p") + WEIGHTS + ("loss_target",) + tuple("m_" + n for n in WEIGHTS) + tuple("v_" + n for n in WEIGHTS)


def _gather_last_axis(packed_full, shard_shape):
    nel = math.prod(shard_shape)
    blocks = packed_full.reshape(N_DEV, -1)[:, :nel].reshape((N_DEV,) + tuple(shard_shape))
    return jnp.concatenate([blocks[d] for d in range(N_DEV)], axis=-1)


def kernel(x, p, lru_w_in, lru_conv_w, lru_conv_b, lru_wa, lru_ba, lru_wx, lru_bx, lru_lambda, lru_w_out, pool_w_in, pool_w_grp, pool_b_grp, pool_scale, pool_w_out, ln_mix_g, ln_mix_b, mlp_w1, mlp_w2, ln_mlp_g, ln_mlp_b, ple_w, ple_gate_w, ple_gate_b, loss_target, m_lru_w_in, m_lru_conv_w, m_lru_conv_b, m_lru_wa, m_lru_ba, m_lru_wx, m_lru_bx, m_lru_lambda, m_lru_w_out, m_pool_w_in, m_pool_w_grp, m_pool_b_grp, m_pool_scale, m_pool_w_out, m_ln_mix_g, m_ln_mix_b, m_mlp_w1, m_mlp_w2, m_ln_mlp_g, m_ln_mlp_b, m_ple_w, m_ple_gate_w, m_ple_gate_b, v_lru_w_in, v_lru_conv_w, v_lru_conv_b, v_lru_wa, v_lru_ba, v_lru_wx, v_lru_bx, v_lru_lambda, v_lru_w_out, v_pool_w_in, v_pool_w_grp, v_pool_b_grp, v_pool_scale, v_pool_w_out, v_ln_mix_g, v_ln_mix_b, v_mlp_w1, v_mlp_w2, v_ln_mlp_g, v_ln_mlp_b, v_ple_w, v_ple_gate_w, v_ple_gate_b):
    A = dict(zip(INPUTS, (x, p, lru_w_in, lru_conv_w, lru_conv_b, lru_wa, lru_ba, lru_wx, lru_bx, lru_lambda, lru_w_out, pool_w_in, pool_w_grp, pool_b_grp, pool_scale, pool_w_out, ln_mix_g, ln_mix_b, mlp_w1, mlp_w2, ln_mlp_g, ln_mlp_b, ple_w, ple_gate_w, ple_gate_b, loss_target, m_lru_w_in, m_lru_conv_w, m_lru_conv_b, m_lru_wa, m_lru_ba, m_lru_wx, m_lru_bx, m_lru_lambda, m_lru_w_out, m_pool_w_in, m_pool_w_grp, m_pool_b_grp, m_pool_scale, m_pool_w_out, m_ln_mix_g, m_ln_mix_b, m_mlp_w1, m_mlp_w2, m_ln_mlp_g, m_ln_mlp_b, m_ple_w, m_ple_gate_w, m_ple_gate_b, v_lru_w_in, v_lru_conv_w, v_lru_conv_b, v_lru_wa, v_lru_ba, v_lru_wx, v_lru_bx, v_lru_lambda, v_lru_w_out, v_pool_w_in, v_pool_w_grp, v_pool_b_grp, v_pool_scale, v_pool_w_out, v_ln_mix_g, v_ln_mix_b, v_mlp_w1, v_mlp_w2, v_ln_mlp_g, v_ln_mlp_b, v_ple_w, v_ple_gate_w, v_ple_gate_b)))
    depth = ln_mix_g.shape[0]
    alpha = (2 * depth) ** 0.25
    S, D = x.shape[1], x.shape[2]
    xs = x.reshape(S, D)
    tgt = loss_target.reshape(S, D)
    p3 = p.reshape(depth, S, p.shape[-1])
    me = 4 * lax.axis_index("x") + 2 * lax.axis_index("y") + lax.axis_index("c")

    def layer_weights(i):
        s = i // 2
        mixer = ("lru_w_in", "lru_w_out") if i % 2 == 0 else ("pool_w_in", "pool_w_grp", "pool_w_out")
        return [(n, s) for n in mixer] + [(n, i) for n in ("mlp_w1", "mlp_w2", "ple_w", "ple_gate_w")]

    def axis_of(key):
        return 0 if key[0] == "small" else BIG_AXIS[key[0]] - 1

    def start_gather(tag, keys, after):
        axes = [axis_of(k) for k in keys]
        lands = [land[k] for k in keys]
        sizes = [a.shape[ax] // N_DEV for a, ax in zip(lands, axes)]
        return keys, _split_start(f"gather_{tag}_start", "gather", [], lands, axes, sizes, after=after), axes, sizes

    def finish_gather(tag, pending, after):
        keys, handle, axes, sizes = pending
        for (n, l), full in zip(keys, _split_wait(f"gather_{tag}_wait", "gather", handle, axes, sizes, after)[1]):
            W[n][l] = full

    def start_exchange(tag, keys, arrs, after):
        axes = [axis_of(k) for k in keys]
        sizes = [a.shape[ax] // N_DEV for a, ax in zip(arrs, axes)]
        lands = []
        for k, a, ax, sz in zip(keys, arrs, axes, sizes):
            if reads_own_block(k):
                shp = list(a.shape)
                shp[ax] = sz
                lands.append(lax.empty((N_DEV,) + tuple(shp), a.dtype))
            else:
                lands.append(_own_block_placed(a, ax, sz, me))
        return keys, _split_start(f"exchange_{tag}_start", "scatter", arrs, lands, axes, sizes, after=after), axes, sizes

    def finish_exchange(tag, pending, after):
        keys, handle, axes, sizes = pending
        sources, landed = _split_wait(f"exchange_{tag}_wait", "scatter", handle, axes, sizes, after)
        partial.update(zip(keys, landed))
        own_grad.update(zip(keys, sources))

    def reads_own_block(key):
        return key[0] not in ("small", "pool_w_grp")

    me1 = jnp.reshape(me, (1,)).astype(jnp.int32)
    land = {}
    W = {n: [None] * A[n].shape[0] for n in BIG}
    small_shard_shapes = [A[n].shape for n in SMALL_SHARDED]
    gathered = _all_gather("gather_small_params", [_pack([A[n] for n in SMALL_SHARDED])], [0])
    def gather_groups(i):
        keys = layer_weights(i)
        mixer, (w1, w2, pw, pg) = keys[:-4], keys[-4:]
        if i % 2 == 0:
            return [("in", mixer[:1]), ("out", mixer[1:]), ("up", [w1]), ("rest", [w2, pw, pg])]
        return [("in", mixer + [w1]), ("rest", [w2, pw, pg])]

    gather_pending = {}

    def send_layer(i, behind):
        for tag, keys in gather_groups(i):
            for k in keys:
                land[k] = _cast_into_window(f"cast_{k[0]}_{k[1]}", A[k[0]], k[1], axis_of(k), me1, after=behind)
            gather_pending[(i, tag)] = start_gather(f"l{i}_{tag}", keys, behind)
            behind = (gather_pending[(i, tag)][1][4],)
        return behind

    layer0_started = send_layer(0, (gathered[0],))
    small_full = gathered[0].reshape(N_DEV, -1, LANES)
    r = 0
    for n, shp in zip(SMALL_SHARDED, small_shard_shapes):
        rows = -(-math.prod(shp) // (LANES * SUBLANES)) * SUBLANES
        W[n] = _gather_last_axis(small_full[:, r:r + rows], shp)
        r += rows
    wa_b, wx_b = _to_bf16("cast_gates", [lru_wa, lru_wx], after=layer0_started)
    n_lru = lru_w_in.shape[0]
    lru_par = [jnp.concatenate([W["lru_conv_w"][s], lru_conv_b[s][None], lru_ba[s][None], lru_bx[s][None],
                                lru_lambda[s][None]], axis=0) for s in range(n_lru)]
    pool_par = [jnp.stack([W["pool_b_grp"][s], W["pool_scale"][s]], axis=0) for s in range(pool_w_in.shape[0])]

    saved = []
    h_in = xs
    (h_in_b,) = _to_bf16("cast_inputs", [xs], after=layer0_started)
    layer1_started = send_layer(1, (layer0_started[0], h_in_b))
    finish_gather("l0_in", gather_pending[(0, "in")], [h_in_b, wa_b, layer1_started[0]])
    for i in range(depth):
        s = i // 2
        sv = {"x0b": h_in_b}
        if i > 0:
            finish_gather(f"l{i}_in", gather_pending[(i, "in")], h_in)
        ln_out = dict(out_dtypes=[F32, F32, BF16], tm=MM_TM_ROWS, tn=D,
                      epi=lambda acc, xp, g, b: _ln_apply(alpha * xp + acc, g, b))
        if i % 2 == 0:
            sv["proj"] = _mm(f"l{i}_lru_in", h_in_b, W["lru_w_in"], "nn", [F32], b_lead=s)
            if i == 0:
                behind = (layer1_started[0], sv["proj"])
                for later in range(2, depth):
                    behind = send_layer(later, behind)
                all_started = behind[0]
            sv["gh"], sv["h"], *sv["gates"] = _lru_fwd(f"l{i}_lru_core", sv["proj"], lru_par[s], wa_b[s], wx_b[s])
            finish_gather(f"l{i}_out", gather_pending[(i, "out")], [sv["gh"], all_started])
            mix_in, mix_w = sv["gh"], W["lru_w_out"]
        else:
            u_pool = _mm(f"l{i}_pool_in", h_in_b, W["pool_w_in"], "nn", [F32], b_lead=s)
            sv["zs"], sv["pooled"], sv["z"] = _pool_fwd(f"l{i}_pool_core", u_pool, W["pool_w_grp"][s], pool_par[s])
            mix_in, mix_w = sv["zs"], W["pool_w_out"]
        sv["z1"], sv["x1"], sv["x1b"] = _mm(f"l{i}_mix_out_ln", mix_in, mix_w, "nn", b_lead=s,
                                            extras=[h_in, ln_mix_g[i][None], ln_mix_b[i][None]], **ln_out)
        if i % 2 == 0:
            finish_gather(f"l{i}_up", gather_pending[(i, "up")], sv["x1b"])
        sv["hpre"], sv["hact"] = _mm(f"l{i}_mlp_up", sv["x1b"], W["mlp_w1"], "nn", [BF16, BF16], b_lead=i,
                                     epi=lambda acc: (acc, jnp.square(jnp.maximum(acc, 0.0))))
        finish_gather(f"l{i}_rest", gather_pending[(i, "rest")], sv["hact"])
        sv["z2"], sv["x2"], sv["x2b"] = _mm(f"l{i}_mlp_down_ln", sv["hact"], W["mlp_w2"], "nn", b_lead=i,
                                            extras=[sv["x1"], ln_mlp_g[i][None], ln_mlp_b[i][None]], **ln_out)
        sv["pp"] = _mm(f"l{i}_ple_up", p3, W["ple_w"], "nn", [F32], a_lead=i, b_lead=i)

        def ple_epi(acc, bg, x2t, ppt):
            gpre = acc + bg
            x3 = x2t + ppt * jax.nn.sigmoid(gpre)
            return x3, x3, gpre

        h_in, h_in_b, sv["gpre"] = _mm(f"l{i}_ple_gate", sv["x2b"], W["ple_gate_w"], "nn", [F32, BF16, F32], b_lead=i,
                                       epi=ple_epi, extras=[ple_gate_b[i][None], sv["x2"], sv["pp"]], tn=MM_TN)
        saved.append(sv)

    dx, dpp, dgpre, sq, dbg = _loss_and_grad("loss", h_in, tgt, saved[-1]["gpre"], saved[-1]["pp"])
    loss = lax.psum(0.5 * sq[0, 0] / D, ("x", "y", "c"))

    dW = {n: [None] * A[n].shape[0] for n in BIG}
    dsmall = {n: [None] * A[n].shape[0] for n in REPLICATED + SMALL_SHARDED}
    small_names = REPLICATED + SMALL_SHARDED
    partial = {}
    own_grad = {}
    exchange_pending = {}
    exchange_token = ()
    for i in reversed(range(depth)):
        s = i // 2
        sv = saved[i]
        dsmall["ple_gate_b"][i] = dbg[0]
        dW["ple_w"][i] = _mm(f"l{i}_d_ple_w", p3, dpp, "tn", [BF16], a_lead=i, after=exchange_token)
        dW["ple_gate_w"][i] = _mm(f"l{i}_d_ple_gate_w", sv["x2b"], dgpre, "tn", [BF16])
        ln_back = dict(out_dtypes=[F32, BF16], tm=MM_TM_ROWS, tn=D, n_sums=2)
        dz2, dz2b, dg, db = _mm(f"l{i}_d_x2_ln", dgpre, W["ple_gate_w"], "nt", b_lead=i,
                                extras=[dx, sv["z2"], ln_mlp_g[i][None]],
                                epi=lambda acc, d, z, g: _ln_grad(acc + d, z, g), after=exchange_token, **ln_back)
        dsmall["ln_mlp_g"][i], dsmall["ln_mlp_b"][i] = dg[0], db[0]
        dhpre = _mm(f"l{i}_d_hpre", dz2b, W["mlp_w2"], "nt", [BF16], b_lead=i, extras=[sv["hpre"]],
                    epi=lambda acc, hp: (acc * (2.0 * jnp.maximum(hp.astype(F32), 0.0)),))
        dW["mlp_w2"][i] = _mm(f"l{i}_d_mlp_w2", sv["hact"], dz2b, "tn", [BF16])
        dW["mlp_w1"][i] = _mm(f"l{i}_d_mlp_w1", sv["x1b"], dhpre, "tn", [BF16])
        mlp_after = ()
        if i == 0:
            early = [(n, 0) for n in ("ple_w", "ple_gate_w", "mlp_w2", "mlp_w1")]
            exchange_early0 = start_exchange("early0", early, [dW[n][l] for n, l in early], ())
            mlp_after = (exchange_early0[1][4],)
        dz1, dz1b, dg, db = _mm(f"l{i}_d_x1_ln", dhpre, W["mlp_w1"], "nt", b_lead=i,
                                extras=[dz2, sv["z1"], ln_mix_g[i][None]],
                                epi=lambda acc, d, z, g: _ln_grad(acc + alpha * d, z, g), after=mlp_after, **ln_back)
        dsmall["ln_mix_g"][i], dsmall["ln_mix_b"][i] = dg[0], db[0]
        if i % 2 == 0:
            dW["lru_w_out"][s] = _mm(f"l{i}_d_lru_w_out", sv["gh"], dz1b, "tn", [BF16])
            dgh = _mm(f"l{i}_d_gh", dz1b, W["lru_w_out"], "nt", [F32], b_lead=s)
            dup, dy, dwa, dwx, dpar = _lru_bwd(f"l{i}_lru_core_bwd", sv["proj"], sv["h"], dgh, sv["gates"],
                                               lru_par[s], wa_b[s], wx_b[s])
            dsmall["lru_wa"][s], dsmall["lru_wx"][s] = dwa, dwx
            dsmall["lru_conv_w"][s] = dpar[0:4]
            for k, n in enumerate(("lru_conv_b", "lru_ba", "lru_bx", "lru_lambda")):
                dsmall[n][s] = dpar[4 + k]
            dmix_in = jnp.concatenate([dup, dy], axis=1)
            win = "lru_w_in"
        else:
            dW["pool_w_out"][s] = _mm(f"l{i}_d_pool_w_out", sv["zs"], dz1b, "tn", [BF16])
            dzs = _mm(f"l{i}_d_zs", dz1b, W["pool_w_out"], "nt", [F32], b_lead=s)
            dmix_in, dW["pool_w_grp"][s], dpar = _pool_bwd(f"l{i}_pool_core_bwd", sv["pooled"], sv["z"], dzs,
                                                          W["pool_w_grp"][s], pool_par[s])
            dsmall["pool_b_grp"][s], dsmall["pool_scale"][s] = dpar[0], dpar[1]
            win = "pool_w_in"
        dW[win][s] = _mm(f"l{i}_d_{win}", sv["x0b"], dmix_in, "tn", [BF16])
        x0_after = ()
        if i > 0:
            keys = layer_weights(i)
            exchange_pending[i] = start_exchange(f"l{i}", keys, [dW[n][l] for n, l in keys], ())
            exchange_token = (exchange_pending[i][1][4],)
        else:
            small_grads = [jnp.stack(dsmall[n]) for n in small_names]
            small_shapes = [g.shape for g in small_grads]
            packed_g = _pack(small_grads)
            assert packed_g.shape[0] % (N_DEV * SUBLANES) == 0, packed_g.shape
            late = [("lru_w_out", 0), ("lru_w_in", 0), ("small", 0)]
            exchange_late0 = start_exchange("late0", late, [dW["lru_w_out"][0], dW["lru_w_in"][0], packed_g], ())
            x0_after = (exchange_late0[1][4],)
        if i > 0:
            def x0_epi(acc, d, gpre, pp):
                dxv = acc + alpha * d
                return (dxv,) + _ple_grad(dxv, gpre, pp)

            dx, dpp, dgpre, dbg = _mm(f"l{i}_d_x0_ple", dmix_in, W[win], "nt", [F32, BF16, BF16], b_lead=s,
                                      extras=[dz1, saved[i - 1]["gpre"], saved[i - 1]["pp"]], epi=x0_epi,
                                      tm=MM_TM_ROWS, tn=D, n_sums=1)
        else:
            dx = _mm(f"l{i}_d_x0", dmix_in, W[win], "nt", [F32], b_lead=s, extras=[dz1],
                     epi=lambda acc, d: (acc + alpha * d,), after=x0_after)
    grad_x = dx.reshape(x.shape)

    for i in range(1, depth):
        finish_exchange(f"l{i}", exchange_pending[i], x0_after[0])
    stacked = {n: None for n in BIG}
    layer0 = layer_weights(0)

    def adamw(n, l, after=()):
        own = dict(own=own_grad[(n, l)], own_axis=axis_of((n, l)), me1=me1) if reads_own_block((n, l)) else {}
        stacked[n] = _adamw_layer(f"adamw_{n}_{l}", partial[(n, l)], A[n], A["m_" + n], A["v_" + n], l, stacked[n],
                                  after=after, **own)

    for n in BIG:
        for l in reversed(range(A[n].shape[0])):
            if (n, l) not in layer0:
                adamw(n, l)
    behind = [dx] + [stacked[n][0] for n in BIG if stacked[n] is not None]
    finish_exchange("early0", exchange_early0, behind)
    finish_exchange("late0", exchange_late0, behind)
    red = _sum8("sum_small", partial[("small", 0)])
    rows = red.shape[0]
    red_land = lax.dynamic_update_slice_in_dim(lax.empty((N_DEV * rows, LANES), F32), red, me * rows, 0)
    small_handle = _split_start("gather_small_start", "gather", [], [red_land], [0], [rows])
    for n, l in layer0:
        adamw(n, l, after=(small_handle[4],))
    outs = {n: [o.reshape(A[n].shape) for o in stacked[n]] for n in BIG}
    red_full = _split_wait("gather_small_wait", "gather", small_handle, [0], [rows],
                           [stacked[n][0] for n, _ in layer0])[1][0]
    small_g = dict(zip(small_names, _unpack(red_full, small_shapes)))
    for n in SMALL_SHARDED:
        width = A[n].shape[-1]
        small_g[n] = lax.dynamic_slice_in_dim(small_g[n], me * width, width, axis=small_g[n].ndim - 1)
    for n in small_names:
        outs[n] = [small_g[n]] + _adamw_small(f"adamw_{n}", A[n], small_g[n], A["m_" + n], A["v_" + n])

    return (loss, grad_x, *[outs[n][0] for n in WEIGHTS], *[outs[n][1] for n in WEIGHTS],
            *[outs[n][2] for n in WEIGHTS], *[outs[n][3] for n in WEIGHTS])
```

```python
import functools
import math

import jax
import jax.numpy as jnp
from jax import lax
from jax.experimental import pallas as pl
from jax.experimental.pallas import tpu as pltpu

F32 = jnp.float32
BF16 = jnp.bfloat16
MESH = pl.DeviceIdType.MESH
N_DEV = 8
LANES = 128
SUBLANES = 8

LN_EPS = 1e-5
LRU_C = 8.0
CONV_WIDTH = 4
POOL_HALO = 16
ADAM_LR = 0.001
ADAM_B1 = 0.9
ADAM_B2 = 0.999
ADAM_EPS = 1e-08
ADAM_WD = 0.01
ADAM_STEP = 10

VMEM_LIMIT = 48 * 1024 * 1024
VMEM_LIMIT_SEQ = 56 * 1024 * 1024
SEQ_CHUNK = 1024
MM_TK = 4096
MM_TK_TOKENS = 4096
MM_TM_ROWS = 512
MM_TN = 512
MM_TN_WIDE = 1024
MM_TN_WIDE_MAX_K = 2048
GELU_C0 = math.sqrt(2.0 / math.pi)
GELU_C1 = 0.044715


def _in_hbm(*arrays):
    return [pltpu.with_memory_space_constraint(a, pltpu.HBM) for a in arrays]


def _cparams(*sem, limit=None):
    return pltpu.CompilerParams(dimension_semantics=tuple(sem) if sem else None,
                                vmem_limit_bytes=VMEM_LIMIT if limit is None else limit)


def _tile(n, pref):
    if n <= pref:
        return n
    t = pref - pref % LANES
    while t >= LANES:
        if n % t == 0:
            return t
        t -= LANES
    return n


def _row_tile(n, pref):
    if n <= pref:
        return n
    t = pref - pref % SUBLANES
    while t >= SUBLANES:
        if n % t == 0:
            return t
        t -= SUBLANES
    return n


def _mm(name, a, b, mode, out_dtypes, epi=None, extras=(), a_lead=None, b_lead=None, tm=1024, tn=None, tk=None,
        after=(), n_sums=0):
    if isinstance(b, (list, tuple)):
        b, b_lead = b[b_lead], None
    a2 = a.shape[-2:]
    b2 = b.shape[-2:]
    if mode == "nn":
        (M, K), N = a2, b2[1]
        assert b2[0] == K
    elif mode == "nt":
        (M, K), N = a2, b2[0]
        assert b2[1] == K
    else:
        (K, M), N = a2, b2[1]
        assert b2[0] == K
    if tk is None:
        tk = MM_TK_TOKENS if mode == "tn" else MM_TK
    if tn is None:
        tn = MM_TN_WIDE if (mode != "tn" and K <= MM_TN_WIDE_MAX_K) else MM_TN
    tm, tn, tk = _tile(M, tm), _tile(N, tn), _tile(K, tk)
    nk = K // tk
    n_extra = len(extras)
    n_out = len(out_dtypes)
    assert n_sums == 0 or tn == N
    resident = {"pipeline_mode": pl.Buffered(1)} if (tn == N and nk == 1) else {}

    def lead(shape, idx, which, **kw):
        if which is None:
            return pl.BlockSpec(shape, idx, **kw)
        return pl.BlockSpec((None,) + shape, lambda i, j, k: (which,) + idx(i, j, k), **kw)

    if mode == "nn":
        a_spec = lead((tm, tk), lambda i, j, k: (i, k), a_lead)
        b_spec = lead((tk, tn), lambda i, j, k: (k, j), b_lead, **resident)
        dims = (((1,), (0,)), ((), ()))
    elif mode == "nt":
        a_spec = lead((tm, tk), lambda i, j, k: (i, k), a_lead)
        b_spec = lead((tn, tk), lambda i, j, k: (j, k), b_lead, **resident)
        dims = (((1,), (1,)), ((), ()))
    else:
        a_spec = lead((tk, tm), lambda i, j, k: (k, i), a_lead)
        b_spec = lead((tk, tn), lambda i, j, k: (k, j), b_lead, **resident)
        dims = (((0,), (0,)), ((), ()))
    e_specs = []
    for e in extras:
        if e.shape[0] == 1:
            e_specs.append(pl.BlockSpec((1, tn), lambda i, j, k: (0, j)))
        else:
            e_specs.append(pl.BlockSpec((tm, tn), lambda i, j, k: (i, j)))

    n_after = len(after)

    def body(a_ref, b_ref, *rest):
        e_refs = rest[:n_extra]
        rest = rest[:n_extra] + rest[n_extra + n_after:]
        o_refs = rest[n_extra:n_extra + n_out]
        s_refs = rest[n_extra + n_out:n_extra + n_out + n_sums]
        part = lax.dot_general(a_ref[...].astype(BF16), b_ref[...].astype(BF16), dims, preferred_element_type=F32)

        def finish(r):
            res = (r,) if epi is None else epi(r, *[e[...] for e in e_refs])
            for o, v in zip(o_refs, res[:n_out]):
                o[...] = v.astype(o.dtype)
            first = pl.program_id(0) == 0
            for sr, v in zip(s_refs, res[n_out:]):
                @pl.when(first)
                def _(sr=sr, v=v):
                    sr[...] = v

                @pl.when(jnp.logical_not(first))
                def _(sr=sr, v=v):
                    sr[...] += v

        if nk == 1:
            finish(part)
            return
        acc = rest[n_extra + n_out + n_sums]
        k = pl.program_id(2)

        @pl.when(k == 0)
        def _():
            acc[...] = part

        @pl.when(jnp.logical_and(k > 0, k < nk - 1))
        def _():
            acc[...] += part

        @pl.when(k == nk - 1)
        def _():
            finish(acc[...] + part)

    outs = pl.pallas_call(
        body,
        name=name,
        grid=(M // tm, N // tn, nk),
        in_specs=[a_spec, b_spec] + e_specs + [pl.BlockSpec(memory_space=pl.ANY)] * n_after,
        out_specs=[pl.BlockSpec((tm, tn), lambda i, j, k: (i, j)) for _ in out_dtypes]
        + [pl.BlockSpec((1, tn), lambda i, j, k: (0, 0))] * n_sums,
        out_shape=[jax.ShapeDtypeStruct((M, N), d) for d in out_dtypes] + [jax.ShapeDtypeStruct((1, N), F32)] * n_sums,
        scratch_shapes=[pltpu.VMEM((tm, tn), F32)] if nk > 1 else [],
        compiler_params=_cparams(*(("arbitrary",) * 3 if n_sums else ("parallel", "parallel", "arbitrary"))),
    )(*_in_hbm(a, b, *extras), *after)
    return outs[0] if n_out + n_sums == 1 else tuple(outs)


def _rowwise(name, fn, tiled, params, outs, accs=(), tm=256, after=()):
    S = tiled[0].shape[0]
    tm = _tile(S, tm)
    nt, npar, no = len(tiled), len(params), len(outs)
    n_after = len(after)

    def body(*refs):
        t_refs = refs[:nt]
        p_refs = refs[nt:nt + npar]
        refs = refs[nt + npar + n_after:]
        o_refs = refs[:no]
        a_refs = refs[no:]
        res = fn(*[r[...] for r in t_refs], *[r[...] for r in p_refs])
        for o, v in zip(o_refs, res[:no]):
            o[...] = v.astype(o.dtype)
        first = pl.program_id(0) == 0
        for ar, v in zip(a_refs, res[no:]):
            @pl.when(first)
            def _(ar=ar, v=v):
                ar[...] = v

            @pl.when(jnp.logical_not(first))
            def _(ar=ar, v=v):
                ar[...] += v

    full = lambda p: pl.BlockSpec(p.shape, lambda i, nd=p.ndim: (0,) * nd)
    res = pl.pallas_call(
        body,
        name=name,
        grid=(S // tm,),
        in_specs=[pl.BlockSpec((tm, t.shape[1]), lambda i: (i, 0)) for t in tiled] + [full(p) for p in params]
        + [pl.BlockSpec(memory_space=pl.ANY)] * n_after,
        out_specs=[pl.BlockSpec((tm, c), lambda i: (i, 0)) for c, _ in outs]
        + [pl.BlockSpec(s, lambda i, nd=len(s): (0,) * nd) for s in accs],
        out_shape=[jax.ShapeDtypeStruct((S, c), d) for c, d in outs] + [jax.ShapeDtypeStruct(s, F32) for s in accs],
        compiler_params=_cparams("arbitrary"),
    )(*_in_hbm(*tiled, *params), *after)
    return res


def _ln_stats(z):
    mu = jnp.mean(z, axis=-1, keepdims=True)
    zc = z - mu
    var = jnp.mean(zc * zc, axis=-1, keepdims=True)
    return zc, lax.rsqrt(var + LN_EPS)


def _ln_apply(z, g, b):
    zc, rstd = _ln_stats(z)
    y = zc * rstd * g + b
    return z, y, y


def _ln_grad(dy, z, g):
    zc, rstd = _ln_stats(z)
    xhat = zc * rstd
    dxh = dy * g
    m1 = jnp.mean(dxh, axis=-1, keepdims=True)
    m2 = jnp.mean(dxh * xhat, axis=-1, keepdims=True)
    dz = rstd * (dxh - m1 - xhat * m2)
    return dz, dz, jnp.sum(dy * xhat, axis=0, keepdims=True), jnp.sum(dy, axis=0, keepdims=True)


def _ple_grad(dx3, gpre, pp):
    gate = jax.nn.sigmoid(gpre)
    dgpre = dx3 * pp * gate * (1.0 - gate)
    return dx3 * gate, dgpre, jnp.sum(dgpre, axis=0, keepdims=True)


def _loss_and_grad(name, y, target, gpre, pp):
    d = y.shape[1]

    def fn(y, t, gpre, pp):
        err = y - t
        sq = jnp.sum(jnp.sum(err * err, axis=0, keepdims=True), axis=1, keepdims=True)
        dy = err * (1.0 / d)
        dpp, dgpre, dbg = _ple_grad(dy, gpre, pp)
        return dy, dpp, dgpre, jnp.broadcast_to(sq, (1, LANES)), dbg

    return _rowwise(name, fn, [y, target, gpre, pp], [], [(d, F32), (d, BF16), (d, BF16)], accs=[(1, LANES), (1, d)])


def _rows(shape):
    return lax.broadcasted_iota(jnp.int32, shape, 0)


def _gelu(y):
    t = jnp.tanh(GELU_C0 * (y + GELU_C1 * y * y * y))
    return 0.5 * y * (1.0 + t), t


def _gelu_grad(y, t):
    return 0.5 * (1.0 + t) + 0.5 * y * (1.0 - t * t) * GELU_C0 * (1.0 + 3.0 * GELU_C1 * y * y)


def _softplus(x):
    return jnp.maximum(x, 0.0) + jnp.log(1.0 + jnp.exp(-jnp.abs(x)))


def _conv_fwd(xs, cw, cb):
    n = xs.shape[0]
    u = cw[3:4] * xs
    for k in (1, 2, 3):
        u = u + cw[3 - k:4 - k] * pltpu.roll(xs, k, 0)
    del n
    return u[SUBLANES:] + cb


def _lru_gates(u, wa, wx, ba, bx, sp, grow):
    ub = u.astype(BF16)
    r = jax.nn.sigmoid(jnp.dot(ub, wa, preferred_element_type=F32) + ba)
    ig = jax.nn.sigmoid(jnp.dot(ub, wx, preferred_element_type=F32) + bx)
    log_a = (-LRU_C) * r * sp
    a = jnp.exp(log_a)
    mult = jnp.sqrt(jnp.tanh(-log_a) * (1.0 + a * a))
    mult = jnp.where(grow == 0, 1.0, mult)
    return ub, r, ig, a, mult


def _scan8_fwd(a, b):
    row = _rows(a.shape)
    for k in (1, 2, 4):
        m = row >= k
        b = jnp.where(m, a * pltpu.roll(b, k, 0) + b, b)
        a = jnp.where(m, a * pltpu.roll(a, k, 0), a)
    return a, b


def _scan8_bwd(c, d):
    row = _rows(c.shape)
    for k in (1, 2, 4):
        m = row < SUBLANES - k
        d = jnp.where(m, c * pltpu.roll(d, SUBLANES - k, 0) + d, d)
        c = jnp.where(m, c * pltpu.roll(c, SUBLANES - k, 0), c)
    return c, d


def _pad_copy(dst, src, front, back):
    s, c = src.shape
    if front:
        dst[pl.ds(0, front), :] = jnp.zeros((front, c), dst.dtype)
    if back:
        dst[pl.ds(front + s, back), :] = jnp.zeros((back, c), dst.dtype)
    dst[pl.ds(front, s), :] = src[...].astype(dst.dtype)


def _lru_fwd(name, proj, par, wa, wx):
    S = proj.shape[0]
    R = proj.shape[1] // 2
    H = R // LANES
    ch = _tile(S, SEQ_CHUNK)
    nch = S // ch
    H8 = SUBLANES

    def body(up_ref, y_ref, par_ref, wa_ref, wx_ref, gh_ref, h_ref, r_ref, ig_ref, a_ref, mult_ref, gy_ref, dgy_ref,
             u_ref, up_pad):
        _pad_copy(up_pad, up_ref, H8, 0)
        par = par_ref[...]
        cw, cb, ba, bx = par[0:4], par[4:5], par[5:6], par[6:7]
        sp = _softplus(-par[7:8])
        wa_m, wx_m = wa_ref[...], wx_ref[...]

        def chunk(ci, carry):
            r0 = pl.multiple_of(ci * ch, ch)
            xs = up_pad[pl.ds(r0, ch + H8), :]
            u = _conv_fwd(xs, cw, cb)
            grow = _rows(u.shape) + r0
            _, r, ig, a, mult = _lru_gates(u, wa_m, wx_m, ba, bx, sp, grow)
            for ref, val in ((r_ref, r), (ig_ref, ig), (a_ref, a), (mult_ref, mult), (u_ref, u)):
                ref[pl.ds(r0, ch), :] = val
            bt = mult * (ig * u)
            hs = []
            for j in range(ch // H8):
                aa, bb = _scan8_fwd(a[j * H8:(j + 1) * H8], bt[j * H8:(j + 1) * H8])
                hj = bb + aa * carry
                carry = hj[H8 - 1:H8]
                hs.append(hj)
            h = jnp.concatenate(hs, axis=0)
            h_ref[pl.ds(r0, ch), :] = h
            y = y_ref[pl.ds(r0, ch), :]
            gy, t = _gelu(y)
            gy_ref[pl.ds(r0, ch), :] = gy
            dgy_ref[pl.ds(r0, ch), :] = _gelu_grad(y, t)
            gh_ref[pl.ds(r0, ch), :] = (h * gy).astype(gh_ref.dtype)
            return carry

        lax.fori_loop(0, nch, chunk, jnp.zeros((1, LANES), F32))

    col = lambda off: pl.BlockSpec((S, LANES), lambda h: (0, h + off))
    return pl.pallas_call(
        body,
        name=name,
        grid=(H,),
        in_specs=[col(0), col(H), pl.BlockSpec((8, LANES), lambda h: (0, h)),
                  pl.BlockSpec((None, LANES, LANES), lambda h: (h, 0, 0)),
                  pl.BlockSpec((None, LANES, LANES), lambda h: (h, 0, 0))],
        out_specs=[col(0)] * 9,
        out_shape=[jax.ShapeDtypeStruct((S, R), BF16)] + [jax.ShapeDtypeStruct((S, R), F32)] * 8,
        scratch_shapes=[pltpu.VMEM((S + H8, LANES), F32)],
        compiler_params=_cparams("parallel"),
    )(proj, proj, par, wa, wx)


def _lru_bwd(name, proj, h, dgh, gates, par, wa, wx):
    S = proj.shape[0]
    R = proj.shape[1] // 2
    H = R // LANES
    ch = _tile(S, SEQ_CHUNK)
    nch = S // ch
    H8 = SUBLANES
    nb = ch // H8

    def body(up_ref, h_ref, dgh_ref, r_ref, ig_ref, a_ref, mult_ref, gy_ref, dgy_ref, u_ref, par_ref, wa_ref, wx_ref,
             dup_ref, dy_ref, dwa_ref, dwx_ref, dpar_ref, up_pad, du_pad, vec_acc):
        _pad_copy(up_pad, up_ref, H8, 0)
        du_pad[pl.ds(S, H8), :] = jnp.zeros((H8, LANES), F32)
        par = par_ref[...]
        cw, lam = par[0:4], par[7:8]
        sp = _softplus(-lam)
        wa_m, wx_m = wa_ref[...], wx_ref[...]
        dwa_ref[...] = jnp.zeros_like(dwa_ref)
        dwx_ref[...] = jnp.zeros_like(dwx_ref)
        vec_acc[...] = jnp.zeros_like(vec_acc)
        nt_dims = (((1,), (1,)), ((), ()))
        tn_dims = (((0,), (0,)), ((), ()))

        def chunk(it, carry):
            lam_next, a_next = carry
            ci = nch - 1 - it
            r0 = pl.multiple_of(ci * ch, ch)
            here = pl.ds(r0, ch)
            u = u_ref[here, :]
            row = _rows(u.shape)
            grow = row + r0
            ub = u.astype(BF16)
            r, ig, a, mult = r_ref[here, :], ig_ref[here, :], a_ref[here, :], mult_ref[here, :]
            hcur = h_ref[here, :]
            before = h_ref[pl.ds(pl.multiple_of(jnp.maximum(r0 - H8, 0), H8), H8), :][H8 - 1:H8]
            hprev = jnp.where(row == 0, jnp.where(ci > 0, before, 0.0), pltpu.roll(hcur, 1, 0))
            dgh = dgh_ref[here, :]
            dy_ref[here, :] = (dgh * hcur * dgy_ref[here, :]).astype(dy_ref.dtype)
            dh = dgh * gy_ref[here, :]
            c = jnp.where(row == ch - 1, a_next, pltpu.roll(a, ch - 1, 0))
            ls = [None] * nb
            for j in range(nb - 1, -1, -1):
                cc, dd = _scan8_bwd(c[j * H8:(j + 1) * H8], dh[j * H8:(j + 1) * H8])
                lj = dd + cc * lam_next
                lam_next = lj[0:1]
                ls[j] = lj
            lmb = jnp.concatenate(ls, axis=0)
            da = lmb * hprev
            gu = ig * u
            dmult = lmb * gu
            dlog_a = da * a + jnp.where(grow == 0, 0.0, dmult * (-(a * a) / mult))
            dr = dlog_a * ((-LRU_C) * sp)
            drp = dr * r * (1.0 - r)
            dip = (lmb * mult * u) * ig * (1.0 - ig)
            drb, dib = drp.astype(BF16), dip.astype(BF16)
            du = (lmb * mult * ig
                  + lax.dot_general(drb, wa_m, nt_dims, preferred_element_type=F32)
                  + lax.dot_general(dib, wx_m, nt_dims, preferred_element_type=F32))
            du_pad[pl.ds(r0, ch), :] = du
            dwa_ref[...] += lax.dot_general(ub, drb, tn_dims, preferred_element_type=F32)
            dwx_ref[...] += lax.dot_general(ub, dib, tn_dims, preferred_element_type=F32)
            ssum = lambda v: jnp.sum(v, axis=0, keepdims=True)
            vec_acc[0:1, :] += ssum(drp)
            vec_acc[1:2, :] += ssum(dip)
            vec_acc[2:3, :] += ssum(dlog_a * ((-LRU_C) * r))
            return lam_next, a[0:1]

        zero = jnp.zeros((1, LANES), F32)
        lax.fori_loop(0, nch, chunk, (zero, zero))

        def conv_chunk(ci, acc):
            r0 = pl.multiple_of(ci * ch, ch)
            ds = du_pad[pl.ds(r0, ch + H8), :]
            xs = up_pad[pl.ds(r0, ch + H8), :]
            n = ch + H8
            du = ds[:ch]
            dup = cw[3:4] * du
            new = [acc[3] + jnp.sum(du * xs[H8:], axis=0, keepdims=True)]
            for k in (1, 2, 3):
                dup = dup + cw[3 - k:4 - k] * pltpu.roll(ds, n - k, 0)[:ch]
                new.append(acc[3 - k] + jnp.sum(du * pltpu.roll(xs, k, 0)[H8:], axis=0, keepdims=True))
            dup_ref[pl.ds(r0, ch), :] = dup.astype(dup_ref.dtype)
            return (new[3], new[2], new[1], new[0], acc[4] + jnp.sum(du, axis=0, keepdims=True))

        acc = lax.fori_loop(0, nch, conv_chunk, (zero,) * 5)
        dlam = vec_acc[2:3, :] * (-jax.nn.sigmoid(-lam))
        dpar_ref[...] = jnp.concatenate(list(acc) + [vec_acc[0:1, :], vec_acc[1:2, :], dlam], axis=0)

    col = lambda off: pl.BlockSpec((S, LANES), lambda h: (0, h + off))
    head = pl.BlockSpec((None, LANES, LANES), lambda h: (h, 0, 0))
    return pl.pallas_call(
        body,
        name=name,
        grid=(H,),
        in_specs=[col(0)] * 10 + [pl.BlockSpec((8, LANES), lambda h: (0, h)), head, head],
        out_specs=[col(0), col(0), head, head, pl.BlockSpec((8, LANES), lambda h: (0, h))],
        out_shape=[jax.ShapeDtypeStruct((S, R), BF16), jax.ShapeDtypeStruct((S, R), BF16),
                   jax.ShapeDtypeStruct((H, LANES, LANES), F32), jax.ShapeDtypeStruct((H, LANES, LANES), F32),
                   jax.ShapeDtypeStruct((8, R), F32)],
        scratch_shapes=[pltpu.VMEM((S + H8, LANES), F32), pltpu.VMEM((S + H8, LANES), F32),
                        pltpu.VMEM((8, LANES), F32)],
        compiler_params=_cparams("parallel", limit=VMEM_LIMIT_SEQ),
    )(proj, h, dgh, *gates, par, wa, wx)


def _window_sum(xs, g, up):
    n = xs.shape[0]
    s = xs
    for lvl, k in enumerate((1, 2, 4, 8)):
        sh = pltpu.roll(s, (n - k) if up else k, 0)
        s = s + jnp.where(g >= lvl, sh, 0.0)
    return s


def _pool_count(grow, g):
    return jnp.minimum(grow + 1, lax.shift_left(jnp.int32(2), g)).astype(F32)


def _pool_fwd(name, u, wgrp, par):
    S, D = u.shape
    G, W = wgrp.shape[0], wgrp.shape[1]
    ch = _tile(S, SEQ_CHUNK)
    nch = S // ch
    PH = POOL_HALO

    def body(u_ref, w_ref, par_ref, zs_ref, pooled_ref, z_ref, u_pad):
        g = pl.program_id(0)
        _pad_copy(u_pad, u_ref, PH, 0)
        par = par_ref[...]
        w = w_ref[...]

        def chunk(ci, _):
            r0 = pl.multiple_of(ci * ch, ch)
            xs = u_pad[pl.ds(r0, ch + PH), :]
            ws = _window_sum(xs, g, False)[PH:]
            uc = xs[PH:]
            cnt = _pool_count(_rows(uc.shape) + r0, g)
            pooled = (ws / cnt - uc).astype(BF16)
            z = jnp.dot(pooled, w, preferred_element_type=F32) + par[0:1]
            pooled_ref[pl.ds(r0, ch), :] = pooled
            z_ref[pl.ds(r0, ch), :] = z
            zs_ref[pl.ds(r0, ch), :] = (z * par[1:2]).astype(zs_ref.dtype)
            return 0

        lax.fori_loop(0, nch, chunk, 0)

    blk = pl.BlockSpec((S, W), lambda g: (0, g))
    return pl.pallas_call(
        body,
        name=name,
        grid=(G,),
        in_specs=[blk, pl.BlockSpec((None, W, W), lambda g: (g, 0, 0)), pl.BlockSpec((2, W), lambda g: (0, g))],
        out_specs=[blk, blk, blk],
        out_shape=[jax.ShapeDtypeStruct((S, D), BF16), jax.ShapeDtypeStruct((S, D), BF16),
                   jax.ShapeDtypeStruct((S, D), F32)],
        scratch_shapes=[pltpu.VMEM((S + PH, W), F32)],
        compiler_params=_cparams("parallel"),
    )(u, wgrp, par)


def _pool_bwd(name, pooled, z, dzs, wgrp, par):
    S, D = z.shape
    G, W = wgrp.shape[0], wgrp.shape[1]
    ch = _tile(S, SEQ_CHUNK)
    nch = S // ch
    PH = POOL_HALO

    def body(pooled_ref, z_ref, dzs_ref, w_ref, par_ref, du_ref, dw_ref, dpar_ref, q_pad, dw_acc):
        g = pl.program_id(0)
        q_pad[pl.ds(S, PH), :] = jnp.zeros((PH, W), F32)
        par = par_ref[...]
        w = w_ref[...]
        dw_acc[...] = jnp.zeros_like(dw_acc)

        def chunk(ci, acc):
            db, dsc = acc
            r0 = pl.multiple_of(ci * ch, ch)
            cnt = _pool_count(_rows((ch, W)) + r0, g)
            pooled = pooled_ref[pl.ds(r0, ch), :]
            z = z_ref[pl.ds(r0, ch), :]
            dzs = dzs_ref[pl.ds(r0, ch), :]
            dz = dzs * par[1:2]
            dzb = dz.astype(BF16)
            dw_acc[...] += lax.dot_general(pooled, dzb, (((0,), (0,)), ((), ())), preferred_element_type=F32)
            dpooled = lax.dot_general(dzb, w, (((1,), (1,)), ((), ())), preferred_element_type=F32)
            q_pad[pl.ds(r0, ch), :] = dpooled / cnt
            return (db + jnp.sum(dz, axis=0, keepdims=True), dsc + jnp.sum(dzs * z, axis=0, keepdims=True))

        zero = jnp.zeros((1, W), F32)
        db, dsc = lax.fori_loop(0, nch, chunk, (zero, zero))
        dpar_ref[...] = jnp.concatenate([db, dsc], axis=0)
        dw_ref[...] = dw_acc[...].astype(dw_ref.dtype)

        def back(ci, _):
            r0 = pl.multiple_of(ci * ch, ch)
            qs = q_pad[pl.ds(r0, ch + PH), :]
            ws = _window_sum(qs, g, True)[:ch]
            qc = qs[:ch]
            cnt = _pool_count(_rows(qc.shape) + r0, g)
            du_ref[pl.ds(r0, ch), :] = (ws - qc * cnt).astype(du_ref.dtype)
            return 0

        lax.fori_loop(0, nch, back, 0)

    blk = pl.BlockSpec((S, W), lambda g: (0, g))
    wspec = pl.BlockSpec((None, W, W), lambda g: (g, 0, 0))
    pspec = pl.BlockSpec((2, W), lambda g: (0, g))
    return pl.pallas_call(
        body,
        name=name,
        grid=(G,),
        in_specs=[blk, blk, blk, wspec, pspec],
        out_specs=[blk, wspec, pspec],
        out_shape=[jax.ShapeDtypeStruct((S, D), BF16), jax.ShapeDtypeStruct((G, W, W), BF16),
                   jax.ShapeDtypeStruct((2, D), F32)],
        scratch_shapes=[pltpu.VMEM((S + PH, W), F32), pltpu.VMEM((W, W), F32)],
        compiler_params=_cparams("parallel"),
    )(pooled, z, dzs, wgrp, par)


def _my_place():
    x, y, c = lax.axis_index("x"), lax.axis_index("y"), lax.axis_index("c")
    return x, y, c, 4 * x + 2 * y + c


def _peers(x, y, c):
    out = []
    for d in range(1, N_DEV):
        px = 1 - x if d & 4 else x
        py = 1 - y if d & 2 else y
        pc = 1 - c if d & 1 else c
        out.append(((px, py, pc), 4 * px + 2 * py + pc))
    return out


def _window(ref, axis, start, size):
    idx = [slice(None)] * len(ref.shape)
    idx[axis] = pl.ds(start, size)
    return ref.at[tuple(idx)]


def _to_bf16(name, arrs, after=()):
    outs = []
    for i, a in enumerate(arrs):
        a2 = a.reshape(-1, a.shape[-1])
        tr = _tile(a2.shape[0], 512)
        o = pl.pallas_call(
            lambda a_ref, *rest: rest[-1].__setitem__(Ellipsis, a_ref[...].astype(BF16)),
            name=f"{name}_{i}",
            grid=(a2.shape[0] // tr,),
            in_specs=[pl.BlockSpec((tr, a2.shape[1]), lambda r: (r, 0))] + [pl.BlockSpec(memory_space=pl.ANY)] * len(after),
            out_specs=pl.BlockSpec((tr, a2.shape[1]), lambda r: (r, 0)),
            out_shape=jax.ShapeDtypeStruct(a2.shape, BF16),
            compiler_params=_cparams("parallel"),
        )(a2, *after)
        outs.append(o.reshape(a.shape))
    return outs


def _all_gather(name, shards, axes):
    n = len(shards)
    sizes = [s.shape[ax] for s, ax in zip(shards, axes)]

    def body(*refs):
        ins, outs = refs[:n], refs[n:2 * n]
        send, recv, loc = refs[2 * n:]
        x, y, c, me = _my_place()
        peers = _peers(x, y, c)
        local = []
        for i in range(n):
            dst = _window(outs[i], axes[i], me * sizes[i], sizes[i])
            cp = pltpu.make_async_copy(ins[i], dst, loc.at[i])
            cp.start()
            local.append(cp)
            for peer, _ in peers:
                pltpu.make_async_remote_copy(src_ref=ins[i], dst_ref=dst, send_sem=send.at[i], recv_sem=recv.at[i],
                                             device_id=peer, device_id_type=MESH).start()
        for i in range(n):
            local[i].wait()
            seven = _window(outs[i], axes[i], 0, (N_DEV - 1) * sizes[i])
            pltpu.make_async_remote_copy(src_ref=seven, dst_ref=seven, send_sem=send.at[i], recv_sem=recv.at[i],
                                         device_id=(x, y, c), device_id_type=MESH).wait()

    def full_shape(s, ax):
        shp = list(s.shape)
        shp[ax] *= N_DEV
        return jax.ShapeDtypeStruct(tuple(shp), s.dtype)

    any_spec = pl.BlockSpec(memory_space=pl.ANY)
    return pl.pallas_call(
        body,
        name=name,
        in_specs=[any_spec] * n,
        out_specs=[any_spec] * n,
        out_shape=[full_shape(s, ax) for s, ax in zip(shards, axes)],
        scratch_shapes=[pltpu.SemaphoreType.DMA((n,)), pltpu.SemaphoreType.DMA((n,)), pltpu.SemaphoreType.DMA((n,))],
        compiler_params=pltpu.CompilerParams(has_side_effects=True),
    )(*shards)


HBM_SPEC = pl.BlockSpec(memory_space=pltpu.HBM)
SEM_SPEC = pl.BlockSpec(memory_space=pltpu.SEMAPHORE)
SPLIT_EFFECT = pltpu.SideEffectType.DATAFLOW_SIDE_EFFECTING


def _push_all(kind, src, dst, axis, size, send_sem, recv_sem, place):
    x, y, c, me = place
    for peer, pidx in _peers(x, y, c):
        if kind == "gather":
            s = d = _window(dst, axis, me * size, size)
        else:
            s, d = _window(src, axis, pidx * size, size), dst.at[me]
        pltpu.make_async_remote_copy(src_ref=s, dst_ref=d, send_sem=send_sem, recv_sem=recv_sem, device_id=peer,
                                     device_id_type=MESH).start()


def _drain_all(kind, dst, axis, size, send_sem, recv_sem, place):
    x, y, c, _ = place
    seven = _window(dst, axis, 0, (N_DEV - 1) * size) if kind == "gather" else dst.at[pl.ds(0, N_DEV - 1)]
    pltpu.make_async_remote_copy(src_ref=seven, dst_ref=seven, send_sem=send_sem, recv_sem=recv_sem,
                                 device_id=(x, y, c), device_id_type=MESH).wait()


def _own_block_placed(src, axis, size, me):
    own = lax.dynamic_slice_in_dim(src, me * size, size, axis)
    return lax.dynamic_update_slice_in_dim(lax.empty((N_DEV,) + own.shape, src.dtype), own[None], me, 0)


def _split_start(name, kind, srcs, lands, axes, sizes, after=()):
    n, ns, na = len(lands), len(srcs), len(after)

    def body(*refs):
        src_refs, land_refs = refs[:ns], refs[ns:ns + n]
        send, recv = refs[ns + n + na], refs[ns + n + na + 1]
        token = refs[-1]
        place = _my_place()
        for k in range(n):
            _push_all(kind, src_refs[k] if ns else None, land_refs[k], axes[k], sizes[k], send.at[k], recv.at[k], place)
        token[...] = jnp.zeros_like(token)

    hbm = lambda a: pltpu.HBM(a.shape, a.dtype)
    res = pl.pallas_call(
        body,
        name=name,
        out_shape=(pltpu.SemaphoreType.DMA((n,)), pltpu.SemaphoreType.DMA((n,)), *[hbm(a) for a in srcs],
                   *[hbm(a) for a in lands], jax.ShapeDtypeStruct((SUBLANES, LANES), F32)),
        in_specs=[HBM_SPEC] * (ns + n) + [pl.BlockSpec(memory_space=pl.ANY)] * na,
        out_specs=(SEM_SPEC, SEM_SPEC, *[HBM_SPEC] * (ns + n), pl.BlockSpec(memory_space=pltpu.VMEM)),
        input_output_aliases={k: 2 + k for k in range(ns + n)},
        compiler_params=pltpu.CompilerParams(has_side_effects=SPLIT_EFFECT),
    )(*[pltpu.with_memory_space_constraint(a, pltpu.HBM) for a in (*srcs, *lands)], *after)
    return res[0], res[1], list(res[2:2 + ns]), list(res[2 + ns:2 + ns + n]), res[-1]


def _split_wait(name, kind, handle, axes, sizes, after):
    send, recv, srcs, lands, _ = handle
    n, ns = len(lands), len(srcs)
    after = list(after) if isinstance(after, (list, tuple)) else [after]

    def body(*refs):
        land_refs = refs[ns:ns + n]
        send_ref, recv_ref = refs[ns + n], refs[ns + n + 1]
        place = _my_place()
        for k in range(n):
            _drain_all(kind, land_refs[k], axes[k], sizes[k], send_ref.at[k], recv_ref.at[k], place)

    hbm = lambda a: pltpu.HBM(a.shape, a.dtype)
    res = pl.pallas_call(
        body,
        name=name,
        out_shape=tuple(hbm(a) for a in (*srcs, *lands)),
        in_specs=[HBM_SPEC] * (ns + n) + [SEM_SPEC, SEM_SPEC] + [pl.BlockSpec(memory_space=pl.ANY)] * len(after),
        out_specs=tuple([HBM_SPEC] * (ns + n)),
        input_output_aliases={k: k for k in range(ns + n)},
        compiler_params=pltpu.CompilerParams(has_side_effects=SPLIT_EFFECT),
    )(*srcs, *lands, send, recv, *after)
    return list(res[:ns]), list(res[ns:])


def _cast_into_window(name, a, l, axis, me1, after=()):
    shp = a.shape[1:]
    cast = lambda me_ref, a_ref, *rest: rest[-1].__setitem__(Ellipsis, a_ref[...].astype(BF16))
    if len(shp) == 3:
        assert axis == 1
        G, r, c = shp
        full = (G, r * N_DEV, c)
        grid = (G,)
        in_spec = pl.BlockSpec((None, None, r, c), lambda g, me: (l, g, 0, 0))
        out_spec = pl.BlockSpec((None, r, c), lambda g, me: (g, me[0], 0))
    else:
        r, c = shp
        tr = _tile(r, 512)
        nb = r // tr
        grid = (nb,)
        in_spec = pl.BlockSpec((None, tr, c), lambda i, me: (l, i, 0))
        if axis == 0:
            full = (r * N_DEV, c)
            out_spec = pl.BlockSpec((tr, c), lambda i, me: (me[0] * nb + i, 0))
        else:
            full = (r, c * N_DEV)
            out_spec = pl.BlockSpec((tr, c), lambda i, me: (i, me[0]))
    return pl.pallas_call(
        cast,
        name=name,
        grid_spec=pltpu.PrefetchScalarGridSpec(
            num_scalar_prefetch=1, grid=grid,
            in_specs=[in_spec] + [pl.BlockSpec(memory_space=pl.ANY)] * len(after), out_specs=out_spec),
        out_shape=jax.ShapeDtypeStruct(full, BF16),
        compiler_params=_cparams("arbitrary"),
    )(me1, a, *after)


def _adamw_math(w, g, m, v):
    m = ADAM_B1 * m + (1.0 - ADAM_B1) * g
    v = ADAM_B2 * v + (1.0 - ADAM_B2) * jnp.square(g)
    m_hat = m / (1.0 - ADAM_B1 ** ADAM_STEP)
    v_hat = v / (1.0 - ADAM_B2 ** ADAM_STEP)
    delta = -ADAM_LR * (m_hat / (jnp.sqrt(v_hat) + ADAM_EPS) + ADAM_WD * w)
    return delta, m, v


def _sum_slots(buf_ref):
    g = buf_ref[0].astype(F32)
    for s in range(1, N_DEV):
        g = g + buf_ref[s].astype(F32)
    return g


def _adamw_layer(name, buf, w, m, v, l, prev, after=(), own=None, own_axis=0, me1=None):
    shape = w.shape
    L, C = shape[0], shape[-1]
    Rr = math.prod(shape[1:-1])
    buf3 = buf.reshape(N_DEV, Rr, C)
    w3, m3, v3 = (t.reshape(L, Rr, C) for t in (w, m, v))
    tr = _tile(Rr, 2 * LANES) if Rr % LANES == 0 else Rr
    nb = Rr // tr
    n_pass = (0 if prev is None else 4) + len(after)
    n_lead = 1 if own is None else 3

    def body(*refs):
        buf_ref = refs[0] if own is None else refs[1]
        w_ref, m_ref, v_ref = refs[n_lead], refs[n_lead + 1], refs[n_lead + 2]
        g_out, d_out, m_out, v_out = refs[n_lead + 3 + n_pass:]
        if own is None:
            g = _sum_slots(buf_ref)
        else:
            me_ref, mine = refs[0], refs[2][...].astype(F32)
            g = None
            for s in range(N_DEV):
                term = jnp.where(me_ref[0] == s, mine, buf_ref[s].astype(F32))
                g = term if g is None else g + term
        d, mm, vv = _adamw_math(w_ref[...], g, m_ref[...], v_ref[...])
        g_out[...] = g
        d_out[...] = d
        m_out[...] = mm
        v_out[...] = vv

    out_shape = [jax.ShapeDtypeStruct((L, Rr, C), F32)] * 4
    passed = [pl.BlockSpec(memory_space=pl.ANY)] * n_pass
    aliases = {} if prev is None else {n_lead + 3 + k: k for k in range(4)}
    if own is None:
        spec = pl.BlockSpec((None, tr, C), lambda r: (l, r, 0))
        outs = pl.pallas_call(
            body, name=name, grid=(nb,),
            in_specs=[pl.BlockSpec((N_DEV, tr, C), lambda r: (0, r, 0)), spec, spec, spec] + passed,
            out_specs=[spec] * 4, out_shape=out_shape, input_output_aliases=aliases,
            compiler_params=_cparams("parallel"),
        )(buf3, w3, m3, v3, *(prev or ()), *after)
    else:
        spec = pl.BlockSpec((None, tr, C), lambda r, me: (l, r, 0))
        own_idx = (lambda r, me: (me[0] * nb + r, 0)) if own_axis == 0 else (lambda r, me: (r, me[0]))
        outs = pl.pallas_call(
            body, name=name,
            grid_spec=pltpu.PrefetchScalarGridSpec(
                num_scalar_prefetch=1, grid=(nb,),
                in_specs=[pl.BlockSpec((N_DEV, tr, C), lambda r, me: (0, r, 0)), pl.BlockSpec((tr, C), own_idx),
                          spec, spec, spec] + passed,
                out_specs=[spec] * 4),
            out_shape=out_shape, input_output_aliases=aliases,
            compiler_params=_cparams("arbitrary"),
        )(me1, buf3, own, w3, m3, v3, *(prev or ()), *after)
    return list(outs)


def _sum8(name, buf):
    R = buf.shape[1]

    def body(buf_ref, o_ref):
        o_ref[...] = _sum_slots(buf_ref)

    return pl.pallas_call(
        body,
        name=name,
        in_specs=[pl.BlockSpec(buf.shape, lambda: (0, 0, 0))],
        out_specs=pl.BlockSpec((R, LANES), lambda: (0, 0)),
        out_shape=jax.ShapeDtypeStruct((R, LANES), F32),
        compiler_params=_cparams(),
    )(buf)


def _adamw_small(name, w, g, m, v):
    shape = w.shape
    w2, g2, m2, v2 = (t.reshape(-1, shape[-1]) for t in (w, g, m, v))
    R, C = w2.shape
    tr = _row_tile(R, 512)

    def body(w_ref, g_ref, m_ref, v_ref, d_out, m_out, v_out):
        d, mm, vv = _adamw_math(w_ref[...], g_ref[...], m_ref[...], v_ref[...])
        d_out[...] = d
        m_out[...] = mm
        v_out[...] = vv

    spec = pl.BlockSpec((tr, C), lambda r: (r, 0))
    outs = pl.pallas_call(
        body,
        name=name,
        grid=(R // tr,),
        in_specs=[spec] * 4,
        out_specs=[spec] * 3,
        out_shape=[jax.ShapeDtypeStruct((R, C), F32)] * 3,
        compiler_params=_cparams("parallel"),
    )(w2, g2, m2, v2)
    return [o.reshape(shape) for o in outs]


def _pack(arrs, pad_rows_to=SUBLANES):
    parts = []
    for a in arrs:
        flat = a.reshape(-1)
        per = LANES * pad_rows_to
        padded = -(-flat.shape[0] // per) * per
        if padded != flat.shape[0]:
            flat = jnp.pad(flat, (0, padded - flat.shape[0]))
        parts.append(flat.reshape(-1, LANES))
    return jnp.concatenate(parts, axis=0)


def _unpack(packed, shapes, pad_rows_to=SUBLANES):
    out = []
    r = 0
    for shp in shapes:
        nel = math.prod(shp)
        per = LANES * pad_rows_to
        rows = -(-nel // per) * pad_rows_to
        out.append(packed[r:r + rows].reshape(-1)[:nel].reshape(shp))
        r += rows
    return out


BIG = ("lru_w_in", "lru_w_out", "pool_w_in", "pool_w_grp", "pool_w_out", "mlp_w1", "mlp_w2", "ple_w", "ple_gate_w")
BIG_AXIS = {"lru_w_in": 2, "lru_w_out": 1, "pool_w_in": 1, "pool_w_grp": 2, "pool_w_out": 1, "mlp_w1": 2,
            "mlp_w2": 1, "ple_w": 2, "ple_gate_w": 1}
SMALL_SHARDED = ("lru_conv_w", "pool_b_grp", "pool_scale")
REPLICATED = ("lru_conv_b", "lru_wa", "lru_ba", "lru_wx", "lru_bx", "lru_lambda", "ln_mix_g", "ln_mix_b",
              "ln_mlp_g", "ln_mlp_b", "ple_gate_b")
WEIGHTS = ("lru_w_in", "lru_conv_w", "lru_conv_b", "lru_wa", "lru_ba", "lru_wx", "lru_bx", "lru_lambda", "lru_w_out",
           "pool_w_in", "pool_w_grp", "pool_b_grp", "pool_scale", "pool_w_out", "ln_mix_g", "ln_mix_b", "mlp_w1",
           "mlp_w2", "ln_mlp_g", "ln_mlp_b", "ple_w", "ple_gate_w", "ple_gate_b")
INPUTS = ("x", "p") + WEIGHTS + ("loss_target",) + tuple("m_" + n for n in WEIGHTS) + tuple("v_" + n for n in WEIGHTS)


def _gather_last_axis(packed_full, shard_shape):
    nel = math.prod(shard_shape)
    blocks = packed_full.reshape(N_DEV, -1)[:, :nel].reshape((N_DEV,) + tuple(shard_shape))
    return jnp.concatenate([blocks[d] for d in range(N_DEV)], axis=-1)


def kernel(x, p, lru_w_in, lru_conv_w, lru_conv_b, lru_wa, lru_ba, lru_wx, lru_bx, lru_lambda, lru_w_out, pool_w_in, pool_w_grp, pool_b_grp, pool_scale, pool_w_out, ln_mix_g, ln_mix_b, mlp_w1, mlp_w2, ln_mlp_g, ln_mlp_b, ple_w, ple_gate_w, ple_gate_b, loss_target, m_lru_w_in, m_lru_conv_w, m_lru_conv_b, m_lru_wa, m_lru_ba, m_lru_wx, m_lru_bx, m_lru_lambda, m_lru_w_out, m_pool_w_in, m_pool_w_grp, m_pool_b_grp, m_pool_scale, m_pool_w_out, m_ln_mix_g, m_ln_mix_b, m_mlp_w1, m_mlp_w2, m_ln_mlp_g, m_ln_mlp_b, m_ple_w, m_ple_gate_w, m_ple_gate_b, v_lru_w_in, v_lru_conv_w, v_lru_conv_b, v_lru_wa, v_lru_ba, v_lru_wx, v_lru_bx, v_lru_lambda, v_lru_w_out, v_pool_w_in, v_pool_w_grp, v_pool_b_grp, v_pool_scale, v_pool_w_out, v_ln_mix_g, v_ln_mix_b, v_mlp_w1, v_mlp_w2, v_ln_mlp_g, v_ln_mlp_b, v_ple_w, v_ple_gate_w, v_ple_gate_b):
    A = dict(zip(INPUTS, (x, p, lru_w_in, lru_conv_w, lru_conv_b, lru_wa, lru_ba, lru_wx, lru_bx, lru_lambda, lru_w_out, pool_w_in, pool_w_grp, pool_b_grp, pool_scale, pool_w_out, ln_mix_g, ln_mix_b, mlp_w1, mlp_w2, ln_mlp_g, ln_mlp_b, ple_w, ple_gate_w, ple_gate_b, loss_target, m_lru_w_in, m_lru_conv_w, m_lru_conv_b, m_lru_wa, m_lru_ba, m_lru_wx, m_lru_bx, m_lru_lambda, m_lru_w_out, m_pool_w_in, m_pool_w_grp, m_pool_b_grp, m_pool_scale, m_pool_w_out, m_ln_mix_g, m_ln_mix_b, m_mlp_w1, m_mlp_w2, m_ln_mlp_g, m_ln_mlp_b, m_ple_w, m_ple_gate_w, m_ple_gate_b, v_lru_w_in, v_lru_conv_w, v_lru_conv_b, v_lru_wa, v_lru_ba, v_lru_wx, v_lru_bx, v_lru_lambda, v_lru_w_out, v_pool_w_in, v_pool_w_grp, v_pool_b_grp, v_pool_scale, v_pool_w_out, v_ln_mix_g, v_ln_mix_b, v_mlp_w1, v_mlp_w2, v_ln_mlp_g, v_ln_mlp_b, v_ple_w, v_ple_gate_w, v_ple_gate_b)))
    depth = ln_mix_g.shape[0]
    alpha = (2 * depth) ** 0.25
    S, D = x.shape[1], x.shape[2]
    xs = x.reshape(S, D)
    tgt = loss_target.reshape(S, D)
    p3 = p.reshape(depth, S, p.shape[-1])
    me = 4 * lax.axis_index("x") + 2 * lax.axis_index("y") + lax.axis_index("c")

    def layer_weights(i):
        s = i // 2
        mixer = ("lru_w_in", "lru_w_out") if i % 2 == 0 else ("pool_w_in", "pool_w_grp", "pool_w_out")
        return [(n, s) for n in mixer] + [(n, i) for n in ("mlp_w1", "mlp_w2", "ple_w", "ple_gate_w")]

    def axis_of(key):
        return 0 if key[0] == "small" else BIG_AXIS[key[0]] - 1

    def start_gather(tag, keys, after):
        axes = [axis_of(k) for k in keys]
        lands = [land[k] for k in keys]
        sizes = [a.shape[ax] // N_DEV for a, ax in zip(lands, axes)]
        return keys, _split_start(f"gather_{tag}_start", "gather", [], lands, axes, sizes, after=after), axes, sizes

    def finish_gather(tag, pending, after):
        keys, handle, axes, sizes = pending
        for (n, l), full in zip(keys, _split_wait(f"gather_{tag}_wait", "gather", handle, axes, sizes, after)[1]):
            W[n][l] = full

    def start_exchange(tag, keys, arrs, after):
        axes = [axis_of(k) for k in keys]
        sizes = [a.shape[ax] // N_DEV for a, ax in zip(arrs, axes)]
        lands = []
        for k, a, ax, sz in zip(keys, arrs, axes, sizes):
            if reads_own_block(k):
                shp = list(a.shape)
                shp[ax] = sz
                lands.append(lax.empty((N_DEV,) + tuple(shp), a.dtype))
            else:
                lands.append(_own_block_placed(a, ax, sz, me))
        return keys, _split_start(f"exchange_{tag}_start", "scatter", arrs, lands, axes, sizes, after=after), axes, sizes

    def finish_exchange(tag, pending, after):
        keys, handle, axes, sizes = pending
        sources, landed = _split_wait(f"exchange_{tag}_wait", "scatter", handle, axes, sizes, after)
        partial.update(zip(keys, landed))
        own_grad.update(zip(keys, sources))

    def reads_own_block(key):
        return key[0] not in ("small", "pool_w_grp")

    me1 = jnp.reshape(me, (1,)).astype(jnp.int32)
    land = {}
    W = {n: [None] * A[n].shape[0] for n in BIG}
    small_shard_shapes = [A[n].shape for n in SMALL_SHARDED]
    gathered = _all_gather("gather_small_params", [_pack([A[n] for n in SMALL_SHARDED])], [0])
    def gather_groups(i):
        keys = layer_weights(i)
        mixer, (w1, w2, pw, pg) = keys[:-4], keys[-4:]
        if i % 2 == 0:
            return [("in", mixer[:1]), ("out", mixer[1:]), ("up", [w1]), ("rest", [w2, pw, pg])]
        return [("in", mixer + [w1]), ("rest", [w2, pw, pg])]

    gather_pending = {}

    def send_layer(i, behind):
        for tag, keys in gather_groups(i):
            for k in keys:
                land[k] = _cast_into_window(f"cast_{k[0]}_{k[1]}", A[k[0]], k[1], axis_of(k), me1, after=behind)
            gather_pending[(i, tag)] = start_gather(f"l{i}_{tag}", keys, behind)
            behind = (gather_pending[(i, tag)][1][4],)
        return behind

    layer0_started = send_layer(0, (gathered[0],))
    small_full = gathered[0].reshape(N_DEV, -1, LANES)
    r = 0
    for n, shp in zip(SMALL_SHARDED, small_shard_shapes):
        rows = -(-math.prod(shp) // (LANES * SUBLANES)) * SUBLANES
        W[n] = _gather_last_axis(small_full[:, r:r + rows], shp)
        r += rows
    wa_b, wx_b = _to_bf16("cast_gates", [lru_wa, lru_wx], after=layer0_started)
    n_lru = lru_w_in.shape[0]
    lru_par = [jnp.concatenate([W["lru_conv_w"][s], lru_conv_b[s][None], lru_ba[s][None], lru_bx[s][None],
                                lru_lambda[s][None]], axis=0) for s in range(n_lru)]
    pool_par = [jnp.stack([W["pool_b_grp"][s], W["pool_scale"][s]], axis=0) for s in range(pool_w_in.shape[0])]

    saved = []
    h_in = xs
    (h_in_b,) = _to_bf16("cast_inputs", [xs], after=layer0_started)
    layer1_started = send_layer(1, (layer0_started[0], h_in_b))
    finish_gather("l0_in", gather_pending[(0, "in")], [h_in_b, wa_b, layer1_started[0]])
    for i in range(depth):
        s = i // 2
        sv = {"x0b": h_in_b}
        if i > 0:
            finish_gather(f"l{i}_in", gather_pending[(i, "in")], h_in)
        ln_out = dict(out_dtypes=[F32, F32, BF16], tm=MM_TM_ROWS, tn=D,
                      epi=lambda acc, xp, g, b: _ln_apply(alpha * xp + acc, g, b))
        if i % 2 == 0:
            sv["proj"] = _mm(f"l{i}_lru_in", h_in_b, W["lru_w_in"], "nn", [F32], b_lead=s)
            if i == 0:
                behind = (layer1_started[0], sv["proj"])
                for later in range(2, depth):
                    behind = send_layer(later, behind)
                all_started = behind[0]
            sv["gh"], sv["h"], *sv["gates"] = _lru_fwd(f"l{i}_lru_core", sv["proj"], lru_par[s], wa_b[s], wx_b[s])
            finish_gather(f"l{i}_out", gather_pending[(i, "out")], [sv["gh"], all_started])
            mix_in, mix_w = sv["gh"], W["lru_w_out"]
        else:
            u_pool = _mm(f"l{i}_pool_in", h_in_b, W["pool_w_in"], "nn", [F32], b_lead=s)
            sv["zs"], sv["pooled"], sv["z"] = _pool_fwd(f"l{i}_pool_core", u_pool, W["pool_w_grp"][s], pool_par[s])
            mix_in, mix_w = sv["zs"], W["pool_w_out"]
        sv["z1"], sv["x1"], sv["x1b"] = _mm(f"l{i}_mix_out_ln", mix_in, mix_w, "nn", b_lead=s,
                                            extras=[h_in, ln_mix_g[i][None], ln_mix_b[i][None]], **ln_out)
        if i % 2 == 0:
            finish_gather(f"l{i}_up", gather_pending[(i, "up")], sv["x1b"])
        sv["hpre"], sv["hact"] = _mm(f"l{i}_mlp_up", sv["x1b"], W["mlp_w1"], "nn", [BF16, BF16], b_lead=i,
                                     epi=lambda acc: (acc, jnp.square(jnp.maximum(acc, 0.0))))
        finish_gather(f"l{i}_rest", gather_pending[(i, "rest")], sv["hact"])
        sv["z2"], sv["x2"], sv["x2b"] = _mm(f"l{i}_mlp_down_ln", sv["hact"], W["mlp_w2"], "nn", b_lead=i,
                                            extras=[sv["x1"], ln_mlp_g[i][None], ln_mlp_b[i][None]], **ln_out)
        sv["pp"] = _mm(f"l{i}_ple_up", p3, W["ple_w"], "nn", [F32], a_lead=i, b_lead=i)

        def ple_epi(acc, bg, x2t, ppt):
            gpre = acc + bg
            x3 = x2t + ppt * jax.nn.sigmoid(gpre)
            return x3, x3, gpre

        h_in, h_in_b, sv["gpre"] = _mm(f"l{i}_ple_gate", sv["x2b"], W["ple_gate_w"], "nn", [F32, BF16, F32], b_lead=i,
                                       epi=ple_epi, extras=[ple_gate_b[i][None], sv["x2"], sv["pp"]], tn=MM_TN)
        saved.append(sv)

    dx, dpp, dgpre, sq, dbg = _loss_and_grad("loss", h_in, tgt, saved[-1]["gpre"], saved[-1]["pp"])
    loss = lax.psum(0.5 * sq[0, 0] / D, ("x", "y", "c"))

    dW = {n: [None] * A[n].shape[0] for n in BIG}
    dsmall = {n: [None] * A[n].shape[0] for n in REPLICATED + SMALL_SHARDED}
    small_names = REPLICATED + SMALL_SHARDED
    partial = {}
    own_grad = {}
    exchange_pending = {}
    exchange_token = ()
    for i in reversed(range(depth)):
        s = i // 2
        sv = saved[i]
        dsmall["ple_gate_b"][i] = dbg[0]
        dW["ple_w"][i] = _mm(f"l{i}_d_ple_w", p3, dpp, "tn", [BF16], a_lead=i, after=exchange_token)
        dW["ple_gate_w"][i] = _mm(f"l{i}_d_ple_gate_w", sv["x2b"], dgpre, "tn", [BF16])
        ln_back = dict(out_dtypes=[F32, BF16], tm=MM_TM_ROWS, tn=D, n_sums=2)
        dz2, dz2b, dg, db = _mm(f"l{i}_d_x2_ln", dgpre, W["ple_gate_w"], "nt", b_lead=i,
                                extras=[dx, sv["z2"], ln_mlp_g[i][None]],
                                epi=lambda acc, d, z, g: _ln_grad(acc + d, z, g), after=exchange_token, **ln_back)
        dsmall["ln_mlp_g"][i], dsmall["ln_mlp_b"][i] = dg[0], db[0]
        dhpre = _mm(f"l{i}_d_hpre", dz2b, W["mlp_w2"], "nt", [BF16], b_lead=i, extras=[sv["hpre"]],
                    epi=lambda acc, hp: (acc * (2.0 * jnp.maximum(hp.astype(F32), 0.0)),))
        dW["mlp_w2"][i] = _mm(f"l{i}_d_mlp_w2", sv["hact"], dz2b, "tn", [BF16])
        dW["mlp_w1"][i] = _mm(f"l{i}_d_mlp_w1", sv["x1b"], dhpre, "tn", [BF16])
        mlp_after = ()
        if i == 0:
            early = [(n, 0) for n in ("ple_w", "ple_gate_w", "mlp_w2", "mlp_w1")]
            exchange_early0 = start_exchange("early0", early, [dW[n][l] for n, l in early], ())
            mlp_after = (exchange_early0[1][4],)
        dz1, dz1b, dg, db = _mm(f"l{i}_d_x1_ln", dhpre, W["mlp_w1"], "nt", b_lead=i,
                                extras=[dz2, sv["z1"], ln_mix_g[i][None]],
                                epi=lambda acc, d, z, g: _ln_grad(acc + alpha * d, z, g), after=mlp_after, **ln_back)
        dsmall["ln_mix_g"][i], dsmall["ln_mix_b"][i] = dg[0], db[0]
        if i % 2 == 0:
            dW["lru_w_out"][s] = _mm(f"l{i}_d_lru_w_out", sv["gh"], dz1b, "tn", [BF16])
            dgh = _mm(f"l{i}_d_gh", dz1b, W["lru_w_out"], "nt", [F32], b_lead=s)
            dup, dy, dwa, dwx, dpar = _lru_bwd(f"l{i}_lru_core_bwd", sv["proj"], sv["h"], dgh, sv["gates"],
                                               lru_par[s], wa_b[s], wx_b[s])
            dsmall["lru_wa"][s], dsmall["lru_wx"][s] = dwa, dwx
            dsmall["lru_conv_w"][s] = dpar[0:4]
            for k, n in enumerate(("lru_conv_b", "lru_ba", "lru_bx", "lru_lambda")):
                dsmall[n][s] = dpar[4 + k]
            dmix_in = jnp.concatenate([dup, dy], axis=1)
            win = "lru_w_in"
        else:
            dW["pool_w_out"][s] = _mm(f"l{i}_d_pool_w_out", sv["zs"], dz1b, "tn", [BF16])
            dzs = _mm(f"l{i}_d_zs", dz1b, W["pool_w_out"], "nt", [F32], b_lead=s)
            dmix_in, dW["pool_w_grp"][s], dpar = _pool_bwd(f"l{i}_pool_core_bwd", sv["pooled"], sv["z"], dzs,
                                                          W["pool_w_grp"][s], pool_par[s])
            dsmall["pool_b_grp"][s], dsmall["pool_scale"][s] = dpar[0], dpar[1]
            win = "pool_w_in"
        dW[win][s] = _mm(f"l{i}_d_{win}", sv["x0b"], dmix_in, "tn", [BF16])
        x0_after = ()
        if i > 0:
            keys = layer_weights(i)
            exchange_pending[i] = start_exchange(f"l{i}", keys, [dW[n][l] for n, l in keys], ())
            exchange_token = (exchange_pending[i][1][4],)
        else:
            small_grads = [jnp.stack(dsmall[n]) for n in small_names]
            small_shapes = [g.shape for g in small_grads]
            packed_g = _pack(small_grads)
            assert packed_g.shape[0] % (N_DEV * SUBLANES) == 0, packed_g.shape
            late = [("lru_w_out", 0), ("lru_w_in", 0), ("small", 0)]
            exchange_late0 = start_exchange("late0", late, [dW["lru_w_out"][0], dW["lru_w_in"][0], packed_g], ())
            x0_after = (exchange_late0[1][4],)
        if i > 0:
            def x0_epi(acc, d, gpre, pp):
                dxv = acc + alpha * d
                return (dxv,) + _ple_grad(dxv, gpre, pp)

            dx, dpp, dgpre, dbg = _mm(f"l{i}_d_x0_ple", dmix_in, W[win], "nt", [F32, BF16, BF16], b_lead=s,
                                      extras=[dz1, saved[i - 1]["gpre"], saved[i - 1]["pp"]], epi=x0_epi,
                                      tm=MM_TM_ROWS, tn=D, n_sums=1)
        else:
            dx = _mm(f"l{i}_d_x0", dmix_in, W[win], "nt", [F32], b_lead=s, extras=[dz1],
                     epi=lambda acc, d: (acc + alpha * d,), after=x0_after)
    grad_x = dx.reshape(x.shape)

    for i in range(1, depth):
        finish_exchange(f"l{i}", exchange_pending[i], x0_after[0])
    stacked = {n: None for n in BIG}
    layer0 = layer_weights(0)

    def adamw(n, l, after=()):
        own = dict(own=own_grad[(n, l)], own_axis=axis_of((n, l)), me1=me1) if reads_own_block((n, l)) else {}
        stacked[n] = _adamw_layer(f"adamw_{n}_{l}", partial[(n, l)], A[n], A["m_" + n], A["v_" + n], l, stacked[n],
                                  after=after, **own)

    for n in BIG:
        for l in reversed(range(A[n].shape[0])):
            if (n, l) not in layer0:
                adamw(n, l)
    behind = [dx] + [stacked[n][0] for n in BIG if stacked[n] is not None]
    finish_exchange("early0", exchange_early0, behind)
    finish_exchange("late0", exchange_late0, behind)
    red = _sum8("sum_small", partial[("small", 0)])
    rows = red.shape[0]
    red_land = lax.dynamic_update_slice_in_dim(lax.empty((N_DEV * rows, LANES), F32), red, me * rows, 0)
    small_handle = _split_start("gather_small_start", "gather", [], [red_land], [0], [rows])
    for n, l in layer0:
        adamw(n, l, after=(small_handle[4],))
    outs = {n: [o.reshape(A[n].shape) for o in stacked[n]] for n in BIG}
    red_full = _split_wait("gather_small_wait", "gather", small_handle, [0], [rows],
                           [stacked[n][0] for n, _ in layer0])[1][0]
    small_g = dict(zip(small_names, _unpack(red_full, small_shapes)))
    for n in SMALL_SHARDED:
        width = A[n].shape[-1]
        small_g[n] = lax.dynamic_slice_in_dim(small_g[n], me * width, width, axis=small_g[n].ndim - 1)
    for n in small_names:
        outs[n] = [small_g[n]] + _adamw_small(f"adamw_{n}", A[n], small_g[n], A["m_" + n], A["v_" + n])

    return (loss, grad_x, *[outs[n][0] for n in WEIGHTS], *[outs[n][1] for n in WEIGHTS],
            *[outs[n][2] for n in WEIGHTS], *[outs[n][3] for n in WEIGHTS])
```

```python
import functools
import math

import jax
import jax.numpy as jnp
from jax import lax
from jax.experimental import pallas as pl
from jax.experimental.pallas import tpu as pltpu

F32 = jnp.float32
BF16 = jnp.bfloat16
MESH = pl.DeviceIdType.MESH
N_DEV = 8
LANES = 128
SUBLANES = 8

LN_EPS = 1e-5
LRU_C = 8.0
CONV_WIDTH = 4
POOL_HALO = 16
ADAM_LR = 0.001
ADAM_B1 = 0.9
ADAM_B2 = 0.999
ADAM_EPS = 1e-08
ADAM_WD = 0.01
ADAM_STEP = 10

VMEM_LIMIT = 48 * 1024 * 1024
VMEM_LIMIT_SEQ = 56 * 1024 * 1024
SEQ_CHUNK = 1024
MM_TK = 4096
MM_TK_TOKENS = 4096
MM_TM_ROWS = 512
MM_TN = 512
MM_TN_WIDE = 1024
MM_TN_WIDE_MAX_K = 2048
GELU_C0 = math.sqrt(2.0 / math.pi)
GELU_C1 = 0.044715


def _in_hbm(*arrays):
    return [pltpu.with_memory_space_constraint(a, pltpu.HBM) for a in arrays]


def _cparams(*sem, limit=None):
    return pltpu.CompilerParams(dimension_semantics=tuple(sem) if sem else None,
                                vmem_limit_bytes=VMEM_LIMIT if limit is None else limit)


def _tile(n, pref):
    if n <= pref:
        return n
    t = pref - pref % LANES
    while t >= LANES:
        if n % t == 0:
            return t
        t -= LANES
    return n


def _row_tile(n, pref):
    if n <= pref:
        return n
    t = pref - pref % SUBLANES
    while t >= SUBLANES:
        if n % t == 0:
            return t
        t -= SUBLANES
    return n


def _mm(name, a, b, mode, out_dtypes, epi=None, extras=(), a_lead=None, b_lead=None, tm=1024, tn=None, tk=None,
        after=(), n_sums=0):
    if isinstance(b, (list, tuple)):
        b, b_lead = b[b_lead], None
    a2 = a.shape[-2:]
    b2 = b.shape[-2:]
    if mode == "nn":
        (M, K), N = a2, b2[1]
        assert b2[0] == K
    elif mode == "nt":
        (M, K), N = a2, b2[0]
        assert b2[1] == K
    else:
        (K, M), N = a2, b2[1]
        assert b2[0] == K
    if tk is None:
        tk = MM_TK_TOKENS if mode == "tn" else MM_TK
    if tn is None:
        tn = MM_TN_WIDE if (mode != "tn" and K <= MM_TN_WIDE_MAX_K) else MM_TN
    tm, tn, tk = _tile(M, tm), _tile(N, tn), _tile(K, tk)
    nk = K // tk
    n_extra = len(extras)
    n_out = len(out_dtypes)
    assert n_sums == 0 or tn == N
    resident = {"pipeline_mode": pl.Buffered(1)} if (tn == N and nk == 1) else {}

    def lead(shape, idx, which, **kw):
        if which is None:
            return pl.BlockSpec(shape, idx, **kw)
        return pl.BlockSpec((None,) + shape, lambda i, j, k: (which,) + idx(i, j, k), **kw)

    if mode == "nn":
        a_spec = lead((tm, tk), lambda i, j, k: (i, k), a_lead)
        b_spec = lead((tk, tn), lambda i, j, k: (k, j), b_lead, **resident)
        dims = (((1,), (0,)), ((), ()))
    elif mode == "nt":
        a_spec = lead((tm, tk), lambda i, j, k: (i, k), a_lead)
        b_spec = lead((tn, tk), lambda i, j, k: (j, k), b_lead, **resident)
        dims = (((1,), (1,)), ((), ()))
    else:
        a_spec = lead((tk, tm), lambda i, j, k: (k, i), a_lead)
        b_spec = lead((tk, tn), lambda i, j, k: (k, j), b_lead, **resident)
        dims = (((0,), (0,)), ((), ()))
    e_specs = []
    for e in extras:
        if e.shape[0] == 1:
            e_specs.append(pl.BlockSpec((1, tn), lambda i, j, k: (0, j)))
        else:
            e_specs.append(pl.BlockSpec((tm, tn), lambda i, j, k: (i, j)))

    n_after = len(after)

    def body(a_ref, b_ref, *rest):
        e_refs = rest[:n_extra]
        rest = rest[:n_extra] + rest[n_extra + n_after:]
        o_refs = rest[n_extra:n_extra + n_out]
        s_refs = rest[n_extra + n_out:n_extra + n_out + n_sums]
        part = lax.dot_general(a_ref[...].astype(BF16), b_ref[...].astype(BF16), dims, preferred_element_type=F32)

        def finish(r):
            res = (r,) if epi is None else epi(r, *[e[...] for e in e_refs])
            for o, v in zip(o_refs, res[:n_out]):
                o[...] = v.astype(o.dtype)
            first = pl.program_id(0) == 0
            for sr, v in zip(s_refs, res[n_out:]):
                @pl.when(first)
                def _(sr=sr, v=v):
                    sr[...] = v

                @pl.when(jnp.logical_not(first))
                def _(sr=sr, v=v):
                    sr[...] += v

        if nk == 1:
            finish(part)
            return
        acc = rest[n_extra + n_out + n_sums]
        k = pl.program_id(2)

        @pl.when(k == 0)
        def _():
            acc[...] = part

        @pl.when(jnp.logical_and(k > 0, k < nk - 1))
        def _():
            acc[...] += part

        @pl.when(k == nk - 1)
        def _():
            finish(acc[...] + part)

    outs = pl.pallas_call(
        body,
        name=name,
        grid=(M // tm, N // tn, nk),
        in_specs=[a_spec, b_spec] + e_specs + [pl.BlockSpec(memory_space=pl.ANY)] * n_after,
        out_specs=[pl.BlockSpec((tm, tn), lambda i, j, k: (i, j)) for _ in out_dtypes]
        + [pl.BlockSpec((1, tn), lambda i, j, k: (0, 0))] * n_sums,
        out_shape=[jax.ShapeDtypeStruct((M, N), d) for d in out_dtypes] + [jax.ShapeDtypeStruct((1, N), F32)] * n_sums,
        scratch_shapes=[pltpu.VMEM((tm, tn), F32)] if nk > 1 else [],
        compiler_params=_cparams(*(("arbitrary",) * 3 if n_sums else ("parallel", "parallel", "arbitrary"))),
    )(*_in_hbm(a, b, *extras), *after)
    return outs[0] if n_out + n_sums == 1 else tuple(outs)


def _rowwise(name, fn, tiled, params, outs, accs=(), tm=256, after=()):
    S = tiled[0].shape[0]
    tm = _tile(S, tm)
    nt, npar, no = len(tiled), len(params), len(outs)
    n_after = len(after)

    def body(*refs):
        t_refs = refs[:nt]
        p_refs = refs[nt:nt + npar]
        refs = refs[nt + npar + n_after:]
        o_refs = refs[:no]
        a_refs = refs[no:]
        res = fn(*[r[...] for r in t_refs], *[r[...] for r in p_refs])
        for o, v in zip(o_refs, res[:no]):
            o[...] = v.astype(o.dtype)
        first = pl.program_id(0) == 0
        for ar, v in zip(a_refs, res[no:]):
            @pl.when(first)
            def _(ar=ar, v=v):
                ar[...] = v

            @pl.when(jnp.logical_not(first))
            def _(ar=ar, v=v):
                ar[...] += v

    full = lambda p: pl.BlockSpec(p.shape, lambda i, nd=p.ndim: (0,) * nd)
    res = pl.pallas_call(
        body,
        name=name,
        grid=(S // tm,),
        in_specs=[pl.BlockSpec((tm, t.shape[1]), lambda i: (i, 0)) for t in tiled] + [full(p) for p in params]
        + [pl.BlockSpec(memory_space=pl.ANY)] * n_after,
        out_specs=[pl.BlockSpec((tm, c), lambda i: (i, 0)) for c, _ in outs]
        + [pl.BlockSpec(s, lambda i, nd=len(s): (0,) * nd) for s in accs],
        out_shape=[jax.ShapeDtypeStruct((S, c), d) for c, d in outs] + [jax.ShapeDtypeStruct(s, F32) for s in accs],
        compiler_params=_cparams("arbitrary"),
    )(*_in_hbm(*tiled, *params), *after)
    return res


def _ln_stats(z):
    mu = jnp.mean(z, axis=-1, keepdims=True)
    zc = z - mu
    var = jnp.mean(zc * zc, axis=-1, keepdims=True)
    return zc, lax.rsqrt(var + LN_EPS)


def _ln_apply(z, g, b):
    zc, rstd = _ln_stats(z)
    y = zc * rstd * g + b
    return z, y, y


def _ln_grad(dy, z, g):
    zc, rstd = _ln_stats(z)
    xhat = zc * rstd
    dxh = dy * g
    m1 = jnp.mean(dxh, axis=-1, keepdims=True)
    m2 = jnp.mean(dxh * xhat, axis=-1, keepdims=True)
    dz = rstd * (dxh - m1 - xhat * m2)
    return dz, dz, jnp.sum(dy * xhat, axis=0, keepdims=True), jnp.sum(dy, axis=0, keepdims=True)


def _ple_grad(dx3, gpre, pp):
    gate = jax.nn.sigmoid(gpre)
    dgpre = dx3 * pp * gate * (1.0 - gate)
    return dx3 * gate, dgpre, jnp.sum(dgpre, axis=0, keepdims=True)


def _loss_and_grad(name, y, target, gpre, pp):
    d = y.shape[1]

    def fn(y, t, gpre, pp):
        err = y - t
        sq = jnp.sum(jnp.sum(err * err, axis=0, keepdims=True), axis=1, keepdims=True)
        dy = err * (1.0 / d)
        dpp, dgpre, dbg = _ple_grad(dy, gpre, pp)
        return dy, dpp, dgpre, jnp.broadcast_to(sq, (1, LANES)), dbg

    return _rowwise(name, fn, [y, target, gpre, pp], [], [(d, F32), (d, BF16), (d, BF16)], accs=[(1, LANES), (1, d)])


def _rows(shape):
    return lax.broadcasted_iota(jnp.int32, shape, 0)


def _gelu(y):
    t = jnp.tanh(GELU_C0 * (y + GELU_C1 * y * y * y))
    return 0.5 * y * (1.0 + t), t


def _gelu_grad(y, t):
    return 0.5 * (1.0 + t) + 0.5 * y * (1.0 - t * t) * GELU_C0 * (1.0 + 3.0 * GELU_C1 * y * y)


def _softplus(x):
    return jnp.maximum(x, 0.0) + jnp.log(1.0 + jnp.exp(-jnp.abs(x)))


def _conv_fwd(xs, cw, cb):
    n = xs.shape[0]
    u = cw[3:4] * xs
    for k in (1, 2, 3):
        u = u + cw[3 - k:4 - k] * pltpu.roll(xs, k, 0)
    del n
    return u[SUBLANES:] + cb


def _lru_gates(u, wa, wx, ba, bx, sp, grow):
    ub = u.astype(BF16)
    r = jax.nn.sigmoid(jnp.dot(ub, wa, preferred_element_type=F32) + ba)
    ig = jax.nn.sigmoid(jnp.dot(ub, wx, preferred_element_type=F32) + bx)
    log_a = (-LRU_C) * r * sp
    a = jnp.exp(log_a)
    mult = jnp.sqrt(jnp.tanh(-log_a) * (1.0 + a * a))
    mult = jnp.where(grow == 0, 1.0, mult)
    return ub, r, ig, a, mult


def _scan8_fwd(a, b):
    row = _rows(a.shape)
    for k in (1, 2, 4):
        m = row >= k
        b = jnp.where(m, a * pltpu.roll(b, k, 0) + b, b)
        a = jnp.where(m, a * pltpu.roll(a, k, 0), a)
    return a, b


def _scan8_bwd(c, d):
    row = _rows(c.shape)
    for k in (1, 2, 4):
        m = row < SUBLANES - k
        d = jnp.where(m, c * pltpu.roll(d, SUBLANES - k, 0) + d, d)
        c = jnp.where(m, c * pltpu.roll(c, SUBLANES - k, 0), c)
    return c, d


def _pad_copy(dst, src, front, back):
    s, c = src.shape
    if front:
        dst[pl.ds(0, front), :] = jnp.zeros((front, c), dst.dtype)
    if back:
        dst[pl.ds(front + s, back), :] = jnp.zeros((back, c), dst.dtype)
    dst[pl.ds(front, s), :] = src[...].astype(dst.dtype)


def _lru_fwd(name, proj, par, wa, wx):
    S = proj.shape[0]
    R = proj.shape[1] // 2
    H = R // LANES
    ch = _tile(S, SEQ_CHUNK)
    nch = S // ch
    H8 = SUBLANES

    def body(up_ref, y_ref, par_ref, wa_ref, wx_ref, gh_ref, h_ref, r_ref, ig_ref, a_ref, mult_ref, gy_ref, dgy_ref,
             u_ref, up_pad):
        _pad_copy(up_pad, up_ref, H8, 0)
        par = par_ref[...]
        cw, cb, ba, bx = par[0:4], par[4:5], par[5:6], par[6:7]
        sp = _softplus(-par[7:8])
        wa_m, wx_m = wa_ref[...], wx_ref[...]

        def chunk(ci, carry):
            r0 = pl.multiple_of(ci * ch, ch)
            xs = up_pad[pl.ds(r0, ch + H8), :]
            u = _conv_fwd(xs, cw, cb)
            grow = _rows(u.shape) + r0
            _, r, ig, a, mult = _lru_gates(u, wa_m, wx_m, ba, bx, sp, grow)
            for ref, val in ((r_ref, r), (ig_ref, ig), (a_ref, a), (mult_ref, mult), (u_ref, u)):
                ref[pl.ds(r0, ch), :] = val
            bt = mult * (ig * u)
            hs = []
            for j in range(ch // H8):
                aa, bb = _scan8_fwd(a[j * H8:(j + 1) * H8], bt[j * H8:(j + 1) * H8])
                hj = bb + aa * carry
                carry = hj[H8 - 1:H8]
                hs.append(hj)
            h = jnp.concatenate(hs, axis=0)
            h_ref[pl.ds(r0, ch), :] = h
            y = y_ref[pl.ds(r0, ch), :]
            gy, t = _gelu(y)
            gy_ref[pl.ds(r0, ch), :] = gy
            dgy_ref[pl.ds(r0, ch), :] = _gelu_grad(y, t)
            gh_ref[pl.ds(r0, ch), :] = (h * gy).astype(gh_ref.dtype)
            return carry

        lax.fori_loop(0, nch, chunk, jnp.zeros((1, LANES), F32))

    col = lambda off: pl.BlockSpec((S, LANES), lambda h: (0, h + off))
    return pl.pallas_call(
        body,
        name=name,
        grid=(H,),
        in_specs=[col(0), col(H), pl.BlockSpec((8, LANES), lambda h: (0, h)),
                  pl.BlockSpec((None, LANES, LANES), lambda h: (h, 0, 0)),
                  pl.BlockSpec((None, LANES, LANES), lambda h: (h, 0, 0))],
        out_specs=[col(0)] * 9,
        out_shape=[jax.ShapeDtypeStruct((S, R), BF16)] + [jax.ShapeDtypeStruct((S, R), F32)] * 8,
        scratch_shapes=[pltpu.VMEM((S + H8, LANES), F32)],
        compiler_params=_cparams("parallel"),
    )(proj, proj, par, wa, wx)


def _lru_bwd(name, proj, h, dgh, gates, par, wa, wx):
    S = proj.shape[0]
    R = proj.shape[1] // 2
    H = R // LANES
    ch = _tile(S, SEQ_CHUNK)
    nch = S // ch
    H8 = SUBLANES
    nb = ch // H8

    def body(up_ref, h_ref, dgh_ref, r_ref, ig_ref, a_ref, mult_ref, gy_ref, dgy_ref, u_ref, par_ref, wa_ref, wx_ref,
             dup_ref, dy_ref, dwa_ref, dwx_ref, dpar_ref, up_pad, du_pad, vec_acc):
        _pad_copy(up_pad, up_ref, H8, 0)
        du_pad[pl.ds(S, H8), :] = jnp.zeros((H8, LANES), F32)
        par = par_ref[...]
        cw, lam = par[0:4], par[7:8]
        sp = _softplus(-lam)
        wa_m, wx_m = wa_ref[...], wx_ref[...]
        dwa_ref[...] = jnp.zeros_like(dwa_ref)
        dwx_ref[...] = jnp.zeros_like(dwx_ref)
        vec_acc[...] = jnp.zeros_like(vec_acc)
        nt_dims = (((1,), (1,)), ((), ()))
        tn_dims = (((0,), (0,)), ((), ()))

        def chunk(it, carry):
            lam_next, a_next = carry
            ci = nch - 1 - it
            r0 = pl.multiple_of(ci * ch, ch)
            here = pl.ds(r0, ch)
            u = u_ref[here, :]
            row = _rows(u.shape)
            grow = row + r0
            ub = u.astype(BF16)
            r, ig, a, mult = r_ref[here, :], ig_ref[here, :], a_ref[here, :], mult_ref[here, :]
            hcur = h_ref[here, :]
            before = h_ref[pl.ds(pl.multiple_of(jnp.maximum(r0 - H8, 0), H8), H8), :][H8 - 1:H8]
            hprev = jnp.where(row == 0, jnp.where(ci > 0, before, 0.0), pltpu.roll(hcur, 1, 0))
            dgh = dgh_ref[here, :]
            dy_ref[here, :] = (dgh * hcur * dgy_ref[here, :]).astype(dy_ref.dtype)
            dh = dgh * gy_ref[here, :]
            c = jnp.where(row == ch - 1, a_next, pltpu.roll(a, ch - 1, 0))
            ls = [None] * nb
            for j in range(nb - 1, -1, -1):
                cc, dd = _scan8_bwd(c[j * H8:(j + 1) * H8], dh[j * H8:(j + 1) * H8])
                lj = dd + cc * lam_next
                lam_next = lj[0:1]
                ls[j] = lj
            lmb = jnp.concatenate(ls, axis=0)
            da = lmb * hprev
            gu = ig * u
            dmult = lmb * gu
            dlog_a = da * a + jnp.where(grow == 0, 0.0, dmult * (-(a * a) / mult))
            dr = dlog_a * ((-LRU_C) * sp)
            drp = dr * r * (1.0 - r)
            dip = (lmb * mult * u) * ig * (1.0 - ig)
            drb, dib = drp.astype(BF16), dip.astype(BF16)
            du = (lmb * mult * ig
                  + lax.dot_general(drb, wa_m, nt_dims, preferred_element_type=F32)
                  + lax.dot_general(dib, wx_m, nt_dims, preferred_element_type=F32))
            du_pad[pl.ds(r0, ch), :] = du
            dwa_ref[...] += lax.dot_general(ub, drb, tn_dims, preferred_element_type=F32)
            dwx_ref[...] += lax.dot_general(ub, dib, tn_dims, preferred_element_type=F32)
            ssum = lambda v: jnp.sum(v, axis=0, keepdims=True)
            vec_acc[0:1, :] += ssum(drp)
            vec_acc[1:2, :] += ssum(dip)
            vec_acc[2:3, :] += ssum(dlog_a * ((-LRU_C) * r))
            return lam_next, a[0:1]

        zero = jnp.zeros((1, LANES), F32)
        lax.fori_loop(0, nch, chunk, (zero, zero))

        def conv_chunk(ci, acc):
            r0 = pl.multiple_of(ci * ch, ch)
            ds = du_pad[pl.ds(r0, ch + H8), :]
            xs = up_pad[pl.ds(r0, ch + H8), :]
            n = ch + H8
            du = ds[:ch]
            dup = cw[3:4] * du
            new = [acc[3] + jnp.sum(du * xs[H8:], axis=0, keepdims=True)]
            for k in (1, 2, 3):
                dup = dup + cw[3 - k:4 - k] * pltpu.roll(ds, n - k, 0)[:ch]
                new.append(acc[3 - k] + jnp.sum(du * pltpu.roll(xs, k, 0)[H8:], axis=0, keepdims=True))
            dup_ref[pl.ds(r0, ch), :] = dup.astype(dup_ref.dtype)
            return (new[3], new[2], new[1], new[0], acc[4] + jnp.sum(du, axis=0, keepdims=True))

        acc = lax.fori_loop(0, nch, conv_chunk, (zero,) * 5)
        dlam = vec_acc[2:3, :] * (-jax.nn.sigmoid(-lam))
        dpar_ref[...] = jnp.concatenate(list(acc) + [vec_acc[0:1, :], vec_acc[1:2, :], dlam], axis=0)

    col = lambda off: pl.BlockSpec((S, LANES), lambda h: (0, h + off))
    head = pl.BlockSpec((None, LANES, LANES), lambda h: (h, 0, 0))
    return pl.pallas_call(
        body,
        name=name,
        grid=(H,),
        in_specs=[col(0)] * 10 + [pl.BlockSpec((8, LANES), lambda h: (0, h)), head, head],
        out_specs=[col(0), col(0), head, head, pl.BlockSpec((8, LANES), lambda h: (0, h))],
        out_shape=[jax.ShapeDtypeStruct((S, R), BF16), jax.ShapeDtypeStruct((S, R), BF16),
                   jax.ShapeDtypeStruct((H, LANES, LANES), F32), jax.ShapeDtypeStruct((H, LANES, LANES), F32),
                   jax.ShapeDtypeStruct((8, R), F32)],
        scratch_shapes=[pltpu.VMEM((S + H8, LANES), F32), pltpu.VMEM((S + H8, LANES), F32),
                        pltpu.VMEM((8, LANES), F32)],
        compiler_params=_cparams("parallel", limit=VMEM_LIMIT_SEQ),
    )(proj, h, dgh, *gates, par, wa, wx)


def _window_sum(xs, g, up):
    n = xs.shape[0]
    s = xs
    for lvl, k in enumerate((1, 2, 4, 8)):
        sh = pltpu.roll(s, (n - k) if up else k, 0)
        s = s + jnp.where(g >= lvl, sh, 0.0)
    return s


def _pool_count(grow, g):
    return jnp.minimum(grow + 1, lax.shift_left(jnp.int32(2), g)).astype(F32)


def _pool_fwd(name, u, wgrp, par):
    S, D = u.shape
    G, W = wgrp.shape[0], wgrp.shape[1]
    ch = _tile(S, SEQ_CHUNK)
    nch = S // ch
    PH = POOL_HALO

    def body(u_ref, w_ref, par_ref, zs_ref, pooled_ref, z_ref, u_pad):
        g = pl.program_id(0)
        _pad_copy(u_pad, u_ref, PH, 0)
        par = par_ref[...]
        w = w_ref[...]

        def chunk(ci, _):
            r0 = pl.multiple_of(ci * ch, ch)
            xs = u_pad[pl.ds(r0, ch + PH), :]
            ws = _window_sum(xs, g, False)[PH:]
            uc = xs[PH:]
            cnt = _pool_count(_rows(uc.shape) + r0, g)
            pooled = (ws / cnt - uc).astype(BF16)
            z = jnp.dot(pooled, w, preferred_element_type=F32) + par[0:1]
            pooled_ref[pl.ds(r0, ch), :] = pooled
            z_ref[pl.ds(r0, ch), :] = z
            zs_ref[pl.ds(r0, ch), :] = (z * par[1:2]).astype(zs_ref.dtype)
            return 0

        lax.fori_loop(0, nch, chunk, 0)

    blk = pl.BlockSpec((S, W), lambda g: (0, g))
    return pl.pallas_call(
        body,
        name=name,
        grid=(G,),
        in_specs=[blk, pl.BlockSpec((None, W, W), lambda g: (g, 0, 0)), pl.BlockSpec((2, W), lambda g: (0, g))],
        out_specs=[blk, blk, blk],
        out_shape=[jax.ShapeDtypeStruct((S, D), BF16), jax.ShapeDtypeStruct((S, D), BF16),
                   jax.ShapeDtypeStruct((S, D), F32)],
        scratch_shapes=[pltpu.VMEM((S + PH, W), F32)],
        compiler_params=_cparams("parallel"),
    )(u, wgrp, par)


def _pool_bwd(name, pooled, z, dzs, wgrp, par):
    S, D = z.shape
    G, W = wgrp.shape[0], wgrp.shape[1]
    ch = _tile(S, SEQ_CHUNK)
    nch = S // ch
    PH = POOL_HALO

    def body(pooled_ref, z_ref, dzs_ref, w_ref, par_ref, du_ref, dw_ref, dpar_ref, q_pad, dw_acc):
        g = pl.program_id(0)
        q_pad[pl.ds(S, PH), :] = jnp.zeros((PH, W), F32)
        par = par_ref[...]
        w = w_ref[...]
        dw_acc[...] = jnp.zeros_like(dw_acc)

        def chunk(ci, acc):
            db, dsc = acc
            r0 = pl.multiple_of(ci * ch, ch)
            cnt = _pool_count(_rows((ch, W)) + r0, g)
            pooled = pooled_ref[pl.ds(r0, ch), :]
            z = z_ref[pl.ds(r0, ch), :]
            dzs = dzs_ref[pl.ds(r0, ch), :]
            dz = dzs * par[1:2]
            dzb = dz.astype(BF16)
            dw_acc[...] += lax.dot_general(pooled, dzb, (((0,), (0,)), ((), ())), preferred_element_type=F32)
            dpooled = lax.dot_general(dzb, w, (((1,), (1,)), ((), ())), preferred_element_type=F32)
            q_pad[pl.ds(r0, ch), :] = dpooled / cnt
            return (db + jnp.sum(dz, axis=0, keepdims=True), dsc + jnp.sum(dzs * z, axis=0, keepdims=True))

        zero = jnp.zeros((1, W), F32)
        db, dsc = lax.fori_loop(0, nch, chunk, (zero, zero))
        dpar_ref[...] = jnp.concatenate([db, dsc], axis=0)
        dw_ref[...] = dw_acc[...].astype(dw_ref.dtype)

        def back(ci, _):
            r0 = pl.multiple_of(ci * ch, ch)
            qs = q_pad[pl.ds(r0, ch + PH), :]
            ws = _window_sum(qs, g, True)[:ch]
            qc = qs[:ch]
            cnt = _pool_count(_rows(qc.shape) + r0, g)
            du_ref[pl.ds(r0, ch), :] = (ws - qc * cnt).astype(du_ref.dtype)
            return 0

        lax.fori_loop(0, nch, back, 0)

    blk = pl.BlockSpec((S, W), lambda g: (0, g))
    wspec = pl.BlockSpec((None, W, W), lambda g: (g, 0, 0))
    pspec = pl.BlockSpec((2, W), lambda g: (0, g))
    return pl.pallas_call(
        body,
        name=name,
        grid=(G,),
        in_specs=[blk, blk, blk, wspec, pspec],
        out_specs=[blk, wspec, pspec],
        out_shape=[jax.ShapeDtypeStruct((S, D), BF16), jax.ShapeDtypeStruct((G, W, W), BF16),
                   jax.ShapeDtypeStruct((2, D), F32)],
        scratch_shapes=[pltpu.VMEM((S + PH, W), F32), pltpu.VMEM((W, W), F32)],
        compiler_params=_cparams("parallel"),
    )(pooled, z, dzs, wgrp, par)


def _my_place():
    x, y, c = lax.axis_index("x"), lax.axis_index("y"), lax.axis_index("c")
    return x, y, c, 4 * x + 2 * y + c


def _peers(x, y, c):
    out = []
    for d in range(1, N_DEV):
        px = 1 - x if d & 4 else x
        py = 1 - y if d & 2 else y
        pc = 1 - c if d & 1 else c
        out.append(((px, py, pc), 4 * px + 2 * py + pc))
    return out


def _window(ref, axis, start, size):
    idx = [slice(None)] * len(ref.shape)
    idx[axis] = pl.ds(start, size)
    return ref.at[tuple(idx)]


def _to_bf16(name, arrs, after=()):
    outs = []
    for i, a in enumerate(arrs):
        a2 = a.reshape(-1, a.shape[-1])
        tr = _tile(a2.shape[0], 512)
        o = pl.pallas_call(
            lambda a_ref, *rest: rest[-1].__setitem__(Ellipsis, a_ref[...].astype(BF16)),
            name=f"{name}_{i}",
            grid=(a2.shape[0] // tr,),
            in_specs=[pl.BlockSpec((tr, a2.shape[1]), lambda r: (r, 0))] + [pl.BlockSpec(memory_space=pl.ANY)] * len(after),
            out_specs=pl.BlockSpec((tr, a2.shape[1]), lambda r: (r, 0)),
            out_shape=jax.ShapeDtypeStruct(a2.shape, BF16),
            compiler_params=_cparams("parallel"),
        )(a2, *after)
        outs.append(o.reshape(a.shape))
    return outs


def _all_gather(name, shards, axes):
    n = len(shards)
    sizes = [s.shape[ax] for s, ax in zip(shards, axes)]

    def body(*refs):
        ins, outs = refs[:n], refs[n:2 * n]
        send, recv, loc = refs[2 * n:]
        x, y, c, me = _my_place()
        peers = _peers(x, y, c)
        local = []
        for i in range(n):
            dst = _window(outs[i], axes[i], me * sizes[i], sizes[i])
            cp = pltpu.make_async_copy(ins[i], dst, loc.at[i])
            cp.start()
            local.append(cp)
            for peer, _ in peers:
                pltpu.make_async_remote_copy(src_ref=ins[i], dst_ref=dst, send_sem=send.at[i], recv_sem=recv.at[i],
                                             device_id=peer, device_id_type=MESH).start()
        for i in range(n):
            local[i].wait()
            seven = _window(outs[i], axes[i], 0, (N_DEV - 1) * sizes[i])
            pltpu.make_async_remote_copy(src_ref=seven, dst_ref=seven, send_sem=send.at[i], recv_sem=recv.at[i],
                                         device_id=(x, y, c), device_id_type=MESH).wait()

    def full_shape(s, ax):
        shp = list(s.shape)
        shp[ax] *= N_DEV
        return jax.ShapeDtypeStruct(tuple(shp), s.dtype)

    any_spec = pl.BlockSpec(memory_space=pl.ANY)
    return pl.pallas_call(
        body,
        name=name,
        in_specs=[any_spec] * n,
        out_specs=[any_spec] * n,
        out_shape=[full_shape(s, ax) for s, ax in zip(shards, axes)],
        scratch_shapes=[pltpu.SemaphoreType.DMA((n,)), pltpu.SemaphoreType.DMA((n,)), pltpu.SemaphoreType.DMA((n,))],
        compiler_params=pltpu.CompilerParams(has_side_effects=True),
    )(*shards)


HBM_SPEC = pl.BlockSpec(memory_space=pltpu.HBM)
SEM_SPEC = pl.BlockSpec(memory_space=pltpu.SEMAPHORE)
SPLIT_EFFECT = pltpu.SideEffectType.DATAFLOW_SIDE_EFFECTING


def _push_all(kind, src, dst, axis, size, send_sem, recv_sem, place):
    x, y, c, me = place
    for peer, pidx in _peers(x, y, c):
        if kind == "gather":
            s = d = _window(dst, axis, me * size, size)
        else:
            s, d = _window(src, axis, pidx * size, size), dst.at[me]
        pltpu.make_async_remote_copy(src_ref=s, dst_ref=d, send_sem=send_sem, recv_sem=recv_sem, device_id=peer,
                                     device_id_type=MESH).start()


def _drain_all(kind, dst, axis, size, send_sem, recv_sem, place):
    x, y, c, _ = place
    seven = _window(dst, axis, 0, (N_DEV - 1) * size) if kind == "gather" else dst.at[pl.ds(0, N_DEV - 1)]
    pltpu.make_async_remote_copy(src_ref=seven, dst_ref=seven, send_sem=send_sem, recv_sem=recv_sem,
                                 device_id=(x, y, c), device_id_type=MESH).wait()


def _own_block_placed(src, axis, size, me):
    own = lax.dynamic_slice_in_dim(src, me * size, size, axis)
    return lax.dynamic_update_slice_in_dim(lax.empty((N_DEV,) + own.shape, src.dtype), own[None], me, 0)


def _split_start(name, kind, srcs, lands, axes, sizes, after=()):
    n, ns, na = len(lands), len(srcs), len(after)

    def body(*refs):
        src_refs, land_refs = refs[:ns], refs[ns:ns + n]
        send, recv = refs[ns + n + na], refs[ns + n + na + 1]
        token = refs[-1]
        place = _my_place()
        for k in range(n):
            _push_all(kind, src_refs[k] if ns else None, land_refs[k], axes[k], sizes[k], send.at[k], recv.at[k], place)
        token[...] = jnp.zeros_like(token)

    hbm = lambda a: pltpu.HBM(a.shape, a.dtype)
    res = pl.pallas_call(
        body,
        name=name,
        out_shape=(pltpu.SemaphoreType.DMA((n,)), pltpu.SemaphoreType.DMA((n,)), *[hbm(a) for a in srcs],
                   *[hbm(a) for a in lands], jax.ShapeDtypeStruct((SUBLANES, LANES), F32)),
        in_specs=[HBM_SPEC] * (ns + n) + [pl.BlockSpec(memory_space=pl.ANY)] * na,
        out_specs=(SEM_SPEC, SEM_SPEC, *[HBM_SPEC] * (ns + n), pl.BlockSpec(memory_space=pltpu.VMEM)),
        input_output_aliases={k: 2 + k for k in range(ns + n)},
        compiler_params=pltpu.CompilerParams(has_side_effects=SPLIT_EFFECT),
    )(*[pltpu.with_memory_space_constraint(a, pltpu.HBM) for a in (*srcs, *lands)], *after)
    return res[0], res[1], list(res[2:2 + ns]), list(res[2 + ns:2 + ns + n]), res[-1]


def _split_wait(name, kind, handle, axes, sizes, after):
    send, recv, srcs, lands, _ = handle
    n, ns = len(lands), len(srcs)
    after = list(after) if isinstance(after, (list, tuple)) else [after]

    def body(*refs):
        land_refs = refs[ns:ns + n]
        send_ref, recv_ref = refs[ns + n], refs[ns + n + 1]
        place = _my_place()
        for k in range(n):
            _drain_all(kind, land_refs[k], axes[k], sizes[k], send_ref.at[k], recv_ref.at[k], place)

    hbm = lambda a: pltpu.HBM(a.shape, a.dtype)
    res = pl.pallas_call(
        body,
        name=name,
        out_shape=tuple(hbm(a) for a in (*srcs, *lands)),
        in_specs=[HBM_SPEC] * (ns + n) + [SEM_SPEC, SEM_SPEC] + [pl.BlockSpec(memory_space=pl.ANY)] * len(after),
        out_specs=tuple([HBM_SPEC] * (ns + n)),
        input_output_aliases={k: k for k in range(ns + n)},
        compiler_params=pltpu.CompilerParams(has_side_effects=SPLIT_EFFECT),
    )(*srcs, *lands, send, recv, *after)
    return list(res[:ns]), list(res[ns:])


def _cast_into_window(name, a, l, axis, me1, after=()):
    shp = a.shape[1:]
    cast = lambda me_ref, a_ref, *rest: rest[-1].__setitem__(Ellipsis, a_ref[...].astype(BF16))
    if len(shp) == 3:
        assert axis == 1
        G, r, c = shp
        full = (G, r * N_DEV, c)
        grid = (G,)
        in_spec = pl.BlockSpec((None, None, r, c), lambda g, me: (l, g, 0, 0))
        out_spec = pl.BlockSpec((None, r, c), lambda g, me: (g, me[0], 0))
    else:
        r, c = shp
        tr = _tile(r, 512)
        nb = r // tr
        grid = (nb,)
        in_spec = pl.BlockSpec((None, tr, c), lambda i, me: (l, i, 0))
        if axis == 0:
            full = (r * N_DEV, c)
            out_spec = pl.BlockSpec((tr, c), lambda i, me: (me[0] * nb + i, 0))
        else:
            full = (r, c * N_DEV)
            out_spec = pl.BlockSpec((tr, c), lambda i, me: (i, me[0]))
    return pl.pallas_call(
        cast,
        name=name,
        grid_spec=pltpu.PrefetchScalarGridSpec(
            num_scalar_prefetch=1, grid=grid,
            in_specs=[in_spec] + [pl.BlockSpec(memory_space=pl.ANY)] * len(after), out_specs=out_spec),
        out_shape=jax.ShapeDtypeStruct(full, BF16),
        compiler_params=_cparams("arbitrary"),
    )(me1, a, *after)


def _adamw_math(w, g, m, v):
    m = ADAM_B1 * m + (1.0 - ADAM_B1) * g
    v = ADAM_B2 * v + (1.0 - ADAM_B2) * jnp.square(g)
    m_hat = m / (1.0 - ADAM_B1 ** ADAM_STEP)
    v_hat = v / (1.0 - ADAM_B2 ** ADAM_STEP)
    delta = -ADAM_LR * (m_hat / (jnp.sqrt(v_hat) + ADAM_EPS) + ADAM_WD * w)
    return delta, m, v


def _sum_slots(buf_ref):
    g = buf_ref[0].astype(F32)
    for s in range(1, N_DEV):
        g = g + buf_ref[s].astype(F32)
    return g


def _adamw_layer(name, buf, w, m, v, l, prev, after=(), own=None, own_axis=0, me1=None):
    shape = w.shape
    L, C = shape[0], shape[-1]
    Rr = math.prod(shape[1:-1])
    buf3 = buf.reshape(N_DEV, Rr, C)
    w3, m3, v3 = (t.reshape(L, Rr, C) for t in (w, m, v))
    tr = _tile(Rr, 2 * LANES) if Rr % LANES == 0 else Rr
    nb = Rr // tr
    n_pass = (0 if prev is None else 4) + len(after)
    n_lead = 1 if own is None else 3

    def body(*refs):
        buf_ref = refs[0] if own is None else refs[1]
        w_ref, m_ref, v_ref = refs[n_lead], refs[n_lead + 1], refs[n_lead + 2]
        g_out, d_out, m_out, v_out = refs[n_lead + 3 + n_pass:]
        if own is None:
            g = _sum_slots(buf_ref)
        else:
            me_ref, mine = refs[0], refs[2][...].astype(F32)
            g = None
            for s in range(N_DEV):
                term = jnp.where(me_ref[0] == s, mine, buf_ref[s].astype(F32))
                g = term if g is None else g + term
        d, mm, vv = _adamw_math(w_ref[...], g, m_ref[...], v_ref[...])
        g_out[...] = g
        d_out[...] = d
        m_out[...] = mm
        v_out[...] = vv

    out_shape = [jax.ShapeDtypeStruct((L, Rr, C), F32)] * 4
    passed = [pl.BlockSpec(memory_space=pl.ANY)] * n_pass
    aliases = {} if prev is None else {n_lead + 3 + k: k for k in range(4)}
    if own is None:
        spec = pl.BlockSpec((None, tr, C), lambda r: (l, r, 0))
        outs = pl.pallas_call(
            body, name=name, grid=(nb,),
            in_specs=[pl.BlockSpec((N_DEV, tr, C), lambda r: (0, r, 0)), spec, spec, spec] + passed,
            out_specs=[spec] * 4, out_shape=out_shape, input_output_aliases=aliases,
            compiler_params=_cparams("parallel"),
        )(buf3, w3, m3, v3, *(prev or ()), *after)
    else:
        spec = pl.BlockSpec((None, tr, C), lambda r, me: (l, r, 0))
        own_idx = (lambda r, me: (me[0] * nb + r, 0)) if own_axis == 0 else (lambda r, me: (r, me[0]))
        outs = pl.pallas_call(
            body, name=name,
            grid_spec=pltpu.PrefetchScalarGridSpec(
                num_scalar_prefetch=1, grid=(nb,),
                in_specs=[pl.BlockSpec((N_DEV, tr, C), lambda r, me: (0, r, 0)), pl.BlockSpec((tr, C), own_idx),
                          spec, spec, spec] + passed,
                out_specs=[spec] * 4),
            out_shape=out_shape, input_output_aliases=aliases,
            compiler_params=_cparams("arbitrary"),
        )(me1, buf3, own, w3, m3, v3, *(prev or ()), *after)
    return list(outs)


def _sum8(name, buf):
    R = buf.shape[1]

    def body(buf_ref, o_ref):
        o_ref[...] = _sum_slots(buf_ref)

    return pl.pallas_call(
        body,
        name=name,
        in_specs=[pl.BlockSpec(buf.shape, lambda: (0, 0, 0))],
        out_specs=pl.BlockSpec((R, LANES), lambda: (0, 0)),
        out_shape=jax.ShapeDtypeStruct((R, LANES), F32),
        compiler_params=_cparams(),
    )(buf)


def _adamw_small(name, w, g, m, v):
    shape = w.shape
    w2, g2, m2, v2 = (t.reshape(-1, shape[-1]) for t in (w, g, m, v))
    R, C = w2.shape
    tr = _row_tile(R, 512)

    def body(w_ref, g_ref, m_ref, v_ref, d_out, m_out, v_out):
        d, mm, vv = _adamw_math(w_ref[...], g_ref[...], m_ref[...], v_ref[...])
        d_out[...] = d
        m_out[...] = mm
        v_out[...] = vv

    spec = pl.BlockSpec((tr, C), lambda r: (r, 0))
    outs = pl.pallas_call(
        body,
        name=name,
        grid=(R // tr,),
        in_specs=[spec] * 4,
        out_specs=[spec] * 3,
        out_shape=[jax.ShapeDtypeStruct((R, C), F32)] * 3,
        compiler_params=_cparams("parallel"),
    )(w2, g2, m2, v2)
    return [o.reshape(shape) for o in outs]


def _pack(arrs, pad_rows_to=SUBLANES):
    parts = []
    for a in arrs:
        flat = a.reshape(-1)
        per = LANES * pad_rows_to
        padded = -(-flat.shape[0] // per) * per
        if padded != flat.shape[0]:
            flat = jnp.pad(flat, (0, padded - flat.shape[0]))
        parts.append(flat.reshape(-1, LANES))
    return jnp.concatenate(parts, axis=0)


def _unpack(packed, shapes, pad_rows_to=SUBLANES):
    out = []
    r = 0
    for shp in shapes:
        nel = math.prod(shp)
        per = LANES * pad_rows_to
        rows = -(-nel // per) * pad_rows_to
        out.append(packed[r:r + rows].reshape(-1)[:nel].reshape(shp))
        r += rows
    return out


BIG = ("lru_w_in", "lru_w_out", "pool_w_in", "pool_w_grp", "pool_w_out", "mlp_w1", "mlp_w2", "ple_w", "ple_gate_w")
BIG_AXIS = {"lru_w_in": 2, "lru_w_out": 1, "pool_w_in": 1, "pool_w_grp": 2, "pool_w_out": 1, "mlp_w1": 2,
            "mlp_w2": 1, "ple_w": 2, "ple_gate_w": 1}
SMALL_SHARDED = ("lru_conv_w", "pool_b_grp", "pool_scale")
REPLICATED = ("lru_conv_b", "lru_wa", "lru_ba", "lru_wx", "lru_bx", "lru_lambda", "ln_mix_g", "ln_mix_b",
              "ln_mlp_g", "ln_mlp_b", "ple_gate_b")
WEIGHTS = ("lru_w_in", "lru_conv_w", "lru_conv_b", "lru_wa", "lru_ba", "lru_wx", "lru_bx", "lru_lambda", "lru_w_out",
           "pool_w_in", "pool_w_grp", "pool_b_grp", "pool_scale", "pool_w_out", "ln_mix_g", "ln_mix_b", "mlp_w1",
           "mlp_w2", "ln_mlp_g", "ln_mlp_b", "ple_w", "ple_gate_w", "ple_gate_b")
INPUTS = ("x", "p") + WEIGHTS + ("loss_target",) + tuple("m_" + n for n in WEIGHTS) + tuple("v_" + n for n in WEIGHTS)


def _gather_last_axis(packed_full, shard_shape):
    nel = math.prod(shard_shape)
    blocks = packed_full.reshape(N_DEV, -1)[:, :nel].reshape((N_DEV,) + tuple(shard_shape))
    return jnp.concatenate([blocks[d] for d in range(N_DEV)], axis=-1)


def kernel(x, p, lru_w_in, lru_conv_w, lru_conv_b, lru_wa, lru_ba, lru_wx, lru_bx, lru_lambda, lru_w_out, pool_w_in, pool_w_grp, pool_b_grp, pool_scale, pool_w_out, ln_mix_g, ln_mix_b, mlp_w1, mlp_w2, ln_mlp_g, ln_mlp_b, ple_w, ple_gate_w, ple_gate_b, loss_target, m_lru_w_in, m_lru_conv_w, m_lru_conv_b, m_lru_wa, m_lru_ba, m_lru_wx, m_lru_bx, m_lru_lambda, m_lru_w_out, m_pool_w_in, m_pool_w_grp, m_pool_b_grp, m_pool_scale, m_pool_w_out, m_ln_mix_g, m_ln_mix_b, m_mlp_w1, m_mlp_w2, m_ln_mlp_g, m_ln_mlp_b, m_ple_w, m_ple_gate_w, m_ple_gate_b, v_lru_w_in, v_lru_conv_w, v_lru_conv_b, v_lru_wa, v_lru_ba, v_lru_wx, v_lru_bx, v_lru_lambda, v_lru_w_out, v_pool_w_in, v_pool_w_grp, v_pool_b_grp, v_pool_scale, v_pool_w_out, v_ln_mix_g, v_ln_mix_b, v_mlp_w1, v_mlp_w2, v_ln_mlp_g, v_ln_mlp_b, v_ple_w, v_ple_gate_w, v_ple_gate_b):
    A = dict(zip(INPUTS, (x, p, lru_w_in, lru_conv_w, lru_conv_b, lru_wa, lru_ba, lru_wx, lru_bx, lru_lambda, lru_w_out, pool_w_in, pool_w_grp, pool_b_grp, pool_scale, pool_w_out, ln_mix_g, ln_mix_b, mlp_w1, mlp_w2, ln_mlp_g, ln_mlp_b, ple_w, ple_gate_w, ple_gate_b, loss_target, m_lru_w_in, m_lru_conv_w, m_lru_conv_b, m_lru_wa, m_lru_ba, m_lru_wx, m_lru_bx, m_lru_lambda, m_lru_w_out, m_pool_w_in, m_pool_w_grp, m_pool_b_grp, m_pool_scale, m_pool_w_out, m_ln_mix_g, m_ln_mix_b, m_mlp_w1, m_mlp_w2, m_ln_mlp_g, m_ln_mlp_b, m_ple_w, m_ple_gate_w, m_ple_gate_b, v_lru_w_in, v_lru_conv_w, v_lru_conv_b, v_lru_wa, v_lru_ba, v_lru_wx, v_lru_bx, v_lru_lambda, v_lru_w_out, v_pool_w_in, v_pool_w_grp, v_pool_b_grp, v_pool_scale, v_pool_w_out, v_ln_mix_g, v_ln_mix_b, v_mlp_w1, v_mlp_w2, v_ln_mlp_g, v_ln_mlp_b, v_ple_w, v_ple_gate_w, v_ple_gate_b)))
    depth = ln_mix_g.shape[0]
    alpha = (2 * depth) ** 0.25
    S, D = x.shape[1], x.shape[2]
    xs = x.reshape(S, D)
    tgt = loss_target.reshape(S, D)
    p3 = p.reshape(depth, S, p.shape[-1])
    me = 4 * lax.axis_index("x") + 2 * lax.axis_index("y") + lax.axis_index("c")

    def layer_weights(i):
        s = i // 2
        mixer = ("lru_w_in", "lru_w_out") if i % 2 == 0 else ("pool_w_in", "pool_w_grp", "pool_w_out")
        return [(n, s) for n in mixer] + [(n, i) for n in ("mlp_w1", "mlp_w2", "ple_w", "ple_gate_w")]

    def axis_of(key):
        return 0 if key[0] == "small" else BIG_AXIS[key[0]] - 1

    def start_gather(tag, keys, after):
        axes = [axis_of(k) for k in keys]
        lands = [land[k] for k in keys]
        sizes = [a.shape[ax] // N_DEV for a, ax in zip(lands, axes)]
        return keys, _split_start(f"gather_{tag}_start", "gather", [], lands, axes, sizes, after=after), axes, sizes

    def finish_gather(tag, pending, after):
        keys, handle, axes, sizes = pending
        for (n, l), full in zip(keys, _split_wait(f"gather_{tag}_wait", "gather", handle, axes, sizes, after)[1]):
            W[n][l] = full

    def start_exchange(tag, keys, arrs, after):
        axes = [axis_of(k) for k in keys]
        sizes = [a.shape[ax] // N_DEV for a, ax in zip(arrs, axes)]
        lands = []
        for k, a, ax, sz in zip(keys, arrs, axes, sizes):
            if reads_own_block(k):
                shp = list(a.shape)
                shp[ax] = sz
                lands.append(lax.empty((N_DEV,) + tuple(shp), a.dtype))
            else:
                lands.append(_own_block_placed(a, ax, sz, me))
        return keys, _split_start(f"exchange_{tag}_start", "scatter", arrs, lands, axes, sizes, after=after), axes, sizes

    def finish_exchange(tag, pending, after):
        keys, handle, axes, sizes = pending
        sources, landed = _split_wait(f"exchange_{tag}_wait", "scatter", handle, axes, sizes, after)
        partial.update(zip(keys, landed))
        own_grad.update(zip(keys, sources))

    def reads_own_block(key):
        return key[0] not in ("small", "pool_w_grp")

    me1 = jnp.reshape(me, (1,)).astype(jnp.int32)
    land = {}
    W = {n: [None] * A[n].shape[0] for n in BIG}
    small_shard_shapes = [A[n].shape for n in SMALL_SHARDED]
    gathered = _all_gather("gather_small_params", [_pack([A[n] for n in SMALL_SHARDED])], [0])
    def gather_groups(i):
        keys = layer_weights(i)
        mixer, (w1, w2, pw, pg) = keys[:-4], keys[-4:]
        if i % 2 == 0:
            return [("in", mixer[:1]), ("out", mixer[1:]), ("up", [w1]), ("rest", [w2, pw, pg])]
        return [("in", mixer + [w1]), ("rest", [w2, pw, pg])]

    gather_pending = {}

    def send_layer(i, behind):
        for tag, keys in gather_groups(i):
            for k in keys:
                land[k] = _cast_into_window(f"cast_{k[0]}_{k[1]}", A[k[0]], k[1], axis_of(k), me1, after=behind)
            gather_pending[(i, tag)] = start_gather(f"l{i}_{tag}", keys, behind)
            behind = (gather_pending[(i, tag)][1][4],)
        return behind

    layer0_started = send_layer(0, (gathered[0],))
    small_full = gathered[0].reshape(N_DEV, -1, LANES)
    r = 0
    for n, shp in zip(SMALL_SHARDED, small_shard_shapes):
        rows = -(-math.prod(shp) // (LANES * SUBLANES)) * SUBLANES
        W[n] = _gather_last_axis(small_full[:, r:r + rows], shp)
        r += rows
    wa_b, wx_b = _to_bf16("cast_gates", [lru_wa, lru_wx], after=layer0_started)
    n_lru = lru_w_in.shape[0]
    lru_par = [jnp.concatenate([W["lru_conv_w"][s], lru_conv_b[s][None], lru_ba[s][None], lru_bx[s][None],
                                lru_lambda[s][None]], axis=0) for s in range(n_lru)]
    pool_par = [jnp.stack([W["pool_b_grp"][s], W["pool_scale"][s]], axis=0) for s in range(pool_w_in.shape[0])]

    saved = []
    h_in = xs
    (h_in_b,) = _to_bf16("cast_inputs", [xs], after=layer0_started)
    layer1_started = send_layer(1, (layer0_started[0], h_in_b))
    finish_gather("l0_in", gather_pending[(0, "in")], [h_in_b, wa_b, layer1_started[0]])
    for i in range(depth):
        s = i // 2
        sv = {"x0b": h_in_b}
        if i > 0:
            finish_gather(f"l{i}_in", gather_pending[(i, "in")], h_in)
        ln_out = dict(out_dtypes=[F32, F32, BF16], tm=MM_TM_ROWS, tn=D,
                      epi=lambda acc, xp, g, b: _ln_apply(alpha * xp + acc, g, b))
        if i % 2 == 0:
            sv["proj"] = _mm(f"l{i}_lru_in", h_in_b, W["lru_w_in"], "nn", [F32], b_lead=s)
            if i == 0:
                behind = (layer1_started[0], sv["proj"])
                for later in range(2, depth):
                    behind = send_layer(later, behind)
                all_started = behind[0]
            sv["gh"], sv["h"], *sv["gates"] = _lru_fwd(f"l{i}_lru_core", sv["proj"], lru_par[s], wa_b[s], wx_b[s])
            finish_gather(f"l{i}_out", gather_pending[(i, "out")], [sv["gh"], all_started])
            mix_in, mix_w = sv["gh"], W["lru_w_out"]
        else:
            u_pool = _mm(f"l{i}_pool_in", h_in_b, W["pool_w_in"], "nn", [F32], b_lead=s)
            sv["zs"], sv["pooled"], sv["z"] = _pool_fwd(f"l{i}_pool_core", u_pool, W["pool_w_grp"][s], pool_par[s])
            mix_in, mix_w = sv["zs"], W["pool_w_out"]
        sv["z1"], sv["x1"], sv["x1b"] = _mm(f"l{i}_mix_out_ln", mix_in, mix_w, "nn", b_lead=s,
                                            extras=[h_in, ln_mix_g[i][None], ln_mix_b[i][None]], **ln_out)
        if i % 2 == 0:
            finish_gather(f"l{i}_up", gather_pending[(i, "up")], sv["x1b"])
        sv["hpre"], sv["hact"] = _mm(f"l{i}_mlp_up", sv["x1b"], W["mlp_w1"], "nn", [BF16, BF16], b_lead=i,
                                     epi=lambda acc: (acc, jnp.square(jnp.maximum(acc, 0.0))))
        finish_gather(f"l{i}_rest", gather_pending[(i, "rest")], sv["hact"])
        sv["z2"], sv["x2"], sv["x2b"] = _mm(f"l{i}_mlp_down_ln", sv["hact"], W["mlp_w2"], "nn", b_lead=i,
                                            extras=[sv["x1"], ln_mlp_g[i][None], ln_mlp_b[i][None]], **ln_out)
        sv["pp"] = _mm(f"l{i}_ple_up", p3, W["ple_w"], "nn", [F32], a_lead=i, b_lead=i)

        def ple_epi(acc, bg, x2t, ppt):
            gpre = acc + bg
            x3 = x2t + ppt * jax.nn.sigmoid(gpre)
            return x3, x3, gpre

        h_in, h_in_b, sv["gpre"] = _mm(f"l{i}_ple_gate", sv["x2b"], W["ple_gate_w"], "nn", [F32, BF16, F32], b_lead=i,
                                       epi=ple_epi, extras=[ple_gate_b[i][None], sv["x2"], sv["pp"]],
                                       tm=MM_TM_ROWS, tn=D)
        saved.append(sv)

    dx, dpp, dgpre, sq, dbg = _loss_and_grad("loss", h_in, tgt, saved[-1]["gpre"], saved[-1]["pp"])
    loss = lax.psum(0.5 * sq[0, 0] / D, ("x", "y", "c"))

    dW = {n: [None] * A[n].shape[0] for n in BIG}
    dsmall = {n: [None] * A[n].shape[0] for n in REPLICATED + SMALL_SHARDED}
    small_names = REPLICATED + SMALL_SHARDED
    partial = {}
    own_grad = {}
    exchange_pending = {}
    exchange_token = ()
    for i in reversed(range(depth)):
        s = i // 2
        sv = saved[i]
        dsmall["ple_gate_b"][i] = dbg[0]
        dW["ple_w"][i] = _mm(f"l{i}_d_ple_w", p3, dpp, "tn", [BF16], a_lead=i, after=exchange_token)
        dW["ple_gate_w"][i] = _mm(f"l{i}_d_ple_gate_w", sv["x2b"], dgpre, "tn", [BF16])
        ln_back = dict(out_dtypes=[F32, BF16], tm=MM_TM_ROWS, tn=D, n_sums=2)
        dz2, dz2b, dg, db = _mm(f"l{i}_d_x2_ln", dgpre, W["ple_gate_w"], "nt", b_lead=i,
                                extras=[dx, sv["z2"], ln_mlp_g[i][None]],
                                epi=lambda acc, d, z, g: _ln_grad(acc + d, z, g), after=exchange_token, **ln_back)
        dsmall["ln_mlp_g"][i], dsmall["ln_mlp_b"][i] = dg[0], db[0]
        dhpre = _mm(f"l{i}_d_hpre", dz2b, W["mlp_w2"], "nt", [BF16], b_lead=i, extras=[sv["hpre"]],
                    epi=lambda acc, hp: (acc * (2.0 * jnp.maximum(hp.astype(F32), 0.0)),))
        dW["mlp_w2"][i] = _mm(f"l{i}_d_mlp_w2", sv["hact"], dz2b, "tn", [BF16])
        dW["mlp_w1"][i] = _mm(f"l{i}_d_mlp_w1", sv["x1b"], dhpre, "tn", [BF16])
        mlp_after = ()
        if i == 0:
            early = [(n, 0) for n in ("ple_w", "ple_gate_w", "mlp_w2", "mlp_w1")]
            exchange_early0 = start_exchange("early0", early, [dW[n][l] for n, l in early], ())
            mlp_after = (exchange_early0[1][4],)
        dz1, dz1b, dg, db = _mm(f"l{i}_d_x1_ln", dhpre, W["mlp_w1"], "nt", b_lead=i,
                                extras=[dz2, sv["z1"], ln_mix_g[i][None]],
                                epi=lambda acc, d, z, g: _ln_grad(acc + alpha * d, z, g), after=mlp_after, **ln_back)
        dsmall["ln_mix_g"][i], dsmall["ln_mix_b"][i] = dg[0], db[0]
        if i % 2 == 0:
            dW["lru_w_out"][s] = _mm(f"l{i}_d_lru_w_out", sv["gh"], dz1b, "tn", [BF16])
            dgh = _mm(f"l{i}_d_gh", dz1b, W["lru_w_out"], "nt", [F32], b_lead=s)
            dup, dy, dwa, dwx, dpar = _lru_bwd(f"l{i}_lru_core_bwd", sv["proj"], sv["h"], dgh, sv["gates"],
                                               lru_par[s], wa_b[s], wx_b[s])
            dsmall["lru_wa"][s], dsmall["lru_wx"][s] = dwa, dwx
            dsmall["lru_conv_w"][s] = dpar[0:4]
            for k, n in enumerate(("lru_conv_b", "lru_ba", "lru_bx", "lru_lambda")):
                dsmall[n][s] = dpar[4 + k]
            dmix_in = jnp.concatenate([dup, dy], axis=1)
            win = "lru_w_in"
        else:
            dW["pool_w_out"][s] = _mm(f"l{i}_d_pool_w_out", sv["zs"], dz1b, "tn", [BF16])
            dzs = _mm(f"l{i}_d_zs", dz1b, W["pool_w_out"], "nt", [F32], b_lead=s)
            dmix_in, dW["pool_w_grp"][s], dpar = _pool_bwd(f"l{i}_pool_core_bwd", sv["pooled"], sv["z"], dzs,
                                                          W["pool_w_grp"][s], pool_par[s])
            dsmall["pool_b_grp"][s], dsmall["pool_scale"][s] = dpar[0], dpar[1]
            win = "pool_w_in"
        dW[win][s] = _mm(f"l{i}_d_{win}", sv["x0b"], dmix_in, "tn", [BF16])
        x0_after = ()
        if i > 0:
            keys = layer_weights(i)
            exchange_pending[i] = start_exchange(f"l{i}", keys, [dW[n][l] for n, l in keys], ())
            exchange_token = (exchange_pending[i][1][4],)
        else:
            small_grads = [jnp.stack(dsmall[n]) for n in small_names]
            small_shapes = [g.shape for g in small_grads]
            packed_g = _pack(small_grads)
            assert packed_g.shape[0] % (N_DEV * SUBLANES) == 0, packed_g.shape
            late = [("lru_w_out", 0), ("lru_w_in", 0), ("small", 0)]
            exchange_late0 = start_exchange("late0", late, [dW["lru_w_out"][0], dW["lru_w_in"][0], packed_g], ())
            x0_after = (exchange_late0[1][4],)
        if i > 0:
            def x0_epi(acc, d, gpre, pp):
                dxv = acc + alpha * d
                return (dxv,) + _ple_grad(dxv, gpre, pp)

            dx, dpp, dgpre, dbg = _mm(f"l{i}_d_x0_ple", dmix_in, W[win], "nt", [F32, BF16, BF16], b_lead=s,
                                      extras=[dz1, saved[i - 1]["gpre"], saved[i - 1]["pp"]], epi=x0_epi,
                                      tm=MM_TM_ROWS, tn=D, n_sums=1)
        else:
            dx = _mm(f"l{i}_d_x0", dmix_in, W[win], "nt", [F32], b_lead=s, extras=[dz1],
                     epi=lambda acc, d: (acc + alpha * d,), after=x0_after)
    grad_x = dx.reshape(x.shape)

    for i in range(1, depth):
        finish_exchange(f"l{i}", exchange_pending[i], x0_after[0])
    stacked = {n: None for n in BIG}
    layer0 = layer_weights(0)

    def adamw(n, l, after=()):
        own = dict(own=own_grad[(n, l)], own_axis=axis_of((n, l)), me1=me1) if reads_own_block((n, l)) else {}
        stacked[n] = _adamw_layer(f"adamw_{n}_{l}", partial[(n, l)], A[n], A["m_" + n], A["v_" + n], l, stacked[n],
                                  after=after, **own)

    for n in BIG:
        for l in reversed(range(A[n].shape[0])):
            if (n, l) not in layer0:
                adamw(n, l)
    behind = [dx] + [stacked[n][0] for n in BIG if stacked[n] is not None]
    finish_exchange("early0", exchange_early0, behind)
    finish_exchange("late0", exchange_late0, behind)
    red = _sum8("sum_small", partial[("small", 0)])
    rows = red.shape[0]
    red_land = lax.dynamic_update_slice_in_dim(lax.empty((N_DEV * rows, LANES), F32), red, me * rows, 0)
    small_handle = _split_start("gather_small_start", "gather", [], [red_land], [0], [rows])
    for n, l in layer0:
        adamw(n, l, after=(small_handle[4],))
    outs = {n: [o.reshape(A[n].shape) for o in stacked[n]] for n in BIG}
    red_full = _split_wait("gather_small_wait", "gather", small_handle, [0], [rows],
                           [stacked[n][0] for n, _ in layer0])[1][0]
    small_g = dict(zip(small_names, _unpack(red_full, small_shapes)))
    for n in SMALL_SHARDED:
        width = A[n].shape[-1]
        small_g[n] = lax.dynamic_slice_in_dim(small_g[n], me * width, width, axis=small_g[n].ndim - 1)
    for n in small_names:
        outs[n] = [small_g[n]] + _adamw_small(f"adamw_{n}", A[n], small_g[n], A["m_" + n], A["v_" + n])

    return (loss, grad_x, *[outs[n][0] for n in WEIGHTS], *[outs[n][1] for n in WEIGHTS],
            *[outs[n][2] for n in WEIGHTS], *[outs[n][3] for n in WEIGHTS])
```

```python
import functools
import math

import jax
import jax.numpy as jnp
from jax import lax
from jax.experimental import pallas as pl
from jax.experimental.pallas import tpu as pltpu

F32 = jnp.float32
BF16 = jnp.bfloat16
MESH = pl.DeviceIdType.MESH
N_DEV = 8
LANES = 128
SUBLANES = 8

LN_EPS = 1e-5
LRU_C = 8.0
CONV_WIDTH = 4
POOL_HALO = 16
ADAM_LR = 0.001
ADAM_B1 = 0.9
ADAM_B2 = 0.999
ADAM_EPS = 1e-08
ADAM_WD = 0.01
ADAM_STEP = 10

VMEM_LIMIT = 48 * 1024 * 1024
VMEM_LIMIT_SEQ = 56 * 1024 * 1024
SEQ_CHUNK = 1024
MM_TK = 4096
MM_TK_TOKENS = 4096
MM_TM_ROWS = 512
MM_TN = 512
MM_TN_WIDE = 1024
MM_TN_WIDE_MAX_K = 2048
GELU_C0 = math.sqrt(2.0 / math.pi)
GELU_C1 = 0.044715


def _in_hbm(*arrays):
    return [pltpu.with_memory_space_constraint(a, pltpu.HBM) for a in arrays]


def _cparams(*sem, limit=None):
    return pltpu.CompilerParams(dimension_semantics=tuple(sem) if sem else None,
                                vmem_limit_bytes=VMEM_LIMIT if limit is None else limit)


def _tile(n, pref):
    if n <= pref:
        return n
    t = pref - pref % LANES
    while t >= LANES:
        if n % t == 0:
            return t
        t -= LANES
    return n


def _row_tile(n, pref):
    if n <= pref:
        return n
    t = pref - pref % SUBLANES
    while t >= SUBLANES:
        if n % t == 0:
            return t
        t -= SUBLANES
    return n


def _mm(name, a, b, mode, out_dtypes, epi=None, extras=(), a_lead=None, b_lead=None, tm=1024, tn=None, tk=None,
        after=(), n_sums=0):
    if isinstance(b, (list, tuple)):
        b, b_lead = b[b_lead], None
    a2 = a.shape[-2:]
    b2 = b.shape[-2:]
    if mode == "nn":
        (M, K), N = a2, b2[1]
        assert b2[0] == K
    elif mode == "nt":
        (M, K), N = a2, b2[0]
        assert b2[1] == K
    else:
        (K, M), N = a2, b2[1]
        assert b2[0] == K
    if tk is None:
        tk = MM_TK_TOKENS if mode == "tn" else MM_TK
    if tn is None:
        tn = MM_TN_WIDE if (mode != "tn" and K <= MM_TN_WIDE_MAX_K) else MM_TN
    tm, tn, tk = _tile(M, tm), _tile(N, tn), _tile(K, tk)
    nk = K // tk
    n_extra = len(extras)
    n_out = len(out_dtypes)
    assert n_sums == 0 or tn == N
    resident = {"pipeline_mode": pl.Buffered(1)} if (tn == N and nk == 1) else {}

    def lead(shape, idx, which, **kw):
        if which is None:
            return pl.BlockSpec(shape, idx, **kw)
        return pl.BlockSpec((None,) + shape, lambda i, j, k: (which,) + idx(i, j, k), **kw)

    if mode == "nn":
        a_spec = lead((tm, tk), lambda i, j, k: (i, k), a_lead)
        b_spec = lead((tk, tn), lambda i, j, k: (k, j), b_lead, **resident)
        dims = (((1,), (0,)), ((), ()))
    elif mode == "nt":
        a_spec = lead((tm, tk), lambda i, j, k: (i, k), a_lead)
        b_spec = lead((tn, tk), lambda i, j, k: (j, k), b_lead, **resident)
        dims = (((1,), (1,)), ((), ()))
    else:
        a_spec = lead((tk, tm), lambda i, j, k: (k, i), a_lead)
        b_spec = lead((tk, tn), lambda i, j, k: (k, j), b_lead, **resident)
        dims = (((0,), (0,)), ((), ()))
    e_specs = []
    for e in extras:
        if e.shape[0] == 1:
            e_specs.append(pl.BlockSpec((1, tn), lambda i, j, k: (0, j)))
        else:
            e_specs.append(pl.BlockSpec((tm, tn), lambda i, j, k: (i, j)))

    n_after = len(after)

    def body(a_ref, b_ref, *rest):
        e_refs = rest[:n_extra]
        rest = rest[:n_extra] + rest[n_extra + n_after:]
        o_refs = rest[n_extra:n_extra + n_out]
        s_refs = rest[n_extra + n_out:n_extra + n_out + n_sums]
        part = lax.dot_general(a_ref[...].astype(BF16), b_ref[...].astype(BF16), dims, preferred_element_type=F32)

        def finish(r):
            res = (r,) if epi is None else epi(r, *[e[...] for e in e_refs])
            for o, v in zip(o_refs, res[:n_out]):
                o[...] = v.astype(o.dtype)
            first = pl.program_id(0) == 0
            for sr, v in zip(s_refs, res[n_out:]):
                @pl.when(first)
                def _(sr=sr, v=v):
                    sr[...] = v

                @pl.when(jnp.logical_not(first))
                def _(sr=sr, v=v):
                    sr[...] += v

        if nk == 1:
            finish(part)
            return
        acc = rest[n_extra + n_out + n_sums]
        k = pl.program_id(2)

        @pl.when(k == 0)
        def _():
            acc[...] = part

        @pl.when(jnp.logical_and(k > 0, k < nk - 1))
        def _():
            acc[...] += part

        @pl.when(k == nk - 1)
        def _():
            finish(acc[...] + part)

    outs = pl.pallas_call(
        body,
        name=name,
        grid=(M // tm, N // tn, nk),
        in_specs=[a_spec, b_spec] + e_specs + [pl.BlockSpec(memory_space=pl.ANY)] * n_after,
        out_specs=[pl.BlockSpec((tm, tn), lambda i, j, k: (i, j)) for _ in out_dtypes]
        + [pl.BlockSpec((1, tn), lambda i, j, k: (0, 0))] * n_sums,
        out_shape=[jax.ShapeDtypeStruct((M, N), d) for d in out_dtypes] + [jax.ShapeDtypeStruct((1, N), F32)] * n_sums,
        scratch_shapes=[pltpu.VMEM((tm, tn), F32)] if nk > 1 else [],
        compiler_params=_cparams(*(("arbitrary",) * 3 if n_sums else ("parallel", "parallel", "arbitrary"))),
    )(*_in_hbm(a, b, *extras), *after)
    return outs[0] if n_out + n_sums == 1 else tuple(outs)


def _rowwise(name, fn, tiled, params, outs, accs=(), tm=256, after=()):
    S = tiled[0].shape[0]
    tm = _tile(S, tm)
    nt, npar, no = len(tiled), len(params), len(outs)
    n_after = len(after)

    def body(*refs):
        t_refs = refs[:nt]
        p_refs = refs[nt:nt + npar]
        refs = refs[nt + npar + n_after:]
        o_refs = refs[:no]
        a_refs = refs[no:]
        res = fn(*[r[...] for r in t_refs], *[r[...] for r in p_refs])
        for o, v in zip(o_refs, res[:no]):
            o[...] = v.astype(o.dtype)
        first = pl.program_id(0) == 0
        for ar, v in zip(a_refs, res[no:]):
            @pl.when(first)
            def _(ar=ar, v=v):
                ar[...] = v

            @pl.when(jnp.logical_not(first))
            def _(ar=ar, v=v):
                ar[...] += v

    full = lambda p: pl.BlockSpec(p.shape, lambda i, nd=p.ndim: (0,) * nd)
    res = pl.pallas_call(
        body,
        name=name,
        grid=(S // tm,),
        in_specs=[pl.BlockSpec((tm, t.shape[1]), lambda i: (i, 0)) for t in tiled] + [full(p) for p in params]
        + [pl.BlockSpec(memory_space=pl.ANY)] * n_after,
        out_specs=[pl.BlockSpec((tm, c), lambda i: (i, 0)) for c, _ in outs]
        + [pl.BlockSpec(s, lambda i, nd=len(s): (0,) * nd) for s in accs],
        out_shape=[jax.ShapeDtypeStruct((S, c), d) for c, d in outs] + [jax.ShapeDtypeStruct(s, F32) for s in accs],
        compiler_params=_cparams("arbitrary"),
    )(*_in_hbm(*tiled, *params), *after)
    return res


def _ln_stats(z):
    mu = jnp.mean(z, axis=-1, keepdims=True)
    zc = z - mu
    var = jnp.mean(zc * zc, axis=-1, keepdims=True)
    return zc, lax.rsqrt(var + LN_EPS)


def _ln_apply(z, g, b):
    zc, rstd = _ln_stats(z)
    y = zc * rstd * g + b
    return z, y, y


def _ln_grad(dy, z, g):
    zc, rstd = _ln_stats(z)
    xhat = zc * rstd
    dxh = dy * g
    m1 = jnp.mean(dxh, axis=-1, keepdims=True)
    m2 = jnp.mean(dxh * xhat, axis=-1, keepdims=True)
    dz = rstd * (dxh - m1 - xhat * m2)
    return dz, dz, jnp.sum(dy * xhat, axis=0, keepdims=True), jnp.sum(dy, axis=0, keepdims=True)


def _ple_grad(dx3, gpre, pp):
    gate = jax.nn.sigmoid(gpre)
    dgpre = dx3 * pp * gate * (1.0 - gate)
    return dx3 * gate, dgpre, jnp.sum(dgpre, axis=0, keepdims=True)


def _loss_and_grad(name, y, target, gpre, pp):
    d = y.shape[1]

    def fn(y, t, gpre, pp):
        err = y - t
        sq = jnp.sum(jnp.sum(err * err, axis=0, keepdims=True), axis=1, keepdims=True)
        dy = err * (1.0 / d)
        dpp, dgpre, dbg = _ple_grad(dy, gpre, pp)
        return dy, dpp, dgpre, jnp.broadcast_to(sq, (1, LANES)), dbg

    return _rowwise(name, fn, [y, target, gpre, pp], [], [(d, F32), (d, BF16), (d, BF16)], accs=[(1, LANES), (1, d)])


def _rows(shape):
    return lax.broadcasted_iota(jnp.int32, shape, 0)


def _gelu(y):
    t = jnp.tanh(GELU_C0 * (y + GELU_C1 * y * y * y))
    return 0.5 * y * (1.0 + t), t


def _gelu_grad(y, t):
    return 0.5 * (1.0 + t) + 0.5 * y * (1.0 - t * t) * GELU_C0 * (1.0 + 3.0 * GELU_C1 * y * y)


def _softplus(x):
    return jnp.maximum(x, 0.0) + jnp.log(1.0 + jnp.exp(-jnp.abs(x)))


def _conv_fwd(xs, cw, cb):
    n = xs.shape[0]
    u = cw[3:4] * xs
    for k in (1, 2, 3):
        u = u + cw[3 - k:4 - k] * pltpu.roll(xs, k, 0)
    del n
    return u[SUBLANES:] + cb


def _lru_gates(u, wa, wx, ba, bx, sp, grow):
    ub = u.astype(BF16)
    r = jax.nn.sigmoid(jnp.dot(ub, wa, preferred_element_type=F32) + ba)
    ig = jax.nn.sigmoid(jnp.dot(ub, wx, preferred_element_type=F32) + bx)
    log_a = (-LRU_C) * r * sp
    a = jnp.exp(log_a)
    mult = jnp.sqrt(jnp.tanh(-log_a) * (1.0 + a * a))
    mult = jnp.where(grow == 0, 1.0, mult)
    return ub, r, ig, a, mult


def _scan8_fwd(a, b):
    row = _rows(a.shape)
    for k in (1, 2, 4):
        m = row >= k
        b = jnp.where(m, a * pltpu.roll(b, k, 0) + b, b)
        a = jnp.where(m, a * pltpu.roll(a, k, 0), a)
    return a, b


def _scan8_bwd(c, d):
    row = _rows(c.shape)
    for k in (1, 2, 4):
        m = row < SUBLANES - k
        d = jnp.where(m, c * pltpu.roll(d, SUBLANES - k, 0) + d, d)
        c = jnp.where(m, c * pltpu.roll(c, SUBLANES - k, 0), c)
    return c, d


def _pad_copy(dst, src, front, back):
    s, c = src.shape
    if front:
        dst[pl.ds(0, front), :] = jnp.zeros((front, c), dst.dtype)
    if back:
        dst[pl.ds(front + s, back), :] = jnp.zeros((back, c), dst.dtype)
    dst[pl.ds(front, s), :] = src[...].astype(dst.dtype)


def _lru_fwd(name, proj, par, wa, wx):
    S = proj.shape[0]
    R = proj.shape[1] // 2
    H = R // LANES
    ch = _tile(S, SEQ_CHUNK)
    nch = S // ch
    H8 = SUBLANES

    def body(up_ref, y_ref, par_ref, wa_ref, wx_ref, gh_ref, h_ref, r_ref, ig_ref, a_ref, mult_ref, gy_ref, dgy_ref,
             u_ref, up_pad):
        _pad_copy(up_pad, up_ref, H8, 0)
        par = par_ref[...]
        cw, cb, ba, bx = par[0:4], par[4:5], par[5:6], par[6:7]
        sp = _softplus(-par[7:8])
        wa_m, wx_m = wa_ref[...], wx_ref[...]

        def chunk(ci, carry):
            r0 = pl.multiple_of(ci * ch, ch)
            xs = up_pad[pl.ds(r0, ch + H8), :]
            u = _conv_fwd(xs, cw, cb)
            grow = _rows(u.shape) + r0
            _, r, ig, a, mult = _lru_gates(u, wa_m, wx_m, ba, bx, sp, grow)
            for ref, val in ((r_ref, r), (ig_ref, ig), (a_ref, a), (mult_ref, mult), (u_ref, u)):
                ref[pl.ds(r0, ch), :] = val
            bt = mult * (ig * u)
            hs = []
            for j in range(ch // H8):
                aa, bb = _scan8_fwd(a[j * H8:(j + 1) * H8], bt[j * H8:(j + 1) * H8])
                hj = bb + aa * carry
                carry = hj[H8 - 1:H8]
                hs.append(hj)
            h = jnp.concatenate(hs, axis=0)
            h_ref[pl.ds(r0, ch), :] = h
            y = y_ref[pl.ds(r0, ch), :]
            gy, t = _gelu(y)
            gy_ref[pl.ds(r0, ch), :] = gy
            dgy_ref[pl.ds(r0, ch), :] = _gelu_grad(y, t)
            gh_ref[pl.ds(r0, ch), :] = (h * gy).astype(gh_ref.dtype)
            return carry

        lax.fori_loop(0, nch, chunk, jnp.zeros((1, LANES), F32))

    col = lambda off: pl.BlockSpec((S, LANES), lambda h: (0, h + off))
    return pl.pallas_call(
        body,
        name=name,
        grid=(H,),
        in_specs=[col(0), col(H), pl.BlockSpec((8, LANES), lambda h: (0, h)),
                  pl.BlockSpec((None, LANES, LANES), lambda h: (h, 0, 0)),
                  pl.BlockSpec((None, LANES, LANES), lambda h: (h, 0, 0))],
        out_specs=[col(0)] * 9,
        out_shape=[jax.ShapeDtypeStruct((S, R), BF16)] + [jax.ShapeDtypeStruct((S, R), F32)] * 8,
        scratch_shapes=[pltpu.VMEM((S + H8, LANES), F32)],
        compiler_params=_cparams("parallel"),
    )(proj, proj, par, wa, wx)


def _lru_bwd(name, proj, h, dgh, gates, par, wa, wx):
    S = proj.shape[0]
    R = proj.shape[1] // 2
    H = R // LANES
    ch = _tile(S, SEQ_CHUNK)
    nch = S // ch
    H8 = SUBLANES
    nb = ch // H8

    def body(up_ref, h_ref, dgh_ref, r_ref, ig_ref, a_ref, mult_ref, gy_ref, dgy_ref, u_ref, par_ref, wa_ref, wx_ref,
             dup_ref, dy_ref, dwa_ref, dwx_ref, dpar_ref, up_pad, du_pad, vec_acc):
        _pad_copy(up_pad, up_ref, H8, 0)
        du_pad[pl.ds(S, H8), :] = jnp.zeros((H8, LANES), F32)
        par = par_ref[...]
        cw, lam = par[0:4], par[7:8]
        sp = _softplus(-lam)
        wa_m, wx_m = wa_ref[...], wx_ref[...]
        dwa_ref[...] = jnp.zeros_like(dwa_ref)
        dwx_ref[...] = jnp.zeros_like(dwx_ref)
        vec_acc[...] = jnp.zeros_like(vec_acc)
        nt_dims = (((1,), (1,)), ((), ()))
        tn_dims = (((0,), (0,)), ((), ()))

        def chunk(it, carry):
            lam_next, a_next = carry
            ci = nch - 1 - it
            r0 = pl.multiple_of(ci * ch, ch)
            here = pl.ds(r0, ch)
            u = u_ref[here, :]
            row = _rows(u.shape)
            grow = row + r0
            ub = u.astype(BF16)
            r, ig, a, mult = r_ref[here, :], ig_ref[here, :], a_ref[here, :], mult_ref[here, :]
            hcur = h_ref[here, :]
            before = h_ref[pl.ds(pl.multiple_of(jnp.maximum(r0 - H8, 0), H8), H8), :][H8 - 1:H8]
            hprev = jnp.where(row == 0, jnp.where(ci > 0, before, 0.0), pltpu.roll(hcur, 1, 0))
            dgh = dgh_ref[here, :]
            dy_ref[here, :] = (dgh * hcur * dgy_ref[here, :]).astype(dy_ref.dtype)
            dh = dgh * gy_ref[here, :]
            c = jnp.where(row == ch - 1, a_next, pltpu.roll(a, ch - 1, 0))
            ls = [None] * nb
            for j in range(nb - 1, -1, -1):
                cc, dd = _scan8_bwd(c[j * H8:(j + 1) * H8], dh[j * H8:(j + 1) * H8])
                lj = dd + cc * lam_next
                lam_next = lj[0:1]
                ls[j] = lj
            lmb = jnp.concatenate(ls, axis=0)
            da = lmb * hprev
            gu = ig * u
            dmult = lmb * gu
            dlog_a = da * a + jnp.where(grow == 0, 0.0, dmult * (-(a * a) / mult))
            dr = dlog_a * ((-LRU_C) * sp)
            drp = dr * r * (1.0 - r)
            dip = (lmb * mult * u) * ig * (1.0 - ig)
            drb, dib = drp.astype(BF16), dip.astype(BF16)
            du = (lmb * mult * ig
                  + lax.dot_general(drb, wa_m, nt_dims, preferred_element_type=F32)
                  + lax.dot_general(dib, wx_m, nt_dims, preferred_element_type=F32))
            du_pad[pl.ds(r0, ch), :] = du
            dwa_ref[...] += lax.dot_general(ub, drb, tn_dims, preferred_element_type=F32)
            dwx_ref[...] += lax.dot_general(ub, dib, tn_dims, preferred_element_type=F32)
            ssum = lambda v: jnp.sum(v, axis=0, keepdims=True)
            vec_acc[0:1, :] += ssum(drp)
            vec_acc[1:2, :] += ssum(dip)
            vec_acc[2:3, :] += ssum(dlog_a * ((-LRU_C) * r))
            return lam_next, a[0:1]

        zero = jnp.zeros((1, LANES), F32)
        lax.fori_loop(0, nch, chunk, (zero, zero))

        def conv_chunk(ci, acc):
            r0 = pl.multiple_of(ci * ch, ch)
            ds = du_pad[pl.ds(r0, ch + H8), :]
            xs = up_pad[pl.ds(r0, ch + H8), :]
            n = ch + H8
            du = ds[:ch]
            dup = cw[3:4] * du
            new = [acc[3] + jnp.sum(du * xs[H8:], axis=0, keepdims=True)]
            for k in (1, 2, 3):
                dup = dup + cw[3 - k:4 - k] * pltpu.roll(ds, n - k, 0)[:ch]
                new.append(acc[3 - k] + jnp.sum(du * pltpu.roll(xs, k, 0)[H8:], axis=0, keepdims=True))
            dup_ref[pl.ds(r0, ch), :] = dup.astype(dup_ref.dtype)
            return (new[3], new[2], new[1], new[0], acc[4] + jnp.sum(du, axis=0, keepdims=True))

        acc = lax.fori_loop(0, nch, conv_chunk, (zero,) * 5)
        dlam = vec_acc[2:3, :] * (-jax.nn.sigmoid(-lam))
        dpar_ref[...] = jnp.concatenate(list(acc) + [vec_acc[0:1, :], vec_acc[1:2, :], dlam], axis=0)

    col = lambda off: pl.BlockSpec((S, LANES), lambda h: (0, h + off))
    head = pl.BlockSpec((None, LANES, LANES), lambda h: (h, 0, 0))
    return pl.pallas_call(
        body,
        name=name,
        grid=(H,),
        in_specs=[col(0)] * 10 + [pl.BlockSpec((8, LANES), lambda h: (0, h)), head, head],
        out_specs=[col(0), col(0), head, head, pl.BlockSpec((8, LANES), lambda h: (0, h))],
        out_shape=[jax.ShapeDtypeStruct((S, R), BF16), jax.ShapeDtypeStruct((S, R), BF16),
                   jax.ShapeDtypeStruct((H, LANES, LANES), F32), jax.ShapeDtypeStruct((H, LANES, LANES), F32),
                   jax.ShapeDtypeStruct((8, R), F32)],
        scratch_shapes=[pltpu.VMEM((S + H8, LANES), F32), pltpu.VMEM((S + H8, LANES), F32),
                        pltpu.VMEM((8, LANES), F32)],
        compiler_params=_cparams("parallel", limit=VMEM_LIMIT_SEQ),
    )(proj, h, dgh, *gates, par, wa, wx)


def _window_sum(xs, g, up):
    n = xs.shape[0]
    s = xs
    for lvl, k in enumerate((1, 2, 4, 8)):
        sh = pltpu.roll(s, (n - k) if up else k, 0)
        s = s + jnp.where(g >= lvl, sh, 0.0)
    return s


def _pool_count(grow, g):
    return jnp.minimum(grow + 1, lax.shift_left(jnp.int32(2), g)).astype(F32)


def _pool_fwd(name, u, wgrp, par):
    S, D = u.shape
    G, W = wgrp.shape[0], wgrp.shape[1]
    ch = _tile(S, SEQ_CHUNK)
    nch = S // ch
    PH = POOL_HALO

    def body(u_ref, w_ref, par_ref, zs_ref, pooled_ref, z_ref, u_pad):
        g = pl.program_id(0)
        _pad_copy(u_pad, u_ref, PH, 0)
        par = par_ref[...]
        w = w_ref[...]

        def chunk(ci, _):
            r0 = pl.multiple_of(ci * ch, ch)
            xs = u_pad[pl.ds(r0, ch + PH), :]
            ws = _window_sum(xs, g, False)[PH:]
            uc = xs[PH:]
            cnt = _pool_count(_rows(uc.shape) + r0, g)
            pooled = (ws / cnt - uc).astype(BF16)
            z = jnp.dot(pooled, w, preferred_element_type=F32) + par[0:1]
            pooled_ref[pl.ds(r0, ch), :] = pooled
            z_ref[pl.ds(r0, ch), :] = z
            zs_ref[pl.ds(r0, ch), :] = (z * par[1:2]).astype(zs_ref.dtype)
            return 0

        lax.fori_loop(0, nch, chunk, 0)

    blk = pl.BlockSpec((S, W), lambda g: (0, g))
    return pl.pallas_call(
        body,
        name=name,
        grid=(G,),
        in_specs=[blk, pl.BlockSpec((None, W, W), lambda g: (g, 0, 0)), pl.BlockSpec((2, W), lambda g: (0, g))],
        out_specs=[blk, blk, blk],
        out_shape=[jax.ShapeDtypeStruct((S, D), BF16), jax.ShapeDtypeStruct((S, D), BF16),
                   jax.ShapeDtypeStruct((S, D), F32)],
        scratch_shapes=[pltpu.VMEM((S + PH, W), F32)],
        compiler_params=_cparams("parallel"),
    )(u, wgrp, par)


def _pool_bwd(name, pooled, z, dzs, wgrp, par):
    S, D = z.shape
    G, W = wgrp.shape[0], wgrp.shape[1]
    ch = _tile(S, SEQ_CHUNK)
    nch = S // ch
    PH = POOL_HALO

    def body(pooled_ref, z_ref, dzs_ref, w_ref, par_ref, du_ref, dw_ref, dpar_ref, q_pad, dw_acc):
        g = pl.program_id(0)
        q_pad[pl.ds(S, PH), :] = jnp.zeros((PH, W), F32)
        par = par_ref[...]
        w = w_ref[...]
        dw_acc[...] = jnp.zeros_like(dw_acc)

        def chunk(ci, acc):
            db, dsc = acc
            r0 = pl.multiple_of(ci * ch, ch)
            cnt = _pool_count(_rows((ch, W)) + r0, g)
            pooled = pooled_ref[pl.ds(r0, ch), :]
            z = z_ref[pl.ds(r0, ch), :]
            dzs = dzs_ref[pl.ds(r0, ch), :]
            dz = dzs * par[1:2]
            dzb = dz.astype(BF16)
            dw_acc[...] += lax.dot_general(pooled, dzb, (((0,), (0,)), ((), ())), preferred_element_type=F32)
            dpooled = lax.dot_general(dzb, w, (((1,), (1,)), ((), ())), preferred_element_type=F32)
            q_pad[pl.ds(r0, ch), :] = dpooled / cnt
            return (db + jnp.sum(dz, axis=0, keepdims=True), dsc + jnp.sum(dzs * z, axis=0, keepdims=True))

        zero = jnp.zeros((1, W), F32)
        db, dsc = lax.fori_loop(0, nch, chunk, (zero, zero))
        dpar_ref[...] = jnp.concatenate([db, dsc], axis=0)
        dw_ref[...] = dw_acc[...].astype(dw_ref.dtype)

        def back(ci, _):
            r0 = pl.multiple_of(ci * ch, ch)
            qs = q_pad[pl.ds(r0, ch + PH), :]
            ws = _window_sum(qs, g, True)[:ch]
            qc = qs[:ch]
            cnt = _pool_count(_rows(qc.shape) + r0, g)
            du_ref[pl.ds(r0, ch), :] = (ws - qc * cnt).astype(du_ref.dtype)
            return 0

        lax.fori_loop(0, nch, back, 0)

    blk = pl.BlockSpec((S, W), lambda g: (0, g))
    wspec = pl.BlockSpec((None, W, W), lambda g: (g, 0, 0))
    pspec = pl.BlockSpec((2, W), lambda g: (0, g))
    return pl.pallas_call(
        body,
        name=name,
        grid=(G,),
        in_specs=[blk, blk, blk, wspec, pspec],
        out_specs=[blk, wspec, pspec],
        out_shape=[jax.ShapeDtypeStruct((S, D), BF16), jax.ShapeDtypeStruct((G, W, W), BF16),
                   jax.ShapeDtypeStruct((2, D), F32)],
        scratch_shapes=[pltpu.VMEM((S + PH, W), F32), pltpu.VMEM((W, W), F32)],
        compiler_params=_cparams("parallel"),
    )(pooled, z, dzs, wgrp, par)


def _my_place():
    x, y, c = lax.axis_index("x"), lax.axis_index("y"), lax.axis_index("c")
    return x, y, c, 4 * x + 2 * y + c


def _peers(x, y, c):
    out = []
    for d in range(1, N_DEV):
        px = 1 - x if d & 4 else x
        py = 1 - y if d & 2 else y
        pc = 1 - c if d & 1 else c
        out.append(((px, py, pc), 4 * px + 2 * py + pc))
    return out


def _window(ref, axis, start, size):
    idx = [slice(None)] * len(ref.shape)
    idx[axis] = pl.ds(start, size)
    return ref.at[tuple(idx)]


def _to_bf16(name, arrs, after=()):
    outs = []
    for i, a in enumerate(arrs):
        a2 = a.reshape(-1, a.shape[-1])
        tr = _tile(a2.shape[0], 512)
        o = pl.pallas_call(
            lambda a_ref, *rest: rest[-1].__setitem__(Ellipsis, a_ref[...].astype(BF16)),
            name=f"{name}_{i}",
            grid=(a2.shape[0] // tr,),
            in_specs=[pl.BlockSpec((tr, a2.shape[1]), lambda r: (r, 0))] + [pl.BlockSpec(memory_space=pl.ANY)] * len(after),
            out_specs=pl.BlockSpec((tr, a2.shape[1]), lambda r: (r, 0)),
            out_shape=jax.ShapeDtypeStruct(a2.shape, BF16),
            compiler_params=_cparams("parallel"),
        )(a2, *after)
        outs.append(o.reshape(a.shape))
    return outs


def _all_gather(name, shards, axes):
    n = len(shards)
    sizes = [s.shape[ax] for s, ax in zip(shards, axes)]

    def body(*refs):
        ins, outs = refs[:n], refs[n:2 * n]
        send, recv, loc = refs[2 * n:]
        x, y, c, me = _my_place()
        peers = _peers(x, y, c)
        local = []
        for i in range(n):
            dst = _window(outs[i], axes[i], me * sizes[i], sizes[i])
            cp = pltpu.make_async_copy(ins[i], dst, loc.at[i])
            cp.start()
            local.append(cp)
            for peer, _ in peers:
                pltpu.make_async_remote_copy(src_ref=ins[i], dst_ref=dst, send_sem=send.at[i], recv_sem=recv.at[i],
                                             device_id=peer, device_id_type=MESH).start()
        for i in range(n):
            local[i].wait()
            seven = _window(outs[i], axes[i], 0, (N_DEV - 1) * sizes[i])
            pltpu.make_async_remote_copy(src_ref=seven, dst_ref=seven, send_sem=send.at[i], recv_sem=recv.at[i],
                                         device_id=(x, y, c), device_id_type=MESH).wait()

    def full_shape(s, ax):
        shp = list(s.shape)
        shp[ax] *= N_DEV
        return jax.ShapeDtypeStruct(tuple(shp), s.dtype)

    any_spec = pl.BlockSpec(memory_space=pl.ANY)
    return pl.pallas_call(
        body,
        name=name,
        in_specs=[any_spec] * n,
        out_specs=[any_spec] * n,
        out_shape=[full_shape(s, ax) for s, ax in zip(shards, axes)],
        scratch_shapes=[pltpu.SemaphoreType.DMA((n,)), pltpu.SemaphoreType.DMA((n,)), pltpu.SemaphoreType.DMA((n,))],
        compiler_params=pltpu.CompilerParams(has_side_effects=True),
    )(*shards)


HBM_SPEC = pl.BlockSpec(memory_space=pltpu.HBM)
SEM_SPEC = pl.BlockSpec(memory_space=pltpu.SEMAPHORE)
SPLIT_EFFECT = pltpu.SideEffectType.DATAFLOW_SIDE_EFFECTING


def _push_all(kind, src, dst, axis, size, send_sem, recv_sem, place):
    x, y, c, me = place
    for peer, pidx in _peers(x, y, c):
        if kind == "gather":
            s = d = _window(dst, axis, me * size, size)
        else:
            s, d = _window(src, axis, pidx * size, size), dst.at[me]
        pltpu.make_async_remote_copy(src_ref=s, dst_ref=d, send_sem=send_sem, recv_sem=recv_sem, device_id=peer,
                                     device_id_type=MESH).start()


def _drain_all(kind, dst, axis, size, send_sem, recv_sem, place):
    x, y, c, _ = place
    seven = _window(dst, axis, 0, (N_DEV - 1) * size) if kind == "gather" else dst.at[pl.ds(0, N_DEV - 1)]
    pltpu.make_async_remote_copy(src_ref=seven, dst_ref=seven, send_sem=send_sem, recv_sem=recv_sem,
                                 device_id=(x, y, c), device_id_type=MESH).wait()


def _own_block_placed(src, axis, size, me):
    own = lax.dynamic_slice_in_dim(src, me * size, size, axis)
    return lax.dynamic_update_slice_in_dim(lax.empty((N_DEV,) + own.shape, src.dtype), own[None], me, 0)


def _split_start(name, kind, srcs, lands, axes, sizes, after=(), collective_id=None):
    n, ns, na = len(lands), len(srcs), len(after)

    def body(*refs):
        src_refs, land_refs = refs[:ns], refs[ns:ns + n]
        send, recv = refs[ns + n + na], refs[ns + n + na + 1]
        token = refs[-1]
        place = _my_place()
        if collective_id is not None:
            barrier = pltpu.get_barrier_semaphore()
            for peer, _ in _peers(*place[:3]):
                pl.semaphore_signal(barrier, inc=1, device_id=peer, device_id_type=MESH)
            pl.semaphore_wait(barrier, N_DEV - 1)
        for k in range(n):
            _push_all(kind, src_refs[k] if ns else None, land_refs[k], axes[k], sizes[k], send.at[k], recv.at[k], place)
        token[...] = jnp.zeros_like(token)

    hbm = lambda a: pltpu.HBM(a.shape, a.dtype)
    res = pl.pallas_call(
        body,
        name=name,
        out_shape=(pltpu.SemaphoreType.DMA((n,)), pltpu.SemaphoreType.DMA((n,)), *[hbm(a) for a in srcs],
                   *[hbm(a) for a in lands], jax.ShapeDtypeStruct((SUBLANES, LANES), F32)),
        in_specs=[HBM_SPEC] * (ns + n) + [pl.BlockSpec(memory_space=pl.ANY)] * na,
        out_specs=(SEM_SPEC, SEM_SPEC, *[HBM_SPEC] * (ns + n), pl.BlockSpec(memory_space=pltpu.VMEM)),
        input_output_aliases={k: 2 + k for k in range(ns + n)},
        compiler_params=pltpu.CompilerParams(has_side_effects=SPLIT_EFFECT, collective_id=collective_id),
    )(*[pltpu.with_memory_space_constraint(a, pltpu.HBM) for a in (*srcs, *lands)], *after)
    return res[0], res[1], list(res[2:2 + ns]), list(res[2 + ns:2 + ns + n]), res[-1]


def _split_wait(name, kind, handle, axes, sizes, after):
    send, recv, srcs, lands, _ = handle
    n, ns = len(lands), len(srcs)
    after = list(after) if isinstance(after, (list, tuple)) else [after]

    def body(*refs):
        land_refs = refs[ns:ns + n]
        send_ref, recv_ref = refs[ns + n], refs[ns + n + 1]
        place = _my_place()
        for k in range(n):
            _drain_all(kind, land_refs[k], axes[k], sizes[k], send_ref.at[k], recv_ref.at[k], place)

    hbm = lambda a: pltpu.HBM(a.shape, a.dtype)
    res = pl.pallas_call(
        body,
        name=name,
        out_shape=tuple(hbm(a) for a in (*srcs, *lands)),
        in_specs=[HBM_SPEC] * (ns + n) + [SEM_SPEC, SEM_SPEC] + [pl.BlockSpec(memory_space=pl.ANY)] * len(after),
        out_specs=tuple([HBM_SPEC] * (ns + n)),
        input_output_aliases={k: k for k in range(ns + n)},
        compiler_params=pltpu.CompilerParams(has_side_effects=SPLIT_EFFECT),
    )(*srcs, *lands, send, recv, *after)
    return list(res[:ns]), list(res[ns:])


def _cast_into_window(name, a, l, axis, me1, after=()):
    shp = a.shape[1:]
    cast = lambda me_ref, a_ref, *rest: rest[-1].__setitem__(Ellipsis, a_ref[...].astype(BF16))
    if len(shp) == 3:
        assert axis == 1
        G, r, c = shp
        full = (G, r * N_DEV, c)
        grid = (G,)
        in_spec = pl.BlockSpec((None, None, r, c), lambda g, me: (l, g, 0, 0))
        out_spec = pl.BlockSpec((None, r, c), lambda g, me: (g, me[0], 0))
    else:
        r, c = shp
        tr = _tile(r, 512)
        nb = r // tr
        grid = (nb,)
        in_spec = pl.BlockSpec((None, tr, c), lambda i, me: (l, i, 0))
        if axis == 0:
            full = (r * N_DEV, c)
            out_spec = pl.BlockSpec((tr, c), lambda i, me: (me[0] * nb + i, 0))
        else:
            full = (r, c * N_DEV)
            out_spec = pl.BlockSpec((tr, c), lambda i, me: (i, me[0]))
    return pl.pallas_call(
        cast,
        name=name,
        grid_spec=pltpu.PrefetchScalarGridSpec(
            num_scalar_prefetch=1, grid=grid,
            in_specs=[in_spec] + [pl.BlockSpec(memory_space=pl.ANY)] * len(after), out_specs=out_spec),
        out_shape=jax.ShapeDtypeStruct(full, BF16),
        compiler_params=_cparams("arbitrary"),
    )(me1, a, *after)


def _adamw_math(w, g, m, v):
    m = ADAM_B1 * m + (1.0 - ADAM_B1) * g
    v = ADAM_B2 * v + (1.0 - ADAM_B2) * jnp.square(g)
    m_hat = m / (1.0 - ADAM_B1 ** ADAM_STEP)
    v_hat = v / (1.0 - ADAM_B2 ** ADAM_STEP)
    delta = -ADAM_LR * (m_hat / (jnp.sqrt(v_hat) + ADAM_EPS) + ADAM_WD * w)
    return delta, m, v


def _sum_slots(buf_ref):
    g = buf_ref[0].astype(F32)
    for s in range(1, N_DEV):
        g = g + buf_ref[s].astype(F32)
    return g


def _adamw_layer(name, buf, w, m, v, l, prev, after=(), own=None, own_axis=0, me1=None):
    shape = w.shape
    L, C = shape[0], shape[-1]
    Rr = math.prod(shape[1:-1])
    buf3 = buf.reshape(N_DEV, Rr, C)
    w3, m3, v3 = (t.reshape(L, Rr, C) for t in (w, m, v))
    tr = _tile(Rr, 2 * LANES) if Rr % LANES == 0 else Rr
    nb = Rr // tr
    n_pass = (0 if prev is None else 4) + len(after)
    n_lead = 1 if own is None else 3

    def body(*refs):
        buf_ref = refs[0] if own is None else refs[1]
        w_ref, m_ref, v_ref = refs[n_lead], refs[n_lead + 1], refs[n_lead + 2]
        g_out, d_out, m_out, v_out = refs[n_lead + 3 + n_pass:]
        if own is None:
            g = _sum_slots(buf_ref)
        else:
            me_ref, mine = refs[0], refs[2][...].astype(F32)
            g = None
            for s in range(N_DEV):
                term = jnp.where(me_ref[0] == s, mine, buf_ref[s].astype(F32))
                g = term if g is None else g + term
        d, mm, vv = _adamw_math(w_ref[...], g, m_ref[...], v_ref[...])
        g_out[...] = g
        d_out[...] = d
        m_out[...] = mm
        v_out[...] = vv

    out_shape = [jax.ShapeDtypeStruct((L, Rr, C), F32)] * 4
    passed = [pl.BlockSpec(memory_space=pl.ANY)] * n_pass
    aliases = {} if prev is None else {n_lead + 3 + k: k for k in range(4)}
    if own is None:
        spec = pl.BlockSpec((None, tr, C), lambda r: (l, r, 0))
        outs = pl.pallas_call(
            body, name=name, grid=(nb,),
            in_specs=[pl.BlockSpec((N_DEV, tr, C), lambda r: (0, r, 0)), spec, spec, spec] + passed,
            out_specs=[spec] * 4, out_shape=out_shape, input_output_aliases=aliases,
            compiler_params=_cparams("parallel"),
        )(buf3, w3, m3, v3, *(prev or ()), *after)
    else:
        spec = pl.BlockSpec((None, tr, C), lambda r, me: (l, r, 0))
        own_idx = (lambda r, me: (me[0] * nb + r, 0)) if own_axis == 0 else (lambda r, me: (r, me[0]))
        outs = pl.pallas_call(
            body, name=name,
            grid_spec=pltpu.PrefetchScalarGridSpec(
                num_scalar_prefetch=1, grid=(nb,),
                in_specs=[pl.BlockSpec((N_DEV, tr, C), lambda r, me: (0, r, 0)), pl.BlockSpec((tr, C), own_idx),
                          spec, spec, spec] + passed,
                out_specs=[spec] * 4),
            out_shape=out_shape, input_output_aliases=aliases,
            compiler_params=_cparams("arbitrary"),
        )(me1, buf3, own, w3, m3, v3, *(prev or ()), *after)
    return list(outs)


def _sum8(name, buf):
    R = buf.shape[1]

    def body(buf_ref, o_ref):
        o_ref[...] = _sum_slots(buf_ref)

    return pl.pallas_call(
        body,
        name=name,
        in_specs=[pl.BlockSpec(buf.shape, lambda: (0, 0, 0))],
        out_specs=pl.BlockSpec((R, LANES), lambda: (0, 0)),
        out_shape=jax.ShapeDtypeStruct((R, LANES), F32),
        compiler_params=_cparams(),
    )(buf)


def _adamw_small(name, w, g, m, v):
    shape = w.shape
    w2, g2, m2, v2 = (t.reshape(-1, shape[-1]) for t in (w, g, m, v))
    R, C = w2.shape
    tr = _row_tile(R, 512)

    def body(w_ref, g_ref, m_ref, v_ref, d_out, m_out, v_out):
        d, mm, vv = _adamw_math(w_ref[...], g_ref[...], m_ref[...], v_ref[...])
        d_out[...] = d
        m_out[...] = mm
        v_out[...] = vv

    spec = pl.BlockSpec((tr, C), lambda r: (r, 0))
    outs = pl.pallas_call(
        body,
        name=name,
        grid=(R // tr,),
        in_specs=[spec] * 4,
        out_specs=[spec] * 3,
        out_shape=[jax.ShapeDtypeStruct((R, C), F32)] * 3,
        compiler_params=_cparams("parallel"),
    )(w2, g2, m2, v2)
    return [o.reshape(shape) for o in outs]


def _pack(arrs, pad_rows_to=SUBLANES):
    parts = []
    for a in arrs:
        flat = a.reshape(-1)
        per = LANES * pad_rows_to
        padded = -(-flat.shape[0] // per) * per
        if padded != flat.shape[0]:
            flat = jnp.pad(flat, (0, padded - flat.shape[0]))
        parts.append(flat.reshape(-1, LANES))
    return jnp.concatenate(parts, axis=0)


def _unpack(packed, shapes, pad_rows_to=SUBLANES):
    out = []
    r = 0
    for shp in shapes:
        nel = math.prod(shp)
        per = LANES * pad_rows_to
        rows = -(-nel // per) * pad_rows_to
        out.append(packed[r:r + rows].reshape(-1)[:nel].reshape(shp))
        r += rows
    return out


BIG = ("lru_w_in", "lru_w_out", "pool_w_in", "pool_w_grp", "pool_w_out", "mlp_w1", "mlp_w2", "ple_w", "ple_gate_w")
BIG_AXIS = {"lru_w_in": 2, "lru_w_out": 1, "pool_w_in": 1, "pool_w_grp": 2, "pool_w_out": 1, "mlp_w1": 2,
            "mlp_w2": 1, "ple_w": 2, "ple_gate_w": 1}
SMALL_SHARDED = ("lru_conv_w", "pool_b_grp", "pool_scale")
REPLICATED = ("lru_conv_b", "lru_wa", "lru_ba", "lru_wx", "lru_bx", "lru_lambda", "ln_mix_g", "ln_mix_b",
              "ln_mlp_g", "ln_mlp_b", "ple_gate_b")
WEIGHTS = ("lru_w_in", "lru_conv_w", "lru_conv_b", "lru_wa", "lru_ba", "lru_wx", "lru_bx", "lru_lambda", "lru_w_out",
           "pool_w_in", "pool_w_grp", "pool_b_grp", "pool_scale", "pool_w_out", "ln_mix_g", "ln_mix_b", "mlp_w1",
           "mlp_w2", "ln_mlp_g", "ln_mlp_b", "ple_w", "ple_gate_w", "ple_gate_b")
INPUTS = ("x", "p") + WEIGHTS + ("loss_target",) + tuple("m_" + n for n in WEIGHTS) + tuple("v_" + n for n in WEIGHTS)


def _gather_last_axis(packed_full, shard_shape):
    nel = math.prod(shard_shape)
    blocks = packed_full.reshape(N_DEV, -1)[:, :nel].reshape((N_DEV,) + tuple(shard_shape))
    return jnp.concatenate([blocks[d] for d in range(N_DEV)], axis=-1)


def kernel(x, p, lru_w_in, lru_conv_w, lru_conv_b, lru_wa, lru_ba, lru_wx, lru_bx, lru_lambda, lru_w_out, pool_w_in, pool_w_grp, pool_b_grp, pool_scale, pool_w_out, ln_mix_g, ln_mix_b, mlp_w1, mlp_w2, ln_mlp_g, ln_mlp_b, ple_w, ple_gate_w, ple_gate_b, loss_target, m_lru_w_in, m_lru_conv_w, m_lru_conv_b, m_lru_wa, m_lru_ba, m_lru_wx, m_lru_bx, m_lru_lambda, m_lru_w_out, m_pool_w_in, m_pool_w_grp, m_pool_b_grp, m_pool_scale, m_pool_w_out, m_ln_mix_g, m_ln_mix_b, m_mlp_w1, m_mlp_w2, m_ln_mlp_g, m_ln_mlp_b, m_ple_w, m_ple_gate_w, m_ple_gate_b, v_lru_w_in, v_lru_conv_w, v_lru_conv_b, v_lru_wa, v_lru_ba, v_lru_wx, v_lru_bx, v_lru_lambda, v_lru_w_out, v_pool_w_in, v_pool_w_grp, v_pool_b_grp, v_pool_scale, v_pool_w_out, v_ln_mix_g, v_ln_mix_b, v_mlp_w1, v_mlp_w2, v_ln_mlp_g, v_ln_mlp_b, v_ple_w, v_ple_gate_w, v_ple_gate_b):
    A = dict(zip(INPUTS, (x, p, lru_w_in, lru_conv_w, lru_conv_b, lru_wa, lru_ba, lru_wx, lru_bx, lru_lambda, lru_w_out, pool_w_in, pool_w_grp, pool_b_grp, pool_scale, pool_w_out, ln_mix_g, ln_mix_b, mlp_w1, mlp_w2, ln_mlp_g, ln_mlp_b, ple_w, ple_gate_w, ple_gate_b, loss_target, m_lru_w_in, m_lru_conv_w, m_lru_conv_b, m_lru_wa, m_lru_ba, m_lru_wx, m_lru_bx, m_lru_lambda, m_lru_w_out, m_pool_w_in, m_pool_w_grp, m_pool_b_grp, m_pool_scale, m_pool_w_out, m_ln_mix_g, m_ln_mix_b, m_mlp_w1, m_mlp_w2, m_ln_mlp_g, m_ln_mlp_b, m_ple_w, m_ple_gate_w, m_ple_gate_b, v_lru_w_in, v_lru_conv_w, v_lru_conv_b, v_lru_wa, v_lru_ba, v_lru_wx, v_lru_bx, v_lru_lambda, v_lru_w_out, v_pool_w_in, v_pool_w_grp, v_pool_b_grp, v_pool_scale, v_pool_w_out, v_ln_mix_g, v_ln_mix_b, v_mlp_w1, v_mlp_w2, v_ln_mlp_g, v_ln_mlp_b, v_ple_w, v_ple_gate_w, v_ple_gate_b)))
    depth = ln_mix_g.shape[0]
    alpha = (2 * depth) ** 0.25
    S, D = x.shape[1], x.shape[2]
    xs = x.reshape(S, D)
    tgt = loss_target.reshape(S, D)
    p3 = p.reshape(depth, S, p.shape[-1])
    me = 4 * lax.axis_index("x") + 2 * lax.axis_index("y") + lax.axis_index("c")

    def layer_weights(i):
        s = i // 2
        mixer = ("lru_w_in", "lru_w_out") if i % 2 == 0 else ("pool_w_in", "pool_w_grp", "pool_w_out")
        return [(n, s) for n in mixer] + [(n, i) for n in ("mlp_w1", "mlp_w2", "ple_w", "ple_gate_w")]

    def axis_of(key):
        return 0 if key[0] == "small" else BIG_AXIS[key[0]] - 1

    def start_gather(tag, keys, after):
        axes = [axis_of(k) for k in keys]
        lands = [land[k] for k in keys]
        sizes = [a.shape[ax] // N_DEV for a, ax in zip(lands, axes)]
        return keys, _split_start(f"gather_{tag}_start", "gather", [], lands, axes, sizes, after=after,
                                  collective_id=next(barrier_ids)), axes, sizes

    def finish_gather(tag, pending, after):
        keys, handle, axes, sizes = pending
        for (n, l), full in zip(keys, _split_wait(f"gather_{tag}_wait", "gather", handle, axes, sizes, after)[1]):
            W[n][l] = full

    def start_exchange(tag, keys, arrs, after):
        axes = [axis_of(k) for k in keys]
        sizes = [a.shape[ax] // N_DEV for a, ax in zip(arrs, axes)]
        lands = []
        for k, a, ax, sz in zip(keys, arrs, axes, sizes):
            if reads_own_block(k):
                shp = list(a.shape)
                shp[ax] = sz
                lands.append(lax.empty((N_DEV,) + tuple(shp), a.dtype))
            else:
                lands.append(_own_block_placed(a, ax, sz, me))
        return keys, _split_start(f"exchange_{tag}_start", "scatter", arrs, lands, axes, sizes, after=after,
                                  collective_id=next(barrier_ids)), axes, sizes

    def finish_exchange(tag, pending, after):
        keys, handle, axes, sizes = pending
        sources, landed = _split_wait(f"exchange_{tag}_wait", "scatter", handle, axes, sizes, after)
        partial.update(zip(keys, landed))
        own_grad.update(zip(keys, sources))

    def reads_own_block(key):
        return key[0] not in ("small", "pool_w_grp")

    me1 = jnp.reshape(me, (1,)).astype(jnp.int32)
    barrier_ids = iter(range(64))
    land = {}
    W = {n: [None] * A[n].shape[0] for n in BIG}
    small_shard_shapes = [A[n].shape for n in SMALL_SHARDED]
    gathered = _all_gather("gather_small_params", [_pack([A[n] for n in SMALL_SHARDED])], [0])
    def gather_groups(i):
        keys = layer_weights(i)
        mixer, (w1, w2, pw, pg) = keys[:-4], keys[-4:]
        if i % 2 == 0:
            return [("in", mixer[:1]), ("out", mixer[1:]), ("up", [w1]), ("rest", [w2, pw, pg])]
        return [("in", mixer + [w1]), ("rest", [w2, pw, pg])]

    gather_pending = {}

    def send_layer(i, behind):
        for tag, keys in gather_groups(i):
            for k in keys:
                land[k] = _cast_into_window(f"cast_{k[0]}_{k[1]}", A[k[0]], k[1], axis_of(k), me1, after=behind)
            gather_pending[(i, tag)] = start_gather(f"l{i}_{tag}", keys, behind)
            behind = (gather_pending[(i, tag)][1][4],)
        return behind

    layer0_started = send_layer(0, (gathered[0],))
    small_full = gathered[0].reshape(N_DEV, -1, LANES)
    r = 0
    for n, shp in zip(SMALL_SHARDED, small_shard_shapes):
        rows = -(-math.prod(shp) // (LANES * SUBLANES)) * SUBLANES
        W[n] = _gather_last_axis(small_full[:, r:r + rows], shp)
        r += rows
    wa_b, wx_b = _to_bf16("cast_gates", [lru_wa, lru_wx], after=layer0_started)
    n_lru = lru_w_in.shape[0]
    lru_par = [jnp.concatenate([W["lru_conv_w"][s], lru_conv_b[s][None], lru_ba[s][None], lru_bx[s][None],
                                lru_lambda[s][None]], axis=0) for s in range(n_lru)]
    pool_par = [jnp.stack([W["pool_b_grp"][s], W["pool_scale"][s]], axis=0) for s in range(pool_w_in.shape[0])]

    saved = []
    h_in = xs
    (h_in_b,) = _to_bf16("cast_inputs", [xs], after=layer0_started)
    layer1_started = send_layer(1, (layer0_started[0], h_in_b))
    finish_gather("l0_in", gather_pending[(0, "in")], [h_in_b, wa_b, layer1_started[0]])
    for i in range(depth):
        s = i // 2
        sv = {"x0b": h_in_b}
        if i > 0:
            finish_gather(f"l{i}_in", gather_pending[(i, "in")], h_in)
        ln_out = dict(out_dtypes=[F32, F32, BF16], tm=MM_TM_ROWS, tn=D,
                      epi=lambda acc, xp, g, b: _ln_apply(alpha * xp + acc, g, b))
        if i % 2 == 0:
            sv["proj"] = _mm(f"l{i}_lru_in", h_in_b, W["lru_w_in"], "nn", [F32], b_lead=s)
            if i == 0:
                behind = (layer1_started[0], sv["proj"])
                for later in range(2, depth):
                    behind = send_layer(later, behind)
                all_started = behind[0]
            sv["gh"], sv["h"], *sv["gates"] = _lru_fwd(f"l{i}_lru_core", sv["proj"], lru_par[s], wa_b[s], wx_b[s])
            finish_gather(f"l{i}_out", gather_pending[(i, "out")], [sv["gh"], all_started])
            mix_in, mix_w = sv["gh"], W["lru_w_out"]
        else:
            u_pool = _mm(f"l{i}_pool_in", h_in_b, W["pool_w_in"], "nn", [F32], b_lead=s)
            sv["zs"], sv["pooled"], sv["z"] = _pool_fwd(f"l{i}_pool_core", u_pool, W["pool_w_grp"][s], pool_par[s])
            mix_in, mix_w = sv["zs"], W["pool_w_out"]
        sv["z1"], sv["x1"], sv["x1b"] = _mm(f"l{i}_mix_out_ln", mix_in, mix_w, "nn", b_lead=s,
                                            extras=[h_in, ln_mix_g[i][None], ln_mix_b[i][None]], **ln_out)
        if i % 2 == 0:
            finish_gather(f"l{i}_up", gather_pending[(i, "up")], sv["x1b"])
        sv["hpre"], sv["hact"] = _mm(f"l{i}_mlp_up", sv["x1b"], W["mlp_w1"], "nn", [BF16, BF16], b_lead=i,
                                     epi=lambda acc: (acc, jnp.square(jnp.maximum(acc, 0.0))))
        finish_gather(f"l{i}_rest", gather_pending[(i, "rest")], sv["hact"])
        sv["z2"], sv["x2"], sv["x2b"] = _mm(f"l{i}_mlp_down_ln", sv["hact"], W["mlp_w2"], "nn", b_lead=i,
                                            extras=[sv["x1"], ln_mlp_g[i][None], ln_mlp_b[i][None]], **ln_out)
        sv["pp"] = _mm(f"l{i}_ple_up", p3, W["ple_w"], "nn", [F32], a_lead=i, b_lead=i)

        def ple_epi(acc, bg, x2t, ppt):
            gpre = acc + bg
            x3 = x2t + ppt * jax.nn.sigmoid(gpre)
            return x3, x3, gpre

        h_in, h_in_b, sv["gpre"] = _mm(f"l{i}_ple_gate", sv["x2b"], W["ple_gate_w"], "nn", [F32, BF16, F32], b_lead=i,
                                       epi=ple_epi, extras=[ple_gate_b[i][None], sv["x2"], sv["pp"]],
                                       tm=MM_TM_ROWS, tn=D)
        saved.append(sv)

    dx, dpp, dgpre, sq, dbg = _loss_and_grad("loss", h_in, tgt, saved[-1]["gpre"], saved[-1]["pp"])
    loss = lax.psum(0.5 * sq[0, 0] / D, ("x", "y", "c"))

    dW = {n: [None] * A[n].shape[0] for n in BIG}
    dsmall = {n: [None] * A[n].shape[0] for n in REPLICATED + SMALL_SHARDED}
    small_names = REPLICATED + SMALL_SHARDED
    partial = {}
    own_grad = {}
    exchange_pending = {}
    exchange_token = ()
    for i in reversed(range(depth)):
        s = i // 2
        sv = saved[i]
        dsmall["ple_gate_b"][i] = dbg[0]
        dW["ple_w"][i] = _mm(f"l{i}_d_ple_w", p3, dpp, "tn", [BF16], a_lead=i, after=exchange_token)
        dW["ple_gate_w"][i] = _mm(f"l{i}_d_ple_gate_w", sv["x2b"], dgpre, "tn", [BF16])
        ln_back = dict(out_dtypes=[F32, BF16], tm=MM_TM_ROWS, tn=D, n_sums=2)
        dz2, dz2b, dg, db = _mm(f"l{i}_d_x2_ln", dgpre, W["ple_gate_w"], "nt", b_lead=i,
                                extras=[dx, sv["z2"], ln_mlp_g[i][None]],
                                epi=lambda acc, d, z, g: _ln_grad(acc + d, z, g), after=exchange_token, **ln_back)
        dsmall["ln_mlp_g"][i], dsmall["ln_mlp_b"][i] = dg[0], db[0]
        dhpre = _mm(f"l{i}_d_hpre", dz2b, W["mlp_w2"], "nt", [BF16], b_lead=i, extras=[sv["hpre"]],
                    epi=lambda acc, hp: (acc * (2.0 * jnp.maximum(hp.astype(F32), 0.0)),))
        dW["mlp_w2"][i] = _mm(f"l{i}_d_mlp_w2", sv["hact"], dz2b, "tn", [BF16])
        dW["mlp_w1"][i] = _mm(f"l{i}_d_mlp_w1", sv["x1b"], dhpre, "tn", [BF16])
        mlp_after = ()
        if i == 0:
            early = [(n, 0) for n in ("ple_w", "ple_gate_w", "mlp_w2", "mlp_w1")]
            exchange_early0 = start_exchange("early0", early, [dW[n][l] for n, l in early], ())
            mlp_after = (exchange_early0[1][4],)
        dz1, dz1b, dg, db = _mm(f"l{i}_d_x1_ln", dhpre, W["mlp_w1"], "nt", b_lead=i,
                                extras=[dz2, sv["z1"], ln_mix_g[i][None]],
                                epi=lambda acc, d, z, g: _ln_grad(acc + alpha * d, z, g), after=mlp_after, **ln_back)
        dsmall["ln_mix_g"][i], dsmall["ln_mix_b"][i] = dg[0], db[0]
        if i % 2 == 0:
            dW["lru_w_out"][s] = _mm(f"l{i}_d_lru_w_out", sv["gh"], dz1b, "tn", [BF16])
            dgh = _mm(f"l{i}_d_gh", dz1b, W["lru_w_out"], "nt", [F32], b_lead=s)
            dup, dy, dwa, dwx, dpar = _lru_bwd(f"l{i}_lru_core_bwd", sv["proj"], sv["h"], dgh, sv["gates"],
                                               lru_par[s], wa_b[s], wx_b[s])
            dsmall["lru_wa"][s], dsmall["lru_wx"][s] = dwa, dwx
            dsmall["lru_conv_w"][s] = dpar[0:4]
            for k, n in enumerate(("lru_conv_b", "lru_ba", "lru_bx", "lru_lambda")):
                dsmall[n][s] = dpar[4 + k]
            dmix_in = jnp.concatenate([dup, dy], axis=1)
            win = "lru_w_in"
        else:
            dW["pool_w_out"][s] = _mm(f"l{i}_d_pool_w_out", sv["zs"], dz1b, "tn", [BF16])
            dzs = _mm(f"l{i}_d_zs", dz1b, W["pool_w_out"], "nt", [F32], b_lead=s)
            dmix_in, dW["pool_w_grp"][s], dpar = _pool_bwd(f"l{i}_pool_core_bwd", sv["pooled"], sv["z"], dzs,
                                                          W["pool_w_grp"][s], pool_par[s])
            dsmall["pool_b_grp"][s], dsmall["pool_scale"][s] = dpar[0], dpar[1]
            win = "pool_w_in"
        dW[win][s] = _mm(f"l{i}_d_{win}", sv["x0b"], dmix_in, "tn", [BF16])
        x0_after = ()
        if i > 0:
            keys = layer_weights(i)
            exchange_pending[i] = start_exchange(f"l{i}", keys, [dW[n][l] for n, l in keys], ())
            exchange_token = (exchange_pending[i][1][4],)
        else:
            small_grads = [jnp.stack(dsmall[n]) for n in small_names]
            small_shapes = [g.shape for g in small_grads]
            packed_g = _pack(small_grads)
            assert packed_g.shape[0] % (N_DEV * SUBLANES) == 0, packed_g.shape
            late = [("lru_w_out", 0), ("lru_w_in", 0), ("small", 0)]
            exchange_late0 = start_exchange("late0", late, [dW["lru_w_out"][0], dW["lru_w_in"][0], packed_g], ())
            x0_after = (exchange_late0[1][4],)
        if i > 0:
            def x0_epi(acc, d, gpre, pp):
                dxv = acc + alpha * d
                return (dxv,) + _ple_grad(dxv, gpre, pp)

            dx, dpp, dgpre, dbg = _mm(f"l{i}_d_x0_ple", dmix_in, W[win], "nt", [F32, BF16, BF16], b_lead=s,
                                      extras=[dz1, saved[i - 1]["gpre"], saved[i - 1]["pp"]], epi=x0_epi,
                                      tm=MM_TM_ROWS, tn=D, n_sums=1)
        else:
            dx = _mm(f"l{i}_d_x0", dmix_in, W[win], "nt", [F32], b_lead=s, extras=[dz1],
                     epi=lambda acc, d: (acc + alpha * d,), after=x0_after)
    grad_x = dx.reshape(x.shape)

    for i in range(1, depth):
        finish_exchange(f"l{i}", exchange_pending[i], x0_after[0])
    stacked = {n: None for n in BIG}
    layer0 = layer_weights(0)

    def adamw(n, l, after=()):
        own = dict(own=own_grad[(n, l)], own_axis=axis_of((n, l)), me1=me1) if reads_own_block((n, l)) else {}
        stacked[n] = _adamw_layer(f"adamw_{n}_{l}", partial[(n, l)], A[n], A["m_" + n], A["v_" + n], l, stacked[n],
                                  after=after, **own)

    for n in BIG:
        for l in reversed(range(A[n].shape[0])):
            if (n, l) not in layer0:
                adamw(n, l)
    behind = [dx] + [stacked[n][0] for n in BIG if stacked[n] is not None]
    finish_exchange("early0", exchange_early0, behind)
    finish_exchange("late0", exchange_late0, behind)
    red = _sum8("sum_small", partial[("small", 0)])
    rows = red.shape[0]
    red_land = lax.dynamic_update_slice_in_dim(lax.empty((N_DEV * rows, LANES), F32), red, me * rows, 0)
    small_handle = _split_start("gather_small_start", "gather", [], [red_land], [0], [rows],
                                collective_id=next(barrier_ids))
    for n, l in layer0:
        adamw(n, l, after=(small_handle[4],))
    outs = {n: [o.reshape(A[n].shape) for o in stacked[n]] for n in BIG}
    red_full = _split_wait("gather_small_wait", "gather", small_handle, [0], [rows],
                           [stacked[n][0] for n, _ in layer0])[1][0]
    small_g = dict(zip(small_names, _unpack(red_full, small_shapes)))
    for n in SMALL_SHARDED:
        width = A[n].shape[-1]
        small_g[n] = lax.dynamic_slice_in_dim(small_g[n], me * width, width, axis=small_g[n].ndim - 1)
    for n in small_names:
        outs[n] = [small_g[n]] + _adamw_small(f"adamw_{n}", A[n], small_g[n], A["m_" + n], A["v_" + n])

    return (loss, grad_x, *[outs[n][0] for n in WEIGHTS], *[outs[n][1] for n in WEIGHTS],
            *[outs[n][2] for n in WEIGHTS], *[outs[n][3] for n in WEIGHTS])
```
